```python
import math
import jax, jax.numpy as jnp
from jax import lax
import numpy as np

D_MODEL = 1024
BATCH = 8
SEQ = 4096
DEPTH = 4

MIX_WIDTH = D_MODEL
LRU_WIDTH = D_MODEL // 2
LRU_BLOCKS = 8
LRU_BLOCK = LRU_WIDTH // LRU_BLOCKS
LRU_C = 8.0
GDN_HEAD_DIM = 128
GDN_HEADS = (MIX_WIDTH - LRU_WIDTH) // GDN_HEAD_DIM
GDN_WIDTH = GDN_HEADS * GDN_HEAD_DIM
GDN_CHUNK = 64
CONV_WIDTH = 4
D_FF = 4 * D_MODEL
N_MOD = 6
IN_COLS = 2 * LRU_WIDTH + 4 * GDN_WIDTH + 2 * GDN_HEADS
NORM_EPS = 1e-6

kernel_name = "hymba_rglru_gdn_hybrid"


def rms_norm(x, w):
    xf = x.astype(jnp.float32)
    y = xf * lax.rsqrt(jnp.mean(xf * xf, axis=-1, keepdims=True) + NORM_EPS)
    return (y * w.astype(jnp.float32)).astype(x.dtype)


def causal_depthwise_conv(x, w):
    K = w.shape[0]
    S = x.shape[1]
    xp = jnp.pad(x, ((0, 0), (K - 1, 0), (0, 0)))
    y = xp[:, 0:S] * w[0]
    for k in range(1, K):
        y = y + xp[:, k:k + S] * w[k]
    return y


def rg_lru(x, r_pre, i_pre, lam):
    dt = x.dtype
    xf = x.astype(jnp.float32)
    r = jax.nn.sigmoid(r_pre.astype(jnp.float32))
    i = jax.nn.sigmoid(i_pre.astype(jnp.float32))
    log_a = LRU_C * r * jax.nn.log_sigmoid(lam.astype(jnp.float32))
    a = jnp.exp(log_a)
    mult = jnp.sqrt(jnp.maximum(-jnp.expm1(2.0 * log_a), 1e-12))
    b = mult * (i * xf)

    def combine(left, right):
        a1, b1 = left
        a2, b2 = right
        return a1 * a2, a2 * b1 + b2

    _, h = lax.associative_scan(combine, (a, b), axis=1)
    return h.astype(dt)


def l2_normalize(t):
    return t * lax.rsqrt(jnp.sum(t * t, axis=-1, keepdims=True) + 1e-6)


def gated_delta_rule_chunked(q, k, v, g, beta):
    dt = v.dtype
    B, S, H, Dk = q.shape
    Dv = v.shape[-1]
    C = GDN_CHUNK
    N = S // C
    q = l2_normalize(q.astype(jnp.float32)) * (Dk ** -0.5)
    k = l2_normalize(k.astype(jnp.float32))
    v = v.astype(jnp.float32)

    def chunks(t):
        return t.reshape(B, N, C, H, -1).transpose(0, 3, 1, 2, 4)

    q, k, v = chunks(q), chunks(k), chunks(v)
    g = g.astype(jnp.float32).reshape(B, N, C, H).transpose(0, 3, 1, 2)
    beta = beta.astype(jnp.float32).reshape(B, N, C, H).transpose(0, 3, 1, 2)
    g = jnp.cumsum(g, axis=-1)

    causal = jnp.tril(jnp.ones((C, C), dtype=bool))
    strict = jnp.tril(jnp.ones((C, C), dtype=bool), k=-1)
    decay = jnp.exp(jnp.where(causal, g[..., :, None] - g[..., None, :], -jnp.inf))

    k_beta = k * beta[..., None]
    v_beta = v * beta[..., None]
    Lmat = jnp.where(strict, jnp.einsum('bhnid,bhnjd->bhnij', k_beta, k) * decay, 0.0)
    tmat = Lmat + jnp.eye(C, dtype=jnp.float32)
    u = lax.linalg.triangular_solve(tmat, v_beta, left_side=True, lower=True, unit_diagonal=True)
    w = lax.linalg.triangular_solve(tmat, k_beta * jnp.exp(g)[..., None],
                                    left_side=True, lower=True, unit_diagonal=True)
    attn = jnp.where(causal, jnp.einsum('bhnid,bhnjd->bhnij', q, k) * decay, 0.0)
    q_dec = q * jnp.exp(g)[..., None]
    k_tail = k * jnp.exp(g[..., -1:] - g)[..., None]
    g_last = jnp.exp(g[..., -1])

    def to_front(t):
        return jnp.moveaxis(t, 2, 0)

    xs = (to_front(u), to_front(w), to_front(attn), to_front(q_dec), to_front(k_tail),
          jnp.moveaxis(g_last, 2, 0))

    def step(state, inp):
        u_n, w_n, attn_n, qd_n, kt_n, gl_n = inp
        v_new = u_n - jnp.einsum('bhck,bhkv->bhcv', w_n, state)
        o = jnp.einsum('bhck,bhkv->bhcv', qd_n, state) + jnp.einsum('bhij,bhjv->bhiv', attn_n, v_new)
        state = state * gl_n[..., None, None] + jnp.einsum('bhck,bhcv->bhkv', kt_n, v_new)
        return state, o

    state0 = jnp.zeros((B, H, Dk, Dv), jnp.float32)
    _, o = lax.scan(step, state0, xs)
    o = o.transpose(1, 0, 3, 2, 4).reshape(B, S, H, Dv)
    return o.astype(dt)


def _fwd_setup_inputs(seed: int = 0) -> dict:
    key = jax.random.key(seed)
    ks = jax.random.split(key, 24)
    f32 = jnp.float32
    L, D = DEPTH, D_MODEL

    def nrm(k, shape, std):
        return jax.random.normal(k, shape, f32) * std

    x = nrm(ks[0], (BATCH, SEQ, D), 1.0)
    c = nrm(ks[1], (BATCH, D), 1.0)
    norm_mix_w = 1.0 + nrm(ks[2], (L, D), 0.02)
    norm_mlp_w = 1.0 + nrm(ks[3], (L, D), 0.02)
    w_mod = nrm(ks[4], (L, D, N_MOD * D), 0.005)
    gate_offset = jnp.array([0.0, 0.0, 1.0, 0.0, 0.0, 1.0], f32)[None, :, None]
    b_mod = (nrm(ks[5], (L, N_MOD, D), 0.02) + gate_offset).reshape(L, N_MOD * D)
    w_in = nrm(ks[6], (L, D, IN_COLS), D ** -0.5)
    lru_conv_w = nrm(ks[7], (L, CONV_WIDTH, LRU_WIDTH), CONV_WIDTH ** -0.5)
    lru_conv_b = nrm(ks[8], (L, LRU_WIDTH), 0.01)
    lru_gate_a_w = nrm(ks[9], (L, LRU_BLOCKS, LRU_BLOCK, LRU_BLOCK), LRU_BLOCK ** -0.5)
    lru_gate_a_b = nrm(ks[10], (L, LRU_WIDTH), 0.01)
    lru_gate_x_w = nrm(ks[11], (L, LRU_BLOCKS, LRU_BLOCK, LRU_BLOCK), LRU_BLOCK ** -0.5)
    lru_gate_x_b = nrm(ks[12], (L, LRU_WIDTH), 0.01)
    u = jax.random.uniform(ks[13], (L, LRU_WIDTH), f32, 0.9, 0.999)
    p = u ** (1.0 / LRU_C)
    lru_lambda = jnp.log(p) - jnp.log1p(-p)
    lru_norm_w = 1.0 + nrm(ks[14], (L, LRU_WIDTH), 0.02)
    gdn_conv_w = nrm(ks[15], (L, CONV_WIDTH, 3 * GDN_WIDTH), CONV_WIDTH ** -0.5)
    gdn_a_log = jnp.log(jax.random.uniform(ks[16], (L, GDN_HEADS), f32, 1.0, 16.0))
    dt0 = jnp.exp(jax.random.uniform(ks[17], (L, GDN_HEADS), f32, math.log(1e-3), math.log(1e-1)))
    gdn_dt_bias = dt0 + jnp.log(-jnp.expm1(-dt0))
    gdn_norm_w = 1.0 + nrm(ks[18], (L, GDN_HEAD_DIM), 0.02)
    w_out = nrm(ks[19], (L, MIX_WIDTH, D), MIX_WIDTH ** -0.5)
    w_up = nrm(ks[20], (L, D, D_FF), D ** -0.5)
    w_down = nrm(ks[21], (L, D_FF, D), D_FF ** -0.5)
    final_norm_w = 1.0 + nrm(ks[22], (D,), 0.02)
    return {
        "x": x, "c": c,
        "norm_mix_w": norm_mix_w, "norm_mlp_w": norm_mlp_w,
        "w_mod": w_mod, "b_mod": b_mod,
        "w_in": w_in,
        "lru_conv_w": lru_conv_w, "lru_conv_b": lru_conv_b,
        "lru_gate_a_w": lru_gate_a_w, "lru_gate_a_b": lru_gate_a_b,
        "lru_gate_x_w": lru_gate_x_w, "lru_gate_x_b": lru_gate_x_b,
        "lru_lambda": lru_lambda, "lru_norm_w": lru_norm_w,
        "gdn_conv_w": gdn_conv_w, "gdn_a_log": gdn_a_log, "gdn_dt_bias": gdn_dt_bias,
        "gdn_norm_w": gdn_norm_w,
        "w_out": w_out, "w_up": w_up, "w_down": w_down,
        "final_norm_w": final_norm_w,
    }


def _fwd_reference(x, c, norm_mix_w, norm_mlp_w, w_mod, b_mod, w_in,
              lru_conv_w, lru_conv_b, lru_gate_a_w, lru_gate_a_b,
              lru_gate_x_w, lru_gate_x_b, lru_lambda, lru_norm_w,
              gdn_conv_w, gdn_a_log, gdn_dt_bias, gdn_norm_w,
              w_out, w_up, w_down, final_norm_w):
    B, S, D = x.shape
    o_lx = 0
    o_ly = o_lx + LRU_WIDTH
    o_q = o_ly + LRU_WIDTH
    o_v_end = o_q + 3 * GDN_WIDTH
    o_z = o_v_end
    o_beta = o_z + GDN_WIDTH
    o_alpha = o_beta + GDN_HEADS
    c_act = jax.nn.silu(c)

    for l in range(DEPTH):
        mod = c_act @ w_mod[l] + b_mod[l]
        sh1, sc1, g1, sh2, sc2, g2 = jnp.split(mod[:, None, :], N_MOD, axis=-1)

        h = rms_norm(x, norm_mix_w[l]) * (1.0 + sc1) + sh1
        proj = h @ w_in[l]

        x_lru = proj[..., o_lx:o_ly]
        y_lru = proj[..., o_ly:o_q]
        xr = causal_depthwise_conv(x_lru, lru_conv_w[l]) + lru_conv_b[l]
        xb = xr.reshape(B, S, LRU_BLOCKS, LRU_BLOCK)
        r_pre = jnp.einsum('bsgi,gij->bsgj', xb, lru_gate_a_w[l]).reshape(B, S, LRU_WIDTH) + lru_gate_a_b[l]
        i_pre = jnp.einsum('bsgi,gij->bsgj', xb, lru_gate_x_w[l]).reshape(B, S, LRU_WIDTH) + lru_gate_x_b[l]
        h_lru = rg_lru(xr, r_pre, i_pre, lru_lambda[l])
        out_lru = rms_norm(h_lru * jax.nn.gelu(y_lru), lru_norm_w[l])

        qkv = jax.nn.silu(causal_depthwise_conv(proj[..., o_q:o_v_end], gdn_conv_w[l]))
        q, k, v = jnp.split(qkv.reshape(B, S, 3, GDN_HEADS, GDN_HEAD_DIM), 3, axis=2)
        q, k, v = q[:, :, 0], k[:, :, 0], v[:, :, 0]
        z = proj[..., o_z:o_beta].reshape(B, S, GDN_HEADS, GDN_HEAD_DIM)
        beta = jax.nn.sigmoid(proj[..., o_beta:o_alpha].astype(jnp.float32))
        g = -jnp.exp(gdn_a_log[l].astype(jnp.float32)) * jax.nn.softplus(
            proj[..., o_alpha:o_alpha + GDN_HEADS].astype(jnp.float32) + gdn_dt_bias[l].astype(jnp.float32))
        o = gated_delta_rule_chunked(q, k, v, g, beta)
        out_gdn = (rms_norm(o, gdn_norm_w[l]) * jax.nn.silu(z)).reshape(B, S, GDN_WIDTH)

        mix = jnp.concatenate([out_lru, out_gdn], axis=-1) @ w_out[l]
        x = x + g1 * mix

        h = rms_norm(x, norm_mlp_w[l]) * (1.0 + sc2) + sh2
        x = x + g2 * (jnp.square(jax.nn.relu(h @ w_up[l])) @ w_down[l])

    return rms_norm(x, final_norm_w)


import jax as _jax
import jax.numpy as _jnp

TWIN_FORMAT = 'train_step'
FWD_PARAMS = ['x', 'c', 'norm_mix_w', 'norm_mlp_w', 'w_mod', 'b_mod', 'w_in', 'lru_conv_w', 'lru_conv_b', 'lru_gate_a_w', 'lru_gate_a_b', 'lru_gate_x_w', 'lru_gate_x_b', 'lru_lambda', 'lru_norm_w', 'gdn_conv_w', 'gdn_a_log', 'gdn_dt_bias', 'gdn_norm_w', 'w_out', 'w_up', 'w_down', 'final_norm_w']
TWIN_WEIGHTS = ['norm_mix_w', 'norm_mlp_w', 'w_mod', 'b_mod', 'w_in', 'lru_conv_w', 'lru_conv_b', 'lru_gate_a_w', 'lru_gate_a_b', 'lru_gate_x_w', 'lru_gate_x_b', 'lru_lambda', 'lru_norm_w', 'gdn_conv_w', 'gdn_a_log', 'gdn_dt_bias', 'gdn_norm_w', 'w_out', 'w_up', 'w_down', 'final_norm_w']
TWIN_DIFF_INPUT = 'x'
TWIN_INPUTS = ['x', 'c', 'norm_mix_w', 'norm_mlp_w', 'w_mod', 'b_mod', 'w_in', 'lru_conv_w', 'lru_conv_b', 'lru_gate_a_w', 'lru_gate_a_b', 'lru_gate_x_w', 'lru_gate_x_b', 'lru_lambda', 'lru_norm_w', 'gdn_conv_w', 'gdn_a_log', 'gdn_dt_bias', 'gdn_norm_w', 'w_out', 'w_up', 'w_down', 'final_norm_w', 'loss_target', 'm_norm_mix_w', 'm_norm_mlp_w', 'm_w_mod', 'm_b_mod', 'm_w_in', 'm_lru_conv_w', 'm_lru_conv_b', 'm_lru_gate_a_w', 'm_lru_gate_a_b', 'm_lru_gate_x_w', 'm_lru_gate_x_b', 'm_lru_lambda', 'm_lru_norm_w', 'm_gdn_conv_w', 'm_gdn_a_log', 'm_gdn_dt_bias', 'm_gdn_norm_w', 'm_w_out', 'm_w_up', 'm_w_down', 'm_final_norm_w', 'v_norm_mix_w', 'v_norm_mlp_w', 'v_w_mod', 'v_b_mod', 'v_w_in', 'v_lru_conv_w', 'v_lru_conv_b', 'v_lru_gate_a_w', 'v_lru_gate_a_b', 'v_lru_gate_x_w', 'v_lru_gate_x_b', 'v_lru_lambda', 'v_lru_norm_w', 'v_gdn_conv_w', 'v_gdn_a_log', 'v_gdn_dt_bias', 'v_gdn_norm_w', 'v_w_out', 'v_w_up', 'v_w_down', 'v_final_norm_w']
TWIN_OUTPUTS = ['loss', 'grad_x', 'grad_norm_mix_w', 'grad_norm_mlp_w', 'grad_w_mod', 'grad_b_mod', 'grad_w_in', 'grad_lru_conv_w', 'grad_lru_conv_b', 'grad_lru_gate_a_w', 'grad_lru_gate_a_b', 'grad_lru_gate_x_w', 'grad_lru_gate_x_b', 'grad_lru_lambda', 'grad_lru_norm_w', 'grad_gdn_conv_w', 'grad_gdn_a_log', 'grad_gdn_dt_bias', 'grad_gdn_norm_w', 'grad_w_out', 'grad_w_up', 'grad_w_down', 'grad_final_norm_w', 'delta_norm_mix_w', 'delta_norm_mlp_w', 'delta_w_mod', 'delta_b_mod', 'delta_w_in', 'delta_lru_conv_w', 'delta_lru_conv_b', 'delta_lru_gate_a_w', 'delta_lru_gate_a_b', 'delta_lru_gate_x_w', 'delta_lru_gate_x_b', 'delta_lru_lambda', 'delta_lru_norm_w', 'delta_gdn_conv_w', 'delta_gdn_a_log', 'delta_gdn_dt_bias', 'delta_gdn_norm_w', 'delta_w_out', 'delta_w_up', 'delta_w_down', 'delta_final_norm_w', 'new_m_norm_mix_w', 'new_m_norm_mlp_w', 'new_m_w_mod', 'new_m_b_mod', 'new_m_w_in', 'new_m_lru_conv_w', 'new_m_lru_conv_b', 'new_m_lru_gate_a_w', 'new_m_lru_gate_a_b', 'new_m_lru_gate_x_w', 'new_m_lru_gate_x_b', 'new_m_lru_lambda', 'new_m_lru_norm_w', 'new_m_gdn_conv_w', 'new_m_gdn_a_log', 'new_m_gdn_dt_bias', 'new_m_gdn_norm_w', 'new_m_w_out', 'new_m_w_up', 'new_m_w_down', 'new_m_final_norm_w', 'new_v_norm_mix_w', 'new_v_norm_mlp_w', 'new_v_w_mod', 'new_v_b_mod', 'new_v_w_in', 'new_v_lru_conv_w', 'new_v_lru_conv_b', 'new_v_lru_gate_a_w', 'new_v_lru_gate_a_b', 'new_v_lru_gate_x_w', 'new_v_lru_gate_x_b', 'new_v_lru_lambda', 'new_v_lru_norm_w', 'new_v_gdn_conv_w', 'new_v_gdn_a_log', 'new_v_gdn_dt_bias', 'new_v_gdn_norm_w', 'new_v_w_out', 'new_v_w_up', 'new_v_w_down', 'new_v_final_norm_w']
TWIN_LEAF_KINDS = {'loss': 'loss', 'grad_x': 'grad_x', 'grad_norm_mix_w': 'grad_w', 'grad_norm_mlp_w': 'grad_w', 'grad_w_mod': 'grad_w', 'grad_b_mod': 'grad_w', 'grad_w_in': 'grad_w', 'grad_lru_conv_w': 'grad_w', 'grad_lru_conv_b': 'grad_w', 'grad_lru_gate_a_w': 'grad_w', 'grad_lru_gate_a_b': 'grad_w', 'grad_lru_gate_x_w': 'grad_w', 'grad_lru_gate_x_b': 'grad_w', 'grad_lru_lambda': 'grad_w', 'grad_lru_norm_w': 'grad_w', 'grad_gdn_conv_w': 'grad_w', 'grad_gdn_a_log': 'grad_w', 'grad_gdn_dt_bias': 'grad_w', 'grad_gdn_norm_w': 'grad_w', 'grad_w_out': 'grad_w', 'grad_w_up': 'grad_w', 'grad_w_down': 'grad_w', 'grad_final_norm_w': 'grad_w', 'delta_norm_mix_w': 'delta_w', 'delta_norm_mlp_w': 'delta_w', 'delta_w_mod': 'delta_w', 'delta_b_mod': 'delta_w', 'delta_w_in': 'delta_w', 'delta_lru_conv_w': 'delta_w', 'delta_lru_conv_b': 'delta_w', 'delta_lru_gate_a_w': 'delta_w', 'delta_lru_gate_a_b': 'delta_w', 'delta_lru_gate_x_w': 'delta_w', 'delta_lru_gate_x_b': 'delta_w', 'delta_lru_lambda': 'delta_w', 'delta_lru_norm_w': 'delta_w', 'delta_gdn_conv_w': 'delta_w', 'delta_gdn_a_log': 'delta_w', 'delta_gdn_dt_bias': 'delta_w', 'delta_gdn_norm_w': 'delta_w', 'delta_w_out': 'delta_w', 'delta_w_up': 'delta_w', 'delta_w_down': 'delta_w', 'delta_final_norm_w': 'delta_w', 'new_m_norm_mix_w': 'new_m', 'new_m_norm_mlp_w': 'new_m', 'new_m_w_mod': 'new_m', 'new_m_b_mod': 'new_m', 'new_m_w_in': 'new_m', 'new_m_lru_conv_w': 'new_m', 'new_m_lru_conv_b': 'new_m', 'new_m_lru_gate_a_w': 'new_m', 'new_m_lru_gate_a_b': 'new_m', 'new_m_lru_gate_x_w': 'new_m', 'new_m_lru_gate_x_b': 'new_m', 'new_m_lru_lambda': 'new_m', 'new_m_lru_norm_w': 'new_m', 'new_m_gdn_conv_w': 'new_m', 'new_m_gdn_a_log': 'new_m', 'new_m_gdn_dt_bias': 'new_m', 'new_m_gdn_norm_w': 'new_m', 'new_m_w_out': 'new_m', 'new_m_w_up': 'new_m', 'new_m_w_down': 'new_m', 'new_m_final_norm_w': 'new_m', 'new_v_norm_mix_w': 'new_v', 'new_v_norm_mlp_w': 'new_v', 'new_v_w_mod': 'new_v', 'new_v_b_mod': 'new_v', 'new_v_w_in': 'new_v', 'new_v_lru_conv_w': 'new_v', 'new_v_lru_conv_b': 'new_v', 'new_v_lru_gate_a_w': 'new_v', 'new_v_lru_gate_a_b': 'new_v', 'new_v_lru_gate_x_w': 'new_v', 'new_v_lru_gate_x_b': 'new_v', 'new_v_lru_lambda': 'new_v', 'new_v_lru_norm_w': 'new_v', 'new_v_gdn_conv_w': 'new_v', 'new_v_gdn_a_log': 'new_v', 'new_v_gdn_dt_bias': 'new_v', 'new_v_gdn_norm_w': 'new_v', 'new_v_w_out': 'new_v', 'new_v_w_up': 'new_v', 'new_v_w_down': 'new_v', 'new_v_final_norm_w': 'new_v'}


def _forward(args):
    return _fwd_reference(*[args[k] for k in FWD_PARAMS])


def _output_shape():
    def fwd():
        inp = _fwd_setup_inputs(0)
        return _fwd_reference(*[inp[k] for k in FWD_PARAMS])
    out = _jax.eval_shape(fwd)
    return out.shape, out.dtype

N_MICROBATCH = 1
ADAM_LR = 0.001
ADAM_B1 = 0.9
ADAM_B2 = 0.999
ADAM_EPS = 1e-08
ADAM_WD = 0.01
ADAM_STEP = 10
PER_EXAMPLE_BATCH_AXIS = {'x': 0, 'c': 0, 'loss_target': 0}
SHARED_INPUTS = []
_WEIGHT_DTYPES = {'norm_mix_w': _jnp.float32, 'norm_mlp_w': _jnp.float32, 'w_mod': _jnp.float32, 'b_mod': _jnp.float32, 'w_in': _jnp.float32, 'lru_conv_w': _jnp.float32, 'lru_conv_b': _jnp.float32, 'lru_gate_a_w': _jnp.float32, 'lru_gate_a_b': _jnp.float32, 'lru_gate_x_w': _jnp.float32, 'lru_gate_x_b': _jnp.float32, 'lru_lambda': _jnp.float32, 'lru_norm_w': _jnp.float32, 'gdn_conv_w': _jnp.float32, 'gdn_a_log': _jnp.float32, 'gdn_dt_bias': _jnp.float32, 'gdn_norm_w': _jnp.float32, 'w_out': _jnp.float32, 'w_up': _jnp.float32, 'w_down': _jnp.float32, 'final_norm_w': _jnp.float32}
MOMENT_SCALE = {'norm_mix_w': 1.448742e-01, 'norm_mlp_w': 1.349443e-01, 'w_mod': 1.405557e-01, 'b_mod': 2.505699e-01, 'w_in': 8.603589e-02, 'lru_conv_w': 1.287169e-01, 'lru_conv_b': 6.611975e-01, 'lru_gate_a_w': 2.955435e-02, 'lru_gate_a_b': 2.512109e-02, 'lru_gate_x_w': 5.621210e-02, 'lru_gate_x_b': 4.791305e-02, 'lru_lambda': 5.719720e-02, 'lru_norm_w': 1.220497e-01, 'gdn_conv_w': 5.470529e-02, 'gdn_a_log': 4.289691e-01, 'gdn_dt_bias': 4.125609e-01, 'gdn_norm_w': 1.460177e-01, 'w_out': 1.108240e-01, 'w_up': 6.621642e-02, 'w_down': 1.459041e-01, 'final_norm_w': 3.294479e+01}


def _to_microbatches(a, axis):
    t = _jnp.moveaxis(a, axis, 0)
    t = t.reshape((N_MICROBATCH, t.shape[0] // N_MICROBATCH) + t.shape[1:])
    return _jnp.moveaxis(t, 1, axis + 1)


def setup_inputs(seed: int = 0) -> dict:
    inp = _fwd_setup_inputs(seed)
    key = _jax.random.fold_in(_jax.random.key(seed), 7919)
    shape, _ = _output_shape()
    out = dict(inp)
    out["loss_target"] = _jax.random.normal(_jax.random.fold_in(key, 0), shape, _jnp.float32)
    for i, name in enumerate(TWIN_WEIGHTS):
        w = inp[name].astype(_jnp.float32)
        if MOMENT_SCALE is None:
            s = _jnp.sqrt(_jnp.mean(_jnp.square(w)) + 1e-30)
        else:
            s = MOMENT_SCALE[name]
        km, kv = _jax.random.split(_jax.random.fold_in(key, i + 1))
        out[name] = w
        out["m_" + name] = s * _jax.random.normal(km, w.shape, _jnp.float32)
        out["v_" + name] = (s * s) * _jax.random.uniform(kv, w.shape, _jnp.float32, 0.5, 1.5)
    if N_MICROBATCH > 1:
        for name, axis in PER_EXAMPLE_BATCH_AXIS.items():
            out[name] = _to_microbatches(out[name], axis)
    return {'x': out['x'], 'c': out['c'], 'norm_mix_w': out['norm_mix_w'], 'norm_mlp_w': out['norm_mlp_w'], 'w_mod': out['w_mod'], 'b_mod': out['b_mod'], 'w_in': out['w_in'], 'lru_conv_w': out['lru_conv_w'], 'lru_conv_b': out['lru_conv_b'], 'lru_gate_a_w': out['lru_gate_a_w'], 'lru_gate_a_b': out['lru_gate_a_b'], 'lru_gate_x_w': out['lru_gate_x_w'], 'lru_gate_x_b': out['lru_gate_x_b'], 'lru_lambda': out['lru_lambda'], 'lru_norm_w': out['lru_norm_w'], 'gdn_conv_w': out['gdn_conv_w'], 'gdn_a_log': out['gdn_a_log'], 'gdn_dt_bias': out['gdn_dt_bias'], 'gdn_norm_w': out['gdn_norm_w'], 'w_out': out['w_out'], 'w_up': out['w_up'], 'w_down': out['w_down'], 'final_norm_w': out['final_norm_w'], 'loss_target': out['loss_target'], 'm_norm_mix_w': out['m_norm_mix_w'], 'm_norm_mlp_w': out['m_norm_mlp_w'], 'm_w_mod': out['m_w_mod'], 'm_b_mod': out['m_b_mod'], 'm_w_in': out['m_w_in'], 'm_lru_conv_w': out['m_lru_conv_w'], 'm_lru_conv_b': out['m_lru_conv_b'], 'm_lru_gate_a_w': out['m_lru_gate_a_w'], 'm_lru_gate_a_b': out['m_lru_gate_a_b'], 'm_lru_gate_x_w': out['m_lru_gate_x_w'], 'm_lru_gate_x_b': out['m_lru_gate_x_b'], 'm_lru_lambda': out['m_lru_lambda'], 'm_lru_norm_w': out['m_lru_norm_w'], 'm_gdn_conv_w': out['m_gdn_conv_w'], 'm_gdn_a_log': out['m_gdn_a_log'], 'm_gdn_dt_bias': out['m_gdn_dt_bias'], 'm_gdn_norm_w': out['m_gdn_norm_w'], 'm_w_out': out['m_w_out'], 'm_w_up': out['m_w_up'], 'm_w_down': out['m_w_down'], 'm_final_norm_w': out['m_final_norm_w'], 'v_norm_mix_w': out['v_norm_mix_w'], 'v_norm_mlp_w': out['v_norm_mlp_w'], 'v_w_mod': out['v_w_mod'], 'v_b_mod': out['v_b_mod'], 'v_w_in': out['v_w_in'], 'v_lru_conv_w': out['v_lru_conv_w'], 'v_lru_conv_b': out['v_lru_conv_b'], 'v_lru_gate_a_w': out['v_lru_gate_a_w'], 'v_lru_gate_a_b': out['v_lru_gate_a_b'], 'v_lru_gate_x_w': out['v_lru_gate_x_w'], 'v_lru_gate_x_b': out['v_lru_gate_x_b'], 'v_lru_lambda': out['v_lru_lambda'], 'v_lru_norm_w': out['v_lru_norm_w'], 'v_gdn_conv_w': out['v_gdn_conv_w'], 'v_gdn_a_log': out['v_gdn_a_log'], 'v_gdn_dt_bias': out['v_gdn_dt_bias'], 'v_gdn_norm_w': out['v_gdn_norm_w'], 'v_w_out': out['v_w_out'], 'v_w_up': out['v_w_up'], 'v_w_down': out['v_w_down'], 'v_final_norm_w': out['v_final_norm_w']}


def _loss(weights, diff, rest, loss_target):
    with _jax.named_scope("forward"):
        args = {**rest, TWIN_DIFF_INPUT: diff, **{k: w.astype(_WEIGHT_DTYPES[k]) for k, w in weights.items()}}
        y = _forward(args)
    with _jax.named_scope("loss_head"):
        err = _jnp.square(y.astype(_jnp.float32) - loss_target)
        return 0.5 * _jnp.sum(_jnp.mean(err, axis=-1)) if err.ndim else 0.5 * err


def _adamw(w, g, m, v):
    m = ADAM_B1 * m + (1.0 - ADAM_B1) * g
    v = ADAM_B2 * v + (1.0 - ADAM_B2) * _jnp.square(g)
    m_hat = m / (1.0 - ADAM_B1 ** ADAM_STEP)
    v_hat = v / (1.0 - ADAM_B2 ** ADAM_STEP)
    delta = -ADAM_LR * (m_hat / (_jnp.sqrt(v_hat) + ADAM_EPS) + ADAM_WD * w)
    return delta, m, v


def reference(x, c, norm_mix_w, norm_mlp_w, w_mod, b_mod, w_in, lru_conv_w, lru_conv_b, lru_gate_a_w, lru_gate_a_b, lru_gate_x_w, lru_gate_x_b, lru_lambda, lru_norm_w, gdn_conv_w, gdn_a_log, gdn_dt_bias, gdn_norm_w, w_out, w_up, w_down, final_norm_w, loss_target, m_norm_mix_w, m_norm_mlp_w, m_w_mod, m_b_mod, m_w_in, m_lru_conv_w, m_lru_conv_b, m_lru_gate_a_w, m_lru_gate_a_b, m_lru_gate_x_w, m_lru_gate_x_b, m_lru_lambda, m_lru_norm_w, m_gdn_conv_w, m_gdn_a_log, m_gdn_dt_bias, m_gdn_norm_w, m_w_out, m_w_up, m_w_down, m_final_norm_w, v_norm_mix_w, v_norm_mlp_w, v_w_mod, v_b_mod, v_w_in, v_lru_conv_w, v_lru_conv_b, v_lru_gate_a_w, v_lru_gate_a_b, v_lru_gate_x_w, v_lru_gate_x_b, v_lru_lambda, v_lru_norm_w, v_gdn_conv_w, v_gdn_a_log, v_gdn_dt_bias, v_gdn_norm_w, v_w_out, v_w_up, v_w_down, v_final_norm_w):
    given = dict(x=x, c=c, norm_mix_w=norm_mix_w, norm_mlp_w=norm_mlp_w, w_mod=w_mod, b_mod=b_mod, w_in=w_in, lru_conv_w=lru_conv_w, lru_conv_b=lru_conv_b, lru_gate_a_w=lru_gate_a_w, lru_gate_a_b=lru_gate_a_b, lru_gate_x_w=lru_gate_x_w, lru_gate_x_b=lru_gate_x_b, lru_lambda=lru_lambda, lru_norm_w=lru_norm_w, gdn_conv_w=gdn_conv_w, gdn_a_log=gdn_a_log, gdn_dt_bias=gdn_dt_bias, gdn_norm_w=gdn_norm_w, w_out=w_out, w_up=w_up, w_down=w_down, final_norm_w=final_norm_w, loss_target=loss_target, m_norm_mix_w=m_norm_mix_w, m_norm_mlp_w=m_norm_mlp_w, m_w_mod=m_w_mod, m_b_mod=m_b_mod, m_w_in=m_w_in, m_lru_conv_w=m_lru_conv_w, m_lru_conv_b=m_lru_conv_b, m_lru_gate_a_w=m_lru_gate_a_w, m_lru_gate_a_b=m_lru_gate_a_b, m_lru_gate_x_w=m_lru_gate_x_w, m_lru_gate_x_b=m_lru_gate_x_b, m_lru_lambda=m_lru_lambda, m_lru_norm_w=m_lru_norm_w, m_gdn_conv_w=m_gdn_conv_w, m_gdn_a_log=m_gdn_a_log, m_gdn_dt_bias=m_gdn_dt_bias, m_gdn_norm_w=m_gdn_norm_w, m_w_out=m_w_out, m_w_up=m_w_up, m_w_down=m_w_down, m_final_norm_w=m_final_norm_w, v_norm_mix_w=v_norm_mix_w, v_norm_mlp_w=v_norm_mlp_w, v_w_mod=v_w_mod, v_b_mod=v_b_mod, v_w_in=v_w_in, v_lru_conv_w=v_lru_conv_w, v_lru_conv_b=v_lru_conv_b, v_lru_gate_a_w=v_lru_gate_a_w, v_lru_gate_a_b=v_lru_gate_a_b, v_lru_gate_x_w=v_lru_gate_x_w, v_lru_gate_x_b=v_lru_gate_x_b, v_lru_lambda=v_lru_lambda, v_lru_norm_w=v_lru_norm_w, v_gdn_conv_w=v_gdn_conv_w, v_gdn_a_log=v_gdn_a_log, v_gdn_dt_bias=v_gdn_dt_bias, v_gdn_norm_w=v_gdn_norm_w, v_w_out=v_w_out, v_w_up=v_w_up, v_w_down=v_w_down, v_final_norm_w=v_final_norm_w)
    weights = {n: given[n] for n in TWIN_WEIGHTS}
    shared = {n: given[n] for n in SHARED_INPUTS}
    per_example = {n: given[n] for n in ['x', 'c']}
    grad_fn = _jax.value_and_grad(_loss, argnums=(0, 1))

    def one_microbatch(ex, loss_target):
        ex = dict(ex)
        diff = ex.pop(TWIN_DIFF_INPUT)
        return grad_fn(weights, diff, {**shared, **ex}, loss_target)

    if N_MICROBATCH == 1:
        loss, (grad_w, grad_x) = one_microbatch(per_example, given["loss_target"])
    else:
        def body(carry, xs):
            loss_sum, grad_sum = carry
            l_k, (gw_k, gx_k) = one_microbatch(xs[0], xs[1])
            with _jax.named_scope("update"):
                return (loss_sum + l_k, _jax.tree.map(_jnp.add, grad_sum, gw_k)), gx_k

        init = (_jnp.zeros((), _jnp.float32), _jax.tree.map(_jnp.zeros_like, weights))
        (loss, grad_w), grad_x = _jax.lax.scan(body, init, (per_example, given["loss_target"]))
    with _jax.named_scope("update"):
        delta_w, new_m, new_v = {}, {}, {}
        for n in TWIN_WEIGHTS:
            delta_w[n], new_m[n], new_v[n] = _adamw(weights[n], grad_w[n], given["m_" + n], given["v_" + n])
    return (loss, grad_x, *[grad_w[n] for n in TWIN_WEIGHTS], *[delta_w[n] for n in TWIN_WEIGHTS],
            *[new_m[n] for n in TWIN_WEIGHTS], *[new_v[n] for n in TWIN_WEIGHTS])
```

```python
import functools

import jax
import jax.numpy as jnp
from jax import lax
from jax.experimental import pallas as pl
from jax.experimental.pallas import tpu as pltpu

F32 = jnp.float32
BF16 = jnp.bfloat16
MESH = pl.DeviceIdType.MESH

D_MODEL = 1024
DEPTH = 4
LRU_W = 512
LRU_BLOCKS = 8
LRU_BLOCK = 64
LRU_C = 8.0
HEADS = 4
HEAD_DIM = 128
GDN_W = 512
CHUNK = 128
D_FF = 4096
N_MOD = 6
IN_COLS = 3080
IN_PAD = 3200
NORM_EPS = 1e-6
LANES = 128
SUBLANES = 8
N_DEV = 8
N_CHIPS = 4

ADAM_LR = 0.001
ADAM_B1 = 0.9
ADAM_B2 = 0.999
ADAM_EPS = 1e-08
ADAM_WD = 0.01
ADAM_STEP = 10

VMEM_LIMIT = 56 * 1024 * 1024
HI = lax.Precision.HIGHEST


def _sds(shape, dtype=F32):
    return jax.ShapeDtypeStruct(tuple(shape), dtype)


def _params(sem=None, vmem=VMEM_LIMIT):
    return pltpu.CompilerParams(dimension_semantics=sem, vmem_limit_bytes=vmem)


def _const(shape):
    return pl.BlockSpec(tuple(shape), lambda *_: (0,) * len(shape))


def _row(tm, c, col=0):
    return pl.BlockSpec((tm, c), lambda i: (i, col))


def _dot(a, b):
    return jnp.dot(a, b, preferred_element_type=F32)


def _dot_nt(a, b):
    return lax.dot_general(a, b, (((1,), (1,)), ((), ())), preferred_element_type=F32)


def _dot_tn(a, b):
    return lax.dot_general(a, b, (((0,), (0,)), ((), ())), preferred_element_type=F32)


def _hdot(a, b):
    return jnp.dot(a, b, preferred_element_type=F32, precision=HI)


def _hdot_nt(a, b):
    return lax.dot_general(a, b, (((1,), (1,)), ((), ())), preferred_element_type=F32, precision=HI)


def _hdot_tn(a, b):
    return lax.dot_general(a, b, (((0,), (0,)), ((), ())), preferred_element_type=F32, precision=HI)


def _acc(ref, val, first):
    @pl.when(first)
    def _():
        ref[...] = val

    @pl.when(jnp.logical_not(first))
    def _():
        ref[...] += val


def _colsum(v):
    return jnp.sum(v, axis=0, keepdims=True)


def _rms_parts(x):
    r = lax.rsqrt(jnp.mean(x * x, axis=-1, keepdims=True) + NORM_EPS)
    return x * r, r


def _rms_bwd(dy, xh, r, w):
    dxh = dy * w
    dw = _colsum(dy * xh)
    dx = r * (dxh - xh * jnp.mean(dxh * xh, axis=-1, keepdims=True))
    return dx, dw


def _norm_mod(x, w, sc, sh):
    xh, _ = _rms_parts(x)
    return (xh * w) * (1.0 + sc) + sh


def _norm_mod_bwd(dy, x, w, sc):
    xh, r = _rms_parts(x)
    n = xh * w
    dsh = _colsum(dy)
    dsc = _colsum(dy * n)
    dx, dw = _rms_bwd(dy * (1.0 + sc), xh, r, w)
    return dx, dw, dsc, dsh


def _softplus(x):
    return jnp.maximum(x, 0.0) + jnp.log1p(jnp.exp(-jnp.abs(x)))


def _silu(x):
    return x * jax.nn.sigmoid(x)


def _silu_grad(x):
    s = jax.nn.sigmoid(x)
    return s * (1.0 + x * (1.0 - s))


def _roll_dn(x, d):
    return x if d == 0 else pltpu.roll(x, d, 0)


def _roll_up(x, d):
    return x if d == 0 else pltpu.roll(x, x.shape[0] - d, 0)


def _proj_fwd(x, nw, sc, sh, win):
    S = x.shape[0]
    tm = min(512, S)

    def body(x_ref, nw_ref, sc_ref, sh_ref, w_ref, proj_ref, hb_ref):
        hb = _norm_mod(x_ref[...], nw_ref[...], sc_ref[...], sh_ref[...]).astype(BF16)
        hb_ref[...] = hb
        proj_ref[...] = _dot(hb, w_ref[...])

    vec = _const((1, D_MODEL))
    return pl.pallas_call(
        body, name="proj_fwd", grid=(S // tm,),
        in_specs=[_row(tm, D_MODEL), vec, vec, vec, _const((D_MODEL, IN_PAD))],
        out_specs=[_row(tm, IN_PAD), _row(tm, D_MODEL)],
        out_shape=[_sds((S, IN_PAD)), _sds((S, D_MODEL), BF16)],
        compiler_params=_params(("arbitrary",)),
    )(x, nw, sc, sh, win)


def _proj_bwd(dx1, x, dlx, dly, dqkv, dz, dba, nw, sc, win):
    S = x.shape[0]
    tm = min(512, S)

    def body(dx1_ref, x_ref, dlx_ref, dly_ref, dqkv_ref, dz_ref, dba_ref, nw_ref, sc_ref, w_ref,
             dx_ref, dpb_ref, dnw_ref, dsc_ref, dsh_ref):
        i = pl.program_id(0)
        dba = dba_ref[0] + dba_ref[1] + dba_ref[2] + dba_ref[3]
        dpb = jnp.concatenate([dlx_ref[...], dly_ref[...], dqkv_ref[...], dz_ref[...], dba], axis=-1).astype(BF16)
        dpb_ref[...] = dpb
        dh = _dot_nt(dpb, w_ref[...])
        dx, dnw, dsc, dsh = _norm_mod_bwd(dh, x_ref[...], nw_ref[...], sc_ref[...])
        dx_ref[...] = dx1_ref[...] + dx
        _acc(dnw_ref, dnw, i == 0)
        _acc(dsc_ref, dsc, i == 0)
        _acc(dsh_ref, dsh, i == 0)

    vec = _const((1, D_MODEL))
    return pl.pallas_call(
        body, name="proj_bwd", grid=(S // tm,),
        in_specs=[_row(tm, D_MODEL), _row(tm, D_MODEL), _row(tm, LRU_W), _row(tm, LRU_W), _row(tm, 3 * GDN_W),
                  _row(tm, GDN_W), pl.BlockSpec((HEADS, tm, LANES), lambda i: (0, i, 0)), vec, vec,
                  _const((D_MODEL, IN_PAD))],
        out_specs=[_row(tm, D_MODEL), _row(tm, IN_PAD), vec, vec, vec],
        out_shape=[_sds((S, D_MODEL)), _sds((S, IN_PAD), BF16), _sds((1, D_MODEL)), _sds((1, D_MODEL)),
                   _sds((1, D_MODEL))],
        compiler_params=_params(("arbitrary",)),
    )(dx1, x, dlx, dly, dqkv, dz, dba, nw, sc, win)


def _conv_taps(xx, w, tm):
    y = _roll_dn(xx, 3)[SUBLANES:] * w[0:1]
    y = y + _roll_dn(xx, 2)[SUBLANES:] * w[1:2]
    y = y + _roll_dn(xx, 1)[SUBLANES:] * w[2:3]
    y = y + xx[SUBLANES:] * w[3:4]
    return y


def _conv_fwd(src, col0, C, w8, b, act, name):
    S = src.shape[0]
    tm = min(512, S)
    tc = 512
    hb = tm // SUBLANES
    cb0 = col0 // tc

    def body(x_ref, p_ref, w_ref, b_ref, y_ref):
        i = pl.program_id(0)
        prev = jnp.where(i > 0, p_ref[...], 0.0)
        xx = jnp.concatenate([prev, x_ref[...]], axis=0)
        y = _conv_taps(xx, w_ref[...], tm) + b_ref[...]
        y_ref[...] = _silu(y) if act else y

    return pl.pallas_call(
        body, name=name, grid=(S // tm, C // tc),
        in_specs=[pl.BlockSpec((tm, tc), lambda i, j: (i, cb0 + j)),
                  pl.BlockSpec((SUBLANES, tc), lambda i, j: (jnp.maximum(i * hb - 1, 0), cb0 + j)),
                  pl.BlockSpec((SUBLANES, tc), lambda i, j: (0, j)),
                  pl.BlockSpec((1, tc), lambda i, j: (0, j))],
        out_specs=pl.BlockSpec((tm, tc), lambda i, j: (i, j)),
        out_shape=_sds((S, C)),
        compiler_params=_params(("arbitrary", "arbitrary")),
    )(src, src, w8, b)


def _conv_bwd(src, col0, C, w8, b, dyact, act, name):
    S = src.shape[0]
    tm = min(512, S)
    tc = 512
    hb = tm // SUBLANES
    nt = S // tm
    cb0 = col0 // tc
    last_hb = S // SUBLANES - 1

    def body(x_ref, p_ref, n_ref, dy_ref, dyn_ref, w_ref, b_ref, dx_ref, dw_ref, db_ref):
        i = pl.program_id(1)
        w = w_ref[...]
        prev = jnp.where(i > 0, p_ref[...], 0.0)
        xx = jnp.concatenate([prev, x_ref[...], n_ref[...]], axis=0)
        dy = jnp.concatenate([dy_ref[...], jnp.where(i < nt - 1, dyn_ref[...], 0.0)], axis=0)
        if act:
            ypre = _conv_taps(xx, w, tm + SUBLANES) + b_ref[...]
            dy = dy * _silu_grad(ypre)
        dx = dy[:tm] * w[3:4]
        for d in (1, 2, 3):
            dx = dx + _roll_up(dy, d)[:tm] * w[3 - d:4 - d]
        dx_ref[...] = dx
        xt = xx[:tm + SUBLANES]
        dyt = dy[:tm]
        rows = [_colsum(dyt * _roll_dn(xt, 3 - k)[SUBLANES:]) for k in range(4)]
        dw = jnp.concatenate(rows + [jnp.zeros((SUBLANES - 4, tc), F32)], axis=0)
        _acc(dw_ref, dw, i == 0)
        _acc(db_ref, _colsum(dyt), i == 0)

    return pl.pallas_call(
        body, name=name, grid=(C // tc, nt),
        in_specs=[pl.BlockSpec((tm, tc), lambda j, i: (i, cb0 + j)),
                  pl.BlockSpec((SUBLANES, tc), lambda j, i: (jnp.maximum(i * hb - 1, 0), cb0 + j)),
                  pl.BlockSpec((SUBLANES, tc), lambda j, i: (jnp.minimum((i + 1) * hb, last_hb), cb0 + j)),
                  pl.BlockSpec((tm, tc), lambda j, i: (i, j)),
                  pl.BlockSpec((SUBLANES, tc), lambda j, i: (jnp.minimum((i + 1) * hb, last_hb), j)),
                  pl.BlockSpec((SUBLANES, tc), lambda j, i: (0, j)),
                  pl.BlockSpec((1, tc), lambda j, i: (0, j))],
        out_specs=[pl.BlockSpec((tm, tc), lambda j, i: (i, j)),
                   pl.BlockSpec((SUBLANES, tc), lambda j, i: (0, j)),
                   pl.BlockSpec((1, tc), lambda j, i: (0, j))],
        out_shape=[_sds((S, C)), _sds((SUBLANES, C)), _sds((1, C))],
        compiler_params=_params(("arbitrary", "arbitrary")),
    )(src, src, src, dyact, dyact, w8, b)


def _lru_ab(pre_a, pre_x, xr, lam):
    r = jax.nn.sigmoid(pre_a)
    g = jax.nn.sigmoid(pre_x)
    log_sig = -_softplus(-lam)
    log_a = LRU_C * r * log_sig
    a = jnp.exp(log_a)
    t = jnp.tanh(log_a)
    mult = jnp.sqrt(jnp.maximum(-2.0 * t / (1.0 - t), 1e-12))
    return a, mult * (g * xr)


def _lru_tail(h, ly, lnw):
    xh, _ = _rms_parts(h * jax.nn.gelu(ly))
    return xh * lnw


def _scan_down(a, b):
    n = a.shape[0]
    row = lax.broadcasted_iota(jnp.int32, a.shape, 0)
    d = 1
    while d < n:
        keep = row >= d
        a_s = jnp.where(keep, _roll_dn(a, d), 1.0)
        b_s = jnp.where(keep, _roll_dn(b, d), 0.0)
        b = a * b_s + b
        a = a * a_s
        d *= 2
    return a, b


def _scan_up(a, b):
    n = a.shape[0]
    row = lax.broadcasted_iota(jnp.int32, a.shape, 0)
    d = 1
    while d < n:
        keep = row < n - d
        a_s = jnp.where(keep, _roll_up(a, d), 1.0)
        b_s = jnp.where(keep, _roll_up(b, d), 0.0)
        b = a * b_s + b
        a = a * a_s
        d *= 2
    return a, b


LRU_TM = 256


def _lru_fwd(xr, proj, wa, ba, wx, bx, lam, lnw):
    S = xr.shape[0]
    tm = min(LRU_TM, S)

    def body(xr_ref, ly_ref, wa_ref, ba_ref, wx_ref, bx_ref, lam_ref, lnw_ref, out_ref, h_ref, carry):
        i = pl.program_id(0)

        @pl.when(i == 0)
        def _():
            carry[...] = jnp.zeros_like(carry)

        x = xr_ref[...]
        xb = x.astype(BF16)
        pre_a = _dot(xb, wa_ref[...]) + ba_ref[...]
        pre_x = _dot(xb, wx_ref[...]) + bx_ref[...]
        a, b = _lru_ab(pre_a, pre_x, x, lam_ref[...])
        ca, hl = _scan_down(a, b)
        h = hl + ca * carry[0:1, :]
        carry[0:1, :] = h[tm - 1:tm, :]
        h_ref[...] = h
        out_ref[...] = _lru_tail(h, ly_ref[...], lnw_ref[...])

    vec = _const((1, LRU_W))
    mat = _const((LRU_W, LRU_W))
    return pl.pallas_call(
        body, name="lru_fwd", grid=(S // tm,),
        in_specs=[_row(tm, LRU_W), _row(tm, LRU_W, 1), mat, vec, mat, vec, vec, vec],
        out_specs=[_row(tm, LRU_W), _row(tm, LRU_W)],
        out_shape=[_sds((S, LRU_W)), _sds((S, LRU_W))],
        scratch_shapes=[pltpu.VMEM((SUBLANES, LRU_W), F32)],
        compiler_params=_params(("arbitrary",)),
    )(xr, proj, wa, ba, wx, bx, lam, lnw)


def _lru_bwd(dout, xr, proj, h, wa, ba, wx, bx, lam, lnw):
    S = xr.shape[0]
    tm = min(LRU_TM, S)
    nt = S // tm
    hb = tm // SUBLANES

    def rev(col=0):
        return pl.BlockSpec((tm, LRU_W), lambda i: (nt - 1 - i, col))

    def body(dout_ref, xr_ref, ly_ref, h_ref, hp_ref, wa_ref, ba_ref, wx_ref, bx_ref, lam_ref, lnw_ref,
             dxr_ref, dly_ref, dwa_ref, dba_ref, dwx_ref, dbx_ref, dlam_ref, dlnw_ref, carry):
        i = pl.program_id(0)
        first = i == 0

        @pl.when(first)
        def _():
            carry[...] = jnp.zeros_like(carry)

        x = xr_ref[...]
        xb = x.astype(BF16)
        pre_a = _dot(xb, wa_ref[...]) + ba_ref[...]
        pre_x = _dot(xb, wx_ref[...]) + bx_ref[...]
        (a, b), ab_vjp = jax.vjp(_lru_ab, pre_a, pre_x, x, lam_ref[...])
        h_t = h_ref[...]
        _, tail_vjp = jax.vjp(_lru_tail, h_t, ly_ref[...], lnw_ref[...])
        dh, dly, dlnw = tail_vjp(dout_ref[...])
        dly_ref[...] = dly
        row = lax.broadcasted_iota(jnp.int32, a.shape, 0)
        a_next = jnp.where(row == tm - 1, carry[0:1, :], _roll_up(a, 1))
        ca, gl = _scan_up(a_next, dh)
        g = gl + ca * carry[1:2, :]
        carry[0:1, :] = a[0:1, :]
        carry[1:2, :] = g[0:1, :]
        h_before = jnp.where(i == nt - 1, 0.0, hp_ref[SUBLANES - 1:SUBLANES, :])
        h_prev = jnp.where(row == 0, h_before, _roll_dn(h_t, 1))
        dpa, dpx, dx, dlam = ab_vjp((g * h_prev, g))
        dpab = dpa.astype(BF16)
        dpxb = dpx.astype(BF16)
        dxr_ref[...] = dx + _dot_nt(dpab, wa_ref[...]) + _dot_nt(dpxb, wx_ref[...])
        _acc(dwa_ref, _dot_tn(xb, dpab), first)
        _acc(dwx_ref, _dot_tn(xb, dpxb), first)
        _acc(dba_ref, _colsum(dpa), first)
        _acc(dbx_ref, _colsum(dpx), first)
        _acc(dlam_ref, dlam, first)
        _acc(dlnw_ref, dlnw, first)

    vec = _const((1, LRU_W))
    mat = _const((LRU_W, LRU_W))
    return pl.pallas_call(
        body, name="lru_bwd", grid=(nt,),
        in_specs=[rev(), rev(), rev(1), rev(),
                  pl.BlockSpec((SUBLANES, LRU_W), lambda i: (jnp.maximum((nt - 1 - i) * hb - 1, 0), 0)),
                  mat, vec, mat, vec, vec, vec],
        out_specs=[rev(), rev(), mat, vec, mat, vec, vec, vec],
        out_shape=[_sds((S, LRU_W)), _sds((S, LRU_W)), _sds((LRU_W, LRU_W)), _sds((1, LRU_W)),
                   _sds((LRU_W, LRU_W)), _sds((1, LRU_W)), _sds((1, LRU_W)), _sds((1, LRU_W))],
        scratch_shapes=[pltpu.VMEM((SUBLANES, LRU_W), F32)],
        compiler_params=_params(("arbitrary",)),
    )(dout, xr, proj, h, h, wa, ba, wx, bx, lam, lnw)


def _lane_pick(row_or_tile, lane):
    idx = lax.broadcasted_iota(jnp.int32, row_or_tile.shape, 1)
    return jnp.sum(jnp.where(idx == lane, row_or_tile, 0.0), axis=-1, keepdims=True)


def _unit_lower_inverse(lo):
    n = lo.shape[0]
    ri = lax.broadcasted_iota(jnp.int32, (n, n), 0)
    ci = lax.broadcasted_iota(jnp.int32, (n, n), 1)
    inv = (ri == ci).astype(F32)
    s = 1
    while s < n:
        same_block = (ri & ~(2 * s - 1)) == (ci & ~(2 * s - 1))
        lower_left = same_block & ((ri & s) != 0) & ((ci & s) == 0)
        b = jnp.where(lower_left, lo, 0.0)
        inv = inv - _hdot(_hdot(inv, b), inv)
        s *= 2
    return inv


def _gdn_chunk(q, k, v, ba, alog, dtb, state, head):
    C = q.shape[0]
    beta = jax.nn.sigmoid(_lane_pick(ba, head))
    alpha = _lane_pick(ba, head + HEADS)
    g = -jnp.exp(_lane_pick(alog, head)) * _softplus(alpha + _lane_pick(dtb, head))
    qn = q * lax.rsqrt(jnp.sum(q * q, axis=-1, keepdims=True) + 1e-6) * (HEAD_DIM ** -0.5)
    kn = k * lax.rsqrt(jnp.sum(k * k, axis=-1, keepdims=True) + 1e-6)
    ri = lax.broadcasted_iota(jnp.int32, (C, C), 0)
    ci = lax.broadcasted_iota(jnp.int32, (C, C), 1)
    causal = ri >= ci
    strict = ri > ci
    gc = _hdot(causal.astype(F32), jnp.broadcast_to(g, (C, C)))
    diff = jnp.where(causal, gc - gc.T, 0.0)
    decay = jnp.where(causal, jnp.exp(diff), 0.0)
    eg = jnp.exp(gc)
    kb = kn * beta
    vb = v * beta
    lo = jnp.where(strict, _hdot_nt(kb, kn) * decay, 0.0)
    tinv = _unit_lower_inverse(lo)
    u = _hdot(tinv, vb)
    w = _hdot(tinv, kb * eg)
    attn = jnp.where(causal, _hdot_nt(qn, kn) * decay, 0.0)
    g_last = gc[C - 1:C, :]
    k_tail = kn * jnp.exp(g_last - gc)
    v_new = u - _hdot(w, state)
    o = _hdot(qn * eg, state) + _hdot(attn, v_new)
    new_state = state * jnp.exp(g_last) + _hdot_tn(k_tail, v_new)
    return o, new_state


def _gdn_fwd(qkv, proj, alog, dtb):
    S = qkv.shape[0]
    nc = S // CHUNK
    assert CHUNK == HEAD_DIM

    def body(q_ref, k_ref, v_ref, ba_ref, alog_ref, dtb_ref, o_ref, st_ref, state):
        head = pl.program_id(0)

        @pl.when(pl.program_id(1) == 0)
        def _():
            state[...] = jnp.zeros_like(state)

        s0 = state[...]
        st_ref[0, 0] = s0
        o, s1 = _gdn_chunk(q_ref[...], k_ref[...], v_ref[...], ba_ref[...], alog_ref[...], dtb_ref[...], s0, head)
        o_ref[...] = o
        state[...] = s1

    def col(off):
        return pl.BlockSpec((CHUNK, HEAD_DIM), lambda hd, n: (n, off + hd))

    vec = _const((1, LANES))
    return pl.pallas_call(
        body, name="gdn_fwd", grid=(HEADS, nc),
        in_specs=[col(0), col(HEADS), col(2 * HEADS),
                  pl.BlockSpec((CHUNK, LANES), lambda hd, n: (n, IN_PAD // LANES - 1)), vec, vec],
        out_specs=[col(0), pl.BlockSpec((1, 1, HEAD_DIM, HEAD_DIM), lambda hd, n: (hd, n, 0, 0))],
        out_shape=[_sds((S, GDN_W)), _sds((HEADS, nc, HEAD_DIM, HEAD_DIM))],
        scratch_shapes=[pltpu.VMEM((HEAD_DIM, HEAD_DIM), F32)],
        compiler_params=_params(("arbitrary", "arbitrary")),
    )(qkv, qkv, qkv, proj, alog, dtb)


def _gdn_bwd(do, qkv, proj, states, alog, dtb):
    S = qkv.shape[0]
    nc = S // CHUNK

    def body(do_ref, q_ref, k_ref, v_ref, ba_ref, st_ref, alog_ref, dtb_ref,
             dq_ref, dk_ref, dv_ref, dba_ref, dalog_ref, ddtb_ref, dstate):
        head = pl.program_id(0)
        n = pl.program_id(1)

        @pl.when(n == 0)
        def _():
            dstate[...] = jnp.zeros_like(dstate)

        fn = functools.partial(_gdn_chunk, head=head)
        _, vjp = jax.vjp(fn, q_ref[...], k_ref[...], v_ref[...], ba_ref[...], alog_ref[...], dtb_ref[...], st_ref[0, 0])
        dq, dk, dv, dba, dalog, ddtb, ds = vjp((do_ref[...], dstate[...]))
        dq_ref[...] = dq
        dk_ref[...] = dk
        dv_ref[...] = dv
        dba_ref[0] = dba
        dstate[...] = ds
        first = jnp.logical_and(head == 0, n == 0)
        _acc(dalog_ref, dalog, first)
        _acc(ddtb_ref, ddtb, first)

    def col(off):
        return pl.BlockSpec((CHUNK, HEAD_DIM), lambda hd, n: (nc - 1 - n, off + hd))

    vec = _const((1, LANES))
    outs = pl.pallas_call(
        body, name="gdn_bwd", grid=(HEADS, nc),
        in_specs=[col(0), col(0), col(HEADS), col(2 * HEADS),
                  pl.BlockSpec((CHUNK, LANES), lambda hd, n: (nc - 1 - n, IN_PAD // LANES - 1)),
                  pl.BlockSpec((1, 1, HEAD_DIM, HEAD_DIM), lambda hd, n: (hd, nc - 1 - n, 0, 0)), vec, vec],
        out_specs=[col(0), col(0), col(0),
                   pl.BlockSpec((1, CHUNK, LANES), lambda hd, n: (hd, nc - 1 - n, 0)), vec, vec],
        out_shape=[_sds((S, GDN_W)), _sds((S, GDN_W)), _sds((S, GDN_W)), _sds((HEADS, S, LANES)),
                   _sds((1, LANES)), _sds((1, LANES))],
        scratch_shapes=[pltpu.VMEM((HEAD_DIM, HEAD_DIM), F32)],
        compiler_params=_params(("arbitrary", "arbitrary")),
    )(do, qkv, qkv, qkv, proj, states, alog, dtb)
    dq, dk, dv, dba, dalog, ddtb = outs
    return jnp.concatenate([dq, dk, dv], axis=-1), dba, dalog, ddtb


def _gdn_gate(o, z, gnw):
    outs = []
    for hd in range(HEADS):
        sl = slice(hd * HEAD_DIM, (hd + 1) * HEAD_DIM)
        xh, _ = _rms_parts(o[:, sl])
        outs.append(xh * gnw * _silu(z[:, sl]))
    return jnp.concatenate(outs, axis=-1)


def _out_fwd(x, out_lru, o, proj, gnw, g1, wout):
    S = x.shape[0]
    tm = min(512, S)

    def body(x_ref, lru_ref, o_ref, z_ref, gnw_ref, g1_ref, w_ref, x1_ref, cat_ref):
        cat = jnp.concatenate([lru_ref[...], _gdn_gate(o_ref[...], z_ref[...], gnw_ref[...])], axis=-1).astype(BF16)
        cat_ref[...] = cat
        x1_ref[...] = x_ref[...] + g1_ref[...] * _dot(cat, w_ref[...])

    return pl.pallas_call(
        body, name="out_fwd", grid=(S // tm,),
        in_specs=[_row(tm, D_MODEL), _row(tm, LRU_W), _row(tm, GDN_W), _row(tm, GDN_W, 5), _const((1, LANES)),
                  _const((1, D_MODEL)), _const((D_MODEL, D_MODEL))],
        out_specs=[_row(tm, D_MODEL), _row(tm, D_MODEL)],
        out_shape=[_sds((S, D_MODEL)), _sds((S, D_MODEL), BF16)],
        compiler_params=_params(("arbitrary",)),
    )(x, out_lru, o, proj, gnw, g1, wout)


def _out_bwd(dx1, cat, o, proj, gnw, g1, wout):
    S = dx1.shape[0]
    tm = min(512, S)

    def body(dx1_ref, cat_ref, o_ref, z_ref, gnw_ref, g1_ref, w_ref,
             dlru_ref, do_ref, dz_ref, dmb_ref, dgnw_ref, dg1_ref):
        i = pl.program_id(0)
        d1 = dx1_ref[...]
        mix = _dot(cat_ref[...], w_ref[...])
        _acc(dg1_ref, _colsum(d1 * mix), i == 0)
        dmb = (d1 * g1_ref[...]).astype(BF16)
        dmb_ref[...] = dmb
        dcat = _dot_nt(dmb, w_ref[...])
        dlru_ref[...] = dcat[:, :LRU_W]
        _, vjp = jax.vjp(_gdn_gate, o_ref[...], z_ref[...], gnw_ref[...])
        do, dz, dgnw = vjp(dcat[:, LRU_W:])
        do_ref[...] = do
        dz_ref[...] = dz
        _acc(dgnw_ref, dgnw, i == 0)

    return pl.pallas_call(
        body, name="out_bwd", grid=(S // tm,),
        in_specs=[_row(tm, D_MODEL), _row(tm, D_MODEL), _row(tm, GDN_W), _row(tm, GDN_W, 5), _const((1, LANES)),
                  _const((1, D_MODEL)), _const((D_MODEL, D_MODEL))],
        out_specs=[_row(tm, LRU_W), _row(tm, GDN_W), _row(tm, GDN_W), _row(tm, D_MODEL), _const((1, LANES)),
                   _const((1, D_MODEL))],
        out_shape=[_sds((S, LRU_W)), _sds((S, GDN_W)), _sds((S, GDN_W)), _sds((S, D_MODEL), BF16), _sds((1, LANES)),
                   _sds((1, D_MODEL))],
        compiler_params=_params(("arbitrary",)),
    )(dx1, cat, o, proj, gnw, g1, wout)


MLP_TM = 256


def _load_once(step, pairs, sem):
    @pl.when(step == 0)
    def _():
        copies = [pltpu.make_async_copy(src, dst, sem.at[k]) for k, (src, dst) in enumerate(pairs)]
        for cp in copies:
            cp.start()
        for cp in copies:
            cp.wait()


def _mlp_fwd(x1, nw, sc, sh, g2, wup, wdown):
    S = x1.shape[0]
    tm = min(MLP_TM, S)

    def body(x_ref, nw_ref, sc_ref, sh_ref, g2_ref, wup_hbm, wdown_hbm, x2_ref, wup, wdown, sem):
        _load_once(pl.program_id(0), [(wup_hbm, wup), (wdown_hbm, wdown)], sem)
        x = x_ref[...]
        hb = _norm_mod(x, nw_ref[...], sc_ref[...], sh_ref[...]).astype(BF16)
        r = jnp.maximum(_dot(hb, wup[...]), 0.0)
        x2_ref[...] = x + g2_ref[...] * _dot((r * r).astype(BF16), wdown[...])

    vec = _const((1, D_MODEL))
    anyspec = pl.BlockSpec(memory_space=pl.ANY)
    return pl.pallas_call(
        body, name="mlp_fwd", grid=(S // tm,),
        in_specs=[_row(tm, D_MODEL), vec, vec, vec, vec, anyspec, anyspec],
        out_specs=_row(tm, D_MODEL),
        out_shape=_sds((S, D_MODEL)),
        scratch_shapes=[pltpu.VMEM((D_MODEL, D_FF), BF16), pltpu.VMEM((D_FF, D_MODEL), BF16),
                        pltpu.SemaphoreType.DMA((2,))],
        compiler_params=_params(("arbitrary",)),
    )(x1, nw, sc, sh, g2, wup, wdown)


def _mlp_bwd(dx2, x1, nw, sc, sh, g2, wup, wdown):
    S = x1.shape[0]
    tm = min(MLP_TM, S)

    def body(dx2_ref, x_ref, nw_ref, sc_ref, sh_ref, g2_ref, wup_hbm, wdown_hbm,
             dx1_ref, hb_ref, dupb_ref, actb_ref, ddb_ref, dnw_ref, dsc_ref, dsh_ref, dg2_ref, wup, wdown, sem):
        i = pl.program_id(0)
        _load_once(i, [(wup_hbm, wup), (wdown_hbm, wdown)], sem)
        x = x_ref[...]
        d2 = dx2_ref[...]
        hb = _norm_mod(x, nw_ref[...], sc_ref[...], sh_ref[...]).astype(BF16)
        hb_ref[...] = hb
        r = jnp.maximum(_dot(hb, wup[...]), 0.0)
        actb = (r * r).astype(BF16)
        actb_ref[...] = actb
        down = _dot(actb, wdown[...])
        _acc(dg2_ref, _colsum(d2 * down), i == 0)
        ddb = (d2 * g2_ref[...]).astype(BF16)
        ddb_ref[...] = ddb
        dupb = (_dot_nt(ddb, wdown[...]) * (2.0 * r)).astype(BF16)
        dupb_ref[...] = dupb
        dh = _dot_nt(dupb, wup[...])
        dx, dnw, dsc, dsh = _norm_mod_bwd(dh, x, nw_ref[...], sc_ref[...])
        dx1_ref[...] = d2 + dx
        _acc(dnw_ref, dnw, i == 0)
        _acc(dsc_ref, dsc, i == 0)
        _acc(dsh_ref, dsh, i == 0)

    vec = _const((1, D_MODEL))
    anyspec = pl.BlockSpec(memory_space=pl.ANY)
    return pl.pallas_call(
        body, name="mlp_bwd", grid=(S // tm,),
        in_specs=[_row(tm, D_MODEL), _row(tm, D_MODEL), vec, vec, vec, vec, anyspec, anyspec],
        out_specs=[_row(tm, D_MODEL), _row(tm, D_MODEL), _row(tm, D_FF), _row(tm, D_FF), _row(tm, D_MODEL),
                   vec, vec, vec, vec],
        out_shape=[_sds((S, D_MODEL)), _sds((S, D_MODEL), BF16), _sds((S, D_FF), BF16), _sds((S, D_FF), BF16),
                   _sds((S, D_MODEL), BF16), _sds((1, D_MODEL)), _sds((1, D_MODEL)), _sds((1, D_MODEL)),
                   _sds((1, D_MODEL))],
        scratch_shapes=[pltpu.VMEM((D_MODEL, D_FF), BF16), pltpu.VMEM((D_FF, D_MODEL), BF16),
                        pltpu.SemaphoreType.DMA((2,))],
        compiler_params=_params(("arbitrary",)),
    )(dx2, x1, nw, sc, sh, g2, wup, wdown)


def _matmul_tn(a, b, name):
    K, M = a.shape
    N = b.shape[1]
    tk = min(512, K)
    tm = min(512, M)
    tn = 640 if N % 640 == 0 else min(1024, N)
    nk = K // tk

    def body(a_ref, b_ref, o_ref, acc):
        k = pl.program_id(2)
        _acc(acc, _dot_tn(a_ref[...], b_ref[...]), k == 0)

        @pl.when(k == nk - 1)
        def _():
            o_ref[...] = acc[...]

    return pl.pallas_call(
        body, name=name, grid=(M // tm, N // tn, nk),
        in_specs=[pl.BlockSpec((tk, tm), lambda i, j, k: (k, i)), pl.BlockSpec((tk, tn), lambda i, j, k: (k, j))],
        out_specs=pl.BlockSpec((tm, tn), lambda i, j, k: (i, j)),
        out_shape=_sds((M, N)),
        scratch_shapes=[pltpu.VMEM((tm, tn), F32)],
        compiler_params=_params(("arbitrary", "arbitrary", "arbitrary")),
    )(a, b)


def _loss_head(x, target, fnw):
    S = x.shape[0]
    tm = min(512, S)

    def body(x_ref, t_ref, w_ref, dx_ref, loss_ref, dw_ref):
        i = pl.program_id(0)
        w = w_ref[...]
        xh, r = _rms_parts(x_ref[...])
        err = xh * w - t_ref[...]
        part = 0.5 * jnp.sum(jnp.mean(err * err, axis=-1, keepdims=True), axis=0, keepdims=True)
        _acc(loss_ref, jnp.broadcast_to(part, (SUBLANES, LANES)), i == 0)
        dx, dw = _rms_bwd(err * (1.0 / D_MODEL), xh, r, w)
        dx_ref[...] = dx
        _acc(dw_ref, dw, i == 0)

    vec = _const((1, D_MODEL))
    return pl.pallas_call(
        body, name="loss_head", grid=(S // tm,),
        in_specs=[_row(tm, D_MODEL), _row(tm, D_MODEL), vec],
        out_specs=[_row(tm, D_MODEL), _const((SUBLANES, LANES)), vec],
        out_shape=[_sds((S, D_MODEL)), _sds((SUBLANES, LANES)), _sds((1, D_MODEL))],
        compiler_params=_params(("arbitrary",)),
    )(x, target, fnw)


def _block_diag(w):
    eye = jnp.eye(LRU_BLOCKS, dtype=w.dtype)
    return (eye[:, None, :, None] * w[:, :, None, :]).reshape(LRU_W, LRU_W)


def _diag_blocks(m):
    m4 = m.reshape(LRU_BLOCKS, LRU_BLOCK, LRU_BLOCKS, LRU_BLOCK)
    return jnp.stack([m4[g, :, g, :] for g in range(LRU_BLOCKS)])


def _layer_fwd(x, p):
    proj, h1b = _proj_fwd(x, p["nmw"], p["sc1"], p["sh1"], p["win"])
    xr = _conv_fwd(proj, 0, LRU_W, p["lcw"], p["lcb"], False, "conv_lru_fwd")
    out_lru, h = _lru_fwd(xr, proj, p["wa"].astype(BF16), p["ba"], p["wx"].astype(BF16), p["bx"], p["lam"], p["lnw"])
    qkv = _conv_fwd(proj, 2 * LRU_W, 3 * GDN_W, p["gcw"], p["gcb"], True, "conv_gdn_fwd")
    o, states = _gdn_fwd(qkv, proj, p["alog"], p["dtb"])
    x1, cat = _out_fwd(x, out_lru, o, proj, p["gnw"], p["g1"], p["wout"])
    x2 = _mlp_fwd(x1, p["nmlp"], p["sc2"], p["sh2"], p["g2"], p["wup"], p["wdown"])
    res = dict(x=x, proj=proj, h1b=h1b, xr=xr, h=h, qkv=qkv, o=o, states=states, x1=x1, cat=cat)
    return x2, res


def _layer_bwd(dx2, p, r):
    dx1, h2b, dupb, actb, ddb, dnmlp, dsc2, dsh2, dg2 = _mlp_bwd(
        dx2, r["x1"], p["nmlp"], p["sc2"], p["sh2"], p["g2"], p["wup"], p["wdown"])
    g_wup = _matmul_tn(h2b, dupb, "dw_up")
    g_wdown = _matmul_tn(actb, ddb, "dw_down")
    dlru, do, dz, dmb, dgnw, dg1 = _out_bwd(dx1, r["cat"], r["o"], r["proj"], p["gnw"], p["g1"], p["wout"])
    g_wout = _matmul_tn(r["cat"], dmb, "dw_out")
    dqkv_act, dba, dalog, ddtb = _gdn_bwd(do, r["qkv"], r["proj"], r["states"], p["alog"], p["dtb"])
    dqkv, dgcw, _ = _conv_bwd(r["proj"], 2 * LRU_W, 3 * GDN_W, p["gcw"], p["gcb"], dqkv_act, True, "conv_gdn_bwd")
    wab = p["wa"].astype(BF16)
    wxb = p["wx"].astype(BF16)
    dxr, dly, dwa, dba_, dwx, dbx, dlam, dlnw = _lru_bwd(
        dlru, r["xr"], r["proj"], r["h"], wab, p["ba"], wxb, p["bx"], p["lam"], p["lnw"])
    dlx, dlcw, dlcb = _conv_bwd(r["proj"], 0, LRU_W, p["lcw"], p["lcb"], dxr, False, "conv_lru_bwd")
    dx, dpb, dnmw, dsc1, dsh1 = _proj_bwd(dx1, r["x"], dlx, dly, dqkv, dz, dba, p["nmw"], p["sc1"], p["win"])
    g_win = _matmul_tn(r["h1b"], dpb, "dw_in")
    grads = dict(nmw=dnmw, nmlp=dnmlp, sh1=dsh1, sc1=dsc1, g1=dg1, sh2=dsh2, sc2=dsc2, g2=dg2,
                 win=g_win, lcw=dlcw, lcb=dlcb, wa=dwa, ba=dba_, wx=dwx, bx=dbx, lam=dlam, lnw=dlnw,
                 gcw=dgcw, alog=dalog, dtb=ddtb, gnw=dgnw, wout=g_wout, wup=g_wup, wdown=g_wdown)
    return dx, grads


def _local_step(x, target, fnw, layers):
    def fwd(xc, p):
        return _layer_fwd(xc, p)

    x_out, res = lax.scan(fwd, x, layers)
    dx, loss_blk, dfnw = _loss_head(x_out, target, fnw)

    def bwd(dxc, pr):
        p, r = pr
        return _layer_bwd(dxc, p, r)

    dx0, grads = lax.scan(bwd, dx, (layers, res), reverse=True)
    return loss_blk[0, 0], dx0, dfnw, grads


def _prep_layers(norm_mix_w, norm_mlp_w, mod, win_b, lru_conv_w, lru_conv_b, gate_a_w, gate_a_b, gate_x_w, gate_x_b,
                 lru_lambda, lru_norm_w, gdn_conv_w, gdn_a_log, gdn_dt_bias, gdn_norm_w, wout_b, wup_b, wdown_b):
    L = norm_mix_w.shape[0]

    def vec(a):
        return a.reshape(L, 1, -1)

    def lanes(a):
        return jnp.pad(a, ((0, 0), (0, LANES - a.shape[1]))).reshape(L, 1, LANES)

    def taps(w):
        return jnp.pad(w, ((0, 0), (0, SUBLANES - w.shape[1]), (0, 0)))

    m = mod.reshape(L, N_MOD, 1, D_MODEL)
    return dict(
        nmw=vec(norm_mix_w), nmlp=vec(norm_mlp_w),
        sh1=m[:, 0], sc1=m[:, 1], g1=m[:, 2], sh2=m[:, 3], sc2=m[:, 4], g2=m[:, 5],
        win=win_b, lcw=taps(lru_conv_w), lcb=vec(lru_conv_b),
        wa=jax.vmap(_block_diag)(gate_a_w), ba=vec(gate_a_b), wx=jax.vmap(_block_diag)(gate_x_w), bx=vec(gate_x_b),
        lam=vec(lru_lambda), lnw=vec(lru_norm_w),
        gcw=taps(gdn_conv_w), gcb=jnp.zeros((L, 1, 3 * GDN_W), F32),
        alog=lanes(gdn_a_log), dtb=lanes(gdn_dt_bias), gnw=vec(gdn_norm_w),
        wout=wout_b, wup=wup_b, wdown=wdown_b)


def _position():
    x, y, c = lax.axis_index("x"), lax.axis_index("y"), lax.axis_index("c")
    return x, y, c


def _other_chips(x, y):
    return [(1 - x, y), (x, 1 - y), (1 - x, 1 - y)]


def _all_gather_rows(block, name):
    m, n = block.shape

    def body(x_ref, out_ref, send_sems, recv_sems, local_sem):
        x, y, c = _position()
        me, sibling = (x, y, c), (x, y, 1 - c)
        chips = _other_chips(x, y)

        def rows(px, py, pc):
            return out_ref.at[pl.ds((4 * px + 2 * py + pc) * m, m), :]

        def copy(k, blk, to, src=None):
            return pltpu.make_async_remote_copy(
                src_ref=rows(*blk) if src is None else src, dst_ref=rows(*blk),
                send_sem=send_sems.at[k], recv_sem=recv_sems.at[k], device_id=to, device_id_type=MESH)

        mine = pltpu.make_async_copy(x_ref, rows(*me), local_sem)
        mine.start()
        first = [copy(0, me, sibling, src=x_ref)]
        first += [copy(1 + j, me, (*chip, c), src=x_ref) for j, chip in enumerate(chips)]
        for cp in first:
            cp.start()
        passed = [copy(4 + j, (*chip, c), sibling) for j, chip in enumerate(chips)]
        for j, chip in enumerate(chips):
            copy(1 + j, (*chip, c), me).wait_recv()
            passed[j].start()
        copy(0, sibling, me).wait_recv()
        for j, chip in enumerate(chips):
            copy(4 + j, (*chip, 1 - c), me).wait_recv()
        for cp in first + passed:
            cp.wait_send()
        mine.wait()

    return pl.pallas_call(
        body, name=name,
        out_shape=_sds((N_DEV * m, n)),
        in_specs=[pl.BlockSpec(memory_space=pltpu.VMEM)],
        out_specs=pl.BlockSpec(memory_space=pltpu.VMEM),
        scratch_shapes=[pltpu.SemaphoreType.DMA((7,)), pltpu.SemaphoreType.DMA((7,)), pltpu.SemaphoreType.DMA],
        compiler_params=pltpu.CompilerParams(vmem_limit_bytes=VMEM_LIMIT),
    )(block)


def _hbm_specs(n):
    return [pl.BlockSpec(memory_space=pl.ANY)] * n


def _gather_chips(shards, name):
    n = len(shards)

    def body(*refs):
        ins, outs = refs[:n], refs[n:2 * n]
        send_sems, recv_sems, local_sems = refs[2 * n:]
        x, y, c = _position()
        chips = _other_chips(x, y)
        me = 2 * x + y
        local = [pltpu.make_async_copy(ins[a], outs[a].at[me], local_sems.at[a]) for a in range(n)]
        for cp in local:
            cp.start()

        def copy(a, j, slot):
            px, py = chips[j]
            return pltpu.make_async_remote_copy(
                src_ref=ins[a], dst_ref=outs[a].at[slot], send_sem=send_sems.at[3 * a + j],
                recv_sem=recv_sems.at[3 * a + j], device_id=(px, py, c), device_id_type=MESH)

        sends = [copy(a, j, me) for a in range(n) for j in range(3)]
        for cp in sends:
            cp.start()
        for a in range(n):
            for j, (px, py) in enumerate(chips):
                copy(a, j, 2 * px + py).wait_recv()
        for cp in sends:
            cp.wait_send()
        for cp in local:
            cp.wait()

    return pl.pallas_call(
        body, name=name,
        out_shape=[_sds((N_CHIPS,) + s.shape, s.dtype) for s in shards],
        in_specs=_hbm_specs(n), out_specs=_hbm_specs(n),
        scratch_shapes=[pltpu.SemaphoreType.DMA((3 * n,)), pltpu.SemaphoreType.DMA((3 * n,)),
                        pltpu.SemaphoreType.DMA((n,))],
    )(*shards)


def _send_to_sibling(parts, name):
    n = len(parts)

    def body(*refs):
        ins, outs = refs[:n], refs[n:2 * n]
        send_sems, recv_sems = refs[2 * n:]
        x, y, c = _position()
        copies = [pltpu.make_async_remote_copy(
            src_ref=ins[a].at[1 - c], dst_ref=outs[a], send_sem=send_sems.at[a], recv_sem=recv_sems.at[a],
            device_id=(x, y, 1 - c), device_id_type=MESH) for a in range(n)]
        for cp in copies:
            cp.start()
        for cp in copies:
            cp.wait()

    return pl.pallas_call(
        body, name=name,
        out_shape=[_sds(p.shape[1:], p.dtype) for p in parts],
        in_specs=_hbm_specs(n), out_specs=_hbm_specs(n),
        scratch_shapes=[pltpu.SemaphoreType.DMA((n,)), pltpu.SemaphoreType.DMA((n,))],
    )(*parts)


def _scatter_chips(parts, name):
    n = len(parts)

    def body(*refs):
        ins, outs = refs[:n], refs[n:2 * n]
        send_sems, recv_sems, local_sems = refs[2 * n:]
        x, y, c = _position()
        chips = _other_chips(x, y)
        me = 2 * x + y
        local = [pltpu.make_async_copy(ins[a].at[me], outs[a].at[me], local_sems.at[a]) for a in range(n)]
        for cp in local:
            cp.start()

        def copy(a, j, src_slot, dst_slot):
            px, py = chips[j]
            return pltpu.make_async_remote_copy(
                src_ref=ins[a].at[src_slot], dst_ref=outs[a].at[dst_slot], send_sem=send_sems.at[3 * a + j],
                recv_sem=recv_sems.at[3 * a + j], device_id=(px, py, c), device_id_type=MESH)

        sends = [copy(a, j, 2 * chips[j][0] + chips[j][1], me) for a in range(n) for j in range(3)]
        for cp in sends:
            cp.start()
        for a in range(n):
            for j, (px, py) in enumerate(chips):
                copy(a, j, me, 2 * px + py).wait_recv()
        for cp in sends:
            cp.wait_send()
        for cp in local:
            cp.wait()

    return pl.pallas_call(
        body, name=name,
        out_shape=[_sds(p.shape, p.dtype) for p in parts],
        in_specs=_hbm_specs(n), out_specs=_hbm_specs(n),
        scratch_shapes=[pltpu.SemaphoreType.DMA((3 * n,)), pltpu.SemaphoreType.DMA((3 * n,)),
                        pltpu.SemaphoreType.DMA((n,))],
    )(*parts)


def _share_with_sibling(halves, name):
    n = len(halves)

    def body(*refs):
        ins, outs = refs[:n], refs[n:2 * n]
        send_sems, recv_sems, local_sems = refs[2 * n:]
        x, y, c = _position()
        local = [pltpu.make_async_copy(ins[a], outs[a].at[c], local_sems.at[a]) for a in range(n)]
        for cp in local:
            cp.start()
        sends = [pltpu.make_async_remote_copy(
            src_ref=ins[a], dst_ref=outs[a].at[c], send_sem=send_sems.at[a], recv_sem=recv_sems.at[a],
            device_id=(x, y, 1 - c), device_id_type=MESH) for a in range(n)]
        for cp in sends:
            cp.start()
        for a in range(n):
            pltpu.make_async_remote_copy(
                src_ref=ins[a], dst_ref=outs[a].at[1 - c], send_sem=send_sems.at[a], recv_sem=recv_sems.at[a],
                device_id=(x, y, 1 - c), device_id_type=MESH).wait_recv()
        for cp in sends:
            cp.wait_send()
        for cp in local:
            cp.wait()

    return pl.pallas_call(
        body, name=name,
        out_shape=[_sds((2,) + h.shape, h.dtype) for h in halves],
        in_specs=_hbm_specs(n), out_specs=_hbm_specs(n),
        scratch_shapes=[pltpu.SemaphoreType.DMA((n,)), pltpu.SemaphoreType.DMA((n,)), pltpu.SemaphoreType.DMA((n,))],
    )(*halves)


def _row_tile(rows):
    for t in (512, 256, 128, 64, 32, 16, 8):
        if rows % t == 0:
            return t
    return rows


def _sum_slots(buf, name):
    k, rows, cols = buf.shape
    tm = _row_tile(rows)

    def body(b_ref, o_ref):
        s = b_ref[0]
        for i in range(1, k):
            s = s + b_ref[i]
        o_ref[...] = s

    return pl.pallas_call(
        body, name=name, grid=(rows // tm,),
        in_specs=[pl.BlockSpec((k, tm, cols), lambda i: (0, i, 0))],
        out_specs=pl.BlockSpec((tm, cols), lambda i: (i, 0)),
        out_shape=_sds((rows, cols)),
        compiler_params=_params(("arbitrary",)),
    )(buf)


def _add2(a, b, name):
    rows, cols = a.shape
    tm = _row_tile(rows)

    def body(a_ref, b_ref, o_ref):
        o_ref[...] = a_ref[...] + b_ref[...]

    spec = pl.BlockSpec((tm, cols), lambda i: (i, 0))
    return pl.pallas_call(
        body, name=name, grid=(rows // tm,), in_specs=[spec, spec], out_specs=spec, out_shape=_sds((rows, cols)),
        compiler_params=_params(("arbitrary",)),
    )(a, b)


def _adam_math(w, g, m, v):
    m = ADAM_B1 * m + (1.0 - ADAM_B1) * g
    v = ADAM_B2 * v + (1.0 - ADAM_B2) * jnp.square(g)
    m_hat = m / (1.0 - ADAM_B1 ** ADAM_STEP)
    v_hat = v / (1.0 - ADAM_B2 ** ADAM_STEP)
    delta = -ADAM_LR * (m_hat / (jnp.sqrt(v_hat) + ADAM_EPS) + ADAM_WD * w)
    return delta, m, v


def _adam(w, g, m, v, name):
    rows, cols = w.shape
    tm = _row_tile(rows)

    def body(w_ref, g_ref, m_ref, v_ref, d_ref, nm_ref, nv_ref):
        d, nm, nv = _adam_math(w_ref[...], g_ref[...], m_ref[...], v_ref[...])
        d_ref[...] = d
        nm_ref[...] = nm
        nv_ref[...] = nv

    spec = pl.BlockSpec((tm, cols), lambda i: (i, 0))
    return pl.pallas_call(
        body, name=name, grid=(rows // tm,), in_specs=[spec] * 4, out_specs=[spec] * 3,
        out_shape=[_sds((rows, cols))] * 3, compiler_params=_params(("arbitrary",)),
    )(w, g, m, v)


def _mod_fwd(c_all, w_mod, b_mod_cols):
    L, _, n = w_mod.shape

    def body(c_ref, w_ref, b_ref, o_ref):
        o_ref[0] = _hdot(_silu(c_ref[...]), w_ref[0]) + b_ref[0]

    return pl.pallas_call(
        body, name="mod_fwd", grid=(L,),
        in_specs=[_const((N_DEV, D_MODEL)), pl.BlockSpec((1, D_MODEL, n), lambda l: (l, 0, 0)),
                  pl.BlockSpec((1, 1, n), lambda l: (l, 0, 0))],
        out_specs=pl.BlockSpec((1, N_DEV, n), lambda l: (l, 0, 0)),
        out_shape=_sds((L, N_DEV, n)),
        compiler_params=_params(("arbitrary",)),
    )(c_all, w_mod, b_mod_cols)


def _mod_update(c_all, dmod, w, m, v):
    L, _, n = w.shape
    tn = 512

    def body(c_ref, d_ref, w_ref, m_ref, v_ref, g_ref, dl_ref, nm_ref, nv_ref):
        g = _hdot_tn(_silu(c_ref[...]), d_ref[0])
        g_ref[0] = g
        d, nm, nv = _adam_math(w_ref[0], g, m_ref[0], v_ref[0])
        dl_ref[0] = d
        nm_ref[0] = nm
        nv_ref[0] = nv

    big = pl.BlockSpec((1, D_MODEL, tn), lambda l, j: (l, 0, j))
    return pl.pallas_call(
        body, name="mod_update", grid=(L, n // tn),
        in_specs=[_const((N_DEV, D_MODEL)), pl.BlockSpec((1, N_DEV, tn), lambda l, j: (l, 0, j)), big, big, big],
        out_specs=[big] * 4, out_shape=[_sds(w.shape)] * 4,
        compiler_params=_params(("arbitrary", "arbitrary")),
    )(c_all, dmod, w, m, v)


def _pack_rows(parts, row_multiple):
    flat = jnp.concatenate([p.reshape(-1) for p in parts])
    unit = row_multiple * LANES
    flat = jnp.pad(flat, (0, (-flat.shape[0]) % unit))
    return flat.reshape(-1, LANES)


def _unpack(packed, shapes):
    flat = packed.reshape(-1)
    out, off = [], 0
    for s in shapes:
        n = 1
        for d in s:
            n *= d
        out.append(flat[off:off + n].reshape(s))
        off += n
    return out


def _lane_pad(a):
    return jnp.pad(a, ((0, 0), (0, LANES - a.shape[1])))


WEIGHT_NAMES = ("norm_mix_w", "norm_mlp_w", "w_mod", "b_mod", "w_in", "lru_conv_w", "lru_conv_b", "lru_gate_a_w",
                "lru_gate_a_b", "lru_gate_x_w", "lru_gate_x_b", "lru_lambda", "lru_norm_w", "gdn_conv_w", "gdn_a_log",
                "gdn_dt_bias", "gdn_norm_w", "w_out", "w_up", "w_down", "final_norm_w")


def kernel(x, c, norm_mix_w, norm_mlp_w, w_mod, b_mod, w_in, lru_conv_w, lru_conv_b, lru_gate_a_w, lru_gate_a_b, lru_gate_x_w, lru_gate_x_b, lru_lambda, lru_norm_w, gdn_conv_w, gdn_a_log, gdn_dt_bias, gdn_norm_w, w_out, w_up, w_down, final_norm_w, loss_target, m_norm_mix_w, m_norm_mlp_w, m_w_mod, m_b_mod, m_w_in, m_lru_conv_w, m_lru_conv_b, m_lru_gate_a_w, m_lru_gate_a_b, m_lru_gate_x_w, m_lru_gate_x_b, m_lru_lambda, m_lru_norm_w, m_gdn_conv_w, m_gdn_a_log, m_gdn_dt_bias, m_gdn_norm_w, m_w_out, m_w_up, m_w_down, m_final_norm_w, v_norm_mix_w, v_norm_mlp_w, v_w_mod, v_b_mod, v_w_in, v_lru_conv_w, v_lru_conv_b, v_lru_gate_a_w, v_lru_gate_a_b, v_lru_gate_x_w, v_lru_gate_x_b, v_lru_lambda, v_lru_norm_w, v_gdn_conv_w, v_gdn_a_log, v_gdn_dt_bias, v_gdn_norm_w, v_w_out, v_w_up, v_w_down, v_final_norm_w):
    W = dict(zip(WEIGHT_NAMES, (norm_mix_w, norm_mlp_w, w_mod, b_mod, w_in, lru_conv_w, lru_conv_b, lru_gate_a_w,
                                lru_gate_a_b, lru_gate_x_w, lru_gate_x_b, lru_lambda, lru_norm_w, gdn_conv_w, gdn_a_log,
                                gdn_dt_bias, gdn_norm_w, w_out, w_up, w_down, final_norm_w)))
    M = dict(zip(WEIGHT_NAMES, (m_norm_mix_w, m_norm_mlp_w, m_w_mod, m_b_mod, m_w_in, m_lru_conv_w, m_lru_conv_b,
                                m_lru_gate_a_w, m_lru_gate_a_b, m_lru_gate_x_w, m_lru_gate_x_b, m_lru_lambda,
                                m_lru_norm_w, m_gdn_conv_w, m_gdn_a_log, m_gdn_dt_bias, m_gdn_norm_w, m_w_out, m_w_up,
                                m_w_down, m_final_norm_w)))
    V = dict(zip(WEIGHT_NAMES, (v_norm_mix_w, v_norm_mlp_w, v_w_mod, v_b_mod, v_w_in, v_lru_conv_w, v_lru_conv_b,
                                v_lru_gate_a_w, v_lru_gate_a_b, v_lru_gate_x_w, v_lru_gate_x_b, v_lru_lambda,
                                v_lru_norm_w, v_gdn_conv_w, v_gdn_a_log, v_gdn_dt_bias, v_gdn_norm_w, v_w_out, v_w_up,
                                v_w_down, v_final_norm_w)))
    L = DEPTH
    xi, yi, ci = _position()
    chip = 2 * xi + yi
    dev = 2 * chip + ci
    lcs = LRU_W // N_CHIPS
    gcs = 3 * GDN_W // N_CHIPS
    mcs = N_MOD * D_MODEL // N_CHIPS

    g_in = _all_gather_rows(_pack_rows([c, lru_conv_w, gdn_conv_w], SUBLANES), "gather_small_inputs").reshape(N_DEV, -1)
    c_all = g_in[:, :D_MODEL]
    per_chip = g_in[0::2]
    o1 = D_MODEL + L * 4 * lcs
    lcw_full = per_chip[:, D_MODEL:o1].reshape(N_CHIPS, L, 4, lcs).transpose(1, 2, 0, 3).reshape(L, 4, LRU_W)
    gcw_full = per_chip[:, o1:o1 + L * 4 * gcs].reshape(N_CHIPS, L, 4, gcs).transpose(1, 2, 0, 3).reshape(L, 4, 3 * GDN_W)

    b_cols = lax.dynamic_slice(b_mod, (0, chip * mcs), (L, mcs)).reshape(L, 1, mcs)
    modp = _mod_fwd(c_all, w_mod, b_cols)
    g_mod = _all_gather_rows(modp.reshape(L * N_DEV, mcs), "gather_mod").reshape(N_DEV, L, N_DEV, mcs)
    mod = lax.dynamic_index_in_dim(g_mod[0::2], dev, axis=2, keepdims=False).transpose(1, 0, 2).reshape(L, N_MOD * D_MODEL)

    win_g, wout_g, wup_g, wdown_g = _gather_chips(
        [w_in.astype(BF16), w_out.astype(BF16), w_up.astype(BF16), w_down.astype(BF16)], "gather_weights")
    win_b = jnp.pad(win_g.transpose(1, 2, 0, 3).reshape(L, D_MODEL, IN_COLS), ((0, 0), (0, 0), (0, IN_PAD - IN_COLS)))
    wout_b = wout_g.transpose(1, 0, 2, 3).reshape(L, D_MODEL, D_MODEL)
    wup_b = wup_g.transpose(1, 2, 0, 3).reshape(L, D_MODEL, D_FF)
    wdown_b = wdown_g.transpose(1, 0, 2, 3).reshape(L, D_FF, D_MODEL)

    layers = _prep_layers(norm_mix_w, norm_mlp_w, mod, win_b, lcw_full, lru_conv_b, lru_gate_a_w, lru_gate_a_b,
                          lru_gate_x_w, lru_gate_x_b, lru_lambda, lru_norm_w, gcw_full, gdn_a_log, gdn_dt_bias,
                          gdn_norm_w, wout_b, wup_b, wdown_b)
    loss_local, dx, dfnw, g = _local_step(x[0], loss_target[0], final_norm_w.reshape(1, D_MODEL), layers)
    loss = lax.psum(loss_local, ("x", "y", "c"))

    dmod = jnp.concatenate([g["sh1"], g["sc1"], g["g1"], g["sh2"], g["sc2"], g["g2"]], axis=-1)
    small = [dmod, g["nmw"], g["nmlp"], g["lcw"][:, :4], g["lcb"], jax.vmap(_diag_blocks)(g["wa"]), g["ba"],
             jax.vmap(_diag_blocks)(g["wx"]), g["bx"], g["lam"], g["lnw"], g["gcw"][:, :4], g["alog"], g["dtb"],
             g["gnw"], dfnw]
    small_shapes = [(L, N_MOD * D_MODEL), (L, D_MODEL), (L, D_MODEL), (L, 4, LRU_W), (L, LRU_W),
                    (L, LRU_BLOCKS, LRU_BLOCK, LRU_BLOCK), (L, LRU_W), (L, LRU_BLOCKS, LRU_BLOCK, LRU_BLOCK),
                    (L, LRU_W), (L, LRU_W), (L, LRU_W), (L, 4, 3 * GDN_W), (L, LANES), (L, LANES), (L, LANES),
                    (D_MODEL,)]
    small_names = ["b_mod", "norm_mix_w", "norm_mlp_w", None, "lru_conv_b", "lru_gate_a_w", "lru_gate_a_b",
                   "lru_gate_x_w", "lru_gate_x_b", "lru_lambda", "lru_norm_w", None, "gdn_a_log", "gdn_dt_bias",
                   "gdn_norm_w", "final_norm_w"]
    pack_g = _pack_rows(small, 512)
    rows = pack_g.shape[0]
    all_g = _all_gather_rows(pack_g, "gather_small_grads").reshape(N_DEV, rows, LANES)
    tot = _sum_slots(all_g, "sum_small_grads")
    tot_parts = _unpack(tot, small_shapes)

    def pack_state(S_):
        parts = []
        for nm, shp in zip(small_names, small_shapes):
            if nm is None:
                parts.append(jnp.zeros(shp, F32))
            elif nm in ("gdn_a_log", "gdn_dt_bias"):
                parts.append(_lane_pad(S_[nm]))
            else:
                parts.append(S_[nm])
        return _pack_rows(parts, 512)

    upd = _adam(pack_state(W), tot, pack_state(M), pack_state(V), "adam_small")
    upd_parts = [_unpack(u, small_shapes) for u in upd]

    grads, deltas, new_m, new_v = {}, {}, {}, {}
    for k, nm in enumerate(small_names):
        if nm is None:
            continue
        cut = (lambda a: a[:, :HEADS]) if nm in ("gdn_a_log", "gdn_dt_bias") else (lambda a: a)
        grads[nm] = cut(tot_parts[k])
        deltas[nm], new_m[nm], new_v[nm] = (cut(u[k]) for u in upd_parts)

    g_lcw = lax.dynamic_slice(tot_parts[3], (0, 0, chip * lcs), (L, 4, lcs))
    g_gcw = lax.dynamic_slice(tot_parts[11], (0, 0, chip * gcs), (L, 4, gcs))
    conv_shapes = [(L, 4, lcs), (L, 4, gcs)]
    conv_pack = lambda a, b: _pack_rows([a, b], SUBLANES)
    cu = _adam(conv_pack(lru_conv_w, gdn_conv_w), conv_pack(g_lcw, g_gcw), conv_pack(m_lru_conv_w, m_gdn_conv_w),
               conv_pack(v_lru_conv_w, v_gdn_conv_w), "adam_conv")
    cu_parts = [_unpack(u, conv_shapes) for u in cu]
    for k, nm in enumerate(("lru_conv_w", "gdn_conv_w")):
        grads[nm] = (g_lcw, g_gcw)[k]
        deltas[nm], new_m[nm], new_v[nm] = (u[k] for u in cu_parts)

    dmod_all = all_g[:, :L * N_MOD * D_MODEL // LANES].reshape(N_DEV, L, N_MOD * D_MODEL)
    dmod_cols = lax.dynamic_slice(dmod_all, (0, 0, chip * mcs), (N_DEV, L, mcs)).transpose(1, 0, 2)
    grads["w_mod"], deltas["w_mod"], new_m["w_mod"], new_v["w_mod"] = _mod_update(c_all, dmod_cols, w_mod, m_w_mod, v_w_mod)

    half = L // 2
    gwin = g["win"][:, :, :IN_COLS].reshape(2, half, D_MODEL, N_CHIPS, IN_COLS // N_CHIPS).transpose(0, 3, 1, 2, 4)
    gwout = g["wout"].reshape(2, half, N_CHIPS, D_MODEL // N_CHIPS, D_MODEL).transpose(0, 2, 1, 3, 4)
    gwup = g["wup"].reshape(2, half, D_MODEL, N_CHIPS, D_FF // N_CHIPS).transpose(0, 3, 1, 2, 4)
    gwdown = g["wdown"].reshape(2, half, N_CHIPS, D_FF // N_CHIPS, D_MODEL).transpose(0, 2, 1, 3, 4)
    parts = [gwin, gwout, gwup, gwdown]
    big_names = ["w_in", "w_out", "w_up", "w_down"]
    from_sibling = _send_to_sibling(parts, "pair_send")
    pair = []
    for nm, p, r in zip(big_names, parts, from_sibling):
        own = lax.dynamic_index_in_dim(p, ci, axis=0, keepdims=False)
        cols = own.shape[-1]
        pair.append(_add2(own.reshape(-1, cols), r.reshape(-1, cols), "pair_add_" + nm).reshape(own.shape))
    arrived = _scatter_chips(pair, "chip_scatter")
    halves = []
    for nm, a in zip(big_names, arrived):
        cols = a.shape[-1]
        halves.append(_sum_slots(a.reshape(N_CHIPS, -1, cols), "chip_sum_" + nm).reshape(a.shape[1:]))
    full = _share_with_sibling(halves, "pair_share")
    for nm, f in zip(big_names, full):
        shard = W[nm].shape
        cols = shard[-1]
        gr = f.reshape(-1, cols)
        d_, m_, v_ = _adam(W[nm].reshape(-1, cols), gr, M[nm].reshape(-1, cols), V[nm].reshape(-1, cols), "adam_" + nm)
        grads[nm] = gr.reshape(shard)
        deltas[nm], new_m[nm], new_v[nm] = d_.reshape(shard), m_.reshape(shard), v_.reshape(shard)

    out = [loss, dx[None]]
    for group in (grads, deltas, new_m, new_v):
        out += [group[nm].reshape(W[nm].shape) for nm in WEIGHT_NAMES]
    return tuple(out)
```

```python
import functools

import jax
import jax.numpy as jnp
from jax import lax
from jax.experimental import pallas as pl
from jax.experimental.pallas import tpu as pltpu

F32 = jnp.float32
BF16 = jnp.bfloat16
MESH = pl.DeviceIdType.MESH

D_MODEL = 1024
DEPTH = 4
LRU_W = 512
LRU_BLOCKS = 8
LRU_BLOCK = 64
LRU_C = 8.0
HEADS = 4
HEAD_DIM = 128
GDN_W = 512
CHUNK = 128
D_FF = 4096
N_MOD = 6
IN_COLS = 3080
IN_PAD = 3200
NORM_EPS = 1e-6
LANES = 128
SUBLANES = 8
N_DEV = 8
N_CHIPS = 4

ADAM_LR = 0.001
ADAM_B1 = 0.9
ADAM_B2 = 0.999
ADAM_EPS = 1e-08
ADAM_WD = 0.01
ADAM_STEP = 10

VMEM_LIMIT = 56 * 1024 * 1024
HI = lax.Precision.HIGHEST


def _sds(shape, dtype=F32):
    return jax.ShapeDtypeStruct(tuple(shape), dtype)


def _params(sem=None, vmem=VMEM_LIMIT):
    return pltpu.CompilerParams(dimension_semantics=sem, vmem_limit_bytes=vmem)


def _const(shape):
    return pl.BlockSpec(tuple(shape), lambda *_: (0,) * len(shape))


def _row(tm, c, col=0):
    return pl.BlockSpec((tm, c), lambda i: (i, col))


def _dot(a, b):
    return jnp.dot(a, b, preferred_element_type=F32)


def _dot_nt(a, b):
    return lax.dot_general(a, b, (((1,), (1,)), ((), ())), preferred_element_type=F32)


def _dot_tn(a, b):
    return lax.dot_general(a, b, (((0,), (0,)), ((), ())), preferred_element_type=F32)


def _hdot(a, b):
    return jnp.dot(a, b, preferred_element_type=F32, precision=HI)


def _hdot_nt(a, b):
    return lax.dot_general(a, b, (((1,), (1,)), ((), ())), preferred_element_type=F32, precision=HI)


def _hdot_tn(a, b):
    return lax.dot_general(a, b, (((0,), (0,)), ((), ())), preferred_element_type=F32, precision=HI)


def _acc(ref, val, first):
    @pl.when(first)
    def _():
        ref[...] = val

    @pl.when(jnp.logical_not(first))
    def _():
        ref[...] += val


def _colsum(v):
    return jnp.sum(v, axis=0, keepdims=True)


def _rms_parts(x):
    r = lax.rsqrt(jnp.mean(x * x, axis=-1, keepdims=True) + NORM_EPS)
    return x * r, r


def _rms_bwd(dy, xh, r, w):
    dxh = dy * w
    dw = _colsum(dy * xh)
    dx = r * (dxh - xh * jnp.mean(dxh * xh, axis=-1, keepdims=True))
    return dx, dw


def _norm_mod(x, w, sc, sh):
    xh, _ = _rms_parts(x)
    return (xh * w) * (1.0 + sc) + sh


def _norm_mod_bwd(dy, x, w, sc):
    xh, r = _rms_parts(x)
    n = xh * w
    dsh = _colsum(dy)
    dsc = _colsum(dy * n)
    dx, dw = _rms_bwd(dy * (1.0 + sc), xh, r, w)
    return dx, dw, dsc, dsh


def _softplus(x):
    return jnp.maximum(x, 0.0) + jnp.log1p(jnp.exp(-jnp.abs(x)))


def _silu(x):
    return x * jax.nn.sigmoid(x)


def _silu_grad(x):
    s = jax.nn.sigmoid(x)
    return s * (1.0 + x * (1.0 - s))


def _roll_dn(x, d):
    return x if d == 0 else pltpu.roll(x, d, 0)


def _roll_up(x, d):
    return x if d == 0 else pltpu.roll(x, x.shape[0] - d, 0)


def _proj_fwd(x, nw, sc, sh, win):
    S = x.shape[0]
    tm = min(512, S)

    def body(x_ref, nw_ref, sc_ref, sh_ref, w_ref, proj_ref, hb_ref):
        hb = _norm_mod(x_ref[...], nw_ref[...], sc_ref[...], sh_ref[...]).astype(BF16)
        hb_ref[...] = hb
        proj_ref[...] = _dot(hb, w_ref[...])

    vec = _const((1, D_MODEL))
    return pl.pallas_call(
        body, name="proj_fwd", grid=(S // tm,),
        in_specs=[_row(tm, D_MODEL), vec, vec, vec, _const((D_MODEL, IN_PAD))],
        out_specs=[_row(tm, IN_PAD), _row(tm, D_MODEL)],
        out_shape=[_sds((S, IN_PAD)), _sds((S, D_MODEL), BF16)],
        compiler_params=_params(("arbitrary",)),
    )(x, nw, sc, sh, win)


def _proj_bwd(dx1, x, dlx, dly, dqkv, dz, dba, nw, sc, win):
    S = x.shape[0]
    tm = min(512, S)

    def body(dx1_ref, x_ref, dlx_ref, dly_ref, dqkv_ref, dz_ref, dba_ref, nw_ref, sc_ref, w_ref,
             dx_ref, dpb_ref, dnw_ref, dsc_ref, dsh_ref):
        i = pl.program_id(0)
        dpb = jnp.concatenate([dlx_ref[...], dly_ref[...], dqkv_ref[...], dz_ref[...], dba_ref[...]],
                              axis=-1).astype(BF16)
        dpb_ref[...] = dpb
        dh = _dot_nt(dpb, w_ref[...])
        dx, dnw, dsc, dsh = _norm_mod_bwd(dh, x_ref[...], nw_ref[...], sc_ref[...])
        dx_ref[...] = dx1_ref[...] + dx
        _acc(dnw_ref, dnw, i == 0)
        _acc(dsc_ref, dsc, i == 0)
        _acc(dsh_ref, dsh, i == 0)

    vec = _const((1, D_MODEL))
    return pl.pallas_call(
        body, name="proj_bwd", grid=(S // tm,),
        in_specs=[_row(tm, D_MODEL), _row(tm, D_MODEL), _row(tm, LRU_W), _row(tm, LRU_W), _row(tm, 3 * GDN_W),
                  _row(tm, GDN_W), _row(tm, LANES), vec, vec,
                  _const((D_MODEL, IN_PAD))],
        out_specs=[_row(tm, D_MODEL), _row(tm, IN_PAD), vec, vec, vec],
        out_shape=[_sds((S, D_MODEL)), _sds((S, IN_PAD), BF16), _sds((1, D_MODEL)), _sds((1, D_MODEL)),
                   _sds((1, D_MODEL))],
        compiler_params=_params(("arbitrary",)),
    )(dx1, x, dlx, dly, dqkv, dz, dba, nw, sc, win)


def _conv_taps(xx, w, tm):
    y = _roll_dn(xx, 3)[SUBLANES:] * w[0:1]
    y = y + _roll_dn(xx, 2)[SUBLANES:] * w[1:2]
    y = y + _roll_dn(xx, 1)[SUBLANES:] * w[2:3]
    y = y + xx[SUBLANES:] * w[3:4]
    return y


def _conv_fwd(src, col0, C, w8, b, act, name):
    S = src.shape[0]
    tm = min(512, S)
    tc = 512
    hb = tm // SUBLANES
    cb0 = col0 // tc

    def body(x_ref, p_ref, w_ref, b_ref, y_ref):
        i = pl.program_id(0)
        prev = jnp.where(i > 0, p_ref[...], 0.0)
        xx = jnp.concatenate([prev, x_ref[...]], axis=0)
        y = _conv_taps(xx, w_ref[...], tm) + b_ref[...]
        y_ref[...] = _silu(y) if act else y

    return pl.pallas_call(
        body, name=name, grid=(S // tm, C // tc),
        in_specs=[pl.BlockSpec((tm, tc), lambda i, j: (i, cb0 + j)),
                  pl.BlockSpec((SUBLANES, tc), lambda i, j: (jnp.maximum(i * hb - 1, 0), cb0 + j)),
                  pl.BlockSpec((SUBLANES, tc), lambda i, j: (0, j)),
                  pl.BlockSpec((1, tc), lambda i, j: (0, j))],
        out_specs=pl.BlockSpec((tm, tc), lambda i, j: (i, j)),
        out_shape=_sds((S, C)),
        compiler_params=_params(("arbitrary", "arbitrary")),
    )(src, src, w8, b)


def _conv_bwd(src, col0, C, w8, b, dyact, act, name):
    S = src.shape[0]
    tm = min(512, S)
    tc = 512
    hb = tm // SUBLANES
    nt = S // tm
    cb0 = col0 // tc
    last_hb = S // SUBLANES - 1

    def body(x_ref, p_ref, n_ref, dy_ref, dyn_ref, w_ref, b_ref, dx_ref, dw_ref, db_ref):
        i = pl.program_id(1)
        w = w_ref[...]
        prev = jnp.where(i > 0, p_ref[...], 0.0)
        xx = jnp.concatenate([prev, x_ref[...], n_ref[...]], axis=0)
        dy = jnp.concatenate([dy_ref[...], jnp.where(i < nt - 1, dyn_ref[...], 0.0)], axis=0)
        if act:
            ypre = _conv_taps(xx, w, tm + SUBLANES) + b_ref[...]
            dy = dy * _silu_grad(ypre)
        dx = dy[:tm] * w[3:4]
        for d in (1, 2, 3):
            dx = dx + _roll_up(dy, d)[:tm] * w[3 - d:4 - d]
        dx_ref[...] = dx
        xt = xx[:tm + SUBLANES]
        dyt = dy[:tm]
        rows = [_colsum(dyt * _roll_dn(xt, 3 - k)[SUBLANES:]) for k in range(4)]
        dw = jnp.concatenate(rows + [jnp.zeros((SUBLANES - 4, tc), F32)], axis=0)
        _acc(dw_ref, dw, i == 0)
        _acc(db_ref, _colsum(dyt), i == 0)

    return pl.pallas_call(
        body, name=name, grid=(C // tc, nt),
        in_specs=[pl.BlockSpec((tm, tc), lambda j, i: (i, cb0 + j)),
                  pl.BlockSpec((SUBLANES, tc), lambda j, i: (jnp.maximum(i * hb - 1, 0), cb0 + j)),
                  pl.BlockSpec((SUBLANES, tc), lambda j, i: (jnp.minimum((i + 1) * hb, last_hb), cb0 + j)),
                  pl.BlockSpec((tm, tc), lambda j, i: (i, j)),
                  pl.BlockSpec((SUBLANES, tc), lambda j, i: (jnp.minimum((i + 1) * hb, last_hb), j)),
                  pl.BlockSpec((SUBLANES, tc), lambda j, i: (0, j)),
                  pl.BlockSpec((1, tc), lambda j, i: (0, j))],
        out_specs=[pl.BlockSpec((tm, tc), lambda j, i: (i, j)),
                   pl.BlockSpec((SUBLANES, tc), lambda j, i: (0, j)),
                   pl.BlockSpec((1, tc), lambda j, i: (0, j))],
        out_shape=[_sds((S, C)), _sds((SUBLANES, C)), _sds((1, C))],
        compiler_params=_params(("arbitrary", "arbitrary")),
    )(src, src, src, dyact, dyact, w8, b)


def _lru_ab(pre_a, pre_x, xr, lam):
    r = jax.nn.sigmoid(pre_a)
    g = jax.nn.sigmoid(pre_x)
    log_sig = -_softplus(-lam)
    log_a = LRU_C * r * log_sig
    a = jnp.exp(log_a)
    t = jnp.tanh(log_a)
    mult = jnp.sqrt(jnp.maximum(-2.0 * t / (1.0 - t), 1e-12))
    return a, mult * (g * xr)


def _lru_tail(h, ly, lnw):
    xh, _ = _rms_parts(h * jax.nn.gelu(ly))
    return xh * lnw


def _scan_down(a, b):
    n = a.shape[0]
    row = lax.broadcasted_iota(jnp.int32, a.shape, 0)
    d = 1
    while d < n:
        keep = row >= d
        a_s = jnp.where(keep, _roll_dn(a, d), 1.0)
        b_s = jnp.where(keep, _roll_dn(b, d), 0.0)
        b = a * b_s + b
        a = a * a_s
        d *= 2
    return a, b


def _scan_up(a, b):
    n = a.shape[0]
    row = lax.broadcasted_iota(jnp.int32, a.shape, 0)
    d = 1
    while d < n:
        keep = row < n - d
        a_s = jnp.where(keep, _roll_up(a, d), 1.0)
        b_s = jnp.where(keep, _roll_up(b, d), 0.0)
        b = a * b_s + b
        a = a * a_s
        d *= 2
    return a, b


LRU_TM = 256


def _lru_fwd(xr, proj, wa, ba, wx, bx, lam, lnw):
    S = xr.shape[0]
    tm = min(LRU_TM, S)

    def body(xr_ref, ly_ref, wa_ref, ba_ref, wx_ref, bx_ref, lam_ref, lnw_ref, out_ref, h_ref, carry):
        i = pl.program_id(0)

        @pl.when(i == 0)
        def _():
            carry[...] = jnp.zeros_like(carry)

        x = xr_ref[...]
        xb = x.astype(BF16)
        pre_a = _dot(xb, wa_ref[...]) + ba_ref[...]
        pre_x = _dot(xb, wx_ref[...]) + bx_ref[...]
        a, b = _lru_ab(pre_a, pre_x, x, lam_ref[...])
        ca, hl = _scan_down(a, b)
        h = hl + ca * carry[0:1, :]
        carry[0:1, :] = h[tm - 1:tm, :]
        h_ref[...] = h
        out_ref[...] = _lru_tail(h, ly_ref[...], lnw_ref[...])

    vec = _const((1, LRU_W))
    mat = _const((LRU_W, LRU_W))
    return pl.pallas_call(
        body, name="lru_fwd", grid=(S // tm,),
        in_specs=[_row(tm, LRU_W), _row(tm, LRU_W, 1), mat, vec, mat, vec, vec, vec],
        out_specs=[_row(tm, LRU_W), _row(tm, LRU_W)],
        out_shape=[_sds((S, LRU_W)), _sds((S, LRU_W))],
        scratch_shapes=[pltpu.VMEM((SUBLANES, LRU_W), F32)],
        compiler_params=_params(("arbitrary",)),
    )(xr, proj, wa, ba, wx, bx, lam, lnw)


def _lru_bwd(dout, xr, proj, h, wa, ba, wx, bx, lam, lnw):
    S = xr.shape[0]
    tm = min(LRU_TM, S)
    nt = S // tm
    hb = tm // SUBLANES

    def rev(col=0):
        return pl.BlockSpec((tm, LRU_W), lambda i: (nt - 1 - i, col))

    def body(dout_ref, xr_ref, ly_ref, h_ref, hp_ref, wa_ref, ba_ref, wx_ref, bx_ref, lam_ref, lnw_ref,
             dxr_ref, dly_ref, dwa_ref, dba_ref, dwx_ref, dbx_ref, dlam_ref, dlnw_ref, carry):
        i = pl.program_id(0)
        first = i == 0

        @pl.when(first)
        def _():
            carry[...] = jnp.zeros_like(carry)

        x = xr_ref[...]
        xb = x.astype(BF16)
        pre_a = _dot(xb, wa_ref[...]) + ba_ref[...]
        pre_x = _dot(xb, wx_ref[...]) + bx_ref[...]
        (a, b), ab_vjp = jax.vjp(_lru_ab, pre_a, pre_x, x, lam_ref[...])
        h_t = h_ref[...]
        _, tail_vjp = jax.vjp(_lru_tail, h_t, ly_ref[...], lnw_ref[...])
        dh, dly, dlnw = tail_vjp(dout_ref[...])
        dly_ref[...] = dly
        row = lax.broadcasted_iota(jnp.int32, a.shape, 0)
        a_next = jnp.where(row == tm - 1, carry[0:1, :], _roll_up(a, 1))
        ca, gl = _scan_up(a_next, dh)
        g = gl + ca * carry[1:2, :]
        carry[0:1, :] = a[0:1, :]
        carry[1:2, :] = g[0:1, :]
        h_before = jnp.where(i == nt - 1, 0.0, hp_ref[SUBLANES - 1:SUBLANES, :])
        h_prev = jnp.where(row == 0, h_before, _roll_dn(h_t, 1))
        dpa, dpx, dx, dlam = ab_vjp((g * h_prev, g))
        dpab = dpa.astype(BF16)
        dpxb = dpx.astype(BF16)
        dxr_ref[...] = dx + _dot_nt(dpab, wa_ref[...]) + _dot_nt(dpxb, wx_ref[...])
        _acc(dwa_ref, _dot_tn(xb, dpab), first)
        _acc(dwx_ref, _dot_tn(xb, dpxb), first)
        _acc(dba_ref, _colsum(dpa), first)
        _acc(dbx_ref, _colsum(dpx), first)
        _acc(dlam_ref, dlam, first)
        _acc(dlnw_ref, dlnw, first)

    vec = _const((1, LRU_W))
    mat = _const((LRU_W, LRU_W))
    return pl.pallas_call(
        body, name="lru_bwd", grid=(nt,),
        in_specs=[rev(), rev(), rev(1), rev(),
                  pl.BlockSpec((SUBLANES, LRU_W), lambda i: (jnp.maximum((nt - 1 - i) * hb - 1, 0), 0)),
                  mat, vec, mat, vec, vec, vec],
        out_specs=[rev(), rev(), mat, vec, mat, vec, vec, vec],
        out_shape=[_sds((S, LRU_W)), _sds((S, LRU_W)), _sds((LRU_W, LRU_W)), _sds((1, LRU_W)),
                   _sds((LRU_W, LRU_W)), _sds((1, LRU_W)), _sds((1, LRU_W)), _sds((1, LRU_W))],
        scratch_shapes=[pltpu.VMEM((SUBLANES, LRU_W), F32)],
        compiler_params=_params(("arbitrary",)),
    )(dout, xr, proj, h, h, wa, ba, wx, bx, lam, lnw)


def _lane_pick(row_or_tile, lane):
    idx = lax.broadcasted_iota(jnp.int32, row_or_tile.shape, 1)
    return jnp.sum(jnp.where(idx == lane, row_or_tile, 0.0), axis=-1, keepdims=True)


def _unit_lower_inverses(los):
    n = los[0].shape[0]
    ri = lax.broadcasted_iota(jnp.int32, (n, n), 0)
    ci = lax.broadcasted_iota(jnp.int32, (n, n), 1)
    invs = [(ri == ci).astype(F32) for _ in los]
    s = 1
    while s < n:
        same_block = (ri & ~(2 * s - 1)) == (ci & ~(2 * s - 1))
        lower_left = same_block & ((ri & s) != 0) & ((ci & s) == 0)
        left = [_hdot(inv, jnp.where(lower_left, lo, 0.0)) for inv, lo in zip(invs, los)]
        invs = [inv - _hdot(t, inv) for inv, t in zip(invs, left)]
        s *= 2
    return invs


@jax.custom_vjp
def _unit_lower_inverses_diff(los):
    return _unit_lower_inverses(los)


def _unit_lower_inverses_fwd(los):
    invs = _unit_lower_inverses(los)
    return invs, invs


def _unit_lower_inverses_bwd(invs, cts):
    right = [_hdot_nt(ct, inv) for ct, inv in zip(cts, invs)]
    return ([-_hdot_tn(inv, r) for inv, r in zip(invs, right)],)


_unit_lower_inverses_diff.defvjp(_unit_lower_inverses_fwd, _unit_lower_inverses_bwd)


def _gdn_chunk(qs, ks, vs, ba, alog, dtb, states, inverses=_unit_lower_inverses):
    C = qs[0].shape[0]
    heads = range(len(qs))
    ri = lax.broadcasted_iota(jnp.int32, (C, C), 0)
    ci = lax.broadcasted_iota(jnp.int32, (C, C), 1)
    causal = ri >= ci
    strict = ri > ci
    tri = causal.astype(F32)
    betas = [jax.nn.sigmoid(_lane_pick(ba, h)) for h in heads]
    gs = [-jnp.exp(_lane_pick(alog, h)) * _softplus(_lane_pick(ba, h + HEADS) + _lane_pick(dtb, h)) for h in heads]
    qn = [q * lax.rsqrt(jnp.sum(q * q, axis=-1, keepdims=True) + 1e-6) * (HEAD_DIM ** -0.5) for q in qs]
    kn = [k * lax.rsqrt(jnp.sum(k * k, axis=-1, keepdims=True) + 1e-6) for k in ks]
    gc = [_hdot(tri, jnp.broadcast_to(g, (C, C))) for g in gs]
    decay = [jnp.where(causal, jnp.exp(jnp.where(causal, c - c.T, 0.0)), 0.0) for c in gc]
    eg = [jnp.exp(c) for c in gc]
    kb = [k * b for k, b in zip(kn, betas)]
    vb = [v * b for v, b in zip(vs, betas)]
    los = [jnp.where(strict, _hdot_nt(a, k) * d, 0.0) for a, k, d in zip(kb, kn, decay)]
    attn = [jnp.where(causal, _hdot_nt(q, k) * d, 0.0) for q, k, d in zip(qn, kn, decay)]
    tinv = inverses(los)
    u = [_hdot(t, x) for t, x in zip(tinv, vb)]
    w = [_hdot(t, a * e) for t, a, e in zip(tinv, kb, eg)]
    g_last = [c[C - 1:C, :] for c in gc]
    k_tail = [k * jnp.exp(gl - c) for k, gl, c in zip(kn, g_last, gc)]
    v_new = [a - _hdot(b, s) for a, b, s in zip(u, w, states)]
    o_state = [_hdot(q * e, s) for q, e, s in zip(qn, eg, states)]
    o = [a + _hdot(at, vn) for a, at, vn in zip(o_state, attn, v_new)]
    new_states = [s * jnp.exp(gl) + _hdot_tn(kt, vn) for s, gl, kt, vn in zip(states, g_last, k_tail, v_new)]
    return o, new_states


def _gdn_fwd(qkv, proj, alog, dtb):
    S = qkv.shape[0]
    nc = S // CHUNK
    assert CHUNK == HEAD_DIM

    def body(q_ref, k_ref, v_ref, ba_ref, alog_ref, dtb_ref, o_ref, st_ref, state):
        @pl.when(pl.program_id(0) == 0)
        def _():
            state[...] = jnp.zeros_like(state)

        sls = [slice(hd * HEAD_DIM, (hd + 1) * HEAD_DIM) for hd in range(HEADS)]
        s0 = [state[hd] for hd in range(HEADS)]
        for hd in range(HEADS):
            st_ref[hd, 0] = s0[hd]
        o, s1 = _gdn_chunk([q_ref[:, sl] for sl in sls], [k_ref[:, sl] for sl in sls], [v_ref[:, sl] for sl in sls],
                           ba_ref[...], alog_ref[...], dtb_ref[...], s0)
        for hd in range(HEADS):
            o_ref[:, sls[hd]] = o[hd]
            state[hd] = s1[hd]

    def col(j):
        return pl.BlockSpec((CHUNK, GDN_W), lambda n: (n, j))

    vec = _const((1, LANES))
    return pl.pallas_call(
        body, name="gdn_fwd", grid=(nc,),
        in_specs=[col(0), col(1), col(2), pl.BlockSpec((CHUNK, LANES), lambda n: (n, IN_PAD // LANES - 1)), vec, vec],
        out_specs=[col(0), pl.BlockSpec((HEADS, 1, HEAD_DIM, HEAD_DIM), lambda n: (0, n, 0, 0))],
        out_shape=[_sds((S, GDN_W)), _sds((HEADS, nc, HEAD_DIM, HEAD_DIM))],
        scratch_shapes=[pltpu.VMEM((HEADS, HEAD_DIM, HEAD_DIM), F32)],
        compiler_params=_params(("arbitrary",)),
    )(qkv, qkv, qkv, proj, alog, dtb)


def _gdn_bwd(do, qkv, proj, states, alog, dtb):
    S = qkv.shape[0]
    nc = S // CHUNK

    def body(do_ref, q_ref, k_ref, v_ref, ba_ref, st_ref, alog_ref, dtb_ref,
             dqkv_ref, dba_ref, dalog_ref, ddtb_ref, dstate):
        n = pl.program_id(0)

        @pl.when(n == 0)
        def _():
            dstate[...] = jnp.zeros_like(dstate)

        sls = [slice(hd * HEAD_DIM, (hd + 1) * HEAD_DIM) for hd in range(HEADS)]
        fn = functools.partial(_gdn_chunk, inverses=_unit_lower_inverses_diff)
        _, vjp = jax.vjp(fn, [q_ref[:, sl] for sl in sls], [k_ref[:, sl] for sl in sls], [v_ref[:, sl] for sl in sls],
                         ba_ref[...], alog_ref[...], dtb_ref[...], [st_ref[hd, 0] for hd in range(HEADS)])
        dq, dk, dv, dba, dalog, ddtb, ds = vjp(([do_ref[:, sl] for sl in sls], [dstate[hd] for hd in range(HEADS)]))
        for hd in range(HEADS):
            dqkv_ref[:, sls[hd]] = dq[hd]
            dqkv_ref[:, GDN_W + hd * HEAD_DIM:GDN_W + (hd + 1) * HEAD_DIM] = dk[hd]
            dqkv_ref[:, 2 * GDN_W + hd * HEAD_DIM:2 * GDN_W + (hd + 1) * HEAD_DIM] = dv[hd]
            dstate[hd] = ds[hd]
        dba_ref[...] = dba
        _acc(dalog_ref, dalog, n == 0)
        _acc(ddtb_ref, ddtb, n == 0)

    def col(j):
        return pl.BlockSpec((CHUNK, GDN_W), lambda n: (nc - 1 - n, j))

    vec = _const((1, LANES))
    return pl.pallas_call(
        body, name="gdn_bwd", grid=(nc,),
        in_specs=[col(0), col(0), col(1), col(2),
                  pl.BlockSpec((CHUNK, LANES), lambda n: (nc - 1 - n, IN_PAD // LANES - 1)),
                  pl.BlockSpec((HEADS, 1, HEAD_DIM, HEAD_DIM), lambda n: (0, nc - 1 - n, 0, 0)), vec, vec],
        out_specs=[pl.BlockSpec((CHUNK, 3 * GDN_W), lambda n: (nc - 1 - n, 0)),
                   pl.BlockSpec((CHUNK, LANES), lambda n: (nc - 1 - n, 0)), vec, vec],
        out_shape=[_sds((S, 3 * GDN_W)), _sds((S, LANES)), _sds((1, LANES)), _sds((1, LANES))],
        scratch_shapes=[pltpu.VMEM((HEADS, HEAD_DIM, HEAD_DIM), F32)],
        compiler_params=_params(("arbitrary",)),
    )(do, qkv, qkv, qkv, proj, states, alog, dtb)


def _gdn_gate(o, z, gnw):
    outs = []
    for hd in range(HEADS):
        sl = slice(hd * HEAD_DIM, (hd + 1) * HEAD_DIM)
        xh, _ = _rms_parts(o[:, sl])
        outs.append(xh * gnw * _silu(z[:, sl]))
    return jnp.concatenate(outs, axis=-1)


def _out_fwd(x, out_lru, o, proj, gnw, g1, wout):
    S = x.shape[0]
    tm = min(512, S)

    def body(x_ref, lru_ref, o_ref, z_ref, gnw_ref, g1_ref, w_ref, x1_ref, cat_ref):
        cat = jnp.concatenate([lru_ref[...], _gdn_gate(o_ref[...], z_ref[...], gnw_ref[...])], axis=-1).astype(BF16)
        cat_ref[...] = cat
        x1_ref[...] = x_ref[...] + g1_ref[...] * _dot(cat, w_ref[...])

    return pl.pallas_call(
        body, name="out_fwd", grid=(S // tm,),
        in_specs=[_row(tm, D_MODEL), _row(tm, LRU_W), _row(tm, GDN_W), _row(tm, GDN_W, 5), _const((1, LANES)),
                  _const((1, D_MODEL)), _const((D_MODEL, D_MODEL))],
        out_specs=[_row(tm, D_MODEL), _row(tm, D_MODEL)],
        out_shape=[_sds((S, D_MODEL)), _sds((S, D_MODEL), BF16)],
        compiler_params=_params(("arbitrary",)),
    )(x, out_lru, o, proj, gnw, g1, wout)


def _out_bwd(dx1, cat, o, proj, gnw, g1, wout):
    S = dx1.shape[0]
    tm = min(512, S)

    def body(dx1_ref, cat_ref, o_ref, z_ref, gnw_ref, g1_ref, w_ref,
             dlru_ref, do_ref, dz_ref, dmb_ref, dgnw_ref, dg1_ref):
        i = pl.program_id(0)
        d1 = dx1_ref[...]
        mix = _dot(cat_ref[...], w_ref[...])
        _acc(dg1_ref, _colsum(d1 * mix), i == 0)
        dmb = (d1 * g1_ref[...]).astype(BF16)
        dmb_ref[...] = dmb
        dcat = _dot_nt(dmb, w_ref[...])
        dlru_ref[...] = dcat[:, :LRU_W]
        _, vjp = jax.vjp(_gdn_gate, o_ref[...], z_ref[...], gnw_ref[...])
        do, dz, dgnw = vjp(dcat[:, LRU_W:])
        do_ref[...] = do
        dz_ref[...] = dz
        _acc(dgnw_ref, dgnw, i == 0)

    return pl.pallas_call(
        body, name="out_bwd", grid=(S // tm,),
        in_specs=[_row(tm, D_MODEL), _row(tm, D_MODEL), _row(tm, GDN_W), _row(tm, GDN_W, 5), _const((1, LANES)),
                  _const((1, D_MODEL)), _const((D_MODEL, D_MODEL))],
        out_specs=[_row(tm, LRU_W), _row(tm, GDN_W), _row(tm, GDN_W), _row(tm, D_MODEL), _const((1, LANES)),
                   _const((1, D_MODEL))],
        out_shape=[_sds((S, LRU_W)), _sds((S, GDN_W)), _sds((S, GDN_W)), _sds((S, D_MODEL), BF16), _sds((1, LANES)),
                   _sds((1, D_MODEL))],
        compiler_params=_params(("arbitrary",)),
    )(dx1, cat, o, proj, gnw, g1, wout)


MLP_TM = 256


def _load_once(step, pairs, sem):
    @pl.when(step == 0)
    def _():
        copies = [pltpu.make_async_copy(src, dst, sem.at[k]) for k, (src, dst) in enumerate(pairs)]
        for cp in copies:
            cp.start()
        for cp in copies:
            cp.wait()


def _mlp_fwd(x1, nw, sc, sh, g2, wup, wdown):
    S = x1.shape[0]
    tm = min(MLP_TM, S)

    def body(x_ref, nw_ref, sc_ref, sh_ref, g2_ref, wup_hbm, wdown_hbm, x2_ref, wup, wdown, sem):
        _load_once(pl.program_id(0), [(wup_hbm, wup), (wdown_hbm, wdown)], sem)
        x = x_ref[...]
        hb = _norm_mod(x, nw_ref[...], sc_ref[...], sh_ref[...]).astype(BF16)
        r = jnp.maximum(_dot(hb, wup[...]), 0.0)
        x2_ref[...] = x + g2_ref[...] * _dot((r * r).astype(BF16), wdown[...])

    vec = _const((1, D_MODEL))
    anyspec = pl.BlockSpec(memory_space=pl.ANY)
    return pl.pallas_call(
        body, name="mlp_fwd", grid=(S // tm,),
        in_specs=[_row(tm, D_MODEL), vec, vec, vec, vec, anyspec, anyspec],
        out_specs=_row(tm, D_MODEL),
        out_shape=_sds((S, D_MODEL)),
        scratch_shapes=[pltpu.VMEM((D_MODEL, D_FF), BF16), pltpu.VMEM((D_FF, D_MODEL), BF16),
                        pltpu.SemaphoreType.DMA((2,))],
        compiler_params=_params(("arbitrary",)),
    )(x1, nw, sc, sh, g2, wup, wdown)


def _mlp_bwd(dx2, x1, nw, sc, sh, g2, wup, wdown):
    S = x1.shape[0]
    tm = min(MLP_TM, S)

    def body(dx2_ref, x_ref, nw_ref, sc_ref, sh_ref, g2_ref, wup_hbm, wdown_hbm,
             dx1_ref, hb_ref, dupb_ref, actb_ref, ddb_ref, dnw_ref, dsc_ref, dsh_ref, dg2_ref, wup, wdown, sem):
        i = pl.program_id(0)
        _load_once(i, [(wup_hbm, wup), (wdown_hbm, wdown)], sem)
        x = x_ref[...]
        d2 = dx2_ref[...]
        hb = _norm_mod(x, nw_ref[...], sc_ref[...], sh_ref[...]).astype(BF16)
        hb_ref[...] = hb
        r = jnp.maximum(_dot(hb, wup[...]), 0.0)
        actb = (r * r).astype(BF16)
        actb_ref[...] = actb
        down = _dot(actb, wdown[...])
        _acc(dg2_ref, _colsum(d2 * down), i == 0)
        ddb = (d2 * g2_ref[...]).astype(BF16)
        ddb_ref[...] = ddb
        dupb = (_dot_nt(ddb, wdown[...]) * (2.0 * r)).astype(BF16)
        dupb_ref[...] = dupb
        dh = _dot_nt(dupb, wup[...])
        dx, dnw, dsc, dsh = _norm_mod_bwd(dh, x, nw_ref[...], sc_ref[...])
        dx1_ref[...] = d2 + dx
        _acc(dnw_ref, dnw, i == 0)
        _acc(dsc_ref, dsc, i == 0)
        _acc(dsh_ref, dsh, i == 0)

    vec = _const((1, D_MODEL))
    anyspec = pl.BlockSpec(memory_space=pl.ANY)
    return pl.pallas_call(
        body, name="mlp_bwd", grid=(S // tm,),
        in_specs=[_row(tm, D_MODEL), _row(tm, D_MODEL), vec, vec, vec, vec, anyspec, anyspec],
        out_specs=[_row(tm, D_MODEL), _row(tm, D_MODEL), _row(tm, D_FF), _row(tm, D_FF), _row(tm, D_MODEL),
                   vec, vec, vec, vec],
        out_shape=[_sds((S, D_MODEL)), _sds((S, D_MODEL), BF16), _sds((S, D_FF), BF16), _sds((S, D_FF), BF16),
                   _sds((S, D_MODEL), BF16), _sds((1, D_MODEL)), _sds((1, D_MODEL)), _sds((1, D_MODEL)),
                   _sds((1, D_MODEL))],
        scratch_shapes=[pltpu.VMEM((D_MODEL, D_FF), BF16), pltpu.VMEM((D_FF, D_MODEL), BF16),
                        pltpu.SemaphoreType.DMA((2,))],
        compiler_params=_params(("arbitrary",)),
    )(dx2, x1, nw, sc, sh, g2, wup, wdown)


def _matmul_tn(a, b, name):
    K, M = a.shape
    N = b.shape[1]
    tk = min(512, K)
    tm = min(512, M)
    tn = 640 if N % 640 == 0 else min(1024, N)
    nk = K // tk

    def body(a_ref, b_ref, o_ref, acc):
        k = pl.program_id(2)
        _acc(acc, _dot_tn(a_ref[...], b_ref[...]), k == 0)

        @pl.when(k == nk - 1)
        def _():
            o_ref[...] = acc[...]

    return pl.pallas_call(
        body, name=name, grid=(M // tm, N // tn, nk),
        in_specs=[pl.BlockSpec((tk, tm), lambda i, j, k: (k, i)), pl.BlockSpec((tk, tn), lambda i, j, k: (k, j))],
        out_specs=pl.BlockSpec((tm, tn), lambda i, j, k: (i, j)),
        out_shape=_sds((M, N)),
        scratch_shapes=[pltpu.VMEM((tm, tn), F32)],
        compiler_params=_params(("arbitrary", "arbitrary", "arbitrary")),
    )(a, b)


def _loss_head(x, target, fnw):
    S = x.shape[0]
    tm = min(512, S)

    def body(x_ref, t_ref, w_ref, dx_ref, loss_ref, dw_ref):
        i = pl.program_id(0)
        w = w_ref[...]
        xh, r = _rms_parts(x_ref[...])
        err = xh * w - t_ref[...]
        part = 0.5 * jnp.sum(jnp.mean(err * err, axis=-1, keepdims=True), axis=0, keepdims=True)
        _acc(loss_ref, jnp.broadcast_to(part, (SUBLANES, LANES)), i == 0)
        dx, dw = _rms_bwd(err * (1.0 / D_MODEL), xh, r, w)
        dx_ref[...] = dx
        _acc(dw_ref, dw, i == 0)

    vec = _const((1, D_MODEL))
    return pl.pallas_call(
        body, name="loss_head", grid=(S // tm,),
        in_specs=[_row(tm, D_MODEL), _row(tm, D_MODEL), vec],
        out_specs=[_row(tm, D_MODEL), _const((SUBLANES, LANES)), vec],
        out_shape=[_sds((S, D_MODEL)), _sds((SUBLANES, LANES)), _sds((1, D_MODEL))],
        compiler_params=_params(("arbitrary",)),
    )(x, target, fnw)


def _block_diag(w):
    eye = jnp.eye(LRU_BLOCKS, dtype=w.dtype)
    return (eye[:, None, :, None] * w[:, :, None, :]).reshape(LRU_W, LRU_W)


def _diag_blocks(m):
    m4 = m.reshape(LRU_BLOCKS, LRU_BLOCK, LRU_BLOCKS, LRU_BLOCK)
    return jnp.stack([m4[g, :, g, :] for g in range(LRU_BLOCKS)])


def _layer_fwd(x, p):
    proj, h1b = _proj_fwd(x, p["nmw"], p["sc1"], p["sh1"], p["win"])
    xr = _conv_fwd(proj, 0, LRU_W, p["lcw"], p["lcb"], False, "conv_lru_fwd")
    out_lru, h = _lru_fwd(xr, proj, p["wa"].astype(BF16), p["ba"], p["wx"].astype(BF16), p["bx"], p["lam"], p["lnw"])
    qkv = _conv_fwd(proj, 2 * LRU_W, 3 * GDN_W, p["gcw"], p["gcb"], True, "conv_gdn_fwd")
    o, states = _gdn_fwd(qkv, proj, p["alog"], p["dtb"])
    x1, cat = _out_fwd(x, out_lru, o, proj, p["gnw"], p["g1"], p["wout"])
    x2 = _mlp_fwd(x1, p["nmlp"], p["sc2"], p["sh2"], p["g2"], p["wup"], p["wdown"])
    res = dict(x=x, proj=proj, h1b=h1b, xr=xr, h=h, qkv=qkv, o=o, states=states, x1=x1, cat=cat)
    return x2, res


def _layer_bwd(dx2, p, r):
    dx1, h2b, dupb, actb, ddb, dnmlp, dsc2, dsh2, dg2 = _mlp_bwd(
        dx2, r["x1"], p["nmlp"], p["sc2"], p["sh2"], p["g2"], p["wup"], p["wdown"])
    g_wup = _matmul_tn(h2b, dupb, "dw_up")
    g_wdown = _matmul_tn(actb, ddb, "dw_down")
    dlru, do, dz, dmb, dgnw, dg1 = _out_bwd(dx1, r["cat"], r["o"], r["proj"], p["gnw"], p["g1"], p["wout"])
    g_wout = _matmul_tn(r["cat"], dmb, "dw_out")
    dqkv_act, dba, dalog, ddtb = _gdn_bwd(do, r["qkv"], r["proj"], r["states"], p["alog"], p["dtb"])
    dqkv, dgcw, _ = _conv_bwd(r["proj"], 2 * LRU_W, 3 * GDN_W, p["gcw"], p["gcb"], dqkv_act, True, "conv_gdn_bwd")
    wab = p["wa"].astype(BF16)
    wxb = p["wx"].astype(BF16)
    dxr, dly, dwa, dba_, dwx, dbx, dlam, dlnw = _lru_bwd(
        dlru, r["xr"], r["proj"], r["h"], wab, p["ba"], wxb, p["bx"], p["lam"], p["lnw"])
    dlx, dlcw, dlcb = _conv_bwd(r["proj"], 0, LRU_W, p["lcw"], p["lcb"], dxr, False, "conv_lru_bwd")
    dx, dpb, dnmw, dsc1, dsh1 = _proj_bwd(dx1, r["x"], dlx, dly, dqkv, dz, dba, p["nmw"], p["sc1"], p["win"])
    g_win = _matmul_tn(r["h1b"], dpb, "dw_in")
    grads = dict(nmw=dnmw, nmlp=dnmlp, sh1=dsh1, sc1=dsc1, g1=dg1, sh2=dsh2, sc2=dsc2, g2=dg2,
                 win=g_win, lcw=dlcw, lcb=dlcb, wa=dwa, ba=dba_, wx=dwx, bx=dbx, lam=dlam, lnw=dlnw,
                 gcw=dgcw, alog=dalog, dtb=ddtb, gnw=dgnw, wout=g_wout, wup=g_wup, wdown=g_wdown)
    return dx, grads


def _local_step(x, target, fnw, layers):
    def fwd(xc, p):
        return _layer_fwd(xc, p)

    x_out, res = lax.scan(fwd, x, layers)
    dx, loss_blk, dfnw = _loss_head(x_out, target, fnw)

    def bwd(dxc, pr):
        p, r = pr
        return _layer_bwd(dxc, p, r)

    dx0, grads = lax.scan(bwd, dx, (layers, res), reverse=True)
    return loss_blk[0, 0], dx0, dfnw, grads


def _prep_layers(norm_mix_w, norm_mlp_w, mod, win_b, lru_conv_w, lru_conv_b, gate_a_w, gate_a_b, gate_x_w, gate_x_b,
                 lru_lambda, lru_norm_w, gdn_conv_w, gdn_a_log, gdn_dt_bias, gdn_norm_w, wout_b, wup_b, wdown_b):
    L = norm_mix_w.shape[0]

    def vec(a):
        return a.reshape(L, 1, -1)

    def lanes(a):
        return jnp.pad(a, ((0, 0), (0, LANES - a.shape[1]))).reshape(L, 1, LANES)

    def taps(w):
        return jnp.pad(w, ((0, 0), (0, SUBLANES - w.shape[1]), (0, 0)))

    m = mod.reshape(L, N_MOD, 1, D_MODEL)
    return dict(
        nmw=vec(norm_mix_w), nmlp=vec(norm_mlp_w),
        sh1=m[:, 0], sc1=m[:, 1], g1=m[:, 2], sh2=m[:, 3], sc2=m[:, 4], g2=m[:, 5],
        win=win_b, lcw=taps(lru_conv_w), lcb=vec(lru_conv_b),
        wa=jax.vmap(_block_diag)(gate_a_w), ba=vec(gate_a_b), wx=jax.vmap(_block_diag)(gate_x_w), bx=vec(gate_x_b),
        lam=vec(lru_lambda), lnw=vec(lru_norm_w),
        gcw=taps(gdn_conv_w), gcb=jnp.zeros((L, 1, 3 * GDN_W), F32),
        alog=lanes(gdn_a_log), dtb=lanes(gdn_dt_bias), gnw=vec(gdn_norm_w),
        wout=wout_b, wup=wup_b, wdown=wdown_b)


def _position():
    x, y, c = lax.axis_index("x"), lax.axis_index("y"), lax.axis_index("c")
    return x, y, c


def _other_chips(x, y):
    return [(1 - x, y), (x, 1 - y), (1 - x, 1 - y)]


def _all_gather_rows(block, name):
    m, n = block.shape

    def body(x_ref, out_ref, send_sems, recv_sems, local_sem):
        x, y, c = _position()
        me, sibling = (x, y, c), (x, y, 1 - c)
        chips = _other_chips(x, y)

        def rows(px, py, pc):
            return out_ref.at[pl.ds((4 * px + 2 * py + pc) * m, m), :]

        def copy(k, blk, to, src=None):
            return pltpu.make_async_remote_copy(
                src_ref=rows(*blk) if src is None else src, dst_ref=rows(*blk),
                send_sem=send_sems.at[k], recv_sem=recv_sems.at[k], device_id=to, device_id_type=MESH)

        mine = pltpu.make_async_copy(x_ref, rows(*me), local_sem)
        mine.start()
        first = [copy(0, me, sibling, src=x_ref)]
        first += [copy(1 + j, me, (*chip, c), src=x_ref) for j, chip in enumerate(chips)]
        for cp in first:
            cp.start()
        passed = [copy(4 + j, (*chip, c), sibling) for j, chip in enumerate(chips)]
        for j, chip in enumerate(chips):
            copy(1 + j, (*chip, c), me).wait_recv()
            passed[j].start()
        copy(0, sibling, me).wait_recv()
        for j, chip in enumerate(chips):
            copy(4 + j, (*chip, 1 - c), me).wait_recv()
        for cp in first + passed:
            cp.wait_send()
        mine.wait()

    return pl.pallas_call(
        body, name=name,
        out_shape=_sds((N_DEV * m, n)),
        in_specs=[pl.BlockSpec(memory_space=pltpu.VMEM)],
        out_specs=pl.BlockSpec(memory_space=pltpu.VMEM),
        scratch_shapes=[pltpu.SemaphoreType.DMA((7,)), pltpu.SemaphoreType.DMA((7,)), pltpu.SemaphoreType.DMA],
        compiler_params=pltpu.CompilerParams(vmem_limit_bytes=VMEM_LIMIT),
    )(block)


def _hbm_specs(n):
    return [pl.BlockSpec(memory_space=pl.ANY)] * n


def _gather_chips(shards, name):
    n = len(shards)

    def body(*refs):
        ins, outs = refs[:n], refs[n:2 * n]
        send_sems, recv_sems, local_sems = refs[2 * n:]
        x, y, c = _position()
        chips = _other_chips(x, y)
        me = 2 * x + y
        local = [pltpu.make_async_copy(ins[a], outs[a].at[me], local_sems.at[a]) for a in range(n)]
        for cp in local:
            cp.start()

        def copy(a, j, slot):
            px, py = chips[j]
            return pltpu.make_async_remote_copy(
                src_ref=ins[a], dst_ref=outs[a].at[slot], send_sem=send_sems.at[3 * a + j],
                recv_sem=recv_sems.at[3 * a + j], device_id=(px, py, c), device_id_type=MESH)

        sends = [copy(a, j, me) for a in range(n) for j in range(3)]
        for cp in sends:
            cp.start()
        for a in range(n):
            for j, (px, py) in enumerate(chips):
                copy(a, j, 2 * px + py).wait_recv()
        for cp in sends:
            cp.wait_send()
        for cp in local:
            cp.wait()

    return pl.pallas_call(
        body, name=name,
        out_shape=[_sds((N_CHIPS,) + s.shape, s.dtype) for s in shards],
        in_specs=_hbm_specs(n), out_specs=_hbm_specs(n),
        scratch_shapes=[pltpu.SemaphoreType.DMA((3 * n,)), pltpu.SemaphoreType.DMA((3 * n,)),
                        pltpu.SemaphoreType.DMA((n,))],
    )(*shards)


def _send_to_sibling(parts, name):
    n = len(parts)

    def body(*refs):
        ins, outs = refs[:n], refs[n:2 * n]
        send_sems, recv_sems = refs[2 * n:]
        x, y, c = _position()
        copies = [pltpu.make_async_remote_copy(
            src_ref=ins[a].at[1 - c], dst_ref=outs[a], send_sem=send_sems.at[a], recv_sem=recv_sems.at[a],
            device_id=(x, y, 1 - c), device_id_type=MESH) for a in range(n)]
        for cp in copies:
            cp.start()
        for cp in copies:
            cp.wait()

    return pl.pallas_call(
        body, name=name,
        out_shape=[_sds(p.shape[1:], p.dtype) for p in parts],
        in_specs=_hbm_specs(n), out_specs=_hbm_specs(n),
        scratch_shapes=[pltpu.SemaphoreType.DMA((n,)), pltpu.SemaphoreType.DMA((n,))],
    )(*parts)


def _scatter_chips(parts, name):
    n = len(parts)

    def body(*refs):
        ins, outs = refs[:n], refs[n:2 * n]
        send_sems, recv_sems, local_sems = refs[2 * n:]
        x, y, c = _position()
        chips = _other_chips(x, y)
        me = 2 * x + y
        local = [pltpu.make_async_copy(ins[a].at[me], outs[a].at[me], local_sems.at[a]) for a in range(n)]
        for cp in local:
            cp.start()

        def copy(a, j, src_slot, dst_slot):
            px, py = chips[j]
            return pltpu.make_async_remote_copy(
                src_ref=ins[a].at[src_slot], dst_ref=outs[a].at[dst_slot], send_sem=send_sems.at[3 * a + j],
                recv_sem=recv_sems.at[3 * a + j], device_id=(px, py, c), device_id_type=MESH)

        sends = [copy(a, j, 2 * chips[j][0] + chips[j][1], me) for a in range(n) for j in range(3)]
        for cp in sends:
            cp.start()
        for a in range(n):
            for j, (px, py) in enumerate(chips):
                copy(a, j, me, 2 * px + py).wait_recv()
        for cp in sends:
            cp.wait_send()
        for cp in local:
            cp.wait()

    return pl.pallas_call(
        body, name=name,
        out_shape=[_sds(p.shape, p.dtype) for p in parts],
        in_specs=_hbm_specs(n), out_specs=_hbm_specs(n),
        scratch_shapes=[pltpu.SemaphoreType.DMA((3 * n,)), pltpu.SemaphoreType.DMA((3 * n,)),
                        pltpu.SemaphoreType.DMA((n,))],
    )(*parts)


def _share_with_sibling(halves, name):
    n = len(halves)

    def body(*refs):
        ins, outs = refs[:n], refs[n:2 * n]
        send_sems, recv_sems, local_sems = refs[2 * n:]
        x, y, c = _position()
        local = [pltpu.make_async_copy(ins[a], outs[a].at[c], local_sems.at[a]) for a in range(n)]
        for cp in local:
            cp.start()
        sends = [pltpu.make_async_remote_copy(
            src_ref=ins[a], dst_ref=outs[a].at[c], send_sem=send_sems.at[a], recv_sem=recv_sems.at[a],
            device_id=(x, y, 1 - c), device_id_type=MESH) for a in range(n)]
        for cp in sends:
            cp.start()
        for a in range(n):
            pltpu.make_async_remote_copy(
                src_ref=ins[a], dst_ref=outs[a].at[1 - c], send_sem=send_sems.at[a], recv_sem=recv_sems.at[a],
                device_id=(x, y, 1 - c), device_id_type=MESH).wait_recv()
        for cp in sends:
            cp.wait_send()
        for cp in local:
            cp.wait()

    return pl.pallas_call(
        body, name=name,
        out_shape=[_sds((2,) + h.shape, h.dtype) for h in halves],
        in_specs=_hbm_specs(n), out_specs=_hbm_specs(n),
        scratch_shapes=[pltpu.SemaphoreType.DMA((n,)), pltpu.SemaphoreType.DMA((n,)), pltpu.SemaphoreType.DMA((n,))],
    )(*halves)


def _row_tile(rows):
    for t in (512, 256, 128, 64, 32, 16, 8):
        if rows % t == 0:
            return t
    return rows


def _sum_slots(buf, name):
    k, rows, cols = buf.shape
    tm = _row_tile(rows)

    def body(b_ref, o_ref):
        s = b_ref[0]
        for i in range(1, k):
            s = s + b_ref[i]
        o_ref[...] = s

    return pl.pallas_call(
        body, name=name, grid=(rows // tm,),
        in_specs=[pl.BlockSpec((k, tm, cols), lambda i: (0, i, 0))],
        out_specs=pl.BlockSpec((tm, cols), lambda i: (i, 0)),
        out_shape=_sds((rows, cols)),
        compiler_params=_params(("arbitrary",)),
    )(buf)


def _add2(a, b, name):
    rows, cols = a.shape
    tm = _row_tile(rows)

    def body(a_ref, b_ref, o_ref):
        o_ref[...] = a_ref[...] + b_ref[...]

    spec = pl.BlockSpec((tm, cols), lambda i: (i, 0))
    return pl.pallas_call(
        body, name=name, grid=(rows // tm,), in_specs=[spec, spec], out_specs=spec, out_shape=_sds((rows, cols)),
        compiler_params=_params(("arbitrary",)),
    )(a, b)


def _adam_math(w, g, m, v):
    m = ADAM_B1 * m + (1.0 - ADAM_B1) * g
    v = ADAM_B2 * v + (1.0 - ADAM_B2) * jnp.square(g)
    m_hat = m / (1.0 - ADAM_B1 ** ADAM_STEP)
    v_hat = v / (1.0 - ADAM_B2 ** ADAM_STEP)
    delta = -ADAM_LR * (m_hat / (jnp.sqrt(v_hat) + ADAM_EPS) + ADAM_WD * w)
    return delta, m, v


def _adam(w, g, m, v, name):
    rows, cols = w.shape
    tm = _row_tile(rows)

    def body(w_ref, g_ref, m_ref, v_ref, d_ref, nm_ref, nv_ref):
        d, nm, nv = _adam_math(w_ref[...], g_ref[...], m_ref[...], v_ref[...])
        d_ref[...] = d
        nm_ref[...] = nm
        nv_ref[...] = nv

    spec = pl.BlockSpec((tm, cols), lambda i: (i, 0))
    return pl.pallas_call(
        body, name=name, grid=(rows // tm,), in_specs=[spec] * 4, out_specs=[spec] * 3,
        out_shape=[_sds((rows, cols))] * 3, compiler_params=_params(("arbitrary",)),
    )(w, g, m, v)


def _mod_fwd(c_all, w_mod, b_mod_cols):
    L, _, n = w_mod.shape

    def body(c_ref, w_ref, b_ref, o_ref):
        o_ref[0] = _hdot(_silu(c_ref[...]), w_ref[0]) + b_ref[0]

    return pl.pallas_call(
        body, name="mod_fwd", grid=(L,),
        in_specs=[_const((N_DEV, D_MODEL)), pl.BlockSpec((1, D_MODEL, n), lambda l: (l, 0, 0)),
                  pl.BlockSpec((1, 1, n), lambda l: (l, 0, 0))],
        out_specs=pl.BlockSpec((1, N_DEV, n), lambda l: (l, 0, 0)),
        out_shape=_sds((L, N_DEV, n)),
        compiler_params=_params(("arbitrary",)),
    )(c_all, w_mod, b_mod_cols)


def _mod_update(c_all, dmod, w, m, v):
    L, _, n = w.shape
    tn = 512

    def body(c_ref, d_ref, w_ref, m_ref, v_ref, g_ref, dl_ref, nm_ref, nv_ref):
        g = _hdot_tn(_silu(c_ref[...]), d_ref[0])
        g_ref[0] = g
        d, nm, nv = _adam_math(w_ref[0], g, m_ref[0], v_ref[0])
        dl_ref[0] = d
        nm_ref[0] = nm
        nv_ref[0] = nv

    big = pl.BlockSpec((1, D_MODEL, tn), lambda l, j: (l, 0, j))
    return pl.pallas_call(
        body, name="mod_update", grid=(L, n // tn),
        in_specs=[_const((N_DEV, D_MODEL)), pl.BlockSpec((1, N_DEV, tn), lambda l, j: (l, 0, j)), big, big, big],
        out_specs=[big] * 4, out_shape=[_sds(w.shape)] * 4,
        compiler_params=_params(("arbitrary", "arbitrary")),
    )(c_all, dmod, w, m, v)


def _pack_rows(parts, row_multiple):
    flat = jnp.concatenate([p.reshape(-1) for p in parts])
    unit = row_multiple * LANES
    flat = jnp.pad(flat, (0, (-flat.shape[0]) % unit))
    return flat.reshape(-1, LANES)


def _unpack(packed, shapes):
    flat = packed.reshape(-1)
    out, off = [], 0
    for s in shapes:
        n = 1
        for d in s:
            n *= d
        out.append(flat[off:off + n].reshape(s))
        off += n
    return out


def _lane_pad(a):
    return jnp.pad(a, ((0, 0), (0, LANES - a.shape[1])))


WEIGHT_NAMES = ("norm_mix_w", "norm_mlp_w", "w_mod", "b_mod", "w_in", "lru_conv_w", "lru_conv_b", "lru_gate_a_w",
                "lru_gate_a_b", "lru_gate_x_w", "lru_gate_x_b", "lru_lambda", "lru_norm_w", "gdn_conv_w", "gdn_a_log",
                "gdn_dt_bias", "gdn_norm_w", "w_out", "w_up", "w_down", "final_norm_w")


def kernel(x, c, norm_mix_w, norm_mlp_w, w_mod, b_mod, w_in, lru_conv_w, lru_conv_b, lru_gate_a_w, lru_gate_a_b, lru_gate_x_w, lru_gate_x_b, lru_lambda, lru_norm_w, gdn_conv_w, gdn_a_log, gdn_dt_bias, gdn_norm_w, w_out, w_up, w_down, final_norm_w, loss_target, m_norm_mix_w, m_norm_mlp_w, m_w_mod, m_b_mod, m_w_in, m_lru_conv_w, m_lru_conv_b, m_lru_gate_a_w, m_lru_gate_a_b, m_lru_gate_x_w, m_lru_gate_x_b, m_lru_lambda, m_lru_norm_w, m_gdn_conv_w, m_gdn_a_log, m_gdn_dt_bias, m_gdn_norm_w, m_w_out, m_w_up, m_w_down, m_final_norm_w, v_norm_mix_w, v_norm_mlp_w, v_w_mod, v_b_mod, v_w_in, v_lru_conv_w, v_lru_conv_b, v_lru_gate_a_w, v_lru_gate_a_b, v_lru_gate_x_w, v_lru_gate_x_b, v_lru_lambda, v_lru_norm_w, v_gdn_conv_w, v_gdn_a_log, v_gdn_dt_bias, v_gdn_norm_w, v_w_out, v_w_up, v_w_down, v_final_norm_w):
    W = dict(zip(WEIGHT_NAMES, (norm_mix_w, norm_mlp_w, w_mod, b_mod, w_in, lru_conv_w, lru_conv_b, lru_gate_a_w,
                                lru_gate_a_b, lru_gate_x_w, lru_gate_x_b, lru_lambda, lru_norm_w, gdn_conv_w, gdn_a_log,
                                gdn_dt_bias, gdn_norm_w, w_out, w_up, w_down, final_norm_w)))
    M = dict(zip(WEIGHT_NAMES, (m_norm_mix_w, m_norm_mlp_w, m_w_mod, m_b_mod, m_w_in, m_lru_conv_w, m_lru_conv_b,
                                m_lru_gate_a_w, m_lru_gate_a_b, m_lru_gate_x_w, m_lru_gate_x_b, m_lru_lambda,
                                m_lru_norm_w, m_gdn_conv_w, m_gdn_a_log, m_gdn_dt_bias, m_gdn_norm_w, m_w_out, m_w_up,
                                m_w_down, m_final_norm_w)))
    V = dict(zip(WEIGHT_NAMES, (v_norm_mix_w, v_norm_mlp_w, v_w_mod, v_b_mod, v_w_in, v_lru_conv_w, v_lru_conv_b,
                                v_lru_gate_a_w, v_lru_gate_a_b, v_lru_gate_x_w, v_lru_gate_x_b, v_lru_lambda,
                                v_lru_norm_w, v_gdn_conv_w, v_gdn_a_log, v_gdn_dt_bias, v_gdn_norm_w, v_w_out, v_w_up,
                                v_w_down, v_final_norm_w)))
    L = DEPTH
    xi, yi, ci = _position()
    chip = 2 * xi + yi
    dev = 2 * chip + ci
    lcs = LRU_W // N_CHIPS
    gcs = 3 * GDN_W // N_CHIPS
    mcs = N_MOD * D_MODEL // N_CHIPS

    g_in = _all_gather_rows(_pack_rows([c, lru_conv_w, gdn_conv_w], SUBLANES), "gather_small_inputs").reshape(N_DEV, -1)
    c_all = g_in[:, :D_MODEL]
    per_chip = g_in[0::2]
    o1 = D_MODEL + L * 4 * lcs
    lcw_full = per_chip[:, D_MODEL:o1].reshape(N_CHIPS, L, 4, lcs).transpose(1, 2, 0, 3).reshape(L, 4, LRU_W)
    gcw_full = per_chip[:, o1:o1 + L * 4 * gcs].reshape(N_CHIPS, L, 4, gcs).transpose(1, 2, 0, 3).reshape(L, 4, 3 * GDN_W)

    b_cols = lax.dynamic_slice(b_mod, (0, chip * mcs), (L, mcs)).reshape(L, 1, mcs)
    modp = _mod_fwd(c_all, w_mod, b_cols)
    g_mod = _all_gather_rows(modp.reshape(L * N_DEV, mcs), "gather_mod").reshape(N_DEV, L, N_DEV, mcs)
    mod = lax.dynamic_index_in_dim(g_mod[0::2], dev, axis=2, keepdims=False).transpose(1, 0, 2).reshape(L, N_MOD * D_MODEL)

    win_g, wout_g, wup_g, wdown_g = _gather_chips(
        [w_in.astype(BF16), w_out.astype(BF16), w_up.astype(BF16), w_down.astype(BF16)], "gather_weights")
    win_b = jnp.pad(win_g.transpose(1, 2, 0, 3).reshape(L, D_MODEL, IN_COLS), ((0, 0), (0, 0), (0, IN_PAD - IN_COLS)))
    wout_b = wout_g.transpose(1, 0, 2, 3).reshape(L, D_MODEL, D_MODEL)
    wup_b = wup_g.transpose(1, 2, 0, 3).reshape(L, D_MODEL, D_FF)
    wdown_b = wdown_g.transpose(1, 0, 2, 3).reshape(L, D_FF, D_MODEL)

    layers = _prep_layers(norm_mix_w, norm_mlp_w, mod, win_b, lcw_full, lru_conv_b, lru_gate_a_w, lru_gate_a_b,
                          lru_gate_x_w, lru_gate_x_b, lru_lambda, lru_norm_w, gcw_full, gdn_a_log, gdn_dt_bias,
                          gdn_norm_w, wout_b, wup_b, wdown_b)
    loss_local, dx, dfnw, g = _local_step(x[0], loss_target[0], final_norm_w.reshape(1, D_MODEL), layers)
    loss = lax.psum(loss_local, ("x", "y", "c"))

    dmod = jnp.concatenate([g["sh1"], g["sc1"], g["g1"], g["sh2"], g["sc2"], g["g2"]], axis=-1)
    small = [dmod, g["nmw"], g["nmlp"], g["lcw"][:, :4], g["lcb"], jax.vmap(_diag_blocks)(g["wa"]), g["ba"],
             jax.vmap(_diag_blocks)(g["wx"]), g["bx"], g["lam"], g["lnw"], g["gcw"][:, :4], g["alog"], g["dtb"],
             g["gnw"], dfnw]
    small_shapes = [(L, N_MOD * D_MODEL), (L, D_MODEL), (L, D_MODEL), (L, 4, LRU_W), (L, LRU_W),
                    (L, LRU_BLOCKS, LRU_BLOCK, LRU_BLOCK), (L, LRU_W), (L, LRU_BLOCKS, LRU_BLOCK, LRU_BLOCK),
                    (L, LRU_W), (L, LRU_W), (L, LRU_W), (L, 4, 3 * GDN_W), (L, LANES), (L, LANES), (L, LANES),
                    (D_MODEL,)]
    small_names = ["b_mod", "norm_mix_w", "norm_mlp_w", None, "lru_conv_b", "lru_gate_a_w", "lru_gate_a_b",
                   "lru_gate_x_w", "lru_gate_x_b", "lru_lambda", "lru_norm_w", None, "gdn_a_log", "gdn_dt_bias",
                   "gdn_norm_w", "final_norm_w"]
    pack_g = _pack_rows(small, 512)
    rows = pack_g.shape[0]
    all_g = _all_gather_rows(pack_g, "gather_small_grads").reshape(N_DEV, rows, LANES)
    tot = _sum_slots(all_g, "sum_small_grads")
    tot_parts = _unpack(tot, small_shapes)

    def pack_state(S_):
        parts = []
        for nm, shp in zip(small_names, small_shapes):
            if nm is None:
                parts.append(jnp.zeros(shp, F32))
            elif nm in ("gdn_a_log", "gdn_dt_bias"):
                parts.append(_lane_pad(S_[nm]))
            else:
                parts.append(S_[nm])
        return _pack_rows(parts, 512)

    upd = _adam(pack_state(W), tot, pack_state(M), pack_state(V), "adam_small")
    upd_parts = [_unpack(u, small_shapes) for u in upd]

    grads, deltas, new_m, new_v = {}, {}, {}, {}
    for k, nm in enumerate(small_names):
        if nm is None:
            continue
        cut = (lambda a: a[:, :HEADS]) if nm in ("gdn_a_log", "gdn_dt_bias") else (lambda a: a)
        grads[nm] = cut(tot_parts[k])
        deltas[nm], new_m[nm], new_v[nm] = (cut(u[k]) for u in upd_parts)

    g_lcw = lax.dynamic_slice(tot_parts[3], (0, 0, chip * lcs), (L, 4, lcs))
    g_gcw = lax.dynamic_slice(tot_parts[11], (0, 0, chip * gcs), (L, 4, gcs))
    conv_shapes = [(L, 4, lcs), (L, 4, gcs)]
    conv_pack = lambda a, b: _pack_rows([a, b], SUBLANES)
    cu = _adam(conv_pack(lru_conv_w, gdn_conv_w), conv_pack(g_lcw, g_gcw), conv_pack(m_lru_conv_w, m_gdn_conv_w),
               conv_pack(v_lru_conv_w, v_gdn_conv_w), "adam_conv")
    cu_parts = [_unpack(u, conv_shapes) for u in cu]
    for k, nm in enumerate(("lru_conv_w", "gdn_conv_w")):
        grads[nm] = (g_lcw, g_gcw)[k]
        deltas[nm], new_m[nm], new_v[nm] = (u[k] for u in cu_parts)

    dmod_all = all_g[:, :L * N_MOD * D_MODEL // LANES].reshape(N_DEV, L, N_MOD * D_MODEL)
    dmod_cols = lax.dynamic_slice(dmod_all, (0, 0, chip * mcs), (N_DEV, L, mcs)).transpose(1, 0, 2)
    grads["w_mod"], deltas["w_mod"], new_m["w_mod"], new_v["w_mod"] = _mod_update(c_all, dmod_cols, w_mod, m_w_mod, v_w_mod)

    half = L // 2
    gwin = g["win"][:, :, :IN_COLS].reshape(2, half, D_MODEL, N_CHIPS, IN_COLS // N_CHIPS).transpose(0, 3, 1, 2, 4)
    gwout = g["wout"].reshape(2, half, N_CHIPS, D_MODEL // N_CHIPS, D_MODEL).transpose(0, 2, 1, 3, 4)
    gwup = g["wup"].reshape(2, half, D_MODEL, N_CHIPS, D_FF // N_CHIPS).transpose(0, 3, 1, 2, 4)
    gwdown = g["wdown"].reshape(2, half, N_CHIPS, D_FF // N_CHIPS, D_MODEL).transpose(0, 2, 1, 3, 4)
    parts = [gwin, gwout, gwup, gwdown]
    big_names = ["w_in", "w_out", "w_up", "w_down"]
    from_sibling = _send_to_sibling(parts, "pair_send")
    pair = []
    for nm, p, r in zip(big_names, parts, from_sibling):
        own = lax.dynamic_index_in_dim(p, ci, axis=0, keepdims=False)
        cols = own.shape[-1]
        pair.append(_add2(own.reshape(-1, cols), r.reshape(-1, cols), "pair_add_" + nm).reshape(own.shape))
    arrived = _scatter_chips(pair, "chip_scatter")
    halves = []
    for nm, a in zip(big_names, arrived):
        cols = a.shape[-1]
        halves.append(_sum_slots(a.reshape(N_CHIPS, -1, cols), "chip_sum_" + nm).reshape(a.shape[1:]))
    full = _share_with_sibling(halves, "pair_share")
    for nm, f in zip(big_names, full):
        shard = W[nm].shape
        cols = shard[-1]
        gr = f.reshape(-1, cols)
        d_, m_, v_ = _adam(W[nm].reshape(-1, cols), gr, M[nm].reshape(-1, cols), V[nm].reshape(-1, cols), "adam_" + nm)
        grads[nm] = gr.reshape(shard)
        deltas[nm], new_m[nm], new_v[nm] = d_.reshape(shard), m_.reshape(shard), v_.reshape(shard)

    out = [loss, dx[None]]
    for group in (grads, deltas, new_m, new_v):
        out += [group[nm].reshape(W[nm].shape) for nm in WEIGHT_NAMES]
    return tuple(out)
```

```python
import functools

import jax
import jax.numpy as jnp
from jax import lax
from jax.experimental import pallas as pl
from jax.experimental.pallas import tpu as pltpu

F32 = jnp.float32
BF16 = jnp.bfloat16
MESH = pl.DeviceIdType.MESH

D_MODEL = 1024
DEPTH = 4
LRU_W = 512
LRU_BLOCKS = 8
LRU_BLOCK = 64
LRU_C = 8.0
HEADS = 4
HEAD_DIM = 128
GDN_W = 512
CHUNK = 128
D_FF = 4096
N_MOD = 6
IN_COLS = 3080
IN_PAD = 3200
NORM_EPS = 1e-6
LANES = 128
SUBLANES = 8
N_DEV = 8
N_CHIPS = 4

ADAM_LR = 0.001
ADAM_B1 = 0.9
ADAM_B2 = 0.999
ADAM_EPS = 1e-08
ADAM_WD = 0.01
ADAM_STEP = 10

VMEM_LIMIT = 56 * 1024 * 1024
HI = lax.Precision.HIGHEST


def _sds(shape, dtype=F32):
    return jax.ShapeDtypeStruct(tuple(shape), dtype)


def _params(sem=None, vmem=VMEM_LIMIT):
    return pltpu.CompilerParams(dimension_semantics=sem, vmem_limit_bytes=vmem)


def _const(shape):
    return pl.BlockSpec(tuple(shape), lambda *_: (0,) * len(shape))


def _row(tm, c, col=0):
    return pl.BlockSpec((tm, c), lambda i: (i, col))


def _dot(a, b):
    return jnp.dot(a, b, preferred_element_type=F32)


def _dot_nt(a, b):
    return lax.dot_general(a, b, (((1,), (1,)), ((), ())), preferred_element_type=F32)


def _dot_tn(a, b):
    return lax.dot_general(a, b, (((0,), (0,)), ((), ())), preferred_element_type=F32)


def _hdot(a, b):
    return jnp.dot(a, b, preferred_element_type=F32, precision=HI)


def _hdot_nt(a, b):
    return lax.dot_general(a, b, (((1,), (1,)), ((), ())), preferred_element_type=F32, precision=HI)


def _hdot_tn(a, b):
    return lax.dot_general(a, b, (((0,), (0,)), ((), ())), preferred_element_type=F32, precision=HI)


def _acc(ref, val, first):
    @pl.when(first)
    def _():
        ref[...] = val

    @pl.when(jnp.logical_not(first))
    def _():
        ref[...] += val


def _colsum(v):
    return jnp.sum(v, axis=0, keepdims=True)


def _rms_parts(x):
    r = lax.rsqrt(jnp.mean(x * x, axis=-1, keepdims=True) + NORM_EPS)
    return x * r, r


def _rms_bwd(dy, xh, r, w):
    dxh = dy * w
    dw = _colsum(dy * xh)
    dx = r * (dxh - xh * jnp.mean(dxh * xh, axis=-1, keepdims=True))
    return dx, dw


def _norm_mod(x, w, sc, sh):
    xh, _ = _rms_parts(x)
    return (xh * w) * (1.0 + sc) + sh


def _norm_mod_bwd(dy, x, w, sc):
    xh, r = _rms_parts(x)
    n = xh * w
    dsh = _colsum(dy)
    dsc = _colsum(dy * n)
    dx, dw = _rms_bwd(dy * (1.0 + sc), xh, r, w)
    return dx, dw, dsc, dsh


def _softplus(x):
    return jnp.maximum(x, 0.0) + jnp.log1p(jnp.exp(-jnp.abs(x)))


def _silu(x):
    return x * jax.nn.sigmoid(x)


def _silu_grad(x):
    s = jax.nn.sigmoid(x)
    return s * (1.0 + x * (1.0 - s))


def _roll_dn(x, d):
    return x if d == 0 else pltpu.roll(x, d, 0)


def _roll_up(x, d):
    return x if d == 0 else pltpu.roll(x, x.shape[0] - d, 0)


def _proj_fwd(x, nw, sc, sh, win):
    S = x.shape[0]
    tm = min(512, S)

    def body(x_ref, nw_ref, sc_ref, sh_ref, w_ref, proj_ref, hb_ref):
        hb = _norm_mod(x_ref[...], nw_ref[...], sc_ref[...], sh_ref[...]).astype(BF16)
        hb_ref[...] = hb
        proj_ref[...] = _dot(hb, w_ref[...])

    vec = _const((1, D_MODEL))
    return pl.pallas_call(
        body, name="proj_fwd", grid=(S // tm,),
        in_specs=[_row(tm, D_MODEL), vec, vec, vec, _const((D_MODEL, IN_PAD))],
        out_specs=[_row(tm, IN_PAD), _row(tm, D_MODEL)],
        out_shape=[_sds((S, IN_PAD)), _sds((S, D_MODEL), BF16)],
        compiler_params=_params(("arbitrary",)),
    )(x, nw, sc, sh, win)


def _proj_bwd(dx1, x, dlx, dly, dqkv, dz, dba, nw, sc, win):
    S = x.shape[0]
    tm = min(512, S)

    def body(dx1_ref, x_ref, dlx_ref, dly_ref, dqkv_ref, dz_ref, dba_ref, nw_ref, sc_ref, w_ref,
             dx_ref, dpb_ref, dnw_ref, dsc_ref, dsh_ref):
        i = pl.program_id(0)
        dpb = jnp.concatenate([dlx_ref[...], dly_ref[...], dqkv_ref[...], dz_ref[...], dba_ref[...]],
                              axis=-1).astype(BF16)
        dpb_ref[...] = dpb
        dh = _dot_nt(dpb, w_ref[...])
        dx, dnw, dsc, dsh = _norm_mod_bwd(dh, x_ref[...], nw_ref[...], sc_ref[...])
        dx_ref[...] = dx1_ref[...] + dx
        _acc(dnw_ref, dnw, i == 0)
        _acc(dsc_ref, dsc, i == 0)
        _acc(dsh_ref, dsh, i == 0)

    vec = _const((1, D_MODEL))
    return pl.pallas_call(
        body, name="proj_bwd", grid=(S // tm,),
        in_specs=[_row(tm, D_MODEL), _row(tm, D_MODEL), _row(tm, LRU_W), _row(tm, LRU_W), _row(tm, 3 * GDN_W),
                  _row(tm, GDN_W), _row(tm, LANES), vec, vec,
                  _const((D_MODEL, IN_PAD))],
        out_specs=[_row(tm, D_MODEL), _row(tm, IN_PAD), vec, vec, vec],
        out_shape=[_sds((S, D_MODEL)), _sds((S, IN_PAD), BF16), _sds((1, D_MODEL)), _sds((1, D_MODEL)),
                   _sds((1, D_MODEL))],
        compiler_params=_params(("arbitrary",)),
    )(dx1, x, dlx, dly, dqkv, dz, dba, nw, sc, win)


def _conv_taps(xx, w, tm):
    y = _roll_dn(xx, 3)[SUBLANES:] * w[0:1]
    y = y + _roll_dn(xx, 2)[SUBLANES:] * w[1:2]
    y = y + _roll_dn(xx, 1)[SUBLANES:] * w[2:3]
    y = y + xx[SUBLANES:] * w[3:4]
    return y


def _conv_fwd(src, col0, C, w8, b, act, name):
    S = src.shape[0]
    tm = min(512, S)
    tc = 512
    hb = tm // SUBLANES
    cb0 = col0 // tc

    def body(x_ref, p_ref, w_ref, b_ref, y_ref):
        i = pl.program_id(0)
        prev = jnp.where(i > 0, p_ref[...], 0.0)
        xx = jnp.concatenate([prev, x_ref[...]], axis=0)
        y = _conv_taps(xx, w_ref[...], tm) + b_ref[...]
        y_ref[...] = _silu(y) if act else y

    return pl.pallas_call(
        body, name=name, grid=(S // tm, C // tc),
        in_specs=[pl.BlockSpec((tm, tc), lambda i, j: (i, cb0 + j)),
                  pl.BlockSpec((SUBLANES, tc), lambda i, j: (jnp.maximum(i * hb - 1, 0), cb0 + j)),
                  pl.BlockSpec((SUBLANES, tc), lambda i, j: (0, j)),
                  pl.BlockSpec((1, tc), lambda i, j: (0, j))],
        out_specs=pl.BlockSpec((tm, tc), lambda i, j: (i, j)),
        out_shape=_sds((S, C)),
        compiler_params=_params(("arbitrary", "arbitrary")),
    )(src, src, w8, b)


def _conv_bwd(src, col0, C, w8, b, dyact, act, name):
    S = src.shape[0]
    tm = min(512, S)
    tc = 512
    hb = tm // SUBLANES
    nt = S // tm
    cb0 = col0 // tc
    last_hb = S // SUBLANES - 1

    def body(x_ref, p_ref, n_ref, dy_ref, dyn_ref, w_ref, b_ref, dx_ref, dw_ref, db_ref):
        i = pl.program_id(1)
        w = w_ref[...]
        prev = jnp.where(i > 0, p_ref[...], 0.0)
        xx = jnp.concatenate([prev, x_ref[...], n_ref[...]], axis=0)
        dy = jnp.concatenate([dy_ref[...], jnp.where(i < nt - 1, dyn_ref[...], 0.0)], axis=0)
        if act:
            ypre = _conv_taps(xx, w, tm + SUBLANES) + b_ref[...]
            dy = dy * _silu_grad(ypre)
        dx = dy[:tm] * w[3:4]
        for d in (1, 2, 3):
            dx = dx + _roll_up(dy, d)[:tm] * w[3 - d:4 - d]
        dx_ref[...] = dx
        xt = xx[:tm + SUBLANES]
        dyt = dy[:tm]
        rows = [_colsum(dyt * _roll_dn(xt, 3 - k)[SUBLANES:]) for k in range(4)]
        dw = jnp.concatenate(rows + [jnp.zeros((SUBLANES - 4, tc), F32)], axis=0)
        _acc(dw_ref, dw, i == 0)
        _acc(db_ref, _colsum(dyt), i == 0)

    return pl.pallas_call(
        body, name=name, grid=(C // tc, nt),
        in_specs=[pl.BlockSpec((tm, tc), lambda j, i: (i, cb0 + j)),
                  pl.BlockSpec((SUBLANES, tc), lambda j, i: (jnp.maximum(i * hb - 1, 0), cb0 + j)),
                  pl.BlockSpec((SUBLANES, tc), lambda j, i: (jnp.minimum((i + 1) * hb, last_hb), cb0 + j)),
                  pl.BlockSpec((tm, tc), lambda j, i: (i, j)),
                  pl.BlockSpec((SUBLANES, tc), lambda j, i: (jnp.minimum((i + 1) * hb, last_hb), j)),
                  pl.BlockSpec((SUBLANES, tc), lambda j, i: (0, j)),
                  pl.BlockSpec((1, tc), lambda j, i: (0, j))],
        out_specs=[pl.BlockSpec((tm, tc), lambda j, i: (i, j)),
                   pl.BlockSpec((SUBLANES, tc), lambda j, i: (0, j)),
                   pl.BlockSpec((1, tc), lambda j, i: (0, j))],
        out_shape=[_sds((S, C)), _sds((SUBLANES, C)), _sds((1, C))],
        compiler_params=_params(("arbitrary", "arbitrary")),
    )(src, src, src, dyact, dyact, w8, b)


def _lru_ab(pre_a, pre_x, xr, lam):
    r = jax.nn.sigmoid(pre_a)
    g = jax.nn.sigmoid(pre_x)
    log_sig = -_softplus(-lam)
    log_a = LRU_C * r * log_sig
    a = jnp.exp(log_a)
    t = jnp.tanh(log_a)
    mult = jnp.sqrt(jnp.maximum(-2.0 * t / (1.0 - t), 1e-12))
    return a, mult * (g * xr)


def _lru_tail(h, ly, lnw):
    xh, _ = _rms_parts(h * jax.nn.gelu(ly))
    return xh * lnw


def _scan_down(a, b):
    n = a.shape[0]
    row = lax.broadcasted_iota(jnp.int32, a.shape, 0)
    d = 1
    while d < n:
        keep = row >= d
        a_s = jnp.where(keep, _roll_dn(a, d), 1.0)
        b_s = jnp.where(keep, _roll_dn(b, d), 0.0)
        b = a * b_s + b
        a = a * a_s
        d *= 2
    return a, b


def _scan_up(a, b):
    n = a.shape[0]
    row = lax.broadcasted_iota(jnp.int32, a.shape, 0)
    d = 1
    while d < n:
        keep = row < n - d
        a_s = jnp.where(keep, _roll_up(a, d), 1.0)
        b_s = jnp.where(keep, _roll_up(b, d), 0.0)
        b = a * b_s + b
        a = a * a_s
        d *= 2
    return a, b


LRU_TM = 256


def _lru_fwd(xr, proj, wa, ba, wx, bx, lam, lnw):
    S = xr.shape[0]
    tm = min(LRU_TM, S)

    def body(xr_ref, ly_ref, wa_ref, ba_ref, wx_ref, bx_ref, lam_ref, lnw_ref, out_ref, h_ref, carry):
        i = pl.program_id(0)

        @pl.when(i == 0)
        def _():
            carry[...] = jnp.zeros_like(carry)

        x = xr_ref[...]
        xb = x.astype(BF16)
        pre_a = _dot(xb, wa_ref[...]) + ba_ref[...]
        pre_x = _dot(xb, wx_ref[...]) + bx_ref[...]
        a, b = _lru_ab(pre_a, pre_x, x, lam_ref[...])
        ca, hl = _scan_down(a, b)
        h = hl + ca * carry[0:1, :]
        carry[0:1, :] = h[tm - 1:tm, :]
        h_ref[...] = h
        out_ref[...] = _lru_tail(h, ly_ref[...], lnw_ref[...])

    vec = _const((1, LRU_W))
    mat = _const((LRU_W, LRU_W))
    return pl.pallas_call(
        body, name="lru_fwd", grid=(S // tm,),
        in_specs=[_row(tm, LRU_W), _row(tm, LRU_W, 1), mat, vec, mat, vec, vec, vec],
        out_specs=[_row(tm, LRU_W), _row(tm, LRU_W)],
        out_shape=[_sds((S, LRU_W)), _sds((S, LRU_W))],
        scratch_shapes=[pltpu.VMEM((SUBLANES, LRU_W), F32)],
        compiler_params=_params(("arbitrary",)),
    )(xr, proj, wa, ba, wx, bx, lam, lnw)


def _lru_bwd(dout, xr, proj, h, wa, ba, wx, bx, lam, lnw):
    S = xr.shape[0]
    tm = min(LRU_TM, S)
    nt = S // tm
    hb = tm // SUBLANES

    def rev(col=0):
        return pl.BlockSpec((tm, LRU_W), lambda i: (nt - 1 - i, col))

    def body(dout_ref, xr_ref, ly_ref, h_ref, hp_ref, wa_ref, ba_ref, wx_ref, bx_ref, lam_ref, lnw_ref,
             dxr_ref, dly_ref, dwa_ref, dba_ref, dwx_ref, dbx_ref, dlam_ref, dlnw_ref, carry):
        i = pl.program_id(0)
        first = i == 0

        @pl.when(first)
        def _():
            carry[...] = jnp.zeros_like(carry)

        x = xr_ref[...]
        xb = x.astype(BF16)
        pre_a = _dot(xb, wa_ref[...]) + ba_ref[...]
        pre_x = _dot(xb, wx_ref[...]) + bx_ref[...]
        (a, b), ab_vjp = jax.vjp(_lru_ab, pre_a, pre_x, x, lam_ref[...])
        h_t = h_ref[...]
        _, tail_vjp = jax.vjp(_lru_tail, h_t, ly_ref[...], lnw_ref[...])
        dh, dly, dlnw = tail_vjp(dout_ref[...])
        dly_ref[...] = dly
        row = lax.broadcasted_iota(jnp.int32, a.shape, 0)
        a_next = jnp.where(row == tm - 1, carry[0:1, :], _roll_up(a, 1))
        ca, gl = _scan_up(a_next, dh)
        g = gl + ca * carry[1:2, :]
        carry[0:1, :] = a[0:1, :]
        carry[1:2, :] = g[0:1, :]
        h_before = jnp.where(i == nt - 1, 0.0, hp_ref[SUBLANES - 1:SUBLANES, :])
        h_prev = jnp.where(row == 0, h_before, _roll_dn(h_t, 1))
        dpa, dpx, dx, dlam = ab_vjp((g * h_prev, g))
        dpab = dpa.astype(BF16)
        dpxb = dpx.astype(BF16)
        dxr_ref[...] = dx + _dot_nt(dpab, wa_ref[...]) + _dot_nt(dpxb, wx_ref[...])
        _acc(dwa_ref, _dot_tn(xb, dpab), first)
        _acc(dwx_ref, _dot_tn(xb, dpxb), first)
        _acc(dba_ref, _colsum(dpa), first)
        _acc(dbx_ref, _colsum(dpx), first)
        _acc(dlam_ref, dlam, first)
        _acc(dlnw_ref, dlnw, first)

    vec = _const((1, LRU_W))
    mat = _const((LRU_W, LRU_W))
    return pl.pallas_call(
        body, name="lru_bwd", grid=(nt,),
        in_specs=[rev(), rev(), rev(1), rev(),
                  pl.BlockSpec((SUBLANES, LRU_W), lambda i: (jnp.maximum((nt - 1 - i) * hb - 1, 0), 0)),
                  mat, vec, mat, vec, vec, vec],
        out_specs=[rev(), rev(), mat, vec, mat, vec, vec, vec],
        out_shape=[_sds((S, LRU_W)), _sds((S, LRU_W)), _sds((LRU_W, LRU_W)), _sds((1, LRU_W)),
                   _sds((LRU_W, LRU_W)), _sds((1, LRU_W)), _sds((1, LRU_W)), _sds((1, LRU_W))],
        scratch_shapes=[pltpu.VMEM((SUBLANES, LRU_W), F32)],
        compiler_params=_params(("arbitrary",)),
    )(dout, xr, proj, h, h, wa, ba, wx, bx, lam, lnw)


def _lane_pick(row_or_tile, lane):
    idx = lax.broadcasted_iota(jnp.int32, row_or_tile.shape, 1)
    return jnp.sum(jnp.where(idx == lane, row_or_tile, 0.0), axis=-1, keepdims=True)


def _unit_lower_inverses(los):
    n = los[0].shape[0]
    ri = lax.broadcasted_iota(jnp.int32, (n, n), 0)
    ci = lax.broadcasted_iota(jnp.int32, (n, n), 1)
    invs = [(ri == ci).astype(F32) for _ in los]
    s = 1
    while s < n:
        same_block = (ri & ~(2 * s - 1)) == (ci & ~(2 * s - 1))
        lower_left = same_block & ((ri & s) != 0) & ((ci & s) == 0)
        left = [_hdot(inv, jnp.where(lower_left, lo, 0.0)) for inv, lo in zip(invs, los)]
        invs = [inv - _hdot(t, inv) for inv, t in zip(invs, left)]
        s *= 2
    return invs


@jax.custom_vjp
def _unit_lower_inverses_diff(los):
    return _unit_lower_inverses(los)


def _unit_lower_inverses_fwd(los):
    invs = _unit_lower_inverses(los)
    return invs, invs


def _unit_lower_inverses_bwd(invs, cts):
    right = [_hdot_nt(ct, inv) for ct, inv in zip(cts, invs)]
    return ([-_hdot_tn(inv, r) for inv, r in zip(invs, right)],)


_unit_lower_inverses_diff.defvjp(_unit_lower_inverses_fwd, _unit_lower_inverses_bwd)


def _gdn_chunk(qs, ks, vs, ba, alog, dtb, states, inverses=_unit_lower_inverses):
    C = qs[0].shape[0]
    heads = range(len(qs))
    ri = lax.broadcasted_iota(jnp.int32, (C, C), 0)
    ci = lax.broadcasted_iota(jnp.int32, (C, C), 1)
    causal = ri >= ci
    strict = ri > ci
    tri = causal.astype(F32)
    betas = [jax.nn.sigmoid(_lane_pick(ba, h)) for h in heads]
    gs = [-jnp.exp(_lane_pick(alog, h)) * _softplus(_lane_pick(ba, h + HEADS) + _lane_pick(dtb, h)) for h in heads]
    qn = [q * lax.rsqrt(jnp.sum(q * q, axis=-1, keepdims=True) + 1e-6) * (HEAD_DIM ** -0.5) for q in qs]
    kn = [k * lax.rsqrt(jnp.sum(k * k, axis=-1, keepdims=True) + 1e-6) for k in ks]
    gc = [_hdot(tri, jnp.broadcast_to(g, (C, C))) for g in gs]
    decay = [jnp.where(causal, jnp.exp(jnp.where(causal, c - c.T, 0.0)), 0.0) for c in gc]
    eg = [jnp.exp(c) for c in gc]
    kb = [k * b for k, b in zip(kn, betas)]
    vb = [v * b for v, b in zip(vs, betas)]
    los = [jnp.where(strict, _hdot_nt(a, k) * d, 0.0) for a, k, d in zip(kb, kn, decay)]
    attn = [jnp.where(causal, _hdot_nt(q, k) * d, 0.0) for q, k, d in zip(qn, kn, decay)]
    tinv = inverses(los)
    u = [_hdot(t, x) for t, x in zip(tinv, vb)]
    w = [_hdot(t, a * e) for t, a, e in zip(tinv, kb, eg)]
    g_last = [c[C - 1:C, :] for c in gc]
    k_tail = [k * jnp.exp(gl - c) for k, gl, c in zip(kn, g_last, gc)]
    v_new = [a - _hdot(b, s) for a, b, s in zip(u, w, states)]
    o_state = [_hdot(q * e, s) for q, e, s in zip(qn, eg, states)]
    o = [a + _hdot(at, vn) for a, at, vn in zip(o_state, attn, v_new)]
    new_states = [s * jnp.exp(gl) + _hdot_tn(kt, vn) for s, gl, kt, vn in zip(states, g_last, k_tail, v_new)]
    return o, new_states


def _gdn_fwd(qkv, proj, alog, dtb):
    S = qkv.shape[0]
    nc = S // CHUNK
    assert CHUNK == HEAD_DIM

    def body(q_ref, k_ref, v_ref, ba_ref, alog_ref, dtb_ref, o_ref, st_ref, state):
        @pl.when(pl.program_id(0) == 0)
        def _():
            state[...] = jnp.zeros_like(state)

        sls = [slice(hd * HEAD_DIM, (hd + 1) * HEAD_DIM) for hd in range(HEADS)]
        s0 = [state[hd] for hd in range(HEADS)]
        for hd in range(HEADS):
            st_ref[hd, 0] = s0[hd]
        o, s1 = _gdn_chunk([q_ref[:, sl] for sl in sls], [k_ref[:, sl] for sl in sls], [v_ref[:, sl] for sl in sls],
                           ba_ref[...], alog_ref[...], dtb_ref[...], s0)
        for hd in range(HEADS):
            o_ref[:, sls[hd]] = o[hd]
            state[hd] = s1[hd]

    def col(j):
        return pl.BlockSpec((CHUNK, GDN_W), lambda n: (n, j))

    vec = _const((1, LANES))
    return pl.pallas_call(
        body, name="gdn_fwd", grid=(nc,),
        in_specs=[col(0), col(1), col(2), pl.BlockSpec((CHUNK, LANES), lambda n: (n, IN_PAD // LANES - 1)), vec, vec],
        out_specs=[col(0), pl.BlockSpec((HEADS, 1, HEAD_DIM, HEAD_DIM), lambda n: (0, n, 0, 0))],
        out_shape=[_sds((S, GDN_W)), _sds((HEADS, nc, HEAD_DIM, HEAD_DIM))],
        scratch_shapes=[pltpu.VMEM((HEADS, HEAD_DIM, HEAD_DIM), F32)],
        compiler_params=_params(("arbitrary",)),
    )(qkv, qkv, qkv, proj, alog, dtb)


def _gdn_bwd(do, qkv, proj, states, alog, dtb):
    S = qkv.shape[0]
    nc = S // CHUNK

    def body(do_ref, q_ref, k_ref, v_ref, ba_ref, st_ref, alog_ref, dtb_ref,
             dqkv_ref, dba_ref, dalog_ref, ddtb_ref, dstate):
        n = pl.program_id(0)

        @pl.when(n == 0)
        def _():
            dstate[...] = jnp.zeros_like(dstate)

        sls = [slice(hd * HEAD_DIM, (hd + 1) * HEAD_DIM) for hd in range(HEADS)]
        fn = functools.partial(_gdn_chunk, inverses=_unit_lower_inverses_diff)
        _, vjp = jax.vjp(fn, [q_ref[:, sl] for sl in sls], [k_ref[:, sl] for sl in sls], [v_ref[:, sl] for sl in sls],
                         ba_ref[...], alog_ref[...], dtb_ref[...], [st_ref[hd, 0] for hd in range(HEADS)])
        dq, dk, dv, dba, dalog, ddtb, ds = vjp(([do_ref[:, sl] for sl in sls], [dstate[hd] for hd in range(HEADS)]))
        for hd in range(HEADS):
            dqkv_ref[:, sls[hd]] = dq[hd]
            dqkv_ref[:, GDN_W + hd * HEAD_DIM:GDN_W + (hd + 1) * HEAD_DIM] = dk[hd]
            dqkv_ref[:, 2 * GDN_W + hd * HEAD_DIM:2 * GDN_W + (hd + 1) * HEAD_DIM] = dv[hd]
            dstate[hd] = ds[hd]
        dba_ref[...] = dba
        _acc(dalog_ref, dalog, n == 0)
        _acc(ddtb_ref, ddtb, n == 0)

    def col(j):
        return pl.BlockSpec((CHUNK, GDN_W), lambda n: (nc - 1 - n, j))

    vec = _const((1, LANES))
    return pl.pallas_call(
        body, name="gdn_bwd", grid=(nc,),
        in_specs=[col(0), col(0), col(1), col(2),
                  pl.BlockSpec((CHUNK, LANES), lambda n: (nc - 1 - n, IN_PAD // LANES - 1)),
                  pl.BlockSpec((HEADS, 1, HEAD_DIM, HEAD_DIM), lambda n: (0, nc - 1 - n, 0, 0)), vec, vec],
        out_specs=[pl.BlockSpec((CHUNK, 3 * GDN_W), lambda n: (nc - 1 - n, 0)),
                   pl.BlockSpec((CHUNK, LANES), lambda n: (nc - 1 - n, 0)), vec, vec],
        out_shape=[_sds((S, 3 * GDN_W)), _sds((S, LANES)), _sds((1, LANES)), _sds((1, LANES))],
        scratch_shapes=[pltpu.VMEM((HEADS, HEAD_DIM, HEAD_DIM), F32)],
        compiler_params=_params(("arbitrary",)),
    )(do, qkv, qkv, qkv, proj, states, alog, dtb)


def _gdn_gate(o, z, gnw):
    outs = []
    for hd in range(HEADS):
        sl = slice(hd * HEAD_DIM, (hd + 1) * HEAD_DIM)
        xh, _ = _rms_parts(o[:, sl])
        outs.append(xh * gnw * _silu(z[:, sl]))
    return jnp.concatenate(outs, axis=-1)


def _out_fwd(x, out_lru, o, proj, gnw, g1, wout):
    S = x.shape[0]
    tm = min(512, S)

    def body(x_ref, lru_ref, o_ref, z_ref, gnw_ref, g1_ref, w_ref, x1_ref, cat_ref):
        cat = jnp.concatenate([lru_ref[...], _gdn_gate(o_ref[...], z_ref[...], gnw_ref[...])], axis=-1).astype(BF16)
        cat_ref[...] = cat
        x1_ref[...] = x_ref[...] + g1_ref[...] * _dot(cat, w_ref[...])

    return pl.pallas_call(
        body, name="out_fwd", grid=(S // tm,),
        in_specs=[_row(tm, D_MODEL), _row(tm, LRU_W), _row(tm, GDN_W), _row(tm, GDN_W, 5), _const((1, LANES)),
                  _const((1, D_MODEL)), _const((D_MODEL, D_MODEL))],
        out_specs=[_row(tm, D_MODEL), _row(tm, D_MODEL)],
        out_shape=[_sds((S, D_MODEL)), _sds((S, D_MODEL), BF16)],
        compiler_params=_params(("arbitrary",)),
    )(x, out_lru, o, proj, gnw, g1, wout)


def _out_bwd(dx1, cat, o, proj, gnw, g1, wout):
    S = dx1.shape[0]
    tm = min(512, S)

    def body(dx1_ref, cat_ref, o_ref, z_ref, gnw_ref, g1_ref, w_ref,
             dlru_ref, do_ref, dz_ref, dmb_ref, dgnw_ref, dg1_ref):
        i = pl.program_id(0)
        d1 = dx1_ref[...]
        mix = _dot(cat_ref[...], w_ref[...])
        _acc(dg1_ref, _colsum(d1 * mix), i == 0)
        dmb = (d1 * g1_ref[...]).astype(BF16)
        dmb_ref[...] = dmb
        dcat = _dot_nt(dmb, w_ref[...])
        dlru_ref[...] = dcat[:, :LRU_W]
        _, vjp = jax.vjp(_gdn_gate, o_ref[...], z_ref[...], gnw_ref[...])
        do, dz, dgnw = vjp(dcat[:, LRU_W:])
        do_ref[...] = do
        dz_ref[...] = dz
        _acc(dgnw_ref, dgnw, i == 0)

    return pl.pallas_call(
        body, name="out_bwd", grid=(S // tm,),
        in_specs=[_row(tm, D_MODEL), _row(tm, D_MODEL), _row(tm, GDN_W), _row(tm, GDN_W, 5), _const((1, LANES)),
                  _const((1, D_MODEL)), _const((D_MODEL, D_MODEL))],
        out_specs=[_row(tm, LRU_W), _row(tm, GDN_W), _row(tm, GDN_W), _row(tm, D_MODEL), _const((1, LANES)),
                   _const((1, D_MODEL))],
        out_shape=[_sds((S, LRU_W)), _sds((S, GDN_W)), _sds((S, GDN_W)), _sds((S, D_MODEL), BF16), _sds((1, LANES)),
                   _sds((1, D_MODEL))],
        compiler_params=_params(("arbitrary",)),
    )(dx1, cat, o, proj, gnw, g1, wout)


MLP_TM = 256


def _load_once(step, pairs, sem):
    @pl.when(step == 0)
    def _():
        copies = [pltpu.make_async_copy(src, dst, sem.at[k]) for k, (src, dst) in enumerate(pairs)]
        for cp in copies:
            cp.start()
        for cp in copies:
            cp.wait()


def _mlp_fwd(x1, nw, sc, sh, g2, wup, wdown):
    S = x1.shape[0]
    tm = min(MLP_TM, S)

    def body(x_ref, nw_ref, sc_ref, sh_ref, g2_ref, wup_hbm, wdown_hbm, x2_ref, wup, wdown, sem):
        _load_once(pl.program_id(0), [(wup_hbm, wup), (wdown_hbm, wdown)], sem)
        x = x_ref[...]
        hb = _norm_mod(x, nw_ref[...], sc_ref[...], sh_ref[...]).astype(BF16)
        r = jnp.maximum(_dot(hb, wup[...]), 0.0)
        x2_ref[...] = x + g2_ref[...] * _dot((r * r).astype(BF16), wdown[...])

    vec = _const((1, D_MODEL))
    anyspec = pl.BlockSpec(memory_space=pl.ANY)
    return pl.pallas_call(
        body, name="mlp_fwd", grid=(S // tm,),
        in_specs=[_row(tm, D_MODEL), vec, vec, vec, vec, anyspec, anyspec],
        out_specs=_row(tm, D_MODEL),
        out_shape=_sds((S, D_MODEL)),
        scratch_shapes=[pltpu.VMEM((D_MODEL, D_FF), BF16), pltpu.VMEM((D_FF, D_MODEL), BF16),
                        pltpu.SemaphoreType.DMA((2,))],
        compiler_params=_params(("arbitrary",)),
    )(x1, nw, sc, sh, g2, wup, wdown)


def _mlp_bwd(dx2, x1, nw, sc, sh, g2, wup, wdown):
    S = x1.shape[0]
    tm = min(MLP_TM, S)

    def body(dx2_ref, x_ref, nw_ref, sc_ref, sh_ref, g2_ref, wup_hbm, wdown_hbm,
             dx1_ref, hb_ref, dupb_ref, actb_ref, ddb_ref, dnw_ref, dsc_ref, dsh_ref, dg2_ref, wup, wdown, sem):
        i = pl.program_id(0)
        _load_once(i, [(wup_hbm, wup), (wdown_hbm, wdown)], sem)
        x = x_ref[...]
        d2 = dx2_ref[...]
        hb = _norm_mod(x, nw_ref[...], sc_ref[...], sh_ref[...]).astype(BF16)
        hb_ref[...] = hb
        r = jnp.maximum(_dot(hb, wup[...]), 0.0)
        actb = (r * r).astype(BF16)
        actb_ref[...] = actb
        down = _dot(actb, wdown[...])
        _acc(dg2_ref, _colsum(d2 * down), i == 0)
        ddb = (d2 * g2_ref[...]).astype(BF16)
        ddb_ref[...] = ddb
        dupb = (_dot_nt(ddb, wdown[...]) * (2.0 * r)).astype(BF16)
        dupb_ref[...] = dupb
        dh = _dot_nt(dupb, wup[...])
        dx, dnw, dsc, dsh = _norm_mod_bwd(dh, x, nw_ref[...], sc_ref[...])
        dx1_ref[...] = d2 + dx
        _acc(dnw_ref, dnw, i == 0)
        _acc(dsc_ref, dsc, i == 0)
        _acc(dsh_ref, dsh, i == 0)

    vec = _const((1, D_MODEL))
    anyspec = pl.BlockSpec(memory_space=pl.ANY)
    return pl.pallas_call(
        body, name="mlp_bwd", grid=(S // tm,),
        in_specs=[_row(tm, D_MODEL), _row(tm, D_MODEL), vec, vec, vec, vec, anyspec, anyspec],
        out_specs=[_row(tm, D_MODEL), _row(tm, D_MODEL), _row(tm, D_FF), _row(tm, D_FF), _row(tm, D_MODEL),
                   vec, vec, vec, vec],
        out_shape=[_sds((S, D_MODEL)), _sds((S, D_MODEL), BF16), _sds((S, D_FF), BF16), _sds((S, D_FF), BF16),
                   _sds((S, D_MODEL), BF16), _sds((1, D_MODEL)), _sds((1, D_MODEL)), _sds((1, D_MODEL)),
                   _sds((1, D_MODEL))],
        scratch_shapes=[pltpu.VMEM((D_MODEL, D_FF), BF16), pltpu.VMEM((D_FF, D_MODEL), BF16),
                        pltpu.SemaphoreType.DMA((2,))],
        compiler_params=_params(("arbitrary",)),
    )(dx2, x1, nw, sc, sh, g2, wup, wdown)


def _matmul_tn(a, b, name):
    K, M = a.shape
    N = b.shape[1]
    tk = min(512, K)
    tm = min(512, M)
    tn = 640 if N % 640 == 0 else min(1024, N)
    nk = K // tk

    def body(a_ref, b_ref, o_ref, acc):
        k = pl.program_id(2)
        _acc(acc, _dot_tn(a_ref[...], b_ref[...]), k == 0)

        @pl.when(k == nk - 1)
        def _():
            o_ref[...] = acc[...]

    return pl.pallas_call(
        body, name=name, grid=(M // tm, N // tn, nk),
        in_specs=[pl.BlockSpec((tk, tm), lambda i, j, k: (k, i)), pl.BlockSpec((tk, tn), lambda i, j, k: (k, j))],
        out_specs=pl.BlockSpec((tm, tn), lambda i, j, k: (i, j)),
        out_shape=_sds((M, N)),
        scratch_shapes=[pltpu.VMEM((tm, tn), F32)],
        compiler_params=_params(("arbitrary", "arbitrary", "arbitrary")),
    )(a, b)


def _loss_head(x, target, fnw):
    S = x.shape[0]
    tm = min(512, S)

    def body(x_ref, t_ref, w_ref, dx_ref, loss_ref, dw_ref):
        i = pl.program_id(0)
        w = w_ref[...]
        xh, r = _rms_parts(x_ref[...])
        err = xh * w - t_ref[...]
        part = 0.5 * jnp.sum(jnp.mean(err * err, axis=-1, keepdims=True), axis=0, keepdims=True)
        _acc(loss_ref, jnp.broadcast_to(part, (SUBLANES, LANES)), i == 0)
        dx, dw = _rms_bwd(err * (1.0 / D_MODEL), xh, r, w)
        dx_ref[...] = dx
        _acc(dw_ref, dw, i == 0)

    vec = _const((1, D_MODEL))
    return pl.pallas_call(
        body, name="loss_head", grid=(S // tm,),
        in_specs=[_row(tm, D_MODEL), _row(tm, D_MODEL), vec],
        out_specs=[_row(tm, D_MODEL), _const((SUBLANES, LANES)), vec],
        out_shape=[_sds((S, D_MODEL)), _sds((SUBLANES, LANES)), _sds((1, D_MODEL))],
        compiler_params=_params(("arbitrary",)),
    )(x, target, fnw)


def _block_diag(w):
    eye = jnp.eye(LRU_BLOCKS, dtype=w.dtype)
    return (eye[:, None, :, None] * w[:, :, None, :]).reshape(LRU_W, LRU_W)


def _diag_blocks(m):
    m4 = m.reshape(LRU_BLOCKS, LRU_BLOCK, LRU_BLOCKS, LRU_BLOCK)
    return jnp.stack([m4[g, :, g, :] for g in range(LRU_BLOCKS)])


def _layer_fwd(x, p):
    proj, h1b = _proj_fwd(x, p["nmw"], p["sc1"], p["sh1"], p["win"])
    xr = _conv_fwd(proj, 0, LRU_W, p["lcw"], p["lcb"], False, "conv_lru_fwd")
    out_lru, h = _lru_fwd(xr, proj, p["wa"].astype(BF16), p["ba"], p["wx"].astype(BF16), p["bx"], p["lam"], p["lnw"])
    qkv = _conv_fwd(proj, 2 * LRU_W, 3 * GDN_W, p["gcw"], p["gcb"], True, "conv_gdn_fwd")
    o, states = _gdn_fwd(qkv, proj, p["alog"], p["dtb"])
    x1, cat = _out_fwd(x, out_lru, o, proj, p["gnw"], p["g1"], p["wout"])
    x2 = _mlp_fwd(x1, p["nmlp"], p["sc2"], p["sh2"], p["g2"], p["wup"], p["wdown"])
    res = dict(x=x, proj=proj, h1b=h1b, xr=xr, h=h, qkv=qkv, o=o, states=states, x1=x1, cat=cat)
    return x2, res


def _layer_bwd(dx2, p, r):
    dx1, h2b, dupb, actb, ddb, dnmlp, dsc2, dsh2, dg2 = _mlp_bwd(
        dx2, r["x1"], p["nmlp"], p["sc2"], p["sh2"], p["g2"], p["wup"], p["wdown"])
    g_wup = _matmul_tn(h2b, dupb, "dw_up")
    g_wdown = _matmul_tn(actb, ddb, "dw_down")
    dlru, do, dz, dmb, dgnw, dg1 = _out_bwd(dx1, r["cat"], r["o"], r["proj"], p["gnw"], p["g1"], p["wout"])
    g_wout = _matmul_tn(r["cat"], dmb, "dw_out")
    dqkv_act, dba, dalog, ddtb = _gdn_bwd(do, r["qkv"], r["proj"], r["states"], p["alog"], p["dtb"])
    dqkv, dgcw, _ = _conv_bwd(r["proj"], 2 * LRU_W, 3 * GDN_W, p["gcw"], p["gcb"], dqkv_act, True, "conv_gdn_bwd")
    wab = p["wa"].astype(BF16)
    wxb = p["wx"].astype(BF16)
    dxr, dly, dwa, dba_, dwx, dbx, dlam, dlnw = _lru_bwd(
        dlru, r["xr"], r["proj"], r["h"], wab, p["ba"], wxb, p["bx"], p["lam"], p["lnw"])
    dlx, dlcw, dlcb = _conv_bwd(r["proj"], 0, LRU_W, p["lcw"], p["lcb"], dxr, False, "conv_lru_bwd")
    dx, dpb, dnmw, dsc1, dsh1 = _proj_bwd(dx1, r["x"], dlx, dly, dqkv, dz, dba, p["nmw"], p["sc1"], p["win"])
    g_win = _matmul_tn(r["h1b"], dpb, "dw_in")
    grads = dict(nmw=dnmw, nmlp=dnmlp, sh1=dsh1, sc1=dsc1, g1=dg1, sh2=dsh2, sc2=dsc2, g2=dg2,
                 win=g_win, lcw=dlcw, lcb=dlcb, wa=dwa, ba=dba_, wx=dwx, bx=dbx, lam=dlam, lnw=dlnw,
                 gcw=dgcw, alog=dalog, dtb=ddtb, gnw=dgnw, wout=g_wout, wup=g_wup, wdown=g_wdown)
    return dx, grads


def _local_step(x, target, fnw, layers):
    res = []
    for p in layers:
        x, r = _layer_fwd(x, p)
        res.append(r)
    dx, loss_blk, dfnw = _loss_head(x, target, fnw)
    grads = [None] * len(layers)
    for l in reversed(range(len(layers))):
        dx, grads[l] = _layer_bwd(dx, layers[l], res[l])
    stacked = {k: jnp.stack([g[k] for g in grads]) for k in grads[0]}
    return loss_blk[0, 0], dx, dfnw, stacked


def _prep_layers(norm_mix_w, norm_mlp_w, mod, win_b, lru_conv_w, lru_conv_b, gate_a_w, gate_a_b, gate_x_w, gate_x_b,
                 lru_lambda, lru_norm_w, gdn_conv_w, gdn_a_log, gdn_dt_bias, gdn_norm_w, wout_b, wup_b, wdown_b):
    L = norm_mix_w.shape[0]

    def vec(a):
        return a.reshape(L, 1, -1)

    def lanes(a):
        return jnp.pad(a, ((0, 0), (0, LANES - a.shape[1]))).reshape(L, 1, LANES)

    def taps(w):
        return jnp.pad(w, ((0, 0), (0, SUBLANES - w.shape[1]), (0, 0)))

    m = mod.reshape(L, N_MOD, 1, D_MODEL)
    return dict(
        nmw=vec(norm_mix_w), nmlp=vec(norm_mlp_w),
        sh1=m[:, 0], sc1=m[:, 1], g1=m[:, 2], sh2=m[:, 3], sc2=m[:, 4], g2=m[:, 5],
        win=win_b, lcw=taps(lru_conv_w), lcb=vec(lru_conv_b),
        wa=jax.vmap(_block_diag)(gate_a_w), ba=vec(gate_a_b), wx=jax.vmap(_block_diag)(gate_x_w), bx=vec(gate_x_b),
        lam=vec(lru_lambda), lnw=vec(lru_norm_w),
        gcw=taps(gdn_conv_w), gcb=jnp.zeros((L, 1, 3 * GDN_W), F32),
        alog=lanes(gdn_a_log), dtb=lanes(gdn_dt_bias), gnw=vec(gdn_norm_w),
        wout=wout_b, wup=wup_b, wdown=wdown_b)


def _position():
    x, y, c = lax.axis_index("x"), lax.axis_index("y"), lax.axis_index("c")
    return x, y, c


def _other_chips(x, y):
    return [(1 - x, y), (x, 1 - y), (1 - x, 1 - y)]


def _all_gather_rows(block, name):
    m, n = block.shape

    def body(x_ref, out_ref, send_sems, recv_sems, local_sem):
        x, y, c = _position()
        me, sibling = (x, y, c), (x, y, 1 - c)
        chips = _other_chips(x, y)

        def rows(px, py, pc):
            return out_ref.at[pl.ds((4 * px + 2 * py + pc) * m, m), :]

        def copy(k, blk, to, src=None):
            return pltpu.make_async_remote_copy(
                src_ref=rows(*blk) if src is None else src, dst_ref=rows(*blk),
                send_sem=send_sems.at[k], recv_sem=recv_sems.at[k], device_id=to, device_id_type=MESH)

        mine = pltpu.make_async_copy(x_ref, rows(*me), local_sem)
        mine.start()
        first = [copy(0, me, sibling, src=x_ref)]
        first += [copy(1 + j, me, (*chip, c), src=x_ref) for j, chip in enumerate(chips)]
        for cp in first:
            cp.start()
        passed = [copy(4 + j, (*chip, c), sibling) for j, chip in enumerate(chips)]
        for j, chip in enumerate(chips):
            copy(1 + j, (*chip, c), me).wait_recv()
            passed[j].start()
        copy(0, sibling, me).wait_recv()
        for j, chip in enumerate(chips):
            copy(4 + j, (*chip, 1 - c), me).wait_recv()
        for cp in first + passed:
            cp.wait_send()
        mine.wait()

    return pl.pallas_call(
        body, name=name,
        out_shape=_sds((N_DEV * m, n)),
        in_specs=[pl.BlockSpec(memory_space=pltpu.VMEM)],
        out_specs=pl.BlockSpec(memory_space=pltpu.VMEM),
        scratch_shapes=[pltpu.SemaphoreType.DMA((7,)), pltpu.SemaphoreType.DMA((7,)), pltpu.SemaphoreType.DMA],
        compiler_params=pltpu.CompilerParams(vmem_limit_bytes=VMEM_LIMIT),
    )(block)


def _hbm_specs(n):
    return [pl.BlockSpec(memory_space=pl.ANY)] * n


def _gather_chips(shards, name):
    n = len(shards)

    def body(*refs):
        ins, outs = refs[:n], refs[n:2 * n]
        send_sems, recv_sems, local_sems = refs[2 * n:]
        x, y, c = _position()
        chips = _other_chips(x, y)
        me = 2 * x + y
        local = [pltpu.make_async_copy(ins[a], outs[a].at[me], local_sems.at[a]) for a in range(n)]
        for cp in local:
            cp.start()

        def copy(a, j, slot):
            px, py = chips[j]
            return pltpu.make_async_remote_copy(
                src_ref=ins[a], dst_ref=outs[a].at[slot], send_sem=send_sems.at[3 * a + j],
                recv_sem=recv_sems.at[3 * a + j], device_id=(px, py, c), device_id_type=MESH)

        sends = [copy(a, j, me) for a in range(n) for j in range(3)]
        for cp in sends:
            cp.start()
        for a in range(n):
            for j, (px, py) in enumerate(chips):
                copy(a, j, 2 * px + py).wait_recv()
        for cp in sends:
            cp.wait_send()
        for cp in local:
            cp.wait()

    return pl.pallas_call(
        body, name=name,
        out_shape=[_sds((N_CHIPS,) + s.shape, s.dtype) for s in shards],
        in_specs=_hbm_specs(n), out_specs=_hbm_specs(n),
        scratch_shapes=[pltpu.SemaphoreType.DMA((3 * n,)), pltpu.SemaphoreType.DMA((3 * n,)),
                        pltpu.SemaphoreType.DMA((n,))],
    )(*shards)


def _send_to_sibling(parts, name):
    n = len(parts)

    def body(*refs):
        ins, outs = refs[:n], refs[n:2 * n]
        send_sems, recv_sems = refs[2 * n:]
        x, y, c = _position()
        copies = [pltpu.make_async_remote_copy(
            src_ref=ins[a].at[1 - c], dst_ref=outs[a], send_sem=send_sems.at[a], recv_sem=recv_sems.at[a],
            device_id=(x, y, 1 - c), device_id_type=MESH) for a in range(n)]
        for cp in copies:
            cp.start()
        for cp in copies:
            cp.wait()

    return pl.pallas_call(
        body, name=name,
        out_shape=[_sds(p.shape[1:], p.dtype) for p in parts],
        in_specs=_hbm_specs(n), out_specs=_hbm_specs(n),
        scratch_shapes=[pltpu.SemaphoreType.DMA((n,)), pltpu.SemaphoreType.DMA((n,))],
    )(*parts)


def _scatter_chips(parts, name):
    n = len(parts)

    def body(*refs):
        ins, outs = refs[:n], refs[n:2 * n]
        send_sems, recv_sems, local_sems = refs[2 * n:]
        x, y, c = _position()
        chips = _other_chips(x, y)
        me = 2 * x + y
        local = [pltpu.make_async_copy(ins[a].at[me], outs[a].at[me], local_sems.at[a]) for a in range(n)]
        for cp in local:
            cp.start()

        def copy(a, j, src_slot, dst_slot):
            px, py = chips[j]
            return pltpu.make_async_remote_copy(
                src_ref=ins[a].at[src_slot], dst_ref=outs[a].at[dst_slot], send_sem=send_sems.at[3 * a + j],
                recv_sem=recv_sems.at[3 * a + j], device_id=(px, py, c), device_id_type=MESH)

        sends = [copy(a, j, 2 * chips[j][0] + chips[j][1], me) for a in range(n) for j in range(3)]
        for cp in sends:
            cp.start()
        for a in range(n):
            for j, (px, py) in enumerate(chips):
                copy(a, j, me, 2 * px + py).wait_recv()
        for cp in sends:
            cp.wait_send()
        for cp in local:
            cp.wait()

    return pl.pallas_call(
        body, name=name,
        out_shape=[_sds(p.shape, p.dtype) for p in parts],
        in_specs=_hbm_specs(n), out_specs=_hbm_specs(n),
        scratch_shapes=[pltpu.SemaphoreType.DMA((3 * n,)), pltpu.SemaphoreType.DMA((3 * n,)),
                        pltpu.SemaphoreType.DMA((n,))],
    )(*parts)


def _share_with_sibling(halves, name):
    n = len(halves)

    def body(*refs):
        ins, outs = refs[:n], refs[n:2 * n]
        send_sems, recv_sems, local_sems = refs[2 * n:]
        x, y, c = _position()
        local = [pltpu.make_async_copy(ins[a], outs[a].at[c], local_sems.at[a]) for a in range(n)]
        for cp in local:
            cp.start()
        sends = [pltpu.make_async_remote_copy(
            src_ref=ins[a], dst_ref=outs[a].at[c], send_sem=send_sems.at[a], recv_sem=recv_sems.at[a],
            device_id=(x, y, 1 - c), device_id_type=MESH) for a in range(n)]
        for cp in sends:
            cp.start()
        for a in range(n):
            pltpu.make_async_remote_copy(
                src_ref=ins[a], dst_ref=outs[a].at[1 - c], send_sem=send_sems.at[a], recv_sem=recv_sems.at[a],
                device_id=(x, y, 1 - c), device_id_type=MESH).wait_recv()
        for cp in sends:
            cp.wait_send()
        for cp in local:
            cp.wait()

    return pl.pallas_call(
        body, name=name,
        out_shape=[_sds((2,) + h.shape, h.dtype) for h in halves],
        in_specs=_hbm_specs(n), out_specs=_hbm_specs(n),
        scratch_shapes=[pltpu.SemaphoreType.DMA((n,)), pltpu.SemaphoreType.DMA((n,)), pltpu.SemaphoreType.DMA((n,))],
    )(*halves)


def _row_tile(rows):
    for t in (512, 256, 128, 64, 32, 16, 8):
        if rows % t == 0:
            return t
    return rows


def _sum_slots(buf, name):
    k, rows, cols = buf.shape
    tm = _row_tile(rows)

    def body(b_ref, o_ref):
        s = b_ref[0]
        for i in range(1, k):
            s = s + b_ref[i]
        o_ref[...] = s

    return pl.pallas_call(
        body, name=name, grid=(rows // tm,),
        in_specs=[pl.BlockSpec((k, tm, cols), lambda i: (0, i, 0))],
        out_specs=pl.BlockSpec((tm, cols), lambda i: (i, 0)),
        out_shape=_sds((rows, cols)),
        compiler_params=_params(("arbitrary",)),
    )(buf)


def _add2(a, b, name):
    rows, cols = a.shape
    tm = _row_tile(rows)

    def body(a_ref, b_ref, o_ref):
        o_ref[...] = a_ref[...] + b_ref[...]

    spec = pl.BlockSpec((tm, cols), lambda i: (i, 0))
    return pl.pallas_call(
        body, name=name, grid=(rows // tm,), in_specs=[spec, spec], out_specs=spec, out_shape=_sds((rows, cols)),
        compiler_params=_params(("arbitrary",)),
    )(a, b)


def _adam_math(w, g, m, v):
    m = ADAM_B1 * m + (1.0 - ADAM_B1) * g
    v = ADAM_B2 * v + (1.0 - ADAM_B2) * jnp.square(g)
    m_hat = m / (1.0 - ADAM_B1 ** ADAM_STEP)
    v_hat = v / (1.0 - ADAM_B2 ** ADAM_STEP)
    delta = -ADAM_LR * (m_hat / (jnp.sqrt(v_hat) + ADAM_EPS) + ADAM_WD * w)
    return delta, m, v


def _adam(w, g, m, v, name):
    rows, cols = w.shape
    tm = _row_tile(rows)

    def body(w_ref, g_ref, m_ref, v_ref, d_ref, nm_ref, nv_ref):
        d, nm, nv = _adam_math(w_ref[...], g_ref[...], m_ref[...], v_ref[...])
        d_ref[...] = d
        nm_ref[...] = nm
        nv_ref[...] = nv

    spec = pl.BlockSpec((tm, cols), lambda i: (i, 0))
    return pl.pallas_call(
        body, name=name, grid=(rows // tm,), in_specs=[spec] * 4, out_specs=[spec] * 3,
        out_shape=[_sds((rows, cols))] * 3, compiler_params=_params(("arbitrary",)),
    )(w, g, m, v)


def _mod_fwd(c_all, w_mod, b_mod_cols):
    L, _, n = w_mod.shape

    def body(c_ref, w_ref, b_ref, o_ref):
        o_ref[0] = _hdot(_silu(c_ref[...]), w_ref[0]) + b_ref[0]

    return pl.pallas_call(
        body, name="mod_fwd", grid=(L,),
        in_specs=[_const((N_DEV, D_MODEL)), pl.BlockSpec((1, D_MODEL, n), lambda l: (l, 0, 0)),
                  pl.BlockSpec((1, 1, n), lambda l: (l, 0, 0))],
        out_specs=pl.BlockSpec((1, N_DEV, n), lambda l: (l, 0, 0)),
        out_shape=_sds((L, N_DEV, n)),
        compiler_params=_params(("arbitrary",)),
    )(c_all, w_mod, b_mod_cols)


def _mod_update(c_all, dmod, w, m, v):
    L, _, n = w.shape
    tn = 512

    def body(c_ref, d_ref, w_ref, m_ref, v_ref, g_ref, dl_ref, nm_ref, nv_ref):
        g = _hdot_tn(_silu(c_ref[...]), d_ref[0])
        g_ref[0] = g
        d, nm, nv = _adam_math(w_ref[0], g, m_ref[0], v_ref[0])
        dl_ref[0] = d
        nm_ref[0] = nm
        nv_ref[0] = nv

    big = pl.BlockSpec((1, D_MODEL, tn), lambda l, j: (l, 0, j))
    return pl.pallas_call(
        body, name="mod_update", grid=(L, n // tn),
        in_specs=[_const((N_DEV, D_MODEL)), pl.BlockSpec((1, N_DEV, tn), lambda l, j: (l, 0, j)), big, big, big],
        out_specs=[big] * 4, out_shape=[_sds(w.shape)] * 4,
        compiler_params=_params(("arbitrary", "arbitrary")),
    )(c_all, dmod, w, m, v)


def _pack_rows(parts, row_multiple):
    flat = jnp.concatenate([p.reshape(-1) for p in parts])
    unit = row_multiple * LANES
    flat = jnp.pad(flat, (0, (-flat.shape[0]) % unit))
    return flat.reshape(-1, LANES)


def _unpack(packed, shapes):
    flat = packed.reshape(-1)
    out, off = [], 0
    for s in shapes:
        n = 1
        for d in s:
            n *= d
        out.append(flat[off:off + n].reshape(s))
        off += n
    return out


def _lane_pad(a):
    return jnp.pad(a, ((0, 0), (0, LANES - a.shape[1])))


WEIGHT_NAMES = ("norm_mix_w", "norm_mlp_w", "w_mod", "b_mod", "w_in", "lru_conv_w", "lru_conv_b", "lru_gate_a_w",
                "lru_gate_a_b", "lru_gate_x_w", "lru_gate_x_b", "lru_lambda", "lru_norm_w", "gdn_conv_w", "gdn_a_log",
                "gdn_dt_bias", "gdn_norm_w", "w_out", "w_up", "w_down", "final_norm_w")


def kernel(x, c, norm_mix_w, norm_mlp_w, w_mod, b_mod, w_in, lru_conv_w, lru_conv_b, lru_gate_a_w, lru_gate_a_b, lru_gate_x_w, lru_gate_x_b, lru_lambda, lru_norm_w, gdn_conv_w, gdn_a_log, gdn_dt_bias, gdn_norm_w, w_out, w_up, w_down, final_norm_w, loss_target, m_norm_mix_w, m_norm_mlp_w, m_w_mod, m_b_mod, m_w_in, m_lru_conv_w, m_lru_conv_b, m_lru_gate_a_w, m_lru_gate_a_b, m_lru_gate_x_w, m_lru_gate_x_b, m_lru_lambda, m_lru_norm_w, m_gdn_conv_w, m_gdn_a_log, m_gdn_dt_bias, m_gdn_norm_w, m_w_out, m_w_up, m_w_down, m_final_norm_w, v_norm_mix_w, v_norm_mlp_w, v_w_mod, v_b_mod, v_w_in, v_lru_conv_w, v_lru_conv_b, v_lru_gate_a_w, v_lru_gate_a_b, v_lru_gate_x_w, v_lru_gate_x_b, v_lru_lambda, v_lru_norm_w, v_gdn_conv_w, v_gdn_a_log, v_gdn_dt_bias, v_gdn_norm_w, v_w_out, v_w_up, v_w_down, v_final_norm_w):
    W = dict(zip(WEIGHT_NAMES, (norm_mix_w, norm_mlp_w, w_mod, b_mod, w_in, lru_conv_w, lru_conv_b, lru_gate_a_w,
                                lru_gate_a_b, lru_gate_x_w, lru_gate_x_b, lru_lambda, lru_norm_w, gdn_conv_w, gdn_a_log,
                                gdn_dt_bias, gdn_norm_w, w_out, w_up, w_down, final_norm_w)))
    M = dict(zip(WEIGHT_NAMES, (m_norm_mix_w, m_norm_mlp_w, m_w_mod, m_b_mod, m_w_in, m_lru_conv_w, m_lru_conv_b,
                                m_lru_gate_a_w, m_lru_gate_a_b, m_lru_gate_x_w, m_lru_gate_x_b, m_lru_lambda,
                                m_lru_norm_w, m_gdn_conv_w, m_gdn_a_log, m_gdn_dt_bias, m_gdn_norm_w, m_w_out, m_w_up,
                                m_w_down, m_final_norm_w)))
    V = dict(zip(WEIGHT_NAMES, (v_norm_mix_w, v_norm_mlp_w, v_w_mod, v_b_mod, v_w_in, v_lru_conv_w, v_lru_conv_b,
                                v_lru_gate_a_w, v_lru_gate_a_b, v_lru_gate_x_w, v_lru_gate_x_b, v_lru_lambda,
                                v_lru_norm_w, v_gdn_conv_w, v_gdn_a_log, v_gdn_dt_bias, v_gdn_norm_w, v_w_out, v_w_up,
                                v_w_down, v_final_norm_w)))
    L = DEPTH
    xi, yi, ci = _position()
    chip = 2 * xi + yi
    dev = 2 * chip + ci
    lcs = LRU_W // N_CHIPS
    gcs = 3 * GDN_W // N_CHIPS
    mcs = N_MOD * D_MODEL // N_CHIPS

    g_in = _all_gather_rows(_pack_rows([c, lru_conv_w, gdn_conv_w], SUBLANES), "gather_small_inputs").reshape(N_DEV, -1)
    c_all = g_in[:, :D_MODEL]
    per_chip = g_in[0::2]
    o1 = D_MODEL + L * 4 * lcs
    lcw_full = per_chip[:, D_MODEL:o1].reshape(N_CHIPS, L, 4, lcs).transpose(1, 2, 0, 3).reshape(L, 4, LRU_W)
    gcw_full = per_chip[:, o1:o1 + L * 4 * gcs].reshape(N_CHIPS, L, 4, gcs).transpose(1, 2, 0, 3).reshape(L, 4, 3 * GDN_W)

    b_cols = lax.dynamic_slice(b_mod, (0, chip * mcs), (L, mcs)).reshape(L, 1, mcs)
    modp = _mod_fwd(c_all, w_mod, b_cols)
    g_mod = _all_gather_rows(modp.reshape(L * N_DEV, mcs), "gather_mod").reshape(N_DEV, L, N_DEV, mcs)
    mod = lax.dynamic_index_in_dim(g_mod[0::2], dev, axis=2, keepdims=False).transpose(1, 0, 2).reshape(L, N_MOD * D_MODEL)

    stacked = _prep_layers(norm_mix_w, norm_mlp_w, mod, None, lcw_full, lru_conv_b, lru_gate_a_w, lru_gate_a_b,
                           lru_gate_x_w, lru_gate_x_b, lru_lambda, lru_norm_w, gcw_full, gdn_a_log, gdn_dt_bias,
                           gdn_norm_w, None, None, None)
    layers = []
    for l in range(L):
        win_g, wout_g, wup_g, wdown_g = _gather_chips(
            [w_in[l].astype(BF16), w_out[l].astype(BF16), w_up[l].astype(BF16), w_down[l].astype(BF16)],
            "gather_weights")
        p = {k: v[l] for k, v in stacked.items() if v is not None}
        p["win"] = jnp.pad(win_g.transpose(1, 0, 2).reshape(D_MODEL, IN_COLS), ((0, 0), (0, IN_PAD - IN_COLS)))
        p["wout"] = wout_g.reshape(D_MODEL, D_MODEL)
        p["wup"] = wup_g.transpose(1, 0, 2).reshape(D_MODEL, D_FF)
        p["wdown"] = wdown_g.reshape(D_FF, D_MODEL)
        layers.append(p)
    loss_local, dx, dfnw, g = _local_step(x[0], loss_target[0], final_norm_w.reshape(1, D_MODEL), layers)
    loss = lax.psum(loss_local, ("x", "y", "c"))

    dmod = jnp.concatenate([g["sh1"], g["sc1"], g["g1"], g["sh2"], g["sc2"], g["g2"]], axis=-1)
    small = [dmod, g["nmw"], g["nmlp"], g["lcw"][:, :4], g["lcb"], jax.vmap(_diag_blocks)(g["wa"]), g["ba"],
             jax.vmap(_diag_blocks)(g["wx"]), g["bx"], g["lam"], g["lnw"], g["gcw"][:, :4], g["alog"], g["dtb"],
             g["gnw"], dfnw]
    small_shapes = [(L, N_MOD * D_MODEL), (L, D_MODEL), (L, D_MODEL), (L, 4, LRU_W), (L, LRU_W),
                    (L, LRU_BLOCKS, LRU_BLOCK, LRU_BLOCK), (L, LRU_W), (L, LRU_BLOCKS, LRU_BLOCK, LRU_BLOCK),
                    (L, LRU_W), (L, LRU_W), (L, LRU_W), (L, 4, 3 * GDN_W), (L, LANES), (L, LANES), (L, LANES),
                    (D_MODEL,)]
    small_names = ["b_mod", "norm_mix_w", "norm_mlp_w", None, "lru_conv_b", "lru_gate_a_w", "lru_gate_a_b",
                   "lru_gate_x_w", "lru_gate_x_b", "lru_lambda", "lru_norm_w", None, "gdn_a_log", "gdn_dt_bias",
                   "gdn_norm_w", "final_norm_w"]
    pack_g = _pack_rows(small, 512)
    rows = pack_g.shape[0]
    all_g = _all_gather_rows(pack_g, "gather_small_grads").reshape(N_DEV, rows, LANES)
    tot = _sum_slots(all_g, "sum_small_grads")
    tot_parts = _unpack(tot, small_shapes)

    def pack_state(S_):
        parts = []
        for nm, shp in zip(small_names, small_shapes):
            if nm is None:
                parts.append(jnp.zeros(shp, F32))
            elif nm in ("gdn_a_log", "gdn_dt_bias"):
                parts.append(_lane_pad(S_[nm]))
            else:
                parts.append(S_[nm])
        return _pack_rows(parts, 512)

    upd = _adam(pack_state(W), tot, pack_state(M), pack_state(V), "adam_small")
    upd_parts = [_unpack(u, small_shapes) for u in upd]

    grads, deltas, new_m, new_v = {}, {}, {}, {}
    for k, nm in enumerate(small_names):
        if nm is None:
            continue
        cut = (lambda a: a[:, :HEADS]) if nm in ("gdn_a_log", "gdn_dt_bias") else (lambda a: a)
        grads[nm] = cut(tot_parts[k])
        deltas[nm], new_m[nm], new_v[nm] = (cut(u[k]) for u in upd_parts)

    g_lcw = lax.dynamic_slice(tot_parts[3], (0, 0, chip * lcs), (L, 4, lcs))
    g_gcw = lax.dynamic_slice(tot_parts[11], (0, 0, chip * gcs), (L, 4, gcs))
    conv_shapes = [(L, 4, lcs), (L, 4, gcs)]
    conv_pack = lambda a, b: _pack_rows([a, b], SUBLANES)
    cu = _adam(conv_pack(lru_conv_w, gdn_conv_w), conv_pack(g_lcw, g_gcw), conv_pack(m_lru_conv_w, m_gdn_conv_w),
               conv_pack(v_lru_conv_w, v_gdn_conv_w), "adam_conv")
    cu_parts = [_unpack(u, conv_shapes) for u in cu]
    for k, nm in enumerate(("lru_conv_w", "gdn_conv_w")):
        grads[nm] = (g_lcw, g_gcw)[k]
        deltas[nm], new_m[nm], new_v[nm] = (u[k] for u in cu_parts)

    dmod_all = all_g[:, :L * N_MOD * D_MODEL // LANES].reshape(N_DEV, L, N_MOD * D_MODEL)
    dmod_cols = lax.dynamic_slice(dmod_all, (0, 0, chip * mcs), (N_DEV, L, mcs)).transpose(1, 0, 2)
    grads["w_mod"], deltas["w_mod"], new_m["w_mod"], new_v["w_mod"] = _mod_update(c_all, dmod_cols, w_mod, m_w_mod, v_w_mod)

    half = L // 2
    gwin = g["win"][:, :, :IN_COLS].reshape(2, half, D_MODEL, N_CHIPS, IN_COLS // N_CHIPS).transpose(0, 3, 1, 2, 4)
    gwout = g["wout"].reshape(2, half, N_CHIPS, D_MODEL // N_CHIPS, D_MODEL).transpose(0, 2, 1, 3, 4)
    gwup = g["wup"].reshape(2, half, D_MODEL, N_CHIPS, D_FF // N_CHIPS).transpose(0, 3, 1, 2, 4)
    gwdown = g["wdown"].reshape(2, half, N_CHIPS, D_FF // N_CHIPS, D_MODEL).transpose(0, 2, 1, 3, 4)
    parts = [gwin, gwout, gwup, gwdown]
    big_names = ["w_in", "w_out", "w_up", "w_down"]
    from_sibling = _send_to_sibling(parts, "pair_send")
    pair = []
    for nm, p, r in zip(big_names, parts, from_sibling):
        own = lax.dynamic_index_in_dim(p, ci, axis=0, keepdims=False)
        cols = own.shape[-1]
        pair.append(_add2(own.reshape(-1, cols), r.reshape(-1, cols), "pair_add_" + nm).reshape(own.shape))
    arrived = _scatter_chips(pair, "chip_scatter")
    halves = []
    for nm, a in zip(big_names, arrived):
        cols = a.shape[-1]
        halves.append(_sum_slots(a.reshape(N_CHIPS, -1, cols), "chip_sum_" + nm).reshape(a.shape[1:]))
    full = _share_with_sibling(halves, "pair_share")
    for nm, f in zip(big_names, full):
        shard = W[nm].shape
        cols = shard[-1]
        gr = f.reshape(-1, cols)
        d_, m_, v_ = _adam(W[nm].reshape(-1, cols), gr, M[nm].reshape(-1, cols), V[nm].reshape(-1, cols), "adam_" + nm)
        grads[nm] = gr.reshape(shard)
        deltas[nm], new_m[nm], new_v[nm] = d_.reshape(shard), m_.reshape(shard), v_.reshape(shard)

    out = [loss, dx[None]]
    for group in (grads, deltas, new_m, new_v):
        out += [group[nm].reshape(W[nm].shape) for nm in WEIGHT_NAMES]
    return tuple(out)
```

```python
import functools

import jax
import jax.numpy as jnp
from jax import lax
from jax.experimental import pallas as pl
from jax.experimental.pallas import tpu as pltpu

F32 = jnp.float32
BF16 = jnp.bfloat16
MESH = pl.DeviceIdType.MESH

D_MODEL = 1024
DEPTH = 4
LRU_W = 512
LRU_BLOCKS = 8
LRU_BLOCK = 64
LRU_C = 8.0
HEADS = 4
HEAD_DIM = 128
GDN_W = 512
CHUNK = 128
D_FF = 4096
N_MOD = 6
IN_COLS = 3080
IN_PAD = 3200
NORM_EPS = 1e-6
LANES = 128
SUBLANES = 8
N_DEV = 8
N_CHIPS = 4

ADAM_LR = 0.001
ADAM_B1 = 0.9
ADAM_B2 = 0.999
ADAM_EPS = 1e-08
ADAM_WD = 0.01
ADAM_STEP = 10

VMEM_LIMIT = 56 * 1024 * 1024
HI = lax.Precision.HIGHEST


def _sds(shape, dtype=F32):
    return jax.ShapeDtypeStruct(tuple(shape), dtype)


def _params(sem=None, vmem=VMEM_LIMIT):
    return pltpu.CompilerParams(dimension_semantics=sem, vmem_limit_bytes=vmem)


def _const(shape):
    return pl.BlockSpec(tuple(shape), lambda *_: (0,) * len(shape))


def _row(tm, c, col=0):
    return pl.BlockSpec((tm, c), lambda i: (i, col))


def _dot(a, b):
    return jnp.dot(a, b, preferred_element_type=F32)


def _dot_nt(a, b):
    return lax.dot_general(a, b, (((1,), (1,)), ((), ())), preferred_element_type=F32)


def _dot_tn(a, b):
    return lax.dot_general(a, b, (((0,), (0,)), ((), ())), preferred_element_type=F32)


def _hdot(a, b):
    return jnp.dot(a, b, preferred_element_type=F32, precision=HI)


def _hdot_nt(a, b):
    return lax.dot_general(a, b, (((1,), (1,)), ((), ())), preferred_element_type=F32, precision=HI)


def _hdot_tn(a, b):
    return lax.dot_general(a, b, (((0,), (0,)), ((), ())), preferred_element_type=F32, precision=HI)


def _acc(ref, val, first):
    @pl.when(first)
    def _():
        ref[...] = val

    @pl.when(jnp.logical_not(first))
    def _():
        ref[...] += val


def _colsum(v):
    return jnp.sum(v, axis=0, keepdims=True)


def _rms_parts(x):
    r = lax.rsqrt(jnp.mean(x * x, axis=-1, keepdims=True) + NORM_EPS)
    return x * r, r


def _rms_bwd(dy, xh, r, w):
    dxh = dy * w
    dw = _colsum(dy * xh)
    dx = r * (dxh - xh * jnp.mean(dxh * xh, axis=-1, keepdims=True))
    return dx, dw


def _norm_mod(x, w, sc, sh):
    xh, _ = _rms_parts(x)
    return (xh * w) * (1.0 + sc) + sh


def _norm_mod_bwd(dy, x, w, sc):
    xh, r = _rms_parts(x)
    n = xh * w
    dsh = _colsum(dy)
    dsc = _colsum(dy * n)
    dx, dw = _rms_bwd(dy * (1.0 + sc), xh, r, w)
    return dx, dw, dsc, dsh


def _softplus(x):
    return jnp.maximum(x, 0.0) + jnp.log1p(jnp.exp(-jnp.abs(x)))


def _silu(x):
    return x * jax.nn.sigmoid(x)


def _silu_grad(x):
    s = jax.nn.sigmoid(x)
    return s * (1.0 + x * (1.0 - s))


def _roll_dn(x, d):
    return x if d == 0 else pltpu.roll(x, d, 0)


def _roll_up(x, d):
    return x if d == 0 else pltpu.roll(x, x.shape[0] - d, 0)


def _proj_fwd(x, nw, sc, sh, win):
    S = x.shape[0]
    tm = min(512, S)

    def body(x_ref, nw_ref, sc_ref, sh_ref, w_ref, proj_ref, hb_ref):
        hb = _norm_mod(x_ref[...], nw_ref[...], sc_ref[...], sh_ref[...]).astype(BF16)
        hb_ref[...] = hb
        proj_ref[...] = _dot(hb, w_ref[...])

    vec = _const((1, D_MODEL))
    return pl.pallas_call(
        body, name="proj_fwd", grid=(S // tm,),
        in_specs=[_row(tm, D_MODEL), vec, vec, vec, _const((D_MODEL, IN_PAD))],
        out_specs=[_row(tm, IN_PAD), _row(tm, D_MODEL)],
        out_shape=[_sds((S, IN_PAD)), _sds((S, D_MODEL), BF16)],
        compiler_params=_params(("arbitrary",)),
    )(x, nw, sc, sh, win)


def _proj_bwd(dx1, x, dlx, dly, dqkv, dz, dba, nw, sc, win):
    S = x.shape[0]
    tm = min(512, S)

    def body(dx1_ref, x_ref, dlx_ref, dly_ref, dqkv_ref, dz_ref, dba_ref, nw_ref, sc_ref, w_ref,
             dx_ref, dpb_ref, dnw_ref, dsc_ref, dsh_ref):
        i = pl.program_id(0)
        dpb = jnp.concatenate([dlx_ref[...], dly_ref[...], dqkv_ref[...], dz_ref[...], dba_ref[...]],
                              axis=-1).astype(BF16)
        dpb_ref[...] = dpb
        dh = _dot_nt(dpb, w_ref[...])
        dx, dnw, dsc, dsh = _norm_mod_bwd(dh, x_ref[...], nw_ref[...], sc_ref[...])
        dx_ref[...] = dx1_ref[...] + dx
        _acc(dnw_ref, dnw, i == 0)
        _acc(dsc_ref, dsc, i == 0)
        _acc(dsh_ref, dsh, i == 0)

    vec = _const((1, D_MODEL))
    return pl.pallas_call(
        body, name="proj_bwd", grid=(S // tm,),
        in_specs=[_row(tm, D_MODEL), _row(tm, D_MODEL), _row(tm, LRU_W), _row(tm, LRU_W), _row(tm, 3 * GDN_W),
                  _row(tm, GDN_W), _row(tm, LANES), vec, vec,
                  _const((D_MODEL, IN_PAD))],
        out_specs=[_row(tm, D_MODEL), _row(tm, IN_PAD), vec, vec, vec],
        out_shape=[_sds((S, D_MODEL)), _sds((S, IN_PAD), BF16), _sds((1, D_MODEL)), _sds((1, D_MODEL)),
                   _sds((1, D_MODEL))],
        compiler_params=_params(("arbitrary",)),
    )(dx1, x, dlx, dly, dqkv, dz, dba, nw, sc, win)


def _conv_taps(xx, w, tm):
    y = _roll_dn(xx, 3)[SUBLANES:] * w[0:1]
    y = y + _roll_dn(xx, 2)[SUBLANES:] * w[1:2]
    y = y + _roll_dn(xx, 1)[SUBLANES:] * w[2:3]
    y = y + xx[SUBLANES:] * w[3:4]
    return y


def _conv_fwd(src, col0, C, w8, b, act, name):
    S = src.shape[0]
    tm = min(512, S)
    tc = 512
    hb = tm // SUBLANES
    cb0 = col0 // tc

    def body(x_ref, p_ref, w_ref, b_ref, y_ref):
        i = pl.program_id(0)
        prev = jnp.where(i > 0, p_ref[...], 0.0)
        xx = jnp.concatenate([prev, x_ref[...]], axis=0)
        y = _conv_taps(xx, w_ref[...], tm) + b_ref[...]
        y_ref[...] = _silu(y) if act else y

    return pl.pallas_call(
        body, name=name, grid=(S // tm, C // tc),
        in_specs=[pl.BlockSpec((tm, tc), lambda i, j: (i, cb0 + j)),
                  pl.BlockSpec((SUBLANES, tc), lambda i, j: (jnp.maximum(i * hb - 1, 0), cb0 + j)),
                  pl.BlockSpec((SUBLANES, tc), lambda i, j: (0, j)),
                  pl.BlockSpec((1, tc), lambda i, j: (0, j))],
        out_specs=pl.BlockSpec((tm, tc), lambda i, j: (i, j)),
        out_shape=_sds((S, C)),
        compiler_params=_params(("arbitrary", "arbitrary")),
    )(src, src, w8, b)


def _conv_bwd(src, col0, C, w8, b, dyact, act, name):
    S = src.shape[0]
    tm = min(512, S)
    tc = 512
    hb = tm // SUBLANES
    nt = S // tm
    cb0 = col0 // tc
    last_hb = S // SUBLANES - 1

    def body(x_ref, p_ref, n_ref, dy_ref, dyn_ref, w_ref, b_ref, dx_ref, dw_ref, db_ref):
        i = pl.program_id(1)
        w = w_ref[...]
        prev = jnp.where(i > 0, p_ref[...], 0.0)
        xx = jnp.concatenate([prev, x_ref[...], n_ref[...]], axis=0)
        dy = jnp.concatenate([dy_ref[...], jnp.where(i < nt - 1, dyn_ref[...], 0.0)], axis=0)
        if act:
            ypre = _conv_taps(xx, w, tm + SUBLANES) + b_ref[...]
            dy = dy * _silu_grad(ypre)
        dx = dy[:tm] * w[3:4]
        for d in (1, 2, 3):
            dx = dx + _roll_up(dy, d)[:tm] * w[3 - d:4 - d]
        dx_ref[...] = dx
        xt = xx[:tm + SUBLANES]
        dyt = dy[:tm]
        rows = [_colsum(dyt * _roll_dn(xt, 3 - k)[SUBLANES:]) for k in range(4)]
        dw = jnp.concatenate(rows + [jnp.zeros((SUBLANES - 4, tc), F32)], axis=0)
        _acc(dw_ref, dw, i == 0)
        _acc(db_ref, _colsum(dyt), i == 0)

    return pl.pallas_call(
        body, name=name, grid=(C // tc, nt),
        in_specs=[pl.BlockSpec((tm, tc), lambda j, i: (i, cb0 + j)),
                  pl.BlockSpec((SUBLANES, tc), lambda j, i: (jnp.maximum(i * hb - 1, 0), cb0 + j)),
                  pl.BlockSpec((SUBLANES, tc), lambda j, i: (jnp.minimum((i + 1) * hb, last_hb), cb0 + j)),
                  pl.BlockSpec((tm, tc), lambda j, i: (i, j)),
                  pl.BlockSpec((SUBLANES, tc), lambda j, i: (jnp.minimum((i + 1) * hb, last_hb), j)),
                  pl.BlockSpec((SUBLANES, tc), lambda j, i: (0, j)),
                  pl.BlockSpec((1, tc), lambda j, i: (0, j))],
        out_specs=[pl.BlockSpec((tm, tc), lambda j, i: (i, j)),
                   pl.BlockSpec((SUBLANES, tc), lambda j, i: (0, j)),
                   pl.BlockSpec((1, tc), lambda j, i: (0, j))],
        out_shape=[_sds((S, C)), _sds((SUBLANES, C)), _sds((1, C))],
        compiler_params=_params(("arbitrary", "arbitrary")),
    )(src, src, src, dyact, dyact, w8, b)


def _lru_ab(pre_a, pre_x, xr, lam):
    r = jax.nn.sigmoid(pre_a)
    g = jax.nn.sigmoid(pre_x)
    log_sig = -_softplus(-lam)
    log_a = LRU_C * r * log_sig
    a = jnp.exp(log_a)
    t = jnp.tanh(log_a)
    mult = jnp.sqrt(jnp.maximum(-2.0 * t / (1.0 - t), 1e-12))
    return a, mult * (g * xr)


def _lru_tail(h, ly, lnw):
    xh, _ = _rms_parts(h * jax.nn.gelu(ly))
    return xh * lnw


def _scan_down(a, b):
    n = a.shape[0]
    row = lax.broadcasted_iota(jnp.int32, a.shape, 0)
    d = 1
    while d < n:
        keep = row >= d
        a_s = jnp.where(keep, _roll_dn(a, d), 1.0)
        b_s = jnp.where(keep, _roll_dn(b, d), 0.0)
        b = a * b_s + b
        a = a * a_s
        d *= 2
    return a, b


def _scan_up(a, b):
    n = a.shape[0]
    row = lax.broadcasted_iota(jnp.int32, a.shape, 0)
    d = 1
    while d < n:
        keep = row < n - d
        a_s = jnp.where(keep, _roll_up(a, d), 1.0)
        b_s = jnp.where(keep, _roll_up(b, d), 0.0)
        b = a * b_s + b
        a = a * a_s
        d *= 2
    return a, b


LRU_TM = 256


def _lru_fwd(xr, proj, wa, ba, wx, bx, lam, lnw):
    S = xr.shape[0]
    tm = min(LRU_TM, S)

    def body(xr_ref, ly_ref, wa_ref, ba_ref, wx_ref, bx_ref, lam_ref, lnw_ref, out_ref, h_ref, carry):
        i = pl.program_id(0)

        @pl.when(i == 0)
        def _():
            carry[...] = jnp.zeros_like(carry)

        x = xr_ref[...]
        xb = x.astype(BF16)
        pre_a = _dot(xb, wa_ref[...]) + ba_ref[...]
        pre_x = _dot(xb, wx_ref[...]) + bx_ref[...]
        a, b = _lru_ab(pre_a, pre_x, x, lam_ref[...])
        ca, hl = _scan_down(a, b)
        h = hl + ca * carry[0:1, :]
        carry[0:1, :] = h[tm - 1:tm, :]
        h_ref[...] = h
        out_ref[...] = _lru_tail(h, ly_ref[...], lnw_ref[...])

    vec = _const((1, LRU_W))
    mat = _const((LRU_W, LRU_W))
    return pl.pallas_call(
        body, name="lru_fwd", grid=(S // tm,),
        in_specs=[_row(tm, LRU_W), _row(tm, LRU_W, 1), mat, vec, mat, vec, vec, vec],
        out_specs=[_row(tm, LRU_W), _row(tm, LRU_W)],
        out_shape=[_sds((S, LRU_W)), _sds((S, LRU_W))],
        scratch_shapes=[pltpu.VMEM((SUBLANES, LRU_W), F32)],
        compiler_params=_params(("arbitrary",)),
    )(xr, proj, wa, ba, wx, bx, lam, lnw)


def _lru_bwd(dout, xr, proj, h, wa, ba, wx, bx, lam, lnw):
    S = xr.shape[0]
    tm = min(LRU_TM, S)
    nt = S // tm
    hb = tm // SUBLANES

    def rev(col=0):
        return pl.BlockSpec((tm, LRU_W), lambda i: (nt - 1 - i, col))

    def body(dout_ref, xr_ref, ly_ref, h_ref, hp_ref, wa_ref, ba_ref, wx_ref, bx_ref, lam_ref, lnw_ref,
             dxr_ref, dly_ref, dwa_ref, dba_ref, dwx_ref, dbx_ref, dlam_ref, dlnw_ref, carry):
        i = pl.program_id(0)
        first = i == 0

        @pl.when(first)
        def _():
            carry[...] = jnp.zeros_like(carry)

        x = xr_ref[...]
        xb = x.astype(BF16)
        pre_a = _dot(xb, wa_ref[...]) + ba_ref[...]
        pre_x = _dot(xb, wx_ref[...]) + bx_ref[...]
        (a, b), ab_vjp = jax.vjp(_lru_ab, pre_a, pre_x, x, lam_ref[...])
        h_t = h_ref[...]
        _, tail_vjp = jax.vjp(_lru_tail, h_t, ly_ref[...], lnw_ref[...])
        dh, dly, dlnw = tail_vjp(dout_ref[...])
        dly_ref[...] = dly
        row = lax.broadcasted_iota(jnp.int32, a.shape, 0)
        a_next = jnp.where(row == tm - 1, carry[0:1, :], _roll_up(a, 1))
        ca, gl = _scan_up(a_next, dh)
        g = gl + ca * carry[1:2, :]
        carry[0:1, :] = a[0:1, :]
        carry[1:2, :] = g[0:1, :]
        h_before = jnp.where(i == nt - 1, 0.0, hp_ref[SUBLANES - 1:SUBLANES, :])
        h_prev = jnp.where(row == 0, h_before, _roll_dn(h_t, 1))
        dpa, dpx, dx, dlam = ab_vjp((g * h_prev, g))
        dpab = dpa.astype(BF16)
        dpxb = dpx.astype(BF16)
        dxr_ref[...] = dx + _dot_nt(dpab, wa_ref[...]) + _dot_nt(dpxb, wx_ref[...])
        _acc(dwa_ref, _dot_tn(xb, dpab), first)
        _acc(dwx_ref, _dot_tn(xb, dpxb), first)
        _acc(dba_ref, _colsum(dpa), first)
        _acc(dbx_ref, _colsum(dpx), first)
        _acc(dlam_ref, dlam, first)
        _acc(dlnw_ref, dlnw, first)

    vec = _const((1, LRU_W))
    mat = _const((LRU_W, LRU_W))
    return pl.pallas_call(
        body, name="lru_bwd", grid=(nt,),
        in_specs=[rev(), rev(), rev(1), rev(),
                  pl.BlockSpec((SUBLANES, LRU_W), lambda i: (jnp.maximum((nt - 1 - i) * hb - 1, 0), 0)),
                  mat, vec, mat, vec, vec, vec],
        out_specs=[rev(), rev(), mat, vec, mat, vec, vec, vec],
        out_shape=[_sds((S, LRU_W)), _sds((S, LRU_W)), _sds((LRU_W, LRU_W)), _sds((1, LRU_W)),
                   _sds((LRU_W, LRU_W)), _sds((1, LRU_W)), _sds((1, LRU_W)), _sds((1, LRU_W))],
        scratch_shapes=[pltpu.VMEM((SUBLANES, LRU_W), F32)],
        compiler_params=_params(("arbitrary",)),
    )(dout, xr, proj, h, h, wa, ba, wx, bx, lam, lnw)


def _lane_pick(row_or_tile, lane):
    idx = lax.broadcasted_iota(jnp.int32, row_or_tile.shape, 1)
    return jnp.sum(jnp.where(idx == lane, row_or_tile, 0.0), axis=-1, keepdims=True)


def _unit_lower_inverses(los):
    n = los[0].shape[0]
    ri = lax.broadcasted_iota(jnp.int32, (n, n), 0)
    ci = lax.broadcasted_iota(jnp.int32, (n, n), 1)
    invs = [(ri == ci).astype(F32) for _ in los]
    s = 1
    while s < n:
        same_block = (ri & ~(2 * s - 1)) == (ci & ~(2 * s - 1))
        lower_left = same_block & ((ri & s) != 0) & ((ci & s) == 0)
        left = [_hdot(inv, jnp.where(lower_left, lo, 0.0)) for inv, lo in zip(invs, los)]
        invs = [inv - _hdot(t, inv) for inv, t in zip(invs, left)]
        s *= 2
    return invs


@jax.custom_vjp
def _unit_lower_inverses_diff(los):
    return _unit_lower_inverses(los)


def _unit_lower_inverses_fwd(los):
    invs = _unit_lower_inverses(los)
    return invs, invs


def _unit_lower_inverses_bwd(invs, cts):
    right = [_hdot_nt(ct, inv) for ct, inv in zip(cts, invs)]
    return ([-_hdot_tn(inv, r) for inv, r in zip(invs, right)],)


_unit_lower_inverses_diff.defvjp(_unit_lower_inverses_fwd, _unit_lower_inverses_bwd)


def _gdn_chunk(qs, ks, vs, ba, alog, dtb, states, inverses=_unit_lower_inverses):
    C = qs[0].shape[0]
    heads = range(len(qs))
    ri = lax.broadcasted_iota(jnp.int32, (C, C), 0)
    ci = lax.broadcasted_iota(jnp.int32, (C, C), 1)
    causal = ri >= ci
    strict = ri > ci
    tri = causal.astype(F32)
    betas = [jax.nn.sigmoid(_lane_pick(ba, h)) for h in heads]
    gs = [-jnp.exp(_lane_pick(alog, h)) * _softplus(_lane_pick(ba, h + HEADS) + _lane_pick(dtb, h)) for h in heads]
    qn = [q * lax.rsqrt(jnp.sum(q * q, axis=-1, keepdims=True) + 1e-6) * (HEAD_DIM ** -0.5) for q in qs]
    kn = [k * lax.rsqrt(jnp.sum(k * k, axis=-1, keepdims=True) + 1e-6) for k in ks]
    gc = [_hdot(tri, jnp.broadcast_to(g, (C, C))) for g in gs]
    decay = [jnp.where(causal, jnp.exp(jnp.where(causal, c - c.T, 0.0)), 0.0) for c in gc]
    eg = [jnp.exp(c) for c in gc]
    kb = [k * b for k, b in zip(kn, betas)]
    vb = [v * b for v, b in zip(vs, betas)]
    los = [jnp.where(strict, _hdot_nt(a, k) * d, 0.0) for a, k, d in zip(kb, kn, decay)]
    attn = [jnp.where(causal, _hdot_nt(q, k) * d, 0.0) for q, k, d in zip(qn, kn, decay)]
    tinv = inverses(los)
    u = [_hdot(t, x) for t, x in zip(tinv, vb)]
    w = [_hdot(t, a * e) for t, a, e in zip(tinv, kb, eg)]
    g_last = [c[C - 1:C, :] for c in gc]
    k_tail = [k * jnp.exp(gl - c) for k, gl, c in zip(kn, g_last, gc)]
    v_new = [a - _hdot(b, s) for a, b, s in zip(u, w, states)]
    o_state = [_hdot(q * e, s) for q, e, s in zip(qn, eg, states)]
    o = [a + _hdot(at, vn) for a, at, vn in zip(o_state, attn, v_new)]
    new_states = [s * jnp.exp(gl) + _hdot_tn(kt, vn) for s, gl, kt, vn in zip(states, g_last, k_tail, v_new)]
    return o, new_states


def _gdn_fwd(qkv, proj, alog, dtb):
    S = qkv.shape[0]
    nc = S // CHUNK
    assert CHUNK == HEAD_DIM

    def body(q_ref, k_ref, v_ref, ba_ref, alog_ref, dtb_ref, o_ref, st_ref, state):
        @pl.when(pl.program_id(0) == 0)
        def _():
            state[...] = jnp.zeros_like(state)

        sls = [slice(hd * HEAD_DIM, (hd + 1) * HEAD_DIM) for hd in range(HEADS)]
        s0 = [state[hd] for hd in range(HEADS)]
        for hd in range(HEADS):
            st_ref[hd, 0] = s0[hd]
        o, s1 = _gdn_chunk([q_ref[:, sl] for sl in sls], [k_ref[:, sl] for sl in sls], [v_ref[:, sl] for sl in sls],
                           ba_ref[...], alog_ref[...], dtb_ref[...], s0)
        for hd in range(HEADS):
            o_ref[:, sls[hd]] = o[hd]
            state[hd] = s1[hd]

    def col(j):
        return pl.BlockSpec((CHUNK, GDN_W), lambda n: (n, j))

    vec = _const((1, LANES))
    return pl.pallas_call(
        body, name="gdn_fwd", grid=(nc,),
        in_specs=[col(0), col(1), col(2), pl.BlockSpec((CHUNK, LANES), lambda n: (n, IN_PAD // LANES - 1)), vec, vec],
        out_specs=[col(0), pl.BlockSpec((HEADS, 1, HEAD_DIM, HEAD_DIM), lambda n: (0, n, 0, 0))],
        out_shape=[_sds((S, GDN_W)), _sds((HEADS, nc, HEAD_DIM, HEAD_DIM))],
        scratch_shapes=[pltpu.VMEM((HEADS, HEAD_DIM, HEAD_DIM), F32)],
        compiler_params=_params(("arbitrary",)),
    )(qkv, qkv, qkv, proj, alog, dtb)


def _gdn_bwd(do, qkv, proj, states, alog, dtb):
    S = qkv.shape[0]
    nc = S // CHUNK

    def body(do_ref, q_ref, k_ref, v_ref, ba_ref, st_ref, alog_ref, dtb_ref,
             dqkv_ref, dba_ref, dalog_ref, ddtb_ref, dstate):
        n = pl.program_id(0)

        @pl.when(n == 0)
        def _():
            dstate[...] = jnp.zeros_like(dstate)

        sls = [slice(hd * HEAD_DIM, (hd + 1) * HEAD_DIM) for hd in range(HEADS)]
        fn = functools.partial(_gdn_chunk, inverses=_unit_lower_inverses_diff)
        _, vjp = jax.vjp(fn, [q_ref[:, sl] for sl in sls], [k_ref[:, sl] for sl in sls], [v_ref[:, sl] for sl in sls],
                         ba_ref[...], alog_ref[...], dtb_ref[...], [st_ref[hd, 0] for hd in range(HEADS)])
        dq, dk, dv, dba, dalog, ddtb, ds = vjp(([do_ref[:, sl] for sl in sls], [dstate[hd] for hd in range(HEADS)]))
        for hd in range(HEADS):
            dqkv_ref[:, sls[hd]] = dq[hd]
            dqkv_ref[:, GDN_W + hd * HEAD_DIM:GDN_W + (hd + 1) * HEAD_DIM] = dk[hd]
            dqkv_ref[:, 2 * GDN_W + hd * HEAD_DIM:2 * GDN_W + (hd + 1) * HEAD_DIM] = dv[hd]
            dstate[hd] = ds[hd]
        dba_ref[...] = dba
        _acc(dalog_ref, dalog, n == 0)
        _acc(ddtb_ref, ddtb, n == 0)

    def col(j):
        return pl.BlockSpec((CHUNK, GDN_W), lambda n: (nc - 1 - n, j))

    vec = _const((1, LANES))
    return pl.pallas_call(
        body, name="gdn_bwd", grid=(nc,),
        in_specs=[col(0), col(0), col(1), col(2),
                  pl.BlockSpec((CHUNK, LANES), lambda n: (nc - 1 - n, IN_PAD // LANES - 1)),
                  pl.BlockSpec((HEADS, 1, HEAD_DIM, HEAD_DIM), lambda n: (0, nc - 1 - n, 0, 0)), vec, vec],
        out_specs=[pl.BlockSpec((CHUNK, 3 * GDN_W), lambda n: (nc - 1 - n, 0)),
                   pl.BlockSpec((CHUNK, LANES), lambda n: (nc - 1 - n, 0)), vec, vec],
        out_shape=[_sds((S, 3 * GDN_W)), _sds((S, LANES)), _sds((1, LANES)), _sds((1, LANES))],
        scratch_shapes=[pltpu.VMEM((HEADS, HEAD_DIM, HEAD_DIM), F32)],
        compiler_params=_params(("arbitrary",)),
    )(do, qkv, qkv, qkv, proj, states, alog, dtb)


def _gdn_gate(o, z, gnw):
    outs = []
    for hd in range(HEADS):
        sl = slice(hd * HEAD_DIM, (hd + 1) * HEAD_DIM)
        xh, _ = _rms_parts(o[:, sl])
        outs.append(xh * gnw * _silu(z[:, sl]))
    return jnp.concatenate(outs, axis=-1)


def _out_fwd(x, out_lru, o, proj, gnw, g1, wout):
    S = x.shape[0]
    tm = min(512, S)

    def body(x_ref, lru_ref, o_ref, z_ref, gnw_ref, g1_ref, w_ref, x1_ref, cat_ref):
        cat = jnp.concatenate([lru_ref[...], _gdn_gate(o_ref[...], z_ref[...], gnw_ref[...])], axis=-1).astype(BF16)
        cat_ref[...] = cat
        x1_ref[...] = x_ref[...] + g1_ref[...] * _dot(cat, w_ref[...])

    return pl.pallas_call(
        body, name="out_fwd", grid=(S // tm,),
        in_specs=[_row(tm, D_MODEL), _row(tm, LRU_W), _row(tm, GDN_W), _row(tm, GDN_W, 5), _const((1, LANES)),
                  _const((1, D_MODEL)), _const((D_MODEL, D_MODEL))],
        out_specs=[_row(tm, D_MODEL), _row(tm, D_MODEL)],
        out_shape=[_sds((S, D_MODEL)), _sds((S, D_MODEL), BF16)],
        compiler_params=_params(("arbitrary",)),
    )(x, out_lru, o, proj, gnw, g1, wout)


def _out_bwd(dx1, cat, o, proj, gnw, g1, wout):
    S = dx1.shape[0]
    tm = min(512, S)

    def body(dx1_ref, cat_ref, o_ref, z_ref, gnw_ref, g1_ref, w_ref,
             dlru_ref, do_ref, dz_ref, dmb_ref, dgnw_ref, dg1_ref):
        i = pl.program_id(0)
        d1 = dx1_ref[...]
        mix = _dot(cat_ref[...], w_ref[...])
        _acc(dg1_ref, _colsum(d1 * mix), i == 0)
        dmb = (d1 * g1_ref[...]).astype(BF16)
        dmb_ref[...] = dmb
        dcat = _dot_nt(dmb, w_ref[...])
        dlru_ref[...] = dcat[:, :LRU_W]
        _, vjp = jax.vjp(_gdn_gate, o_ref[...], z_ref[...], gnw_ref[...])
        do, dz, dgnw = vjp(dcat[:, LRU_W:])
        do_ref[...] = do
        dz_ref[...] = dz
        _acc(dgnw_ref, dgnw, i == 0)

    return pl.pallas_call(
        body, name="out_bwd", grid=(S // tm,),
        in_specs=[_row(tm, D_MODEL), _row(tm, D_MODEL), _row(tm, GDN_W), _row(tm, GDN_W, 5), _const((1, LANES)),
                  _const((1, D_MODEL)), _const((D_MODEL, D_MODEL))],
        out_specs=[_row(tm, LRU_W), _row(tm, GDN_W), _row(tm, GDN_W), _row(tm, D_MODEL), _const((1, LANES)),
                   _const((1, D_MODEL))],
        out_shape=[_sds((S, LRU_W)), _sds((S, GDN_W)), _sds((S, GDN_W)), _sds((S, D_MODEL), BF16), _sds((1, LANES)),
                   _sds((1, D_MODEL))],
        compiler_params=_params(("arbitrary",)),
    )(dx1, cat, o, proj, gnw, g1, wout)


MLP_TM = 256


def _load_once(step, pairs, sem):
    @pl.when(step == 0)
    def _():
        copies = [pltpu.make_async_copy(src, dst, sem.at[k]) for k, (src, dst) in enumerate(pairs)]
        for cp in copies:
            cp.start()
        for cp in copies:
            cp.wait()


def _mlp_fwd(x1, nw, sc, sh, g2, wup, wdown):
    S = x1.shape[0]
    tm = min(MLP_TM, S)

    def body(x_ref, nw_ref, sc_ref, sh_ref, g2_ref, wup_hbm, wdown_hbm, x2_ref, wup, wdown, sem):
        _load_once(pl.program_id(0), [(wup_hbm, wup), (wdown_hbm, wdown)], sem)
        x = x_ref[...]
        hb = _norm_mod(x, nw_ref[...], sc_ref[...], sh_ref[...]).astype(BF16)
        r = jnp.maximum(_dot(hb, wup[...]), 0.0)
        x2_ref[...] = x + g2_ref[...] * _dot((r * r).astype(BF16), wdown[...])

    vec = _const((1, D_MODEL))
    anyspec = pl.BlockSpec(memory_space=pl.ANY)
    return pl.pallas_call(
        body, name="mlp_fwd", grid=(S // tm,),
        in_specs=[_row(tm, D_MODEL), vec, vec, vec, vec, anyspec, anyspec],
        out_specs=_row(tm, D_MODEL),
        out_shape=_sds((S, D_MODEL)),
        scratch_shapes=[pltpu.VMEM((D_MODEL, D_FF), BF16), pltpu.VMEM((D_FF, D_MODEL), BF16),
                        pltpu.SemaphoreType.DMA((2,))],
        compiler_params=_params(("arbitrary",)),
    )(x1, nw, sc, sh, g2, wup, wdown)


def _mlp_bwd(dx2, x1, nw, sc, sh, g2, wup, wdown):
    S = x1.shape[0]
    tm = min(MLP_TM, S)

    def body(dx2_ref, x_ref, nw_ref, sc_ref, sh_ref, g2_ref, wup_hbm, wdown_hbm,
             dx1_ref, hb_ref, dupb_ref, actb_ref, ddb_ref, dnw_ref, dsc_ref, dsh_ref, dg2_ref, wup, wdown, sem):
        i = pl.program_id(0)
        _load_once(i, [(wup_hbm, wup), (wdown_hbm, wdown)], sem)
        x = x_ref[...]
        d2 = dx2_ref[...]
        hb = _norm_mod(x, nw_ref[...], sc_ref[...], sh_ref[...]).astype(BF16)
        hb_ref[...] = hb
        r = jnp.maximum(_dot(hb, wup[...]), 0.0)
        actb = (r * r).astype(BF16)
        actb_ref[...] = actb
        down = _dot(actb, wdown[...])
        _acc(dg2_ref, _colsum(d2 * down), i == 0)
        ddb = (d2 * g2_ref[...]).astype(BF16)
        ddb_ref[...] = ddb
        dupb = (_dot_nt(ddb, wdown[...]) * (2.0 * r)).astype(BF16)
        dupb_ref[...] = dupb
        dh = _dot_nt(dupb, wup[...])
        dx, dnw, dsc, dsh = _norm_mod_bwd(dh, x, nw_ref[...], sc_ref[...])
        dx1_ref[...] = d2 + dx
        _acc(dnw_ref, dnw, i == 0)
        _acc(dsc_ref, dsc, i == 0)
        _acc(dsh_ref, dsh, i == 0)

    vec = _const((1, D_MODEL))
    anyspec = pl.BlockSpec(memory_space=pl.ANY)
    return pl.pallas_call(
        body, name="mlp_bwd", grid=(S // tm,),
        in_specs=[_row(tm, D_MODEL), _row(tm, D_MODEL), vec, vec, vec, vec, anyspec, anyspec],
        out_specs=[_row(tm, D_MODEL), _row(tm, D_MODEL), _row(tm, D_FF), _row(tm, D_FF), _row(tm, D_MODEL),
                   vec, vec, vec, vec],
        out_shape=[_sds((S, D_MODEL)), _sds((S, D_MODEL), BF16), _sds((S, D_FF), BF16), _sds((S, D_FF), BF16),
                   _sds((S, D_MODEL), BF16), _sds((1, D_MODEL)), _sds((1, D_MODEL)), _sds((1, D_MODEL)),
                   _sds((1, D_MODEL))],
        scratch_shapes=[pltpu.VMEM((D_MODEL, D_FF), BF16), pltpu.VMEM((D_FF, D_MODEL), BF16),
                        pltpu.SemaphoreType.DMA((2,))],
        compiler_params=_params(("arbitrary",)),
    )(dx2, x1, nw, sc, sh, g2, wup, wdown)


def _matmul_tn(a, b, name, shards=None):
    K, M = a.shape
    N = b.shape[1]
    tk = min(512, K)
    if shards == "cols":
        tm, tn = M // 2, N // N_CHIPS
        out_spec = pl.BlockSpec((1, 1, tm, tn), lambda i, j, k: (i, j, 0, 0))
        out_shape = _sds((2, N_CHIPS, tm, tn))
    elif shards == "rows":
        tm, tn = M // (2 * N_CHIPS), N
        out_spec = pl.BlockSpec((1, 1, tm, tn), lambda i, j, k: (i % 2, i // 2, 0, 0))
        out_shape = _sds((2, N_CHIPS, tm, tn))
    else:
        tm = min(512, M)
        tn = 640 if N % 640 == 0 else min(1024, N)
        out_spec = pl.BlockSpec((tm, tn), lambda i, j, k: (i, j))
        out_shape = _sds((M, N))
    nk = K // tk

    def body(a_ref, b_ref, o_ref, acc):
        k = pl.program_id(2)
        _acc(acc, _dot_tn(a_ref[...], b_ref[...]), k == 0)

        @pl.when(k == nk - 1)
        def _():
            o_ref[...] = acc[...].reshape(o_ref.shape)

    return pl.pallas_call(
        body, name=name, grid=(M // tm, N // tn, nk),
        in_specs=[pl.BlockSpec((tk, tm), lambda i, j, k: (k, i)), pl.BlockSpec((tk, tn), lambda i, j, k: (k, j))],
        out_specs=out_spec, out_shape=out_shape,
        scratch_shapes=[pltpu.VMEM((tm, tn), F32)],
        compiler_params=_params(("arbitrary", "arbitrary", "arbitrary")),
    )(a, b)


def _loss_head(x, target, fnw):
    S = x.shape[0]
    tm = min(512, S)

    def body(x_ref, t_ref, w_ref, dx_ref, loss_ref, dw_ref):
        i = pl.program_id(0)
        w = w_ref[...]
        xh, r = _rms_parts(x_ref[...])
        err = xh * w - t_ref[...]
        part = 0.5 * jnp.sum(jnp.mean(err * err, axis=-1, keepdims=True), axis=0, keepdims=True)
        _acc(loss_ref, jnp.broadcast_to(part, (SUBLANES, LANES)), i == 0)
        dx, dw = _rms_bwd(err * (1.0 / D_MODEL), xh, r, w)
        dx_ref[...] = dx
        _acc(dw_ref, dw, i == 0)

    vec = _const((1, D_MODEL))
    return pl.pallas_call(
        body, name="loss_head", grid=(S // tm,),
        in_specs=[_row(tm, D_MODEL), _row(tm, D_MODEL), vec],
        out_specs=[_row(tm, D_MODEL), _const((SUBLANES, LANES)), vec],
        out_shape=[_sds((S, D_MODEL)), _sds((SUBLANES, LANES)), _sds((1, D_MODEL))],
        compiler_params=_params(("arbitrary",)),
    )(x, target, fnw)


def _block_diag(w):
    eye = jnp.eye(LRU_BLOCKS, dtype=w.dtype)
    return (eye[:, None, :, None] * w[:, :, None, :]).reshape(LRU_W, LRU_W)


def _diag_blocks(m):
    m4 = m.reshape(LRU_BLOCKS, LRU_BLOCK, LRU_BLOCKS, LRU_BLOCK)
    return jnp.stack([m4[g, :, g, :] for g in range(LRU_BLOCKS)])


def _layer_fwd(x, p):
    proj, h1b = _proj_fwd(x, p["nmw"], p["sc1"], p["sh1"], p["win"])
    xr = _conv_fwd(proj, 0, LRU_W, p["lcw"], p["lcb"], False, "conv_lru_fwd")
    out_lru, h = _lru_fwd(xr, proj, p["wa"].astype(BF16), p["ba"], p["wx"].astype(BF16), p["bx"], p["lam"], p["lnw"])
    qkv = _conv_fwd(proj, 2 * LRU_W, 3 * GDN_W, p["gcw"], p["gcb"], True, "conv_gdn_fwd")
    o, states = _gdn_fwd(qkv, proj, p["alog"], p["dtb"])
    x1, cat = _out_fwd(x, out_lru, o, proj, p["gnw"], p["g1"], p["wout"])
    x2 = _mlp_fwd(x1, p["nmlp"], p["sc2"], p["sh2"], p["g2"], p["wup"], p["wdown"])
    res = dict(x=x, proj=proj, h1b=h1b, xr=xr, h=h, qkv=qkv, o=o, states=states, x1=x1, cat=cat)
    return x2, res


def _layer_bwd(dx2, p, r, sharded=False):
    dx1, h2b, dupb, actb, ddb, dnmlp, dsc2, dsh2, dg2 = _mlp_bwd(
        dx2, r["x1"], p["nmlp"], p["sc2"], p["sh2"], p["g2"], p["wup"], p["wdown"])
    g_wup = _matmul_tn(h2b, dupb, "dw_up", "cols" if sharded else None)
    g_wdown = _matmul_tn(actb, ddb, "dw_down", "rows" if sharded else None)
    dlru, do, dz, dmb, dgnw, dg1 = _out_bwd(dx1, r["cat"], r["o"], r["proj"], p["gnw"], p["g1"], p["wout"])
    g_wout = _matmul_tn(r["cat"], dmb, "dw_out", "rows" if sharded else None)
    dqkv_act, dba, dalog, ddtb = _gdn_bwd(do, r["qkv"], r["proj"], r["states"], p["alog"], p["dtb"])
    dqkv, dgcw, _ = _conv_bwd(r["proj"], 2 * LRU_W, 3 * GDN_W, p["gcw"], p["gcb"], dqkv_act, True, "conv_gdn_bwd")
    wab = p["wa"].astype(BF16)
    wxb = p["wx"].astype(BF16)
    dxr, dly, dwa, dba_, dwx, dbx, dlam, dlnw = _lru_bwd(
        dlru, r["xr"], r["proj"], r["h"], wab, p["ba"], wxb, p["bx"], p["lam"], p["lnw"])
    dlx, dlcw, dlcb = _conv_bwd(r["proj"], 0, LRU_W, p["lcw"], p["lcb"], dxr, False, "conv_lru_bwd")
    dx, dpb, dnmw, dsc1, dsh1 = _proj_bwd(dx1, r["x"], dlx, dly, dqkv, dz, dba, p["nmw"], p["sc1"], p["win"])
    g_win = _matmul_tn(r["h1b"], dpb, "dw_in")
    grads = dict(nmw=dnmw, nmlp=dnmlp, sh1=dsh1, sc1=dsc1, g1=dg1, sh2=dsh2, sc2=dsc2, g2=dg2,
                 win=g_win, lcw=dlcw, lcb=dlcb, wa=dwa, ba=dba_, wx=dwx, bx=dbx, lam=dlam, lnw=dlnw,
                 gcw=dgcw, alog=dalog, dtb=ddtb, gnw=dgnw, wout=g_wout, wup=g_wup, wdown=g_wdown)
    return dx, grads


def _local_step(x, target, fnw, layers):
    res = []
    for p in layers:
        x, r = _layer_fwd(x, p)
        res.append(r)
    dx, loss_blk, dfnw = _loss_head(x, target, fnw)
    grads = [None] * len(layers)
    for l in reversed(range(len(layers))):
        dx, grads[l] = _layer_bwd(dx, layers[l], res[l])
    stacked = {k: jnp.stack([g[k] for g in grads]) for k in grads[0]}
    return loss_blk[0, 0], dx, dfnw, stacked


def _prep_layers(norm_mix_w, norm_mlp_w, mod, win_b, lru_conv_w, lru_conv_b, gate_a_w, gate_a_b, gate_x_w, gate_x_b,
                 lru_lambda, lru_norm_w, gdn_conv_w, gdn_a_log, gdn_dt_bias, gdn_norm_w, wout_b, wup_b, wdown_b):
    L = norm_mix_w.shape[0]

    def vec(a):
        return a.reshape(L, 1, -1)

    def lanes(a):
        return jnp.pad(a, ((0, 0), (0, LANES - a.shape[1]))).reshape(L, 1, LANES)

    def taps(w):
        return jnp.pad(w, ((0, 0), (0, SUBLANES - w.shape[1]), (0, 0)))

    m = mod.reshape(L, N_MOD, 1, D_MODEL)
    return dict(
        nmw=vec(norm_mix_w), nmlp=vec(norm_mlp_w),
        sh1=m[:, 0], sc1=m[:, 1], g1=m[:, 2], sh2=m[:, 3], sc2=m[:, 4], g2=m[:, 5],
        win=win_b, lcw=taps(lru_conv_w), lcb=vec(lru_conv_b),
        wa=jax.vmap(_block_diag)(gate_a_w), ba=vec(gate_a_b), wx=jax.vmap(_block_diag)(gate_x_w), bx=vec(gate_x_b),
        lam=vec(lru_lambda), lnw=vec(lru_norm_w),
        gcw=taps(gdn_conv_w), gcb=jnp.zeros((L, 1, 3 * GDN_W), F32),
        alog=lanes(gdn_a_log), dtb=lanes(gdn_dt_bias), gnw=vec(gdn_norm_w),
        wout=wout_b, wup=wup_b, wdown=wdown_b)


def _position():
    x, y, c = lax.axis_index("x"), lax.axis_index("y"), lax.axis_index("c")
    return x, y, c


def _other_chips(x, y):
    return [(1 - x, y), (x, 1 - y), (1 - x, 1 - y)]


def _all_gather_rows(block, name):
    m, n = block.shape

    def body(x_ref, out_ref, send_sems, recv_sems, local_sem):
        x, y, c = _position()
        me, sibling = (x, y, c), (x, y, 1 - c)
        chips = _other_chips(x, y)

        def rows(px, py, pc):
            return out_ref.at[pl.ds((4 * px + 2 * py + pc) * m, m), :]

        def copy(k, blk, to, src=None):
            return pltpu.make_async_remote_copy(
                src_ref=rows(*blk) if src is None else src, dst_ref=rows(*blk),
                send_sem=send_sems.at[k], recv_sem=recv_sems.at[k], device_id=to, device_id_type=MESH)

        mine = pltpu.make_async_copy(x_ref, rows(*me), local_sem)
        mine.start()
        first = [copy(0, me, sibling, src=x_ref)]
        first += [copy(1 + j, me, (*chip, c), src=x_ref) for j, chip in enumerate(chips)]
        for cp in first:
            cp.start()
        passed = [copy(4 + j, (*chip, c), sibling) for j, chip in enumerate(chips)]
        for j, chip in enumerate(chips):
            copy(1 + j, (*chip, c), me).wait_recv()
            passed[j].start()
        copy(0, sibling, me).wait_recv()
        for j, chip in enumerate(chips):
            copy(4 + j, (*chip, 1 - c), me).wait_recv()
        for cp in first + passed:
            cp.wait_send()
        mine.wait()

    return pl.pallas_call(
        body, name=name,
        out_shape=_sds((N_DEV * m, n)),
        in_specs=[pl.BlockSpec(memory_space=pltpu.VMEM)],
        out_specs=pl.BlockSpec(memory_space=pltpu.VMEM),
        scratch_shapes=[pltpu.SemaphoreType.DMA((7,)), pltpu.SemaphoreType.DMA((7,)), pltpu.SemaphoreType.DMA],
        compiler_params=pltpu.CompilerParams(vmem_limit_bytes=VMEM_LIMIT),
    )(block)


def _hbm_specs(n):
    return [pl.BlockSpec(memory_space=pl.ANY)] * n


def _gather_chips(shards, name):
    n = len(shards)

    def body(*refs):
        ins, outs = refs[:n], refs[n:2 * n]
        send_sems, recv_sems = refs[2 * n:]
        x, y, c = _position()
        chips = _other_chips(x, y)
        me = 2 * x + y

        def first(a, j, slot):
            h = ins[a].shape[0] // 2
            return pltpu.make_async_remote_copy(
                src_ref=ins[a].at[pl.ds(pl.multiple_of(c * h, SUBLANES), h)], dst_ref=outs[a].at[slot, c],
                send_sem=send_sems.at[3 * a + j], recv_sem=recv_sems.at[3 * a + j],
                device_id=(chips[j][0], chips[j][1], c), device_id_type=MESH)

        def second(a, j, half):
            slot = 2 * chips[j][0] + chips[j][1]
            return pltpu.make_async_remote_copy(
                src_ref=outs[a].at[slot, c], dst_ref=outs[a].at[slot, half],
                send_sem=send_sems.at[3 * (n + a) + j], recv_sem=recv_sems.at[3 * (n + a) + j],
                device_id=(x, y, 1 - c), device_id_type=MESH)

        sends = [first(a, j, me) for a in range(n) for j in range(3)]
        for cp in sends:
            cp.start()
        for a in range(n):
            for j, (px, py) in enumerate(chips):
                first(a, j, 2 * px + py).wait_recv()
                passed = second(a, j, c)
                passed.start()
                sends.append(passed)
        for a in range(n):
            for j in range(3):
                second(a, j, 1 - c).wait_recv()
        for cp in sends:
            cp.wait_send()

    return pl.pallas_call(
        body, name=name,
        out_shape=[_sds((N_CHIPS, 2, s.shape[0] // 2, s.shape[1]), s.dtype) for s in shards],
        in_specs=_hbm_specs(n), out_specs=_hbm_specs(n),
        scratch_shapes=[pltpu.SemaphoreType.DMA((6 * n,)), pltpu.SemaphoreType.DMA((6 * n,))],
    )(*shards)


def _send_to_sibling(parts, name):
    n = len(parts)

    def body(*refs):
        ins, outs = refs[:n], refs[n:2 * n]
        send_sems, recv_sems = refs[2 * n:]
        x, y, c = _position()
        copies = [pltpu.make_async_remote_copy(
            src_ref=ins[a].at[1 - c], dst_ref=outs[a], send_sem=send_sems.at[a], recv_sem=recv_sems.at[a],
            device_id=(x, y, 1 - c), device_id_type=MESH) for a in range(n)]
        for cp in copies:
            cp.start()
        for cp in copies:
            cp.wait()

    return pl.pallas_call(
        body, name=name,
        out_shape=[_sds(p.shape[1:], p.dtype) for p in parts],
        in_specs=_hbm_specs(n), out_specs=_hbm_specs(n),
        scratch_shapes=[pltpu.SemaphoreType.DMA((n,)), pltpu.SemaphoreType.DMA((n,))],
    )(*parts)


def _scatter_chips(parts, name):
    n = len(parts)

    def body(*refs):
        ins, outs = refs[:n], refs[n:2 * n]
        send_sems, recv_sems = refs[2 * n:]
        x, y, c = _position()
        chips = _other_chips(x, y)
        me = 2 * x + y

        def copy(a, j, src_slot, dst_slot):
            px, py = chips[j]
            return pltpu.make_async_remote_copy(
                src_ref=ins[a].at[src_slot], dst_ref=outs[a].at[dst_slot], send_sem=send_sems.at[3 * a + j],
                recv_sem=recv_sems.at[3 * a + j], device_id=(px, py, c), device_id_type=MESH)

        sends = [copy(a, j, 2 * chips[j][0] + chips[j][1], me) for a in range(n) for j in range(3)]
        for cp in sends:
            cp.start()
        for a in range(n):
            for j, (px, py) in enumerate(chips):
                copy(a, j, me, 2 * px + py).wait_recv()
        for cp in sends:
            cp.wait_send()

    return pl.pallas_call(
        body, name=name,
        out_shape=[_sds(p.shape, p.dtype) for p in parts],
        in_specs=_hbm_specs(n), out_specs=_hbm_specs(n),
        scratch_shapes=[pltpu.SemaphoreType.DMA((3 * n,)), pltpu.SemaphoreType.DMA((3 * n,))],
    )(*parts)


def _swap_row_halves(arrays, name):
    n = len(arrays)

    def body(*refs):
        outs = refs[n:2 * n]
        send_sems, recv_sems = refs[2 * n:]
        x, y, c = _position()

        def copy(a, half):
            h = outs[a].shape[0] // 2
            rows = outs[a].at[pl.ds(pl.multiple_of(half * h, SUBLANES), h)]
            return pltpu.make_async_remote_copy(
                src_ref=rows, dst_ref=rows, send_sem=send_sems.at[a], recv_sem=recv_sems.at[a],
                device_id=(x, y, 1 - c), device_id_type=MESH)

        sends = [copy(a, c) for a in range(n)]
        for cp in sends:
            cp.start()
        for a in range(n):
            copy(a, 1 - c).wait_recv()
        for cp in sends:
            cp.wait_send()

    return pl.pallas_call(
        body, name=name,
        out_shape=[_sds(a.shape, a.dtype) for a in arrays],
        in_specs=_hbm_specs(n), out_specs=_hbm_specs(n),
        input_output_aliases={a: a for a in range(n)},
        scratch_shapes=[pltpu.SemaphoreType.DMA((n,)), pltpu.SemaphoreType.DMA((n,))],
    )(*arrays)


def _row_tile(rows):
    for t in (512, 256, 128, 64, 32, 16, 8):
        if rows % t == 0:
            return t
    return rows


def _sum_slots(buf, name):
    k, rows, cols = buf.shape
    tm = _row_tile(rows)

    def body(b_ref, o_ref):
        s = b_ref[0]
        for i in range(1, k):
            s = s + b_ref[i]
        o_ref[...] = s

    return pl.pallas_call(
        body, name=name, grid=(rows // tm,),
        in_specs=[pl.BlockSpec((k, tm, cols), lambda i: (0, i, 0))],
        out_specs=pl.BlockSpec((tm, cols), lambda i: (i, 0)),
        out_shape=_sds((rows, cols)),
        compiler_params=_params(("arbitrary",)),
    )(buf)


def _pair_add(part, from_sibling, core, name):
    _, k, h, cols = part.shape
    rows = k * h
    tm = _row_tile(rows)

    def body(core_ref, a_ref, b_ref, o_ref):
        o_ref[...] = (a_ref[0] + b_ref[...]).astype(BF16)

    out = pl.pallas_call(
        body, name=name,
        grid_spec=pltpu.PrefetchScalarGridSpec(
            num_scalar_prefetch=1, grid=(rows // tm,),
            in_specs=[pl.BlockSpec((1, tm, cols), lambda i, cr: (cr[0], i, 0)),
                      pl.BlockSpec((tm, cols), lambda i, cr: (i, 0))],
            out_specs=pl.BlockSpec((tm, cols), lambda i, cr: (i, 0))),
        out_shape=_sds((rows, cols), BF16),
        compiler_params=_params(("arbitrary",)),
    )(core, part.reshape(2, rows, cols), from_sibling.reshape(rows, cols))
    return out.reshape(k, h, cols)


def _sum_adam(arrived, own, w, m, v, layer, place, name):
    _, h, cols = arrived.shape
    tm = min(256, h)
    nb = h // tm

    def body(place_ref, arr_ref, own_ref, w_ref, m_ref, v_ref, g_ref, d_ref, nm_ref, nv_ref):
        for chip in range(N_CHIPS):
            @pl.when(place_ref[1] == chip)
            def _():
                terms = [own_ref[0] if j == chip else arr_ref[j] for j in range(N_CHIPS)]
                g = terms[0].astype(F32)
                for t in terms[1:]:
                    g = g + t.astype(F32)
                g_ref[...] = g
                d, nm, nv = _adam_math(w_ref[0], g, m_ref[0], v_ref[0])
                d_ref[...] = d
                nm_ref[...] = nm
                nv_ref[...] = nv

    state = pl.BlockSpec((1, tm, cols), lambda i, pr: (layer, pr[0] * nb + i, 0))
    result = pl.BlockSpec((tm, cols), lambda i, pr: (pr[0] * nb + i, 0))
    return pl.pallas_call(
        body, name=name,
        grid_spec=pltpu.PrefetchScalarGridSpec(
            num_scalar_prefetch=1, grid=(nb,),
            in_specs=[pl.BlockSpec((N_CHIPS, tm, cols), lambda i, pr: (0, i, 0)),
                      pl.BlockSpec((1, tm, cols), lambda i, pr: (pr[1], i, 0)), state, state, state],
            out_specs=[result] * 4),
        out_shape=[_sds((2 * h, cols))] * 4,
        compiler_params=_params(("arbitrary",)),
    )(place, arrived, own, w, m, v)


def _adam_math(w, g, m, v):
    m = ADAM_B1 * m + (1.0 - ADAM_B1) * g
    v = ADAM_B2 * v + (1.0 - ADAM_B2) * jnp.square(g)
    m_hat = m / (1.0 - ADAM_B1 ** ADAM_STEP)
    v_hat = v / (1.0 - ADAM_B2 ** ADAM_STEP)
    delta = -ADAM_LR * (m_hat / (jnp.sqrt(v_hat) + ADAM_EPS) + ADAM_WD * w)
    return delta, m, v


def _adam(w, g, m, v, name):
    rows, cols = w.shape
    tm = _row_tile(rows)

    def body(w_ref, g_ref, m_ref, v_ref, d_ref, nm_ref, nv_ref):
        d, nm, nv = _adam_math(w_ref[...], g_ref[...], m_ref[...], v_ref[...])
        d_ref[...] = d
        nm_ref[...] = nm
        nv_ref[...] = nv

    spec = pl.BlockSpec((tm, cols), lambda i: (i, 0))
    return pl.pallas_call(
        body, name=name, grid=(rows // tm,), in_specs=[spec] * 4, out_specs=[spec] * 3,
        out_shape=[_sds((rows, cols))] * 3, compiler_params=_params(("arbitrary",)),
    )(w, g, m, v)


def _mod_fwd(c_all, w_mod, b_mod_cols):
    L, _, n = w_mod.shape

    def body(c_ref, w_ref, b_ref, o_ref):
        o_ref[0] = _hdot(_silu(c_ref[...]), w_ref[0]) + b_ref[0]

    return pl.pallas_call(
        body, name="mod_fwd", grid=(L,),
        in_specs=[_const((N_DEV, D_MODEL)), pl.BlockSpec((1, D_MODEL, n), lambda l: (l, 0, 0)),
                  pl.BlockSpec((1, 1, n), lambda l: (l, 0, 0))],
        out_specs=pl.BlockSpec((1, N_DEV, n), lambda l: (l, 0, 0)),
        out_shape=_sds((L, N_DEV, n)),
        compiler_params=_params(("arbitrary",)),
    )(c_all, w_mod, b_mod_cols)


def _mod_update(c_all, dmod, w, m, v):
    L, _, n = w.shape
    tn = 512

    def body(c_ref, d_ref, w_ref, m_ref, v_ref, g_ref, dl_ref, nm_ref, nv_ref):
        g = _hdot_tn(_silu(c_ref[...]), d_ref[0])
        g_ref[0] = g
        d, nm, nv = _adam_math(w_ref[0], g, m_ref[0], v_ref[0])
        dl_ref[0] = d
        nm_ref[0] = nm
        nv_ref[0] = nv

    big = pl.BlockSpec((1, D_MODEL, tn), lambda l, j: (l, 0, j))
    return pl.pallas_call(
        body, name="mod_update", grid=(L, n // tn),
        in_specs=[_const((N_DEV, D_MODEL)), pl.BlockSpec((1, N_DEV, tn), lambda l, j: (l, 0, j)), big, big, big],
        out_specs=[big] * 4, out_shape=[_sds(w.shape)] * 4,
        compiler_params=_params(("arbitrary", "arbitrary")),
    )(c_all, dmod, w, m, v)


def _pack_rows(parts, row_multiple):
    flat = jnp.concatenate([p.reshape(-1) for p in parts])
    unit = row_multiple * LANES
    flat = jnp.pad(flat, (0, (-flat.shape[0]) % unit))
    return flat.reshape(-1, LANES)


def _unpack(packed, shapes):
    flat = packed.reshape(-1)
    out, off = [], 0
    for s in shapes:
        n = 1
        for d in s:
            n *= d
        out.append(flat[off:off + n].reshape(s))
        off += n
    return out


def _lane_pad(a):
    return jnp.pad(a, ((0, 0), (0, LANES - a.shape[1])))


WEIGHT_NAMES = ("norm_mix_w", "norm_mlp_w", "w_mod", "b_mod", "w_in", "lru_conv_w", "lru_conv_b", "lru_gate_a_w",
                "lru_gate_a_b", "lru_gate_x_w", "lru_gate_x_b", "lru_lambda", "lru_norm_w", "gdn_conv_w", "gdn_a_log",
                "gdn_dt_bias", "gdn_norm_w", "w_out", "w_up", "w_down", "final_norm_w")


def kernel(x, c, norm_mix_w, norm_mlp_w, w_mod, b_mod, w_in, lru_conv_w, lru_conv_b, lru_gate_a_w, lru_gate_a_b, lru_gate_x_w, lru_gate_x_b, lru_lambda, lru_norm_w, gdn_conv_w, gdn_a_log, gdn_dt_bias, gdn_norm_w, w_out, w_up, w_down, final_norm_w, loss_target, m_norm_mix_w, m_norm_mlp_w, m_w_mod, m_b_mod, m_w_in, m_lru_conv_w, m_lru_conv_b, m_lru_gate_a_w, m_lru_gate_a_b, m_lru_gate_x_w, m_lru_gate_x_b, m_lru_lambda, m_lru_norm_w, m_gdn_conv_w, m_gdn_a_log, m_gdn_dt_bias, m_gdn_norm_w, m_w_out, m_w_up, m_w_down, m_final_norm_w, v_norm_mix_w, v_norm_mlp_w, v_w_mod, v_b_mod, v_w_in, v_lru_conv_w, v_lru_conv_b, v_lru_gate_a_w, v_lru_gate_a_b, v_lru_gate_x_w, v_lru_gate_x_b, v_lru_lambda, v_lru_norm_w, v_gdn_conv_w, v_gdn_a_log, v_gdn_dt_bias, v_gdn_norm_w, v_w_out, v_w_up, v_w_down, v_final_norm_w):
    W = dict(zip(WEIGHT_NAMES, (norm_mix_w, norm_mlp_w, w_mod, b_mod, w_in, lru_conv_w, lru_conv_b, lru_gate_a_w,
                                lru_gate_a_b, lru_gate_x_w, lru_gate_x_b, lru_lambda, lru_norm_w, gdn_conv_w, gdn_a_log,
                                gdn_dt_bias, gdn_norm_w, w_out, w_up, w_down, final_norm_w)))
    M = dict(zip(WEIGHT_NAMES, (m_norm_mix_w, m_norm_mlp_w, m_w_mod, m_b_mod, m_w_in, m_lru_conv_w, m_lru_conv_b,
                                m_lru_gate_a_w, m_lru_gate_a_b, m_lru_gate_x_w, m_lru_gate_x_b, m_lru_lambda,
                                m_lru_norm_w, m_gdn_conv_w, m_gdn_a_log, m_gdn_dt_bias, m_gdn_norm_w, m_w_out, m_w_up,
                                m_w_down, m_final_norm_w)))
    V = dict(zip(WEIGHT_NAMES, (v_norm_mix_w, v_norm_mlp_w, v_w_mod, v_b_mod, v_w_in, v_lru_conv_w, v_lru_conv_b,
                                v_lru_gate_a_w, v_lru_gate_a_b, v_lru_gate_x_w, v_lru_gate_x_b, v_lru_lambda,
                                v_lru_norm_w, v_gdn_conv_w, v_gdn_a_log, v_gdn_dt_bias, v_gdn_norm_w, v_w_out, v_w_up,
                                v_w_down, v_final_norm_w)))
    L = DEPTH
    xi, yi, ci = _position()
    chip = 2 * xi + yi
    dev = 2 * chip + ci
    lcs = LRU_W // N_CHIPS
    gcs = 3 * GDN_W // N_CHIPS
    mcs = N_MOD * D_MODEL // N_CHIPS

    g_in = _all_gather_rows(_pack_rows([c, lru_conv_w, gdn_conv_w], SUBLANES), "gather_small_inputs").reshape(N_DEV, -1)
    c_all = g_in[:, :D_MODEL]
    per_chip = g_in[0::2]
    o1 = D_MODEL + L * 4 * lcs
    lcw_full = per_chip[:, D_MODEL:o1].reshape(N_CHIPS, L, 4, lcs).transpose(1, 2, 0, 3).reshape(L, 4, LRU_W)
    gcw_full = per_chip[:, o1:o1 + L * 4 * gcs].reshape(N_CHIPS, L, 4, gcs).transpose(1, 2, 0, 3).reshape(L, 4, 3 * GDN_W)

    b_cols = lax.dynamic_slice(b_mod, (0, chip * mcs), (L, mcs)).reshape(L, 1, mcs)
    modp = _mod_fwd(c_all, w_mod, b_cols)
    g_mod = _all_gather_rows(modp.reshape(L * N_DEV, mcs), "gather_mod").reshape(N_DEV, L, N_DEV, mcs)
    mod = lax.dynamic_index_in_dim(g_mod[0::2], dev, axis=2, keepdims=False).transpose(1, 0, 2).reshape(L, N_MOD * D_MODEL)

    stacked = _prep_layers(norm_mix_w, norm_mlp_w, mod, None, lcw_full, lru_conv_b, lru_gate_a_w, lru_gate_a_b,
                           lru_gate_x_w, lru_gate_x_b, lru_lambda, lru_norm_w, gcw_full, gdn_a_log, gdn_dt_bias,
                           gdn_norm_w, None, None, None)
    layers = []
    for l in range(L):
        shards = [w_in[l].astype(BF16), w_out[l].astype(BF16), w_up[l].astype(BF16), w_down[l].astype(BF16)]
        gathered = _gather_chips(shards, "gather_weights")
        win_g, wout_g, wup_g, wdown_g = (
            lax.dynamic_update_slice(got, own.reshape((1,) + got.shape[1:]), (chip, 0, 0, 0)).reshape(
                (N_CHIPS,) + own.shape) for got, own in zip(gathered, shards))
        p = {k: v[l] for k, v in stacked.items() if v is not None}
        p["win"] = jnp.pad(win_g.transpose(1, 0, 2).reshape(D_MODEL, IN_COLS), ((0, 0), (0, IN_PAD - IN_COLS)))
        p["wout"] = wout_g.reshape(D_MODEL, D_MODEL)
        p["wup"] = wup_g.transpose(1, 0, 2).reshape(D_MODEL, D_FF)
        p["wdown"] = wdown_g.reshape(D_FF, D_MODEL)
        layers.append(p)
    xs = x[0]
    res = []
    for p in layers:
        xs, r = _layer_fwd(xs, p)
        res.append(r)
    dx, loss_blk, dfnw = _loss_head(xs, loss_target[0], final_norm_w.reshape(1, D_MODEL))
    loss_local = loss_blk[0, 0]

    big_names = ["w_in", "w_out", "w_up", "w_down"]
    core = jnp.reshape(ci, (1,)).astype(jnp.int32)
    place = jnp.stack([ci, chip]).astype(jnp.int32)
    layer_grads = [None] * L
    big = [None] * L
    for l in reversed(range(L)):
        dx, gl = _layer_bwd(dx, layers[l], res[l], sharded=True)
        layer_grads[l] = gl
        gwin = gl["win"][:, :IN_COLS].reshape(2, D_MODEL // 2, N_CHIPS, IN_COLS // N_CHIPS).transpose(0, 2, 1, 3)
        parts = [gwin, gl["wout"], gl["wup"], gl["wdown"]]
        from_sibling = _send_to_sibling(parts, "pair_send")
        pair = [_pair_add(p, r, core, "pair_add_" + nm) for nm, p, r in zip(big_names, parts, from_sibling)]
        arrived = _scatter_chips(pair, "chip_scatter")
        halves = []
        for nm, a, own in zip(big_names, arrived, pair):
            halves += _sum_adam(a, own, W[nm], M[nm], V[nm], l, place, "sum_adam_" + nm)
        big[l] = _swap_row_halves(halves, "pair_swap")
    small_keys = [k for k in layer_grads[0] if k not in ("win", "wout", "wup", "wdown")]
    g = {k: jnp.stack([gl[k] for gl in layer_grads]) for k in small_keys}
    loss = lax.psum(loss_local, ("x", "y", "c"))

    dmod = jnp.concatenate([g["sh1"], g["sc1"], g["g1"], g["sh2"], g["sc2"], g["g2"]], axis=-1)
    small = [dmod, g["nmw"], g["nmlp"], g["lcw"][:, :4], g["lcb"], jax.vmap(_diag_blocks)(g["wa"]), g["ba"],
             jax.vmap(_diag_blocks)(g["wx"]), g["bx"], g["lam"], g["lnw"], g["gcw"][:, :4], g["alog"], g["dtb"],
             g["gnw"], dfnw]
    small_shapes = [(L, N_MOD * D_MODEL), (L, D_MODEL), (L, D_MODEL), (L, 4, LRU_W), (L, LRU_W),
                    (L, LRU_BLOCKS, LRU_BLOCK, LRU_BLOCK), (L, LRU_W), (L, LRU_BLOCKS, LRU_BLOCK, LRU_BLOCK),
                    (L, LRU_W), (L, LRU_W), (L, LRU_W), (L, 4, 3 * GDN_W), (L, LANES), (L, LANES), (L, LANES),
                    (D_MODEL,)]
    small_names = ["b_mod", "norm_mix_w", "norm_mlp_w", None, "lru_conv_b", "lru_gate_a_w", "lru_gate_a_b",
                   "lru_gate_x_w", "lru_gate_x_b", "lru_lambda", "lru_norm_w", None, "gdn_a_log", "gdn_dt_bias",
                   "gdn_norm_w", "final_norm_w"]
    pack_g = _pack_rows(small, 512)
    rows = pack_g.shape[0]
    all_g = _all_gather_rows(pack_g, "gather_small_grads").reshape(N_DEV, rows, LANES)
    tot = _sum_slots(all_g, "sum_small_grads")
    tot_parts = _unpack(tot, small_shapes)

    def pack_state(S_):
        parts = []
        for nm, shp in zip(small_names, small_shapes):
            if nm is None:
                parts.append(jnp.zeros(shp, F32))
            elif nm in ("gdn_a_log", "gdn_dt_bias"):
                parts.append(_lane_pad(S_[nm]))
            else:
                parts.append(S_[nm])
        return _pack_rows(parts, 512)

    upd = _adam(pack_state(W), tot, pack_state(M), pack_state(V), "adam_small")
    upd_parts = [_unpack(u, small_shapes) for u in upd]

    grads, deltas, new_m, new_v = {}, {}, {}, {}
    for k, nm in enumerate(small_names):
        if nm is None:
            continue
        cut = (lambda a: a[:, :HEADS]) if nm in ("gdn_a_log", "gdn_dt_bias") else (lambda a: a)
        grads[nm] = cut(tot_parts[k])
        deltas[nm], new_m[nm], new_v[nm] = (cut(u[k]) for u in upd_parts)

    g_lcw = lax.dynamic_slice(tot_parts[3], (0, 0, chip * lcs), (L, 4, lcs))
    g_gcw = lax.dynamic_slice(tot_parts[11], (0, 0, chip * gcs), (L, 4, gcs))
    conv_shapes = [(L, 4, lcs), (L, 4, gcs)]
    conv_pack = lambda a, b: _pack_rows([a, b], SUBLANES)
    cu = _adam(conv_pack(lru_conv_w, gdn_conv_w), conv_pack(g_lcw, g_gcw), conv_pack(m_lru_conv_w, m_gdn_conv_w),
               conv_pack(v_lru_conv_w, v_gdn_conv_w), "adam_conv")
    cu_parts = [_unpack(u, conv_shapes) for u in cu]
    for k, nm in enumerate(("lru_conv_w", "gdn_conv_w")):
        grads[nm] = (g_lcw, g_gcw)[k]
        deltas[nm], new_m[nm], new_v[nm] = (u[k] for u in cu_parts)

    dmod_all = all_g[:, :L * N_MOD * D_MODEL // LANES].reshape(N_DEV, L, N_MOD * D_MODEL)
    dmod_cols = lax.dynamic_slice(dmod_all, (0, 0, chip * mcs), (N_DEV, L, mcs)).transpose(1, 0, 2)
    grads["w_mod"], deltas["w_mod"], new_m["w_mod"], new_v["w_mod"] = _mod_update(c_all, dmod_cols, w_mod, m_w_mod, v_w_mod)

    for k, nm in enumerate(big_names):
        grads[nm], deltas[nm], new_m[nm], new_v[nm] = (jnp.stack([big[l][4 * k + i] for l in range(L)])
                                                       for i in range(4))

    out = [loss, dx[None]]
    for group in (grads, deltas, new_m, new_v):
        out += [group[nm].reshape(W[nm].shape) for nm in WEIGHT_NAMES]
    return tuple(out)
```

```python
import functools

import jax
import jax.numpy as jnp
from jax import lax
from jax.experimental import pallas as pl
from jax.experimental.pallas import tpu as pltpu

F32 = jnp.float32
BF16 = jnp.bfloat16
MESH = pl.DeviceIdType.MESH

D_MODEL = 1024
DEPTH = 4
LRU_W = 512
LRU_BLOCKS = 8
LRU_BLOCK = 64
LRU_C = 8.0
HEADS = 4
HEAD_DIM = 128
GDN_W = 512
CHUNK = 128
D_FF = 4096
N_MOD = 6
IN_COLS = 3080
IN_PAD = 3200
NORM_EPS = 1e-6
LANES = 128
SUBLANES = 8
N_DEV = 8
N_CHIPS = 4

ADAM_LR = 0.001
ADAM_B1 = 0.9
ADAM_B2 = 0.999
ADAM_EPS = 1e-08
ADAM_WD = 0.01
ADAM_STEP = 10

VMEM_LIMIT = 56 * 1024 * 1024
HI = lax.Precision.HIGHEST


def _sds(shape, dtype=F32):
    return jax.ShapeDtypeStruct(tuple(shape), dtype)


def _params(sem=None, vmem=VMEM_LIMIT):
    return pltpu.CompilerParams(dimension_semantics=sem, vmem_limit_bytes=vmem)


def _const(shape):
    return pl.BlockSpec(tuple(shape), lambda *_: (0,) * len(shape))


def _row(tm, c, col=0):
    return pl.BlockSpec((tm, c), lambda i: (i, col))


def _dot(a, b):
    return jnp.dot(a, b, preferred_element_type=F32)


def _dot_nt(a, b):
    return lax.dot_general(a, b, (((1,), (1,)), ((), ())), preferred_element_type=F32)


def _dot_tn(a, b):
    return lax.dot_general(a, b, (((0,), (0,)), ((), ())), preferred_element_type=F32)


def _hdot(a, b):
    return jnp.dot(a, b, preferred_element_type=F32, precision=HI)


def _hdot_nt(a, b):
    return lax.dot_general(a, b, (((1,), (1,)), ((), ())), preferred_element_type=F32, precision=HI)


def _hdot_tn(a, b):
    return lax.dot_general(a, b, (((0,), (0,)), ((), ())), preferred_element_type=F32, precision=HI)


_DIMS = {"nn": (((1,), (0,)), ((), ())), "nt": (((1,), (1,)), ((), ())), "tn": (((0,), (0,)), ((), ()))}


def _mm_raw(a, b, dims, passes):
    dn = _DIMS[dims]

    def dot(p, q):
        return lax.dot_general(p, q, dn, preferred_element_type=F32)

    a_hi = a.astype(BF16)
    b_hi = b.astype(BF16)
    if passes == 1:
        return dot(a_hi, b_hi)
    a_lo = (a - a_hi.astype(F32)).astype(BF16)
    b_lo = (b - b_hi.astype(F32)).astype(BF16)
    return dot(a_hi, b_hi) + (dot(a_hi, b_lo) + dot(a_lo, b_hi))


@functools.partial(jax.custom_vjp, nondiff_argnums=(2, 3))
def _mm(a, b, dims, passes):
    return _mm_raw(a, b, dims, passes)


def _mm_fwd(a, b, dims, passes):
    return _mm_raw(a, b, dims, passes), (a, b)


def _mm_bwd(dims, passes, res, ct):
    a, b = res
    if dims == "nn":
        return _mm_raw(ct, b, "nt", passes), _mm_raw(a, ct, "tn", passes)
    if dims == "nt":
        return _mm_raw(ct, b, "nn", passes), _mm_raw(ct, a, "tn", passes)
    return _mm_raw(b, ct, "nt", passes), _mm_raw(a, ct, "nn", passes)


_mm.defvjp(_mm_fwd, _mm_bwd)


def _acc(ref, val, first):
    @pl.when(first)
    def _():
        ref[...] = val

    @pl.when(jnp.logical_not(first))
    def _():
        ref[...] += val


def _colsum(v):
    return jnp.sum(v, axis=0, keepdims=True)


def _rms_parts(x):
    r = lax.rsqrt(jnp.mean(x * x, axis=-1, keepdims=True) + NORM_EPS)
    return x * r, r


def _rms_bwd(dy, xh, r, w):
    dxh = dy * w
    dw = _colsum(dy * xh)
    dx = r * (dxh - xh * jnp.mean(dxh * xh, axis=-1, keepdims=True))
    return dx, dw


def _norm_mod(x, w, sc, sh):
    xh, _ = _rms_parts(x)
    return (xh * w) * (1.0 + sc) + sh


def _norm_mod_bwd(dy, x, w, sc):
    xh, r = _rms_parts(x)
    n = xh * w
    dsh = _colsum(dy)
    dsc = _colsum(dy * n)
    dx, dw = _rms_bwd(dy * (1.0 + sc), xh, r, w)
    return dx, dw, dsc, dsh


def _softplus(x):
    return jnp.maximum(x, 0.0) + jnp.log1p(jnp.exp(-jnp.abs(x)))


def _silu(x):
    return x * jax.nn.sigmoid(x)


def _silu_grad(x):
    s = jax.nn.sigmoid(x)
    return s * (1.0 + x * (1.0 - s))


def _roll_dn(x, d):
    return x if d == 0 else pltpu.roll(x, d, 0)


def _roll_up(x, d):
    return x if d == 0 else pltpu.roll(x, x.shape[0] - d, 0)


def _proj_fwd(x, nw, sc, sh, win):
    S = x.shape[0]
    tm = min(512, S)

    def body(x_ref, nw_ref, sc_ref, sh_ref, w_ref, proj_ref, hb_ref):
        hb = _norm_mod(x_ref[...], nw_ref[...], sc_ref[...], sh_ref[...]).astype(BF16)
        hb_ref[...] = hb
        proj_ref[...] = _dot(hb, w_ref[...])

    vec = _const((1, D_MODEL))
    return pl.pallas_call(
        body, name="proj_fwd", grid=(S // tm,),
        in_specs=[_row(tm, D_MODEL), vec, vec, vec, _const((D_MODEL, IN_PAD))],
        out_specs=[_row(tm, IN_PAD), _row(tm, D_MODEL)],
        out_shape=[_sds((S, IN_PAD)), _sds((S, D_MODEL), BF16)],
        compiler_params=_params(("arbitrary",)),
    )(x, nw, sc, sh, win)


def _proj_bwd(dx1, x, dlx, dly, dqkv, dz, dba, nw, sc, win):
    S = x.shape[0]
    tm = min(512, S)

    def body(dx1_ref, x_ref, dlx_ref, dly_ref, dqkv_ref, dz_ref, dba_ref, nw_ref, sc_ref, w_ref,
             dx_ref, dpb_ref, dnw_ref, dsc_ref, dsh_ref):
        i = pl.program_id(0)
        dpb = jnp.concatenate([dlx_ref[...], dly_ref[...], dqkv_ref[...], dz_ref[...], dba_ref[...]],
                              axis=-1).astype(BF16)
        dpb_ref[...] = dpb
        dh = _dot_nt(dpb, w_ref[...])
        dx, dnw, dsc, dsh = _norm_mod_bwd(dh, x_ref[...], nw_ref[...], sc_ref[...])
        dx_ref[...] = dx1_ref[...] + dx
        _acc(dnw_ref, dnw, i == 0)
        _acc(dsc_ref, dsc, i == 0)
        _acc(dsh_ref, dsh, i == 0)

    vec = _const((1, D_MODEL))
    return pl.pallas_call(
        body, name="proj_bwd", grid=(S // tm,),
        in_specs=[_row(tm, D_MODEL), _row(tm, D_MODEL), _row(tm, LRU_W), _row(tm, LRU_W), _row(tm, 3 * GDN_W),
                  _row(tm, GDN_W), _row(tm, LANES), vec, vec,
                  _const((D_MODEL, IN_PAD))],
        out_specs=[_row(tm, D_MODEL), _row(tm, IN_PAD), vec, vec, vec],
        out_shape=[_sds((S, D_MODEL)), _sds((S, IN_PAD), BF16), _sds((1, D_MODEL)), _sds((1, D_MODEL)),
                   _sds((1, D_MODEL))],
        compiler_params=_params(("arbitrary",)),
    )(dx1, x, dlx, dly, dqkv, dz, dba, nw, sc, win)


def _conv_taps(xx, w, tm):
    y = _roll_dn(xx, 3)[SUBLANES:] * w[0:1]
    y = y + _roll_dn(xx, 2)[SUBLANES:] * w[1:2]
    y = y + _roll_dn(xx, 1)[SUBLANES:] * w[2:3]
    y = y + xx[SUBLANES:] * w[3:4]
    return y


def _conv_fwd(src, col0, C, w8, b, act, name):
    S = src.shape[0]
    tm = min(512, S)
    tc = 512
    hb = tm // SUBLANES
    cb0 = col0 // tc

    def body(x_ref, p_ref, w_ref, b_ref, y_ref):
        i = pl.program_id(0)
        prev = jnp.where(i > 0, p_ref[...], 0.0)
        xx = jnp.concatenate([prev, x_ref[...]], axis=0)
        y = _conv_taps(xx, w_ref[...], tm) + b_ref[...]
        y_ref[...] = _silu(y) if act else y

    return pl.pallas_call(
        body, name=name, grid=(S // tm, C // tc),
        in_specs=[pl.BlockSpec((tm, tc), lambda i, j: (i, cb0 + j)),
                  pl.BlockSpec((SUBLANES, tc), lambda i, j: (jnp.maximum(i * hb - 1, 0), cb0 + j)),
                  pl.BlockSpec((SUBLANES, tc), lambda i, j: (0, j)),
                  pl.BlockSpec((1, tc), lambda i, j: (0, j))],
        out_specs=pl.BlockSpec((tm, tc), lambda i, j: (i, j)),
        out_shape=_sds((S, C)),
        compiler_params=_params(("arbitrary", "arbitrary")),
    )(src, src, w8, b)


def _conv_bwd(src, col0, C, w8, b, dyact, act, name):
    S = src.shape[0]
    tm = min(512, S)
    tc = 512
    hb = tm // SUBLANES
    nt = S // tm
    cb0 = col0 // tc
    last_hb = S // SUBLANES - 1

    def body(x_ref, p_ref, n_ref, dy_ref, dyn_ref, w_ref, b_ref, dx_ref, dw_ref, db_ref):
        i = pl.program_id(1)
        w = w_ref[...]
        prev = jnp.where(i > 0, p_ref[...], 0.0)
        xx = jnp.concatenate([prev, x_ref[...], n_ref[...]], axis=0)
        dy = jnp.concatenate([dy_ref[...], jnp.where(i < nt - 1, dyn_ref[...], 0.0)], axis=0)
        if act:
            ypre = _conv_taps(xx, w, tm + SUBLANES) + b_ref[...]
            dy = dy * _silu_grad(ypre)
        dx = dy[:tm] * w[3:4]
        for d in (1, 2, 3):
            dx = dx + _roll_up(dy, d)[:tm] * w[3 - d:4 - d]
        dx_ref[...] = dx
        xt = xx[:tm + SUBLANES]
        dyt = dy[:tm]
        rows = [_colsum(dyt * _roll_dn(xt, 3 - k)[SUBLANES:]) for k in range(4)]
        dw = jnp.concatenate(rows + [jnp.zeros((SUBLANES - 4, tc), F32)], axis=0)
        _acc(dw_ref, dw, i == 0)
        _acc(db_ref, _colsum(dyt), i == 0)

    return pl.pallas_call(
        body, name=name, grid=(C // tc, nt),
        in_specs=[pl.BlockSpec((tm, tc), lambda j, i: (i, cb0 + j)),
                  pl.BlockSpec((SUBLANES, tc), lambda j, i: (jnp.maximum(i * hb - 1, 0), cb0 + j)),
                  pl.BlockSpec((SUBLANES, tc), lambda j, i: (jnp.minimum((i + 1) * hb, last_hb), cb0 + j)),
                  pl.BlockSpec((tm, tc), lambda j, i: (i, j)),
                  pl.BlockSpec((SUBLANES, tc), lambda j, i: (jnp.minimum((i + 1) * hb, last_hb), j)),
                  pl.BlockSpec((SUBLANES, tc), lambda j, i: (0, j)),
                  pl.BlockSpec((1, tc), lambda j, i: (0, j))],
        out_specs=[pl.BlockSpec((tm, tc), lambda j, i: (i, j)),
                   pl.BlockSpec((SUBLANES, tc), lambda j, i: (0, j)),
                   pl.BlockSpec((1, tc), lambda j, i: (0, j))],
        out_shape=[_sds((S, C)), _sds((SUBLANES, C)), _sds((1, C))],
        compiler_params=_params(("arbitrary", "arbitrary")),
    )(src, src, src, dyact, dyact, w8, b)


def _lru_ab(pre_a, pre_x, xr, lam):
    r = jax.nn.sigmoid(pre_a)
    g = jax.nn.sigmoid(pre_x)
    log_sig = -_softplus(-lam)
    log_a = LRU_C * r * log_sig
    a = jnp.exp(log_a)
    t = jnp.tanh(log_a)
    mult = jnp.sqrt(jnp.maximum(-2.0 * t / (1.0 - t), 1e-12))
    return a, mult * (g * xr)


def _lru_tail(h, ly, lnw):
    xh, _ = _rms_parts(h * jax.nn.gelu(ly))
    return xh * lnw


def _scan_down(a, b):
    n = a.shape[0]
    row = lax.broadcasted_iota(jnp.int32, a.shape, 0)
    d = 1
    while d < n:
        keep = row >= d
        a_s = jnp.where(keep, _roll_dn(a, d), 1.0)
        b_s = jnp.where(keep, _roll_dn(b, d), 0.0)
        b = a * b_s + b
        a = a * a_s
        d *= 2
    return a, b


def _scan_up(a, b):
    n = a.shape[0]
    row = lax.broadcasted_iota(jnp.int32, a.shape, 0)
    d = 1
    while d < n:
        keep = row < n - d
        a_s = jnp.where(keep, _roll_up(a, d), 1.0)
        b_s = jnp.where(keep, _roll_up(b, d), 0.0)
        b = a * b_s + b
        a = a * a_s
        d *= 2
    return a, b


LRU_TM = 256


def _lru_fwd(xr, proj, wa, ba, wx, bx, lam, lnw):
    S = xr.shape[0]
    tm = min(LRU_TM, S)

    def body(xr_ref, ly_ref, wa_ref, ba_ref, wx_ref, bx_ref, lam_ref, lnw_ref, out_ref, h_ref, carry):
        i = pl.program_id(0)

        @pl.when(i == 0)
        def _():
            carry[...] = jnp.zeros_like(carry)

        x = xr_ref[...]
        xb = x.astype(BF16)
        pre_a = _dot(xb, wa_ref[...]) + ba_ref[...]
        pre_x = _dot(xb, wx_ref[...]) + bx_ref[...]
        a, b = _lru_ab(pre_a, pre_x, x, lam_ref[...])
        ca, hl = _scan_down(a, b)
        h = hl + ca * carry[0:1, :]
        carry[0:1, :] = h[tm - 1:tm, :]
        h_ref[...] = h
        out_ref[...] = _lru_tail(h, ly_ref[...], lnw_ref[...])

    vec = _const((1, LRU_W))
    mat = _const((LRU_W, LRU_W))
    return pl.pallas_call(
        body, name="lru_fwd", grid=(S // tm,),
        in_specs=[_row(tm, LRU_W), _row(tm, LRU_W, 1), mat, vec, mat, vec, vec, vec],
        out_specs=[_row(tm, LRU_W), _row(tm, LRU_W)],
        out_shape=[_sds((S, LRU_W)), _sds((S, LRU_W))],
        scratch_shapes=[pltpu.VMEM((SUBLANES, LRU_W), F32)],
        compiler_params=_params(("arbitrary",)),
    )(xr, proj, wa, ba, wx, bx, lam, lnw)


def _lru_bwd(dout, xr, proj, h, wa, ba, wx, bx, lam, lnw):
    S = xr.shape[0]
    tm = min(LRU_TM, S)
    nt = S // tm
    hb = tm // SUBLANES

    def rev(col=0):
        return pl.BlockSpec((tm, LRU_W), lambda i: (nt - 1 - i, col))

    def body(dout_ref, xr_ref, ly_ref, h_ref, hp_ref, wa_ref, ba_ref, wx_ref, bx_ref, lam_ref, lnw_ref,
             dxr_ref, dly_ref, dwa_ref, dba_ref, dwx_ref, dbx_ref, dlam_ref, dlnw_ref, carry):
        i = pl.program_id(0)
        first = i == 0

        @pl.when(first)
        def _():
            carry[...] = jnp.zeros_like(carry)

        x = xr_ref[...]
        xb = x.astype(BF16)
        pre_a = _dot(xb, wa_ref[...]) + ba_ref[...]
        pre_x = _dot(xb, wx_ref[...]) + bx_ref[...]
        (a, b), ab_vjp = jax.vjp(_lru_ab, pre_a, pre_x, x, lam_ref[...])
        h_t = h_ref[...]
        _, tail_vjp = jax.vjp(_lru_tail, h_t, ly_ref[...], lnw_ref[...])
        dh, dly, dlnw = tail_vjp(dout_ref[...])
        dly_ref[...] = dly
        row = lax.broadcasted_iota(jnp.int32, a.shape, 0)
        a_next = jnp.where(row == tm - 1, carry[0:1, :], _roll_up(a, 1))
        ca, gl = _scan_up(a_next, dh)
        g = gl + ca * carry[1:2, :]
        carry[0:1, :] = a[0:1, :]
        carry[1:2, :] = g[0:1, :]
        h_before = jnp.where(i == nt - 1, 0.0, hp_ref[SUBLANES - 1:SUBLANES, :])
        h_prev = jnp.where(row == 0, h_before, _roll_dn(h_t, 1))
        dpa, dpx, dx, dlam = ab_vjp((g * h_prev, g))
        dpab = dpa.astype(BF16)
        dpxb = dpx.astype(BF16)
        dxr_ref[...] = dx + _dot_nt(dpab, wa_ref[...]) + _dot_nt(dpxb, wx_ref[...])
        _acc(dwa_ref, _dot_tn(xb, dpab), first)
        _acc(dwx_ref, _dot_tn(xb, dpxb), first)
        _acc(dba_ref, _colsum(dpa), first)
        _acc(dbx_ref, _colsum(dpx), first)
        _acc(dlam_ref, dlam, first)
        _acc(dlnw_ref, dlnw, first)

    vec = _const((1, LRU_W))
    mat = _const((LRU_W, LRU_W))
    return pl.pallas_call(
        body, name="lru_bwd", grid=(nt,),
        in_specs=[rev(), rev(), rev(1), rev(),
                  pl.BlockSpec((SUBLANES, LRU_W), lambda i: (jnp.maximum((nt - 1 - i) * hb - 1, 0), 0)),
                  mat, vec, mat, vec, vec, vec],
        out_specs=[rev(), rev(), mat, vec, mat, vec, vec, vec],
        out_shape=[_sds((S, LRU_W)), _sds((S, LRU_W)), _sds((LRU_W, LRU_W)), _sds((1, LRU_W)),
                   _sds((LRU_W, LRU_W)), _sds((1, LRU_W)), _sds((1, LRU_W)), _sds((1, LRU_W))],
        scratch_shapes=[pltpu.VMEM((SUBLANES, LRU_W), F32)],
        compiler_params=_params(("arbitrary",)),
    )(dout, xr, proj, h, h, wa, ba, wx, bx, lam, lnw)


def _lane_pick(row_or_tile, lane):
    idx = lax.broadcasted_iota(jnp.int32, row_or_tile.shape, 1)
    return jnp.sum(jnp.where(idx == lane, row_or_tile, 0.0), axis=-1, keepdims=True)


def _unit_lower_inverses(los):
    n = los[0].shape[0]
    ri = lax.broadcasted_iota(jnp.int32, (n, n), 0)
    ci = lax.broadcasted_iota(jnp.int32, (n, n), 1)
    invs = [(ri == ci).astype(F32) for _ in los]
    s = 1
    while s < n:
        same_block = (ri & ~(2 * s - 1)) == (ci & ~(2 * s - 1))
        lower_left = same_block & ((ri & s) != 0) & ((ci & s) == 0)
        left = [_mm_raw(inv, jnp.where(lower_left, lo, 0.0), "nn", 3) for inv, lo in zip(invs, los)]
        invs = [inv - _mm_raw(t, inv, "nn", 3) for inv, t in zip(invs, left)]
        s *= 2
    return invs


@jax.custom_vjp
def _unit_lower_inverses_diff(los):
    return _unit_lower_inverses(los)


def _unit_lower_inverses_fwd(los):
    invs = _unit_lower_inverses(los)
    return invs, invs


def _unit_lower_inverses_bwd(invs, cts):
    right = [_mm_raw(ct, inv, "nt", 3) for ct, inv in zip(cts, invs)]
    return ([-_mm_raw(inv, r, "tn", 3) for inv, r in zip(invs, right)],)


_unit_lower_inverses_diff.defvjp(_unit_lower_inverses_fwd, _unit_lower_inverses_bwd)


def _gdn_chunk(qs, ks, vs, ba, alog, dtb, states, inverses=_unit_lower_inverses, mm=_mm_raw):
    C = qs[0].shape[0]
    heads = range(len(qs))
    ri = lax.broadcasted_iota(jnp.int32, (C, C), 0)
    ci = lax.broadcasted_iota(jnp.int32, (C, C), 1)
    causal = ri >= ci
    strict = ri > ci
    tri = causal.astype(F32)
    betas = [jax.nn.sigmoid(_lane_pick(ba, h)) for h in heads]
    gs = [-jnp.exp(_lane_pick(alog, h)) * _softplus(_lane_pick(ba, h + HEADS) + _lane_pick(dtb, h)) for h in heads]
    qn = [q * lax.rsqrt(jnp.sum(q * q, axis=-1, keepdims=True) + 1e-6) * (HEAD_DIM ** -0.5) for q in qs]
    kn = [k * lax.rsqrt(jnp.sum(k * k, axis=-1, keepdims=True) + 1e-6) for k in ks]
    gc = [_hdot(tri, jnp.broadcast_to(g, (C, C))) for g in gs]
    decay = [jnp.where(causal, jnp.exp(jnp.where(causal, c - c.T, 0.0)), 0.0) for c in gc]
    eg = [jnp.exp(c) for c in gc]
    kb = [k * b for k, b in zip(kn, betas)]
    vb = [v * b for v, b in zip(vs, betas)]
    los = [jnp.where(strict, mm(a, k, "nt", 1) * d, 0.0) for a, k, d in zip(kb, kn, decay)]
    attn = [jnp.where(causal, mm(q, k, "nt", 1) * d, 0.0) for q, k, d in zip(qn, kn, decay)]
    tinv = inverses(los)
    u = [mm(t, x, "nn", 3) for t, x in zip(tinv, vb)]
    w = [mm(t, a * e, "nn", 3) for t, a, e in zip(tinv, kb, eg)]
    g_last = [c[C - 1:C, :] for c in gc]
    k_tail = [k * jnp.exp(gl - c) for k, gl, c in zip(kn, g_last, gc)]
    v_new = [a - mm(b, s, "nn", 1) for a, b, s in zip(u, w, states)]
    o_state = [mm(q * e, s, "nn", 1) for q, e, s in zip(qn, eg, states)]
    o = [a + mm(at, vn, "nn", 1) for a, at, vn in zip(o_state, attn, v_new)]
    new_states = [s * jnp.exp(gl) + mm(kt, vn, "tn", 1) for s, gl, kt, vn in zip(states, g_last, k_tail, v_new)]
    return o, new_states


def _gdn_fwd(qkv, proj, alog, dtb):
    S = qkv.shape[0]
    nc = S // CHUNK
    assert CHUNK == HEAD_DIM

    def body(q_ref, k_ref, v_ref, ba_ref, alog_ref, dtb_ref, o_ref, st_ref, state):
        @pl.when(pl.program_id(0) == 0)
        def _():
            state[...] = jnp.zeros_like(state)

        sls = [slice(hd * HEAD_DIM, (hd + 1) * HEAD_DIM) for hd in range(HEADS)]
        s0 = [state[hd] for hd in range(HEADS)]
        for hd in range(HEADS):
            st_ref[hd, 0] = s0[hd]
        o, s1 = _gdn_chunk([q_ref[:, sl] for sl in sls], [k_ref[:, sl] for sl in sls], [v_ref[:, sl] for sl in sls],
                           ba_ref[...], alog_ref[...], dtb_ref[...], s0)
        for hd in range(HEADS):
            o_ref[:, sls[hd]] = o[hd]
            state[hd] = s1[hd]

    def col(j):
        return pl.BlockSpec((CHUNK, GDN_W), lambda n: (n, j))

    vec = _const((1, LANES))
    return pl.pallas_call(
        body, name="gdn_fwd", grid=(nc,),
        in_specs=[col(0), col(1), col(2), pl.BlockSpec((CHUNK, LANES), lambda n: (n, IN_PAD // LANES - 1)), vec, vec],
        out_specs=[col(0), pl.BlockSpec((HEADS, 1, HEAD_DIM, HEAD_DIM), lambda n: (0, n, 0, 0))],
        out_shape=[_sds((S, GDN_W)), _sds((HEADS, nc, HEAD_DIM, HEAD_DIM))],
        scratch_shapes=[pltpu.VMEM((HEADS, HEAD_DIM, HEAD_DIM), F32)],
        compiler_params=_params(("arbitrary",)),
    )(qkv, qkv, qkv, proj, alog, dtb)


def _gdn_bwd(do, qkv, proj, states, alog, dtb):
    S = qkv.shape[0]
    nc = S // CHUNK

    def body(do_ref, q_ref, k_ref, v_ref, ba_ref, st_ref, alog_ref, dtb_ref,
             dqkv_ref, dba_ref, dalog_ref, ddtb_ref, dstate):
        n = pl.program_id(0)

        @pl.when(n == 0)
        def _():
            dstate[...] = jnp.zeros_like(dstate)

        sls = [slice(hd * HEAD_DIM, (hd + 1) * HEAD_DIM) for hd in range(HEADS)]
        fn = functools.partial(_gdn_chunk, inverses=_unit_lower_inverses_diff, mm=_mm)
        _, vjp = jax.vjp(fn, [q_ref[:, sl] for sl in sls], [k_ref[:, sl] for sl in sls], [v_ref[:, sl] for sl in sls],
                         ba_ref[...], alog_ref[...], dtb_ref[...], [st_ref[hd, 0] for hd in range(HEADS)])
        dq, dk, dv, dba, dalog, ddtb, ds = vjp(([do_ref[:, sl] for sl in sls], [dstate[hd] for hd in range(HEADS)]))
        for hd in range(HEADS):
            dqkv_ref[:, sls[hd]] = dq[hd]
            dqkv_ref[:, GDN_W + hd * HEAD_DIM:GDN_W + (hd + 1) * HEAD_DIM] = dk[hd]
            dqkv_ref[:, 2 * GDN_W + hd * HEAD_DIM:2 * GDN_W + (hd + 1) * HEAD_DIM] = dv[hd]
            dstate[hd] = ds[hd]
        dba_ref[...] = dba
        _acc(dalog_ref, dalog, n == 0)
        _acc(ddtb_ref, ddtb, n == 0)

    def col(j):
        return pl.BlockSpec((CHUNK, GDN_W), lambda n: (nc - 1 - n, j))

    vec = _const((1, LANES))
    return pl.pallas_call(
        body, name="gdn_bwd", grid=(nc,),
        in_specs=[col(0), col(0), col(1), col(2),
                  pl.BlockSpec((CHUNK, LANES), lambda n: (nc - 1 - n, IN_PAD // LANES - 1)),
                  pl.BlockSpec((HEADS, 1, HEAD_DIM, HEAD_DIM), lambda n: (0, nc - 1 - n, 0, 0)), vec, vec],
        out_specs=[pl.BlockSpec((CHUNK, 3 * GDN_W), lambda n: (nc - 1 - n, 0)),
                   pl.BlockSpec((CHUNK, LANES), lambda n: (nc - 1 - n, 0)), vec, vec],
        out_shape=[_sds((S, 3 * GDN_W)), _sds((S, LANES)), _sds((1, LANES)), _sds((1, LANES))],
        scratch_shapes=[pltpu.VMEM((HEADS, HEAD_DIM, HEAD_DIM), F32)],
        compiler_params=_params(("arbitrary",)),
    )(do, qkv, qkv, qkv, proj, states, alog, dtb)


def _gdn_gate(o, z, gnw):
    outs = []
    for hd in range(HEADS):
        sl = slice(hd * HEAD_DIM, (hd + 1) * HEAD_DIM)
        xh, _ = _rms_parts(o[:, sl])
        outs.append(xh * gnw * _silu(z[:, sl]))
    return jnp.concatenate(outs, axis=-1)


def _out_fwd(x, out_lru, o, proj, gnw, g1, wout):
    S = x.shape[0]
    tm = min(512, S)

    def body(x_ref, lru_ref, o_ref, z_ref, gnw_ref, g1_ref, w_ref, x1_ref, cat_ref):
        cat = jnp.concatenate([lru_ref[...], _gdn_gate(o_ref[...], z_ref[...], gnw_ref[...])], axis=-1).astype(BF16)
        cat_ref[...] = cat
        x1_ref[...] = x_ref[...] + g1_ref[...] * _dot(cat, w_ref[...])

    return pl.pallas_call(
        body, name="out_fwd", grid=(S // tm,),
        in_specs=[_row(tm, D_MODEL), _row(tm, LRU_W), _row(tm, GDN_W), _row(tm, GDN_W, 5), _const((1, LANES)),
                  _const((1, D_MODEL)), _const((D_MODEL, D_MODEL))],
        out_specs=[_row(tm, D_MODEL), _row(tm, D_MODEL)],
        out_shape=[_sds((S, D_MODEL)), _sds((S, D_MODEL), BF16)],
        compiler_params=_params(("arbitrary",)),
    )(x, out_lru, o, proj, gnw, g1, wout)


def _out_bwd(dx1, cat, o, proj, gnw, g1, wout):
    S = dx1.shape[0]
    tm = min(512, S)

    def body(dx1_ref, cat_ref, o_ref, z_ref, gnw_ref, g1_ref, w_ref,
             dlru_ref, do_ref, dz_ref, dmb_ref, dgnw_ref, dg1_ref):
        i = pl.program_id(0)
        d1 = dx1_ref[...]
        mix = _dot(cat_ref[...], w_ref[...])
        _acc(dg1_ref, _colsum(d1 * mix), i == 0)
        dmb = (d1 * g1_ref[...]).astype(BF16)
        dmb_ref[...] = dmb
        dcat = _dot_nt(dmb, w_ref[...])
        dlru_ref[...] = dcat[:, :LRU_W]
        _, vjp = jax.vjp(_gdn_gate, o_ref[...], z_ref[...], gnw_ref[...])
        do, dz, dgnw = vjp(dcat[:, LRU_W:])
        do_ref[...] = do
        dz_ref[...] = dz
        _acc(dgnw_ref, dgnw, i == 0)

    return pl.pallas_call(
        body, name="out_bwd", grid=(S // tm,),
        in_specs=[_row(tm, D_MODEL), _row(tm, D_MODEL), _row(tm, GDN_W), _row(tm, GDN_W, 5), _const((1, LANES)),
                  _const((1, D_MODEL)), _const((D_MODEL, D_MODEL))],
        out_specs=[_row(tm, LRU_W), _row(tm, GDN_W), _row(tm, GDN_W), _row(tm, D_MODEL), _const((1, LANES)),
                   _const((1, D_MODEL))],
        out_shape=[_sds((S, LRU_W)), _sds((S, GDN_W)), _sds((S, GDN_W)), _sds((S, D_MODEL), BF16), _sds((1, LANES)),
                   _sds((1, D_MODEL))],
        compiler_params=_params(("arbitrary",)),
    )(dx1, cat, o, proj, gnw, g1, wout)


MLP_TM = 256


def _load_once(step, pairs, sem):
    @pl.when(step == 0)
    def _():
        copies = [pltpu.make_async_copy(src, dst, sem.at[k]) for k, (src, dst) in enumerate(pairs)]
        for cp in copies:
            cp.start()
        for cp in copies:
            cp.wait()


def _mlp_fwd(x1, nw, sc, sh, g2, wup, wdown):
    S = x1.shape[0]
    tm = min(MLP_TM, S)

    def body(x_ref, nw_ref, sc_ref, sh_ref, g2_ref, wup_hbm, wdown_hbm, x2_ref, wup, wdown, sem):
        _load_once(pl.program_id(0), [(wup_hbm, wup), (wdown_hbm, wdown)], sem)
        x = x_ref[...]
        hb = _norm_mod(x, nw_ref[...], sc_ref[...], sh_ref[...]).astype(BF16)
        r = jnp.maximum(_dot(hb, wup[...]), 0.0)
        x2_ref[...] = x + g2_ref[...] * _dot((r * r).astype(BF16), wdown[...])

    vec = _const((1, D_MODEL))
    anyspec = pl.BlockSpec(memory_space=pl.ANY)
    return pl.pallas_call(
        body, name="mlp_fwd", grid=(S // tm,),
        in_specs=[_row(tm, D_MODEL), vec, vec, vec, vec, anyspec, anyspec],
        out_specs=_row(tm, D_MODEL),
        out_shape=_sds((S, D_MODEL)),
        scratch_shapes=[pltpu.VMEM((D_MODEL, D_FF), BF16), pltpu.VMEM((D_FF, D_MODEL), BF16),
                        pltpu.SemaphoreType.DMA((2,))],
        compiler_params=_params(("arbitrary",)),
    )(x1, nw, sc, sh, g2, wup, wdown)


def _mlp_bwd(dx2, x1, nw, sc, sh, g2, wup, wdown):
    S = x1.shape[0]
    tm = min(MLP_TM, S)

    def body(dx2_ref, x_ref, nw_ref, sc_ref, sh_ref, g2_ref, wup_hbm, wdown_hbm,
             dx1_ref, hb_ref, dupb_ref, actb_ref, ddb_ref, dnw_ref, dsc_ref, dsh_ref, dg2_ref, wup, wdown, sem):
        i = pl.program_id(0)
        _load_once(i, [(wup_hbm, wup), (wdown_hbm, wdown)], sem)
        x = x_ref[...]
        d2 = dx2_ref[...]
        hb = _norm_mod(x, nw_ref[...], sc_ref[...], sh_ref[...]).astype(BF16)
        hb_ref[...] = hb
        r = jnp.maximum(_dot(hb, wup[...]), 0.0)
        actb = (r * r).astype(BF16)
        actb_ref[...] = actb
        down = _dot(actb, wdown[...])
        _acc(dg2_ref, _colsum(d2 * down), i == 0)
        ddb = (d2 * g2_ref[...]).astype(BF16)
        ddb_ref[...] = ddb
        dupb = (_dot_nt(ddb, wdown[...]) * (2.0 * r)).astype(BF16)
        dupb_ref[...] = dupb
        dh = _dot_nt(dupb, wup[...])
        dx, dnw, dsc, dsh = _norm_mod_bwd(dh, x, nw_ref[...], sc_ref[...])
        dx1_ref[...] = d2 + dx
        _acc(dnw_ref, dnw, i == 0)
        _acc(dsc_ref, dsc, i == 0)
        _acc(dsh_ref, dsh, i == 0)

    vec = _const((1, D_MODEL))
    anyspec = pl.BlockSpec(memory_space=pl.ANY)
    return pl.pallas_call(
        body, name="mlp_bwd", grid=(S // tm,),
        in_specs=[_row(tm, D_MODEL), _row(tm, D_MODEL), vec, vec, vec, vec, anyspec, anyspec],
        out_specs=[_row(tm, D_MODEL), _row(tm, D_MODEL), _row(tm, D_FF), _row(tm, D_FF), _row(tm, D_MODEL),
                   vec, vec, vec, vec],
        out_shape=[_sds((S, D_MODEL)), _sds((S, D_MODEL), BF16), _sds((S, D_FF), BF16), _sds((S, D_FF), BF16),
                   _sds((S, D_MODEL), BF16), _sds((1, D_MODEL)), _sds((1, D_MODEL)), _sds((1, D_MODEL)),
                   _sds((1, D_MODEL))],
        scratch_shapes=[pltpu.VMEM((D_MODEL, D_FF), BF16), pltpu.VMEM((D_FF, D_MODEL), BF16),
                        pltpu.SemaphoreType.DMA((2,))],
        compiler_params=_params(("arbitrary",)),
    )(dx2, x1, nw, sc, sh, g2, wup, wdown)


def _matmul_tn(a, b, name, shards=None):
    K, M = a.shape
    N = b.shape[1]
    tk = min(512, K)
    if shards == "cols":
        tm, tn = M // 2, N // N_CHIPS
        out_spec = pl.BlockSpec((1, 1, tm, tn), lambda i, j, k: (i, j, 0, 0))
        out_shape = _sds((2, N_CHIPS, tm, tn))
    elif shards == "rows":
        h, tn = M // (2 * N_CHIPS), N
        tm = max(512, 2 * h)
        per_tile = tm // (2 * h)
        out_spec = pl.BlockSpec((2, per_tile, h, tn), lambda i, j, k: (0, i, 0, 0))
        out_shape = _sds((2, N_CHIPS, h, tn))
    else:
        tm = min(512, M)
        tn = 640 if N % 640 == 0 else min(1024, N)
        out_spec = pl.BlockSpec((tm, tn), lambda i, j, k: (i, j))
        out_shape = _sds((M, N))
    nk = K // tk

    def body(a_ref, b_ref, o_ref, acc):
        k = pl.program_id(2)
        _acc(acc, _dot_tn(a_ref[...], b_ref[...]), k == 0)

        @pl.when(k == nk - 1)
        def _():
            if shards == "rows":
                for s in range(per_tile):
                    for half in range(2):
                        r0 = (2 * s + half) * h
                        o_ref[half, s] = acc[r0:r0 + h, :]
            else:
                o_ref[...] = acc[...].reshape(o_ref.shape)

    return pl.pallas_call(
        body, name=name, grid=(M // tm, N // tn, nk),
        in_specs=[pl.BlockSpec((tk, tm), lambda i, j, k: (k, i)), pl.BlockSpec((tk, tn), lambda i, j, k: (k, j))],
        out_specs=out_spec, out_shape=out_shape,
        scratch_shapes=[pltpu.VMEM((tm, tn), F32)],
        compiler_params=_params(("arbitrary", "arbitrary", "arbitrary")),
    )(a, b)


def _loss_head(x, target, fnw):
    S = x.shape[0]
    tm = min(512, S)

    def body(x_ref, t_ref, w_ref, dx_ref, loss_ref, dw_ref):
        i = pl.program_id(0)
        w = w_ref[...]
        xh, r = _rms_parts(x_ref[...])
        err = xh * w - t_ref[...]
        part = 0.5 * jnp.sum(jnp.mean(err * err, axis=-1, keepdims=True), axis=0, keepdims=True)
        _acc(loss_ref, jnp.broadcast_to(part, (SUBLANES, LANES)), i == 0)
        dx, dw = _rms_bwd(err * (1.0 / D_MODEL), xh, r, w)
        dx_ref[...] = dx
        _acc(dw_ref, dw, i == 0)

    vec = _const((1, D_MODEL))
    return pl.pallas_call(
        body, name="loss_head", grid=(S // tm,),
        in_specs=[_row(tm, D_MODEL), _row(tm, D_MODEL), vec],
        out_specs=[_row(tm, D_MODEL), _const((SUBLANES, LANES)), vec],
        out_shape=[_sds((S, D_MODEL)), _sds((SUBLANES, LANES)), _sds((1, D_MODEL))],
        compiler_params=_params(("arbitrary",)),
    )(x, target, fnw)


def _block_diag(w):
    eye = jnp.eye(LRU_BLOCKS, dtype=w.dtype)
    return (eye[:, None, :, None] * w[:, :, None, :]).reshape(LRU_W, LRU_W)


def _diag_blocks(m):
    m4 = m.reshape(LRU_BLOCKS, LRU_BLOCK, LRU_BLOCKS, LRU_BLOCK)
    return jnp.stack([m4[g, :, g, :] for g in range(LRU_BLOCKS)])


def _layer_fwd(x, p):
    proj, h1b = _proj_fwd(x, p["nmw"], p["sc1"], p["sh1"], p["win"])
    xr = _conv_fwd(proj, 0, LRU_W, p["lcw"], p["lcb"], False, "conv_lru_fwd")
    out_lru, h = _lru_fwd(xr, proj, p["wa"].astype(BF16), p["ba"], p["wx"].astype(BF16), p["bx"], p["lam"], p["lnw"])
    qkv = _conv_fwd(proj, 2 * LRU_W, 3 * GDN_W, p["gcw"], p["gcb"], True, "conv_gdn_fwd")
    o, states = _gdn_fwd(qkv, proj, p["alog"], p["dtb"])
    x1, cat = _out_fwd(x, out_lru, o, proj, p["gnw"], p["g1"], p["wout"])
    x2 = _mlp_fwd(x1, p["nmlp"], p["sc2"], p["sh2"], p["g2"], p["wup"], p["wdown"])
    res = dict(x=x, proj=proj, h1b=h1b, xr=xr, h=h, qkv=qkv, o=o, states=states, x1=x1, cat=cat)
    return x2, res


def _layer_bwd(dx2, p, r, sharded=False):
    dx1, h2b, dupb, actb, ddb, dnmlp, dsc2, dsh2, dg2 = _mlp_bwd(
        dx2, r["x1"], p["nmlp"], p["sc2"], p["sh2"], p["g2"], p["wup"], p["wdown"])
    g_wup = _matmul_tn(h2b, dupb, "dw_up", "cols" if sharded else None)
    g_wdown = _matmul_tn(actb, ddb, "dw_down", "rows" if sharded else None)
    dlru, do, dz, dmb, dgnw, dg1 = _out_bwd(dx1, r["cat"], r["o"], r["proj"], p["gnw"], p["g1"], p["wout"])
    g_wout = _matmul_tn(r["cat"], dmb, "dw_out", "rows" if sharded else None)
    dqkv_act, dba, dalog, ddtb = _gdn_bwd(do, r["qkv"], r["proj"], r["states"], p["alog"], p["dtb"])
    dqkv, dgcw, _ = _conv_bwd(r["proj"], 2 * LRU_W, 3 * GDN_W, p["gcw"], p["gcb"], dqkv_act, True, "conv_gdn_bwd")
    wab = p["wa"].astype(BF16)
    wxb = p["wx"].astype(BF16)
    dxr, dly, dwa, dba_, dwx, dbx, dlam, dlnw = _lru_bwd(
        dlru, r["xr"], r["proj"], r["h"], wab, p["ba"], wxb, p["bx"], p["lam"], p["lnw"])
    dlx, dlcw, dlcb = _conv_bwd(r["proj"], 0, LRU_W, p["lcw"], p["lcb"], dxr, False, "conv_lru_bwd")
    dx, dpb, dnmw, dsc1, dsh1 = _proj_bwd(dx1, r["x"], dlx, dly, dqkv, dz, dba, p["nmw"], p["sc1"], p["win"])
    g_win = _matmul_tn(r["h1b"], dpb, "dw_in")
    grads = dict(nmw=dnmw, nmlp=dnmlp, sh1=dsh1, sc1=dsc1, g1=dg1, sh2=dsh2, sc2=dsc2, g2=dg2,
                 win=g_win, lcw=dlcw, lcb=dlcb, wa=dwa, ba=dba_, wx=dwx, bx=dbx, lam=dlam, lnw=dlnw,
                 gcw=dgcw, alog=dalog, dtb=ddtb, gnw=dgnw, wout=g_wout, wup=g_wup, wdown=g_wdown)
    return dx, grads


def _local_step(x, target, fnw, layers):
    res = []
    for p in layers:
        x, r = _layer_fwd(x, p)
        res.append(r)
    dx, loss_blk, dfnw = _loss_head(x, target, fnw)
    grads = [None] * len(layers)
    for l in reversed(range(len(layers))):
        dx, grads[l] = _layer_bwd(dx, layers[l], res[l])
    stacked = {k: jnp.stack([g[k] for g in grads]) for k in grads[0]}
    return loss_blk[0, 0], dx, dfnw, stacked


def _prep_layers(norm_mix_w, norm_mlp_w, mod, win_b, lru_conv_w, lru_conv_b, gate_a_w, gate_a_b, gate_x_w, gate_x_b,
                 lru_lambda, lru_norm_w, gdn_conv_w, gdn_a_log, gdn_dt_bias, gdn_norm_w, wout_b, wup_b, wdown_b):
    L = norm_mix_w.shape[0]

    def vec(a):
        return a.reshape(L, 1, -1)

    def lanes(a):
        return jnp.pad(a, ((0, 0), (0, LANES - a.shape[1]))).reshape(L, 1, LANES)

    def taps(w):
        return jnp.pad(w, ((0, 0), (0, SUBLANES - w.shape[1]), (0, 0)))

    m = mod.reshape(L, N_MOD, 1, D_MODEL)
    return dict(
        nmw=vec(norm_mix_w), nmlp=vec(norm_mlp_w),
        sh1=m[:, 0], sc1=m[:, 1], g1=m[:, 2], sh2=m[:, 3], sc2=m[:, 4], g2=m[:, 5],
        win=win_b, lcw=taps(lru_conv_w), lcb=vec(lru_conv_b),
        wa=jax.vmap(_block_diag)(gate_a_w), ba=vec(gate_a_b), wx=jax.vmap(_block_diag)(gate_x_w), bx=vec(gate_x_b),
        lam=vec(lru_lambda), lnw=vec(lru_norm_w),
        gcw=taps(gdn_conv_w), gcb=jnp.zeros((L, 1, 3 * GDN_W), F32),
        alog=lanes(gdn_a_log), dtb=lanes(gdn_dt_bias), gnw=vec(gdn_norm_w),
        wout=wout_b, wup=wup_b, wdown=wdown_b)


def _position():
    x, y, c = lax.axis_index("x"), lax.axis_index("y"), lax.axis_index("c")
    return x, y, c


def _other_chips(x, y):
    return [(1 - x, y), (x, 1 - y), (1 - x, 1 - y)]


def _all_gather_rows(block, name):
    m, n = block.shape

    def body(x_ref, out_ref, send_sems, recv_sems, local_sem):
        x, y, c = _position()
        me, sibling = (x, y, c), (x, y, 1 - c)
        chips = _other_chips(x, y)

        def rows(px, py, pc):
            return out_ref.at[pl.ds((4 * px + 2 * py + pc) * m, m), :]

        def copy(k, blk, to, src=None):
            return pltpu.make_async_remote_copy(
                src_ref=rows(*blk) if src is None else src, dst_ref=rows(*blk),
                send_sem=send_sems.at[k], recv_sem=recv_sems.at[k], device_id=to, device_id_type=MESH)

        mine = pltpu.make_async_copy(x_ref, rows(*me), local_sem)
        mine.start()
        first = [copy(0, me, sibling, src=x_ref)]
        first += [copy(1 + j, me, (*chip, c), src=x_ref) for j, chip in enumerate(chips)]
        for cp in first:
            cp.start()
        passed = [copy(4 + j, (*chip, c), sibling) for j, chip in enumerate(chips)]
        for j, chip in enumerate(chips):
            copy(1 + j, (*chip, c), me).wait_recv()
            passed[j].start()
        copy(0, sibling, me).wait_recv()
        for j, chip in enumerate(chips):
            copy(4 + j, (*chip, 1 - c), me).wait_recv()
        for cp in first + passed:
            cp.wait_send()
        mine.wait()

    return pl.pallas_call(
        body, name=name,
        out_shape=_sds((N_DEV * m, n)),
        in_specs=[pl.BlockSpec(memory_space=pltpu.VMEM)],
        out_specs=pl.BlockSpec(memory_space=pltpu.VMEM),
        scratch_shapes=[pltpu.SemaphoreType.DMA((7,)), pltpu.SemaphoreType.DMA((7,)), pltpu.SemaphoreType.DMA],
        compiler_params=pltpu.CompilerParams(vmem_limit_bytes=VMEM_LIMIT),
    )(block)


def _hbm_specs(n):
    return [pl.BlockSpec(memory_space=pl.ANY)] * n


def _gather_chips(shards, name):
    n = len(shards)

    def body(*refs):
        ins, outs = refs[:n], refs[n:2 * n]
        send_sems, recv_sems = refs[2 * n:]
        x, y, c = _position()
        chips = _other_chips(x, y)
        me = 2 * x + y

        def first(a, j, slot):
            h = ins[a].shape[0] // 2
            return pltpu.make_async_remote_copy(
                src_ref=ins[a].at[pl.ds(pl.multiple_of(c * h, SUBLANES), h)], dst_ref=outs[a].at[slot, c],
                send_sem=send_sems.at[3 * a + j], recv_sem=recv_sems.at[3 * a + j],
                device_id=(chips[j][0], chips[j][1], c), device_id_type=MESH)

        def second(a, j, half):
            slot = 2 * chips[j][0] + chips[j][1]
            return pltpu.make_async_remote_copy(
                src_ref=outs[a].at[slot, c], dst_ref=outs[a].at[slot, half],
                send_sem=send_sems.at[3 * (n + a) + j], recv_sem=recv_sems.at[3 * (n + a) + j],
                device_id=(x, y, 1 - c), device_id_type=MESH)

        sends = [first(a, j, me) for a in range(n) for j in range(3)]
        for cp in sends:
            cp.start()
        for a in range(n):
            for j, (px, py) in enumerate(chips):
                first(a, j, 2 * px + py).wait_recv()
                passed = second(a, j, c)
                passed.start()
                sends.append(passed)
        for a in range(n):
            for j in range(3):
                second(a, j, 1 - c).wait_recv()
        for cp in sends:
            cp.wait_send()

    return pl.pallas_call(
        body, name=name,
        out_shape=[_sds((N_CHIPS, 2, s.shape[0] // 2, s.shape[1]), s.dtype) for s in shards],
        in_specs=_hbm_specs(n), out_specs=_hbm_specs(n),
        scratch_shapes=[pltpu.SemaphoreType.DMA((6 * n,)), pltpu.SemaphoreType.DMA((6 * n,))],
    )(*shards)


def _send_to_sibling(parts, name):
    n = len(parts)

    def body(*refs):
        ins, outs = refs[:n], refs[n:2 * n]
        send_sems, recv_sems = refs[2 * n:]
        x, y, c = _position()
        copies = [pltpu.make_async_remote_copy(
            src_ref=ins[a].at[1 - c], dst_ref=outs[a], send_sem=send_sems.at[a], recv_sem=recv_sems.at[a],
            device_id=(x, y, 1 - c), device_id_type=MESH) for a in range(n)]
        for cp in copies:
            cp.start()
        for cp in copies:
            cp.wait()

    return pl.pallas_call(
        body, name=name,
        out_shape=[_sds(p.shape[1:], p.dtype) for p in parts],
        in_specs=_hbm_specs(n), out_specs=_hbm_specs(n),
        scratch_shapes=[pltpu.SemaphoreType.DMA((n,)), pltpu.SemaphoreType.DMA((n,))],
    )(*parts)


def _scatter_chips(parts, name):
    n = len(parts)

    def body(*refs):
        ins, outs = refs[:n], refs[n:2 * n]
        send_sems, recv_sems = refs[2 * n:]
        x, y, c = _position()
        chips = _other_chips(x, y)
        me = 2 * x + y

        def copy(a, j, src_slot, dst_slot):
            px, py = chips[j]
            return pltpu.make_async_remote_copy(
                src_ref=ins[a].at[src_slot], dst_ref=outs[a].at[dst_slot], send_sem=send_sems.at[3 * a + j],
                recv_sem=recv_sems.at[3 * a + j], device_id=(px, py, c), device_id_type=MESH)

        sends = [copy(a, j, 2 * chips[j][0] + chips[j][1], me) for a in range(n) for j in range(3)]
        for cp in sends:
            cp.start()
        for a in range(n):
            for j, (px, py) in enumerate(chips):
                copy(a, j, me, 2 * px + py).wait_recv()
        for cp in sends:
            cp.wait_send()

    return pl.pallas_call(
        body, name=name,
        out_shape=[_sds(p.shape, p.dtype) for p in parts],
        in_specs=_hbm_specs(n), out_specs=_hbm_specs(n),
        scratch_shapes=[pltpu.SemaphoreType.DMA((3 * n,)), pltpu.SemaphoreType.DMA((3 * n,))],
    )(*parts)


def _swap_row_halves(arrays, name):
    n = len(arrays)

    def body(*refs):
        outs = refs[n:2 * n]
        send_sems, recv_sems = refs[2 * n:]
        x, y, c = _position()

        def copy(a, half):
            h = outs[a].shape[0] // 2
            rows = outs[a].at[pl.ds(pl.multiple_of(half * h, SUBLANES), h)]
            return pltpu.make_async_remote_copy(
                src_ref=rows, dst_ref=rows, send_sem=send_sems.at[a], recv_sem=recv_sems.at[a],
                device_id=(x, y, 1 - c), device_id_type=MESH)

        sends = [copy(a, c) for a in range(n)]
        for cp in sends:
            cp.start()
        for a in range(n):
            copy(a, 1 - c).wait_recv()
        for cp in sends:
            cp.wait_send()

    return pl.pallas_call(
        body, name=name,
        out_shape=[_sds(a.shape, a.dtype) for a in arrays],
        in_specs=_hbm_specs(n), out_specs=_hbm_specs(n),
        input_output_aliases={a: a for a in range(n)},
        scratch_shapes=[pltpu.SemaphoreType.DMA((n,)), pltpu.SemaphoreType.DMA((n,))],
    )(*arrays)


def _row_tile(rows):
    for t in (512, 256, 128, 64, 32, 16, 8):
        if rows % t == 0:
            return t
    return rows


def _sum_slots(buf, name):
    k, rows, cols = buf.shape
    tm = _row_tile(rows)

    def body(b_ref, o_ref):
        s = b_ref[0]
        for i in range(1, k):
            s = s + b_ref[i]
        o_ref[...] = s

    return pl.pallas_call(
        body, name=name, grid=(rows // tm,),
        in_specs=[pl.BlockSpec((k, tm, cols), lambda i: (0, i, 0))],
        out_specs=pl.BlockSpec((tm, cols), lambda i: (i, 0)),
        out_shape=_sds((rows, cols)),
        compiler_params=_params(("arbitrary",)),
    )(buf)


def _pair_add(part, from_sibling, core, name):
    _, k, h, cols = part.shape
    rows = k * h
    tm = _row_tile(rows)

    def body(core_ref, a_ref, b_ref, o_ref):
        o_ref[...] = (a_ref[0] + b_ref[...]).astype(BF16)

    out = pl.pallas_call(
        body, name=name,
        grid_spec=pltpu.PrefetchScalarGridSpec(
            num_scalar_prefetch=1, grid=(rows // tm,),
            in_specs=[pl.BlockSpec((1, tm, cols), lambda i, cr: (cr[0], i, 0)),
                      pl.BlockSpec((tm, cols), lambda i, cr: (i, 0))],
            out_specs=pl.BlockSpec((tm, cols), lambda i, cr: (i, 0))),
        out_shape=_sds((rows, cols), BF16),
        compiler_params=_params(("arbitrary",)),
    )(core, part.reshape(2, rows, cols), from_sibling.reshape(rows, cols))
    return out.reshape(k, h, cols)


def _sum_adam(arrived, own, w, m, v, layer, place, name):
    _, h, cols = arrived.shape
    tm = min(256, h)
    nb = h // tm

    def body(place_ref, arr_ref, own_ref, w_ref, m_ref, v_ref, g_ref, d_ref, nm_ref, nv_ref):
        for chip in range(N_CHIPS):
            @pl.when(place_ref[1] == chip)
            def _():
                terms = [own_ref[0] if j == chip else arr_ref[j] for j in range(N_CHIPS)]
                g = terms[0].astype(F32)
                for t in terms[1:]:
                    g = g + t.astype(F32)
                g_ref[...] = g
                d, nm, nv = _adam_math(w_ref[0], g, m_ref[0], v_ref[0])
                d_ref[...] = d
                nm_ref[...] = nm
                nv_ref[...] = nv

    state = pl.BlockSpec((1, tm, cols), lambda i, pr: (layer, pr[0] * nb + i, 0))
    result = pl.BlockSpec((tm, cols), lambda i, pr: (pr[0] * nb + i, 0))
    return pl.pallas_call(
        body, name=name,
        grid_spec=pltpu.PrefetchScalarGridSpec(
            num_scalar_prefetch=1, grid=(nb,),
            in_specs=[pl.BlockSpec((N_CHIPS, tm, cols), lambda i, pr: (0, i, 0)),
                      pl.BlockSpec((1, tm, cols), lambda i, pr: (pr[1], i, 0)), state, state, state],
            out_specs=[result] * 4),
        out_shape=[_sds((2 * h, cols))] * 4,
        compiler_params=_params(("arbitrary",)),
    )(place, arrived, own, w, m, v)


def _adam_math(w, g, m, v):
    m = ADAM_B1 * m + (1.0 - ADAM_B1) * g
    v = ADAM_B2 * v + (1.0 - ADAM_B2) * jnp.square(g)
    m_hat = m / (1.0 - ADAM_B1 ** ADAM_STEP)
    v_hat = v / (1.0 - ADAM_B2 ** ADAM_STEP)
    delta = -ADAM_LR * (m_hat / (jnp.sqrt(v_hat) + ADAM_EPS) + ADAM_WD * w)
    return delta, m, v


def _adam(w, g, m, v, name):
    rows, cols = w.shape
    tm = _row_tile(rows)

    def body(w_ref, g_ref, m_ref, v_ref, d_ref, nm_ref, nv_ref):
        d, nm, nv = _adam_math(w_ref[...], g_ref[...], m_ref[...], v_ref[...])
        d_ref[...] = d
        nm_ref[...] = nm
        nv_ref[...] = nv

    spec = pl.BlockSpec((tm, cols), lambda i: (i, 0))
    return pl.pallas_call(
        body, name=name, grid=(rows // tm,), in_specs=[spec] * 4, out_specs=[spec] * 3,
        out_shape=[_sds((rows, cols))] * 3, compiler_params=_params(("arbitrary",)),
    )(w, g, m, v)


def _mod_fwd(c_all, w_mod, b_mod_cols):
    L, _, n = w_mod.shape

    def body(c_ref, w_ref, b_ref, o_ref):
        o_ref[0] = _hdot(_silu(c_ref[...]), w_ref[0]) + b_ref[0]

    return pl.pallas_call(
        body, name="mod_fwd", grid=(L,),
        in_specs=[_const((N_DEV, D_MODEL)), pl.BlockSpec((1, D_MODEL, n), lambda l: (l, 0, 0)),
                  pl.BlockSpec((1, 1, n), lambda l: (l, 0, 0))],
        out_specs=pl.BlockSpec((1, N_DEV, n), lambda l: (l, 0, 0)),
        out_shape=_sds((L, N_DEV, n)),
        compiler_params=_params(("arbitrary",)),
    )(c_all, w_mod, b_mod_cols)


def _mod_update(c_all, dmod, w, m, v):
    L, _, n = w.shape
    tn = 512

    def body(c_ref, d_ref, w_ref, m_ref, v_ref, g_ref, dl_ref, nm_ref, nv_ref):
        g = _hdot_tn(_silu(c_ref[...]), d_ref[0])
        g_ref[0] = g
        d, nm, nv = _adam_math(w_ref[0], g, m_ref[0], v_ref[0])
        dl_ref[0] = d
        nm_ref[0] = nm
        nv_ref[0] = nv

    big = pl.BlockSpec((1, D_MODEL, tn), lambda l, j: (l, 0, j))
    return pl.pallas_call(
        body, name="mod_update", grid=(L, n // tn),
        in_specs=[_const((N_DEV, D_MODEL)), pl.BlockSpec((1, N_DEV, tn), lambda l, j: (l, 0, j)), big, big, big],
        out_specs=[big] * 4, out_shape=[_sds(w.shape)] * 4,
        compiler_params=_params(("arbitrary", "arbitrary")),
    )(c_all, dmod, w, m, v)


def _pack_rows(parts, row_multiple):
    flat = jnp.concatenate([p.reshape(-1) for p in parts])
    unit = row_multiple * LANES
    flat = jnp.pad(flat, (0, (-flat.shape[0]) % unit))
    return flat.reshape(-1, LANES)


def _unpack(packed, shapes):
    flat = packed.reshape(-1)
    out, off = [], 0
    for s in shapes:
        n = 1
        for d in s:
            n *= d
        out.append(flat[off:off + n].reshape(s))
        off += n
    return out


def _lane_pad(a):
    return jnp.pad(a, ((0, 0), (0, LANES - a.shape[1])))


WEIGHT_NAMES = ("norm_mix_w", "norm_mlp_w", "w_mod", "b_mod", "w_in", "lru_conv_w", "lru_conv_b", "lru_gate_a_w",
                "lru_gate_a_b", "lru_gate_x_w", "lru_gate_x_b", "lru_lambda", "lru_norm_w", "gdn_conv_w", "gdn_a_log",
                "gdn_dt_bias", "gdn_norm_w", "w_out", "w_up", "w_down", "final_norm_w")


def kernel(x, c, norm_mix_w, norm_mlp_w, w_mod, b_mod, w_in, lru_conv_w, lru_conv_b, lru_gate_a_w, lru_gate_a_b, lru_gate_x_w, lru_gate_x_b, lru_lambda, lru_norm_w, gdn_conv_w, gdn_a_log, gdn_dt_bias, gdn_norm_w, w_out, w_up, w_down, final_norm_w, loss_target, m_norm_mix_w, m_norm_mlp_w, m_w_mod, m_b_mod, m_w_in, m_lru_conv_w, m_lru_conv_b, m_lru_gate_a_w, m_lru_gate_a_b, m_lru_gate_x_w, m_lru_gate_x_b, m_lru_lambda, m_lru_norm_w, m_gdn_conv_w, m_gdn_a_log, m_gdn_dt_bias, m_gdn_norm_w, m_w_out, m_w_up, m_w_down, m_final_norm_w, v_norm_mix_w, v_norm_mlp_w, v_w_mod, v_b_mod, v_w_in, v_lru_conv_w, v_lru_conv_b, v_lru_gate_a_w, v_lru_gate_a_b, v_lru_gate_x_w, v_lru_gate_x_b, v_lru_lambda, v_lru_norm_w, v_gdn_conv_w, v_gdn_a_log, v_gdn_dt_bias, v_gdn_norm_w, v_w_out, v_w_up, v_w_down, v_final_norm_w):
    W = dict(zip(WEIGHT_NAMES, (norm_mix_w, norm_mlp_w, w_mod, b_mod, w_in, lru_conv_w, lru_conv_b, lru_gate_a_w,
                                lru_gate_a_b, lru_gate_x_w, lru_gate_x_b, lru_lambda, lru_norm_w, gdn_conv_w, gdn_a_log,
                                gdn_dt_bias, gdn_norm_w, w_out, w_up, w_down, final_norm_w)))
    M = dict(zip(WEIGHT_NAMES, (m_norm_mix_w, m_norm_mlp_w, m_w_mod, m_b_mod, m_w_in, m_lru_conv_w, m_lru_conv_b,
                                m_lru_gate_a_w, m_lru_gate_a_b, m_lru_gate_x_w, m_lru_gate_x_b, m_lru_lambda,
                                m_lru_norm_w, m_gdn_conv_w, m_gdn_a_log, m_gdn_dt_bias, m_gdn_norm_w, m_w_out, m_w_up,
                                m_w_down, m_final_norm_w)))
    V = dict(zip(WEIGHT_NAMES, (v_norm_mix_w, v_norm_mlp_w, v_w_mod, v_b_mod, v_w_in, v_lru_conv_w, v_lru_conv_b,
                                v_lru_gate_a_w, v_lru_gate_a_b, v_lru_gate_x_w, v_lru_gate_x_b, v_lru_lambda,
                                v_lru_norm_w, v_gdn_conv_w, v_gdn_a_log, v_gdn_dt_bias, v_gdn_norm_w, v_w_out, v_w_up,
                                v_w_down, v_final_norm_w)))
    L = DEPTH
    xi, yi, ci = _position()
    chip = 2 * xi + yi
    dev = 2 * chip + ci
    lcs = LRU_W // N_CHIPS
    gcs = 3 * GDN_W // N_CHIPS
    mcs = N_MOD * D_MODEL // N_CHIPS

    g_in = _all_gather_rows(_pack_rows([c, lru_conv_w, gdn_conv_w], SUBLANES), "gather_small_inputs").reshape(N_DEV, -1)
    c_all = g_in[:, :D_MODEL]
    per_chip = g_in[0::2]
    o1 = D_MODEL + L * 4 * lcs
    lcw_full = per_chip[:, D_MODEL:o1].reshape(N_CHIPS, L, 4, lcs).transpose(1, 2, 0, 3).reshape(L, 4, LRU_W)
    gcw_full = per_chip[:, o1:o1 + L * 4 * gcs].reshape(N_CHIPS, L, 4, gcs).transpose(1, 2, 0, 3).reshape(L, 4, 3 * GDN_W)

    b_cols = lax.dynamic_slice(b_mod, (0, chip * mcs), (L, mcs)).reshape(L, 1, mcs)
    modp = _mod_fwd(c_all, w_mod, b_cols)
    g_mod = _all_gather_rows(modp.reshape(L * N_DEV, mcs), "gather_mod").reshape(N_DEV, L, N_DEV, mcs)
    mod = lax.dynamic_index_in_dim(g_mod[0::2], dev, axis=2, keepdims=False).transpose(1, 0, 2).reshape(L, N_MOD * D_MODEL)

    stacked = _prep_layers(norm_mix_w, norm_mlp_w, mod, None, lcw_full, lru_conv_b, lru_gate_a_w, lru_gate_a_b,
                           lru_gate_x_w, lru_gate_x_b, lru_lambda, lru_norm_w, gcw_full, gdn_a_log, gdn_dt_bias,
                           gdn_norm_w, None, None, None)
    layers = []
    for l in range(L):
        shards = [w_in[l].astype(BF16), w_out[l].astype(BF16), w_up[l].astype(BF16), w_down[l].astype(BF16)]
        gathered = _gather_chips(shards, "gather_weights")
        win_g, wout_g, wup_g, wdown_g = (
            lax.dynamic_update_slice(got, own.reshape((1,) + got.shape[1:]), (chip, 0, 0, 0)).reshape(
                (N_CHIPS,) + own.shape) for got, own in zip(gathered, shards))
        p = {k: v[l] for k, v in stacked.items() if v is not None}
        p["win"] = jnp.pad(win_g.transpose(1, 0, 2).reshape(D_MODEL, IN_COLS), ((0, 0), (0, IN_PAD - IN_COLS)))
        p["wout"] = wout_g.reshape(D_MODEL, D_MODEL)
        p["wup"] = wup_g.transpose(1, 0, 2).reshape(D_MODEL, D_FF)
        p["wdown"] = wdown_g.reshape(D_FF, D_MODEL)
        layers.append(p)
    xs = x[0]
    res = []
    for p in layers:
        xs, r = _layer_fwd(xs, p)
        res.append(r)
    dx, loss_blk, dfnw = _loss_head(xs, loss_target[0], final_norm_w.reshape(1, D_MODEL))
    loss_local = loss_blk[0, 0]

    big_names = ["w_in", "w_out", "w_up", "w_down"]
    core = jnp.reshape(ci, (1,)).astype(jnp.int32)
    place = jnp.stack([ci, chip]).astype(jnp.int32)
    layer_grads = [None] * L
    big = [None] * L
    for l in reversed(range(L)):
        dx, gl = _layer_bwd(dx, layers[l], res[l], sharded=True)
        layer_grads[l] = gl
        gwin = gl["win"][:, :IN_COLS].reshape(2, D_MODEL // 2, N_CHIPS, IN_COLS // N_CHIPS).transpose(0, 2, 1, 3)
        parts = [gwin, gl["wout"], gl["wup"], gl["wdown"]]
        from_sibling = _send_to_sibling(parts, "pair_send")
        pair = [_pair_add(p, r, core, "pair_add_" + nm) for nm, p, r in zip(big_names, parts, from_sibling)]
        arrived = _scatter_chips(pair, "chip_scatter")
        halves = []
        for nm, a, own in zip(big_names, arrived, pair):
            halves += _sum_adam(a, own, W[nm], M[nm], V[nm], l, place, "sum_adam_" + nm)
        big[l] = _swap_row_halves(halves, "pair_swap")
    small_keys = [k for k in layer_grads[0] if k not in ("win", "wout", "wup", "wdown")]
    g = {k: jnp.stack([gl[k] for gl in layer_grads]) for k in small_keys}
    loss = lax.psum(loss_local, ("x", "y", "c"))

    dmod = jnp.concatenate([g["sh1"], g["sc1"], g["g1"], g["sh2"], g["sc2"], g["g2"]], axis=-1)
    small = [dmod, g["nmw"], g["nmlp"], g["lcw"][:, :4], g["lcb"], jax.vmap(_diag_blocks)(g["wa"]), g["ba"],
             jax.vmap(_diag_blocks)(g["wx"]), g["bx"], g["lam"], g["lnw"], g["gcw"][:, :4], g["alog"], g["dtb"],
             g["gnw"], dfnw]
    small_shapes = [(L, N_MOD * D_MODEL), (L, D_MODEL), (L, D_MODEL), (L, 4, LRU_W), (L, LRU_W),
                    (L, LRU_BLOCKS, LRU_BLOCK, LRU_BLOCK), (L, LRU_W), (L, LRU_BLOCKS, LRU_BLOCK, LRU_BLOCK),
                    (L, LRU_W), (L, LRU_W), (L, LRU_W), (L, 4, 3 * GDN_W), (L, LANES), (L, LANES), (L, LANES),
                    (D_MODEL,)]
    small_names = ["b_mod", "norm_mix_w", "norm_mlp_w", None, "lru_conv_b", "lru_gate_a_w", "lru_gate_a_b",
                   "lru_gate_x_w", "lru_gate_x_b", "lru_lambda", "lru_norm_w", None, "gdn_a_log", "gdn_dt_bias",
                   "gdn_norm_w", "final_norm_w"]
    pack_g = _pack_rows(small, 512)
    rows = pack_g.shape[0]
    all_g = _all_gather_rows(pack_g, "gather_small_grads").reshape(N_DEV, rows, LANES)
    tot = _sum_slots(all_g, "sum_small_grads")
    tot_parts = _unpack(tot, small_shapes)

    def pack_state(S_):
        parts = []
        for nm, shp in zip(small_names, small_shapes):
            if nm is None:
                parts.append(jnp.zeros(shp, F32))
            elif nm in ("gdn_a_log", "gdn_dt_bias"):
                parts.append(_lane_pad(S_[nm]))
            else:
                parts.append(S_[nm])
        return _pack_rows(parts, 512)

    upd = _adam(pack_state(W), tot, pack_state(M), pack_state(V), "adam_small")
    upd_parts = [_unpack(u, small_shapes) for u in upd]

    grads, deltas, new_m, new_v = {}, {}, {}, {}
    for k, nm in enumerate(small_names):
        if nm is None:
            continue
        cut = (lambda a: a[:, :HEADS]) if nm in ("gdn_a_log", "gdn_dt_bias") else (lambda a: a)
        grads[nm] = cut(tot_parts[k])
        deltas[nm], new_m[nm], new_v[nm] = (cut(u[k]) for u in upd_parts)

    g_lcw = lax.dynamic_slice(tot_parts[3], (0, 0, chip * lcs), (L, 4, lcs))
    g_gcw = lax.dynamic_slice(tot_parts[11], (0, 0, chip * gcs), (L, 4, gcs))
    conv_shapes = [(L, 4, lcs), (L, 4, gcs)]
    conv_pack = lambda a, b: _pack_rows([a, b], SUBLANES)
    cu = _adam(conv_pack(lru_conv_w, gdn_conv_w), conv_pack(g_lcw, g_gcw), conv_pack(m_lru_conv_w, m_gdn_conv_w),
               conv_pack(v_lru_conv_w, v_gdn_conv_w), "adam_conv")
    cu_parts = [_unpack(u, conv_shapes) for u in cu]
    for k, nm in enumerate(("lru_conv_w", "gdn_conv_w")):
        grads[nm] = (g_lcw, g_gcw)[k]
        deltas[nm], new_m[nm], new_v[nm] = (u[k] for u in cu_parts)

    dmod_all = all_g[:, :L * N_MOD * D_MODEL // LANES].reshape(N_DEV, L, N_MOD * D_MODEL)
    dmod_cols = lax.dynamic_slice(dmod_all, (0, 0, chip * mcs), (N_DEV, L, mcs)).transpose(1, 0, 2)
    grads["w_mod"], deltas["w_mod"], new_m["w_mod"], new_v["w_mod"] = _mod_update(c_all, dmod_cols, w_mod, m_w_mod, v_w_mod)

    for k, nm in enumerate(big_names):
        grads[nm], deltas[nm], new_m[nm], new_v[nm] = (jnp.stack([big[l][4 * k + i] for l in range(L)])
                                                       for i in range(4))

    out = [loss, dx[None]]
    for group in (grads, deltas, new_m, new_v):
        out += [group[nm].reshape(W[nm].shape) for nm in WEIGHT_NAMES]
    return tuple(out)
```

```python
import functools

import jax
import jax.numpy as jnp
from jax import lax
from jax.experimental import pallas as pl
from jax.experimental.pallas import tpu as pltpu

F32 = jnp.float32
BF16 = jnp.bfloat16
MESH = pl.DeviceIdType.MESH

D_MODEL = 1024
DEPTH = 4
LRU_W = 512
LRU_BLOCKS = 8
LRU_BLOCK = 64
LRU_C = 8.0
HEADS = 4
HEAD_DIM = 128
GDN_W = 512
CHUNK = 128
D_FF = 4096
N_MOD = 6
IN_COLS = 3080
IN_PAD = 3200
NORM_EPS = 1e-6
LANES = 128
SUBLANES = 8
N_DEV = 8
N_CHIPS = 4

ADAM_LR = 0.001
ADAM_B1 = 0.9
ADAM_B2 = 0.999
ADAM_EPS = 1e-08
ADAM_WD = 0.01
ADAM_STEP = 10

VMEM_LIMIT = 56 * 1024 * 1024
HI = lax.Precision.HIGHEST


def _sds(shape, dtype=F32):
    return jax.ShapeDtypeStruct(tuple(shape), dtype)


def _params(sem=None, vmem=VMEM_LIMIT):
    return pltpu.CompilerParams(dimension_semantics=sem, vmem_limit_bytes=vmem)


def _const(shape):
    return pl.BlockSpec(tuple(shape), lambda *_: (0,) * len(shape))


def _row(tm, c, col=0):
    return pl.BlockSpec((tm, c), lambda i: (i, col))


def _dot(a, b):
    return jnp.dot(a, b, preferred_element_type=F32)


def _dot_nt(a, b):
    return lax.dot_general(a, b, (((1,), (1,)), ((), ())), preferred_element_type=F32)


def _dot_tn(a, b):
    return lax.dot_general(a, b, (((0,), (0,)), ((), ())), preferred_element_type=F32)


def _hdot(a, b):
    return jnp.dot(a, b, preferred_element_type=F32, precision=HI)


def _hdot_nt(a, b):
    return lax.dot_general(a, b, (((1,), (1,)), ((), ())), preferred_element_type=F32, precision=HI)


def _hdot_tn(a, b):
    return lax.dot_general(a, b, (((0,), (0,)), ((), ())), preferred_element_type=F32, precision=HI)


_DIMS = {"nn": (((1,), (0,)), ((), ())), "nt": (((1,), (1,)), ((), ())), "tn": (((0,), (0,)), ((), ()))}


def _mm_raw(a, b, dims, passes):
    dn = _DIMS[dims]

    def dot(p, q):
        return lax.dot_general(p, q, dn, preferred_element_type=F32)

    a_hi = a.astype(BF16)
    b_hi = b.astype(BF16)
    if passes == 1:
        return dot(a_hi, b_hi)
    a_lo = (a - a_hi.astype(F32)).astype(BF16)
    b_lo = (b - b_hi.astype(F32)).astype(BF16)
    return dot(a_hi, b_hi) + (dot(a_hi, b_lo) + dot(a_lo, b_hi))


@functools.partial(jax.custom_vjp, nondiff_argnums=(2, 3))
def _mm(a, b, dims, passes):
    return _mm_raw(a, b, dims, passes)


def _mm_fwd(a, b, dims, passes):
    return _mm_raw(a, b, dims, passes), (a, b)


def _mm_bwd(dims, passes, res, ct):
    a, b = res
    if dims == "nn":
        return _mm_raw(ct, b, "nt", passes), _mm_raw(a, ct, "tn", passes)
    if dims == "nt":
        return _mm_raw(ct, b, "nn", passes), _mm_raw(ct, a, "tn", passes)
    return _mm_raw(b, ct, "nt", passes), _mm_raw(a, ct, "nn", passes)


_mm.defvjp(_mm_fwd, _mm_bwd)


def _acc(ref, val, first):
    @pl.when(first)
    def _():
        ref[...] = val

    @pl.when(jnp.logical_not(first))
    def _():
        ref[...] += val


def _colsum(v):
    return jnp.sum(v, axis=0, keepdims=True)


def _rms_parts(x):
    r = lax.rsqrt(jnp.mean(x * x, axis=-1, keepdims=True) + NORM_EPS)
    return x * r, r


def _rms_bwd(dy, xh, r, w):
    dxh = dy * w
    dw = _colsum(dy * xh)
    dx = r * (dxh - xh * jnp.mean(dxh * xh, axis=-1, keepdims=True))
    return dx, dw


def _norm_mod(x, w, sc, sh):
    xh, _ = _rms_parts(x)
    return (xh * w) * (1.0 + sc) + sh


def _norm_mod_bwd(dy, x, w, sc):
    xh, r = _rms_parts(x)
    n = xh * w
    dsh = _colsum(dy)
    dsc = _colsum(dy * n)
    dx, dw = _rms_bwd(dy * (1.0 + sc), xh, r, w)
    return dx, dw, dsc, dsh


def _softplus(x):
    return jnp.maximum(x, 0.0) + jnp.log1p(jnp.exp(-jnp.abs(x)))


def _silu(x):
    return x * jax.nn.sigmoid(x)


def _silu_grad(x):
    s = jax.nn.sigmoid(x)
    return s * (1.0 + x * (1.0 - s))


def _roll_dn(x, d):
    return x if d == 0 else pltpu.roll(x, d, 0)


def _roll_up(x, d):
    return x if d == 0 else pltpu.roll(x, x.shape[0] - d, 0)


def _proj_fwd(x, nw, sc, sh, win):
    S = x.shape[0]
    tm = min(512, S)

    def body(x_ref, nw_ref, sc_ref, sh_ref, w_ref, proj_ref, hb_ref):
        hb = _norm_mod(x_ref[...], nw_ref[...], sc_ref[...], sh_ref[...]).astype(BF16)
        hb_ref[...] = hb
        proj_ref[...] = _dot(hb, w_ref[...])

    vec = _const((1, D_MODEL))
    return pl.pallas_call(
        body, name="proj_fwd", grid=(S // tm,),
        in_specs=[_row(tm, D_MODEL), vec, vec, vec, _const((D_MODEL, IN_PAD))],
        out_specs=[_row(tm, IN_PAD), _row(tm, D_MODEL)],
        out_shape=[_sds((S, IN_PAD)), _sds((S, D_MODEL), BF16)],
        compiler_params=_params(("arbitrary",)),
    )(x, nw, sc, sh, win)


def _proj_bwd(dx1, x, dlx, dly, dqkv, dz, dba, nw, sc, win):
    S = x.shape[0]
    tm = min(512, S)

    def body(dx1_ref, x_ref, dlx_ref, dly_ref, dqkv_ref, dz_ref, dba_ref, nw_ref, sc_ref, w_ref,
             dx_ref, dpb_ref, dnw_ref, dsc_ref, dsh_ref):
        i = pl.program_id(0)
        dpb = jnp.concatenate([dlx_ref[...], dly_ref[...], dqkv_ref[...], dz_ref[...], dba_ref[...]],
                              axis=-1).astype(BF16)
        dpb_ref[...] = dpb
        dh = _dot_nt(dpb, w_ref[...])
        dx, dnw, dsc, dsh = _norm_mod_bwd(dh, x_ref[...], nw_ref[...], sc_ref[...])
        dx_ref[...] = dx1_ref[...] + dx
        _acc(dnw_ref, dnw, i == 0)
        _acc(dsc_ref, dsc, i == 0)
        _acc(dsh_ref, dsh, i == 0)

    vec = _const((1, D_MODEL))
    return pl.pallas_call(
        body, name="proj_bwd", grid=(S // tm,),
        in_specs=[_row(tm, D_MODEL), _row(tm, D_MODEL), _row(tm, LRU_W), _row(tm, LRU_W), _row(tm, 3 * GDN_W),
                  _row(tm, GDN_W), _row(tm, LANES), vec, vec,
                  _const((D_MODEL, IN_PAD))],
        out_specs=[_row(tm, D_MODEL), _row(tm, IN_PAD), vec, vec, vec],
        out_shape=[_sds((S, D_MODEL)), _sds((S, IN_PAD), BF16), _sds((1, D_MODEL)), _sds((1, D_MODEL)),
                   _sds((1, D_MODEL))],
        compiler_params=_params(("arbitrary",)),
    )(dx1, x, dlx, dly, dqkv, dz, dba, nw, sc, win)


def _conv_taps(xx, w, tm):
    y = _roll_dn(xx, 3)[SUBLANES:] * w[0:1]
    y = y + _roll_dn(xx, 2)[SUBLANES:] * w[1:2]
    y = y + _roll_dn(xx, 1)[SUBLANES:] * w[2:3]
    y = y + xx[SUBLANES:] * w[3:4]
    return y


def _conv_fwd(src, col0, C, w8, b, act, name):
    S = src.shape[0]
    tm = min(512, S)
    tc = 512
    hb = tm // SUBLANES
    cb0 = col0 // tc

    def body(x_ref, p_ref, w_ref, b_ref, y_ref):
        i = pl.program_id(0)
        prev = jnp.where(i > 0, p_ref[...], 0.0)
        xx = jnp.concatenate([prev, x_ref[...]], axis=0)
        y = _conv_taps(xx, w_ref[...], tm) + b_ref[...]
        y_ref[...] = _silu(y) if act else y

    return pl.pallas_call(
        body, name=name, grid=(S // tm, C // tc),
        in_specs=[pl.BlockSpec((tm, tc), lambda i, j: (i, cb0 + j)),
                  pl.BlockSpec((SUBLANES, tc), lambda i, j: (jnp.maximum(i * hb - 1, 0), cb0 + j)),
                  pl.BlockSpec((SUBLANES, tc), lambda i, j: (0, j)),
                  pl.BlockSpec((1, tc), lambda i, j: (0, j))],
        out_specs=pl.BlockSpec((tm, tc), lambda i, j: (i, j)),
        out_shape=_sds((S, C)),
        compiler_params=_params(("arbitrary", "arbitrary")),
    )(src, src, w8, b)


def _conv_bwd(src, col0, C, w8, b, dyact, act, name):
    S = src.shape[0]
    tm = min(512, S)
    tc = 512
    hb = tm // SUBLANES
    nt = S // tm
    cb0 = col0 // tc
    last_hb = S // SUBLANES - 1

    def body(x_ref, p_ref, n_ref, dy_ref, dyn_ref, w_ref, b_ref, dx_ref, dw_ref, db_ref):
        i = pl.program_id(1)
        w = w_ref[...]
        prev = jnp.where(i > 0, p_ref[...], 0.0)
        xx = jnp.concatenate([prev, x_ref[...], n_ref[...]], axis=0)
        dy = jnp.concatenate([dy_ref[...], jnp.where(i < nt - 1, dyn_ref[...], 0.0)], axis=0)
        if act:
            ypre = _conv_taps(xx, w, tm + SUBLANES) + b_ref[...]
            dy = dy * _silu_grad(ypre)
        dx = dy[:tm] * w[3:4]
        for d in (1, 2, 3):
            dx = dx + _roll_up(dy, d)[:tm] * w[3 - d:4 - d]
        dx_ref[...] = dx
        xt = xx[:tm + SUBLANES]
        dyt = dy[:tm]
        rows = [_colsum(dyt * _roll_dn(xt, 3 - k)[SUBLANES:]) for k in range(4)]
        dw = jnp.concatenate(rows + [jnp.zeros((SUBLANES - 4, tc), F32)], axis=0)
        _acc(dw_ref, dw, i == 0)
        _acc(db_ref, _colsum(dyt), i == 0)

    return pl.pallas_call(
        body, name=name, grid=(C // tc, nt),
        in_specs=[pl.BlockSpec((tm, tc), lambda j, i: (i, cb0 + j)),
                  pl.BlockSpec((SUBLANES, tc), lambda j, i: (jnp.maximum(i * hb - 1, 0), cb0 + j)),
                  pl.BlockSpec((SUBLANES, tc), lambda j, i: (jnp.minimum((i + 1) * hb, last_hb), cb0 + j)),
                  pl.BlockSpec((tm, tc), lambda j, i: (i, j)),
                  pl.BlockSpec((SUBLANES, tc), lambda j, i: (jnp.minimum((i + 1) * hb, last_hb), j)),
                  pl.BlockSpec((SUBLANES, tc), lambda j, i: (0, j)),
                  pl.BlockSpec((1, tc), lambda j, i: (0, j))],
        out_specs=[pl.BlockSpec((tm, tc), lambda j, i: (i, j)),
                   pl.BlockSpec((SUBLANES, tc), lambda j, i: (0, j)),
                   pl.BlockSpec((1, tc), lambda j, i: (0, j))],
        out_shape=[_sds((S, C)), _sds((SUBLANES, C)), _sds((1, C))],
        compiler_params=_params(("arbitrary", "arbitrary")),
    )(src, src, src, dyact, dyact, w8, b)


def _lru_ab(pre_a, pre_x, xr, lam):
    r = jax.nn.sigmoid(pre_a)
    g = jax.nn.sigmoid(pre_x)
    log_sig = -_softplus(-lam)
    log_a = LRU_C * r * log_sig
    a = jnp.exp(log_a)
    t = jnp.tanh(log_a)
    mult = jnp.sqrt(jnp.maximum(-2.0 * t / (1.0 - t), 1e-12))
    return a, mult * (g * xr)


def _lru_tail(h, ly, lnw):
    xh, _ = _rms_parts(h * jax.nn.gelu(ly))
    return xh * lnw


def _scan_down(a, b):
    n = a.shape[0]
    row = lax.broadcasted_iota(jnp.int32, a.shape, 0)
    d = 1
    while d < n:
        keep = row >= d
        a_s = jnp.where(keep, _roll_dn(a, d), 1.0)
        b_s = jnp.where(keep, _roll_dn(b, d), 0.0)
        b = a * b_s + b
        a = a * a_s
        d *= 2
    return a, b


def _scan_up(a, b):
    n = a.shape[0]
    row = lax.broadcasted_iota(jnp.int32, a.shape, 0)
    d = 1
    while d < n:
        keep = row < n - d
        a_s = jnp.where(keep, _roll_up(a, d), 1.0)
        b_s = jnp.where(keep, _roll_up(b, d), 0.0)
        b = a * b_s + b
        a = a * a_s
        d *= 2
    return a, b


LRU_TM = 256


def _lru_fwd(xr, proj, wa, ba, wx, bx, lam, lnw):
    S = xr.shape[0]
    tm = min(LRU_TM, S)

    def body(xr_ref, ly_ref, wa_ref, ba_ref, wx_ref, bx_ref, lam_ref, lnw_ref, out_ref, h_ref, carry):
        i = pl.program_id(0)

        @pl.when(i == 0)
        def _():
            carry[...] = jnp.zeros_like(carry)

        x = xr_ref[...]
        xb = x.astype(BF16)
        pre_a = _dot(xb, wa_ref[...]) + ba_ref[...]
        pre_x = _dot(xb, wx_ref[...]) + bx_ref[...]
        a, b = _lru_ab(pre_a, pre_x, x, lam_ref[...])
        ca, hl = _scan_down(a, b)
        h = hl + ca * carry[0:1, :]
        carry[0:1, :] = h[tm - 1:tm, :]
        h_ref[...] = h
        out_ref[...] = _lru_tail(h, ly_ref[...], lnw_ref[...])

    vec = _const((1, LRU_W))
    mat = _const((LRU_W, LRU_W))
    return pl.pallas_call(
        body, name="lru_fwd", grid=(S // tm,),
        in_specs=[_row(tm, LRU_W), _row(tm, LRU_W, 1), mat, vec, mat, vec, vec, vec],
        out_specs=[_row(tm, LRU_W), _row(tm, LRU_W)],
        out_shape=[_sds((S, LRU_W)), _sds((S, LRU_W))],
        scratch_shapes=[pltpu.VMEM((SUBLANES, LRU_W), F32)],
        compiler_params=_params(("arbitrary",)),
    )(xr, proj, wa, ba, wx, bx, lam, lnw)


def _lru_bwd(dout, xr, proj, h, wa, ba, wx, bx, lam, lnw):
    S = xr.shape[0]
    tm = min(LRU_TM, S)
    nt = S // tm
    hb = tm // SUBLANES

    def rev(col=0):
        return pl.BlockSpec((tm, LRU_W), lambda i: (nt - 1 - i, col))

    def body(dout_ref, xr_ref, ly_ref, h_ref, hp_ref, wa_ref, ba_ref, wx_ref, bx_ref, lam_ref, lnw_ref,
             dxr_ref, dly_ref, dwa_ref, dba_ref, dwx_ref, dbx_ref, dlam_ref, dlnw_ref, carry):
        i = pl.program_id(0)
        first = i == 0

        @pl.when(first)
        def _():
            carry[...] = jnp.zeros_like(carry)

        x = xr_ref[...]
        xb = x.astype(BF16)
        pre_a = _dot(xb, wa_ref[...]) + ba_ref[...]
        pre_x = _dot(xb, wx_ref[...]) + bx_ref[...]
        (a, b), ab_vjp = jax.vjp(_lru_ab, pre_a, pre_x, x, lam_ref[...])
        h_t = h_ref[...]
        _, tail_vjp = jax.vjp(_lru_tail, h_t, ly_ref[...], lnw_ref[...])
        dh, dly, dlnw = tail_vjp(dout_ref[...])
        dly_ref[...] = dly
        row = lax.broadcasted_iota(jnp.int32, a.shape, 0)
        a_next = jnp.where(row == tm - 1, carry[0:1, :], _roll_up(a, 1))
        ca, gl = _scan_up(a_next, dh)
        g = gl + ca * carry[1:2, :]
        carry[0:1, :] = a[0:1, :]
        carry[1:2, :] = g[0:1, :]
        h_before = jnp.where(i == nt - 1, 0.0, hp_ref[SUBLANES - 1:SUBLANES, :])
        h_prev = jnp.where(row == 0, h_before, _roll_dn(h_t, 1))
        dpa, dpx, dx, dlam = ab_vjp((g * h_prev, g))
        dpab = dpa.astype(BF16)
        dpxb = dpx.astype(BF16)
        dxr_ref[...] = dx + _dot_nt(dpab, wa_ref[...]) + _dot_nt(dpxb, wx_ref[...])
        _acc(dwa_ref, _dot_tn(xb, dpab), first)
        _acc(dwx_ref, _dot_tn(xb, dpxb), first)
        _acc(dba_ref, _colsum(dpa), first)
        _acc(dbx_ref, _colsum(dpx), first)
        _acc(dlam_ref, dlam, first)
        _acc(dlnw_ref, dlnw, first)

    vec = _const((1, LRU_W))
    mat = _const((LRU_W, LRU_W))
    return pl.pallas_call(
        body, name="lru_bwd", grid=(nt,),
        in_specs=[rev(), rev(), rev(1), rev(),
                  pl.BlockSpec((SUBLANES, LRU_W), lambda i: (jnp.maximum((nt - 1 - i) * hb - 1, 0), 0)),
                  mat, vec, mat, vec, vec, vec],
        out_specs=[rev(), rev(), mat, vec, mat, vec, vec, vec],
        out_shape=[_sds((S, LRU_W)), _sds((S, LRU_W)), _sds((LRU_W, LRU_W)), _sds((1, LRU_W)),
                   _sds((LRU_W, LRU_W)), _sds((1, LRU_W)), _sds((1, LRU_W)), _sds((1, LRU_W))],
        scratch_shapes=[pltpu.VMEM((SUBLANES, LRU_W), F32)],
        compiler_params=_params(("arbitrary",)),
    )(dout, xr, proj, h, h, wa, ba, wx, bx, lam, lnw)


def _lane_pick(row_or_tile, lane):
    idx = lax.broadcasted_iota(jnp.int32, row_or_tile.shape, 1)
    return jnp.sum(jnp.where(idx == lane, row_or_tile, 0.0), axis=-1, keepdims=True)


def _unit_lower_inverses(los):
    n = los[0].shape[0]
    ri = lax.broadcasted_iota(jnp.int32, (n, n), 0)
    ci = lax.broadcasted_iota(jnp.int32, (n, n), 1)
    invs = [(ri == ci).astype(F32) for _ in los]
    s = 1
    while s < n:
        same_block = (ri & ~(2 * s - 1)) == (ci & ~(2 * s - 1))
        lower_left = same_block & ((ri & s) != 0) & ((ci & s) == 0)
        left = [_mm_raw(inv, jnp.where(lower_left, lo, 0.0), "nn", 3) for inv, lo in zip(invs, los)]
        invs = [inv - _mm_raw(t, inv, "nn", 3) for inv, t in zip(invs, left)]
        s *= 2
    return invs


@jax.custom_vjp
def _unit_lower_inverses_diff(los):
    return _unit_lower_inverses(los)


def _unit_lower_inverses_fwd(los):
    invs = _unit_lower_inverses(los)
    return invs, invs


def _unit_lower_inverses_bwd(invs, cts):
    right = [_mm_raw(ct, inv, "nt", 3) for ct, inv in zip(cts, invs)]
    return ([-_mm_raw(inv, r, "tn", 3) for inv, r in zip(invs, right)],)


_unit_lower_inverses_diff.defvjp(_unit_lower_inverses_fwd, _unit_lower_inverses_bwd)


def _gdn_chunk(qs, ks, vs, ba, alog, dtb, states, inverses=_unit_lower_inverses, mm=_mm_raw):
    C = qs[0].shape[0]
    heads = range(len(qs))
    ri = lax.broadcasted_iota(jnp.int32, (C, C), 0)
    ci = lax.broadcasted_iota(jnp.int32, (C, C), 1)
    causal = ri >= ci
    strict = ri > ci
    tri = causal.astype(F32)
    betas = [jax.nn.sigmoid(_lane_pick(ba, h)) for h in heads]
    gs = [-jnp.exp(_lane_pick(alog, h)) * _softplus(_lane_pick(ba, h + HEADS) + _lane_pick(dtb, h)) for h in heads]
    qn = [q * lax.rsqrt(jnp.sum(q * q, axis=-1, keepdims=True) + 1e-6) * (HEAD_DIM ** -0.5) for q in qs]
    kn = [k * lax.rsqrt(jnp.sum(k * k, axis=-1, keepdims=True) + 1e-6) for k in ks]
    gc = [_hdot(tri, jnp.broadcast_to(g, (C, C))) for g in gs]
    decay = [jnp.where(causal, jnp.exp(jnp.where(causal, c - c.T, 0.0)), 0.0) for c in gc]
    eg = [jnp.exp(c) for c in gc]
    kb = [k * b for k, b in zip(kn, betas)]
    vb = [v * b for v, b in zip(vs, betas)]
    los = [jnp.where(strict, mm(a, k, "nt", 1) * d, 0.0) for a, k, d in zip(kb, kn, decay)]
    attn = [jnp.where(causal, mm(q, k, "nt", 1) * d, 0.0) for q, k, d in zip(qn, kn, decay)]
    tinv = inverses(los)
    u = [mm(t, x, "nn", 3) for t, x in zip(tinv, vb)]
    w = [mm(t, a * e, "nn", 3) for t, a, e in zip(tinv, kb, eg)]
    g_last = [c[C - 1:C, :] for c in gc]
    k_tail = [k * jnp.exp(gl - c) for k, gl, c in zip(kn, g_last, gc)]
    v_new = [a - mm(b, s, "nn", 1) for a, b, s in zip(u, w, states)]
    o_state = [mm(q * e, s, "nn", 1) for q, e, s in zip(qn, eg, states)]
    o = [a + mm(at, vn, "nn", 1) for a, at, vn in zip(o_state, attn, v_new)]
    new_states = [s * jnp.exp(gl) + mm(kt, vn, "tn", 1) for s, gl, kt, vn in zip(states, g_last, k_tail, v_new)]
    return o, new_states


def _gdn_fwd(qkv, proj, alog, dtb, gather=()):
    S = qkv.shape[0]
    nc = S // CHUNK
    nk = len(gather)
    assert CHUNK == HEAD_DIM

    def body(*refs):
        q_ref, k_ref, v_ref, ba_ref, alog_ref, dtb_ref = refs[:6]
        o_ref, st_ref = refs[6 + nk:8 + nk]
        state = refs[8 + 2 * nk]
        if nk:
            start, finish = _gather_steps(refs[6:6 + nk], refs[8 + nk:8 + 2 * nk], *refs[9 + 2 * nk:])
            pl.when(pl.program_id(0) == 0)(start)

        @pl.when(pl.program_id(0) == 0)
        def _():
            state[...] = jnp.zeros_like(state)

        sls = [slice(hd * HEAD_DIM, (hd + 1) * HEAD_DIM) for hd in range(HEADS)]
        s0 = [state[hd] for hd in range(HEADS)]
        for hd in range(HEADS):
            st_ref[hd, 0] = s0[hd]
        o, s1 = _gdn_chunk([q_ref[:, sl] for sl in sls], [k_ref[:, sl] for sl in sls], [v_ref[:, sl] for sl in sls],
                           ba_ref[...], alog_ref[...], dtb_ref[...], s0)
        for hd in range(HEADS):
            o_ref[:, sls[hd]] = o[hd]
            state[hd] = s1[hd]
        if nk:
            pl.when(pl.program_id(0) == nc - 1)(finish)

    def col(j):
        return pl.BlockSpec((CHUNK, GDN_W), lambda n: (n, j))

    vec = _const((1, LANES))
    outs = pl.pallas_call(
        body, name="gdn_fwd", grid=(nc,),
        in_specs=[col(0), col(1), col(2), pl.BlockSpec((CHUNK, LANES), lambda n: (n, IN_PAD // LANES - 1)), vec, vec]
        + _hbm_specs(nk),
        out_specs=[col(0), pl.BlockSpec((HEADS, 1, HEAD_DIM, HEAD_DIM), lambda n: (0, n, 0, 0))] + _hbm_specs(nk),
        out_shape=[_sds((S, GDN_W)), _sds((HEADS, nc, HEAD_DIM, HEAD_DIM))] + (_gather_out_shapes(gather) if nk else []),
        scratch_shapes=[pltpu.VMEM((HEADS, HEAD_DIM, HEAD_DIM), F32)] + (_gather_scratch(nk) if nk else []),
        compiler_params=_params(("arbitrary",)),
    )(qkv, qkv, qkv, proj, alog, dtb, *gather)
    return outs[0], outs[1], list(outs[2:])


def _gdn_bwd(do, qkv, proj, states, alog, dtb, scatter=()):
    S = qkv.shape[0]
    nc = S // CHUNK
    nk = len(scatter)

    def body(*refs):
        do_ref, q_ref, k_ref, v_ref, ba_ref, st_ref, alog_ref, dtb_ref = refs[:8]
        dqkv_ref, dba_ref, dalog_ref, ddtb_ref = refs[8 + nk:12 + nk]
        dstate = refs[12 + 2 * nk]
        n = pl.program_id(0)
        if nk:
            start, finish = _scatter_steps(refs[8:8 + nk], refs[12 + nk:12 + 2 * nk], *refs[13 + 2 * nk:])
            pl.when(n == 0)(start)

        @pl.when(n == 0)
        def _():
            dstate[...] = jnp.zeros_like(dstate)

        sls = [slice(hd * HEAD_DIM, (hd + 1) * HEAD_DIM) for hd in range(HEADS)]
        fn = functools.partial(_gdn_chunk, inverses=_unit_lower_inverses_diff, mm=_mm)
        _, vjp = jax.vjp(fn, [q_ref[:, sl] for sl in sls], [k_ref[:, sl] for sl in sls], [v_ref[:, sl] for sl in sls],
                         ba_ref[...], alog_ref[...], dtb_ref[...], [st_ref[hd, 0] for hd in range(HEADS)])
        dq, dk, dv, dba, dalog, ddtb, ds = vjp(([do_ref[:, sl] for sl in sls], [dstate[hd] for hd in range(HEADS)]))
        for hd in range(HEADS):
            dqkv_ref[:, sls[hd]] = dq[hd]
            dqkv_ref[:, GDN_W + hd * HEAD_DIM:GDN_W + (hd + 1) * HEAD_DIM] = dk[hd]
            dqkv_ref[:, 2 * GDN_W + hd * HEAD_DIM:2 * GDN_W + (hd + 1) * HEAD_DIM] = dv[hd]
            dstate[hd] = ds[hd]
        dba_ref[...] = dba
        _acc(dalog_ref, dalog, n == 0)
        _acc(ddtb_ref, ddtb, n == 0)
        if nk:
            pl.when(n == nc - 1)(finish)

    def col(j):
        return pl.BlockSpec((CHUNK, GDN_W), lambda n: (nc - 1 - n, j))

    vec = _const((1, LANES))
    outs = pl.pallas_call(
        body, name="gdn_bwd", grid=(nc,),
        in_specs=[col(0), col(0), col(1), col(2),
                  pl.BlockSpec((CHUNK, LANES), lambda n: (nc - 1 - n, IN_PAD // LANES - 1)),
                  pl.BlockSpec((HEADS, 1, HEAD_DIM, HEAD_DIM), lambda n: (0, nc - 1 - n, 0, 0)), vec, vec]
        + _hbm_specs(nk),
        out_specs=[pl.BlockSpec((CHUNK, 3 * GDN_W), lambda n: (nc - 1 - n, 0)),
                   pl.BlockSpec((CHUNK, LANES), lambda n: (nc - 1 - n, 0)), vec, vec] + _hbm_specs(nk),
        out_shape=[_sds((S, 3 * GDN_W)), _sds((S, LANES)), _sds((1, LANES)), _sds((1, LANES))]
        + [_sds(p.shape, p.dtype) for p in scatter],
        scratch_shapes=[pltpu.VMEM((HEADS, HEAD_DIM, HEAD_DIM), F32)] + (_scatter_scratch(nk) if nk else []),
        compiler_params=_params(("arbitrary",)),
    )(do, qkv, qkv, qkv, proj, states, alog, dtb, *scatter)
    return outs[0], outs[1], outs[2], outs[3], list(outs[4:])


def _gdn_gate(o, z, gnw):
    outs = []
    for hd in range(HEADS):
        sl = slice(hd * HEAD_DIM, (hd + 1) * HEAD_DIM)
        xh, _ = _rms_parts(o[:, sl])
        outs.append(xh * gnw * _silu(z[:, sl]))
    return jnp.concatenate(outs, axis=-1)


def _out_fwd(x, out_lru, o, proj, gnw, g1, wout):
    S = x.shape[0]
    tm = min(512, S)

    def body(x_ref, lru_ref, o_ref, z_ref, gnw_ref, g1_ref, w_ref, x1_ref, cat_ref):
        cat = jnp.concatenate([lru_ref[...], _gdn_gate(o_ref[...], z_ref[...], gnw_ref[...])], axis=-1).astype(BF16)
        cat_ref[...] = cat
        x1_ref[...] = x_ref[...] + g1_ref[...] * _dot(cat, w_ref[...])

    return pl.pallas_call(
        body, name="out_fwd", grid=(S // tm,),
        in_specs=[_row(tm, D_MODEL), _row(tm, LRU_W), _row(tm, GDN_W), _row(tm, GDN_W, 5), _const((1, LANES)),
                  _const((1, D_MODEL)), _const((D_MODEL, D_MODEL))],
        out_specs=[_row(tm, D_MODEL), _row(tm, D_MODEL)],
        out_shape=[_sds((S, D_MODEL)), _sds((S, D_MODEL), BF16)],
        compiler_params=_params(("arbitrary",)),
    )(x, out_lru, o, proj, gnw, g1, wout)


def _out_bwd(dx1, cat, o, proj, gnw, g1, wout):
    S = dx1.shape[0]
    tm = min(512, S)

    def body(dx1_ref, cat_ref, o_ref, z_ref, gnw_ref, g1_ref, w_ref,
             dlru_ref, do_ref, dz_ref, dmb_ref, dgnw_ref, dg1_ref):
        i = pl.program_id(0)
        d1 = dx1_ref[...]
        mix = _dot(cat_ref[...], w_ref[...])
        _acc(dg1_ref, _colsum(d1 * mix), i == 0)
        dmb = (d1 * g1_ref[...]).astype(BF16)
        dmb_ref[...] = dmb
        dcat = _dot_nt(dmb, w_ref[...])
        dlru_ref[...] = dcat[:, :LRU_W]
        _, vjp = jax.vjp(_gdn_gate, o_ref[...], z_ref[...], gnw_ref[...])
        do, dz, dgnw = vjp(dcat[:, LRU_W:])
        do_ref[...] = do
        dz_ref[...] = dz
        _acc(dgnw_ref, dgnw, i == 0)

    return pl.pallas_call(
        body, name="out_bwd", grid=(S // tm,),
        in_specs=[_row(tm, D_MODEL), _row(tm, D_MODEL), _row(tm, GDN_W), _row(tm, GDN_W, 5), _const((1, LANES)),
                  _const((1, D_MODEL)), _const((D_MODEL, D_MODEL))],
        out_specs=[_row(tm, LRU_W), _row(tm, GDN_W), _row(tm, GDN_W), _row(tm, D_MODEL), _const((1, LANES)),
                   _const((1, D_MODEL))],
        out_shape=[_sds((S, LRU_W)), _sds((S, GDN_W)), _sds((S, GDN_W)), _sds((S, D_MODEL), BF16), _sds((1, LANES)),
                   _sds((1, D_MODEL))],
        compiler_params=_params(("arbitrary",)),
    )(dx1, cat, o, proj, gnw, g1, wout)


MLP_TM = 256


def _load_once(step, pairs, sem):
    @pl.when(step == 0)
    def _():
        copies = [pltpu.make_async_copy(src, dst, sem.at[k]) for k, (src, dst) in enumerate(pairs)]
        for cp in copies:
            cp.start()
        for cp in copies:
            cp.wait()


def _mlp_fwd(x1, nw, sc, sh, g2, wup, wdown):
    S = x1.shape[0]
    tm = min(MLP_TM, S)

    def body(x_ref, nw_ref, sc_ref, sh_ref, g2_ref, wup_hbm, wdown_hbm, x2_ref, wup, wdown, sem):
        _load_once(pl.program_id(0), [(wup_hbm, wup), (wdown_hbm, wdown)], sem)
        x = x_ref[...]
        hb = _norm_mod(x, nw_ref[...], sc_ref[...], sh_ref[...]).astype(BF16)
        r = jnp.maximum(_dot(hb, wup[...]), 0.0)
        x2_ref[...] = x + g2_ref[...] * _dot((r * r).astype(BF16), wdown[...])

    vec = _const((1, D_MODEL))
    anyspec = pl.BlockSpec(memory_space=pl.ANY)
    return pl.pallas_call(
        body, name="mlp_fwd", grid=(S // tm,),
        in_specs=[_row(tm, D_MODEL), vec, vec, vec, vec, anyspec, anyspec],
        out_specs=_row(tm, D_MODEL),
        out_shape=_sds((S, D_MODEL)),
        scratch_shapes=[pltpu.VMEM((D_MODEL, D_FF), BF16), pltpu.VMEM((D_FF, D_MODEL), BF16),
                        pltpu.SemaphoreType.DMA((2,))],
        compiler_params=_params(("arbitrary",)),
    )(x1, nw, sc, sh, g2, wup, wdown)


def _mlp_bwd(dx2, x1, nw, sc, sh, g2, wup, wdown):
    S = x1.shape[0]
    tm = min(MLP_TM, S)

    def body(dx2_ref, x_ref, nw_ref, sc_ref, sh_ref, g2_ref, wup_hbm, wdown_hbm,
             dx1_ref, hb_ref, dupb_ref, actb_ref, ddb_ref, dnw_ref, dsc_ref, dsh_ref, dg2_ref, wup, wdown, sem):
        i = pl.program_id(0)
        _load_once(i, [(wup_hbm, wup), (wdown_hbm, wdown)], sem)
        x = x_ref[...]
        d2 = dx2_ref[...]
        hb = _norm_mod(x, nw_ref[...], sc_ref[...], sh_ref[...]).astype(BF16)
        hb_ref[...] = hb
        r = jnp.maximum(_dot(hb, wup[...]), 0.0)
        actb = (r * r).astype(BF16)
        actb_ref[...] = actb
        down = _dot(actb, wdown[...])
        _acc(dg2_ref, _colsum(d2 * down), i == 0)
        ddb = (d2 * g2_ref[...]).astype(BF16)
        ddb_ref[...] = ddb
        dupb = (_dot_nt(ddb, wdown[...]) * (2.0 * r)).astype(BF16)
        dupb_ref[...] = dupb
        dh = _dot_nt(dupb, wup[...])
        dx, dnw, dsc, dsh = _norm_mod_bwd(dh, x, nw_ref[...], sc_ref[...])
        dx1_ref[...] = d2 + dx
        _acc(dnw_ref, dnw, i == 0)
        _acc(dsc_ref, dsc, i == 0)
        _acc(dsh_ref, dsh, i == 0)

    vec = _const((1, D_MODEL))
    anyspec = pl.BlockSpec(memory_space=pl.ANY)
    return pl.pallas_call(
        body, name="mlp_bwd", grid=(S // tm,),
        in_specs=[_row(tm, D_MODEL), _row(tm, D_MODEL), vec, vec, vec, vec, anyspec, anyspec],
        out_specs=[_row(tm, D_MODEL), _row(tm, D_MODEL), _row(tm, D_FF), _row(tm, D_FF), _row(tm, D_MODEL),
                   vec, vec, vec, vec],
        out_shape=[_sds((S, D_MODEL)), _sds((S, D_MODEL), BF16), _sds((S, D_FF), BF16), _sds((S, D_FF), BF16),
                   _sds((S, D_MODEL), BF16), _sds((1, D_MODEL)), _sds((1, D_MODEL)), _sds((1, D_MODEL)),
                   _sds((1, D_MODEL))],
        scratch_shapes=[pltpu.VMEM((D_MODEL, D_FF), BF16), pltpu.VMEM((D_FF, D_MODEL), BF16),
                        pltpu.SemaphoreType.DMA((2,))],
        compiler_params=_params(("arbitrary",)),
    )(dx2, x1, nw, sc, sh, g2, wup, wdown)


def _matmul_tn(a, b, name, shards=None):
    K, M = a.shape
    N = b.shape[1]
    tk = min(2048, K)
    if shards == "cols":
        tm, tn = M // 2, N // N_CHIPS
        out_spec = pl.BlockSpec((1, 1, tm, tn), lambda i, j, k: (i, j, 0, 0))
        out_shape = _sds((2, N_CHIPS, tm, tn))
    elif shards == "rows":
        h, tn = M // (2 * N_CHIPS), N
        tm = max(512, 2 * h)
        per_tile = tm // (2 * h)
        out_spec = pl.BlockSpec((2, per_tile, h, tn), lambda i, j, k: (0, i, 0, 0))
        out_shape = _sds((2, N_CHIPS, h, tn))
    else:
        tm = min(512, M)
        tn = 640 if N % 640 == 0 else min(1024, N)
        out_spec = pl.BlockSpec((tm, tn), lambda i, j, k: (i, j))
        out_shape = _sds((M, N))
    nk = K // tk

    def body(a_ref, b_ref, o_ref, acc):
        k = pl.program_id(2)
        _acc(acc, _dot_tn(a_ref[...], b_ref[...]), k == 0)

        @pl.when(k == nk - 1)
        def _():
            if shards == "rows":
                for s in range(per_tile):
                    for half in range(2):
                        r0 = (2 * s + half) * h
                        o_ref[half, s] = acc[r0:r0 + h, :]
            else:
                o_ref[...] = acc[...].reshape(o_ref.shape)

    return pl.pallas_call(
        body, name=name, grid=(M // tm, N // tn, nk),
        in_specs=[pl.BlockSpec((tk, tm), lambda i, j, k: (k, i)), pl.BlockSpec((tk, tn), lambda i, j, k: (k, j))],
        out_specs=out_spec, out_shape=out_shape,
        scratch_shapes=[pltpu.VMEM((tm, tn), F32)],
        compiler_params=_params(("arbitrary", "arbitrary", "arbitrary")),
    )(a, b)


def _loss_head(x, target, fnw):
    S = x.shape[0]
    tm = min(512, S)

    def body(x_ref, t_ref, w_ref, dx_ref, loss_ref, dw_ref):
        i = pl.program_id(0)
        w = w_ref[...]
        xh, r = _rms_parts(x_ref[...])
        err = xh * w - t_ref[...]
        part = 0.5 * jnp.sum(jnp.mean(err * err, axis=-1, keepdims=True), axis=0, keepdims=True)
        _acc(loss_ref, jnp.broadcast_to(part, (SUBLANES, LANES)), i == 0)
        dx, dw = _rms_bwd(err * (1.0 / D_MODEL), xh, r, w)
        dx_ref[...] = dx
        _acc(dw_ref, dw, i == 0)

    vec = _const((1, D_MODEL))
    return pl.pallas_call(
        body, name="loss_head", grid=(S // tm,),
        in_specs=[_row(tm, D_MODEL), _row(tm, D_MODEL), vec],
        out_specs=[_row(tm, D_MODEL), _const((SUBLANES, LANES)), vec],
        out_shape=[_sds((S, D_MODEL)), _sds((SUBLANES, LANES)), _sds((1, D_MODEL))],
        compiler_params=_params(("arbitrary",)),
    )(x, target, fnw)


def _block_diag(w):
    eye = jnp.eye(LRU_BLOCKS, dtype=w.dtype)
    return (eye[:, None, :, None] * w[:, :, None, :]).reshape(LRU_W, LRU_W)


def _diag_blocks(m):
    m4 = m.reshape(LRU_BLOCKS, LRU_BLOCK, LRU_BLOCKS, LRU_BLOCK)
    return jnp.stack([m4[g, :, g, :] for g in range(LRU_BLOCKS)])


def _layer_fwd(x, p, gather=()):
    proj, h1b = _proj_fwd(x, p["nmw"], p["sc1"], p["sh1"], p["win"])
    xr = _conv_fwd(proj, 0, LRU_W, p["lcw"], p["lcb"], False, "conv_lru_fwd")
    out_lru, h = _lru_fwd(xr, proj, p["wa"].astype(BF16), p["ba"], p["wx"].astype(BF16), p["bx"], p["lam"], p["lnw"])
    qkv = _conv_fwd(proj, 2 * LRU_W, 3 * GDN_W, p["gcw"], p["gcb"], True, "conv_gdn_fwd")
    o, states, gathered = _gdn_fwd(qkv, proj, p["alog"], p["dtb"], gather)
    x1, cat = _out_fwd(x, out_lru, o, proj, p["gnw"], p["g1"], p["wout"])
    x2 = _mlp_fwd(x1, p["nmlp"], p["sc2"], p["sh2"], p["g2"], p["wup"], p["wdown"])
    res = dict(x=x, proj=proj, h1b=h1b, xr=xr, h=h, qkv=qkv, o=o, states=states, x1=x1, cat=cat)
    return x2, res, gathered


def _layer_bwd(dx2, p, r, sharded=False, scatter=()):
    dx1, h2b, dupb, actb, ddb, dnmlp, dsc2, dsh2, dg2 = _mlp_bwd(
        dx2, r["x1"], p["nmlp"], p["sc2"], p["sh2"], p["g2"], p["wup"], p["wdown"])
    g_wup = _matmul_tn(h2b, dupb, "dw_up", "cols" if sharded else None)
    g_wdown = _matmul_tn(actb, ddb, "dw_down", "rows" if sharded else None)
    dlru, do, dz, dmb, dgnw, dg1 = _out_bwd(dx1, r["cat"], r["o"], r["proj"], p["gnw"], p["g1"], p["wout"])
    g_wout = _matmul_tn(r["cat"], dmb, "dw_out", "rows" if sharded else None)
    dqkv_act, dba, dalog, ddtb, arrived = _gdn_bwd(do, r["qkv"], r["proj"], r["states"], p["alog"], p["dtb"], scatter)
    dqkv, dgcw, _ = _conv_bwd(r["proj"], 2 * LRU_W, 3 * GDN_W, p["gcw"], p["gcb"], dqkv_act, True, "conv_gdn_bwd")
    wab = p["wa"].astype(BF16)
    wxb = p["wx"].astype(BF16)
    dxr, dly, dwa, dba_, dwx, dbx, dlam, dlnw = _lru_bwd(
        dlru, r["xr"], r["proj"], r["h"], wab, p["ba"], wxb, p["bx"], p["lam"], p["lnw"])
    dlx, dlcw, dlcb = _conv_bwd(r["proj"], 0, LRU_W, p["lcw"], p["lcb"], dxr, False, "conv_lru_bwd")
    dx, dpb, dnmw, dsc1, dsh1 = _proj_bwd(dx1, r["x"], dlx, dly, dqkv, dz, dba, p["nmw"], p["sc1"], p["win"])
    g_win = _matmul_tn(r["h1b"], dpb, "dw_in")
    grads = dict(nmw=dnmw, nmlp=dnmlp, sh1=dsh1, sc1=dsc1, g1=dg1, sh2=dsh2, sc2=dsc2, g2=dg2,
                 win=g_win, lcw=dlcw, lcb=dlcb, wa=dwa, ba=dba_, wx=dwx, bx=dbx, lam=dlam, lnw=dlnw,
                 gcw=dgcw, alog=dalog, dtb=ddtb, gnw=dgnw, wout=g_wout, wup=g_wup, wdown=g_wdown)
    return dx, grads, arrived


def _local_step(x, target, fnw, layers):
    res = []
    for p in layers:
        x, r, _ = _layer_fwd(x, p)
        res.append(r)
    dx, loss_blk, dfnw = _loss_head(x, target, fnw)
    grads = [None] * len(layers)
    for l in reversed(range(len(layers))):
        dx, grads[l], _ = _layer_bwd(dx, layers[l], res[l])
    stacked = {k: jnp.stack([g[k] for g in grads]) for k in grads[0]}
    return loss_blk[0, 0], dx, dfnw, stacked


def _prep_layers(norm_mix_w, norm_mlp_w, mod, win_b, lru_conv_w, lru_conv_b, gate_a_w, gate_a_b, gate_x_w, gate_x_b,
                 lru_lambda, lru_norm_w, gdn_conv_w, gdn_a_log, gdn_dt_bias, gdn_norm_w, wout_b, wup_b, wdown_b):
    L = norm_mix_w.shape[0]

    def vec(a):
        return a.reshape(L, 1, -1)

    def lanes(a):
        return jnp.pad(a, ((0, 0), (0, LANES - a.shape[1]))).reshape(L, 1, LANES)

    def taps(w):
        return jnp.pad(w, ((0, 0), (0, SUBLANES - w.shape[1]), (0, 0)))

    m = mod.reshape(L, N_MOD, 1, D_MODEL)
    return dict(
        nmw=vec(norm_mix_w), nmlp=vec(norm_mlp_w),
        sh1=m[:, 0], sc1=m[:, 1], g1=m[:, 2], sh2=m[:, 3], sc2=m[:, 4], g2=m[:, 5],
        win=win_b, lcw=taps(lru_conv_w), lcb=vec(lru_conv_b),
        wa=jax.vmap(_block_diag)(gate_a_w), ba=vec(gate_a_b), wx=jax.vmap(_block_diag)(gate_x_w), bx=vec(gate_x_b),
        lam=vec(lru_lambda), lnw=vec(lru_norm_w),
        gcw=taps(gdn_conv_w), gcb=jnp.zeros((L, 1, 3 * GDN_W), F32),
        alog=lanes(gdn_a_log), dtb=lanes(gdn_dt_bias), gnw=vec(gdn_norm_w),
        wout=wout_b, wup=wup_b, wdown=wdown_b)


def _position():
    x, y, c = lax.axis_index("x"), lax.axis_index("y"), lax.axis_index("c")
    return x, y, c


def _other_chips(x, y):
    return [(1 - x, y), (x, 1 - y), (1 - x, 1 - y)]


def _all_gather_rows(block, name):
    m, n = block.shape

    def body(x_ref, out_ref, send_sems, recv_sems, local_sem):
        x, y, c = _position()
        me, sibling = (x, y, c), (x, y, 1 - c)
        chips = _other_chips(x, y)

        def rows(px, py, pc):
            return out_ref.at[pl.ds((4 * px + 2 * py + pc) * m, m), :]

        def copy(k, blk, to, src=None):
            return pltpu.make_async_remote_copy(
                src_ref=rows(*blk) if src is None else src, dst_ref=rows(*blk),
                send_sem=send_sems.at[k], recv_sem=recv_sems.at[k], device_id=to, device_id_type=MESH)

        mine = pltpu.make_async_copy(x_ref, rows(*me), local_sem)
        mine.start()
        first = [copy(0, me, sibling, src=x_ref)]
        first += [copy(1 + j, me, (*chip, c), src=x_ref) for j, chip in enumerate(chips)]
        for cp in first:
            cp.start()
        passed = [copy(4 + j, (*chip, c), sibling) for j, chip in enumerate(chips)]
        for j, chip in enumerate(chips):
            copy(1 + j, (*chip, c), me).wait_recv()
            passed[j].start()
        copy(0, sibling, me).wait_recv()
        for j, chip in enumerate(chips):
            copy(4 + j, (*chip, 1 - c), me).wait_recv()
        for cp in first + passed:
            cp.wait_send()
        mine.wait()

    return pl.pallas_call(
        body, name=name,
        out_shape=_sds((N_DEV * m, n)),
        in_specs=[pl.BlockSpec(memory_space=pltpu.VMEM)],
        out_specs=pl.BlockSpec(memory_space=pltpu.VMEM),
        scratch_shapes=[pltpu.SemaphoreType.DMA((7,)), pltpu.SemaphoreType.DMA((7,)), pltpu.SemaphoreType.DMA],
        compiler_params=pltpu.CompilerParams(vmem_limit_bytes=VMEM_LIMIT),
    )(block)


def _hbm_specs(n):
    return [pl.BlockSpec(memory_space=pl.ANY)] * n


def _gather_chips(shards, name):
    n = len(shards)

    def body(*refs):
        start, finish = _gather_steps(refs[:n], refs[n:2 * n], *refs[2 * n:])
        start()
        finish()

    return pl.pallas_call(
        body, name=name,
        out_shape=_gather_out_shapes(shards), in_specs=_hbm_specs(n), out_specs=_hbm_specs(n),
        scratch_shapes=_gather_scratch(n),
    )(*shards)


def _gather_out_shapes(shards):
    return [_sds((N_CHIPS, 2, s.shape[0] // 2, s.shape[1]), s.dtype) for s in shards]


def _gather_scratch(n):
    return [pltpu.SemaphoreType.DMA((6 * n,)), pltpu.SemaphoreType.DMA((6 * n,))]


def _gather_steps(ins, outs, send_sems, recv_sems):
    n = len(ins)
    x, y, c = _position()
    chips = _other_chips(x, y)
    me = 2 * x + y

    def first(a, j, slot):
        h = ins[a].shape[0] // 2
        return pltpu.make_async_remote_copy(
            src_ref=ins[a].at[pl.ds(pl.multiple_of(c * h, SUBLANES), h)], dst_ref=outs[a].at[slot, c],
            send_sem=send_sems.at[3 * a + j], recv_sem=recv_sems.at[3 * a + j],
            device_id=(chips[j][0], chips[j][1], c), device_id_type=MESH)

    def second(a, j, half):
        slot = 2 * chips[j][0] + chips[j][1]
        return pltpu.make_async_remote_copy(
            src_ref=outs[a].at[slot, c], dst_ref=outs[a].at[slot, half],
            send_sem=send_sems.at[3 * (n + a) + j], recv_sem=recv_sems.at[3 * (n + a) + j],
            device_id=(x, y, 1 - c), device_id_type=MESH)

    def start():
        for a in range(n):
            for j in range(3):
                first(a, j, me).start()

    def finish():
        for a in range(n):
            for j, (px, py) in enumerate(chips):
                first(a, j, 2 * px + py).wait_recv()
                second(a, j, c).start()
        for a in range(n):
            for j in range(3):
                second(a, j, 1 - c).wait_recv()
        for a in range(n):
            for j in range(3):
                first(a, j, me).wait_send()
                second(a, j, c).wait_send()

    return start, finish


def _send_to_sibling(parts, name):
    n = len(parts)

    def body(*refs):
        ins, outs = refs[:n], refs[n:2 * n]
        send_sems, recv_sems = refs[2 * n:]
        x, y, c = _position()
        copies = [pltpu.make_async_remote_copy(
            src_ref=ins[a].at[1 - c], dst_ref=outs[a], send_sem=send_sems.at[a], recv_sem=recv_sems.at[a],
            device_id=(x, y, 1 - c), device_id_type=MESH) for a in range(n)]
        for cp in copies:
            cp.start()
        for cp in copies:
            cp.wait()

    return pl.pallas_call(
        body, name=name,
        out_shape=[_sds(p.shape[1:], p.dtype) for p in parts],
        in_specs=_hbm_specs(n), out_specs=_hbm_specs(n),
        scratch_shapes=[pltpu.SemaphoreType.DMA((n,)), pltpu.SemaphoreType.DMA((n,))],
    )(*parts)


def _scatter_chips(parts, name):
    n = len(parts)

    def body(*refs):
        start, finish = _scatter_steps(refs[:n], refs[n:2 * n], *refs[2 * n:])
        start()
        finish()

    return pl.pallas_call(
        body, name=name,
        out_shape=[_sds(p.shape, p.dtype) for p in parts], in_specs=_hbm_specs(n), out_specs=_hbm_specs(n),
        scratch_shapes=_scatter_scratch(n),
    )(*parts)


def _scatter_scratch(n):
    return [pltpu.SemaphoreType.DMA((3 * n,)), pltpu.SemaphoreType.DMA((3 * n,))]


def _scatter_steps(ins, outs, send_sems, recv_sems):
    n = len(ins)
    x, y, c = _position()
    chips = _other_chips(x, y)
    me = 2 * x + y

    def copy(a, j, src_slot, dst_slot):
        px, py = chips[j]
        return pltpu.make_async_remote_copy(
            src_ref=ins[a].at[src_slot], dst_ref=outs[a].at[dst_slot], send_sem=send_sems.at[3 * a + j],
            recv_sem=recv_sems.at[3 * a + j], device_id=(px, py, c), device_id_type=MESH)

    def start():
        for a in range(n):
            for j in range(3):
                copy(a, j, 2 * chips[j][0] + chips[j][1], me).start()

    def finish():
        for a in range(n):
            for j, (px, py) in enumerate(chips):
                copy(a, j, me, 2 * px + py).wait_recv()
        for a in range(n):
            for j in range(3):
                copy(a, j, 2 * chips[j][0] + chips[j][1], me).wait_send()

    return start, finish


def _swap_row_halves(arrays, name):
    n = len(arrays)

    def body(*refs):
        outs = refs[n:2 * n]
        send_sems, recv_sems = refs[2 * n:]
        x, y, c = _position()

        def copy(a, half):
            h = outs[a].shape[0] // 2
            rows = outs[a].at[pl.ds(pl.multiple_of(half * h, SUBLANES), h)]
            return pltpu.make_async_remote_copy(
                src_ref=rows, dst_ref=rows, send_sem=send_sems.at[a], recv_sem=recv_sems.at[a],
                device_id=(x, y, 1 - c), device_id_type=MESH)

        sends = [copy(a, c) for a in range(n)]
        for cp in sends:
            cp.start()
        for a in range(n):
            copy(a, 1 - c).wait_recv()
        for cp in sends:
            cp.wait_send()

    return pl.pallas_call(
        body, name=name,
        out_shape=[_sds(a.shape, a.dtype) for a in arrays],
        in_specs=_hbm_specs(n), out_specs=_hbm_specs(n),
        input_output_aliases={a: a for a in range(n)},
        scratch_shapes=[pltpu.SemaphoreType.DMA((n,)), pltpu.SemaphoreType.DMA((n,))],
    )(*arrays)


def _row_tile(rows):
    for t in (512, 256, 128, 64, 32, 16, 8):
        if rows % t == 0:
            return t
    return rows


def _sum_slots(buf, name):
    k, rows, cols = buf.shape
    tm = _row_tile(rows)

    def body(b_ref, o_ref):
        s = b_ref[0]
        for i in range(1, k):
            s = s + b_ref[i]
        o_ref[...] = s

    return pl.pallas_call(
        body, name=name, grid=(rows // tm,),
        in_specs=[pl.BlockSpec((k, tm, cols), lambda i: (0, i, 0))],
        out_specs=pl.BlockSpec((tm, cols), lambda i: (i, 0)),
        out_shape=_sds((rows, cols)),
        compiler_params=_params(("arbitrary",)),
    )(buf)


def _pair_add(part, from_sibling, core, name):
    _, k, h, cols = part.shape
    rows = k * h
    tm = _row_tile(rows)

    def body(core_ref, a_ref, b_ref, o_ref):
        o_ref[...] = (a_ref[0] + b_ref[...]).astype(BF16)

    out = pl.pallas_call(
        body, name=name,
        grid_spec=pltpu.PrefetchScalarGridSpec(
            num_scalar_prefetch=1, grid=(rows // tm,),
            in_specs=[pl.BlockSpec((1, tm, cols), lambda i, cr: (cr[0], i, 0)),
                      pl.BlockSpec((tm, cols), lambda i, cr: (i, 0))],
            out_specs=pl.BlockSpec((tm, cols), lambda i, cr: (i, 0))),
        out_shape=_sds((rows, cols), BF16),
        compiler_params=_params(("arbitrary",)),
    )(core, part.reshape(2, rows, cols), from_sibling.reshape(rows, cols))
    return out.reshape(k, h, cols)


def _sum_adam(arrived, own, w, m, v, layer, place, name):
    _, h, cols = arrived.shape
    tm = min(256, h)
    nb = h // tm

    def body(place_ref, arr_ref, own_ref, w_ref, m_ref, v_ref, g_ref, d_ref, nm_ref, nv_ref):
        for chip in range(N_CHIPS):
            @pl.when(place_ref[1] == chip)
            def _():
                terms = [own_ref[0] if j == chip else arr_ref[j] for j in range(N_CHIPS)]
                g = terms[0].astype(F32)
                for t in terms[1:]:
                    g = g + t.astype(F32)
                g_ref[...] = g
                d, nm, nv = _adam_math(w_ref[0], g, m_ref[0], v_ref[0])
                d_ref[...] = d
                nm_ref[...] = nm
                nv_ref[...] = nv

    state = pl.BlockSpec((1, tm, cols), lambda i, pr: (layer, pr[0] * nb + i, 0))
    result = pl.BlockSpec((tm, cols), lambda i, pr: (pr[0] * nb + i, 0))
    return pl.pallas_call(
        body, name=name,
        grid_spec=pltpu.PrefetchScalarGridSpec(
            num_scalar_prefetch=1, grid=(nb,),
            in_specs=[pl.BlockSpec((N_CHIPS, tm, cols), lambda i, pr: (0, i, 0)),
                      pl.BlockSpec((1, tm, cols), lambda i, pr: (pr[1], i, 0)), state, state, state],
            out_specs=[result] * 4),
        out_shape=[_sds((2 * h, cols))] * 4,
        compiler_params=_params(("arbitrary",)),
    )(place, arrived, own, w, m, v)


def _adam_math(w, g, m, v):
    m = ADAM_B1 * m + (1.0 - ADAM_B1) * g
    v = ADAM_B2 * v + (1.0 - ADAM_B2) * jnp.square(g)
    m_hat = m / (1.0 - ADAM_B1 ** ADAM_STEP)
    v_hat = v / (1.0 - ADAM_B2 ** ADAM_STEP)
    delta = -ADAM_LR * (m_hat / (jnp.sqrt(v_hat) + ADAM_EPS) + ADAM_WD * w)
    return delta, m, v


def _adam(w, g, m, v, name):
    rows, cols = w.shape
    tm = _row_tile(rows)

    def body(w_ref, g_ref, m_ref, v_ref, d_ref, nm_ref, nv_ref):
        d, nm, nv = _adam_math(w_ref[...], g_ref[...], m_ref[...], v_ref[...])
        d_ref[...] = d
        nm_ref[...] = nm
        nv_ref[...] = nv

    spec = pl.BlockSpec((tm, cols), lambda i: (i, 0))
    return pl.pallas_call(
        body, name=name, grid=(rows // tm,), in_specs=[spec] * 4, out_specs=[spec] * 3,
        out_shape=[_sds((rows, cols))] * 3, compiler_params=_params(("arbitrary",)),
    )(w, g, m, v)


def _mod_fwd(c_all, w_mod, b_mod_cols):
    L, _, n = w_mod.shape

    def body(c_ref, w_ref, b_ref, o_ref):
        o_ref[0] = _hdot(_silu(c_ref[...]), w_ref[0]) + b_ref[0]

    return pl.pallas_call(
        body, name="mod_fwd", grid=(L,),
        in_specs=[_const((N_DEV, D_MODEL)), pl.BlockSpec((1, D_MODEL, n), lambda l: (l, 0, 0)),
                  pl.BlockSpec((1, 1, n), lambda l: (l, 0, 0))],
        out_specs=pl.BlockSpec((1, N_DEV, n), lambda l: (l, 0, 0)),
        out_shape=_sds((L, N_DEV, n)),
        compiler_params=_params(("arbitrary",)),
    )(c_all, w_mod, b_mod_cols)


def _mod_update(c_all, dmod, w, m, v):
    L, _, n = w.shape
    tn = 512

    def body(c_ref, d_ref, w_ref, m_ref, v_ref, g_ref, dl_ref, nm_ref, nv_ref):
        g = _hdot_tn(_silu(c_ref[...]), d_ref[0])
        g_ref[0] = g
        d, nm, nv = _adam_math(w_ref[0], g, m_ref[0], v_ref[0])
        dl_ref[0] = d
        nm_ref[0] = nm
        nv_ref[0] = nv

    big = pl.BlockSpec((1, D_MODEL, tn), lambda l, j: (l, 0, j))
    return pl.pallas_call(
        body, name="mod_update", grid=(L, n // tn),
        in_specs=[_const((N_DEV, D_MODEL)), pl.BlockSpec((1, N_DEV, tn), lambda l, j: (l, 0, j)), big, big, big],
        out_specs=[big] * 4, out_shape=[_sds(w.shape)] * 4,
        compiler_params=_params(("arbitrary", "arbitrary")),
    )(c_all, dmod, w, m, v)


def _pack_rows(parts, row_multiple):
    flat = jnp.concatenate([p.reshape(-1) for p in parts])
    unit = row_multiple * LANES
    flat = jnp.pad(flat, (0, (-flat.shape[0]) % unit))
    return flat.reshape(-1, LANES)


def _unpack(packed, shapes):
    flat = packed.reshape(-1)
    out, off = [], 0
    for s in shapes:
        n = 1
        for d in s:
            n *= d
        out.append(flat[off:off + n].reshape(s))
        off += n
    return out


def _lane_pad(a):
    return jnp.pad(a, ((0, 0), (0, LANES - a.shape[1])))


WEIGHT_NAMES = ("norm_mix_w", "norm_mlp_w", "w_mod", "b_mod", "w_in", "lru_conv_w", "lru_conv_b", "lru_gate_a_w",
                "lru_gate_a_b", "lru_gate_x_w", "lru_gate_x_b", "lru_lambda", "lru_norm_w", "gdn_conv_w", "gdn_a_log",
                "gdn_dt_bias", "gdn_norm_w", "w_out", "w_up", "w_down", "final_norm_w")


def kernel(x, c, norm_mix_w, norm_mlp_w, w_mod, b_mod, w_in, lru_conv_w, lru_conv_b, lru_gate_a_w, lru_gate_a_b, lru_gate_x_w, lru_gate_x_b, lru_lambda, lru_norm_w, gdn_conv_w, gdn_a_log, gdn_dt_bias, gdn_norm_w, w_out, w_up, w_down, final_norm_w, loss_target, m_norm_mix_w, m_norm_mlp_w, m_w_mod, m_b_mod, m_w_in, m_lru_conv_w, m_lru_conv_b, m_lru_gate_a_w, m_lru_gate_a_b, m_lru_gate_x_w, m_lru_gate_x_b, m_lru_lambda, m_lru_norm_w, m_gdn_conv_w, m_gdn_a_log, m_gdn_dt_bias, m_gdn_norm_w, m_w_out, m_w_up, m_w_down, m_final_norm_w, v_norm_mix_w, v_norm_mlp_w, v_w_mod, v_b_mod, v_w_in, v_lru_conv_w, v_lru_conv_b, v_lru_gate_a_w, v_lru_gate_a_b, v_lru_gate_x_w, v_lru_gate_x_b, v_lru_lambda, v_lru_norm_w, v_gdn_conv_w, v_gdn_a_log, v_gdn_dt_bias, v_gdn_norm_w, v_w_out, v_w_up, v_w_down, v_final_norm_w):
    W = dict(zip(WEIGHT_NAMES, (norm_mix_w, norm_mlp_w, w_mod, b_mod, w_in, lru_conv_w, lru_conv_b, lru_gate_a_w,
                                lru_gate_a_b, lru_gate_x_w, lru_gate_x_b, lru_lambda, lru_norm_w, gdn_conv_w, gdn_a_log,
                                gdn_dt_bias, gdn_norm_w, w_out, w_up, w_down, final_norm_w)))
    M = dict(zip(WEIGHT_NAMES, (m_norm_mix_w, m_norm_mlp_w, m_w_mod, m_b_mod, m_w_in, m_lru_conv_w, m_lru_conv_b,
                                m_lru_gate_a_w, m_lru_gate_a_b, m_lru_gate_x_w, m_lru_gate_x_b, m_lru_lambda,
                                m_lru_norm_w, m_gdn_conv_w, m_gdn_a_log, m_gdn_dt_bias, m_gdn_norm_w, m_w_out, m_w_up,
                                m_w_down, m_final_norm_w)))
    V = dict(zip(WEIGHT_NAMES, (v_norm_mix_w, v_norm_mlp_w, v_w_mod, v_b_mod, v_w_in, v_lru_conv_w, v_lru_conv_b,
                                v_lru_gate_a_w, v_lru_gate_a_b, v_lru_gate_x_w, v_lru_gate_x_b, v_lru_lambda,
                                v_lru_norm_w, v_gdn_conv_w, v_gdn_a_log, v_gdn_dt_bias, v_gdn_norm_w, v_w_out, v_w_up,
                                v_w_down, v_final_norm_w)))
    L = DEPTH
    xi, yi, ci = _position()
    chip = 2 * xi + yi
    dev = 2 * chip + ci
    lcs = LRU_W // N_CHIPS
    gcs = 3 * GDN_W // N_CHIPS
    mcs = N_MOD * D_MODEL // N_CHIPS

    g_in = _all_gather_rows(_pack_rows([c, lru_conv_w, gdn_conv_w], SUBLANES), "gather_small_inputs").reshape(N_DEV, -1)
    c_all = g_in[:, :D_MODEL]
    per_chip = g_in[0::2]
    o1 = D_MODEL + L * 4 * lcs
    lcw_full = per_chip[:, D_MODEL:o1].reshape(N_CHIPS, L, 4, lcs).transpose(1, 2, 0, 3).reshape(L, 4, LRU_W)
    gcw_full = per_chip[:, o1:o1 + L * 4 * gcs].reshape(N_CHIPS, L, 4, gcs).transpose(1, 2, 0, 3).reshape(L, 4, 3 * GDN_W)

    b_cols = lax.dynamic_slice(b_mod, (0, chip * mcs), (L, mcs)).reshape(L, 1, mcs)
    modp = _mod_fwd(c_all, w_mod, b_cols)
    g_mod = _all_gather_rows(modp.reshape(L * N_DEV, mcs), "gather_mod").reshape(N_DEV, L, N_DEV, mcs)
    mod = lax.dynamic_index_in_dim(g_mod[0::2], dev, axis=2, keepdims=False).transpose(1, 0, 2).reshape(L, N_MOD * D_MODEL)

    stacked = _prep_layers(norm_mix_w, norm_mlp_w, mod, None, lcw_full, lru_conv_b, lru_gate_a_w, lru_gate_a_b,
                           lru_gate_x_w, lru_gate_x_b, lru_lambda, lru_norm_w, gcw_full, gdn_a_log, gdn_dt_bias,
                           gdn_norm_w, None, None, None)
    shards = [[w_in[l].astype(BF16), w_out[l].astype(BF16), w_up[l].astype(BF16), w_down[l].astype(BF16)]
              for l in range(L)]

    def layer_params(l, gathered):
        win_g, wout_g, wup_g, wdown_g = (
            lax.dynamic_update_slice(got, own.reshape((1,) + got.shape[1:]), (chip, 0, 0, 0)).reshape(
                (N_CHIPS,) + own.shape) for got, own in zip(gathered, shards[l]))
        p = {k: v[l] for k, v in stacked.items() if v is not None}
        p["win"] = jnp.pad(win_g.transpose(1, 0, 2).reshape(D_MODEL, IN_COLS), ((0, 0), (0, IN_PAD - IN_COLS)))
        p["wout"] = wout_g.reshape(D_MODEL, D_MODEL)
        p["wup"] = wup_g.transpose(1, 0, 2).reshape(D_MODEL, D_FF)
        p["wdown"] = wdown_g.reshape(D_FF, D_MODEL)
        return p

    layers = [layer_params(0, _gather_chips(shards[0], "gather_weights"))]
    xs = x[0]
    res = []
    for l in range(L):
        xs, r, gathered = _layer_fwd(xs, layers[l], shards[l + 1] if l + 1 < L else ())
        res.append(r)
        if l + 1 < L:
            layers.append(layer_params(l + 1, gathered))
    dx, loss_blk, dfnw = _loss_head(xs, loss_target[0], final_norm_w.reshape(1, D_MODEL))
    loss_local = loss_blk[0, 0]

    big_names = ["w_in", "w_out", "w_up", "w_down"]
    core = jnp.reshape(ci, (1,)).astype(jnp.int32)
    place = jnp.stack([ci, chip]).astype(jnp.int32)
    layer_grads = [None] * L
    big = [None] * L

    def apply_update(l, arrived, pair):
        halves = []
        for nm, a, own in zip(big_names, arrived, pair):
            halves += _sum_adam(a, own, W[nm], M[nm], V[nm], l, place, "sum_adam_" + nm)
        big[l] = _swap_row_halves(halves, "pair_swap")

    pair = ()
    for l in reversed(range(L)):
        dx, gl, arrived = _layer_bwd(dx, layers[l], res[l], sharded=True, scatter=pair)
        if pair:
            apply_update(l + 1, arrived, pair)
        layer_grads[l] = gl
        gwin = gl["win"][:, :IN_COLS].reshape(2, D_MODEL // 2, N_CHIPS, IN_COLS // N_CHIPS).transpose(0, 2, 1, 3)
        parts = [gwin, gl["wout"], gl["wup"], gl["wdown"]]
        from_sibling = _send_to_sibling(parts, "pair_send")
        pair = [_pair_add(p, r, core, "pair_add_" + nm) for nm, p, r in zip(big_names, parts, from_sibling)]
    apply_update(0, _scatter_chips(pair, "chip_scatter"), pair)
    small_keys = [k for k in layer_grads[0] if k not in ("win", "wout", "wup", "wdown")]
    g = {k: jnp.stack([gl[k] for gl in layer_grads]) for k in small_keys}
    loss = lax.psum(loss_local, ("x", "y", "c"))

    dmod = jnp.concatenate([g["sh1"], g["sc1"], g["g1"], g["sh2"], g["sc2"], g["g2"]], axis=-1)
    small = [dmod, g["nmw"], g["nmlp"], g["lcw"][:, :4], g["lcb"], jax.vmap(_diag_blocks)(g["wa"]), g["ba"],
             jax.vmap(_diag_blocks)(g["wx"]), g["bx"], g["lam"], g["lnw"], g["gcw"][:, :4], g["alog"], g["dtb"],
             g["gnw"], dfnw]
    small_shapes = [(L, N_MOD * D_MODEL), (L, D_MODEL), (L, D_MODEL), (L, 4, LRU_W), (L, LRU_W),
                    (L, LRU_BLOCKS, LRU_BLOCK, LRU_BLOCK), (L, LRU_W), (L, LRU_BLOCKS, LRU_BLOCK, LRU_BLOCK),
                    (L, LRU_W), (L, LRU_W), (L, LRU_W), (L, 4, 3 * GDN_W), (L, LANES), (L, LANES), (L, LANES),
                    (D_MODEL,)]
    small_names = ["b_mod", "norm_mix_w", "norm_mlp_w", None, "lru_conv_b", "lru_gate_a_w", "lru_gate_a_b",
                   "lru_gate_x_w", "lru_gate_x_b", "lru_lambda", "lru_norm_w", None, "gdn_a_log", "gdn_dt_bias",
                   "gdn_norm_w", "final_norm_w"]
    pack_g = _pack_rows(small, 512)
    rows = pack_g.shape[0]
    all_g = _all_gather_rows(pack_g, "gather_small_grads").reshape(N_DEV, rows, LANES)
    tot = _sum_slots(all_g, "sum_small_grads")
    tot_parts = _unpack(tot, small_shapes)

    def pack_state(S_):
        parts = []
        for nm, shp in zip(small_names, small_shapes):
            if nm is None:
                parts.append(jnp.zeros(shp, F32))
            elif nm in ("gdn_a_log", "gdn_dt_bias"):
                parts.append(_lane_pad(S_[nm]))
            else:
                parts.append(S_[nm])
        return _pack_rows(parts, 512)

    upd = _adam(pack_state(W), tot, pack_state(M), pack_state(V), "adam_small")
    upd_parts = [_unpack(u, small_shapes) for u in upd]

    grads, deltas, new_m, new_v = {}, {}, {}, {}
    for k, nm in enumerate(small_names):
        if nm is None:
            continue
        cut = (lambda a: a[:, :HEADS]) if nm in ("gdn_a_log", "gdn_dt_bias") else (lambda a: a)
        grads[nm] = cut(tot_parts[k])
        deltas[nm], new_m[nm], new_v[nm] = (cut(u[k]) for u in upd_parts)

    g_lcw = lax.dynamic_slice(tot_parts[3], (0, 0, chip * lcs), (L, 4, lcs))
    g_gcw = lax.dynamic_slice(tot_parts[11], (0, 0, chip * gcs), (L, 4, gcs))
    conv_shapes = [(L, 4, lcs), (L, 4, gcs)]
    conv_pack = lambda a, b: _pack_rows([a, b], SUBLANES)
    cu = _adam(conv_pack(lru_conv_w, gdn_conv_w), conv_pack(g_lcw, g_gcw), conv_pack(m_lru_conv_w, m_gdn_conv_w),
               conv_pack(v_lru_conv_w, v_gdn_conv_w), "adam_conv")
    cu_parts = [_unpack(u, conv_shapes) for u in cu]
    for k, nm in enumerate(("lru_conv_w", "gdn_conv_w")):
        grads[nm] = (g_lcw, g_gcw)[k]
        deltas[nm], new_m[nm], new_v[nm] = (u[k] for u in cu_parts)

    dmod_all = all_g[:, :L * N_MOD * D_MODEL // LANES].reshape(N_DEV, L, N_MOD * D_MODEL)
    dmod_cols = lax.dynamic_slice(dmod_all, (0, 0, chip * mcs), (N_DEV, L, mcs)).transpose(1, 0, 2)
    grads["w_mod"], deltas["w_mod"], new_m["w_mod"], new_v["w_mod"] = _mod_update(c_all, dmod_cols, w_mod, m_w_mod, v_w_mod)

    for k, nm in enumerate(big_names):
        grads[nm], deltas[nm], new_m[nm], new_v[nm] = (jnp.stack([big[l][4 * k + i] for l in range(L)])
                                                       for i in range(4))

    out = [loss, dx[None]]
    for group in (grads, deltas, new_m, new_v):
        out += [group[nm].reshape(W[nm].shape) for nm in WEIGHT_NAMES]
    return tuple(out)
```

```python
import functools

import jax
import jax.numpy as jnp
from jax import lax
from jax.experimental import pallas as pl
from jax.experimental.pallas import tpu as pltpu

F32 = jnp.float32
BF16 = jnp.bfloat16
MESH = pl.DeviceIdType.MESH

D_MODEL = 1024
DEPTH = 4
LRU_W = 512
LRU_BLOCKS = 8
LRU_BLOCK = 64
LRU_C = 8.0
HEADS = 4
HEAD_DIM = 128
GDN_W = 512
CHUNK = 128
D_FF = 4096
N_MOD = 6
IN_COLS = 3080
IN_PAD = 3200
NORM_EPS = 1e-6
LANES = 128
SUBLANES = 8
N_DEV = 8
N_CHIPS = 4

ADAM_LR = 0.001
ADAM_B1 = 0.9
ADAM_B2 = 0.999
ADAM_EPS = 1e-08
ADAM_WD = 0.01
ADAM_STEP = 10

VMEM_LIMIT = 56 * 1024 * 1024
HI = lax.Precision.HIGHEST


def _sds(shape, dtype=F32):
    return jax.ShapeDtypeStruct(tuple(shape), dtype)


def _params(sem=None, vmem=VMEM_LIMIT):
    return pltpu.CompilerParams(dimension_semantics=sem, vmem_limit_bytes=vmem)


def _const(shape):
    return pl.BlockSpec(tuple(shape), lambda *_: (0,) * len(shape))


def _row(tm, c, col=0):
    return pl.BlockSpec((tm, c), lambda i: (i, col))


def _dot(a, b):
    return jnp.dot(a, b, preferred_element_type=F32)


def _dot_nt(a, b):
    return lax.dot_general(a, b, (((1,), (1,)), ((), ())), preferred_element_type=F32)


def _dot_tn(a, b):
    return lax.dot_general(a, b, (((0,), (0,)), ((), ())), preferred_element_type=F32)


def _hdot(a, b):
    return jnp.dot(a, b, preferred_element_type=F32, precision=HI)


def _hdot_nt(a, b):
    return lax.dot_general(a, b, (((1,), (1,)), ((), ())), preferred_element_type=F32, precision=HI)


def _hdot_tn(a, b):
    return lax.dot_general(a, b, (((0,), (0,)), ((), ())), preferred_element_type=F32, precision=HI)


_DIMS = {"nn": (((1,), (0,)), ((), ())), "nt": (((1,), (1,)), ((), ())), "tn": (((0,), (0,)), ((), ()))}


def _mm_raw(a, b, dims, passes):
    dn = _DIMS[dims]

    def dot(p, q):
        return lax.dot_general(p, q, dn, preferred_element_type=F32)

    a_hi = a.astype(BF16)
    b_hi = b.astype(BF16)
    if passes == 1:
        return dot(a_hi, b_hi)
    a_lo = (a - a_hi.astype(F32)).astype(BF16)
    b_lo = (b - b_hi.astype(F32)).astype(BF16)
    return dot(a_hi, b_hi) + (dot(a_hi, b_lo) + dot(a_lo, b_hi))


@functools.partial(jax.custom_vjp, nondiff_argnums=(2, 3))
def _mm(a, b, dims, passes):
    return _mm_raw(a, b, dims, passes)


def _mm_fwd(a, b, dims, passes):
    return _mm_raw(a, b, dims, passes), (a, b)


def _mm_bwd(dims, passes, res, ct):
    a, b = res
    if dims == "nn":
        return _mm_raw(ct, b, "nt", passes), _mm_raw(a, ct, "tn", passes)
    if dims == "nt":
        return _mm_raw(ct, b, "nn", passes), _mm_raw(ct, a, "tn", passes)
    return _mm_raw(b, ct, "nt", passes), _mm_raw(a, ct, "nn", passes)


_mm.defvjp(_mm_fwd, _mm_bwd)


def _acc(ref, val, first):
    @pl.when(first)
    def _():
        ref[...] = val

    @pl.when(jnp.logical_not(first))
    def _():
        ref[...] += val


def _colsum(v):
    return jnp.sum(v, axis=0, keepdims=True)


def _rms_parts(x):
    r = lax.rsqrt(jnp.mean(x * x, axis=-1, keepdims=True) + NORM_EPS)
    return x * r, r


def _rms_bwd(dy, xh, r, w):
    dxh = dy * w
    dw = _colsum(dy * xh)
    dx = r * (dxh - xh * jnp.mean(dxh * xh, axis=-1, keepdims=True))
    return dx, dw


def _norm_mod(x, w, sc, sh):
    xh, _ = _rms_parts(x)
    return (xh * w) * (1.0 + sc) + sh


def _norm_mod_bwd(dy, x, w, sc):
    xh, r = _rms_parts(x)
    n = xh * w
    dsh = _colsum(dy)
    dsc = _colsum(dy * n)
    dx, dw = _rms_bwd(dy * (1.0 + sc), xh, r, w)
    return dx, dw, dsc, dsh


def _softplus(x):
    return jnp.maximum(x, 0.0) + jnp.log1p(jnp.exp(-jnp.abs(x)))


def _silu(x):
    return x * jax.nn.sigmoid(x)


def _silu_grad(x):
    s = jax.nn.sigmoid(x)
    return s * (1.0 + x * (1.0 - s))


def _roll_dn(x, d):
    return x if d == 0 else pltpu.roll(x, d, 0)


def _roll_up(x, d):
    return x if d == 0 else pltpu.roll(x, x.shape[0] - d, 0)


def _proj_fwd(x, nw, sc, sh, win):
    S = x.shape[0]
    tm = min(512, S)

    def body(x_ref, nw_ref, sc_ref, sh_ref, w_ref, proj_ref, hb_ref):
        hb = _norm_mod(x_ref[...], nw_ref[...], sc_ref[...], sh_ref[...]).astype(BF16)
        hb_ref[...] = hb
        proj_ref[...] = _dot(hb, w_ref[...])

    vec = _const((1, D_MODEL))
    return pl.pallas_call(
        body, name="proj_fwd", grid=(S // tm,),
        in_specs=[_row(tm, D_MODEL), vec, vec, vec, _const((D_MODEL, IN_PAD))],
        out_specs=[_row(tm, IN_PAD), _row(tm, D_MODEL)],
        out_shape=[_sds((S, IN_PAD)), _sds((S, D_MODEL), BF16)],
        compiler_params=_params(("arbitrary",)),
    )(x, nw, sc, sh, win)


def _proj_bwd(dx1, x, dlx, dly, dqkv, dz, dba, nw, sc, win):
    S = x.shape[0]
    tm = min(512, S)

    def body(dx1_ref, x_ref, dlx_ref, dly_ref, dqkv_ref, dz_ref, dba_ref, nw_ref, sc_ref, w_ref,
             dx_ref, dpb_ref, dnw_ref, dsc_ref, dsh_ref):
        i = pl.program_id(0)
        dpb = jnp.concatenate([dlx_ref[...], dly_ref[...], dqkv_ref[...], dz_ref[...], dba_ref[...]],
                              axis=-1).astype(BF16)
        dpb_ref[...] = dpb
        dh = _dot_nt(dpb, w_ref[...])
        dx, dnw, dsc, dsh = _norm_mod_bwd(dh, x_ref[...], nw_ref[...], sc_ref[...])
        dx_ref[...] = dx1_ref[...] + dx
        _acc(dnw_ref, dnw, i == 0)
        _acc(dsc_ref, dsc, i == 0)
        _acc(dsh_ref, dsh, i == 0)

    vec = _const((1, D_MODEL))
    return pl.pallas_call(
        body, name="proj_bwd", grid=(S // tm,),
        in_specs=[_row(tm, D_MODEL), _row(tm, D_MODEL), _row(tm, LRU_W), _row(tm, LRU_W), _row(tm, 3 * GDN_W),
                  _row(tm, GDN_W), _row(tm, LANES), vec, vec,
                  _const((D_MODEL, IN_PAD))],
        out_specs=[_row(tm, D_MODEL), _row(tm, IN_PAD), vec, vec, vec],
        out_shape=[_sds((S, D_MODEL)), _sds((S, IN_PAD), BF16), _sds((1, D_MODEL)), _sds((1, D_MODEL)),
                   _sds((1, D_MODEL))],
        compiler_params=_params(("arbitrary",)),
    )(dx1, x, dlx, dly, dqkv, dz, dba, nw, sc, win)


def _conv_taps(xx, w, tm):
    y = _roll_dn(xx, 3)[SUBLANES:] * w[0:1]
    y = y + _roll_dn(xx, 2)[SUBLANES:] * w[1:2]
    y = y + _roll_dn(xx, 1)[SUBLANES:] * w[2:3]
    y = y + xx[SUBLANES:] * w[3:4]
    return y


def _conv_fwd(src, col0, C, w8, b, act, name):
    S = src.shape[0]
    tm = min(512, S)
    tc = 512
    hb = tm // SUBLANES
    cb0 = col0 // tc

    def body(x_ref, p_ref, w_ref, b_ref, y_ref):
        i = pl.program_id(0)
        prev = jnp.where(i > 0, p_ref[...], 0.0)
        xx = jnp.concatenate([prev, x_ref[...]], axis=0)
        y = _conv_taps(xx, w_ref[...], tm) + b_ref[...]
        y_ref[...] = _silu(y) if act else y

    return pl.pallas_call(
        body, name=name, grid=(S // tm, C // tc),
        in_specs=[pl.BlockSpec((tm, tc), lambda i, j: (i, cb0 + j)),
                  pl.BlockSpec((SUBLANES, tc), lambda i, j: (jnp.maximum(i * hb - 1, 0), cb0 + j)),
                  pl.BlockSpec((SUBLANES, tc), lambda i, j: (0, j)),
                  pl.BlockSpec((1, tc), lambda i, j: (0, j))],
        out_specs=pl.BlockSpec((tm, tc), lambda i, j: (i, j)),
        out_shape=_sds((S, C)),
        compiler_params=_params(("arbitrary", "arbitrary")),
    )(src, src, w8, b)


def _conv_bwd(src, col0, C, w8, b, dyact, act, name):
    S = src.shape[0]
    tm = min(512, S)
    tc = 512
    hb = tm // SUBLANES
    nt = S // tm
    cb0 = col0 // tc
    last_hb = S // SUBLANES - 1

    def body(x_ref, p_ref, n_ref, dy_ref, dyn_ref, w_ref, b_ref, dx_ref, dw_ref, db_ref):
        i = pl.program_id(1)
        w = w_ref[...]
        prev = jnp.where(i > 0, p_ref[...], 0.0)
        xx = jnp.concatenate([prev, x_ref[...], n_ref[...]], axis=0)
        dy = jnp.concatenate([dy_ref[...], jnp.where(i < nt - 1, dyn_ref[...], 0.0)], axis=0)
        if act:
            ypre = _conv_taps(xx, w, tm + SUBLANES) + b_ref[...]
            dy = dy * _silu_grad(ypre)
        dx = dy[:tm] * w[3:4]
        for d in (1, 2, 3):
            dx = dx + _roll_up(dy, d)[:tm] * w[3 - d:4 - d]
        dx_ref[...] = dx
        xt = xx[:tm + SUBLANES]
        dyt = dy[:tm]
        rows = [_colsum(dyt * _roll_dn(xt, 3 - k)[SUBLANES:]) for k in range(4)]
        dw = jnp.concatenate(rows + [jnp.zeros((SUBLANES - 4, tc), F32)], axis=0)
        _acc(dw_ref, dw, i == 0)
        _acc(db_ref, _colsum(dyt), i == 0)

    return pl.pallas_call(
        body, name=name, grid=(C // tc, nt),
        in_specs=[pl.BlockSpec((tm, tc), lambda j, i: (i, cb0 + j)),
                  pl.BlockSpec((SUBLANES, tc), lambda j, i: (jnp.maximum(i * hb - 1, 0), cb0 + j)),
                  pl.BlockSpec((SUBLANES, tc), lambda j, i: (jnp.minimum((i + 1) * hb, last_hb), cb0 + j)),
                  pl.BlockSpec((tm, tc), lambda j, i: (i, j)),
                  pl.BlockSpec((SUBLANES, tc), lambda j, i: (jnp.minimum((i + 1) * hb, last_hb), j)),
                  pl.BlockSpec((SUBLANES, tc), lambda j, i: (0, j)),
                  pl.BlockSpec((1, tc), lambda j, i: (0, j))],
        out_specs=[pl.BlockSpec((tm, tc), lambda j, i: (i, j)),
                   pl.BlockSpec((SUBLANES, tc), lambda j, i: (0, j)),
                   pl.BlockSpec((1, tc), lambda j, i: (0, j))],
        out_shape=[_sds((S, C)), _sds((SUBLANES, C)), _sds((1, C))],
        compiler_params=_params(("arbitrary", "arbitrary")),
    )(src, src, src, dyact, dyact, w8, b)


def _lru_ab(pre_a, pre_x, xr, lam):
    r = jax.nn.sigmoid(pre_a)
    g = jax.nn.sigmoid(pre_x)
    log_sig = -_softplus(-lam)
    log_a = LRU_C * r * log_sig
    a = jnp.exp(log_a)
    t = jnp.tanh(log_a)
    mult = jnp.sqrt(jnp.maximum(-2.0 * t / (1.0 - t), 1e-12))
    return a, mult * (g * xr)


def _lru_tail(h, ly, lnw):
    xh, _ = _rms_parts(h * jax.nn.gelu(ly))
    return xh * lnw


def _scan_down(a, b):
    n = a.shape[0]
    row = lax.broadcasted_iota(jnp.int32, a.shape, 0)
    d = 1
    while d < n:
        keep = row >= d
        a_s = jnp.where(keep, _roll_dn(a, d), 1.0)
        b_s = jnp.where(keep, _roll_dn(b, d), 0.0)
        b = a * b_s + b
        a = a * a_s
        d *= 2
    return a, b


def _scan_up(a, b):
    n = a.shape[0]
    row = lax.broadcasted_iota(jnp.int32, a.shape, 0)
    d = 1
    while d < n:
        keep = row < n - d
        a_s = jnp.where(keep, _roll_up(a, d), 1.0)
        b_s = jnp.where(keep, _roll_up(b, d), 0.0)
        b = a * b_s + b
        a = a * a_s
        d *= 2
    return a, b


LRU_TM = 256


def _lru_fwd(xr, proj, wa, ba, wx, bx, lam, lnw):
    S = xr.shape[0]
    tm = min(LRU_TM, S)

    def body(xr_ref, ly_ref, wa_ref, ba_ref, wx_ref, bx_ref, lam_ref, lnw_ref, out_ref, h_ref, carry):
        i = pl.program_id(0)

        @pl.when(i == 0)
        def _():
            carry[...] = jnp.zeros_like(carry)

        x = xr_ref[...]
        xb = x.astype(BF16)
        pre_a = _dot(xb, wa_ref[...]) + ba_ref[...]
        pre_x = _dot(xb, wx_ref[...]) + bx_ref[...]
        a, b = _lru_ab(pre_a, pre_x, x, lam_ref[...])
        ca, hl = _scan_down(a, b)
        h = hl + ca * carry[0:1, :]
        carry[0:1, :] = h[tm - 1:tm, :]
        h_ref[...] = h
        out_ref[...] = _lru_tail(h, ly_ref[...], lnw_ref[...])

    vec = _const((1, LRU_W))
    mat = _const((LRU_W, LRU_W))
    return pl.pallas_call(
        body, name="lru_fwd", grid=(S // tm,),
        in_specs=[_row(tm, LRU_W), _row(tm, LRU_W, 1), mat, vec, mat, vec, vec, vec],
        out_specs=[_row(tm, LRU_W), _row(tm, LRU_W)],
        out_shape=[_sds((S, LRU_W)), _sds((S, LRU_W))],
        scratch_shapes=[pltpu.VMEM((SUBLANES, LRU_W), F32)],
        compiler_params=_params(("arbitrary",)),
    )(xr, proj, wa, ba, wx, bx, lam, lnw)


def _lru_bwd(dout, xr, proj, h, wa, ba, wx, bx, lam, lnw):
    S = xr.shape[0]
    tm = min(LRU_TM, S)
    nt = S // tm
    hb = tm // SUBLANES

    def rev(col=0):
        return pl.BlockSpec((tm, LRU_W), lambda i: (nt - 1 - i, col))

    def body(dout_ref, xr_ref, ly_ref, h_ref, hp_ref, wa_ref, ba_ref, wx_ref, bx_ref, lam_ref, lnw_ref,
             dxr_ref, dly_ref, dwa_ref, dba_ref, dwx_ref, dbx_ref, dlam_ref, dlnw_ref, carry):
        i = pl.program_id(0)
        first = i == 0

        @pl.when(first)
        def _():
            carry[...] = jnp.zeros_like(carry)

        x = xr_ref[...]
        xb = x.astype(BF16)
        pre_a = _dot(xb, wa_ref[...]) + ba_ref[...]
        pre_x = _dot(xb, wx_ref[...]) + bx_ref[...]
        (a, b), ab_vjp = jax.vjp(_lru_ab, pre_a, pre_x, x, lam_ref[...])
        h_t = h_ref[...]
        _, tail_vjp = jax.vjp(_lru_tail, h_t, ly_ref[...], lnw_ref[...])
        dh, dly, dlnw = tail_vjp(dout_ref[...])
        dly_ref[...] = dly
        row = lax.broadcasted_iota(jnp.int32, a.shape, 0)
        a_next = jnp.where(row == tm - 1, carry[0:1, :], _roll_up(a, 1))
        ca, gl = _scan_up(a_next, dh)
        g = gl + ca * carry[1:2, :]
        carry[0:1, :] = a[0:1, :]
        carry[1:2, :] = g[0:1, :]
        h_before = jnp.where(i == nt - 1, 0.0, hp_ref[SUBLANES - 1:SUBLANES, :])
        h_prev = jnp.where(row == 0, h_before, _roll_dn(h_t, 1))
        dpa, dpx, dx, dlam = ab_vjp((g * h_prev, g))
        dpab = dpa.astype(BF16)
        dpxb = dpx.astype(BF16)
        dxr_ref[...] = dx + _dot_nt(dpab, wa_ref[...]) + _dot_nt(dpxb, wx_ref[...])
        _acc(dwa_ref, _dot_tn(xb, dpab), first)
        _acc(dwx_ref, _dot_tn(xb, dpxb), first)
        _acc(dba_ref, _colsum(dpa), first)
        _acc(dbx_ref, _colsum(dpx), first)
        _acc(dlam_ref, dlam, first)
        _acc(dlnw_ref, dlnw, first)

    vec = _const((1, LRU_W))
    mat = _const((LRU_W, LRU_W))
    return pl.pallas_call(
        body, name="lru_bwd", grid=(nt,),
        in_specs=[rev(), rev(), rev(1), rev(),
                  pl.BlockSpec((SUBLANES, LRU_W), lambda i: (jnp.maximum((nt - 1 - i) * hb - 1, 0), 0)),
                  mat, vec, mat, vec, vec, vec],
        out_specs=[rev(), rev(), mat, vec, mat, vec, vec, vec],
        out_shape=[_sds((S, LRU_W)), _sds((S, LRU_W)), _sds((LRU_W, LRU_W)), _sds((1, LRU_W)),
                   _sds((LRU_W, LRU_W)), _sds((1, LRU_W)), _sds((1, LRU_W)), _sds((1, LRU_W))],
        scratch_shapes=[pltpu.VMEM((SUBLANES, LRU_W), F32)],
        compiler_params=_params(("arbitrary",)),
    )(dout, xr, proj, h, h, wa, ba, wx, bx, lam, lnw)


def _lane_pick(row_or_tile, lane):
    idx = lax.broadcasted_iota(jnp.int32, row_or_tile.shape, 1)
    return jnp.sum(jnp.where(idx == lane, row_or_tile, 0.0), axis=-1, keepdims=True)


def _unit_lower_inverses(los):
    n = los[0].shape[0]
    ri = lax.broadcasted_iota(jnp.int32, (n, n), 0)
    ci = lax.broadcasted_iota(jnp.int32, (n, n), 1)
    invs = [(ri == ci).astype(F32) for _ in los]
    s = 1
    while s < n:
        same_block = (ri & ~(2 * s - 1)) == (ci & ~(2 * s - 1))
        lower_left = same_block & ((ri & s) != 0) & ((ci & s) == 0)
        left = [_mm_raw(inv, jnp.where(lower_left, lo, 0.0), "nn", 3) for inv, lo in zip(invs, los)]
        invs = [inv - _mm_raw(t, inv, "nn", 3) for inv, t in zip(invs, left)]
        s *= 2
    return invs


@jax.custom_vjp
def _unit_lower_inverses_diff(los):
    return _unit_lower_inverses(los)


def _unit_lower_inverses_fwd(los):
    invs = _unit_lower_inverses(los)
    return invs, invs


def _unit_lower_inverses_bwd(invs, cts):
    right = [_mm_raw(ct, inv, "nt", 3) for ct, inv in zip(cts, invs)]
    return ([-_mm_raw(inv, r, "tn", 3) for inv, r in zip(invs, right)],)


_unit_lower_inverses_diff.defvjp(_unit_lower_inverses_fwd, _unit_lower_inverses_bwd)


def _gdn_chunk(qs, ks, vs, ba, alog, dtb, states, inverses=_unit_lower_inverses, mm=_mm_raw):
    C = qs[0].shape[0]
    heads = range(len(qs))
    ri = lax.broadcasted_iota(jnp.int32, (C, C), 0)
    ci = lax.broadcasted_iota(jnp.int32, (C, C), 1)
    causal = ri >= ci
    strict = ri > ci
    tri = causal.astype(F32)
    betas = [jax.nn.sigmoid(_lane_pick(ba, h)) for h in heads]
    gs = [-jnp.exp(_lane_pick(alog, h)) * _softplus(_lane_pick(ba, h + HEADS) + _lane_pick(dtb, h)) for h in heads]
    qn = [q * lax.rsqrt(jnp.sum(q * q, axis=-1, keepdims=True) + 1e-6) * (HEAD_DIM ** -0.5) for q in qs]
    kn = [k * lax.rsqrt(jnp.sum(k * k, axis=-1, keepdims=True) + 1e-6) for k in ks]
    gc = [_hdot(tri, jnp.broadcast_to(g, (C, C))) for g in gs]
    decay = [jnp.where(causal, jnp.exp(jnp.where(causal, c - c.T, 0.0)), 0.0) for c in gc]
    eg = [jnp.exp(c) for c in gc]
    kb = [k * b for k, b in zip(kn, betas)]
    vb = [v * b for v, b in zip(vs, betas)]
    los = [jnp.where(strict, mm(a, k, "nt", 1) * d, 0.0) for a, k, d in zip(kb, kn, decay)]
    attn = [jnp.where(causal, mm(q, k, "nt", 1) * d, 0.0) for q, k, d in zip(qn, kn, decay)]
    tinv = inverses(los)
    u = [mm(t, x, "nn", 3) for t, x in zip(tinv, vb)]
    w = [mm(t, a * e, "nn", 3) for t, a, e in zip(tinv, kb, eg)]
    g_last = [c[C - 1:C, :] for c in gc]
    k_tail = [k * jnp.exp(gl - c) for k, gl, c in zip(kn, g_last, gc)]
    v_new = [a - mm(b, s, "nn", 1) for a, b, s in zip(u, w, states)]
    o_state = [mm(q * e, s, "nn", 1) for q, e, s in zip(qn, eg, states)]
    o = [a + mm(at, vn, "nn", 1) for a, at, vn in zip(o_state, attn, v_new)]
    new_states = [s * jnp.exp(gl) + mm(kt, vn, "tn", 1) for s, gl, kt, vn in zip(states, g_last, k_tail, v_new)]
    return o, new_states


def _gdn_fwd(qkv, proj, alog, dtb, gather=()):
    S = qkv.shape[0]
    nc = S // CHUNK
    nk = len(gather)
    assert CHUNK == HEAD_DIM

    def body(*refs):
        q_ref, k_ref, v_ref, ba_ref, alog_ref, dtb_ref = refs[:6]
        o_ref, st_ref = refs[6 + nk:8 + nk]
        state = refs[8 + 2 * nk]
        if nk:
            start, finish = _gather_steps(refs[6:6 + nk], refs[8 + nk:8 + 2 * nk], *refs[9 + 2 * nk:])
            pl.when(pl.program_id(0) == 0)(start)

        @pl.when(pl.program_id(0) == 0)
        def _():
            state[...] = jnp.zeros_like(state)

        sls = [slice(hd * HEAD_DIM, (hd + 1) * HEAD_DIM) for hd in range(HEADS)]
        s0 = [state[hd] for hd in range(HEADS)]
        for hd in range(HEADS):
            st_ref[hd, 0] = s0[hd]
        o, s1 = _gdn_chunk([q_ref[:, sl] for sl in sls], [k_ref[:, sl] for sl in sls], [v_ref[:, sl] for sl in sls],
                           ba_ref[...], alog_ref[...], dtb_ref[...], s0)
        for hd in range(HEADS):
            o_ref[:, sls[hd]] = o[hd]
            state[hd] = s1[hd]
        if nk:
            pl.when(pl.program_id(0) == nc - 1)(finish)

    def col(j):
        return pl.BlockSpec((CHUNK, GDN_W), lambda n: (n, j))

    vec = _const((1, LANES))
    outs = pl.pallas_call(
        body, name="gdn_fwd", grid=(nc,),
        in_specs=[col(0), col(1), col(2), pl.BlockSpec((CHUNK, LANES), lambda n: (n, IN_PAD // LANES - 1)), vec, vec]
        + _hbm_specs(nk),
        out_specs=[col(0), pl.BlockSpec((HEADS, 1, HEAD_DIM, HEAD_DIM), lambda n: (0, n, 0, 0))] + _hbm_specs(nk),
        out_shape=[_sds((S, GDN_W)), _sds((HEADS, nc, HEAD_DIM, HEAD_DIM))] + (_gather_out_shapes(gather) if nk else []),
        scratch_shapes=[pltpu.VMEM((HEADS, HEAD_DIM, HEAD_DIM), F32)] + (_gather_scratch(nk) if nk else []),
        compiler_params=_params(("arbitrary",)),
    )(qkv, qkv, qkv, proj, alog, dtb, *gather)
    return outs[0], outs[1], list(outs[2:])


def _gdn_bwd(do, qkv, proj, states, alog, dtb, scatter=()):
    S = qkv.shape[0]
    nc = S // CHUNK
    nk = len(scatter)

    def body(*refs):
        do_ref, q_ref, k_ref, v_ref, ba_ref, st_ref, alog_ref, dtb_ref = refs[:8]
        dqkv_ref, dba_ref, dalog_ref, ddtb_ref = refs[8 + nk:12 + nk]
        dstate = refs[12 + 2 * nk]
        n = pl.program_id(0)
        if nk:
            start, finish = _scatter_steps(refs[8:8 + nk], refs[12 + nk:12 + 2 * nk], *refs[13 + 2 * nk:])
            pl.when(n == 0)(start)

        @pl.when(n == 0)
        def _():
            dstate[...] = jnp.zeros_like(dstate)

        sls = [slice(hd * HEAD_DIM, (hd + 1) * HEAD_DIM) for hd in range(HEADS)]
        fn = functools.partial(_gdn_chunk, inverses=_unit_lower_inverses_diff, mm=_mm)
        _, vjp = jax.vjp(fn, [q_ref[:, sl] for sl in sls], [k_ref[:, sl] for sl in sls], [v_ref[:, sl] for sl in sls],
                         ba_ref[...], alog_ref[...], dtb_ref[...], [st_ref[hd, 0] for hd in range(HEADS)])
        dq, dk, dv, dba, dalog, ddtb, ds = vjp(([do_ref[:, sl] for sl in sls], [dstate[hd] for hd in range(HEADS)]))
        for hd in range(HEADS):
            dqkv_ref[:, sls[hd]] = dq[hd]
            dqkv_ref[:, GDN_W + hd * HEAD_DIM:GDN_W + (hd + 1) * HEAD_DIM] = dk[hd]
            dqkv_ref[:, 2 * GDN_W + hd * HEAD_DIM:2 * GDN_W + (hd + 1) * HEAD_DIM] = dv[hd]
            dstate[hd] = ds[hd]
        dba_ref[...] = dba
        _acc(dalog_ref, dalog, n == 0)
        _acc(ddtb_ref, ddtb, n == 0)
        if nk:
            pl.when(n == nc - 1)(finish)

    def col(j):
        return pl.BlockSpec((CHUNK, GDN_W), lambda n: (nc - 1 - n, j))

    vec = _const((1, LANES))
    outs = pl.pallas_call(
        body, name="gdn_bwd", grid=(nc,),
        in_specs=[col(0), col(0), col(1), col(2),
                  pl.BlockSpec((CHUNK, LANES), lambda n: (nc - 1 - n, IN_PAD // LANES - 1)),
                  pl.BlockSpec((HEADS, 1, HEAD_DIM, HEAD_DIM), lambda n: (0, nc - 1 - n, 0, 0)), vec, vec]
        + _hbm_specs(nk),
        out_specs=[pl.BlockSpec((CHUNK, 3 * GDN_W), lambda n: (nc - 1 - n, 0)),
                   pl.BlockSpec((CHUNK, LANES), lambda n: (nc - 1 - n, 0)), vec, vec] + _hbm_specs(nk),
        out_shape=[_sds((S, 3 * GDN_W)), _sds((S, LANES)), _sds((1, LANES)), _sds((1, LANES))]
        + [_sds(p.shape, p.dtype) for p in scatter],
        scratch_shapes=[pltpu.VMEM((HEADS, HEAD_DIM, HEAD_DIM), F32)] + (_scatter_scratch(nk) if nk else []),
        compiler_params=_params(("arbitrary",)),
    )(do, qkv, qkv, qkv, proj, states, alog, dtb, *scatter)
    return outs[0], outs[1], outs[2], outs[3], list(outs[4:])


def _gdn_gate(o, z, gnw):
    outs = []
    for hd in range(HEADS):
        sl = slice(hd * HEAD_DIM, (hd + 1) * HEAD_DIM)
        xh, _ = _rms_parts(o[:, sl])
        outs.append(xh * gnw * _silu(z[:, sl]))
    return jnp.concatenate(outs, axis=-1)


def _out_fwd(x, out_lru, o, proj, gnw, g1, wout):
    S = x.shape[0]
    tm = min(512, S)

    def body(x_ref, lru_ref, o_ref, z_ref, gnw_ref, g1_ref, w_ref, x1_ref, cat_ref):
        cat = jnp.concatenate([lru_ref[...], _gdn_gate(o_ref[...], z_ref[...], gnw_ref[...])], axis=-1).astype(BF16)
        cat_ref[...] = cat
        x1_ref[...] = x_ref[...] + g1_ref[...] * _dot(cat, w_ref[...])

    return pl.pallas_call(
        body, name="out_fwd", grid=(S // tm,),
        in_specs=[_row(tm, D_MODEL), _row(tm, LRU_W), _row(tm, GDN_W), _row(tm, GDN_W, 5), _const((1, LANES)),
                  _const((1, D_MODEL)), _const((D_MODEL, D_MODEL))],
        out_specs=[_row(tm, D_MODEL), _row(tm, D_MODEL)],
        out_shape=[_sds((S, D_MODEL)), _sds((S, D_MODEL), BF16)],
        compiler_params=_params(("arbitrary",)),
    )(x, out_lru, o, proj, gnw, g1, wout)


def _out_bwd(dx1, cat, o, proj, gnw, g1, wout):
    S = dx1.shape[0]
    tm = min(512, S)

    def body(dx1_ref, cat_ref, o_ref, z_ref, gnw_ref, g1_ref, w_ref,
             dlru_ref, do_ref, dz_ref, dmb_ref, dgnw_ref, dg1_ref):
        i = pl.program_id(0)
        d1 = dx1_ref[...]
        mix = _dot(cat_ref[...], w_ref[...])
        _acc(dg1_ref, _colsum(d1 * mix), i == 0)
        dmb = (d1 * g1_ref[...]).astype(BF16)
        dmb_ref[...] = dmb
        dcat = _dot_nt(dmb, w_ref[...])
        dlru_ref[...] = dcat[:, :LRU_W]
        _, vjp = jax.vjp(_gdn_gate, o_ref[...], z_ref[...], gnw_ref[...])
        do, dz, dgnw = vjp(dcat[:, LRU_W:])
        do_ref[...] = do
        dz_ref[...] = dz
        _acc(dgnw_ref, dgnw, i == 0)

    return pl.pallas_call(
        body, name="out_bwd", grid=(S // tm,),
        in_specs=[_row(tm, D_MODEL), _row(tm, D_MODEL), _row(tm, GDN_W), _row(tm, GDN_W, 5), _const((1, LANES)),
                  _const((1, D_MODEL)), _const((D_MODEL, D_MODEL))],
        out_specs=[_row(tm, LRU_W), _row(tm, GDN_W), _row(tm, GDN_W), _row(tm, D_MODEL), _const((1, LANES)),
                   _const((1, D_MODEL))],
        out_shape=[_sds((S, LRU_W)), _sds((S, GDN_W)), _sds((S, GDN_W)), _sds((S, D_MODEL), BF16), _sds((1, LANES)),
                   _sds((1, D_MODEL))],
        compiler_params=_params(("arbitrary",)),
    )(dx1, cat, o, proj, gnw, g1, wout)


MLP_TM = 256


def _load_once(step, pairs, sem):
    @pl.when(step == 0)
    def _():
        copies = [pltpu.make_async_copy(src, dst, sem.at[k]) for k, (src, dst) in enumerate(pairs)]
        for cp in copies:
            cp.start()
        for cp in copies:
            cp.wait()


def _mlp_fwd(x1, nw, sc, sh, g2, wup, wdown):
    S = x1.shape[0]
    tm = min(MLP_TM, S)

    def body(x_ref, nw_ref, sc_ref, sh_ref, g2_ref, wup_hbm, wdown_hbm, x2_ref, wup, wdown, sem):
        _load_once(pl.program_id(0), [(wup_hbm, wup), (wdown_hbm, wdown)], sem)
        x = x_ref[...]
        hb = _norm_mod(x, nw_ref[...], sc_ref[...], sh_ref[...]).astype(BF16)
        r = jnp.maximum(_dot(hb, wup[...]), 0.0)
        x2_ref[...] = x + g2_ref[...] * _dot((r * r).astype(BF16), wdown[...])

    vec = _const((1, D_MODEL))
    anyspec = pl.BlockSpec(memory_space=pl.ANY)
    return pl.pallas_call(
        body, name="mlp_fwd", grid=(S // tm,),
        in_specs=[_row(tm, D_MODEL), vec, vec, vec, vec, anyspec, anyspec],
        out_specs=_row(tm, D_MODEL),
        out_shape=_sds((S, D_MODEL)),
        scratch_shapes=[pltpu.VMEM((D_MODEL, D_FF), BF16), pltpu.VMEM((D_FF, D_MODEL), BF16),
                        pltpu.SemaphoreType.DMA((2,))],
        compiler_params=_params(("arbitrary",)),
    )(x1, nw, sc, sh, g2, wup, wdown)


def _mlp_bwd(dx2, x1, nw, sc, sh, g2, wup, wdown):
    S = x1.shape[0]
    tm = min(MLP_TM, S)

    def body(dx2_ref, x_ref, nw_ref, sc_ref, sh_ref, g2_ref, wup_hbm, wdown_hbm,
             dx1_ref, hb_ref, dupb_ref, actb_ref, ddb_ref, dnw_ref, dsc_ref, dsh_ref, dg2_ref, wup, wdown, sem):
        i = pl.program_id(0)
        _load_once(i, [(wup_hbm, wup), (wdown_hbm, wdown)], sem)
        x = x_ref[...]
        d2 = dx2_ref[...]
        hb = _norm_mod(x, nw_ref[...], sc_ref[...], sh_ref[...]).astype(BF16)
        hb_ref[...] = hb
        r = jnp.maximum(_dot(hb, wup[...]), 0.0)
        actb = (r * r).astype(BF16)
        actb_ref[...] = actb
        down = _dot(actb, wdown[...])
        _acc(dg2_ref, _colsum(d2 * down), i == 0)
        ddb = (d2 * g2_ref[...]).astype(BF16)
        ddb_ref[...] = ddb
        dupb = (_dot_nt(ddb, wdown[...]) * (2.0 * r)).astype(BF16)
        dupb_ref[...] = dupb
        dh = _dot_nt(dupb, wup[...])
        dx, dnw, dsc, dsh = _norm_mod_bwd(dh, x, nw_ref[...], sc_ref[...])
        dx1_ref[...] = d2 + dx
        _acc(dnw_ref, dnw, i == 0)
        _acc(dsc_ref, dsc, i == 0)
        _acc(dsh_ref, dsh, i == 0)

    vec = _const((1, D_MODEL))
    anyspec = pl.BlockSpec(memory_space=pl.ANY)
    return pl.pallas_call(
        body, name="mlp_bwd", grid=(S // tm,),
        in_specs=[_row(tm, D_MODEL), _row(tm, D_MODEL), vec, vec, vec, vec, anyspec, anyspec],
        out_specs=[_row(tm, D_MODEL), _row(tm, D_MODEL), _row(tm, D_FF), _row(tm, D_FF), _row(tm, D_MODEL),
                   vec, vec, vec, vec],
        out_shape=[_sds((S, D_MODEL)), _sds((S, D_MODEL), BF16), _sds((S, D_FF), BF16), _sds((S, D_FF), BF16),
                   _sds((S, D_MODEL), BF16), _sds((1, D_MODEL)), _sds((1, D_MODEL)), _sds((1, D_MODEL)),
                   _sds((1, D_MODEL))],
        scratch_shapes=[pltpu.VMEM((D_MODEL, D_FF), BF16), pltpu.VMEM((D_FF, D_MODEL), BF16),
                        pltpu.SemaphoreType.DMA((2,))],
        compiler_params=_params(("arbitrary",)),
    )(dx2, x1, nw, sc, sh, g2, wup, wdown)


def _matmul_tn(a, b, name, shards=None, out_dtype=F32):
    K, M = a.shape
    N = b.shape[1]
    tk = min(2048, K)
    if shards == "cols":
        tm, tn = M // 2, N // N_CHIPS
        out_spec = pl.BlockSpec((1, 1, tm, tn), lambda i, j, k: (i, j, 0, 0))
        out_shape = _sds((2, N_CHIPS, tm, tn))
    elif shards == "rows":
        h, tn = M // (2 * N_CHIPS), N
        tm = max(512, 2 * h)
        per_tile = tm // (2 * h)
        out_spec = pl.BlockSpec((2, per_tile, h, tn), lambda i, j, k: (0, i, 0, 0))
        out_shape = _sds((2, N_CHIPS, h, tn))
    else:
        tm = min(512, M)
        tn = 640 if N % 640 == 0 else min(1024, N)
        out_spec = pl.BlockSpec((tm, tn), lambda i, j, k: (i, j))
        out_shape = _sds((M, N))
    nk = K // tk

    def body(a_ref, b_ref, o_ref, acc):
        k = pl.program_id(2)
        _acc(acc, _dot_tn(a_ref[...], b_ref[...]), k == 0)

        @pl.when(k == nk - 1)
        def _():
            if shards == "rows":
                for s in range(per_tile):
                    for half in range(2):
                        r0 = (2 * s + half) * h
                        o_ref[half, s] = acc[r0:r0 + h, :].astype(o_ref.dtype)
            else:
                o_ref[...] = acc[...].reshape(o_ref.shape).astype(o_ref.dtype)

    return pl.pallas_call(
        body, name=name, grid=(M // tm, N // tn, nk),
        in_specs=[pl.BlockSpec((tk, tm), lambda i, j, k: (k, i)), pl.BlockSpec((tk, tn), lambda i, j, k: (k, j))],
        out_specs=out_spec, out_shape=_sds(out_shape.shape, out_dtype),
        scratch_shapes=[pltpu.VMEM((tm, tn), F32)],
        compiler_params=_params(("arbitrary", "arbitrary", "arbitrary")),
    )(a, b)


def _loss_head(x, target, fnw):
    S = x.shape[0]
    tm = min(512, S)

    def body(x_ref, t_ref, w_ref, dx_ref, loss_ref, dw_ref):
        i = pl.program_id(0)
        w = w_ref[...]
        xh, r = _rms_parts(x_ref[...])
        err = xh * w - t_ref[...]
        part = 0.5 * jnp.sum(jnp.mean(err * err, axis=-1, keepdims=True), axis=0, keepdims=True)
        _acc(loss_ref, jnp.broadcast_to(part, (SUBLANES, LANES)), i == 0)
        dx, dw = _rms_bwd(err * (1.0 / D_MODEL), xh, r, w)
        dx_ref[...] = dx
        _acc(dw_ref, dw, i == 0)

    vec = _const((1, D_MODEL))
    return pl.pallas_call(
        body, name="loss_head", grid=(S // tm,),
        in_specs=[_row(tm, D_MODEL), _row(tm, D_MODEL), vec],
        out_specs=[_row(tm, D_MODEL), _const((SUBLANES, LANES)), vec],
        out_shape=[_sds((S, D_MODEL)), _sds((SUBLANES, LANES)), _sds((1, D_MODEL))],
        compiler_params=_params(("arbitrary",)),
    )(x, target, fnw)


def _block_diag(w):
    eye = jnp.eye(LRU_BLOCKS, dtype=w.dtype)
    return (eye[:, None, :, None] * w[:, :, None, :]).reshape(LRU_W, LRU_W)


def _diag_blocks(m):
    m4 = m.reshape(LRU_BLOCKS, LRU_BLOCK, LRU_BLOCKS, LRU_BLOCK)
    return jnp.stack([m4[g, :, g, :] for g in range(LRU_BLOCKS)])


def _layer_fwd(x, p, gather=()):
    proj, h1b = _proj_fwd(x, p["nmw"], p["sc1"], p["sh1"], p["win"])
    xr = _conv_fwd(proj, 0, LRU_W, p["lcw"], p["lcb"], False, "conv_lru_fwd")
    out_lru, h = _lru_fwd(xr, proj, p["wa"].astype(BF16), p["ba"], p["wx"].astype(BF16), p["bx"], p["lam"], p["lnw"])
    qkv = _conv_fwd(proj, 2 * LRU_W, 3 * GDN_W, p["gcw"], p["gcb"], True, "conv_gdn_fwd")
    o, states, gathered = _gdn_fwd(qkv, proj, p["alog"], p["dtb"], gather)
    x1, cat = _out_fwd(x, out_lru, o, proj, p["gnw"], p["g1"], p["wout"])
    x2 = _mlp_fwd(x1, p["nmlp"], p["sc2"], p["sh2"], p["g2"], p["wup"], p["wdown"])
    res = dict(x=x, proj=proj, h1b=h1b, xr=xr, h=h, qkv=qkv, o=o, states=states, x1=x1, cat=cat)
    return x2, res, gathered


def _layer_bwd(dx2, p, r, sharded=False, scatter=()):
    dx1, h2b, dupb, actb, ddb, dnmlp, dsc2, dsh2, dg2 = _mlp_bwd(
        dx2, r["x1"], p["nmlp"], p["sc2"], p["sh2"], p["g2"], p["wup"], p["wdown"])
    gdt = BF16 if sharded else F32
    g_wup = _matmul_tn(h2b, dupb, "dw_up", "cols" if sharded else None, gdt)
    g_wdown = _matmul_tn(actb, ddb, "dw_down", "rows" if sharded else None, gdt)
    dlru, do, dz, dmb, dgnw, dg1 = _out_bwd(dx1, r["cat"], r["o"], r["proj"], p["gnw"], p["g1"], p["wout"])
    g_wout = _matmul_tn(r["cat"], dmb, "dw_out", "rows" if sharded else None, gdt)
    dqkv_act, dba, dalog, ddtb, arrived = _gdn_bwd(do, r["qkv"], r["proj"], r["states"], p["alog"], p["dtb"], scatter)
    dqkv, dgcw, _ = _conv_bwd(r["proj"], 2 * LRU_W, 3 * GDN_W, p["gcw"], p["gcb"], dqkv_act, True, "conv_gdn_bwd")
    wab = p["wa"].astype(BF16)
    wxb = p["wx"].astype(BF16)
    dxr, dly, dwa, dba_, dwx, dbx, dlam, dlnw = _lru_bwd(
        dlru, r["xr"], r["proj"], r["h"], wab, p["ba"], wxb, p["bx"], p["lam"], p["lnw"])
    dlx, dlcw, dlcb = _conv_bwd(r["proj"], 0, LRU_W, p["lcw"], p["lcb"], dxr, False, "conv_lru_bwd")
    dx, dpb, dnmw, dsc1, dsh1 = _proj_bwd(dx1, r["x"], dlx, dly, dqkv, dz, dba, p["nmw"], p["sc1"], p["win"])
    g_win = _matmul_tn(r["h1b"], dpb, "dw_in", None, gdt)
    grads = dict(nmw=dnmw, nmlp=dnmlp, sh1=dsh1, sc1=dsc1, g1=dg1, sh2=dsh2, sc2=dsc2, g2=dg2,
                 win=g_win, lcw=dlcw, lcb=dlcb, wa=dwa, ba=dba_, wx=dwx, bx=dbx, lam=dlam, lnw=dlnw,
                 gcw=dgcw, alog=dalog, dtb=ddtb, gnw=dgnw, wout=g_wout, wup=g_wup, wdown=g_wdown)
    return dx, grads, arrived


def _local_step(x, target, fnw, layers):
    res = []
    for p in layers:
        x, r, _ = _layer_fwd(x, p)
        res.append(r)
    dx, loss_blk, dfnw = _loss_head(x, target, fnw)
    grads = [None] * len(layers)
    for l in reversed(range(len(layers))):
        dx, grads[l], _ = _layer_bwd(dx, layers[l], res[l])
    stacked = {k: jnp.stack([g[k] for g in grads]) for k in grads[0]}
    return loss_blk[0, 0], dx, dfnw, stacked


def _prep_layers(norm_mix_w, norm_mlp_w, mod, win_b, lru_conv_w, lru_conv_b, gate_a_w, gate_a_b, gate_x_w, gate_x_b,
                 lru_lambda, lru_norm_w, gdn_conv_w, gdn_a_log, gdn_dt_bias, gdn_norm_w, wout_b, wup_b, wdown_b):
    L = norm_mix_w.shape[0]

    def vec(a):
        return a.reshape(L, 1, -1)

    def lanes(a):
        return jnp.pad(a, ((0, 0), (0, LANES - a.shape[1]))).reshape(L, 1, LANES)

    def taps(w):
        return jnp.pad(w, ((0, 0), (0, SUBLANES - w.shape[1]), (0, 0)))

    m = mod.reshape(L, N_MOD, 1, D_MODEL)
    return dict(
        nmw=vec(norm_mix_w), nmlp=vec(norm_mlp_w),
        sh1=m[:, 0], sc1=m[:, 1], g1=m[:, 2], sh2=m[:, 3], sc2=m[:, 4], g2=m[:, 5],
        win=win_b, lcw=taps(lru_conv_w), lcb=vec(lru_conv_b),
        wa=jax.vmap(_block_diag)(gate_a_w), ba=vec(gate_a_b), wx=jax.vmap(_block_diag)(gate_x_w), bx=vec(gate_x_b),
        lam=vec(lru_lambda), lnw=vec(lru_norm_w),
        gcw=taps(gdn_conv_w), gcb=jnp.zeros((L, 1, 3 * GDN_W), F32),
        alog=lanes(gdn_a_log), dtb=lanes(gdn_dt_bias), gnw=vec(gdn_norm_w),
        wout=wout_b, wup=wup_b, wdown=wdown_b)


def _position():
    x, y, c = lax.axis_index("x"), lax.axis_index("y"), lax.axis_index("c")
    return x, y, c


def _other_chips(x, y):
    return [(1 - x, y), (x, 1 - y), (1 - x, 1 - y)]


def _all_gather_rows(block, name):
    m, n = block.shape

    def body(x_ref, out_ref, send_sems, recv_sems, local_sem):
        x, y, c = _position()
        me, sibling = (x, y, c), (x, y, 1 - c)
        chips = _other_chips(x, y)

        def rows(px, py, pc):
            return out_ref.at[pl.ds((4 * px + 2 * py + pc) * m, m), :]

        def copy(k, blk, to, src=None):
            return pltpu.make_async_remote_copy(
                src_ref=rows(*blk) if src is None else src, dst_ref=rows(*blk),
                send_sem=send_sems.at[k], recv_sem=recv_sems.at[k], device_id=to, device_id_type=MESH)

        mine = pltpu.make_async_copy(x_ref, rows(*me), local_sem)
        mine.start()
        first = [copy(0, me, sibling, src=x_ref)]
        first += [copy(1 + j, me, (*chip, c), src=x_ref) for j, chip in enumerate(chips)]
        for cp in first:
            cp.start()
        passed = [copy(4 + j, (*chip, c), sibling) for j, chip in enumerate(chips)]
        for j, chip in enumerate(chips):
            copy(1 + j, (*chip, c), me).wait_recv()
            passed[j].start()
        copy(0, sibling, me).wait_recv()
        for j, chip in enumerate(chips):
            copy(4 + j, (*chip, 1 - c), me).wait_recv()
        for cp in first + passed:
            cp.wait_send()
        mine.wait()

    return pl.pallas_call(
        body, name=name,
        out_shape=_sds((N_DEV * m, n)),
        in_specs=[pl.BlockSpec(memory_space=pltpu.VMEM)],
        out_specs=pl.BlockSpec(memory_space=pltpu.VMEM),
        scratch_shapes=[pltpu.SemaphoreType.DMA((7,)), pltpu.SemaphoreType.DMA((7,)), pltpu.SemaphoreType.DMA],
        compiler_params=pltpu.CompilerParams(vmem_limit_bytes=VMEM_LIMIT),
    )(block)


def _hbm_specs(n):
    return [pl.BlockSpec(memory_space=pl.ANY)] * n


def _gather_chips(shards, name):
    n = len(shards)

    def body(*refs):
        start, finish = _gather_steps(refs[:n], refs[n:2 * n], *refs[2 * n:])
        start()
        finish()

    return pl.pallas_call(
        body, name=name,
        out_shape=_gather_out_shapes(shards), in_specs=_hbm_specs(n), out_specs=_hbm_specs(n),
        scratch_shapes=_gather_scratch(n),
    )(*shards)


def _gather_out_shapes(shards):
    return [_sds((N_CHIPS, 2, s.shape[0] // 2, s.shape[1]), s.dtype) for s in shards]


def _gather_scratch(n):
    return [pltpu.SemaphoreType.DMA((6 * n,)), pltpu.SemaphoreType.DMA((6 * n,))]


def _gather_steps(ins, outs, send_sems, recv_sems):
    n = len(ins)
    x, y, c = _position()
    chips = _other_chips(x, y)
    me = 2 * x + y

    def first(a, j, slot):
        h = ins[a].shape[0] // 2
        return pltpu.make_async_remote_copy(
            src_ref=ins[a].at[pl.ds(pl.multiple_of(c * h, SUBLANES), h)], dst_ref=outs[a].at[slot, c],
            send_sem=send_sems.at[3 * a + j], recv_sem=recv_sems.at[3 * a + j],
            device_id=(chips[j][0], chips[j][1], c), device_id_type=MESH)

    def second(a, j, half):
        slot = 2 * chips[j][0] + chips[j][1]
        return pltpu.make_async_remote_copy(
            src_ref=outs[a].at[slot, c], dst_ref=outs[a].at[slot, half],
            send_sem=send_sems.at[3 * (n + a) + j], recv_sem=recv_sems.at[3 * (n + a) + j],
            device_id=(x, y, 1 - c), device_id_type=MESH)

    def start():
        for a in range(n):
            for j in range(3):
                first(a, j, me).start()

    def finish():
        for a in range(n):
            for j, (px, py) in enumerate(chips):
                first(a, j, 2 * px + py).wait_recv()
                second(a, j, c).start()
        for a in range(n):
            for j in range(3):
                second(a, j, 1 - c).wait_recv()
        for a in range(n):
            for j in range(3):
                first(a, j, me).wait_send()
                second(a, j, c).wait_send()

    return start, finish


def _send_to_sibling(parts, name):
    n = len(parts)

    def body(*refs):
        ins, outs = refs[:n], refs[n:2 * n]
        send_sems, recv_sems = refs[2 * n:]
        x, y, c = _position()
        copies = [pltpu.make_async_remote_copy(
            src_ref=ins[a].at[1 - c], dst_ref=outs[a], send_sem=send_sems.at[a], recv_sem=recv_sems.at[a],
            device_id=(x, y, 1 - c), device_id_type=MESH) for a in range(n)]
        for cp in copies:
            cp.start()
        for cp in copies:
            cp.wait()

    return pl.pallas_call(
        body, name=name,
        out_shape=[_sds(p.shape[1:], p.dtype) for p in parts],
        in_specs=_hbm_specs(n), out_specs=_hbm_specs(n),
        scratch_shapes=[pltpu.SemaphoreType.DMA((n,)), pltpu.SemaphoreType.DMA((n,))],
    )(*parts)


def _scatter_chips(parts, name):
    n = len(parts)

    def body(*refs):
        start, finish = _scatter_steps(refs[:n], refs[n:2 * n], *refs[2 * n:])
        start()
        finish()

    return pl.pallas_call(
        body, name=name,
        out_shape=[_sds(p.shape, p.dtype) for p in parts], in_specs=_hbm_specs(n), out_specs=_hbm_specs(n),
        scratch_shapes=_scatter_scratch(n),
    )(*parts)


def _scatter_scratch(n):
    return [pltpu.SemaphoreType.DMA((3 * n,)), pltpu.SemaphoreType.DMA((3 * n,))]


def _scatter_steps(ins, outs, send_sems, recv_sems):
    n = len(ins)
    x, y, c = _position()
    chips = _other_chips(x, y)
    me = 2 * x + y

    def copy(a, j, src_slot, dst_slot):
        px, py = chips[j]
        return pltpu.make_async_remote_copy(
            src_ref=ins[a].at[src_slot], dst_ref=outs[a].at[dst_slot], send_sem=send_sems.at[3 * a + j],
            recv_sem=recv_sems.at[3 * a + j], device_id=(px, py, c), device_id_type=MESH)

    def start():
        for a in range(n):
            for j in range(3):
                copy(a, j, 2 * chips[j][0] + chips[j][1], me).start()

    def finish():
        for a in range(n):
            for j, (px, py) in enumerate(chips):
                copy(a, j, me, 2 * px + py).wait_recv()
        for a in range(n):
            for j in range(3):
                copy(a, j, 2 * chips[j][0] + chips[j][1], me).wait_send()

    return start, finish


def _swap_row_halves(arrays, name):
    n = len(arrays)

    def body(*refs):
        outs = refs[n:2 * n]
        send_sems, recv_sems = refs[2 * n:]
        x, y, c = _position()

        def copy(a, half):
            h = outs[a].shape[0] // 2
            rows = outs[a].at[pl.ds(pl.multiple_of(half * h, SUBLANES), h)]
            return pltpu.make_async_remote_copy(
                src_ref=rows, dst_ref=rows, send_sem=send_sems.at[a], recv_sem=recv_sems.at[a],
                device_id=(x, y, 1 - c), device_id_type=MESH)

        sends = [copy(a, c) for a in range(n)]
        for cp in sends:
            cp.start()
        for a in range(n):
            copy(a, 1 - c).wait_recv()
        for cp in sends:
            cp.wait_send()

    return pl.pallas_call(
        body, name=name,
        out_shape=[_sds(a.shape, a.dtype) for a in arrays],
        in_specs=_hbm_specs(n), out_specs=_hbm_specs(n),
        input_output_aliases={a: a for a in range(n)},
        scratch_shapes=[pltpu.SemaphoreType.DMA((n,)), pltpu.SemaphoreType.DMA((n,))],
    )(*arrays)


def _row_tile(rows):
    for t in (512, 256, 128, 64, 32, 16, 8):
        if rows % t == 0:
            return t
    return rows


def _sum_slots(buf, name):
    k, rows, cols = buf.shape
    tm = _row_tile(rows)

    def body(b_ref, o_ref):
        s = b_ref[0]
        for i in range(1, k):
            s = s + b_ref[i]
        o_ref[...] = s

    return pl.pallas_call(
        body, name=name, grid=(rows // tm,),
        in_specs=[pl.BlockSpec((k, tm, cols), lambda i: (0, i, 0))],
        out_specs=pl.BlockSpec((tm, cols), lambda i: (i, 0)),
        out_shape=_sds((rows, cols)),
        compiler_params=_params(("arbitrary",)),
    )(buf)


def _pair_add(part, from_sibling, core, name):
    _, k, h, cols = part.shape
    rows = k * h
    tm = _row_tile(rows)

    def body(core_ref, a_ref, b_ref, o_ref):
        o_ref[...] = (a_ref[0].astype(F32) + b_ref[...].astype(F32)).astype(BF16)

    out = pl.pallas_call(
        body, name=name,
        grid_spec=pltpu.PrefetchScalarGridSpec(
            num_scalar_prefetch=1, grid=(rows // tm,),
            in_specs=[pl.BlockSpec((1, tm, cols), lambda i, cr: (cr[0], i, 0)),
                      pl.BlockSpec((tm, cols), lambda i, cr: (i, 0))],
            out_specs=pl.BlockSpec((tm, cols), lambda i, cr: (i, 0))),
        out_shape=_sds((rows, cols), BF16),
        compiler_params=_params(("arbitrary",)),
    )(core, part.reshape(2, rows, cols), from_sibling.reshape(rows, cols))
    return out.reshape(k, h, cols)


def _chip_sum(arrived, own, place, name):
    _, h, cols = arrived.shape
    tm = min(256, h)
    nb = h // tm

    def body(place_ref, arr_ref, own_ref, g_ref):
        for chip in range(N_CHIPS):
            @pl.when(place_ref[1] == chip)
            def _():
                terms = [own_ref[0] if j == chip else arr_ref[j] for j in range(N_CHIPS)]
                g = terms[0].astype(F32)
                for t in terms[1:]:
                    g = g + t.astype(F32)
                g_ref[...] = g

    return pl.pallas_call(
        body, name=name,
        grid_spec=pltpu.PrefetchScalarGridSpec(
            num_scalar_prefetch=1, grid=(nb,),
            in_specs=[pl.BlockSpec((N_CHIPS, tm, cols), lambda i, pr: (0, i, 0)),
                      pl.BlockSpec((1, tm, cols), lambda i, pr: (pr[1], i, 0))],
            out_specs=pl.BlockSpec((tm, cols), lambda i, pr: (pr[0] * nb + i, 0))),
        out_shape=_sds((2 * h, cols)),
        compiler_params=_params(("arbitrary",)),
    )(place, arrived, own)


def _adam_layer(g, w, m, v, outs, layer, name):
    rows, cols = g.shape
    tm = _row_tile(rows)

    def body(g_ref, w_ref, m_ref, v_ref, *refs):
        og_ref, od_ref, om_ref, ov_ref = refs[4:]
        gr = g_ref[...]
        og_ref[0] = gr
        d, nm, nv = _adam_math(w_ref[0], gr, m_ref[0], v_ref[0])
        od_ref[0] = d
        om_ref[0] = nm
        ov_ref[0] = nv

    slab = pl.BlockSpec((1, tm, cols), lambda i: (layer, i, 0))
    return pl.pallas_call(
        body, name=name, grid=(rows // tm,),
        in_specs=[pl.BlockSpec((tm, cols), lambda i: (i, 0)), slab, slab, slab] + _hbm_specs(4),
        out_specs=[slab] * 4, out_shape=[_sds(o.shape) for o in outs],
        input_output_aliases={4 + i: i for i in range(4)},
        compiler_params=_params(("arbitrary",)),
    )(g, w, m, v, *outs)


def _adam_math(w, g, m, v):
    m = ADAM_B1 * m + (1.0 - ADAM_B1) * g
    v = ADAM_B2 * v + (1.0 - ADAM_B2) * jnp.square(g)
    m_hat = m / (1.0 - ADAM_B1 ** ADAM_STEP)
    v_hat = v / (1.0 - ADAM_B2 ** ADAM_STEP)
    delta = -ADAM_LR * (m_hat / (jnp.sqrt(v_hat) + ADAM_EPS) + ADAM_WD * w)
    return delta, m, v


def _adam(w, g, m, v, name):
    rows, cols = w.shape
    tm = _row_tile(rows)

    def body(w_ref, g_ref, m_ref, v_ref, d_ref, nm_ref, nv_ref):
        d, nm, nv = _adam_math(w_ref[...], g_ref[...], m_ref[...], v_ref[...])
        d_ref[...] = d
        nm_ref[...] = nm
        nv_ref[...] = nv

    spec = pl.BlockSpec((tm, cols), lambda i: (i, 0))
    return pl.pallas_call(
        body, name=name, grid=(rows // tm,), in_specs=[spec] * 4, out_specs=[spec] * 3,
        out_shape=[_sds((rows, cols))] * 3, compiler_params=_params(("arbitrary",)),
    )(w, g, m, v)


def _mod_fwd(c_all, w_mod, b_mod_cols):
    L, _, n = w_mod.shape

    def body(c_ref, w_ref, b_ref, o_ref):
        o_ref[0] = _hdot(_silu(c_ref[...]), w_ref[0]) + b_ref[0]

    return pl.pallas_call(
        body, name="mod_fwd", grid=(L,),
        in_specs=[_const((N_DEV, D_MODEL)), pl.BlockSpec((1, D_MODEL, n), lambda l: (l, 0, 0)),
                  pl.BlockSpec((1, 1, n), lambda l: (l, 0, 0))],
        out_specs=pl.BlockSpec((1, N_DEV, n), lambda l: (l, 0, 0)),
        out_shape=_sds((L, N_DEV, n)),
        compiler_params=_params(("arbitrary",)),
    )(c_all, w_mod, b_mod_cols)


def _mod_update(c_all, dmod, w, m, v):
    L, _, n = w.shape
    tn = 512

    def body(c_ref, d_ref, w_ref, m_ref, v_ref, g_ref, dl_ref, nm_ref, nv_ref):
        g = _hdot_tn(_silu(c_ref[...]), d_ref[0])
        g_ref[0] = g
        d, nm, nv = _adam_math(w_ref[0], g, m_ref[0], v_ref[0])
        dl_ref[0] = d
        nm_ref[0] = nm
        nv_ref[0] = nv

    big = pl.BlockSpec((1, D_MODEL, tn), lambda l, j: (l, 0, j))
    return pl.pallas_call(
        body, name="mod_update", grid=(L, n // tn),
        in_specs=[_const((N_DEV, D_MODEL)), pl.BlockSpec((1, N_DEV, tn), lambda l, j: (l, 0, j)), big, big, big],
        out_specs=[big] * 4, out_shape=[_sds(w.shape)] * 4,
        compiler_params=_params(("arbitrary", "arbitrary")),
    )(c_all, dmod, w, m, v)


def _pack_rows(parts, row_multiple):
    flat = jnp.concatenate([p.reshape(-1) for p in parts])
    unit = row_multiple * LANES
    flat = jnp.pad(flat, (0, (-flat.shape[0]) % unit))
    return flat.reshape(-1, LANES)


def _unpack(packed, shapes):
    flat = packed.reshape(-1)
    out, off = [], 0
    for s in shapes:
        n = 1
        for d in s:
            n *= d
        out.append(flat[off:off + n].reshape(s))
        off += n
    return out


def _lane_pad(a):
    return jnp.pad(a, ((0, 0), (0, LANES - a.shape[1])))


WEIGHT_NAMES = ("norm_mix_w", "norm_mlp_w", "w_mod", "b_mod", "w_in", "lru_conv_w", "lru_conv_b", "lru_gate_a_w",
                "lru_gate_a_b", "lru_gate_x_w", "lru_gate_x_b", "lru_lambda", "lru_norm_w", "gdn_conv_w", "gdn_a_log",
                "gdn_dt_bias", "gdn_norm_w", "w_out", "w_up", "w_down", "final_norm_w")


def kernel(x, c, norm_mix_w, norm_mlp_w, w_mod, b_mod, w_in, lru_conv_w, lru_conv_b, lru_gate_a_w, lru_gate_a_b, lru_gate_x_w, lru_gate_x_b, lru_lambda, lru_norm_w, gdn_conv_w, gdn_a_log, gdn_dt_bias, gdn_norm_w, w_out, w_up, w_down, final_norm_w, loss_target, m_norm_mix_w, m_norm_mlp_w, m_w_mod, m_b_mod, m_w_in, m_lru_conv_w, m_lru_conv_b, m_lru_gate_a_w, m_lru_gate_a_b, m_lru_gate_x_w, m_lru_gate_x_b, m_lru_lambda, m_lru_norm_w, m_gdn_conv_w, m_gdn_a_log, m_gdn_dt_bias, m_gdn_norm_w, m_w_out, m_w_up, m_w_down, m_final_norm_w, v_norm_mix_w, v_norm_mlp_w, v_w_mod, v_b_mod, v_w_in, v_lru_conv_w, v_lru_conv_b, v_lru_gate_a_w, v_lru_gate_a_b, v_lru_gate_x_w, v_lru_gate_x_b, v_lru_lambda, v_lru_norm_w, v_gdn_conv_w, v_gdn_a_log, v_gdn_dt_bias, v_gdn_norm_w, v_w_out, v_w_up, v_w_down, v_final_norm_w):
    W = dict(zip(WEIGHT_NAMES, (norm_mix_w, norm_mlp_w, w_mod, b_mod, w_in, lru_conv_w, lru_conv_b, lru_gate_a_w,
                                lru_gate_a_b, lru_gate_x_w, lru_gate_x_b, lru_lambda, lru_norm_w, gdn_conv_w, gdn_a_log,
                                gdn_dt_bias, gdn_norm_w, w_out, w_up, w_down, final_norm_w)))
    M = dict(zip(WEIGHT_NAMES, (m_norm_mix_w, m_norm_mlp_w, m_w_mod, m_b_mod, m_w_in, m_lru_conv_w, m_lru_conv_b,
                                m_lru_gate_a_w, m_lru_gate_a_b, m_lru_gate_x_w, m_lru_gate_x_b, m_lru_lambda,
                                m_lru_norm_w, m_gdn_conv_w, m_gdn_a_log, m_gdn_dt_bias, m_gdn_norm_w, m_w_out, m_w_up,
                                m_w_down, m_final_norm_w)))
    V = dict(zip(WEIGHT_NAMES, (v_norm_mix_w, v_norm_mlp_w, v_w_mod, v_b_mod, v_w_in, v_lru_conv_w, v_lru_conv_b,
                                v_lru_gate_a_w, v_lru_gate_a_b, v_lru_gate_x_w, v_lru_gate_x_b, v_lru_lambda,
                                v_lru_norm_w, v_gdn_conv_w, v_gdn_a_log, v_gdn_dt_bias, v_gdn_norm_w, v_w_out, v_w_up,
                                v_w_down, v_final_norm_w)))
    L = DEPTH
    xi, yi, ci = _position()
    chip = 2 * xi + yi
    dev = 2 * chip + ci
    lcs = LRU_W // N_CHIPS
    gcs = 3 * GDN_W // N_CHIPS
    mcs = N_MOD * D_MODEL // N_CHIPS

    g_in = _all_gather_rows(_pack_rows([c, lru_conv_w, gdn_conv_w], SUBLANES), "gather_small_inputs").reshape(N_DEV, -1)
    c_all = g_in[:, :D_MODEL]
    per_chip = g_in[0::2]
    o1 = D_MODEL + L * 4 * lcs
    lcw_full = per_chip[:, D_MODEL:o1].reshape(N_CHIPS, L, 4, lcs).transpose(1, 2, 0, 3).reshape(L, 4, LRU_W)
    gcw_full = per_chip[:, o1:o1 + L * 4 * gcs].reshape(N_CHIPS, L, 4, gcs).transpose(1, 2, 0, 3).reshape(L, 4, 3 * GDN_W)

    b_cols = lax.dynamic_slice(b_mod, (0, chip * mcs), (L, mcs)).reshape(L, 1, mcs)
    modp = _mod_fwd(c_all, w_mod, b_cols)
    g_mod = _all_gather_rows(modp.reshape(L * N_DEV, mcs), "gather_mod").reshape(N_DEV, L, N_DEV, mcs)
    mod = lax.dynamic_index_in_dim(g_mod[0::2], dev, axis=2, keepdims=False).transpose(1, 0, 2).reshape(L, N_MOD * D_MODEL)

    stacked = _prep_layers(norm_mix_w, norm_mlp_w, mod, None, lcw_full, lru_conv_b, lru_gate_a_w, lru_gate_a_b,
                           lru_gate_x_w, lru_gate_x_b, lru_lambda, lru_norm_w, gcw_full, gdn_a_log, gdn_dt_bias,
                           gdn_norm_w, None, None, None)
    shards = [[w_in[l].astype(BF16), w_out[l].astype(BF16), w_up[l].astype(BF16), w_down[l].astype(BF16)]
              for l in range(L)]

    def layer_params(l, gathered):
        win_g, wout_g, wup_g, wdown_g = (
            lax.dynamic_update_slice(got, own.reshape((1,) + got.shape[1:]), (chip, 0, 0, 0)).reshape(
                (N_CHIPS,) + own.shape) for got, own in zip(gathered, shards[l]))
        p = {k: v[l] for k, v in stacked.items() if v is not None}
        p["win"] = jnp.pad(win_g.transpose(1, 0, 2).reshape(D_MODEL, IN_COLS), ((0, 0), (0, IN_PAD - IN_COLS)))
        p["wout"] = wout_g.reshape(D_MODEL, D_MODEL)
        p["wup"] = wup_g.transpose(1, 0, 2).reshape(D_MODEL, D_FF)
        p["wdown"] = wdown_g.reshape(D_FF, D_MODEL)
        return p

    layers = [layer_params(0, _gather_chips(shards[0], "gather_weights"))]
    xs = x[0]
    res = []
    for l in range(L):
        xs, r, gathered = _layer_fwd(xs, layers[l], shards[l + 1] if l + 1 < L else ())
        res.append(r)
        if l + 1 < L:
            layers.append(layer_params(l + 1, gathered))
    dx, loss_blk, dfnw = _loss_head(xs, loss_target[0], final_norm_w.reshape(1, D_MODEL))
    loss_local = loss_blk[0, 0]

    big_names = ["w_in", "w_out", "w_up", "w_down"]
    core = jnp.reshape(ci, (1,)).astype(jnp.int32)
    place = jnp.stack([ci, chip]).astype(jnp.int32)
    layer_grads = [None] * L

    big = {nm: [lax.empty(W[nm].shape, F32) for _ in range(4)] for nm in big_names}

    def apply_update(l, arrived, pair):
        halves = [_chip_sum(a, own, place, "chip_sum_" + nm) for nm, a, own in zip(big_names, arrived, pair)]
        full = _swap_row_halves(halves, "pair_swap")
        for nm, gr in zip(big_names, full):
            big[nm] = _adam_layer(gr, W[nm], M[nm], V[nm], big[nm], l, "adam_" + nm)

    pair = ()
    for l in reversed(range(L)):
        dx, gl, arrived = _layer_bwd(dx, layers[l], res[l], sharded=True, scatter=pair)
        if pair:
            apply_update(l + 1, arrived, pair)
        layer_grads[l] = gl
        gwin = gl["win"][:, :IN_COLS].reshape(2, D_MODEL // 2, N_CHIPS, IN_COLS // N_CHIPS).transpose(0, 2, 1, 3)
        parts = [gwin, gl["wout"], gl["wup"], gl["wdown"]]
        from_sibling = _send_to_sibling(parts, "pair_send")
        pair = [_pair_add(p, r, core, "pair_add_" + nm) for nm, p, r in zip(big_names, parts, from_sibling)]
    apply_update(0, _scatter_chips(pair, "chip_scatter"), pair)
    small_keys = [k for k in layer_grads[0] if k not in ("win", "wout", "wup", "wdown")]
    g = {k: jnp.stack([gl[k] for gl in layer_grads]) for k in small_keys}
    loss = lax.psum(loss_local, ("x", "y", "c"))

    dmod = jnp.concatenate([g["sh1"], g["sc1"], g["g1"], g["sh2"], g["sc2"], g["g2"]], axis=-1)
    small = [dmod, g["nmw"], g["nmlp"], g["lcw"][:, :4], g["lcb"], jax.vmap(_diag_blocks)(g["wa"]), g["ba"],
             jax.vmap(_diag_blocks)(g["wx"]), g["bx"], g["lam"], g["lnw"], g["gcw"][:, :4], g["alog"], g["dtb"],
             g["gnw"], dfnw]
    small_shapes = [(L, N_MOD * D_MODEL), (L, D_MODEL), (L, D_MODEL), (L, 4, LRU_W), (L, LRU_W),
                    (L, LRU_BLOCKS, LRU_BLOCK, LRU_BLOCK), (L, LRU_W), (L, LRU_BLOCKS, LRU_BLOCK, LRU_BLOCK),
                    (L, LRU_W), (L, LRU_W), (L, LRU_W), (L, 4, 3 * GDN_W), (L, LANES), (L, LANES), (L, LANES),
                    (D_MODEL,)]
    small_names = ["b_mod", "norm_mix_w", "norm_mlp_w", None, "lru_conv_b", "lru_gate_a_w", "lru_gate_a_b",
                   "lru_gate_x_w", "lru_gate_x_b", "lru_lambda", "lru_norm_w", None, "gdn_a_log", "gdn_dt_bias",
                   "gdn_norm_w", "final_norm_w"]
    pack_g = _pack_rows(small, 512)
    rows = pack_g.shape[0]
    all_g = _all_gather_rows(pack_g, "gather_small_grads").reshape(N_DEV, rows, LANES)
    tot = _sum_slots(all_g, "sum_small_grads")
    tot_parts = _unpack(tot, small_shapes)

    def pack_state(S_):
        parts = []
        for nm, shp in zip(small_names, small_shapes):
            if nm is None:
                parts.append(jnp.zeros(shp, F32))
            elif nm in ("gdn_a_log", "gdn_dt_bias"):
                parts.append(_lane_pad(S_[nm]))
            else:
                parts.append(S_[nm])
        return _pack_rows(parts, 512)

    upd = _adam(pack_state(W), tot, pack_state(M), pack_state(V), "adam_small")
    upd_parts = [_unpack(u, small_shapes) for u in upd]

    grads, deltas, new_m, new_v = {}, {}, {}, {}
    for k, nm in enumerate(small_names):
        if nm is None:
            continue
        cut = (lambda a: a[:, :HEADS]) if nm in ("gdn_a_log", "gdn_dt_bias") else (lambda a: a)
        grads[nm] = cut(tot_parts[k])
        deltas[nm], new_m[nm], new_v[nm] = (cut(u[k]) for u in upd_parts)

    g_lcw = lax.dynamic_slice(tot_parts[3], (0, 0, chip * lcs), (L, 4, lcs))
    g_gcw = lax.dynamic_slice(tot_parts[11], (0, 0, chip * gcs), (L, 4, gcs))
    conv_shapes = [(L, 4, lcs), (L, 4, gcs)]
    conv_pack = lambda a, b: _pack_rows([a, b], SUBLANES)
    cu = _adam(conv_pack(lru_conv_w, gdn_conv_w), conv_pack(g_lcw, g_gcw), conv_pack(m_lru_conv_w, m_gdn_conv_w),
               conv_pack(v_lru_conv_w, v_gdn_conv_w), "adam_conv")
    cu_parts = [_unpack(u, conv_shapes) for u in cu]
    for k, nm in enumerate(("lru_conv_w", "gdn_conv_w")):
        grads[nm] = (g_lcw, g_gcw)[k]
        deltas[nm], new_m[nm], new_v[nm] = (u[k] for u in cu_parts)

    dmod_all = all_g[:, :L * N_MOD * D_MODEL // LANES].reshape(N_DEV, L, N_MOD * D_MODEL)
    dmod_cols = lax.dynamic_slice(dmod_all, (0, 0, chip * mcs), (N_DEV, L, mcs)).transpose(1, 0, 2)
    grads["w_mod"], deltas["w_mod"], new_m["w_mod"], new_v["w_mod"] = _mod_update(c_all, dmod_cols, w_mod, m_w_mod, v_w_mod)

    for nm in big_names:
        grads[nm], deltas[nm], new_m[nm], new_v[nm] = big[nm]

    out = [loss, dx[None]]
    for group in (grads, deltas, new_m, new_v):
        out += [group[nm].reshape(W[nm].shape) for nm in WEIGHT_NAMES]
    return tuple(out)
```

```python
import functools

import jax
import jax.numpy as jnp
from jax import lax
from jax.experimental import pallas as pl
from jax.experimental.pallas import tpu as pltpu

F32 = jnp.float32
BF16 = jnp.bfloat16
MESH = pl.DeviceIdType.MESH

D_MODEL = 1024
DEPTH = 4
LRU_W = 512
LRU_BLOCKS = 8
LRU_BLOCK = 64
LRU_C = 8.0
HEADS = 4
HEAD_DIM = 128
GDN_W = 512
CHUNK = 128
D_FF = 4096
N_MOD = 6
IN_COLS = 3080
IN_PAD = 3200
NORM_EPS = 1e-6
LANES = 128
SUBLANES = 8
N_DEV = 8
N_CHIPS = 4

ADAM_LR = 0.001
ADAM_B1 = 0.9
ADAM_B2 = 0.999
ADAM_EPS = 1e-08
ADAM_WD = 0.01
ADAM_STEP = 10

VMEM_LIMIT = 56 * 1024 * 1024
HI = lax.Precision.HIGHEST


def _sds(shape, dtype=F32):
    return jax.ShapeDtypeStruct(tuple(shape), dtype)


def _params(sem=None, vmem=VMEM_LIMIT):
    return pltpu.CompilerParams(dimension_semantics=sem, vmem_limit_bytes=vmem)


def _const(shape):
    return pl.BlockSpec(tuple(shape), lambda *_: (0,) * len(shape))


def _row(tm, c, col=0):
    return pl.BlockSpec((tm, c), lambda i: (i, col))


def _dot(a, b):
    return jnp.dot(a, b, preferred_element_type=F32)


def _dot_nt(a, b):
    return lax.dot_general(a, b, (((1,), (1,)), ((), ())), preferred_element_type=F32)


def _dot_tn(a, b):
    return lax.dot_general(a, b, (((0,), (0,)), ((), ())), preferred_element_type=F32)


def _hdot(a, b):
    return jnp.dot(a, b, preferred_element_type=F32, precision=HI)


def _hdot_nt(a, b):
    return lax.dot_general(a, b, (((1,), (1,)), ((), ())), preferred_element_type=F32, precision=HI)


def _hdot_tn(a, b):
    return lax.dot_general(a, b, (((0,), (0,)), ((), ())), preferred_element_type=F32, precision=HI)


_DIMS = {"nn": (((1,), (0,)), ((), ())), "nt": (((1,), (1,)), ((), ())), "tn": (((0,), (0,)), ((), ()))}


def _mm_raw(a, b, dims, passes):
    dn = _DIMS[dims]

    def dot(p, q):
        return lax.dot_general(p, q, dn, preferred_element_type=F32)

    a_hi = a.astype(BF16)
    b_hi = b.astype(BF16)
    if passes == 1:
        return dot(a_hi, b_hi)
    a_lo = (a - a_hi.astype(F32)).astype(BF16)
    b_lo = (b - b_hi.astype(F32)).astype(BF16)
    return dot(a_hi, b_hi) + (dot(a_hi, b_lo) + dot(a_lo, b_hi))


@functools.partial(jax.custom_vjp, nondiff_argnums=(2, 3))
def _mm(a, b, dims, passes):
    return _mm_raw(a, b, dims, passes)


def _mm_fwd(a, b, dims, passes):
    return _mm_raw(a, b, dims, passes), (a, b)


def _mm_bwd(dims, passes, res, ct):
    a, b = res
    if dims == "nn":
        return _mm_raw(ct, b, "nt", passes), _mm_raw(a, ct, "tn", passes)
    if dims == "nt":
        return _mm_raw(ct, b, "nn", passes), _mm_raw(ct, a, "tn", passes)
    return _mm_raw(b, ct, "nt", passes), _mm_raw(a, ct, "nn", passes)


_mm.defvjp(_mm_fwd, _mm_bwd)


def _acc(ref, val, first):
    @pl.when(first)
    def _():
        ref[...] = val

    @pl.when(jnp.logical_not(first))
    def _():
        ref[...] += val


def _colsum(v):
    return jnp.sum(v, axis=0, keepdims=True)


def _rms_parts(x):
    r = lax.rsqrt(jnp.mean(x * x, axis=-1, keepdims=True) + NORM_EPS)
    return x * r, r


def _rms_bwd(dy, xh, r, w):
    dxh = dy * w
    dw = _colsum(dy * xh)
    dx = r * (dxh - xh * jnp.mean(dxh * xh, axis=-1, keepdims=True))
    return dx, dw


def _norm_mod(x, w, sc, sh):
    xh, _ = _rms_parts(x)
    return (xh * w) * (1.0 + sc) + sh


def _norm_mod_bwd(dy, x, w, sc):
    xh, r = _rms_parts(x)
    n = xh * w
    dsh = _colsum(dy)
    dsc = _colsum(dy * n)
    dx, dw = _rms_bwd(dy * (1.0 + sc), xh, r, w)
    return dx, dw, dsc, dsh


def _softplus(x):
    return jnp.maximum(x, 0.0) + jnp.log1p(jnp.exp(-jnp.abs(x)))


def _silu(x):
    return x * jax.nn.sigmoid(x)


def _silu_grad(x):
    s = jax.nn.sigmoid(x)
    return s * (1.0 + x * (1.0 - s))


def _roll_dn(x, d):
    return x if d == 0 else pltpu.roll(x, d, 0)


def _roll_up(x, d):
    return x if d == 0 else pltpu.roll(x, x.shape[0] - d, 0)


def _proj_fwd(x, nw, sc, sh, win):
    S = x.shape[0]
    tm = min(512, S)

    def body(x_ref, nw_ref, sc_ref, sh_ref, w_ref, proj_ref, hb_ref):
        hb = _norm_mod(x_ref[...], nw_ref[...], sc_ref[...], sh_ref[...]).astype(BF16)
        hb_ref[...] = hb
        proj_ref[...] = _dot(hb, w_ref[...])

    vec = _const((1, D_MODEL))
    return pl.pallas_call(
        body, name="proj_fwd", grid=(S // tm,),
        in_specs=[_row(tm, D_MODEL), vec, vec, vec, _const((D_MODEL, IN_PAD))],
        out_specs=[_row(tm, IN_PAD), _row(tm, D_MODEL)],
        out_shape=[_sds((S, IN_PAD)), _sds((S, D_MODEL), BF16)],
        compiler_params=_params(("arbitrary",)),
    )(x, nw, sc, sh, win)


def _proj_bwd(dx1, x, dlx, dly, dqkv, dz, dba, nw, sc, win):
    S = x.shape[0]
    tm = min(512, S)

    def body(dx1_ref, x_ref, dlx_ref, dly_ref, dqkv_ref, dz_ref, dba_ref, nw_ref, sc_ref, w_ref,
             dx_ref, dpb_ref, dnw_ref, dsc_ref, dsh_ref):
        i = pl.program_id(0)
        dpb = jnp.concatenate([dlx_ref[...], dly_ref[...], dqkv_ref[...], dz_ref[...], dba_ref[...]],
                              axis=-1).astype(BF16)
        dpb_ref[...] = dpb
        dh = _dot_nt(dpb, w_ref[...])
        dx, dnw, dsc, dsh = _norm_mod_bwd(dh, x_ref[...], nw_ref[...], sc_ref[...])
        dx_ref[...] = dx1_ref[...] + dx
        _acc(dnw_ref, dnw, i == 0)
        _acc(dsc_ref, dsc, i == 0)
        _acc(dsh_ref, dsh, i == 0)

    vec = _const((1, D_MODEL))
    return pl.pallas_call(
        body, name="proj_bwd", grid=(S // tm,),
        in_specs=[_row(tm, D_MODEL), _row(tm, D_MODEL), _row(tm, LRU_W), _row(tm, LRU_W), _row(tm, 3 * GDN_W),
                  _row(tm, GDN_W), _row(tm, LANES), vec, vec,
                  _const((D_MODEL, IN_PAD))],
        out_specs=[_row(tm, D_MODEL), _row(tm, IN_PAD), vec, vec, vec],
        out_shape=[_sds((S, D_MODEL)), _sds((S, IN_PAD), BF16), _sds((1, D_MODEL)), _sds((1, D_MODEL)),
                   _sds((1, D_MODEL))],
        compiler_params=_params(("arbitrary",)),
    )(dx1, x, dlx, dly, dqkv, dz, dba, nw, sc, win)


def _conv_taps(xx, w, tm):
    y = _roll_dn(xx, 3)[SUBLANES:] * w[0:1]
    y = y + _roll_dn(xx, 2)[SUBLANES:] * w[1:2]
    y = y + _roll_dn(xx, 1)[SUBLANES:] * w[2:3]
    y = y + xx[SUBLANES:] * w[3:4]
    return y


def _conv_fwd(src, col0, C, w8, b, act, name):
    S = src.shape[0]
    tm = min(512, S)
    tc = 512
    hb = tm // SUBLANES
    cb0 = col0 // tc

    def body(x_ref, p_ref, w_ref, b_ref, y_ref):
        i = pl.program_id(0)
        prev = jnp.where(i > 0, p_ref[...], 0.0)
        xx = jnp.concatenate([prev, x_ref[...]], axis=0)
        y = _conv_taps(xx, w_ref[...], tm) + b_ref[...]
        y_ref[...] = _silu(y) if act else y

    return pl.pallas_call(
        body, name=name, grid=(S // tm, C // tc),
        in_specs=[pl.BlockSpec((tm, tc), lambda i, j: (i, cb0 + j)),
                  pl.BlockSpec((SUBLANES, tc), lambda i, j: (jnp.maximum(i * hb - 1, 0), cb0 + j)),
                  pl.BlockSpec((SUBLANES, tc), lambda i, j: (0, j)),
                  pl.BlockSpec((1, tc), lambda i, j: (0, j))],
        out_specs=pl.BlockSpec((tm, tc), lambda i, j: (i, j)),
        out_shape=_sds((S, C)),
        compiler_params=_params(("arbitrary", "arbitrary")),
    )(src, src, w8, b)


def _conv_bwd(src, col0, C, w8, b, dyact, act, name):
    S = src.shape[0]
    tm = min(512, S)
    tc = 512
    hb = tm // SUBLANES
    nt = S // tm
    cb0 = col0 // tc
    last_hb = S // SUBLANES - 1

    def body(x_ref, p_ref, n_ref, dy_ref, dyn_ref, w_ref, b_ref, dx_ref, dw_ref, db_ref):
        i = pl.program_id(1)
        w = w_ref[...]
        prev = jnp.where(i > 0, p_ref[...], 0.0)
        xx = jnp.concatenate([prev, x_ref[...], n_ref[...]], axis=0)
        dy = jnp.concatenate([dy_ref[...], jnp.where(i < nt - 1, dyn_ref[...], 0.0)], axis=0)
        if act:
            ypre = _conv_taps(xx, w, tm + SUBLANES) + b_ref[...]
            dy = dy * _silu_grad(ypre)
        dx = dy[:tm] * w[3:4]
        for d in (1, 2, 3):
            dx = dx + _roll_up(dy, d)[:tm] * w[3 - d:4 - d]
        dx_ref[...] = dx
        xt = xx[:tm + SUBLANES]
        dyt = dy[:tm]
        rows = [_colsum(dyt * _roll_dn(xt, 3 - k)[SUBLANES:]) for k in range(4)]
        dw = jnp.concatenate(rows + [jnp.zeros((SUBLANES - 4, tc), F32)], axis=0)
        _acc(dw_ref, dw, i == 0)
        _acc(db_ref, _colsum(dyt), i == 0)

    return pl.pallas_call(
        body, name=name, grid=(C // tc, nt),
        in_specs=[pl.BlockSpec((tm, tc), lambda j, i: (i, cb0 + j)),
                  pl.BlockSpec((SUBLANES, tc), lambda j, i: (jnp.maximum(i * hb - 1, 0), cb0 + j)),
                  pl.BlockSpec((SUBLANES, tc), lambda j, i: (jnp.minimum((i + 1) * hb, last_hb), cb0 + j)),
                  pl.BlockSpec((tm, tc), lambda j, i: (i, j)),
                  pl.BlockSpec((SUBLANES, tc), lambda j, i: (jnp.minimum((i + 1) * hb, last_hb), j)),
                  pl.BlockSpec((SUBLANES, tc), lambda j, i: (0, j)),
                  pl.BlockSpec((1, tc), lambda j, i: (0, j))],
        out_specs=[pl.BlockSpec((tm, tc), lambda j, i: (i, j)),
                   pl.BlockSpec((SUBLANES, tc), lambda j, i: (0, j)),
                   pl.BlockSpec((1, tc), lambda j, i: (0, j))],
        out_shape=[_sds((S, C)), _sds((SUBLANES, C)), _sds((1, C))],
        compiler_params=_params(("arbitrary", "arbitrary")),
    )(src, src, src, dyact, dyact, w8, b)


def _lru_ab(pre_a, pre_x, xr, lam):
    r = jax.nn.sigmoid(pre_a)
    g = jax.nn.sigmoid(pre_x)
    log_sig = -_softplus(-lam)
    log_a = LRU_C * r * log_sig
    a = jnp.exp(log_a)
    t = jnp.tanh(log_a)
    mult = jnp.sqrt(jnp.maximum(-2.0 * t / (1.0 - t), 1e-12))
    return a, mult * (g * xr)


def _lru_tail(h, ly, lnw):
    xh, _ = _rms_parts(h * jax.nn.gelu(ly))
    return xh * lnw


def _scan_down(a, b):
    n = a.shape[0]
    row = lax.broadcasted_iota(jnp.int32, a.shape, 0)
    d = 1
    while d < n:
        keep = row >= d
        a_s = jnp.where(keep, _roll_dn(a, d), 1.0)
        b_s = jnp.where(keep, _roll_dn(b, d), 0.0)
        b = a * b_s + b
        a = a * a_s
        d *= 2
    return a, b


def _scan_up(a, b):
    n = a.shape[0]
    row = lax.broadcasted_iota(jnp.int32, a.shape, 0)
    d = 1
    while d < n:
        keep = row < n - d
        a_s = jnp.where(keep, _roll_up(a, d), 1.0)
        b_s = jnp.where(keep, _roll_up(b, d), 0.0)
        b = a * b_s + b
        a = a * a_s
        d *= 2
    return a, b


LRU_TM = 256


def _lru_fwd(xr, proj, wa, ba, wx, bx, lam, lnw):
    S = xr.shape[0]
    tm = min(LRU_TM, S)

    def body(xr_ref, ly_ref, wa_ref, ba_ref, wx_ref, bx_ref, lam_ref, lnw_ref, out_ref, h_ref, carry):
        i = pl.program_id(0)

        @pl.when(i == 0)
        def _():
            carry[...] = jnp.zeros_like(carry)

        x = xr_ref[...]
        xb = x.astype(BF16)
        pre_a = _dot(xb, wa_ref[...]) + ba_ref[...]
        pre_x = _dot(xb, wx_ref[...]) + bx_ref[...]
        a, b = _lru_ab(pre_a, pre_x, x, lam_ref[...])
        ca, hl = _scan_down(a, b)
        h = hl + ca * carry[0:1, :]
        carry[0:1, :] = h[tm - 1:tm, :]
        h_ref[...] = h
        out_ref[...] = _lru_tail(h, ly_ref[...], lnw_ref[...])

    vec = _const((1, LRU_W))
    mat = _const((LRU_W, LRU_W))
    return pl.pallas_call(
        body, name="lru_fwd", grid=(S // tm,),
        in_specs=[_row(tm, LRU_W), _row(tm, LRU_W, 1), mat, vec, mat, vec, vec, vec],
        out_specs=[_row(tm, LRU_W), _row(tm, LRU_W)],
        out_shape=[_sds((S, LRU_W)), _sds((S, LRU_W))],
        scratch_shapes=[pltpu.VMEM((SUBLANES, LRU_W), F32)],
        compiler_params=_params(("arbitrary",)),
    )(xr, proj, wa, ba, wx, bx, lam, lnw)


def _lru_bwd(dout, xr, proj, h, wa, ba, wx, bx, lam, lnw):
    S = xr.shape[0]
    tm = min(LRU_TM, S)
    nt = S // tm
    hb = tm // SUBLANES

    def rev(col=0):
        return pl.BlockSpec((tm, LRU_W), lambda i: (nt - 1 - i, col))

    def body(dout_ref, xr_ref, ly_ref, h_ref, hp_ref, wa_ref, ba_ref, wx_ref, bx_ref, lam_ref, lnw_ref,
             dxr_ref, dly_ref, dwa_ref, dba_ref, dwx_ref, dbx_ref, dlam_ref, dlnw_ref, carry):
        i = pl.program_id(0)
        first = i == 0

        @pl.when(first)
        def _():
            carry[...] = jnp.zeros_like(carry)

        x = xr_ref[...]
        xb = x.astype(BF16)
        pre_a = _dot(xb, wa_ref[...]) + ba_ref[...]
        pre_x = _dot(xb, wx_ref[...]) + bx_ref[...]
        (a, b), ab_vjp = jax.vjp(_lru_ab, pre_a, pre_x, x, lam_ref[...])
        h_t = h_ref[...]
        _, tail_vjp = jax.vjp(_lru_tail, h_t, ly_ref[...], lnw_ref[...])
        dh, dly, dlnw = tail_vjp(dout_ref[...])
        dly_ref[...] = dly
        row = lax.broadcasted_iota(jnp.int32, a.shape, 0)
        a_next = jnp.where(row == tm - 1, carry[0:1, :], _roll_up(a, 1))
        ca, gl = _scan_up(a_next, dh)
        g = gl + ca * carry[1:2, :]
        carry[0:1, :] = a[0:1, :]
        carry[1:2, :] = g[0:1, :]
        h_before = jnp.where(i == nt - 1, 0.0, hp_ref[SUBLANES - 1:SUBLANES, :])
        h_prev = jnp.where(row == 0, h_before, _roll_dn(h_t, 1))
        dpa, dpx, dx, dlam = ab_vjp((g * h_prev, g))
        dpab = dpa.astype(BF16)
        dpxb = dpx.astype(BF16)
        dxr_ref[...] = dx + _dot_nt(dpab, wa_ref[...]) + _dot_nt(dpxb, wx_ref[...])
        _acc(dwa_ref, _dot_tn(xb, dpab), first)
        _acc(dwx_ref, _dot_tn(xb, dpxb), first)
        _acc(dba_ref, _colsum(dpa), first)
        _acc(dbx_ref, _colsum(dpx), first)
        _acc(dlam_ref, dlam, first)
        _acc(dlnw_ref, dlnw, first)

    vec = _const((1, LRU_W))
    mat = _const((LRU_W, LRU_W))
    return pl.pallas_call(
        body, name="lru_bwd", grid=(nt,),
        in_specs=[rev(), rev(), rev(1), rev(),
                  pl.BlockSpec((SUBLANES, LRU_W), lambda i: (jnp.maximum((nt - 1 - i) * hb - 1, 0), 0)),
                  mat, vec, mat, vec, vec, vec],
        out_specs=[rev(), rev(), mat, vec, mat, vec, vec, vec],
        out_shape=[_sds((S, LRU_W)), _sds((S, LRU_W)), _sds((LRU_W, LRU_W)), _sds((1, LRU_W)),
                   _sds((LRU_W, LRU_W)), _sds((1, LRU_W)), _sds((1, LRU_W)), _sds((1, LRU_W))],
        scratch_shapes=[pltpu.VMEM((SUBLANES, LRU_W), F32)],
        compiler_params=_params(("arbitrary",)),
    )(dout, xr, proj, h, h, wa, ba, wx, bx, lam, lnw)


def _lane_pick(row_or_tile, lane):
    idx = lax.broadcasted_iota(jnp.int32, row_or_tile.shape, 1)
    return jnp.sum(jnp.where(idx == lane, row_or_tile, 0.0), axis=-1, keepdims=True)


def _unit_lower_inverses(los):
    n = los[0].shape[0]
    ri = lax.broadcasted_iota(jnp.int32, (n, n), 0)
    ci = lax.broadcasted_iota(jnp.int32, (n, n), 1)
    eye = (ri == ci).astype(F32)

    def lower_left_of(s):
        same_block = (ri & ~(2 * s - 1)) == (ci & ~(2 * s - 1))
        return same_block & ((ri & s) != 0) & ((ci & s) == 0)

    invs = [eye - jnp.where(lower_left_of(1), lo, 0.0) for lo in los]
    s = 2
    while s < n:
        lower_left = lower_left_of(s)
        left = [_mm_raw(inv, jnp.where(lower_left, lo, 0.0), "nn", 3) for inv, lo in zip(invs, los)]
        invs = [inv - _mm_raw(t, inv, "nn", 3) for inv, t in zip(invs, left)]
        s *= 2
    return invs


@jax.custom_vjp
def _unit_lower_inverses_diff(los):
    return _unit_lower_inverses(los)


def _unit_lower_inverses_fwd(los):
    invs = _unit_lower_inverses(los)
    return invs, invs


def _unit_lower_inverses_bwd(invs, cts):
    right = [_mm_raw(ct, inv, "nt", 3) for ct, inv in zip(cts, invs)]
    return ([-_mm_raw(inv, r, "tn", 3) for inv, r in zip(invs, right)],)


_unit_lower_inverses_diff.defvjp(_unit_lower_inverses_fwd, _unit_lower_inverses_bwd)


GDN_STEP_CHUNKS = 2


def _gdn_chunk(qs, ks, vs, bas, alog, dtb, states, inverses=_unit_lower_inverses, mm=_mm_raw):
    C = qs[0].shape[0]
    nchunks = len(bas)
    items = [(c, h) for c in range(nchunks) for h in range(HEADS)]
    ri = lax.broadcasted_iota(jnp.int32, (C, C), 0)
    ci = lax.broadcasted_iota(jnp.int32, (C, C), 1)
    causal = ri >= ci
    strict = ri > ci
    tri = causal.astype(F32)
    betas = [jax.nn.sigmoid(_lane_pick(bas[c], h)) for c, h in items]
    gs = [-jnp.exp(_lane_pick(alog, h)) * _softplus(_lane_pick(bas[c], h + HEADS) + _lane_pick(dtb, h))
          for c, h in items]
    qn = [q * lax.rsqrt(jnp.sum(q * q, axis=-1, keepdims=True) + 1e-6) * (HEAD_DIM ** -0.5) for q in qs]
    kn = [k * lax.rsqrt(jnp.sum(k * k, axis=-1, keepdims=True) + 1e-6) for k in ks]
    gc = [_hdot(tri, jnp.broadcast_to(g, (C, C))) for g in gs]
    decay = [jnp.where(causal, jnp.exp(jnp.where(causal, c - c.T, 0.0)), 0.0) for c in gc]
    eg = [jnp.exp(c) for c in gc]
    kb = [k * b for k, b in zip(kn, betas)]
    vb = [v * b for v, b in zip(vs, betas)]
    los = [jnp.where(strict, mm(a, k, "nt", 1) * d, 0.0) for a, k, d in zip(kb, kn, decay)]
    attn = [jnp.where(causal, mm(q, k, "nt", 1) * d, 0.0) for q, k, d in zip(qn, kn, decay)]
    tinv = inverses(los)
    u = [mm(t, x, "nn", 3) for t, x in zip(tinv, vb)]
    w = [mm(t, a * e, "nn", 3) for t, a, e in zip(tinv, kb, eg)]
    g_last = [c[C - 1:C, :] for c in gc]
    k_tail = [k * jnp.exp(gl - c) for k, gl, c in zip(kn, g_last, gc)]
    q_dec = [q * e for q, e in zip(qn, eg)]
    outs = []
    for c in range(nchunks):
        idx = range(c * HEADS, (c + 1) * HEADS)
        v_new = [u[i] - mm(w[i], s, "nn", 1) for i, s in zip(idx, states)]
        o_state = [mm(q_dec[i], s, "nn", 1) for i, s in zip(idx, states)]
        outs += [a + mm(attn[i], vn, "nn", 1) for i, a, vn in zip(idx, o_state, v_new)]
        states = [s * jnp.exp(g_last[i]) + mm(k_tail[i], vn, "tn", 1) for i, s, vn in zip(idx, states, v_new)]
    return outs, states


def _gdn_fwd(qkv, proj, alog, dtb, gather=()):
    S = qkv.shape[0]
    per = min(GDN_STEP_CHUNKS, S // CHUNK)
    T = per * CHUNK
    nc = S // T
    nk = len(gather)
    assert CHUNK == HEAD_DIM

    def body(*refs):
        q_ref, k_ref, v_ref, ba_ref, alog_ref, dtb_ref = refs[:6]
        o_ref, st_ref = refs[6 + nk:8 + nk]
        state = refs[8 + 2 * nk]
        if nk:
            start, finish = _gather_steps(refs[6:6 + nk], refs[8 + nk:8 + 2 * nk], *refs[9 + 2 * nk:])
            pl.when(pl.program_id(0) == 0)(start)

        @pl.when(pl.program_id(0) == 0)
        def _():
            state[...] = jnp.zeros_like(state)

        sls = [slice(hd * HEAD_DIM, (hd + 1) * HEAD_DIM) for hd in range(HEADS)]
        rows = [slice(c * CHUNK, (c + 1) * CHUNK) for c in range(per)]
        s0 = [state[hd] for hd in range(HEADS)]
        for hd in range(HEADS):
            st_ref[hd, 0] = s0[hd]
        o, s1 = _gdn_chunk([q_ref[r, sl] for r in rows for sl in sls], [k_ref[r, sl] for r in rows for sl in sls],
                           [v_ref[r, sl] for r in rows for sl in sls], [ba_ref[r, :] for r in rows],
                           alog_ref[...], dtb_ref[...], s0)
        for c, r in enumerate(rows):
            for hd in range(HEADS):
                o_ref[r, sls[hd]] = o[c * HEADS + hd]
        for hd in range(HEADS):
            state[hd] = s1[hd]
        if nk:
            pl.when(pl.program_id(0) == nc - 1)(finish)

    def col(j):
        return pl.BlockSpec((T, GDN_W),lambda n: (n, j))

    vec = _const((1, LANES))
    outs = pl.pallas_call(
        body, name="gdn_fwd", grid=(nc,),
        in_specs=[col(0), col(1), col(2), pl.BlockSpec((T, LANES),lambda n: (n, IN_PAD // LANES - 1)), vec, vec]
        + _hbm_specs(nk),
        out_specs=[col(0), pl.BlockSpec((HEADS, 1, HEAD_DIM, HEAD_DIM), lambda n: (0, n, 0, 0))] + _hbm_specs(nk),
        out_shape=[_sds((S, GDN_W)), _sds((HEADS, nc, HEAD_DIM, HEAD_DIM))] + (_gather_out_shapes(gather) if nk else []),
        scratch_shapes=[pltpu.VMEM((HEADS, HEAD_DIM, HEAD_DIM), F32)] + (_gather_scratch(nk) if nk else []),
        compiler_params=_params(("arbitrary",)),
    )(qkv, qkv, qkv, proj, alog, dtb, *gather)
    return outs[0], outs[1], list(outs[2:])


def _gdn_bwd(do, qkv, proj, states, alog, dtb, scatter=()):
    S = qkv.shape[0]
    per = min(GDN_STEP_CHUNKS, S // CHUNK)
    T = per * CHUNK
    nc = S // T
    nk = len(scatter)

    def body(*refs):
        do_ref, q_ref, k_ref, v_ref, ba_ref, st_ref, alog_ref, dtb_ref = refs[:8]
        dqkv_ref, dba_ref, dalog_ref, ddtb_ref = refs[8 + nk:12 + nk]
        dstate = refs[12 + 2 * nk]
        n = pl.program_id(0)
        if nk:
            start, finish = _scatter_steps(refs[8:8 + nk], refs[12 + nk:12 + 2 * nk], *refs[13 + 2 * nk:])
            pl.when(n == 0)(start)

        @pl.when(n == 0)
        def _():
            dstate[...] = jnp.zeros_like(dstate)

        sls = [slice(hd * HEAD_DIM, (hd + 1) * HEAD_DIM) for hd in range(HEADS)]
        rows = [slice(c * CHUNK, (c + 1) * CHUNK) for c in range(per)]
        fn = functools.partial(_gdn_chunk, inverses=_unit_lower_inverses_diff, mm=_mm)
        _, vjp = jax.vjp(fn, [q_ref[r, sl] for r in rows for sl in sls], [k_ref[r, sl] for r in rows for sl in sls],
                         [v_ref[r, sl] for r in rows for sl in sls], [ba_ref[r, :] for r in rows],
                         alog_ref[...], dtb_ref[...], [st_ref[hd, 0] for hd in range(HEADS)])
        dq, dk, dv, dba, dalog, ddtb, ds = vjp(([do_ref[r, sl] for r in rows for sl in sls],
                                                [dstate[hd] for hd in range(HEADS)]))
        for c, r in enumerate(rows):
            for hd in range(HEADS):
                i = c * HEADS + hd
                dqkv_ref[r, sls[hd]] = dq[i]
                dqkv_ref[r, GDN_W + hd * HEAD_DIM:GDN_W + (hd + 1) * HEAD_DIM] = dk[i]
                dqkv_ref[r, 2 * GDN_W + hd * HEAD_DIM:2 * GDN_W + (hd + 1) * HEAD_DIM] = dv[i]
            dba_ref[r, :] = dba[c]
        for hd in range(HEADS):
            dstate[hd] = ds[hd]
        _acc(dalog_ref, dalog, n == 0)
        _acc(ddtb_ref, ddtb, n == 0)
        if nk:
            pl.when(n == nc - 1)(finish)

    def col(j):
        return pl.BlockSpec((T, GDN_W),lambda n: (nc - 1 - n, j))

    vec = _const((1, LANES))
    outs = pl.pallas_call(
        body, name="gdn_bwd", grid=(nc,),
        in_specs=[col(0), col(0), col(1), col(2),
                  pl.BlockSpec((T, LANES),lambda n: (nc - 1 - n, IN_PAD // LANES - 1)),
                  pl.BlockSpec((HEADS, 1, HEAD_DIM, HEAD_DIM), lambda n: (0, nc - 1 - n, 0, 0)), vec, vec]
        + _hbm_specs(nk),
        out_specs=[pl.BlockSpec((T, 3 * GDN_W),lambda n: (nc - 1 - n, 0)),
                   pl.BlockSpec((T, LANES),lambda n: (nc - 1 - n, 0)), vec, vec] + _hbm_specs(nk),
        out_shape=[_sds((S, 3 * GDN_W)), _sds((S, LANES)), _sds((1, LANES)), _sds((1, LANES))]
        + [_sds(p.shape, p.dtype) for p in scatter],
        scratch_shapes=[pltpu.VMEM((HEADS, HEAD_DIM, HEAD_DIM), F32)] + (_scatter_scratch(nk) if nk else []),
        compiler_params=_params(("arbitrary",)),
    )(do, qkv, qkv, qkv, proj, states, alog, dtb, *scatter)
    return outs[0], outs[1], outs[2], outs[3], list(outs[4:])


def _gdn_gate(o, z, gnw):
    outs = []
    for hd in range(HEADS):
        sl = slice(hd * HEAD_DIM, (hd + 1) * HEAD_DIM)
        xh, _ = _rms_parts(o[:, sl])
        outs.append(xh * gnw * _silu(z[:, sl]))
    return jnp.concatenate(outs, axis=-1)


def _out_fwd(x, out_lru, o, proj, gnw, g1, wout):
    S = x.shape[0]
    tm = min(512, S)

    def body(x_ref, lru_ref, o_ref, z_ref, gnw_ref, g1_ref, w_ref, x1_ref, cat_ref):
        cat = jnp.concatenate([lru_ref[...], _gdn_gate(o_ref[...], z_ref[...], gnw_ref[...])], axis=-1).astype(BF16)
        cat_ref[...] = cat
        x1_ref[...] = x_ref[...] + g1_ref[...] * _dot(cat, w_ref[...])

    return pl.pallas_call(
        body, name="out_fwd", grid=(S // tm,),
        in_specs=[_row(tm, D_MODEL), _row(tm, LRU_W), _row(tm, GDN_W), _row(tm, GDN_W, 5), _const((1, LANES)),
                  _const((1, D_MODEL)), _const((D_MODEL, D_MODEL))],
        out_specs=[_row(tm, D_MODEL), _row(tm, D_MODEL)],
        out_shape=[_sds((S, D_MODEL)), _sds((S, D_MODEL), BF16)],
        compiler_params=_params(("arbitrary",)),
    )(x, out_lru, o, proj, gnw, g1, wout)


def _out_bwd(dx1, cat, o, proj, gnw, g1, wout):
    S = dx1.shape[0]
    tm = min(512, S)

    def body(dx1_ref, cat_ref, o_ref, z_ref, gnw_ref, g1_ref, w_ref,
             dlru_ref, do_ref, dz_ref, dmb_ref, dgnw_ref, dg1_ref):
        i = pl.program_id(0)
        d1 = dx1_ref[...]
        mix = _dot(cat_ref[...], w_ref[...])
        _acc(dg1_ref, _colsum(d1 * mix), i == 0)
        dmb = (d1 * g1_ref[...]).astype(BF16)
        dmb_ref[...] = dmb
        dcat = _dot_nt(dmb, w_ref[...])
        dlru_ref[...] = dcat[:, :LRU_W]
        _, vjp = jax.vjp(_gdn_gate, o_ref[...], z_ref[...], gnw_ref[...])
        do, dz, dgnw = vjp(dcat[:, LRU_W:])
        do_ref[...] = do
        dz_ref[...] = dz
        _acc(dgnw_ref, dgnw, i == 0)

    return pl.pallas_call(
        body, name="out_bwd", grid=(S // tm,),
        in_specs=[_row(tm, D_MODEL), _row(tm, D_MODEL), _row(tm, GDN_W), _row(tm, GDN_W, 5), _const((1, LANES)),
                  _const((1, D_MODEL)), _const((D_MODEL, D_MODEL))],
        out_specs=[_row(tm, LRU_W), _row(tm, GDN_W), _row(tm, GDN_W), _row(tm, D_MODEL), _const((1, LANES)),
                   _const((1, D_MODEL))],
        out_shape=[_sds((S, LRU_W)), _sds((S, GDN_W)), _sds((S, GDN_W)), _sds((S, D_MODEL), BF16), _sds((1, LANES)),
                   _sds((1, D_MODEL))],
        compiler_params=_params(("arbitrary",)),
    )(dx1, cat, o, proj, gnw, g1, wout)


MLP_TM = 256


def _load_once(step, pairs, sem):
    @pl.when(step == 0)
    def _():
        copies = [pltpu.make_async_copy(src, dst, sem.at[k]) for k, (src, dst) in enumerate(pairs)]
        for cp in copies:
            cp.start()
        for cp in copies:
            cp.wait()


def _mlp_fwd(x1, nw, sc, sh, g2, wup, wdown):
    S = x1.shape[0]
    tm = min(MLP_TM, S)

    def body(x_ref, nw_ref, sc_ref, sh_ref, g2_ref, wup_hbm, wdown_hbm, x2_ref, wup, wdown, sem):
        _load_once(pl.program_id(0), [(wup_hbm, wup), (wdown_hbm, wdown)], sem)
        x = x_ref[...]
        hb = _norm_mod(x, nw_ref[...], sc_ref[...], sh_ref[...]).astype(BF16)
        r = jnp.maximum(_dot(hb, wup[...]), 0.0)
        x2_ref[...] = x + g2_ref[...] * _dot((r * r).astype(BF16), wdown[...])

    vec = _const((1, D_MODEL))
    anyspec = pl.BlockSpec(memory_space=pl.ANY)
    return pl.pallas_call(
        body, name="mlp_fwd", grid=(S // tm,),
        in_specs=[_row(tm, D_MODEL), vec, vec, vec, vec, anyspec, anyspec],
        out_specs=_row(tm, D_MODEL),
        out_shape=_sds((S, D_MODEL)),
        scratch_shapes=[pltpu.VMEM((D_MODEL, D_FF), BF16), pltpu.VMEM((D_FF, D_MODEL), BF16),
                        pltpu.SemaphoreType.DMA((2,))],
        compiler_params=_params(("arbitrary",)),
    )(x1, nw, sc, sh, g2, wup, wdown)


def _mlp_bwd(dx2, x1, nw, sc, sh, g2, wup, wdown):
    S = x1.shape[0]
    tm = min(MLP_TM, S)

    def body(dx2_ref, x_ref, nw_ref, sc_ref, sh_ref, g2_ref, wup_hbm, wdown_hbm,
             dx1_ref, hb_ref, dupb_ref, actb_ref, ddb_ref, dnw_ref, dsc_ref, dsh_ref, dg2_ref, wup, wdown, sem):
        i = pl.program_id(0)
        _load_once(i, [(wup_hbm, wup), (wdown_hbm, wdown)], sem)
        x = x_ref[...]
        d2 = dx2_ref[...]
        hb = _norm_mod(x, nw_ref[...], sc_ref[...], sh_ref[...]).astype(BF16)
        hb_ref[...] = hb
        r = jnp.maximum(_dot(hb, wup[...]), 0.0)
        actb = (r * r).astype(BF16)
        actb_ref[...] = actb
        down = _dot(actb, wdown[...])
        _acc(dg2_ref, _colsum(d2 * down), i == 0)
        ddb = (d2 * g2_ref[...]).astype(BF16)
        ddb_ref[...] = ddb
        dupb = (_dot_nt(ddb, wdown[...]) * (2.0 * r)).astype(BF16)
        dupb_ref[...] = dupb
        dh = _dot_nt(dupb, wup[...])
        dx, dnw, dsc, dsh = _norm_mod_bwd(dh, x, nw_ref[...], sc_ref[...])
        dx1_ref[...] = d2 + dx
        _acc(dnw_ref, dnw, i == 0)
        _acc(dsc_ref, dsc, i == 0)
        _acc(dsh_ref, dsh, i == 0)

    vec = _const((1, D_MODEL))
    anyspec = pl.BlockSpec(memory_space=pl.ANY)
    return pl.pallas_call(
        body, name="mlp_bwd", grid=(S // tm,),
        in_specs=[_row(tm, D_MODEL), _row(tm, D_MODEL), vec, vec, vec, vec, anyspec, anyspec],
        out_specs=[_row(tm, D_MODEL), _row(tm, D_MODEL), _row(tm, D_FF), _row(tm, D_FF), _row(tm, D_MODEL),
                   vec, vec, vec, vec],
        out_shape=[_sds((S, D_MODEL)), _sds((S, D_MODEL), BF16), _sds((S, D_FF), BF16), _sds((S, D_FF), BF16),
                   _sds((S, D_MODEL), BF16), _sds((1, D_MODEL)), _sds((1, D_MODEL)), _sds((1, D_MODEL)),
                   _sds((1, D_MODEL))],
        scratch_shapes=[pltpu.VMEM((D_MODEL, D_FF), BF16), pltpu.VMEM((D_FF, D_MODEL), BF16),
                        pltpu.SemaphoreType.DMA((2,))],
        compiler_params=_params(("arbitrary",)),
    )(dx2, x1, nw, sc, sh, g2, wup, wdown)


def _matmul_tn(a, b, name, shards=None, out_dtype=F32):
    K, M = a.shape
    N = b.shape[1]
    tk = min(2048, K)
    if shards == "cols":
        tm, tn = M // 2, N // N_CHIPS
        out_spec = pl.BlockSpec((1, 1, tm, tn), lambda i, j, k: (i, j, 0, 0))
        out_shape = _sds((2, N_CHIPS, tm, tn))
    elif shards == "rows":
        h, tn = M // (2 * N_CHIPS), N
        tm = max(512, 2 * h)
        per_tile = tm // (2 * h)
        out_spec = pl.BlockSpec((2, per_tile, h, tn), lambda i, j, k: (0, i, 0, 0))
        out_shape = _sds((2, N_CHIPS, h, tn))
    else:
        tm = min(512, M)
        tn = 640 if N % 640 == 0 else min(1024, N)
        out_spec = pl.BlockSpec((tm, tn), lambda i, j, k: (i, j))
        out_shape = _sds((M, N))
    nk = K // tk

    def body(a_ref, b_ref, o_ref, acc):
        k = pl.program_id(2)
        _acc(acc, _dot_tn(a_ref[...], b_ref[...]), k == 0)

        @pl.when(k == nk - 1)
        def _():
            if shards == "rows":
                for s in range(per_tile):
                    for half in range(2):
                        r0 = (2 * s + half) * h
                        o_ref[half, s] = acc[r0:r0 + h, :].astype(o_ref.dtype)
            else:
                o_ref[...] = acc[...].reshape(o_ref.shape).astype(o_ref.dtype)

    return pl.pallas_call(
        body, name=name, grid=(M // tm, N // tn, nk),
        in_specs=[pl.BlockSpec((tk, tm), lambda i, j, k: (k, i)), pl.BlockSpec((tk, tn), lambda i, j, k: (k, j))],
        out_specs=out_spec, out_shape=_sds(out_shape.shape, out_dtype),
        scratch_shapes=[pltpu.VMEM((tm, tn), F32)],
        compiler_params=_params(("arbitrary", "arbitrary", "arbitrary")),
    )(a, b)


def _loss_head(x, target, fnw):
    S = x.shape[0]
    tm = min(512, S)

    def body(x_ref, t_ref, w_ref, dx_ref, loss_ref, dw_ref):
        i = pl.program_id(0)
        w = w_ref[...]
        xh, r = _rms_parts(x_ref[...])
        err = xh * w - t_ref[...]
        part = 0.5 * jnp.sum(jnp.mean(err * err, axis=-1, keepdims=True), axis=0, keepdims=True)
        _acc(loss_ref, jnp.broadcast_to(part, (SUBLANES, LANES)), i == 0)
        dx, dw = _rms_bwd(err * (1.0 / D_MODEL), xh, r, w)
        dx_ref[...] = dx
        _acc(dw_ref, dw, i == 0)

    vec = _const((1, D_MODEL))
    return pl.pallas_call(
        body, name="loss_head", grid=(S // tm,),
        in_specs=[_row(tm, D_MODEL), _row(tm, D_MODEL), vec],
        out_specs=[_row(tm, D_MODEL), _const((SUBLANES, LANES)), vec],
        out_shape=[_sds((S, D_MODEL)), _sds((SUBLANES, LANES)), _sds((1, D_MODEL))],
        compiler_params=_params(("arbitrary",)),
    )(x, target, fnw)


def _block_diag(w):
    eye = jnp.eye(LRU_BLOCKS, dtype=w.dtype)
    return (eye[:, None, :, None] * w[:, :, None, :]).reshape(LRU_W, LRU_W)


def _diag_blocks(m):
    m4 = m.reshape(LRU_BLOCKS, LRU_BLOCK, LRU_BLOCKS, LRU_BLOCK)
    return jnp.stack([m4[g, :, g, :] for g in range(LRU_BLOCKS)])


def _layer_fwd(x, p, gather=()):
    proj, h1b = _proj_fwd(x, p["nmw"], p["sc1"], p["sh1"], p["win"])
    xr = _conv_fwd(proj, 0, LRU_W, p["lcw"], p["lcb"], False, "conv_lru_fwd")
    out_lru, h = _lru_fwd(xr, proj, p["wa"].astype(BF16), p["ba"], p["wx"].astype(BF16), p["bx"], p["lam"], p["lnw"])
    qkv = _conv_fwd(proj, 2 * LRU_W, 3 * GDN_W, p["gcw"], p["gcb"], True, "conv_gdn_fwd")
    o, states, gathered = _gdn_fwd(qkv, proj, p["alog"], p["dtb"], gather)
    x1, cat = _out_fwd(x, out_lru, o, proj, p["gnw"], p["g1"], p["wout"])
    x2 = _mlp_fwd(x1, p["nmlp"], p["sc2"], p["sh2"], p["g2"], p["wup"], p["wdown"])
    res = dict(x=x, proj=proj, h1b=h1b, xr=xr, h=h, qkv=qkv, o=o, states=states, x1=x1, cat=cat)
    return x2, res, gathered


def _layer_bwd(dx2, p, r, sharded=False, scatter=()):
    dx1, h2b, dupb, actb, ddb, dnmlp, dsc2, dsh2, dg2 = _mlp_bwd(
        dx2, r["x1"], p["nmlp"], p["sc2"], p["sh2"], p["g2"], p["wup"], p["wdown"])
    gdt = BF16 if sharded else F32
    g_wup = _matmul_tn(h2b, dupb, "dw_up", "cols" if sharded else None, gdt)
    g_wdown = _matmul_tn(actb, ddb, "dw_down", "rows" if sharded else None, gdt)
    dlru, do, dz, dmb, dgnw, dg1 = _out_bwd(dx1, r["cat"], r["o"], r["proj"], p["gnw"], p["g1"], p["wout"])
    g_wout = _matmul_tn(r["cat"], dmb, "dw_out", "rows" if sharded else None, gdt)
    dqkv_act, dba, dalog, ddtb, arrived = _gdn_bwd(do, r["qkv"], r["proj"], r["states"], p["alog"], p["dtb"], scatter)
    dqkv, dgcw, _ = _conv_bwd(r["proj"], 2 * LRU_W, 3 * GDN_W, p["gcw"], p["gcb"], dqkv_act, True, "conv_gdn_bwd")
    wab = p["wa"].astype(BF16)
    wxb = p["wx"].astype(BF16)
    dxr, dly, dwa, dba_, dwx, dbx, dlam, dlnw = _lru_bwd(
        dlru, r["xr"], r["proj"], r["h"], wab, p["ba"], wxb, p["bx"], p["lam"], p["lnw"])
    dlx, dlcw, dlcb = _conv_bwd(r["proj"], 0, LRU_W, p["lcw"], p["lcb"], dxr, False, "conv_lru_bwd")
    dx, dpb, dnmw, dsc1, dsh1 = _proj_bwd(dx1, r["x"], dlx, dly, dqkv, dz, dba, p["nmw"], p["sc1"], p["win"])
    g_win = _matmul_tn(r["h1b"], dpb, "dw_in", None, gdt)
    grads = dict(nmw=dnmw, nmlp=dnmlp, sh1=dsh1, sc1=dsc1, g1=dg1, sh2=dsh2, sc2=dsc2, g2=dg2,
                 win=g_win, lcw=dlcw, lcb=dlcb, wa=dwa, ba=dba_, wx=dwx, bx=dbx, lam=dlam, lnw=dlnw,
                 gcw=dgcw, alog=dalog, dtb=ddtb, gnw=dgnw, wout=g_wout, wup=g_wup, wdown=g_wdown)
    return dx, grads, arrived


def _local_step(x, target, fnw, layers):
    res = []
    for p in layers:
        x, r, _ = _layer_fwd(x, p)
        res.append(r)
    dx, loss_blk, dfnw = _loss_head(x, target, fnw)
    grads = [None] * len(layers)
    for l in reversed(range(len(layers))):
        dx, grads[l], _ = _layer_bwd(dx, layers[l], res[l])
    stacked = {k: jnp.stack([g[k] for g in grads]) for k in grads[0]}
    return loss_blk[0, 0], dx, dfnw, stacked


def _prep_layers(norm_mix_w, norm_mlp_w, mod, win_b, lru_conv_w, lru_conv_b, gate_a_w, gate_a_b, gate_x_w, gate_x_b,
                 lru_lambda, lru_norm_w, gdn_conv_w, gdn_a_log, gdn_dt_bias, gdn_norm_w, wout_b, wup_b, wdown_b):
    L = norm_mix_w.shape[0]

    def vec(a):
        return a.reshape(L, 1, -1)

    def lanes(a):
        return jnp.pad(a, ((0, 0), (0, LANES - a.shape[1]))).reshape(L, 1, LANES)

    def taps(w):
        return jnp.pad(w, ((0, 0), (0, SUBLANES - w.shape[1]), (0, 0)))

    m = mod.reshape(L, N_MOD, 1, D_MODEL)
    return dict(
        nmw=vec(norm_mix_w), nmlp=vec(norm_mlp_w),
        sh1=m[:, 0], sc1=m[:, 1], g1=m[:, 2], sh2=m[:, 3], sc2=m[:, 4], g2=m[:, 5],
        win=win_b, lcw=taps(lru_conv_w), lcb=vec(lru_conv_b),
        wa=jax.vmap(_block_diag)(gate_a_w), ba=vec(gate_a_b), wx=jax.vmap(_block_diag)(gate_x_w), bx=vec(gate_x_b),
        lam=vec(lru_lambda), lnw=vec(lru_norm_w),
        gcw=taps(gdn_conv_w), gcb=jnp.zeros((L, 1, 3 * GDN_W), F32),
        alog=lanes(gdn_a_log), dtb=lanes(gdn_dt_bias), gnw=vec(gdn_norm_w),
        wout=wout_b, wup=wup_b, wdown=wdown_b)


def _position():
    x, y, c = lax.axis_index("x"), lax.axis_index("y"), lax.axis_index("c")
    return x, y, c


def _other_chips(x, y):
    return [(1 - x, y), (x, 1 - y), (1 - x, 1 - y)]


def _all_gather_rows(block, name):
    m, n = block.shape

    def body(x_ref, out_ref, send_sems, recv_sems, local_sem):
        x, y, c = _position()
        me, sibling = (x, y, c), (x, y, 1 - c)
        chips = _other_chips(x, y)

        def rows(px, py, pc):
            return out_ref.at[pl.ds((4 * px + 2 * py + pc) * m, m), :]

        def copy(k, blk, to, src=None):
            return pltpu.make_async_remote_copy(
                src_ref=rows(*blk) if src is None else src, dst_ref=rows(*blk),
                send_sem=send_sems.at[k], recv_sem=recv_sems.at[k], device_id=to, device_id_type=MESH)

        mine = pltpu.make_async_copy(x_ref, rows(*me), local_sem)
        mine.start()
        first = [copy(0, me, sibling, src=x_ref)]
        first += [copy(1 + j, me, (*chip, c), src=x_ref) for j, chip in enumerate(chips)]
        for cp in first:
            cp.start()
        passed = [copy(4 + j, (*chip, c), sibling) for j, chip in enumerate(chips)]
        for j, chip in enumerate(chips):
            copy(1 + j, (*chip, c), me).wait_recv()
            passed[j].start()
        copy(0, sibling, me).wait_recv()
        for j, chip in enumerate(chips):
            copy(4 + j, (*chip, 1 - c), me).wait_recv()
        for cp in first + passed:
            cp.wait_send()
        mine.wait()

    return pl.pallas_call(
        body, name=name,
        out_shape=_sds((N_DEV * m, n)),
        in_specs=[pl.BlockSpec(memory_space=pltpu.VMEM)],
        out_specs=pl.BlockSpec(memory_space=pltpu.VMEM),
        scratch_shapes=[pltpu.SemaphoreType.DMA((7,)), pltpu.SemaphoreType.DMA((7,)), pltpu.SemaphoreType.DMA],
        compiler_params=pltpu.CompilerParams(vmem_limit_bytes=VMEM_LIMIT),
    )(block)


def _hbm_specs(n):
    return [pl.BlockSpec(memory_space=pl.ANY)] * n


def _gather_chips(shards, name):
    n = len(shards)

    def body(*refs):
        start, finish = _gather_steps(refs[:n], refs[n:2 * n], *refs[2 * n:])
        start()
        finish()

    return pl.pallas_call(
        body, name=name,
        out_shape=_gather_out_shapes(shards), in_specs=_hbm_specs(n), out_specs=_hbm_specs(n),
        scratch_shapes=_gather_scratch(n),
    )(*shards)


def _gather_out_shapes(shards):
    return [_sds((N_CHIPS, 2, s.shape[0] // 2, s.shape[1]), s.dtype) for s in shards]


def _gather_scratch(n):
    return [pltpu.SemaphoreType.DMA((6 * n,)), pltpu.SemaphoreType.DMA((6 * n,))]


def _gather_steps(ins, outs, send_sems, recv_sems):
    n = len(ins)
    x, y, c = _position()
    chips = _other_chips(x, y)
    me = 2 * x + y

    def first(a, j, slot):
        h = ins[a].shape[0] // 2
        return pltpu.make_async_remote_copy(
            src_ref=ins[a].at[pl.ds(pl.multiple_of(c * h, SUBLANES), h)], dst_ref=outs[a].at[slot, c],
            send_sem=send_sems.at[3 * a + j], recv_sem=recv_sems.at[3 * a + j],
            device_id=(chips[j][0], chips[j][1], c), device_id_type=MESH)

    def second(a, j, half):
        slot = 2 * chips[j][0] + chips[j][1]
        return pltpu.make_async_remote_copy(
            src_ref=outs[a].at[slot, c], dst_ref=outs[a].at[slot, half],
            send_sem=send_sems.at[3 * (n + a) + j], recv_sem=recv_sems.at[3 * (n + a) + j],
            device_id=(x, y, 1 - c), device_id_type=MESH)

    def start():
        for a in range(n):
            for j in range(3):
                first(a, j, me).start()

    def finish():
        for a in range(n):
            for j, (px, py) in enumerate(chips):
                first(a, j, 2 * px + py).wait_recv()
                second(a, j, c).start()
        for a in range(n):
            for j in range(3):
                second(a, j, 1 - c).wait_recv()
        for a in range(n):
            for j in range(3):
                first(a, j, me).wait_send()
                second(a, j, c).wait_send()

    return start, finish


def _send_to_sibling(parts, name):
    n = len(parts)

    def body(*refs):
        ins, outs = refs[:n], refs[n:2 * n]
        send_sems, recv_sems = refs[2 * n:]
        x, y, c = _position()
        copies = [pltpu.make_async_remote_copy(
            src_ref=ins[a].at[1 - c], dst_ref=outs[a], send_sem=send_sems.at[a], recv_sem=recv_sems.at[a],
            device_id=(x, y, 1 - c), device_id_type=MESH) for a in range(n)]
        for cp in copies:
            cp.start()
        for cp in copies:
            cp.wait()

    return pl.pallas_call(
        body, name=name,
        out_shape=[_sds(p.shape[1:], p.dtype) for p in parts],
        in_specs=_hbm_specs(n), out_specs=_hbm_specs(n),
        scratch_shapes=[pltpu.SemaphoreType.DMA((n,)), pltpu.SemaphoreType.DMA((n,))],
    )(*parts)


def _scatter_chips(parts, name):
    n = len(parts)

    def body(*refs):
        start, finish = _scatter_steps(refs[:n], refs[n:2 * n], *refs[2 * n:])
        start()
        finish()

    return pl.pallas_call(
        body, name=name,
        out_shape=[_sds(p.shape, p.dtype) for p in parts], in_specs=_hbm_specs(n), out_specs=_hbm_specs(n),
        scratch_shapes=_scatter_scratch(n),
    )(*parts)


def _scatter_scratch(n):
    return [pltpu.SemaphoreType.DMA((3 * n,)), pltpu.SemaphoreType.DMA((3 * n,))]


def _scatter_steps(ins, outs, send_sems, recv_sems):
    n = len(ins)
    x, y, c = _position()
    chips = _other_chips(x, y)
    me = 2 * x + y

    def copy(a, j, src_slot, dst_slot):
        px, py = chips[j]
        return pltpu.make_async_remote_copy(
            src_ref=ins[a].at[src_slot], dst_ref=outs[a].at[dst_slot], send_sem=send_sems.at[3 * a + j],
            recv_sem=recv_sems.at[3 * a + j], device_id=(px, py, c), device_id_type=MESH)

    def start():
        for a in range(n):
            for j in range(3):
                copy(a, j, 2 * chips[j][0] + chips[j][1], me).start()

    def finish():
        for a in range(n):
            for j, (px, py) in enumerate(chips):
                copy(a, j, me, 2 * px + py).wait_recv()
        for a in range(n):
            for j in range(3):
                copy(a, j, 2 * chips[j][0] + chips[j][1], me).wait_send()

    return start, finish


def _swap_row_halves(arrays, name):
    n = len(arrays)

    def body(*refs):
        outs = refs[n:2 * n]
        send_sems, recv_sems = refs[2 * n:]
        x, y, c = _position()

        def copy(a, half):
            h = outs[a].shape[0] // 2
            rows = outs[a].at[pl.ds(pl.multiple_of(half * h, SUBLANES), h)]
            return pltpu.make_async_remote_copy(
                src_ref=rows, dst_ref=rows, send_sem=send_sems.at[a], recv_sem=recv_sems.at[a],
                device_id=(x, y, 1 - c), device_id_type=MESH)

        sends = [copy(a, c) for a in range(n)]
        for cp in sends:
            cp.start()
        for a in range(n):
            copy(a, 1 - c).wait_recv()
        for cp in sends:
            cp.wait_send()

    return pl.pallas_call(
        body, name=name,
        out_shape=[_sds(a.shape, a.dtype) for a in arrays],
        in_specs=_hbm_specs(n), out_specs=_hbm_specs(n),
        input_output_aliases={a: a for a in range(n)},
        scratch_shapes=[pltpu.SemaphoreType.DMA((n,)), pltpu.SemaphoreType.DMA((n,))],
    )(*arrays)


def _row_tile(rows):
    for t in (512, 256, 128, 64, 32, 16, 8):
        if rows % t == 0:
            return t
    return rows


def _sum_slots(buf, name):
    k, rows, cols = buf.shape
    tm = _row_tile(rows)

    def body(b_ref, o_ref):
        s = b_ref[0]
        for i in range(1, k):
            s = s + b_ref[i]
        o_ref[...] = s

    return pl.pallas_call(
        body, name=name, grid=(rows // tm,),
        in_specs=[pl.BlockSpec((k, tm, cols), lambda i: (0, i, 0))],
        out_specs=pl.BlockSpec((tm, cols), lambda i: (i, 0)),
        out_shape=_sds((rows, cols)),
        compiler_params=_params(("arbitrary",)),
    )(buf)


def _pair_add(part, from_sibling, core, name):
    _, k, h, cols = part.shape
    rows = k * h
    tm = _row_tile(rows)

    def body(core_ref, a_ref, b_ref, o_ref):
        o_ref[...] = (a_ref[0].astype(F32) + b_ref[...].astype(F32)).astype(BF16)

    out = pl.pallas_call(
        body, name=name,
        grid_spec=pltpu.PrefetchScalarGridSpec(
            num_scalar_prefetch=1, grid=(rows // tm,),
            in_specs=[pl.BlockSpec((1, tm, cols), lambda i, cr: (cr[0], i, 0)),
                      pl.BlockSpec((tm, cols), lambda i, cr: (i, 0))],
            out_specs=pl.BlockSpec((tm, cols), lambda i, cr: (i, 0))),
        out_shape=_sds((rows, cols), BF16),
        compiler_params=_params(("arbitrary",)),
    )(core, part.reshape(2, rows, cols), from_sibling.reshape(rows, cols))
    return out.reshape(k, h, cols)


def _chip_sum(arrived, own, place, name):
    _, h, cols = arrived.shape
    tm = min(256, h)
    nb = h // tm

    def body(place_ref, arr_ref, own_ref, g_ref):
        for chip in range(N_CHIPS):
            @pl.when(place_ref[1] == chip)
            def _():
                terms = [own_ref[0] if j == chip else arr_ref[j] for j in range(N_CHIPS)]
                g = terms[0].astype(F32)
                for t in terms[1:]:
                    g = g + t.astype(F32)
                g_ref[...] = g

    return pl.pallas_call(
        body, name=name,
        grid_spec=pltpu.PrefetchScalarGridSpec(
            num_scalar_prefetch=1, grid=(nb,),
            in_specs=[pl.BlockSpec((N_CHIPS, tm, cols), lambda i, pr: (0, i, 0)),
                      pl.BlockSpec((1, tm, cols), lambda i, pr: (pr[1], i, 0))],
            out_specs=pl.BlockSpec((tm, cols), lambda i, pr: (pr[0] * nb + i, 0))),
        out_shape=_sds((2 * h, cols)),
        compiler_params=_params(("arbitrary",)),
    )(place, arrived, own)


def _adam_layer(g, w, m, v, outs, layer, name):
    rows, cols = g.shape
    tm = _row_tile(rows)

    def body(g_ref, w_ref, m_ref, v_ref, *refs):
        og_ref, od_ref, om_ref, ov_ref = refs[4:]
        gr = g_ref[...]
        og_ref[0] = gr
        d, nm, nv = _adam_math(w_ref[0], gr, m_ref[0], v_ref[0])
        od_ref[0] = d
        om_ref[0] = nm
        ov_ref[0] = nv

    slab = pl.BlockSpec((1, tm, cols), lambda i: (layer, i, 0))
    return pl.pallas_call(
        body, name=name, grid=(rows // tm,),
        in_specs=[pl.BlockSpec((tm, cols), lambda i: (i, 0)), slab, slab, slab] + _hbm_specs(4),
        out_specs=[slab] * 4, out_shape=[_sds(o.shape) for o in outs],
        input_output_aliases={4 + i: i for i in range(4)},
        compiler_params=_params(("arbitrary",)),
    )(g, w, m, v, *outs)


def _adam_math(w, g, m, v):
    m = ADAM_B1 * m + (1.0 - ADAM_B1) * g
    v = ADAM_B2 * v + (1.0 - ADAM_B2) * jnp.square(g)
    m_hat = m / (1.0 - ADAM_B1 ** ADAM_STEP)
    v_hat = v / (1.0 - ADAM_B2 ** ADAM_STEP)
    delta = -ADAM_LR * (m_hat / (jnp.sqrt(v_hat) + ADAM_EPS) + ADAM_WD * w)
    return delta, m, v


def _adam(w, g, m, v, name):
    rows, cols = w.shape
    tm = _row_tile(rows)

    def body(w_ref, g_ref, m_ref, v_ref, d_ref, nm_ref, nv_ref):
        d, nm, nv = _adam_math(w_ref[...], g_ref[...], m_ref[...], v_ref[...])
        d_ref[...] = d
        nm_ref[...] = nm
        nv_ref[...] = nv

    spec = pl.BlockSpec((tm, cols), lambda i: (i, 0))
    return pl.pallas_call(
        body, name=name, grid=(rows // tm,), in_specs=[spec] * 4, out_specs=[spec] * 3,
        out_shape=[_sds((rows, cols))] * 3, compiler_params=_params(("arbitrary",)),
    )(w, g, m, v)


def _mod_fwd(c_all, w_mod, b_mod_cols):
    L, _, n = w_mod.shape

    def body(c_ref, w_ref, b_ref, o_ref):
        o_ref[0] = _hdot(_silu(c_ref[...]), w_ref[0]) + b_ref[0]

    return pl.pallas_call(
        body, name="mod_fwd", grid=(L,),
        in_specs=[_const((N_DEV, D_MODEL)), pl.BlockSpec((1, D_MODEL, n), lambda l: (l, 0, 0)),
                  pl.BlockSpec((1, 1, n), lambda l: (l, 0, 0))],
        out_specs=pl.BlockSpec((1, N_DEV, n), lambda l: (l, 0, 0)),
        out_shape=_sds((L, N_DEV, n)),
        compiler_params=_params(("arbitrary",)),
    )(c_all, w_mod, b_mod_cols)


def _mod_update(c_all, dmod, w, m, v):
    L, _, n = w.shape
    tn = 512

    def body(c_ref, d_ref, w_ref, m_ref, v_ref, g_ref, dl_ref, nm_ref, nv_ref):
        g = _hdot_tn(_silu(c_ref[...]), d_ref[0])
        g_ref[0] = g
        d, nm, nv = _adam_math(w_ref[0], g, m_ref[0], v_ref[0])
        dl_ref[0] = d
        nm_ref[0] = nm
        nv_ref[0] = nv

    big = pl.BlockSpec((1, D_MODEL, tn), lambda l, j: (l, 0, j))
    return pl.pallas_call(
        body, name="mod_update", grid=(L, n // tn),
        in_specs=[_const((N_DEV, D_MODEL)), pl.BlockSpec((1, N_DEV, tn), lambda l, j: (l, 0, j)), big, big, big],
        out_specs=[big] * 4, out_shape=[_sds(w.shape)] * 4,
        compiler_params=_params(("arbitrary", "arbitrary")),
    )(c_all, dmod, w, m, v)


def _pack_rows(parts, row_multiple):
    flat = jnp.concatenate([p.reshape(-1) for p in parts])
    unit = row_multiple * LANES
    flat = jnp.pad(flat, (0, (-flat.shape[0]) % unit))
    return flat.reshape(-1, LANES)


def _unpack(packed, shapes):
    flat = packed.reshape(-1)
    out, off = [], 0
    for s in shapes:
        n = 1
        for d in s:
            n *= d
        out.append(flat[off:off + n].reshape(s))
        off += n
    return out


def _lane_pad(a):
    return jnp.pad(a, ((0, 0), (0, LANES - a.shape[1])))


WEIGHT_NAMES = ("norm_mix_w", "norm_mlp_w", "w_mod", "b_mod", "w_in", "lru_conv_w", "lru_conv_b", "lru_gate_a_w",
                "lru_gate_a_b", "lru_gate_x_w", "lru_gate_x_b", "lru_lambda", "lru_norm_w", "gdn_conv_w", "gdn_a_log",
                "gdn_dt_bias", "gdn_norm_w", "w_out", "w_up", "w_down", "final_norm_w")


def kernel(x, c, norm_mix_w, norm_mlp_w, w_mod, b_mod, w_in, lru_conv_w, lru_conv_b, lru_gate_a_w, lru_gate_a_b, lru_gate_x_w, lru_gate_x_b, lru_lambda, lru_norm_w, gdn_conv_w, gdn_a_log, gdn_dt_bias, gdn_norm_w, w_out, w_up, w_down, final_norm_w, loss_target, m_norm_mix_w, m_norm_mlp_w, m_w_mod, m_b_mod, m_w_in, m_lru_conv_w, m_lru_conv_b, m_lru_gate_a_w, m_lru_gate_a_b, m_lru_gate_x_w, m_lru_gate_x_b, m_lru_lambda, m_lru_norm_w, m_gdn_conv_w, m_gdn_a_log, m_gdn_dt_bias, m_gdn_norm_w, m_w_out, m_w_up, m_w_down, m_final_norm_w, v_norm_mix_w, v_norm_mlp_w, v_w_mod, v_b_mod, v_w_in, v_lru_conv_w, v_lru_conv_b, v_lru_gate_a_w, v_lru_gate_a_b, v_lru_gate_x_w, v_lru_gate_x_b, v_lru_lambda, v_lru_norm_w, v_gdn_conv_w, v_gdn_a_log, v_gdn_dt_bias, v_gdn_norm_w, v_w_out, v_w_up, v_w_down, v_final_norm_w):
    W = dict(zip(WEIGHT_NAMES, (norm_mix_w, norm_mlp_w, w_mod, b_mod, w_in, lru_conv_w, lru_conv_b, lru_gate_a_w,
                                lru_gate_a_b, lru_gate_x_w, lru_gate_x_b, lru_lambda, lru_norm_w, gdn_conv_w, gdn_a_log,
                                gdn_dt_bias, gdn_norm_w, w_out, w_up, w_down, final_norm_w)))
    M = dict(zip(WEIGHT_NAMES, (m_norm_mix_w, m_norm_mlp_w, m_w_mod, m_b_mod, m_w_in, m_lru_conv_w, m_lru_conv_b,
                                m_lru_gate_a_w, m_lru_gate_a_b, m_lru_gate_x_w, m_lru_gate_x_b, m_lru_lambda,
                                m_lru_norm_w, m_gdn_conv_w, m_gdn_a_log, m_gdn_dt_bias, m_gdn_norm_w, m_w_out, m_w_up,
                                m_w_down, m_final_norm_w)))
    V = dict(zip(WEIGHT_NAMES, (v_norm_mix_w, v_norm_mlp_w, v_w_mod, v_b_mod, v_w_in, v_lru_conv_w, v_lru_conv_b,
                                v_lru_gate_a_w, v_lru_gate_a_b, v_lru_gate_x_w, v_lru_gate_x_b, v_lru_lambda,
                                v_lru_norm_w, v_gdn_conv_w, v_gdn_a_log, v_gdn_dt_bias, v_gdn_norm_w, v_w_out, v_w_up,
                                v_w_down, v_final_norm_w)))
    L = DEPTH
    xi, yi, ci = _position()
    chip = 2 * xi + yi
    dev = 2 * chip + ci
    lcs = LRU_W // N_CHIPS
    gcs = 3 * GDN_W // N_CHIPS
    mcs = N_MOD * D_MODEL // N_CHIPS

    g_in = _all_gather_rows(_pack_rows([c, lru_conv_w, gdn_conv_w], SUBLANES), "gather_small_inputs").reshape(N_DEV, -1)
    c_all = g_in[:, :D_MODEL]
    per_chip = g_in[0::2]
    o1 = D_MODEL + L * 4 * lcs
    lcw_full = per_chip[:, D_MODEL:o1].reshape(N_CHIPS, L, 4, lcs).transpose(1, 2, 0, 3).reshape(L, 4, LRU_W)
    gcw_full = per_chip[:, o1:o1 + L * 4 * gcs].reshape(N_CHIPS, L, 4, gcs).transpose(1, 2, 0, 3).reshape(L, 4, 3 * GDN_W)

    b_cols = lax.dynamic_slice(b_mod, (0, chip * mcs), (L, mcs)).reshape(L, 1, mcs)
    modp = _mod_fwd(c_all, w_mod, b_cols)
    g_mod = _all_gather_rows(modp.reshape(L * N_DEV, mcs), "gather_mod").reshape(N_DEV, L, N_DEV, mcs)
    mod = lax.dynamic_index_in_dim(g_mod[0::2], dev, axis=2, keepdims=False).transpose(1, 0, 2).reshape(L, N_MOD * D_MODEL)

    stacked = _prep_layers(norm_mix_w, norm_mlp_w, mod, None, lcw_full, lru_conv_b, lru_gate_a_w, lru_gate_a_b,
                           lru_gate_x_w, lru_gate_x_b, lru_lambda, lru_norm_w, gcw_full, gdn_a_log, gdn_dt_bias,
                           gdn_norm_w, None, None, None)
    shards = [[w_in[l].astype(BF16), w_out[l].astype(BF16), w_up[l].astype(BF16), w_down[l].astype(BF16)]
              for l in range(L)]

    def layer_params(l, gathered):
        win_g, wout_g, wup_g, wdown_g = (
            lax.dynamic_update_slice(got, own.reshape((1,) + got.shape[1:]), (chip, 0, 0, 0)).reshape(
                (N_CHIPS,) + own.shape) for got, own in zip(gathered, shards[l]))
        p = {k: v[l] for k, v in stacked.items() if v is not None}
        p["win"] = jnp.pad(win_g.transpose(1, 0, 2).reshape(D_MODEL, IN_COLS), ((0, 0), (0, IN_PAD - IN_COLS)))
        p["wout"] = wout_g.reshape(D_MODEL, D_MODEL)
        p["wup"] = wup_g.transpose(1, 0, 2).reshape(D_MODEL, D_FF)
        p["wdown"] = wdown_g.reshape(D_FF, D_MODEL)
        return p

    layers = [layer_params(0, _gather_chips(shards[0], "gather_weights"))]
    xs = x[0]
    res = []
    for l in range(L):
        xs, r, gathered = _layer_fwd(xs, layers[l], shards[l + 1] if l + 1 < L else ())
        res.append(r)
        if l + 1 < L:
            layers.append(layer_params(l + 1, gathered))
    dx, loss_blk, dfnw = _loss_head(xs, loss_target[0], final_norm_w.reshape(1, D_MODEL))
    loss_local = loss_blk[0, 0]

    big_names = ["w_in", "w_out", "w_up", "w_down"]
    core = jnp.reshape(ci, (1,)).astype(jnp.int32)
    place = jnp.stack([ci, chip]).astype(jnp.int32)
    layer_grads = [None] * L

    big = {nm: [lax.empty(W[nm].shape, F32) for _ in range(4)] for nm in big_names}

    def apply_update(l, arrived, pair):
        halves = [_chip_sum(a, own, place, "chip_sum_" + nm) for nm, a, own in zip(big_names, arrived, pair)]
        full = _swap_row_halves(halves, "pair_swap")
        for nm, gr in zip(big_names, full):
            big[nm] = _adam_layer(gr, W[nm], M[nm], V[nm], big[nm], l, "adam_" + nm)

    pair = ()
    for l in reversed(range(L)):
        dx, gl, arrived = _layer_bwd(dx, layers[l], res[l], sharded=True, scatter=pair)
        if pair:
            apply_update(l + 1, arrived, pair)
        layer_grads[l] = gl
        gwin = gl["win"][:, :IN_COLS].reshape(2, D_MODEL // 2, N_CHIPS, IN_COLS // N_CHIPS).transpose(0, 2, 1, 3)
        parts = [gwin, gl["wout"], gl["wup"], gl["wdown"]]
        from_sibling = _send_to_sibling(parts, "pair_send")
        pair = [_pair_add(p, r, core, "pair_add_" + nm) for nm, p, r in zip(big_names, parts, from_sibling)]
    apply_update(0, _scatter_chips(pair, "chip_scatter"), pair)
    small_keys = [k for k in layer_grads[0] if k not in ("win", "wout", "wup", "wdown")]
    g = {k: jnp.stack([gl[k] for gl in layer_grads]) for k in small_keys}
    loss = lax.psum(loss_local, ("x", "y", "c"))

    dmod = jnp.concatenate([g["sh1"], g["sc1"], g["g1"], g["sh2"], g["sc2"], g["g2"]], axis=-1)
    small = [dmod, g["nmw"], g["nmlp"], g["lcw"][:, :4], g["lcb"], jax.vmap(_diag_blocks)(g["wa"]), g["ba"],
             jax.vmap(_diag_blocks)(g["wx"]), g["bx"], g["lam"], g["lnw"], g["gcw"][:, :4], g["alog"], g["dtb"],
             g["gnw"], dfnw]
    small_shapes = [(L, N_MOD * D_MODEL), (L, D_MODEL), (L, D_MODEL), (L, 4, LRU_W), (L, LRU_W),
                    (L, LRU_BLOCKS, LRU_BLOCK, LRU_BLOCK), (L, LRU_W), (L, LRU_BLOCKS, LRU_BLOCK, LRU_BLOCK),
                    (L, LRU_W), (L, LRU_W), (L, LRU_W), (L, 4, 3 * GDN_W), (L, LANES), (L, LANES), (L, LANES),
                    (D_MODEL,)]
    small_names = ["b_mod", "norm_mix_w", "norm_mlp_w", None, "lru_conv_b", "lru_gate_a_w", "lru_gate_a_b",
                   "lru_gate_x_w", "lru_gate_x_b", "lru_lambda", "lru_norm_w", None, "gdn_a_log", "gdn_dt_bias",
                   "gdn_norm_w", "final_norm_w"]
    pack_g = _pack_rows(small, 512)
    rows = pack_g.shape[0]
    all_g = _all_gather_rows(pack_g, "gather_small_grads").reshape(N_DEV, rows, LANES)
    tot = _sum_slots(all_g, "sum_small_grads")
    tot_parts = _unpack(tot, small_shapes)

    def pack_state(S_):
        parts = []
        for nm, shp in zip(small_names, small_shapes):
            if nm is None:
                parts.append(jnp.zeros(shp, F32))
            elif nm in ("gdn_a_log", "gdn_dt_bias"):
                parts.append(_lane_pad(S_[nm]))
            else:
                parts.append(S_[nm])
        return _pack_rows(parts, 512)

    upd = _adam(pack_state(W), tot, pack_state(M), pack_state(V), "adam_small")
    upd_parts = [_unpack(u, small_shapes) for u in upd]

    grads, deltas, new_m, new_v = {}, {}, {}, {}
    for k, nm in enumerate(small_names):
        if nm is None:
            continue
        cut = (lambda a: a[:, :HEADS]) if nm in ("gdn_a_log", "gdn_dt_bias") else (lambda a: a)
        grads[nm] = cut(tot_parts[k])
        deltas[nm], new_m[nm], new_v[nm] = (cut(u[k]) for u in upd_parts)

    g_lcw = lax.dynamic_slice(tot_parts[3], (0, 0, chip * lcs), (L, 4, lcs))
    g_gcw = lax.dynamic_slice(tot_parts[11], (0, 0, chip * gcs), (L, 4, gcs))
    conv_shapes = [(L, 4, lcs), (L, 4, gcs)]
    conv_pack = lambda a, b: _pack_rows([a, b], SUBLANES)
    cu = _adam(conv_pack(lru_conv_w, gdn_conv_w), conv_pack(g_lcw, g_gcw), conv_pack(m_lru_conv_w, m_gdn_conv_w),
               conv_pack(v_lru_conv_w, v_gdn_conv_w), "adam_conv")
    cu_parts = [_unpack(u, conv_shapes) for u in cu]
    for k, nm in enumerate(("lru_conv_w", "gdn_conv_w")):
        grads[nm] = (g_lcw, g_gcw)[k]
        deltas[nm], new_m[nm], new_v[nm] = (u[k] for u in cu_parts)

    dmod_all = all_g[:, :L * N_MOD * D_MODEL // LANES].reshape(N_DEV, L, N_MOD * D_MODEL)
    dmod_cols = lax.dynamic_slice(dmod_all, (0, 0, chip * mcs), (N_DEV, L, mcs)).transpose(1, 0, 2)
    grads["w_mod"], deltas["w_mod"], new_m["w_mod"], new_v["w_mod"] = _mod_update(c_all, dmod_cols, w_mod, m_w_mod, v_w_mod)

    for nm in big_names:
        grads[nm], deltas[nm], new_m[nm], new_v[nm] = big[nm]

    out = [loss, dx[None]]
    for group in (grads, deltas, new_m, new_v):
        out += [group[nm].reshape(W[nm].shape) for nm in WEIGHT_NAMES]
    return tuple(out)
```

```python
import functools

import jax
import jax.numpy as jnp
from jax import lax
from jax.experimental import pallas as pl
from jax.experimental.pallas import tpu as pltpu

F32 = jnp.float32
BF16 = jnp.bfloat16
MESH = pl.DeviceIdType.MESH

D_MODEL = 1024
DEPTH = 4
LRU_W = 512
LRU_BLOCKS = 8
LRU_BLOCK = 64
LRU_C = 8.0
HEADS = 4
HEAD_DIM = 128
GDN_W = 512
CHUNK = 128
D_FF = 4096
N_MOD = 6
IN_COLS = 3080
IN_PAD = 3200
NORM_EPS = 1e-6
LANES = 128
SUBLANES = 8
N_DEV = 8
N_CHIPS = 4

ADAM_LR = 0.001
ADAM_B1 = 0.9
ADAM_B2 = 0.999
ADAM_EPS = 1e-08
ADAM_WD = 0.01
ADAM_STEP = 10

VMEM_LIMIT = 56 * 1024 * 1024
HI = lax.Precision.HIGHEST


def _sds(shape, dtype=F32):
    return jax.ShapeDtypeStruct(tuple(shape), dtype)


def _params(sem=None, vmem=VMEM_LIMIT):
    return pltpu.CompilerParams(dimension_semantics=sem, vmem_limit_bytes=vmem)


def _const(shape):
    return pl.BlockSpec(tuple(shape), lambda *_: (0,) * len(shape))


def _row(tm, c, col=0):
    return pl.BlockSpec((tm, c), lambda i: (i, col))


def _dot(a, b):
    return jnp.dot(a, b, preferred_element_type=F32)


def _dot_nt(a, b):
    return lax.dot_general(a, b, (((1,), (1,)), ((), ())), preferred_element_type=F32)


def _dot_tn(a, b):
    return lax.dot_general(a, b, (((0,), (0,)), ((), ())), preferred_element_type=F32)


def _hdot(a, b):
    return jnp.dot(a, b, preferred_element_type=F32, precision=HI)


def _hdot_nt(a, b):
    return lax.dot_general(a, b, (((1,), (1,)), ((), ())), preferred_element_type=F32, precision=HI)


def _hdot_tn(a, b):
    return lax.dot_general(a, b, (((0,), (0,)), ((), ())), preferred_element_type=F32, precision=HI)


_DIMS = {"nn": (((1,), (0,)), ((), ())), "nt": (((1,), (1,)), ((), ())), "tn": (((0,), (0,)), ((), ()))}


def _mm_raw(a, b, dims, passes):
    dn = _DIMS[dims]

    def dot(p, q):
        return lax.dot_general(p, q, dn, preferred_element_type=F32)

    a_hi = a.astype(BF16)
    b_hi = b.astype(BF16)
    if passes == 1:
        return dot(a_hi, b_hi)
    a_lo = (a - a_hi.astype(F32)).astype(BF16)
    b_lo = (b - b_hi.astype(F32)).astype(BF16)
    return dot(a_hi, b_hi) + (dot(a_hi, b_lo) + dot(a_lo, b_hi))


@functools.partial(jax.custom_vjp, nondiff_argnums=(2, 3))
def _mm(a, b, dims, passes):
    return _mm_raw(a, b, dims, passes)


def _mm_fwd(a, b, dims, passes):
    return _mm_raw(a, b, dims, passes), (a, b)


def _mm_bwd(dims, passes, res, ct):
    a, b = res
    if dims == "nn":
        return _mm_raw(ct, b, "nt", passes), _mm_raw(a, ct, "tn", passes)
    if dims == "nt":
        return _mm_raw(ct, b, "nn", passes), _mm_raw(ct, a, "tn", passes)
    return _mm_raw(b, ct, "nt", passes), _mm_raw(a, ct, "nn", passes)


_mm.defvjp(_mm_fwd, _mm_bwd)


def _acc(ref, val, first):
    @pl.when(first)
    def _():
        ref[...] = val

    @pl.when(jnp.logical_not(first))
    def _():
        ref[...] += val


def _colsum(v):
    return jnp.sum(v, axis=0, keepdims=True)


def _rms_parts(x):
    r = lax.rsqrt(jnp.mean(x * x, axis=-1, keepdims=True) + NORM_EPS)
    return x * r, r


def _rms_bwd(dy, xh, r, w):
    dxh = dy * w
    dw = _colsum(dy * xh)
    dx = r * (dxh - xh * jnp.mean(dxh * xh, axis=-1, keepdims=True))
    return dx, dw


def _norm_mod(x, w, sc, sh):
    xh, _ = _rms_parts(x)
    return (xh * w) * (1.0 + sc) + sh


def _norm_mod_bwd(dy, x, w, sc):
    xh, r = _rms_parts(x)
    n = xh * w
    dsh = _colsum(dy)
    dsc = _colsum(dy * n)
    dx, dw = _rms_bwd(dy * (1.0 + sc), xh, r, w)
    return dx, dw, dsc, dsh


def _softplus(x):
    return jnp.maximum(x, 0.0) + jnp.log1p(jnp.exp(-jnp.abs(x)))


def _silu(x):
    return x * jax.nn.sigmoid(x)


def _silu_grad(x):
    s = jax.nn.sigmoid(x)
    return s * (1.0 + x * (1.0 - s))


def _roll_dn(x, d):
    return x if d == 0 else pltpu.roll(x, d, 0)


def _roll_up(x, d):
    return x if d == 0 else pltpu.roll(x, x.shape[0] - d, 0)


def _proj_fwd(x, nw, sc, sh, win):
    S = x.shape[0]
    tm = min(512, S)

    def body(x_ref, nw_ref, sc_ref, sh_ref, w_ref, proj_ref, hb_ref):
        hb = _norm_mod(x_ref[...], nw_ref[...], sc_ref[...], sh_ref[...]).astype(BF16)
        hb_ref[...] = hb
        proj_ref[...] = _dot(hb, w_ref[...])

    vec = _const((1, D_MODEL))
    return pl.pallas_call(
        body, name="proj_fwd", grid=(S // tm,),
        in_specs=[_row(tm, D_MODEL), vec, vec, vec, _const((D_MODEL, IN_PAD))],
        out_specs=[_row(tm, IN_PAD), _row(tm, D_MODEL)],
        out_shape=[_sds((S, IN_PAD)), _sds((S, D_MODEL), BF16)],
        compiler_params=_params(("arbitrary",)),
    )(x, nw, sc, sh, win)


def _proj_bwd(dx1, x, dlx, dly, dqkv, dz, dba, nw, sc, win):
    S = x.shape[0]
    tm = min(512, S)

    def body(dx1_ref, x_ref, dlx_ref, dly_ref, dqkv_ref, dz_ref, dba_ref, nw_ref, sc_ref, w_ref,
             dx_ref, dpb_ref, dnw_ref, dsc_ref, dsh_ref):
        i = pl.program_id(0)
        dpb = jnp.concatenate([dlx_ref[...], dly_ref[...], dqkv_ref[...], dz_ref[...], dba_ref[...]],
                              axis=-1).astype(BF16)
        dpb_ref[...] = dpb
        dh = _dot_nt(dpb, w_ref[...])
        dx, dnw, dsc, dsh = _norm_mod_bwd(dh, x_ref[...], nw_ref[...], sc_ref[...])
        dx_ref[...] = dx1_ref[...] + dx
        _acc(dnw_ref, dnw, i == 0)
        _acc(dsc_ref, dsc, i == 0)
        _acc(dsh_ref, dsh, i == 0)

    vec = _const((1, D_MODEL))
    return pl.pallas_call(
        body, name="proj_bwd", grid=(S // tm,),
        in_specs=[_row(tm, D_MODEL), _row(tm, D_MODEL), _row(tm, LRU_W), _row(tm, LRU_W), _row(tm, 3 * GDN_W),
                  _row(tm, GDN_W), _row(tm, LANES), vec, vec,
                  _const((D_MODEL, IN_PAD))],
        out_specs=[_row(tm, D_MODEL), _row(tm, IN_PAD), vec, vec, vec],
        out_shape=[_sds((S, D_MODEL)), _sds((S, IN_PAD), BF16), _sds((1, D_MODEL)), _sds((1, D_MODEL)),
                   _sds((1, D_MODEL))],
        compiler_params=_params(("arbitrary",)),
    )(dx1, x, dlx, dly, dqkv, dz, dba, nw, sc, win)


def _conv_taps(xx, w, tm):
    y = _roll_dn(xx, 3)[SUBLANES:] * w[0:1]
    y = y + _roll_dn(xx, 2)[SUBLANES:] * w[1:2]
    y = y + _roll_dn(xx, 1)[SUBLANES:] * w[2:3]
    y = y + xx[SUBLANES:] * w[3:4]
    return y


def _conv_fwd(src, col0, C, w8, b, act, name):
    S = src.shape[0]
    tm = min(512, S)
    tc = 512
    hb = tm // SUBLANES
    cb0 = col0 // tc

    def body(x_ref, p_ref, w_ref, b_ref, y_ref):
        i = pl.program_id(0)
        prev = jnp.where(i > 0, p_ref[...], 0.0)
        xx = jnp.concatenate([prev, x_ref[...]], axis=0)
        y = _conv_taps(xx, w_ref[...], tm) + b_ref[...]
        y_ref[...] = _silu(y) if act else y

    return pl.pallas_call(
        body, name=name, grid=(S // tm, C // tc),
        in_specs=[pl.BlockSpec((tm, tc), lambda i, j: (i, cb0 + j)),
                  pl.BlockSpec((SUBLANES, tc), lambda i, j: (jnp.maximum(i * hb - 1, 0), cb0 + j)),
                  pl.BlockSpec((SUBLANES, tc), lambda i, j: (0, j)),
                  pl.BlockSpec((1, tc), lambda i, j: (0, j))],
        out_specs=pl.BlockSpec((tm, tc), lambda i, j: (i, j)),
        out_shape=_sds((S, C)),
        compiler_params=_params(("arbitrary", "arbitrary")),
    )(src, src, w8, b)


def _conv_bwd(src, col0, C, w8, b, dyact, act, name):
    S = src.shape[0]
    tm = min(512, S)
    tc = 512
    hb = tm // SUBLANES
    nt = S // tm
    cb0 = col0 // tc
    last_hb = S // SUBLANES - 1

    def body(x_ref, p_ref, n_ref, dy_ref, dyn_ref, w_ref, b_ref, dx_ref, dw_ref, db_ref):
        i = pl.program_id(1)
        w = w_ref[...]
        prev = jnp.where(i > 0, p_ref[...], 0.0)
        xx = jnp.concatenate([prev, x_ref[...], n_ref[...]], axis=0)
        dy = jnp.concatenate([dy_ref[...], jnp.where(i < nt - 1, dyn_ref[...], 0.0)], axis=0)
        if act:
            ypre = _conv_taps(xx, w, tm + SUBLANES) + b_ref[...]
            dy = dy * _silu_grad(ypre)
        dx = dy[:tm] * w[3:4]
        for d in (1, 2, 3):
            dx = dx + _roll_up(dy, d)[:tm] * w[3 - d:4 - d]
        dx_ref[...] = dx
        xt = xx[:tm + SUBLANES]
        dyt = dy[:tm]
        rows = [_colsum(dyt * _roll_dn(xt, 3 - k)[SUBLANES:]) for k in range(4)]
        dw = jnp.concatenate(rows + [jnp.zeros((SUBLANES - 4, tc), F32)], axis=0)
        _acc(dw_ref, dw, i == 0)
        _acc(db_ref, _colsum(dyt), i == 0)

    return pl.pallas_call(
        body, name=name, grid=(C // tc, nt),
        in_specs=[pl.BlockSpec((tm, tc), lambda j, i: (i, cb0 + j)),
                  pl.BlockSpec((SUBLANES, tc), lambda j, i: (jnp.maximum(i * hb - 1, 0), cb0 + j)),
                  pl.BlockSpec((SUBLANES, tc), lambda j, i: (jnp.minimum((i + 1) * hb, last_hb), cb0 + j)),
                  pl.BlockSpec((tm, tc), lambda j, i: (i, j)),
                  pl.BlockSpec((SUBLANES, tc), lambda j, i: (jnp.minimum((i + 1) * hb, last_hb), j)),
                  pl.BlockSpec((SUBLANES, tc), lambda j, i: (0, j)),
                  pl.BlockSpec((1, tc), lambda j, i: (0, j))],
        out_specs=[pl.BlockSpec((tm, tc), lambda j, i: (i, j)),
                   pl.BlockSpec((SUBLANES, tc), lambda j, i: (0, j)),
                   pl.BlockSpec((1, tc), lambda j, i: (0, j))],
        out_shape=[_sds((S, C)), _sds((SUBLANES, C)), _sds((1, C))],
        compiler_params=_params(("arbitrary", "arbitrary")),
    )(src, src, src, dyact, dyact, w8, b)


def _lru_ab(pre_a, pre_x, xr, lam):
    r = jax.nn.sigmoid(pre_a)
    g = jax.nn.sigmoid(pre_x)
    log_sig = -_softplus(-lam)
    log_a = LRU_C * r * log_sig
    a = jnp.exp(log_a)
    t = jnp.tanh(log_a)
    mult = jnp.sqrt(jnp.maximum(-2.0 * t / (1.0 - t), 1e-12))
    return a, mult * (g * xr)


def _lru_tail(h, ly, lnw):
    xh, _ = _rms_parts(h * jax.nn.gelu(ly))
    return xh * lnw


def _scan_down(a, b):
    n = a.shape[0]
    row = lax.broadcasted_iota(jnp.int32, a.shape, 0)
    d = 1
    while d < n:
        keep = row >= d
        a_s = jnp.where(keep, _roll_dn(a, d), 1.0)
        b_s = jnp.where(keep, _roll_dn(b, d), 0.0)
        b = a * b_s + b
        a = a * a_s
        d *= 2
    return a, b


def _scan_up(a, b):
    n = a.shape[0]
    row = lax.broadcasted_iota(jnp.int32, a.shape, 0)
    d = 1
    while d < n:
        keep = row < n - d
        a_s = jnp.where(keep, _roll_up(a, d), 1.0)
        b_s = jnp.where(keep, _roll_up(b, d), 0.0)
        b = a * b_s + b
        a = a * a_s
        d *= 2
    return a, b


LRU_TM = 256


def _lru_fwd(xr, proj, wa, ba, wx, bx, lam, lnw):
    S = xr.shape[0]
    tm = min(LRU_TM, S)

    def body(xr_ref, ly_ref, wa_ref, ba_ref, wx_ref, bx_ref, lam_ref, lnw_ref, out_ref, h_ref, carry):
        i = pl.program_id(0)

        @pl.when(i == 0)
        def _():
            carry[...] = jnp.zeros_like(carry)

        x = xr_ref[...]
        xb = x.astype(BF16)
        pre_a = _dot(xb, wa_ref[...]) + ba_ref[...]
        pre_x = _dot(xb, wx_ref[...]) + bx_ref[...]
        a, b = _lru_ab(pre_a, pre_x, x, lam_ref[...])
        ca, hl = _scan_down(a, b)
        h = hl + ca * carry[0:1, :]
        carry[0:1, :] = h[tm - 1:tm, :]
        h_ref[...] = h
        out_ref[...] = _lru_tail(h, ly_ref[...], lnw_ref[...])

    vec = _const((1, LRU_W))
    mat = _const((LRU_W, LRU_W))
    return pl.pallas_call(
        body, name="lru_fwd", grid=(S // tm,),
        in_specs=[_row(tm, LRU_W), _row(tm, LRU_W, 1), mat, vec, mat, vec, vec, vec],
        out_specs=[_row(tm, LRU_W), _row(tm, LRU_W)],
        out_shape=[_sds((S, LRU_W)), _sds((S, LRU_W))],
        scratch_shapes=[pltpu.VMEM((SUBLANES, LRU_W), F32)],
        compiler_params=_params(("arbitrary",)),
    )(xr, proj, wa, ba, wx, bx, lam, lnw)


def _lru_bwd(dout, xr, proj, h, wa, ba, wx, bx, lam, lnw):
    S = xr.shape[0]
    tm = min(LRU_TM, S)
    nt = S // tm
    hb = tm // SUBLANES

    def rev(col=0):
        return pl.BlockSpec((tm, LRU_W), lambda i: (nt - 1 - i, col))

    def body(dout_ref, xr_ref, ly_ref, h_ref, hp_ref, wa_ref, ba_ref, wx_ref, bx_ref, lam_ref, lnw_ref,
             dxr_ref, dly_ref, dwa_ref, dba_ref, dwx_ref, dbx_ref, dlam_ref, dlnw_ref, carry):
        i = pl.program_id(0)
        first = i == 0

        @pl.when(first)
        def _():
            carry[...] = jnp.zeros_like(carry)

        x = xr_ref[...]
        xb = x.astype(BF16)
        pre_a = _dot(xb, wa_ref[...]) + ba_ref[...]
        pre_x = _dot(xb, wx_ref[...]) + bx_ref[...]
        (a, b), ab_vjp = jax.vjp(_lru_ab, pre_a, pre_x, x, lam_ref[...])
        h_t = h_ref[...]
        _, tail_vjp = jax.vjp(_lru_tail, h_t, ly_ref[...], lnw_ref[...])
        dh, dly, dlnw = tail_vjp(dout_ref[...])
        dly_ref[...] = dly
        row = lax.broadcasted_iota(jnp.int32, a.shape, 0)
        a_next = jnp.where(row == tm - 1, carry[0:1, :], _roll_up(a, 1))
        ca, gl = _scan_up(a_next, dh)
        g = gl + ca * carry[1:2, :]
        carry[0:1, :] = a[0:1, :]
        carry[1:2, :] = g[0:1, :]
        h_before = jnp.where(i == nt - 1, 0.0, hp_ref[SUBLANES - 1:SUBLANES, :])
        h_prev = jnp.where(row == 0, h_before, _roll_dn(h_t, 1))
        dpa, dpx, dx, dlam = ab_vjp((g * h_prev, g))
        dpab = dpa.astype(BF16)
        dpxb = dpx.astype(BF16)
        dxr_ref[...] = dx + _dot_nt(dpab, wa_ref[...]) + _dot_nt(dpxb, wx_ref[...])
        _acc(dwa_ref, _dot_tn(xb, dpab), first)
        _acc(dwx_ref, _dot_tn(xb, dpxb), first)
        _acc(dba_ref, _colsum(dpa), first)
        _acc(dbx_ref, _colsum(dpx), first)
        _acc(dlam_ref, dlam, first)
        _acc(dlnw_ref, dlnw, first)

    vec = _const((1, LRU_W))
    mat = _const((LRU_W, LRU_W))
    return pl.pallas_call(
        body, name="lru_bwd", grid=(nt,),
        in_specs=[rev(), rev(), rev(1), rev(),
                  pl.BlockSpec((SUBLANES, LRU_W), lambda i: (jnp.maximum((nt - 1 - i) * hb - 1, 0), 0)),
                  mat, vec, mat, vec, vec, vec],
        out_specs=[rev(), rev(), mat, vec, mat, vec, vec, vec],
        out_shape=[_sds((S, LRU_W)), _sds((S, LRU_W)), _sds((LRU_W, LRU_W)), _sds((1, LRU_W)),
                   _sds((LRU_W, LRU_W)), _sds((1, LRU_W)), _sds((1, LRU_W)), _sds((1, LRU_W))],
        scratch_shapes=[pltpu.VMEM((SUBLANES, LRU_W), F32)],
        compiler_params=_params(("arbitrary",)),
    )(dout, xr, proj, h, h, wa, ba, wx, bx, lam, lnw)


def _lane_pick(row_or_tile, lane):
    idx = lax.broadcasted_iota(jnp.int32, row_or_tile.shape, 1)
    return jnp.sum(jnp.where(idx == lane, row_or_tile, 0.0), axis=-1, keepdims=True)


def _unit_lower_inverses(los):
    n = los[0].shape[0]
    ri = lax.broadcasted_iota(jnp.int32, (n, n), 0)
    ci = lax.broadcasted_iota(jnp.int32, (n, n), 1)
    eye = (ri == ci).astype(F32)

    def lower_left_of(s):
        same_block = (ri & ~(2 * s - 1)) == (ci & ~(2 * s - 1))
        return same_block & ((ri & s) != 0) & ((ci & s) == 0)

    invs = [eye - jnp.where(lower_left_of(1), lo, 0.0) for lo in los]
    s = 2
    while s < n:
        lower_left = lower_left_of(s)
        left = [_mm_raw(inv, jnp.where(lower_left, lo, 0.0), "nn", 3) for inv, lo in zip(invs, los)]
        invs = [inv - _mm_raw(t, inv, "nn", 3) for inv, t in zip(invs, left)]
        s *= 2
    return invs


@jax.custom_vjp
def _unit_lower_inverses_diff(los):
    return _unit_lower_inverses(los)


def _unit_lower_inverses_fwd(los):
    invs = _unit_lower_inverses(los)
    return invs, invs


def _unit_lower_inverses_bwd(invs, cts):
    right = [_mm_raw(ct, inv, "nt", 3) for ct, inv in zip(cts, invs)]
    return ([-_mm_raw(inv, r, "tn", 3) for inv, r in zip(invs, right)],)


_unit_lower_inverses_diff.defvjp(_unit_lower_inverses_fwd, _unit_lower_inverses_bwd)


GDN_STEP_CHUNKS = 2


def _gdn_chunk(qs, ks, vs, bas, alog, dtb, states, inverses=_unit_lower_inverses, mm=_mm_raw):
    C = qs[0].shape[0]
    nchunks = len(bas)
    items = [(c, h) for c in range(nchunks) for h in range(HEADS)]
    ri = lax.broadcasted_iota(jnp.int32, (C, C), 0)
    ci = lax.broadcasted_iota(jnp.int32, (C, C), 1)
    causal = ri >= ci
    strict = ri > ci
    tri = causal.astype(F32)
    betas = [jax.nn.sigmoid(_lane_pick(bas[c], h)) for c, h in items]
    gs = [-jnp.exp(_lane_pick(alog, h)) * _softplus(_lane_pick(bas[c], h + HEADS) + _lane_pick(dtb, h))
          for c, h in items]
    qn = [q * lax.rsqrt(jnp.sum(q * q, axis=-1, keepdims=True) + 1e-6) * (HEAD_DIM ** -0.5) for q in qs]
    kn = [k * lax.rsqrt(jnp.sum(k * k, axis=-1, keepdims=True) + 1e-6) for k in ks]
    gc = [_hdot(tri, jnp.broadcast_to(g, (C, C))) for g in gs]
    decay = [jnp.where(causal, jnp.exp(jnp.where(causal, c - c.T, 0.0)), 0.0) for c in gc]
    eg = [jnp.exp(c) for c in gc]
    kb = [k * b for k, b in zip(kn, betas)]
    vb = [v * b for v, b in zip(vs, betas)]
    los = [jnp.where(strict, mm(a, k, "nt", 1) * d, 0.0) for a, k, d in zip(kb, kn, decay)]
    attn = [jnp.where(causal, mm(q, k, "nt", 1) * d, 0.0) for q, k, d in zip(qn, kn, decay)]
    tinv = inverses(los)
    u = [mm(t, x, "nn", 3) for t, x in zip(tinv, vb)]
    w = [mm(t, a * e, "nn", 3) for t, a, e in zip(tinv, kb, eg)]
    g_last = [c[C - 1:C, :] for c in gc]
    k_tail = [k * jnp.exp(gl - c) for k, gl, c in zip(kn, g_last, gc)]
    q_dec = [q * e for q, e in zip(qn, eg)]
    outs = []
    for c in range(nchunks):
        idx = range(c * HEADS, (c + 1) * HEADS)
        v_new = [u[i] - mm(w[i], s, "nn", 1) for i, s in zip(idx, states)]
        o_state = [mm(q_dec[i], s, "nn", 1) for i, s in zip(idx, states)]
        outs += [a + mm(attn[i], vn, "nn", 1) for i, a, vn in zip(idx, o_state, v_new)]
        states = [s * jnp.exp(g_last[i]) + mm(k_tail[i], vn, "tn", 1) for i, s, vn in zip(idx, states, v_new)]
    return outs, states


def _gdn_fwd(qkv, proj, alog, dtb, gather=()):
    S = qkv.shape[0]
    per = min(GDN_STEP_CHUNKS, S // CHUNK)
    T = per * CHUNK
    nc = S // T
    nk = len(gather)
    assert CHUNK == HEAD_DIM

    def body(*refs):
        q_ref, k_ref, v_ref, ba_ref, alog_ref, dtb_ref = refs[:6]
        o_ref, st_ref = refs[6 + nk:8 + nk]
        state = refs[8 + 2 * nk]
        if nk:
            start, finish = _gather_steps(refs[6:6 + nk], refs[8 + nk:8 + 2 * nk], *refs[9 + 2 * nk:])
            pl.when(pl.program_id(0) == 0)(start)

        @pl.when(pl.program_id(0) == 0)
        def _():
            state[...] = jnp.zeros_like(state)

        sls = [slice(hd * HEAD_DIM, (hd + 1) * HEAD_DIM) for hd in range(HEADS)]
        rows = [slice(c * CHUNK, (c + 1) * CHUNK) for c in range(per)]
        s0 = [state[hd] for hd in range(HEADS)]
        for hd in range(HEADS):
            st_ref[hd, 0] = s0[hd]
        o, s1 = _gdn_chunk([q_ref[r, sl] for r in rows for sl in sls], [k_ref[r, sl] for r in rows for sl in sls],
                           [v_ref[r, sl] for r in rows for sl in sls], [ba_ref[r, :] for r in rows],
                           alog_ref[...], dtb_ref[...], s0)
        for c, r in enumerate(rows):
            for hd in range(HEADS):
                o_ref[r, sls[hd]] = o[c * HEADS + hd]
        for hd in range(HEADS):
            state[hd] = s1[hd]
        if nk:
            pl.when(pl.program_id(0) == nc - 1)(finish)

    def col(j):
        return pl.BlockSpec((T, GDN_W),lambda n: (n, j))

    vec = _const((1, LANES))
    outs = pl.pallas_call(
        body, name="gdn_fwd", grid=(nc,),
        in_specs=[col(0), col(1), col(2), pl.BlockSpec((T, LANES),lambda n: (n, IN_PAD // LANES - 1)), vec, vec]
        + _hbm_specs(nk),
        out_specs=[col(0), pl.BlockSpec((HEADS, 1, HEAD_DIM, HEAD_DIM), lambda n: (0, n, 0, 0))] + _hbm_specs(nk),
        out_shape=[_sds((S, GDN_W)), _sds((HEADS, nc, HEAD_DIM, HEAD_DIM))] + (_gather_out_shapes(gather) if nk else []),
        scratch_shapes=[pltpu.VMEM((HEADS, HEAD_DIM, HEAD_DIM), F32)] + (_gather_scratch(nk) if nk else []),
        compiler_params=_params(("arbitrary",)),
    )(qkv, qkv, qkv, proj, alog, dtb, *gather)
    return outs[0], outs[1], list(outs[2:])


def _gdn_bwd(do, qkv, proj, states, alog, dtb, scatter=()):
    S = qkv.shape[0]
    per = min(GDN_STEP_CHUNKS, S // CHUNK)
    T = per * CHUNK
    nc = S // T
    nk = len(scatter)

    def body(*refs):
        do_ref, q_ref, k_ref, v_ref, ba_ref, st_ref, alog_ref, dtb_ref = refs[:8]
        dqkv_ref, dba_ref, dalog_ref, ddtb_ref = refs[8 + nk:12 + nk]
        dstate = refs[12 + 2 * nk]
        n = pl.program_id(0)
        if nk:
            start, finish = _scatter_steps(refs[8:8 + nk], refs[12 + nk:12 + 2 * nk], *refs[13 + 2 * nk:])
            pl.when(n == 0)(start)

        @pl.when(n == 0)
        def _():
            dstate[...] = jnp.zeros_like(dstate)

        sls = [slice(hd * HEAD_DIM, (hd + 1) * HEAD_DIM) for hd in range(HEADS)]
        rows = [slice(c * CHUNK, (c + 1) * CHUNK) for c in range(per)]
        fn = functools.partial(_gdn_chunk, inverses=_unit_lower_inverses_diff, mm=_mm)
        _, vjp = jax.vjp(fn, [q_ref[r, sl] for r in rows for sl in sls], [k_ref[r, sl] for r in rows for sl in sls],
                         [v_ref[r, sl] for r in rows for sl in sls], [ba_ref[r, :] for r in rows],
                         alog_ref[...], dtb_ref[...], [st_ref[hd, 0] for hd in range(HEADS)])
        dq, dk, dv, dba, dalog, ddtb, ds = vjp(([do_ref[r, sl] for r in rows for sl in sls],
                                                [dstate[hd] for hd in range(HEADS)]))
        for c, r in enumerate(rows):
            for hd in range(HEADS):
                i = c * HEADS + hd
                dqkv_ref[r, sls[hd]] = dq[i]
                dqkv_ref[r, GDN_W + hd * HEAD_DIM:GDN_W + (hd + 1) * HEAD_DIM] = dk[i]
                dqkv_ref[r, 2 * GDN_W + hd * HEAD_DIM:2 * GDN_W + (hd + 1) * HEAD_DIM] = dv[i]
            dba_ref[r, :] = dba[c]
        for hd in range(HEADS):
            dstate[hd] = ds[hd]
        _acc(dalog_ref, dalog, n == 0)
        _acc(ddtb_ref, ddtb, n == 0)
        if nk:
            pl.when(n == nc - 1)(finish)

    def col(j):
        return pl.BlockSpec((T, GDN_W),lambda n: (nc - 1 - n, j))

    vec = _const((1, LANES))
    outs = pl.pallas_call(
        body, name="gdn_bwd", grid=(nc,),
        in_specs=[col(0), col(0), col(1), col(2),
                  pl.BlockSpec((T, LANES),lambda n: (nc - 1 - n, IN_PAD // LANES - 1)),
                  pl.BlockSpec((HEADS, 1, HEAD_DIM, HEAD_DIM), lambda n: (0, nc - 1 - n, 0, 0)), vec, vec]
        + _hbm_specs(nk),
        out_specs=[pl.BlockSpec((T, 3 * GDN_W),lambda n: (nc - 1 - n, 0)),
                   pl.BlockSpec((T, LANES),lambda n: (nc - 1 - n, 0)), vec, vec] + _hbm_specs(nk),
        out_shape=[_sds((S, 3 * GDN_W)), _sds((S, LANES)), _sds((1, LANES)), _sds((1, LANES))]
        + [_sds(p.shape, p.dtype) for p in scatter],
        scratch_shapes=[pltpu.VMEM((HEADS, HEAD_DIM, HEAD_DIM), F32)] + (_scatter_scratch(nk) if nk else []),
        compiler_params=_params(("arbitrary",)),
    )(do, qkv, qkv, qkv, proj, states, alog, dtb, *scatter)
    return outs[0], outs[1], outs[2], outs[3], list(outs[4:])


def _gdn_gate(o, z, gnw):
    outs = []
    for hd in range(HEADS):
        sl = slice(hd * HEAD_DIM, (hd + 1) * HEAD_DIM)
        xh, _ = _rms_parts(o[:, sl])
        outs.append(xh * gnw * _silu(z[:, sl]))
    return jnp.concatenate(outs, axis=-1)


def _out_fwd(x, out_lru, o, proj, gnw, g1, wout):
    S = x.shape[0]
    tm = min(512, S)

    def body(x_ref, lru_ref, o_ref, z_ref, gnw_ref, g1_ref, w_ref, x1_ref, cat_ref):
        cat = jnp.concatenate([lru_ref[...], _gdn_gate(o_ref[...], z_ref[...], gnw_ref[...])], axis=-1).astype(BF16)
        cat_ref[...] = cat
        x1_ref[...] = x_ref[...] + g1_ref[...] * _dot(cat, w_ref[...])

    return pl.pallas_call(
        body, name="out_fwd", grid=(S // tm,),
        in_specs=[_row(tm, D_MODEL), _row(tm, LRU_W), _row(tm, GDN_W), _row(tm, GDN_W, 5), _const((1, LANES)),
                  _const((1, D_MODEL)), _const((D_MODEL, D_MODEL))],
        out_specs=[_row(tm, D_MODEL), _row(tm, D_MODEL)],
        out_shape=[_sds((S, D_MODEL)), _sds((S, D_MODEL), BF16)],
        compiler_params=_params(("arbitrary",)),
    )(x, out_lru, o, proj, gnw, g1, wout)


def _out_bwd(dx1, cat, o, proj, gnw, g1, wout):
    S = dx1.shape[0]
    tm = min(512, S)

    def body(dx1_ref, cat_ref, o_ref, z_ref, gnw_ref, g1_ref, w_ref,
             dlru_ref, do_ref, dz_ref, dmb_ref, dgnw_ref, dg1_ref):
        i = pl.program_id(0)
        d1 = dx1_ref[...]
        mix = _dot(cat_ref[...], w_ref[...])
        _acc(dg1_ref, _colsum(d1 * mix), i == 0)
        dmb = (d1 * g1_ref[...]).astype(BF16)
        dmb_ref[...] = dmb
        dcat = _dot_nt(dmb, w_ref[...])
        dlru_ref[...] = dcat[:, :LRU_W]
        _, vjp = jax.vjp(_gdn_gate, o_ref[...], z_ref[...], gnw_ref[...])
        do, dz, dgnw = vjp(dcat[:, LRU_W:])
        do_ref[...] = do
        dz_ref[...] = dz
        _acc(dgnw_ref, dgnw, i == 0)

    return pl.pallas_call(
        body, name="out_bwd", grid=(S // tm,),
        in_specs=[_row(tm, D_MODEL), _row(tm, D_MODEL), _row(tm, GDN_W), _row(tm, GDN_W, 5), _const((1, LANES)),
                  _const((1, D_MODEL)), _const((D_MODEL, D_MODEL))],
        out_specs=[_row(tm, LRU_W), _row(tm, GDN_W), _row(tm, GDN_W), _row(tm, D_MODEL), _const((1, LANES)),
                   _const((1, D_MODEL))],
        out_shape=[_sds((S, LRU_W)), _sds((S, GDN_W)), _sds((S, GDN_W)), _sds((S, D_MODEL), BF16), _sds((1, LANES)),
                   _sds((1, D_MODEL))],
        compiler_params=_params(("arbitrary",)),
    )(dx1, cat, o, proj, gnw, g1, wout)


MLP_TM = 256


def _load_once(step, pairs, sem):
    @pl.when(step == 0)
    def _():
        copies = [pltpu.make_async_copy(src, dst, sem.at[k]) for k, (src, dst) in enumerate(pairs)]
        for cp in copies:
            cp.start()
        for cp in copies:
            cp.wait()


def _mlp_fwd(x1, nw, sc, sh, g2, wup, wdown, gather=()):
    S = x1.shape[0]
    tm = min(MLP_TM, S)
    nt = S // tm
    nk = len(gather)

    def body(*refs):
        x_ref, nw_ref, sc_ref, sh_ref, g2_ref, wup_hbm, wdown_hbm = refs[:7]
        x2_ref = refs[7 + nk]
        wup, wdown, sem = refs[8 + 2 * nk:11 + 2 * nk]
        step = pl.program_id(0)
        if nk:
            start, finish = _gather_steps(refs[7:7 + nk], refs[8 + nk:8 + 2 * nk], *refs[11 + 2 * nk:])
            pl.when(step == 0)(start)
        _load_once(step, [(wup_hbm, wup), (wdown_hbm, wdown)], sem)
        x = x_ref[...]
        hb = _norm_mod(x, nw_ref[...], sc_ref[...], sh_ref[...]).astype(BF16)
        r = jnp.maximum(_dot(hb, wup[...]), 0.0)
        x2_ref[...] = x + g2_ref[...] * _dot((r * r).astype(BF16), wdown[...])
        if nk:
            pl.when(step == nt - 1)(finish)

    vec = _const((1, D_MODEL))
    anyspec = pl.BlockSpec(memory_space=pl.ANY)
    outs = pl.pallas_call(
        body, name="mlp_fwd", grid=(nt,),
        in_specs=[_row(tm, D_MODEL), vec, vec, vec, vec, anyspec, anyspec] + _hbm_specs(nk),
        out_specs=[_row(tm, D_MODEL)] + _hbm_specs(nk),
        out_shape=[_sds((S, D_MODEL))] + (_gather_out_shapes(gather) if nk else []),
        scratch_shapes=[pltpu.VMEM((D_MODEL, D_FF), BF16), pltpu.VMEM((D_FF, D_MODEL), BF16),
                        pltpu.SemaphoreType.DMA((2,))] + (_gather_scratch(nk) if nk else []),
        compiler_params=_params(("arbitrary",)),
    )(x1, nw, sc, sh, g2, wup, wdown, *gather)
    return outs[0], list(outs[1:])


def _mlp_bwd(dx2, x1, nw, sc, sh, g2, wup, wdown):
    S = x1.shape[0]
    tm = min(MLP_TM, S)

    def body(dx2_ref, x_ref, nw_ref, sc_ref, sh_ref, g2_ref, wup_hbm, wdown_hbm,
             dx1_ref, hb_ref, dupb_ref, actb_ref, ddb_ref, dnw_ref, dsc_ref, dsh_ref, dg2_ref, wup, wdown, sem):
        i = pl.program_id(0)
        _load_once(i, [(wup_hbm, wup), (wdown_hbm, wdown)], sem)
        x = x_ref[...]
        d2 = dx2_ref[...]
        hb = _norm_mod(x, nw_ref[...], sc_ref[...], sh_ref[...]).astype(BF16)
        hb_ref[...] = hb
        r = jnp.maximum(_dot(hb, wup[...]), 0.0)
        actb = (r * r).astype(BF16)
        actb_ref[...] = actb
        down = _dot(actb, wdown[...])
        _acc(dg2_ref, _colsum(d2 * down), i == 0)
        ddb = (d2 * g2_ref[...]).astype(BF16)
        ddb_ref[...] = ddb
        dupb = (_dot_nt(ddb, wdown[...]) * (2.0 * r)).astype(BF16)
        dupb_ref[...] = dupb
        dh = _dot_nt(dupb, wup[...])
        dx, dnw, dsc, dsh = _norm_mod_bwd(dh, x, nw_ref[...], sc_ref[...])
        dx1_ref[...] = d2 + dx
        _acc(dnw_ref, dnw, i == 0)
        _acc(dsc_ref, dsc, i == 0)
        _acc(dsh_ref, dsh, i == 0)

    vec = _const((1, D_MODEL))
    anyspec = pl.BlockSpec(memory_space=pl.ANY)
    return pl.pallas_call(
        body, name="mlp_bwd", grid=(S // tm,),
        in_specs=[_row(tm, D_MODEL), _row(tm, D_MODEL), vec, vec, vec, vec, anyspec, anyspec],
        out_specs=[_row(tm, D_MODEL), _row(tm, D_MODEL), _row(tm, D_FF), _row(tm, D_FF), _row(tm, D_MODEL),
                   vec, vec, vec, vec],
        out_shape=[_sds((S, D_MODEL)), _sds((S, D_MODEL), BF16), _sds((S, D_FF), BF16), _sds((S, D_FF), BF16),
                   _sds((S, D_MODEL), BF16), _sds((1, D_MODEL)), _sds((1, D_MODEL)), _sds((1, D_MODEL)),
                   _sds((1, D_MODEL))],
        scratch_shapes=[pltpu.VMEM((D_MODEL, D_FF), BF16), pltpu.VMEM((D_FF, D_MODEL), BF16),
                        pltpu.SemaphoreType.DMA((2,))],
        compiler_params=_params(("arbitrary",)),
    )(dx2, x1, nw, sc, sh, g2, wup, wdown)


def _matmul_tn(a, b, name, shards=None, out_dtype=F32):
    K, M = a.shape
    N = b.shape[1]
    tk = min(2048, K)
    if shards == "cols":
        tm, tn = M // 2, N // N_CHIPS
        out_spec = pl.BlockSpec((1, 1, tm, tn), lambda i, j, k: (i, j, 0, 0))
        out_shape = _sds((2, N_CHIPS, tm, tn))
    elif shards == "rows":
        h, tn = M // (2 * N_CHIPS), N
        tm = max(512, 2 * h)
        per_tile = tm // (2 * h)
        out_spec = pl.BlockSpec((2, per_tile, h, tn), lambda i, j, k: (0, i, 0, 0))
        out_shape = _sds((2, N_CHIPS, h, tn))
    else:
        tm = min(512, M)
        tn = 640 if N % 640 == 0 else min(1024, N)
        out_spec = pl.BlockSpec((tm, tn), lambda i, j, k: (i, j))
        out_shape = _sds((M, N))
    nk = K // tk

    def body(a_ref, b_ref, o_ref, acc):
        k = pl.program_id(2)
        _acc(acc, _dot_tn(a_ref[...], b_ref[...]), k == 0)

        @pl.when(k == nk - 1)
        def _():
            if shards == "rows":
                for s in range(per_tile):
                    for half in range(2):
                        r0 = (2 * s + half) * h
                        o_ref[half, s] = acc[r0:r0 + h, :].astype(o_ref.dtype)
            else:
                o_ref[...] = acc[...].reshape(o_ref.shape).astype(o_ref.dtype)

    return pl.pallas_call(
        body, name=name, grid=(M // tm, N // tn, nk),
        in_specs=[pl.BlockSpec((tk, tm), lambda i, j, k: (k, i)), pl.BlockSpec((tk, tn), lambda i, j, k: (k, j))],
        out_specs=out_spec, out_shape=_sds(out_shape.shape, out_dtype),
        scratch_shapes=[pltpu.VMEM((tm, tn), F32)],
        compiler_params=_params(("arbitrary", "arbitrary", "arbitrary")),
    )(a, b)


def _loss_head(x, target, fnw):
    S = x.shape[0]
    tm = min(512, S)

    def body(x_ref, t_ref, w_ref, dx_ref, loss_ref, dw_ref):
        i = pl.program_id(0)
        w = w_ref[...]
        xh, r = _rms_parts(x_ref[...])
        err = xh * w - t_ref[...]
        part = 0.5 * jnp.sum(jnp.mean(err * err, axis=-1, keepdims=True), axis=0, keepdims=True)
        _acc(loss_ref, jnp.broadcast_to(part, (SUBLANES, LANES)), i == 0)
        dx, dw = _rms_bwd(err * (1.0 / D_MODEL), xh, r, w)
        dx_ref[...] = dx
        _acc(dw_ref, dw, i == 0)

    vec = _const((1, D_MODEL))
    return pl.pallas_call(
        body, name="loss_head", grid=(S // tm,),
        in_specs=[_row(tm, D_MODEL), _row(tm, D_MODEL), vec],
        out_specs=[_row(tm, D_MODEL), _const((SUBLANES, LANES)), vec],
        out_shape=[_sds((S, D_MODEL)), _sds((SUBLANES, LANES)), _sds((1, D_MODEL))],
        compiler_params=_params(("arbitrary",)),
    )(x, target, fnw)


def _block_diag(w):
    eye = jnp.eye(LRU_BLOCKS, dtype=w.dtype)
    return (eye[:, None, :, None] * w[:, :, None, :]).reshape(LRU_W, LRU_W)


def _diag_blocks(m):
    m4 = m.reshape(LRU_BLOCKS, LRU_BLOCK, LRU_BLOCKS, LRU_BLOCK)
    return jnp.stack([m4[g, :, g, :] for g in range(LRU_BLOCKS)])


def _layer_fwd(x, p, mlp_shards=(), mlp_weights=None, next_shards=()):
    proj, h1b = _proj_fwd(x, p["nmw"], p["sc1"], p["sh1"], p["win"])
    xr = _conv_fwd(proj, 0, LRU_W, p["lcw"], p["lcb"], False, "conv_lru_fwd")
    out_lru, h = _lru_fwd(xr, proj, p["wa"].astype(BF16), p["ba"], p["wx"].astype(BF16), p["bx"], p["lam"], p["lnw"])
    qkv = _conv_fwd(proj, 2 * LRU_W, 3 * GDN_W, p["gcw"], p["gcb"], True, "conv_gdn_fwd")
    o, states, gathered = _gdn_fwd(qkv, proj, p["alog"], p["dtb"], mlp_shards)
    if mlp_weights is not None:
        p = {**p, **mlp_weights(gathered)}
    x1, cat = _out_fwd(x, out_lru, o, proj, p["gnw"], p["g1"], p["wout"])
    x2, gathered_next = _mlp_fwd(x1, p["nmlp"], p["sc2"], p["sh2"], p["g2"], p["wup"], p["wdown"], next_shards)
    res = dict(x=x, proj=proj, h1b=h1b, xr=xr, h=h, qkv=qkv, o=o, states=states, x1=x1, cat=cat)
    return x2, res, p, gathered_next


def _layer_bwd(dx2, p, r, sharded=False, scatter=(), early=None):
    dx1, h2b, dupb, actb, ddb, dnmlp, dsc2, dsh2, dg2 = _mlp_bwd(
        dx2, r["x1"], p["nmlp"], p["sc2"], p["sh2"], p["g2"], p["wup"], p["wdown"])
    gdt = BF16 if sharded else F32
    g_wup = _matmul_tn(h2b, dupb, "dw_up", "cols" if sharded else None, gdt)
    g_wdown = _matmul_tn(actb, ddb, "dw_down", "rows" if sharded else None, gdt)
    dlru, do, dz, dmb, dgnw, dg1 = _out_bwd(dx1, r["cat"], r["o"], r["proj"], p["gnw"], p["g1"], p["wout"])
    g_wout = _matmul_tn(r["cat"], dmb, "dw_out", "rows" if sharded else None, gdt)
    if early is not None:
        scatter = list(scatter) + early([g_wout, g_wup, g_wdown])
    dqkv_act, dba, dalog, ddtb, arrived = _gdn_bwd(do, r["qkv"], r["proj"], r["states"], p["alog"], p["dtb"], scatter)
    dqkv, dgcw, _ = _conv_bwd(r["proj"], 2 * LRU_W, 3 * GDN_W, p["gcw"], p["gcb"], dqkv_act, True, "conv_gdn_bwd")
    wab = p["wa"].astype(BF16)
    wxb = p["wx"].astype(BF16)
    dxr, dly, dwa, dba_, dwx, dbx, dlam, dlnw = _lru_bwd(
        dlru, r["xr"], r["proj"], r["h"], wab, p["ba"], wxb, p["bx"], p["lam"], p["lnw"])
    dlx, dlcw, dlcb = _conv_bwd(r["proj"], 0, LRU_W, p["lcw"], p["lcb"], dxr, False, "conv_lru_bwd")
    dx, dpb, dnmw, dsc1, dsh1 = _proj_bwd(dx1, r["x"], dlx, dly, dqkv, dz, dba, p["nmw"], p["sc1"], p["win"])
    g_win = _matmul_tn(r["h1b"], dpb, "dw_in", None, gdt)
    grads = dict(nmw=dnmw, nmlp=dnmlp, sh1=dsh1, sc1=dsc1, g1=dg1, sh2=dsh2, sc2=dsc2, g2=dg2,
                 win=g_win, lcw=dlcw, lcb=dlcb, wa=dwa, ba=dba_, wx=dwx, bx=dbx, lam=dlam, lnw=dlnw,
                 gcw=dgcw, alog=dalog, dtb=ddtb, gnw=dgnw, wout=g_wout, wup=g_wup, wdown=g_wdown)
    return dx, grads, arrived


def _local_step(x, target, fnw, layers):
    res = []
    for p in layers:
        x, r, _, _ = _layer_fwd(x, p)
        res.append(r)
    dx, loss_blk, dfnw = _loss_head(x, target, fnw)
    grads = [None] * len(layers)
    for l in reversed(range(len(layers))):
        dx, grads[l], _ = _layer_bwd(dx, layers[l], res[l])
    stacked = {k: jnp.stack([g[k] for g in grads]) for k in grads[0]}
    return loss_blk[0, 0], dx, dfnw, stacked


def _prep_layers(norm_mix_w, norm_mlp_w, mod, win_b, lru_conv_w, lru_conv_b, gate_a_w, gate_a_b, gate_x_w, gate_x_b,
                 lru_lambda, lru_norm_w, gdn_conv_w, gdn_a_log, gdn_dt_bias, gdn_norm_w, wout_b, wup_b, wdown_b):
    L = norm_mix_w.shape[0]

    def vec(a):
        return a.reshape(L, 1, -1)

    def lanes(a):
        return jnp.pad(a, ((0, 0), (0, LANES - a.shape[1]))).reshape(L, 1, LANES)

    def taps(w):
        return jnp.pad(w, ((0, 0), (0, SUBLANES - w.shape[1]), (0, 0)))

    m = mod.reshape(L, N_MOD, 1, D_MODEL)
    return dict(
        nmw=vec(norm_mix_w), nmlp=vec(norm_mlp_w),
        sh1=m[:, 0], sc1=m[:, 1], g1=m[:, 2], sh2=m[:, 3], sc2=m[:, 4], g2=m[:, 5],
        win=win_b, lcw=taps(lru_conv_w), lcb=vec(lru_conv_b),
        wa=jax.vmap(_block_diag)(gate_a_w), ba=vec(gate_a_b), wx=jax.vmap(_block_diag)(gate_x_w), bx=vec(gate_x_b),
        lam=vec(lru_lambda), lnw=vec(lru_norm_w),
        gcw=taps(gdn_conv_w), gcb=jnp.zeros((L, 1, 3 * GDN_W), F32),
        alog=lanes(gdn_a_log), dtb=lanes(gdn_dt_bias), gnw=vec(gdn_norm_w),
        wout=wout_b, wup=wup_b, wdown=wdown_b)


def _position():
    x, y, c = lax.axis_index("x"), lax.axis_index("y"), lax.axis_index("c")
    return x, y, c


def _other_chips(x, y):
    return [(1 - x, y), (x, 1 - y), (1 - x, 1 - y)]


def _all_gather_rows(block, name):
    m, n = block.shape

    def body(x_ref, out_ref, send_sems, recv_sems, local_sem):
        x, y, c = _position()
        me, sibling = (x, y, c), (x, y, 1 - c)
        chips = _other_chips(x, y)

        def rows(px, py, pc):
            return out_ref.at[pl.ds((4 * px + 2 * py + pc) * m, m), :]

        def copy(k, blk, to, src=None):
            return pltpu.make_async_remote_copy(
                src_ref=rows(*blk) if src is None else src, dst_ref=rows(*blk),
                send_sem=send_sems.at[k], recv_sem=recv_sems.at[k], device_id=to, device_id_type=MESH)

        mine = pltpu.make_async_copy(x_ref, rows(*me), local_sem)
        mine.start()
        first = [copy(0, me, sibling, src=x_ref)]
        first += [copy(1 + j, me, (*chip, c), src=x_ref) for j, chip in enumerate(chips)]
        for cp in first:
            cp.start()
        passed = [copy(4 + j, (*chip, c), sibling) for j, chip in enumerate(chips)]
        for j, chip in enumerate(chips):
            copy(1 + j, (*chip, c), me).wait_recv()
            passed[j].start()
        copy(0, sibling, me).wait_recv()
        for j, chip in enumerate(chips):
            copy(4 + j, (*chip, 1 - c), me).wait_recv()
        for cp in first + passed:
            cp.wait_send()
        mine.wait()

    return pl.pallas_call(
        body, name=name,
        out_shape=_sds((N_DEV * m, n)),
        in_specs=[pl.BlockSpec(memory_space=pltpu.VMEM)],
        out_specs=pl.BlockSpec(memory_space=pltpu.VMEM),
        scratch_shapes=[pltpu.SemaphoreType.DMA((7,)), pltpu.SemaphoreType.DMA((7,)), pltpu.SemaphoreType.DMA],
        compiler_params=pltpu.CompilerParams(vmem_limit_bytes=VMEM_LIMIT),
    )(block)


def _hbm_specs(n):
    return [pl.BlockSpec(memory_space=pl.ANY)] * n


def _gather_chips(shards, name):
    n = len(shards)

    def body(*refs):
        start, finish = _gather_steps(refs[:n], refs[n:2 * n], *refs[2 * n:])
        start()
        finish()

    return pl.pallas_call(
        body, name=name,
        out_shape=_gather_out_shapes(shards), in_specs=_hbm_specs(n), out_specs=_hbm_specs(n),
        scratch_shapes=_gather_scratch(n),
    )(*shards)


def _gather_out_shapes(shards):
    return [_sds((N_CHIPS, 2, s.shape[0] // 2, s.shape[1]), s.dtype) for s in shards]


def _gather_scratch(n):
    return [pltpu.SemaphoreType.DMA((6 * n,)), pltpu.SemaphoreType.DMA((6 * n,))]


def _gather_steps(ins, outs, send_sems, recv_sems):
    n = len(ins)
    x, y, c = _position()
    chips = _other_chips(x, y)
    me = 2 * x + y

    def first(a, j, slot):
        h = ins[a].shape[0] // 2
        return pltpu.make_async_remote_copy(
            src_ref=ins[a].at[pl.ds(pl.multiple_of(c * h, SUBLANES), h)], dst_ref=outs[a].at[slot, c],
            send_sem=send_sems.at[3 * a + j], recv_sem=recv_sems.at[3 * a + j],
            device_id=(chips[j][0], chips[j][1], c), device_id_type=MESH)

    def second(a, j, half):
        slot = 2 * chips[j][0] + chips[j][1]
        return pltpu.make_async_remote_copy(
            src_ref=outs[a].at[slot, c], dst_ref=outs[a].at[slot, half],
            send_sem=send_sems.at[3 * (n + a) + j], recv_sem=recv_sems.at[3 * (n + a) + j],
            device_id=(x, y, 1 - c), device_id_type=MESH)

    def start():
        for a in range(n):
            for j in range(3):
                first(a, j, me).start()

    def finish():
        for a in range(n):
            for j, (px, py) in enumerate(chips):
                first(a, j, 2 * px + py).wait_recv()
                second(a, j, c).start()
        for a in range(n):
            for j in range(3):
                second(a, j, 1 - c).wait_recv()
        for a in range(n):
            for j in range(3):
                first(a, j, me).wait_send()
                second(a, j, c).wait_send()

    return start, finish


def _send_to_sibling(parts, name):
    n = len(parts)

    def body(*refs):
        ins, outs = refs[:n], refs[n:2 * n]
        send_sems, recv_sems = refs[2 * n:]
        x, y, c = _position()
        copies = [pltpu.make_async_remote_copy(
            src_ref=ins[a].at[1 - c], dst_ref=outs[a], send_sem=send_sems.at[a], recv_sem=recv_sems.at[a],
            device_id=(x, y, 1 - c), device_id_type=MESH) for a in range(n)]
        for cp in copies:
            cp.start()
        for cp in copies:
            cp.wait()

    return pl.pallas_call(
        body, name=name,
        out_shape=[_sds(p.shape[1:], p.dtype) for p in parts],
        in_specs=_hbm_specs(n), out_specs=_hbm_specs(n),
        scratch_shapes=[pltpu.SemaphoreType.DMA((n,)), pltpu.SemaphoreType.DMA((n,))],
    )(*parts)


def _scatter_chips(parts, name):
    n = len(parts)

    def body(*refs):
        start, finish = _scatter_steps(refs[:n], refs[n:2 * n], *refs[2 * n:])
        start()
        finish()

    return pl.pallas_call(
        body, name=name,
        out_shape=[_sds(p.shape, p.dtype) for p in parts], in_specs=_hbm_specs(n), out_specs=_hbm_specs(n),
        scratch_shapes=_scatter_scratch(n),
    )(*parts)


def _scatter_scratch(n):
    return [pltpu.SemaphoreType.DMA((3 * n,)), pltpu.SemaphoreType.DMA((3 * n,))]


def _scatter_steps(ins, outs, send_sems, recv_sems):
    n = len(ins)
    x, y, c = _position()
    chips = _other_chips(x, y)
    me = 2 * x + y

    def copy(a, j, src_slot, dst_slot):
        px, py = chips[j]
        return pltpu.make_async_remote_copy(
            src_ref=ins[a].at[src_slot], dst_ref=outs[a].at[dst_slot], send_sem=send_sems.at[3 * a + j],
            recv_sem=recv_sems.at[3 * a + j], device_id=(px, py, c), device_id_type=MESH)

    def start():
        for a in range(n):
            for j in range(3):
                copy(a, j, 2 * chips[j][0] + chips[j][1], me).start()

    def finish():
        for a in range(n):
            for j, (px, py) in enumerate(chips):
                copy(a, j, me, 2 * px + py).wait_recv()
        for a in range(n):
            for j in range(3):
                copy(a, j, 2 * chips[j][0] + chips[j][1], me).wait_send()

    return start, finish


def _swap_row_halves(arrays, name):
    n = len(arrays)

    def body(*refs):
        outs = refs[n:2 * n]
        send_sems, recv_sems = refs[2 * n:]
        x, y, c = _position()

        def copy(a, half):
            h = outs[a].shape[0] // 2
            rows = outs[a].at[pl.ds(pl.multiple_of(half * h, SUBLANES), h)]
            return pltpu.make_async_remote_copy(
                src_ref=rows, dst_ref=rows, send_sem=send_sems.at[a], recv_sem=recv_sems.at[a],
                device_id=(x, y, 1 - c), device_id_type=MESH)

        sends = [copy(a, c) for a in range(n)]
        for cp in sends:
            cp.start()
        for a in range(n):
            copy(a, 1 - c).wait_recv()
        for cp in sends:
            cp.wait_send()

    return pl.pallas_call(
        body, name=name,
        out_shape=[_sds(a.shape, a.dtype) for a in arrays],
        in_specs=_hbm_specs(n), out_specs=_hbm_specs(n),
        input_output_aliases={a: a for a in range(n)},
        scratch_shapes=[pltpu.SemaphoreType.DMA((n,)), pltpu.SemaphoreType.DMA((n,))],
    )(*arrays)


def _row_tile(rows):
    for t in (512, 256, 128, 64, 32, 16, 8):
        if rows % t == 0:
            return t
    return rows


def _sum_slots(buf, name):
    k, rows, cols = buf.shape
    tm = _row_tile(rows)

    def body(b_ref, o_ref):
        s = b_ref[0]
        for i in range(1, k):
            s = s + b_ref[i]
        o_ref[...] = s

    return pl.pallas_call(
        body, name=name, grid=(rows // tm,),
        in_specs=[pl.BlockSpec((k, tm, cols), lambda i: (0, i, 0))],
        out_specs=pl.BlockSpec((tm, cols), lambda i: (i, 0)),
        out_shape=_sds((rows, cols)),
        compiler_params=_params(("arbitrary",)),
    )(buf)


def _pair_add(part, from_sibling, core, name):
    _, k, h, cols = part.shape
    rows = k * h
    tm = _row_tile(rows)

    def body(core_ref, a_ref, b_ref, o_ref):
        o_ref[...] = (a_ref[0].astype(F32) + b_ref[...].astype(F32)).astype(BF16)

    out = pl.pallas_call(
        body, name=name,
        grid_spec=pltpu.PrefetchScalarGridSpec(
            num_scalar_prefetch=1, grid=(rows // tm,),
            in_specs=[pl.BlockSpec((1, tm, cols), lambda i, cr: (cr[0], i, 0)),
                      pl.BlockSpec((tm, cols), lambda i, cr: (i, 0))],
            out_specs=pl.BlockSpec((tm, cols), lambda i, cr: (i, 0))),
        out_shape=_sds((rows, cols), BF16),
        compiler_params=_params(("arbitrary",)),
    )(core, part.reshape(2, rows, cols), from_sibling.reshape(rows, cols))
    return out.reshape(k, h, cols)


def _chip_sum(arrived, own, place, name):
    _, h, cols = arrived.shape
    tm = min(256, h)
    nb = h // tm

    def body(place_ref, arr_ref, own_ref, g_ref):
        for chip in range(N_CHIPS):
            @pl.when(place_ref[1] == chip)
            def _():
                terms = [own_ref[0] if j == chip else arr_ref[j] for j in range(N_CHIPS)]
                g = terms[0].astype(F32)
                for t in terms[1:]:
                    g = g + t.astype(F32)
                g_ref[...] = g

    return pl.pallas_call(
        body, name=name,
        grid_spec=pltpu.PrefetchScalarGridSpec(
            num_scalar_prefetch=1, grid=(nb,),
            in_specs=[pl.BlockSpec((N_CHIPS, tm, cols), lambda i, pr: (0, i, 0)),
                      pl.BlockSpec((1, tm, cols), lambda i, pr: (pr[1], i, 0))],
            out_specs=pl.BlockSpec((tm, cols), lambda i, pr: (pr[0] * nb + i, 0))),
        out_shape=_sds((2 * h, cols)),
        compiler_params=_params(("arbitrary",)),
    )(place, arrived, own)


def _adam_layer(g, w, m, v, outs, layer, name):
    rows, cols = g.shape
    tm = _row_tile(rows)

    def body(g_ref, w_ref, m_ref, v_ref, *refs):
        og_ref, od_ref, om_ref, ov_ref = refs[4:]
        gr = g_ref[...]
        og_ref[0] = gr
        d, nm, nv = _adam_math(w_ref[0], gr, m_ref[0], v_ref[0])
        od_ref[0] = d
        om_ref[0] = nm
        ov_ref[0] = nv

    slab = pl.BlockSpec((1, tm, cols), lambda i: (layer, i, 0))
    return pl.pallas_call(
        body, name=name, grid=(rows // tm,),
        in_specs=[pl.BlockSpec((tm, cols), lambda i: (i, 0)), slab, slab, slab] + _hbm_specs(4),
        out_specs=[slab] * 4, out_shape=[_sds(o.shape) for o in outs],
        input_output_aliases={4 + i: i for i in range(4)},
        compiler_params=_params(("arbitrary",)),
    )(g, w, m, v, *outs)


def _adam_math(w, g, m, v):
    m = ADAM_B1 * m + (1.0 - ADAM_B1) * g
    v = ADAM_B2 * v + (1.0 - ADAM_B2) * jnp.square(g)
    m_hat = m / (1.0 - ADAM_B1 ** ADAM_STEP)
    v_hat = v / (1.0 - ADAM_B2 ** ADAM_STEP)
    delta = -ADAM_LR * (m_hat / (jnp.sqrt(v_hat) + ADAM_EPS) + ADAM_WD * w)
    return delta, m, v


def _adam(w, g, m, v, name):
    rows, cols = w.shape
    tm = _row_tile(rows)

    def body(w_ref, g_ref, m_ref, v_ref, d_ref, nm_ref, nv_ref):
        d, nm, nv = _adam_math(w_ref[...], g_ref[...], m_ref[...], v_ref[...])
        d_ref[...] = d
        nm_ref[...] = nm
        nv_ref[...] = nv

    spec = pl.BlockSpec((tm, cols), lambda i: (i, 0))
    return pl.pallas_call(
        body, name=name, grid=(rows // tm,), in_specs=[spec] * 4, out_specs=[spec] * 3,
        out_shape=[_sds((rows, cols))] * 3, compiler_params=_params(("arbitrary",)),
    )(w, g, m, v)


def _mod_fwd(c_all, w_mod, b_mod_cols):
    L, _, n = w_mod.shape

    def body(c_ref, w_ref, b_ref, o_ref):
        o_ref[0] = _hdot(_silu(c_ref[...]), w_ref[0]) + b_ref[0]

    return pl.pallas_call(
        body, name="mod_fwd", grid=(L,),
        in_specs=[_const((N_DEV, D_MODEL)), pl.BlockSpec((1, D_MODEL, n), lambda l: (l, 0, 0)),
                  pl.BlockSpec((1, 1, n), lambda l: (l, 0, 0))],
        out_specs=pl.BlockSpec((1, N_DEV, n), lambda l: (l, 0, 0)),
        out_shape=_sds((L, N_DEV, n)),
        compiler_params=_params(("arbitrary",)),
    )(c_all, w_mod, b_mod_cols)


def _mod_update(c_all, dmod, w, m, v):
    L, _, n = w.shape
    tn = 512

    def body(c_ref, d_ref, w_ref, m_ref, v_ref, g_ref, dl_ref, nm_ref, nv_ref):
        g = _hdot_tn(_silu(c_ref[...]), d_ref[0])
        g_ref[0] = g
        d, nm, nv = _adam_math(w_ref[0], g, m_ref[0], v_ref[0])
        dl_ref[0] = d
        nm_ref[0] = nm
        nv_ref[0] = nv

    big = pl.BlockSpec((1, D_MODEL, tn), lambda l, j: (l, 0, j))
    return pl.pallas_call(
        body, name="mod_update", grid=(L, n // tn),
        in_specs=[_const((N_DEV, D_MODEL)), pl.BlockSpec((1, N_DEV, tn), lambda l, j: (l, 0, j)), big, big, big],
        out_specs=[big] * 4, out_shape=[_sds(w.shape)] * 4,
        compiler_params=_params(("arbitrary", "arbitrary")),
    )(c_all, dmod, w, m, v)


def _pack_rows(parts, row_multiple):
    flat = jnp.concatenate([p.reshape(-1) for p in parts])
    unit = row_multiple * LANES
    flat = jnp.pad(flat, (0, (-flat.shape[0]) % unit))
    return flat.reshape(-1, LANES)


def _unpack(packed, shapes):
    flat = packed.reshape(-1)
    out, off = [], 0
    for s in shapes:
        n = 1
        for d in s:
            n *= d
        out.append(flat[off:off + n].reshape(s))
        off += n
    return out


def _lane_pad(a):
    return jnp.pad(a, ((0, 0), (0, LANES - a.shape[1])))


WEIGHT_NAMES = ("norm_mix_w", "norm_mlp_w", "w_mod", "b_mod", "w_in", "lru_conv_w", "lru_conv_b", "lru_gate_a_w",
                "lru_gate_a_b", "lru_gate_x_w", "lru_gate_x_b", "lru_lambda", "lru_norm_w", "gdn_conv_w", "gdn_a_log",
                "gdn_dt_bias", "gdn_norm_w", "w_out", "w_up", "w_down", "final_norm_w")


def kernel(x, c, norm_mix_w, norm_mlp_w, w_mod, b_mod, w_in, lru_conv_w, lru_conv_b, lru_gate_a_w, lru_gate_a_b, lru_gate_x_w, lru_gate_x_b, lru_lambda, lru_norm_w, gdn_conv_w, gdn_a_log, gdn_dt_bias, gdn_norm_w, w_out, w_up, w_down, final_norm_w, loss_target, m_norm_mix_w, m_norm_mlp_w, m_w_mod, m_b_mod, m_w_in, m_lru_conv_w, m_lru_conv_b, m_lru_gate_a_w, m_lru_gate_a_b, m_lru_gate_x_w, m_lru_gate_x_b, m_lru_lambda, m_lru_norm_w, m_gdn_conv_w, m_gdn_a_log, m_gdn_dt_bias, m_gdn_norm_w, m_w_out, m_w_up, m_w_down, m_final_norm_w, v_norm_mix_w, v_norm_mlp_w, v_w_mod, v_b_mod, v_w_in, v_lru_conv_w, v_lru_conv_b, v_lru_gate_a_w, v_lru_gate_a_b, v_lru_gate_x_w, v_lru_gate_x_b, v_lru_lambda, v_lru_norm_w, v_gdn_conv_w, v_gdn_a_log, v_gdn_dt_bias, v_gdn_norm_w, v_w_out, v_w_up, v_w_down, v_final_norm_w):
    W = dict(zip(WEIGHT_NAMES, (norm_mix_w, norm_mlp_w, w_mod, b_mod, w_in, lru_conv_w, lru_conv_b, lru_gate_a_w,
                                lru_gate_a_b, lru_gate_x_w, lru_gate_x_b, lru_lambda, lru_norm_w, gdn_conv_w, gdn_a_log,
                                gdn_dt_bias, gdn_norm_w, w_out, w_up, w_down, final_norm_w)))
    M = dict(zip(WEIGHT_NAMES, (m_norm_mix_w, m_norm_mlp_w, m_w_mod, m_b_mod, m_w_in, m_lru_conv_w, m_lru_conv_b,
                                m_lru_gate_a_w, m_lru_gate_a_b, m_lru_gate_x_w, m_lru_gate_x_b, m_lru_lambda,
                                m_lru_norm_w, m_gdn_conv_w, m_gdn_a_log, m_gdn_dt_bias, m_gdn_norm_w, m_w_out, m_w_up,
                                m_w_down, m_final_norm_w)))
    V = dict(zip(WEIGHT_NAMES, (v_norm_mix_w, v_norm_mlp_w, v_w_mod, v_b_mod, v_w_in, v_lru_conv_w, v_lru_conv_b,
                                v_lru_gate_a_w, v_lru_gate_a_b, v_lru_gate_x_w, v_lru_gate_x_b, v_lru_lambda,
                                v_lru_norm_w, v_gdn_conv_w, v_gdn_a_log, v_gdn_dt_bias, v_gdn_norm_w, v_w_out, v_w_up,
                                v_w_down, v_final_norm_w)))
    L = DEPTH
    xi, yi, ci = _position()
    chip = 2 * xi + yi
    dev = 2 * chip + ci
    lcs = LRU_W // N_CHIPS
    gcs = 3 * GDN_W // N_CHIPS
    mcs = N_MOD * D_MODEL // N_CHIPS

    g_in = _all_gather_rows(_pack_rows([c, lru_conv_w, gdn_conv_w], SUBLANES), "gather_small_inputs").reshape(N_DEV, -1)
    c_all = g_in[:, :D_MODEL]
    per_chip = g_in[0::2]
    o1 = D_MODEL + L * 4 * lcs
    lcw_full = per_chip[:, D_MODEL:o1].reshape(N_CHIPS, L, 4, lcs).transpose(1, 2, 0, 3).reshape(L, 4, LRU_W)
    gcw_full = per_chip[:, o1:o1 + L * 4 * gcs].reshape(N_CHIPS, L, 4, gcs).transpose(1, 2, 0, 3).reshape(L, 4, 3 * GDN_W)

    b_cols = lax.dynamic_slice(b_mod, (0, chip * mcs), (L, mcs)).reshape(L, 1, mcs)
    modp = _mod_fwd(c_all, w_mod, b_cols)
    g_mod = _all_gather_rows(modp.reshape(L * N_DEV, mcs), "gather_mod").reshape(N_DEV, L, N_DEV, mcs)
    mod = lax.dynamic_index_in_dim(g_mod[0::2], dev, axis=2, keepdims=False).transpose(1, 0, 2).reshape(L, N_MOD * D_MODEL)

    stacked = _prep_layers(norm_mix_w, norm_mlp_w, mod, None, lcw_full, lru_conv_b, lru_gate_a_w, lru_gate_a_b,
                           lru_gate_x_w, lru_gate_x_b, lru_lambda, lru_norm_w, gcw_full, gdn_a_log, gdn_dt_bias,
                           gdn_norm_w, None, None, None)
    shards = [[w_in[l].astype(BF16), w_out[l].astype(BF16), w_up[l].astype(BF16), w_down[l].astype(BF16)]
              for l in range(L)]

    def with_own(gathered, own):
        return [lax.dynamic_update_slice(got, o.reshape((1,) + got.shape[1:]), (chip, 0, 0, 0)).reshape(
            (N_CHIPS,) + o.shape) for got, o in zip(gathered, own)]

    def mixer_weights(l, gathered):
        win_g, wout_g = with_own(gathered, shards[l][:2])
        return dict(win=jnp.pad(win_g.transpose(1, 0, 2).reshape(D_MODEL, IN_COLS), ((0, 0), (0, IN_PAD - IN_COLS))),
                    wout=wout_g.reshape(D_MODEL, D_MODEL))

    def mlp_weights(l, gathered):
        wup_g, wdown_g = with_own(gathered, shards[l][2:])
        return dict(wup=wup_g.transpose(1, 0, 2).reshape(D_MODEL, D_FF), wdown=wdown_g.reshape(D_FF, D_MODEL))

    mixer = mixer_weights(0, _gather_chips(shards[0][:2], "gather_weights"))
    layers = []
    xs = x[0]
    res = []
    for l in range(L):
        p = {k: v[l] for k, v in stacked.items() if v is not None}
        p.update(mixer)
        xs, r, p, gathered = _layer_fwd(xs, p, shards[l][2:], functools.partial(mlp_weights, l),
                                        shards[l + 1][:2] if l + 1 < L else ())
        res.append(r)
        layers.append(p)
        if l + 1 < L:
            mixer = mixer_weights(l + 1, gathered)
    dx, loss_blk, dfnw = _loss_head(xs, loss_target[0], final_norm_w.reshape(1, D_MODEL))
    loss_local = loss_blk[0, 0]

    big_names = ["w_in", "w_out", "w_up", "w_down"]
    core = jnp.reshape(ci, (1,)).astype(jnp.int32)
    place = jnp.stack([ci, chip]).astype(jnp.int32)
    layer_grads = [None] * L

    big = {nm: [lax.empty(W[nm].shape, F32) for _ in range(4)] for nm in big_names}

    def pair_sums(names, parts):
        from_sibling = _send_to_sibling(parts, "pair_send")
        return [_pair_add(p, r, core, "pair_add_" + nm) for nm, p, r in zip(names, parts, from_sibling)]

    def apply_update(l, names, arrived, pair):
        halves = [_chip_sum(a, own, place, "chip_sum_" + nm) for nm, a, own in zip(names, arrived, pair)]
        full = _swap_row_halves(halves, "pair_swap")
        for nm, gr in zip(names, full):
            big[nm] = _adam_layer(gr, W[nm], M[nm], V[nm], big[nm], l, "adam_" + nm)

    win_pair = []
    for l in reversed(range(L)):
        early_pair = []

        def early(parts, early_pair=early_pair):
            early_pair.extend(pair_sums(big_names[1:], parts))
            return early_pair

        dx, gl, arrived = _layer_bwd(dx, layers[l], res[l], sharded=True, scatter=win_pair, early=early)
        if win_pair:
            apply_update(l + 1, big_names[:1], arrived[:1], win_pair)
        apply_update(l, big_names[1:], arrived[len(win_pair):], early_pair)
        layer_grads[l] = gl
        gwin = gl["win"][:, :IN_COLS].reshape(2, D_MODEL // 2, N_CHIPS, IN_COLS // N_CHIPS).transpose(0, 2, 1, 3)
        win_pair = pair_sums(big_names[:1], [gwin])
    apply_update(0, big_names[:1], _scatter_chips(win_pair, "chip_scatter"), win_pair)
    small_keys = [k for k in layer_grads[0] if k not in ("win", "wout", "wup", "wdown")]
    g = {k: jnp.stack([gl[k] for gl in layer_grads]) for k in small_keys}
    loss = lax.psum(loss_local, ("x", "y", "c"))

    dmod = jnp.concatenate([g["sh1"], g["sc1"], g["g1"], g["sh2"], g["sc2"], g["g2"]], axis=-1)
    small = [dmod, g["nmw"], g["nmlp"], g["lcw"][:, :4], g["lcb"], jax.vmap(_diag_blocks)(g["wa"]), g["ba"],
             jax.vmap(_diag_blocks)(g["wx"]), g["bx"], g["lam"], g["lnw"], g["gcw"][:, :4], g["alog"], g["dtb"],
             g["gnw"], dfnw]
    small_shapes = [(L, N_MOD * D_MODEL), (L, D_MODEL), (L, D_MODEL), (L, 4, LRU_W), (L, LRU_W),
                    (L, LRU_BLOCKS, LRU_BLOCK, LRU_BLOCK), (L, LRU_W), (L, LRU_BLOCKS, LRU_BLOCK, LRU_BLOCK),
                    (L, LRU_W), (L, LRU_W), (L, LRU_W), (L, 4, 3 * GDN_W), (L, LANES), (L, LANES), (L, LANES),
                    (D_MODEL,)]
    small_names = ["b_mod", "norm_mix_w", "norm_mlp_w", None, "lru_conv_b", "lru_gate_a_w", "lru_gate_a_b",
                   "lru_gate_x_w", "lru_gate_x_b", "lru_lambda", "lru_norm_w", None, "gdn_a_log", "gdn_dt_bias",
                   "gdn_norm_w", "final_norm_w"]
    pack_g = _pack_rows(small, 512)
    rows = pack_g.shape[0]
    all_g = _all_gather_rows(pack_g, "gather_small_grads").reshape(N_DEV, rows, LANES)
    tot = _sum_slots(all_g, "sum_small_grads")
    tot_parts = _unpack(tot, small_shapes)

    def pack_state(S_):
        parts = []
        for nm, shp in zip(small_names, small_shapes):
            if nm is None:
                parts.append(jnp.zeros(shp, F32))
            elif nm in ("gdn_a_log", "gdn_dt_bias"):
                parts.append(_lane_pad(S_[nm]))
            else:
                parts.append(S_[nm])
        return _pack_rows(parts, 512)

    upd = _adam(pack_state(W), tot, pack_state(M), pack_state(V), "adam_small")
    upd_parts = [_unpack(u, small_shapes) for u in upd]

    grads, deltas, new_m, new_v = {}, {}, {}, {}
    for k, nm in enumerate(small_names):
        if nm is None:
            continue
        cut = (lambda a: a[:, :HEADS]) if nm in ("gdn_a_log", "gdn_dt_bias") else (lambda a: a)
        grads[nm] = cut(tot_parts[k])
        deltas[nm], new_m[nm], new_v[nm] = (cut(u[k]) for u in upd_parts)

    g_lcw = lax.dynamic_slice(tot_parts[3], (0, 0, chip * lcs), (L, 4, lcs))
    g_gcw = lax.dynamic_slice(tot_parts[11], (0, 0, chip * gcs), (L, 4, gcs))
    conv_shapes = [(L, 4, lcs), (L, 4, gcs)]
    conv_pack = lambda a, b: _pack_rows([a, b], SUBLANES)
    cu = _adam(conv_pack(lru_conv_w, gdn_conv_w), conv_pack(g_lcw, g_gcw), conv_pack(m_lru_conv_w, m_gdn_conv_w),
               conv_pack(v_lru_conv_w, v_gdn_conv_w), "adam_conv")
    cu_parts = [_unpack(u, conv_shapes) for u in cu]
    for k, nm in enumerate(("lru_conv_w", "gdn_conv_w")):
        grads[nm] = (g_lcw, g_gcw)[k]
        deltas[nm], new_m[nm], new_v[nm] = (u[k] for u in cu_parts)

    dmod_all = all_g[:, :L * N_MOD * D_MODEL // LANES].reshape(N_DEV, L, N_MOD * D_MODEL)
    dmod_cols = lax.dynamic_slice(dmod_all, (0, 0, chip * mcs), (N_DEV, L, mcs)).transpose(1, 0, 2)
    grads["w_mod"], deltas["w_mod"], new_m["w_mod"], new_v["w_mod"] = _mod_update(c_all, dmod_cols, w_mod, m_w_mod, v_w_mod)

    for nm in big_names:
        grads[nm], deltas[nm], new_m[nm], new_v[nm] = big[nm]

    out = [loss, dx[None]]
    for group in (grads, deltas, new_m, new_v):
        out += [group[nm].reshape(W[nm].shape) for nm in WEIGHT_NAMES]
    return tuple(out)
```

```python
import functools

import jax
import jax.numpy as jnp
from jax import lax
from jax.experimental import pallas as pl
from jax.experimental.pallas import tpu as pltpu

F32 = jnp.float32
BF16 = jnp.bfloat16
MESH = pl.DeviceIdType.MESH

D_MODEL = 1024
DEPTH = 4
LRU_W = 512
LRU_BLOCKS = 8
LRU_BLOCK = 64
LRU_C = 8.0
HEADS = 4
HEAD_DIM = 128
GDN_W = 512
CHUNK = 128
D_FF = 4096
N_MOD = 6
IN_COLS = 3080
IN_PAD = 3200
NORM_EPS = 1e-6
LANES = 128
SUBLANES = 8
N_DEV = 8
N_CHIPS = 4

ADAM_LR = 0.001
ADAM_B1 = 0.9
ADAM_B2 = 0.999
ADAM_EPS = 1e-08
ADAM_WD = 0.01
ADAM_STEP = 10

VMEM_LIMIT = 56 * 1024 * 1024
HI = lax.Precision.HIGHEST


def _sds(shape, dtype=F32):
    return jax.ShapeDtypeStruct(tuple(shape), dtype)


def _params(sem=None, vmem=VMEM_LIMIT):
    return pltpu.CompilerParams(dimension_semantics=sem, vmem_limit_bytes=vmem)


def _const(shape):
    return pl.BlockSpec(tuple(shape), lambda *_: (0,) * len(shape))


def _row(tm, c, col=0):
    return pl.BlockSpec((tm, c), lambda i: (i, col))


def _dot(a, b):
    return jnp.dot(a, b, preferred_element_type=F32)


def _dot_nt(a, b):
    return lax.dot_general(a, b, (((1,), (1,)), ((), ())), preferred_element_type=F32)


def _dot_tn(a, b):
    return lax.dot_general(a, b, (((0,), (0,)), ((), ())), preferred_element_type=F32)


def _hdot(a, b):
    return jnp.dot(a, b, preferred_element_type=F32, precision=HI)


def _hdot_nt(a, b):
    return lax.dot_general(a, b, (((1,), (1,)), ((), ())), preferred_element_type=F32, precision=HI)


def _hdot_tn(a, b):
    return lax.dot_general(a, b, (((0,), (0,)), ((), ())), preferred_element_type=F32, precision=HI)


_DIMS = {"nn": (((1,), (0,)), ((), ())), "nt": (((1,), (1,)), ((), ())), "tn": (((0,), (0,)), ((), ()))}


def _mm_raw(a, b, dims, passes):
    dn = _DIMS[dims]

    def dot(p, q):
        return lax.dot_general(p, q, dn, preferred_element_type=F32)

    a_hi = a.astype(BF16)
    b_hi = b.astype(BF16)
    if passes == 1:
        return dot(a_hi, b_hi)
    a_lo = (a - a_hi.astype(F32)).astype(BF16)
    b_lo = (b - b_hi.astype(F32)).astype(BF16)
    return dot(a_hi, b_hi) + (dot(a_hi, b_lo) + dot(a_lo, b_hi))


@functools.partial(jax.custom_vjp, nondiff_argnums=(2, 3))
def _mm(a, b, dims, passes):
    return _mm_raw(a, b, dims, passes)


def _mm_fwd(a, b, dims, passes):
    return _mm_raw(a, b, dims, passes), (a, b)


def _mm_bwd(dims, passes, res, ct):
    a, b = res
    if dims == "nn":
        return _mm_raw(ct, b, "nt", passes), _mm_raw(a, ct, "tn", passes)
    if dims == "nt":
        return _mm_raw(ct, b, "nn", passes), _mm_raw(ct, a, "tn", passes)
    return _mm_raw(b, ct, "nt", passes), _mm_raw(a, ct, "nn", passes)


_mm.defvjp(_mm_fwd, _mm_bwd)


def _acc(ref, val, first):
    @pl.when(first)
    def _():
        ref[...] = val

    @pl.when(jnp.logical_not(first))
    def _():
        ref[...] += val


def _colsum(v):
    return jnp.sum(v, axis=0, keepdims=True)


def _rms_parts(x):
    r = lax.rsqrt(jnp.mean(x * x, axis=-1, keepdims=True) + NORM_EPS)
    return x * r, r


def _rms_bwd(dy, xh, r, w):
    dxh = dy * w
    dw = _colsum(dy * xh)
    dx = r * (dxh - xh * jnp.mean(dxh * xh, axis=-1, keepdims=True))
    return dx, dw


def _norm_mod(x, w, sc, sh):
    xh, _ = _rms_parts(x)
    return (xh * w) * (1.0 + sc) + sh


def _norm_mod_bwd(dy, x, w, sc):
    xh, r = _rms_parts(x)
    n = xh * w
    dsh = _colsum(dy)
    dsc = _colsum(dy * n)
    dx, dw = _rms_bwd(dy * (1.0 + sc), xh, r, w)
    return dx, dw, dsc, dsh


def _softplus(x):
    return jnp.maximum(x, 0.0) + jnp.log1p(jnp.exp(-jnp.abs(x)))


def _silu(x):
    return x * jax.nn.sigmoid(x)


def _silu_grad(x):
    s = jax.nn.sigmoid(x)
    return s * (1.0 + x * (1.0 - s))


def _roll_dn(x, d):
    return x if d == 0 else pltpu.roll(x, d, 0)


def _roll_up(x, d):
    return x if d == 0 else pltpu.roll(x, x.shape[0] - d, 0)


def _proj_fwd(x, nw, sc, sh, win):
    S = x.shape[0]
    tm = min(512, S)

    def body(x_ref, nw_ref, sc_ref, sh_ref, w_ref, proj_ref, hb_ref):
        hb = _norm_mod(x_ref[...], nw_ref[...], sc_ref[...], sh_ref[...]).astype(BF16)
        hb_ref[...] = hb
        proj_ref[...] = _dot(hb, w_ref[...])

    vec = _const((1, D_MODEL))
    return pl.pallas_call(
        body, name="proj_fwd", grid=(S // tm,),
        in_specs=[_row(tm, D_MODEL), vec, vec, vec, _const((D_MODEL, IN_PAD))],
        out_specs=[_row(tm, IN_PAD), _row(tm, D_MODEL)],
        out_shape=[_sds((S, IN_PAD)), _sds((S, D_MODEL), BF16)],
        compiler_params=_params(("arbitrary",)),
    )(x, nw, sc, sh, win)


def _proj_bwd(dx1, x, dlx, dly, dqkv, dz, dba, nw, sc, win):
    S = x.shape[0]
    tm = min(512, S)

    def body(dx1_ref, x_ref, dlx_ref, dly_ref, dqkv_ref, dz_ref, dba_ref, nw_ref, sc_ref, w_ref,
             dx_ref, dpb_ref, dnw_ref, dsc_ref, dsh_ref):
        i = pl.program_id(0)
        dpb = jnp.concatenate([dlx_ref[...], dly_ref[...], dqkv_ref[...], dz_ref[...], dba_ref[...]],
                              axis=-1).astype(BF16)
        dpb_ref[...] = dpb
        dh = _dot_nt(dpb, w_ref[...])
        dx, dnw, dsc, dsh = _norm_mod_bwd(dh, x_ref[...], nw_ref[...], sc_ref[...])
        dx_ref[...] = dx1_ref[...] + dx
        _acc(dnw_ref, dnw, i == 0)
        _acc(dsc_ref, dsc, i == 0)
        _acc(dsh_ref, dsh, i == 0)

    vec = _const((1, D_MODEL))
    return pl.pallas_call(
        body, name="proj_bwd", grid=(S // tm,),
        in_specs=[_row(tm, D_MODEL), _row(tm, D_MODEL), _row(tm, LRU_W), _row(tm, LRU_W), _row(tm, 3 * GDN_W),
                  _row(tm, GDN_W), _row(tm, LANES), vec, vec,
                  _const((D_MODEL, IN_PAD))],
        out_specs=[_row(tm, D_MODEL), _row(tm, IN_PAD), vec, vec, vec],
        out_shape=[_sds((S, D_MODEL)), _sds((S, IN_PAD), BF16), _sds((1, D_MODEL)), _sds((1, D_MODEL)),
                   _sds((1, D_MODEL))],
        compiler_params=_params(("arbitrary",)),
    )(dx1, x, dlx, dly, dqkv, dz, dba, nw, sc, win)


def _conv_taps(xx, w, tm):
    y = _roll_dn(xx, 3)[SUBLANES:] * w[0:1]
    y = y + _roll_dn(xx, 2)[SUBLANES:] * w[1:2]
    y = y + _roll_dn(xx, 1)[SUBLANES:] * w[2:3]
    y = y + xx[SUBLANES:] * w[3:4]
    return y


def _conv_fwd(src, col0, C, w8, b, act, name):
    S = src.shape[0]
    tm = min(512, S)
    tc = 512
    hb = tm // SUBLANES
    cb0 = col0 // tc

    def body(x_ref, p_ref, w_ref, b_ref, y_ref):
        i = pl.program_id(0)
        prev = jnp.where(i > 0, p_ref[...], 0.0)
        xx = jnp.concatenate([prev, x_ref[...]], axis=0)
        y = _conv_taps(xx, w_ref[...], tm) + b_ref[...]
        y_ref[...] = _silu(y) if act else y

    return pl.pallas_call(
        body, name=name, grid=(S // tm, C // tc),
        in_specs=[pl.BlockSpec((tm, tc), lambda i, j: (i, cb0 + j)),
                  pl.BlockSpec((SUBLANES, tc), lambda i, j: (jnp.maximum(i * hb - 1, 0), cb0 + j)),
                  pl.BlockSpec((SUBLANES, tc), lambda i, j: (0, j)),
                  pl.BlockSpec((1, tc), lambda i, j: (0, j))],
        out_specs=pl.BlockSpec((tm, tc), lambda i, j: (i, j)),
        out_shape=_sds((S, C)),
        compiler_params=_params(("arbitrary", "arbitrary")),
    )(src, src, w8, b)


def _conv_bwd(src, col0, C, w8, b, dyact, act, name):
    S = src.shape[0]
    tm = min(512, S)
    tc = 512
    hb = tm // SUBLANES
    nt = S // tm
    cb0 = col0 // tc
    last_hb = S // SUBLANES - 1

    def body(x_ref, p_ref, n_ref, dy_ref, dyn_ref, w_ref, b_ref, dx_ref, dw_ref, db_ref):
        i = pl.program_id(1)
        w = w_ref[...]
        prev = jnp.where(i > 0, p_ref[...], 0.0)
        xx = jnp.concatenate([prev, x_ref[...], n_ref[...]], axis=0)
        dy = jnp.concatenate([dy_ref[...], jnp.where(i < nt - 1, dyn_ref[...], 0.0)], axis=0)
        if act:
            ypre = _conv_taps(xx, w, tm + SUBLANES) + b_ref[...]
            dy = dy * _silu_grad(ypre)
        dx = dy[:tm] * w[3:4]
        for d in (1, 2, 3):
            dx = dx + _roll_up(dy, d)[:tm] * w[3 - d:4 - d]
        dx_ref[...] = dx
        xt = xx[:tm + SUBLANES]
        dyt = dy[:tm]
        rows = [_colsum(dyt * _roll_dn(xt, 3 - k)[SUBLANES:]) for k in range(4)]
        dw = jnp.concatenate(rows + [jnp.zeros((SUBLANES - 4, tc), F32)], axis=0)
        _acc(dw_ref, dw, i == 0)
        _acc(db_ref, _colsum(dyt), i == 0)

    return pl.pallas_call(
        body, name=name, grid=(C // tc, nt),
        in_specs=[pl.BlockSpec((tm, tc), lambda j, i: (i, cb0 + j)),
                  pl.BlockSpec((SUBLANES, tc), lambda j, i: (jnp.maximum(i * hb - 1, 0), cb0 + j)),
                  pl.BlockSpec((SUBLANES, tc), lambda j, i: (jnp.minimum((i + 1) * hb, last_hb), cb0 + j)),
                  pl.BlockSpec((tm, tc), lambda j, i: (i, j)),
                  pl.BlockSpec((SUBLANES, tc), lambda j, i: (jnp.minimum((i + 1) * hb, last_hb), j)),
                  pl.BlockSpec((SUBLANES, tc), lambda j, i: (0, j)),
                  pl.BlockSpec((1, tc), lambda j, i: (0, j))],
        out_specs=[pl.BlockSpec((tm, tc), lambda j, i: (i, j)),
                   pl.BlockSpec((SUBLANES, tc), lambda j, i: (0, j)),
                   pl.BlockSpec((1, tc), lambda j, i: (0, j))],
        out_shape=[_sds((S, C)), _sds((SUBLANES, C)), _sds((1, C))],
        compiler_params=_params(("arbitrary", "arbitrary")),
    )(src, src, src, dyact, dyact, w8, b)


def _lru_ab(pre_a, pre_x, xr, lam):
    r = jax.nn.sigmoid(pre_a)
    g = jax.nn.sigmoid(pre_x)
    log_sig = -_softplus(-lam)
    log_a = LRU_C * r * log_sig
    a = jnp.exp(log_a)
    t = jnp.tanh(log_a)
    mult = jnp.sqrt(jnp.maximum(-2.0 * t / (1.0 - t), 1e-12))
    return a, mult * (g * xr)


def _lru_tail(h, ly, lnw):
    xh, _ = _rms_parts(h * jax.nn.gelu(ly))
    return xh * lnw


def _scan_down(a, b):
    n = a.shape[0]
    row = lax.broadcasted_iota(jnp.int32, a.shape, 0)
    d = 1
    while d < n:
        keep = row >= d
        a_s = jnp.where(keep, _roll_dn(a, d), 1.0)
        b_s = jnp.where(keep, _roll_dn(b, d), 0.0)
        b = a * b_s + b
        a = a * a_s
        d *= 2
    return a, b


def _scan_up(a, b):
    n = a.shape[0]
    row = lax.broadcasted_iota(jnp.int32, a.shape, 0)
    d = 1
    while d < n:
        keep = row < n - d
        a_s = jnp.where(keep, _roll_up(a, d), 1.0)
        b_s = jnp.where(keep, _roll_up(b, d), 0.0)
        b = a * b_s + b
        a = a * a_s
        d *= 2
    return a, b


LRU_TM = 256


def _lru_fwd(xr, proj, wa, ba, wx, bx, lam, lnw):
    S = xr.shape[0]
    tm = min(LRU_TM, S)

    def body(xr_ref, ly_ref, wa_ref, ba_ref, wx_ref, bx_ref, lam_ref, lnw_ref, out_ref, h_ref, carry):
        i = pl.program_id(0)

        @pl.when(i == 0)
        def _():
            carry[...] = jnp.zeros_like(carry)

        x = xr_ref[...]
        xb = x.astype(BF16)
        pre_a = _dot(xb, wa_ref[...]) + ba_ref[...]
        pre_x = _dot(xb, wx_ref[...]) + bx_ref[...]
        a, b = _lru_ab(pre_a, pre_x, x, lam_ref[...])
        ca, hl = _scan_down(a, b)
        h = hl + ca * carry[0:1, :]
        carry[0:1, :] = h[tm - 1:tm, :]
        h_ref[...] = h
        out_ref[...] = _lru_tail(h, ly_ref[...], lnw_ref[...])

    vec = _const((1, LRU_W))
    mat = _const((LRU_W, LRU_W))
    return pl.pallas_call(
        body, name="lru_fwd", grid=(S // tm,),
        in_specs=[_row(tm, LRU_W), _row(tm, LRU_W, 1), mat, vec, mat, vec, vec, vec],
        out_specs=[_row(tm, LRU_W), _row(tm, LRU_W)],
        out_shape=[_sds((S, LRU_W)), _sds((S, LRU_W))],
        scratch_shapes=[pltpu.VMEM((SUBLANES, LRU_W), F32)],
        compiler_params=_params(("arbitrary",)),
    )(xr, proj, wa, ba, wx, bx, lam, lnw)


def _lru_bwd(dout, xr, proj, h, wa, ba, wx, bx, lam, lnw):
    S = xr.shape[0]
    tm = min(LRU_TM, S)
    nt = S // tm
    hb = tm // SUBLANES

    def rev(col=0):
        return pl.BlockSpec((tm, LRU_W), lambda i: (nt - 1 - i, col))

    def body(dout_ref, xr_ref, ly_ref, h_ref, hp_ref, wa_ref, ba_ref, wx_ref, bx_ref, lam_ref, lnw_ref,
             dxr_ref, dly_ref, dwa_ref, dba_ref, dwx_ref, dbx_ref, dlam_ref, dlnw_ref, carry):
        i = pl.program_id(0)
        first = i == 0

        @pl.when(first)
        def _():
            carry[...] = jnp.zeros_like(carry)

        x = xr_ref[...]
        xb = x.astype(BF16)
        pre_a = _dot(xb, wa_ref[...]) + ba_ref[...]
        pre_x = _dot(xb, wx_ref[...]) + bx_ref[...]
        (a, b), ab_vjp = jax.vjp(_lru_ab, pre_a, pre_x, x, lam_ref[...])
        h_t = h_ref[...]
        _, tail_vjp = jax.vjp(_lru_tail, h_t, ly_ref[...], lnw_ref[...])
        dh, dly, dlnw = tail_vjp(dout_ref[...])
        dly_ref[...] = dly
        row = lax.broadcasted_iota(jnp.int32, a.shape, 0)
        a_next = jnp.where(row == tm - 1, carry[0:1, :], _roll_up(a, 1))
        ca, gl = _scan_up(a_next, dh)
        g = gl + ca * carry[1:2, :]
        carry[0:1, :] = a[0:1, :]
        carry[1:2, :] = g[0:1, :]
        h_before = jnp.where(i == nt - 1, 0.0, hp_ref[SUBLANES - 1:SUBLANES, :])
        h_prev = jnp.where(row == 0, h_before, _roll_dn(h_t, 1))
        dpa, dpx, dx, dlam = ab_vjp((g * h_prev, g))
        dpab = dpa.astype(BF16)
        dpxb = dpx.astype(BF16)
        dxr_ref[...] = dx + _dot_nt(dpab, wa_ref[...]) + _dot_nt(dpxb, wx_ref[...])
        _acc(dwa_ref, _dot_tn(xb, dpab), first)
        _acc(dwx_ref, _dot_tn(xb, dpxb), first)
        _acc(dba_ref, _colsum(dpa), first)
        _acc(dbx_ref, _colsum(dpx), first)
        _acc(dlam_ref, dlam, first)
        _acc(dlnw_ref, dlnw, first)

    vec = _const((1, LRU_W))
    mat = _const((LRU_W, LRU_W))
    return pl.pallas_call(
        body, name="lru_bwd", grid=(nt,),
        in_specs=[rev(), rev(), rev(1), rev(),
                  pl.BlockSpec((SUBLANES, LRU_W), lambda i: (jnp.maximum((nt - 1 - i) * hb - 1, 0), 0)),
                  mat, vec, mat, vec, vec, vec],
        out_specs=[rev(), rev(), mat, vec, mat, vec, vec, vec],
        out_shape=[_sds((S, LRU_W)), _sds((S, LRU_W)), _sds((LRU_W, LRU_W)), _sds((1, LRU_W)),
                   _sds((LRU_W, LRU_W)), _sds((1, LRU_W)), _sds((1, LRU_W)), _sds((1, LRU_W))],
        scratch_shapes=[pltpu.VMEM((SUBLANES, LRU_W), F32)],
        compiler_params=_params(("arbitrary",)),
    )(dout, xr, proj, h, h, wa, ba, wx, bx, lam, lnw)


def _lane_pick(row_or_tile, lane):
    idx = lax.broadcasted_iota(jnp.int32, row_or_tile.shape, 1)
    return jnp.sum(jnp.where(idx == lane, row_or_tile, 0.0), axis=-1, keepdims=True)


def _unit_lower_inverses(los):
    n = los[0].shape[0]
    ri = lax.broadcasted_iota(jnp.int32, (n, n), 0)
    ci = lax.broadcasted_iota(jnp.int32, (n, n), 1)
    eye = (ri == ci).astype(F32)

    def lower_left_of(s):
        same_block = (ri & ~(2 * s - 1)) == (ci & ~(2 * s - 1))
        return same_block & ((ri & s) != 0) & ((ci & s) == 0)

    invs = [eye - jnp.where(lower_left_of(1), lo, 0.0) for lo in los]
    s = 2
    while s < n:
        lower_left = lower_left_of(s)
        left = [_mm_raw(inv, jnp.where(lower_left, lo, 0.0), "nn", 3) for inv, lo in zip(invs, los)]
        invs = [inv - _mm_raw(t, inv, "nn", 3) for inv, t in zip(invs, left)]
        s *= 2
    return invs


@jax.custom_vjp
def _unit_lower_inverses_diff(los):
    return _unit_lower_inverses(los)


def _unit_lower_inverses_fwd(los):
    invs = _unit_lower_inverses(los)
    return invs, invs


def _unit_lower_inverses_bwd(invs, cts):
    right = [_mm_raw(ct, inv, "nt", 3) for ct, inv in zip(cts, invs)]
    return ([-_mm_raw(inv, r, "tn", 3) for inv, r in zip(invs, right)],)


_unit_lower_inverses_diff.defvjp(_unit_lower_inverses_fwd, _unit_lower_inverses_bwd)


GDN_STEP_CHUNKS = 2


def _gdn_chunk(qs, ks, vs, bas, alog, dtb, states, inverses=_unit_lower_inverses, mm=_mm_raw):
    C = qs[0].shape[0]
    nchunks = len(bas)
    items = [(c, h) for c in range(nchunks) for h in range(HEADS)]
    ri = lax.broadcasted_iota(jnp.int32, (C, C), 0)
    ci = lax.broadcasted_iota(jnp.int32, (C, C), 1)
    causal = ri >= ci
    strict = ri > ci
    tri = causal.astype(F32)
    betas = [jax.nn.sigmoid(_lane_pick(bas[c], h)) for c, h in items]
    gs = [-jnp.exp(_lane_pick(alog, h)) * _softplus(_lane_pick(bas[c], h + HEADS) + _lane_pick(dtb, h))
          for c, h in items]
    qn = [q * lax.rsqrt(jnp.sum(q * q, axis=-1, keepdims=True) + 1e-6) * (HEAD_DIM ** -0.5) for q in qs]
    kn = [k * lax.rsqrt(jnp.sum(k * k, axis=-1, keepdims=True) + 1e-6) for k in ks]
    gc = [_hdot(tri, jnp.broadcast_to(g, (C, C))) for g in gs]
    decay = [jnp.where(causal, jnp.exp(jnp.where(causal, c - c.T, 0.0)), 0.0) for c in gc]
    eg = [jnp.exp(c) for c in gc]
    kb = [k * b for k, b in zip(kn, betas)]
    vb = [v * b for v, b in zip(vs, betas)]
    los = [jnp.where(strict, mm(a, k, "nt", 1) * d, 0.0) for a, k, d in zip(kb, kn, decay)]
    attn = [jnp.where(causal, mm(q, k, "nt", 1) * d, 0.0) for q, k, d in zip(qn, kn, decay)]
    tinv = inverses(los)
    u = [mm(t, x, "nn", 3) for t, x in zip(tinv, vb)]
    w = [mm(t, a * e, "nn", 3) for t, a, e in zip(tinv, kb, eg)]
    g_last = [c[C - 1:C, :] for c in gc]
    k_tail = [k * jnp.exp(gl - c) for k, gl, c in zip(kn, g_last, gc)]
    q_dec = [q * e for q, e in zip(qn, eg)]
    outs = []
    for c in range(nchunks):
        idx = range(c * HEADS, (c + 1) * HEADS)
        v_new = [u[i] - mm(w[i], s, "nn", 1) for i, s in zip(idx, states)]
        o_state = [mm(q_dec[i], s, "nn", 1) for i, s in zip(idx, states)]
        outs += [a + mm(attn[i], vn, "nn", 1) for i, a, vn in zip(idx, o_state, v_new)]
        states = [s * jnp.exp(g_last[i]) + mm(k_tail[i], vn, "tn", 1) for i, s, vn in zip(idx, states, v_new)]
    return outs, states


def _gdn_fwd(qkv, proj, alog, dtb, gather=()):
    S = qkv.shape[0]
    per = min(GDN_STEP_CHUNKS, S // CHUNK)
    T = per * CHUNK
    nc = S // T
    nk = len(gather)
    assert CHUNK == HEAD_DIM

    def body(*refs):
        q_ref, k_ref, v_ref, ba_ref, alog_ref, dtb_ref = refs[:6]
        o_ref, st_ref = refs[6 + nk:8 + nk]
        state = refs[8 + 2 * nk]
        if nk:
            start, finish = _gather_steps(refs[6:6 + nk], refs[8 + nk:8 + 2 * nk], *refs[9 + 2 * nk:])
            pl.when(pl.program_id(0) == 0)(start)

        @pl.when(pl.program_id(0) == 0)
        def _():
            state[...] = jnp.zeros_like(state)

        sls = [slice(hd * HEAD_DIM, (hd + 1) * HEAD_DIM) for hd in range(HEADS)]
        rows = [slice(c * CHUNK, (c + 1) * CHUNK) for c in range(per)]
        s0 = [state[hd] for hd in range(HEADS)]
        for hd in range(HEADS):
            st_ref[hd, 0] = s0[hd]
        o, s1 = _gdn_chunk([q_ref[r, sl] for r in rows for sl in sls], [k_ref[r, sl] for r in rows for sl in sls],
                           [v_ref[r, sl] for r in rows for sl in sls], [ba_ref[r, :] for r in rows],
                           alog_ref[...], dtb_ref[...], s0)
        for c, r in enumerate(rows):
            for hd in range(HEADS):
                o_ref[r, sls[hd]] = o[c * HEADS + hd]
        for hd in range(HEADS):
            state[hd] = s1[hd]
        if nk:
            pl.when(pl.program_id(0) == nc - 1)(finish)

    def col(j):
        return pl.BlockSpec((T, GDN_W),lambda n: (n, j))

    vec = _const((1, LANES))
    outs = pl.pallas_call(
        body, name="gdn_fwd", grid=(nc,),
        in_specs=[col(0), col(1), col(2), pl.BlockSpec((T, LANES),lambda n: (n, IN_PAD // LANES - 1)), vec, vec]
        + _hbm_specs(nk),
        out_specs=[col(0), pl.BlockSpec((HEADS, 1, HEAD_DIM, HEAD_DIM), lambda n: (0, n, 0, 0))] + _hbm_specs(nk),
        out_shape=[_sds((S, GDN_W)), _sds((HEADS, nc, HEAD_DIM, HEAD_DIM))] + (_gather_out_shapes(gather) if nk else []),
        scratch_shapes=[pltpu.VMEM((HEADS, HEAD_DIM, HEAD_DIM), F32)] + (_gather_scratch(nk) if nk else []),
        compiler_params=_params(("arbitrary",)),
    )(qkv, qkv, qkv, proj, alog, dtb, *gather)
    return outs[0], outs[1], list(outs[2:])


def _gdn_bwd(do, qkv, proj, states, alog, dtb, scatter=()):
    S = qkv.shape[0]
    per = min(GDN_STEP_CHUNKS, S // CHUNK)
    T = per * CHUNK
    nc = S // T
    nk = len(scatter)

    def body(*refs):
        do_ref, q_ref, k_ref, v_ref, ba_ref, st_ref, alog_ref, dtb_ref = refs[:8]
        dqkv_ref, dba_ref, dalog_ref, ddtb_ref = refs[8 + nk:12 + nk]
        dstate = refs[12 + 2 * nk]
        n = pl.program_id(0)
        if nk:
            start, finish = _scatter_steps(refs[8:8 + nk], refs[12 + nk:12 + 2 * nk], *refs[13 + 2 * nk:])
            pl.when(n == 0)(start)

        @pl.when(n == 0)
        def _():
            dstate[...] = jnp.zeros_like(dstate)

        sls = [slice(hd * HEAD_DIM, (hd + 1) * HEAD_DIM) for hd in range(HEADS)]
        rows = [slice(c * CHUNK, (c + 1) * CHUNK) for c in range(per)]
        fn = functools.partial(_gdn_chunk, inverses=_unit_lower_inverses_diff, mm=_mm)
        _, vjp = jax.vjp(fn, [q_ref[r, sl] for r in rows for sl in sls], [k_ref[r, sl] for r in rows for sl in sls],
                         [v_ref[r, sl] for r in rows for sl in sls], [ba_ref[r, :] for r in rows],
                         alog_ref[...], dtb_ref[...], [st_ref[hd, 0] for hd in range(HEADS)])
        dq, dk, dv, dba, dalog, ddtb, ds = vjp(([do_ref[r, sl] for r in rows for sl in sls],
                                                [dstate[hd] for hd in range(HEADS)]))
        for c, r in enumerate(rows):
            for hd in range(HEADS):
                i = c * HEADS + hd
                dqkv_ref[r, sls[hd]] = dq[i]
                dqkv_ref[r, GDN_W + hd * HEAD_DIM:GDN_W + (hd + 1) * HEAD_DIM] = dk[i]
                dqkv_ref[r, 2 * GDN_W + hd * HEAD_DIM:2 * GDN_W + (hd + 1) * HEAD_DIM] = dv[i]
            dba_ref[r, :] = dba[c]
        for hd in range(HEADS):
            dstate[hd] = ds[hd]
        _acc(dalog_ref, dalog, n == 0)
        _acc(ddtb_ref, ddtb, n == 0)
        if nk:
            pl.when(n == nc - 1)(finish)

    def col(j):
        return pl.BlockSpec((T, GDN_W),lambda n: (nc - 1 - n, j))

    vec = _const((1, LANES))
    outs = pl.pallas_call(
        body, name="gdn_bwd", grid=(nc,),
        in_specs=[col(0), col(0), col(1), col(2),
                  pl.BlockSpec((T, LANES),lambda n: (nc - 1 - n, IN_PAD // LANES - 1)),
                  pl.BlockSpec((HEADS, 1, HEAD_DIM, HEAD_DIM), lambda n: (0, nc - 1 - n, 0, 0)), vec, vec]
        + _hbm_specs(nk),
        out_specs=[pl.BlockSpec((T, 3 * GDN_W),lambda n: (nc - 1 - n, 0)),
                   pl.BlockSpec((T, LANES),lambda n: (nc - 1 - n, 0)), vec, vec] + _hbm_specs(nk),
        out_shape=[_sds((S, 3 * GDN_W)), _sds((S, LANES)), _sds((1, LANES)), _sds((1, LANES))]
        + [_sds(p.shape, p.dtype) for p in scatter],
        scratch_shapes=[pltpu.VMEM((HEADS, HEAD_DIM, HEAD_DIM), F32)] + (_scatter_scratch(nk) if nk else []),
        compiler_params=_params(("arbitrary",)),
    )(do, qkv, qkv, qkv, proj, states, alog, dtb, *scatter)
    return outs[0], outs[1], outs[2], outs[3], list(outs[4:])


def _gdn_gate(o, z, gnw):
    outs = []
    for hd in range(HEADS):
        sl = slice(hd * HEAD_DIM, (hd + 1) * HEAD_DIM)
        xh, _ = _rms_parts(o[:, sl])
        outs.append(xh * gnw * _silu(z[:, sl]))
    return jnp.concatenate(outs, axis=-1)


def _out_fwd(x, out_lru, o, proj, gnw, g1, wout):
    S = x.shape[0]
    tm = min(512, S)

    def body(x_ref, lru_ref, o_ref, z_ref, gnw_ref, g1_ref, w_ref, x1_ref, cat_ref):
        cat = jnp.concatenate([lru_ref[...], _gdn_gate(o_ref[...], z_ref[...], gnw_ref[...])], axis=-1).astype(BF16)
        cat_ref[...] = cat
        x1_ref[...] = x_ref[...] + g1_ref[...] * _dot(cat, w_ref[...])

    return pl.pallas_call(
        body, name="out_fwd", grid=(S // tm,),
        in_specs=[_row(tm, D_MODEL), _row(tm, LRU_W), _row(tm, GDN_W), _row(tm, GDN_W, 5), _const((1, LANES)),
                  _const((1, D_MODEL)), _const((D_MODEL, D_MODEL))],
        out_specs=[_row(tm, D_MODEL), _row(tm, D_MODEL)],
        out_shape=[_sds((S, D_MODEL)), _sds((S, D_MODEL), BF16)],
        compiler_params=_params(("arbitrary",)),
    )(x, out_lru, o, proj, gnw, g1, wout)


def _out_bwd(dx1, cat, o, proj, gnw, g1, wout):
    S = dx1.shape[0]
    tm = min(512, S)

    def body(dx1_ref, cat_ref, o_ref, z_ref, gnw_ref, g1_ref, w_ref,
             dlru_ref, do_ref, dz_ref, dmb_ref, dgnw_ref, dg1_ref):
        i = pl.program_id(0)
        d1 = dx1_ref[...]
        mix = _dot(cat_ref[...], w_ref[...])
        _acc(dg1_ref, _colsum(d1 * mix), i == 0)
        dmb = (d1 * g1_ref[...]).astype(BF16)
        dmb_ref[...] = dmb
        dcat = _dot_nt(dmb, w_ref[...])
        dlru_ref[...] = dcat[:, :LRU_W]
        _, vjp = jax.vjp(_gdn_gate, o_ref[...], z_ref[...], gnw_ref[...])
        do, dz, dgnw = vjp(dcat[:, LRU_W:])
        do_ref[...] = do
        dz_ref[...] = dz
        _acc(dgnw_ref, dgnw, i == 0)

    return pl.pallas_call(
        body, name="out_bwd", grid=(S // tm,),
        in_specs=[_row(tm, D_MODEL), _row(tm, D_MODEL), _row(tm, GDN_W), _row(tm, GDN_W, 5), _const((1, LANES)),
                  _const((1, D_MODEL)), _const((D_MODEL, D_MODEL))],
        out_specs=[_row(tm, LRU_W), _row(tm, GDN_W), _row(tm, GDN_W), _row(tm, D_MODEL), _const((1, LANES)),
                   _const((1, D_MODEL))],
        out_shape=[_sds((S, LRU_W)), _sds((S, GDN_W)), _sds((S, GDN_W)), _sds((S, D_MODEL), BF16), _sds((1, LANES)),
                   _sds((1, D_MODEL))],
        compiler_params=_params(("arbitrary",)),
    )(dx1, cat, o, proj, gnw, g1, wout)


MLP_TM = 256


def _load_once(step, pairs, sem):
    @pl.when(step == 0)
    def _():
        copies = [pltpu.make_async_copy(src, dst, sem.at[k]) for k, (src, dst) in enumerate(pairs)]
        for cp in copies:
            cp.start()
        for cp in copies:
            cp.wait()


def _mlp_fwd(x1, nw, sc, sh, g2, wup, wdown, gather=()):
    S = x1.shape[0]
    tm = min(MLP_TM, S)
    nt = S // tm
    nk = len(gather)

    def body(*refs):
        x_ref, nw_ref, sc_ref, sh_ref, g2_ref, wup_hbm, wdown_hbm = refs[:7]
        x2_ref = refs[7 + nk]
        wup, wdown, sem = refs[8 + 2 * nk:11 + 2 * nk]
        step = pl.program_id(0)
        if nk:
            start, finish = _gather_steps(refs[7:7 + nk], refs[8 + nk:8 + 2 * nk], *refs[11 + 2 * nk:])
            pl.when(step == 0)(start)
        _load_once(step, [(wup_hbm, wup), (wdown_hbm, wdown)], sem)
        x = x_ref[...]
        hb = _norm_mod(x, nw_ref[...], sc_ref[...], sh_ref[...]).astype(BF16)
        r = jnp.maximum(_dot(hb, wup[...]), 0.0)
        x2_ref[...] = x + g2_ref[...] * _dot((r * r).astype(BF16), wdown[...])
        if nk:
            pl.when(step == nt - 1)(finish)

    vec = _const((1, D_MODEL))
    anyspec = pl.BlockSpec(memory_space=pl.ANY)
    outs = pl.pallas_call(
        body, name="mlp_fwd", grid=(nt,),
        in_specs=[_row(tm, D_MODEL), vec, vec, vec, vec, anyspec, anyspec] + _hbm_specs(nk),
        out_specs=[_row(tm, D_MODEL)] + _hbm_specs(nk),
        out_shape=[_sds((S, D_MODEL))] + (_gather_out_shapes(gather) if nk else []),
        scratch_shapes=[pltpu.VMEM((D_MODEL, D_FF), BF16), pltpu.VMEM((D_FF, D_MODEL), BF16),
                        pltpu.SemaphoreType.DMA((2,))] + (_gather_scratch(nk) if nk else []),
        compiler_params=_params(("arbitrary",)),
    )(x1, nw, sc, sh, g2, wup, wdown, *gather)
    return outs[0], list(outs[1:])


def _mlp_bwd(dx2, x1, nw, sc, sh, g2, wup, wdown):
    S = x1.shape[0]
    tm = min(MLP_TM, S)

    def body(dx2_ref, x_ref, nw_ref, sc_ref, sh_ref, g2_ref, wup_hbm, wdown_hbm,
             dx1_ref, hb_ref, dupb_ref, actb_ref, d2b_ref, dnw_ref, dsc_ref, dsh_ref, wup, wdown, sem):
        i = pl.program_id(0)
        _load_once(i, [(wup_hbm, wup), (wdown_hbm, wdown)], sem)
        x = x_ref[...]
        d2 = dx2_ref[...]
        hb = _norm_mod(x, nw_ref[...], sc_ref[...], sh_ref[...]).astype(BF16)
        hb_ref[...] = hb
        r = jnp.maximum(_dot(hb, wup[...]), 0.0)
        actb = (r * r).astype(BF16)
        actb_ref[...] = actb
        d2b_ref[...] = d2.astype(BF16)
        ddb = (d2 * g2_ref[...]).astype(BF16)
        dupb = (_dot_nt(ddb, wdown[...]) * (2.0 * r)).astype(BF16)
        dupb_ref[...] = dupb
        dh = _dot_nt(dupb, wup[...])
        dx, dnw, dsc, dsh = _norm_mod_bwd(dh, x, nw_ref[...], sc_ref[...])
        dx1_ref[...] = d2 + dx
        _acc(dnw_ref, dnw, i == 0)
        _acc(dsc_ref, dsc, i == 0)
        _acc(dsh_ref, dsh, i == 0)

    vec = _const((1, D_MODEL))
    anyspec = pl.BlockSpec(memory_space=pl.ANY)
    return pl.pallas_call(
        body, name="mlp_bwd", grid=(S // tm,),
        in_specs=[_row(tm, D_MODEL), _row(tm, D_MODEL), vec, vec, vec, vec, anyspec, anyspec],
        out_specs=[_row(tm, D_MODEL), _row(tm, D_MODEL), _row(tm, D_FF), _row(tm, D_FF), _row(tm, D_MODEL),
                   vec, vec, vec],
        out_shape=[_sds((S, D_MODEL)), _sds((S, D_MODEL), BF16), _sds((S, D_FF), BF16), _sds((S, D_FF), BF16),
                   _sds((S, D_MODEL), BF16), _sds((1, D_MODEL)), _sds((1, D_MODEL)), _sds((1, D_MODEL))],
        scratch_shapes=[pltpu.VMEM((D_MODEL, D_FF), BF16), pltpu.VMEM((D_FF, D_MODEL), BF16),
                        pltpu.SemaphoreType.DMA((2,))],
        compiler_params=_params(("arbitrary",)),
    )(dx2, x1, nw, sc, sh, g2, wup, wdown)


def _dw_down(act, d2b, g2, wdown, sharded, out_dtype):
    K, M = act.shape
    N = d2b.shape[1]
    tk = min(2048, K)
    nk = K // tk
    if sharded:
        h = M // (2 * N_CHIPS)
        tm = 2 * h
        out_spec = pl.BlockSpec((2, 1, h, N), lambda i, k: (0, i, 0, 0))
        out_shape = _sds((2, N_CHIPS, h, N), out_dtype)
    else:
        tm = min(512, M)
        out_spec = pl.BlockSpec((tm, N), lambda i, k: (i, 0))
        out_shape = _sds((M, N), out_dtype)

    def body(a_ref, b_ref, g2_ref, w_ref, o_ref, dg2_ref, acc):
        i = pl.program_id(0)
        k = pl.program_id(1)
        _acc(acc, _dot_tn(a_ref[...], b_ref[...]), k == 0)

        @pl.when(k == nk - 1)
        def _():
            g = acc[...]
            _acc(dg2_ref, _colsum(g * w_ref[...].astype(F32)), i == 0)
            out = (g * g2_ref[...]).astype(o_ref.dtype)
            if sharded:
                o_ref[0, 0] = out[:h]
                o_ref[1, 0] = out[h:]
            else:
                o_ref[...] = out

    vec = pl.BlockSpec((1, N), lambda i, k: (0, 0))
    return pl.pallas_call(
        body, name="dw_down", grid=(M // tm, nk),
        in_specs=[pl.BlockSpec((tk, tm), lambda i, k: (k, i)), pl.BlockSpec((tk, N), lambda i, k: (k, 0)), vec,
                  pl.BlockSpec((tm, N), lambda i, k: (i, 0))],
        out_specs=[out_spec, vec], out_shape=[out_shape, _sds((1, N))],
        scratch_shapes=[pltpu.VMEM((tm, N), F32)],
        compiler_params=_params(("arbitrary", "arbitrary")),
    )(act, d2b, g2, wdown)


def _matmul_tn(a, b, name, shards=None, out_dtype=F32):
    K, M = a.shape
    N = b.shape[1]
    tk = min(2048, K)
    if shards == "cols":
        tm, tn = M // 2, N // N_CHIPS
        out_spec = pl.BlockSpec((1, 1, tm, tn), lambda i, j, k: (i, j, 0, 0))
        out_shape = _sds((2, N_CHIPS, tm, tn))
    elif shards == "rows":
        h, tn = M // (2 * N_CHIPS), N
        tm = max(512, 2 * h)
        per_tile = tm // (2 * h)
        out_spec = pl.BlockSpec((2, per_tile, h, tn), lambda i, j, k: (0, i, 0, 0))
        out_shape = _sds((2, N_CHIPS, h, tn))
    else:
        tm = min(512, M)
        tn = 640 if N % 640 == 0 else min(1024, N)
        out_spec = pl.BlockSpec((tm, tn), lambda i, j, k: (i, j))
        out_shape = _sds((M, N))
    nk = K // tk

    def body(a_ref, b_ref, o_ref, acc):
        k = pl.program_id(2)
        _acc(acc, _dot_tn(a_ref[...], b_ref[...]), k == 0)

        @pl.when(k == nk - 1)
        def _():
            if shards == "rows":
                for s in range(per_tile):
                    for half in range(2):
                        r0 = (2 * s + half) * h
                        o_ref[half, s] = acc[r0:r0 + h, :].astype(o_ref.dtype)
            else:
                o_ref[...] = acc[...].reshape(o_ref.shape).astype(o_ref.dtype)

    return pl.pallas_call(
        body, name=name, grid=(M // tm, N // tn, nk),
        in_specs=[pl.BlockSpec((tk, tm), lambda i, j, k: (k, i)), pl.BlockSpec((tk, tn), lambda i, j, k: (k, j))],
        out_specs=out_spec, out_shape=_sds(out_shape.shape, out_dtype),
        scratch_shapes=[pltpu.VMEM((tm, tn), F32)],
        compiler_params=_params(("arbitrary", "arbitrary", "arbitrary")),
    )(a, b)


def _loss_head(x, target, fnw):
    S = x.shape[0]
    tm = min(512, S)

    def body(x_ref, t_ref, w_ref, dx_ref, loss_ref, dw_ref):
        i = pl.program_id(0)
        w = w_ref[...]
        xh, r = _rms_parts(x_ref[...])
        err = xh * w - t_ref[...]
        part = 0.5 * jnp.sum(jnp.mean(err * err, axis=-1, keepdims=True), axis=0, keepdims=True)
        _acc(loss_ref, jnp.broadcast_to(part, (SUBLANES, LANES)), i == 0)
        dx, dw = _rms_bwd(err * (1.0 / D_MODEL), xh, r, w)
        dx_ref[...] = dx
        _acc(dw_ref, dw, i == 0)

    vec = _const((1, D_MODEL))
    return pl.pallas_call(
        body, name="loss_head", grid=(S // tm,),
        in_specs=[_row(tm, D_MODEL), _row(tm, D_MODEL), vec],
        out_specs=[_row(tm, D_MODEL), _const((SUBLANES, LANES)), vec],
        out_shape=[_sds((S, D_MODEL)), _sds((SUBLANES, LANES)), _sds((1, D_MODEL))],
        compiler_params=_params(("arbitrary",)),
    )(x, target, fnw)


def _block_diag(w):
    eye = jnp.eye(LRU_BLOCKS, dtype=w.dtype)
    return (eye[:, None, :, None] * w[:, :, None, :]).reshape(LRU_W, LRU_W)


def _diag_blocks(m):
    m4 = m.reshape(LRU_BLOCKS, LRU_BLOCK, LRU_BLOCKS, LRU_BLOCK)
    return jnp.stack([m4[g, :, g, :] for g in range(LRU_BLOCKS)])


def _layer_fwd(x, p, mlp_shards=(), mlp_weights=None, next_shards=()):
    proj, h1b = _proj_fwd(x, p["nmw"], p["sc1"], p["sh1"], p["win"])
    xr = _conv_fwd(proj, 0, LRU_W, p["lcw"], p["lcb"], False, "conv_lru_fwd")
    out_lru, h = _lru_fwd(xr, proj, p["wa"].astype(BF16), p["ba"], p["wx"].astype(BF16), p["bx"], p["lam"], p["lnw"])
    qkv = _conv_fwd(proj, 2 * LRU_W, 3 * GDN_W, p["gcw"], p["gcb"], True, "conv_gdn_fwd")
    o, states, gathered = _gdn_fwd(qkv, proj, p["alog"], p["dtb"], mlp_shards)
    if mlp_weights is not None:
        p = {**p, **mlp_weights(gathered)}
    x1, cat = _out_fwd(x, out_lru, o, proj, p["gnw"], p["g1"], p["wout"])
    x2, gathered_next = _mlp_fwd(x1, p["nmlp"], p["sc2"], p["sh2"], p["g2"], p["wup"], p["wdown"], next_shards)
    res = dict(x=x, proj=proj, h1b=h1b, xr=xr, h=h, qkv=qkv, o=o, states=states, x1=x1, cat=cat)
    return x2, res, p, gathered_next


def _layer_bwd(dx2, p, r, sharded=False, scatter=(), early=None):
    dx1, h2b, dupb, actb, d2b, dnmlp, dsc2, dsh2 = _mlp_bwd(
        dx2, r["x1"], p["nmlp"], p["sc2"], p["sh2"], p["g2"], p["wup"], p["wdown"])
    gdt = BF16 if sharded else F32
    g_wup = _matmul_tn(h2b, dupb, "dw_up", "cols" if sharded else None, gdt)
    g_wdown, dg2 = _dw_down(actb, d2b, p["g2"], p["wdown"], sharded, gdt)
    dlru, do, dz, dmb, dgnw, dg1 = _out_bwd(dx1, r["cat"], r["o"], r["proj"], p["gnw"], p["g1"], p["wout"])
    g_wout = _matmul_tn(r["cat"], dmb, "dw_out", "rows" if sharded else None, gdt)
    if early is not None:
        scatter = list(scatter) + early([g_wout, g_wup, g_wdown])
    dqkv_act, dba, dalog, ddtb, arrived = _gdn_bwd(do, r["qkv"], r["proj"], r["states"], p["alog"], p["dtb"], scatter)
    dqkv, dgcw, _ = _conv_bwd(r["proj"], 2 * LRU_W, 3 * GDN_W, p["gcw"], p["gcb"], dqkv_act, True, "conv_gdn_bwd")
    wab = p["wa"].astype(BF16)
    wxb = p["wx"].astype(BF16)
    dxr, dly, dwa, dba_, dwx, dbx, dlam, dlnw = _lru_bwd(
        dlru, r["xr"], r["proj"], r["h"], wab, p["ba"], wxb, p["bx"], p["lam"], p["lnw"])
    dlx, dlcw, dlcb = _conv_bwd(r["proj"], 0, LRU_W, p["lcw"], p["lcb"], dxr, False, "conv_lru_bwd")
    dx, dpb, dnmw, dsc1, dsh1 = _proj_bwd(dx1, r["x"], dlx, dly, dqkv, dz, dba, p["nmw"], p["sc1"], p["win"])
    g_win = _matmul_tn(r["h1b"], dpb, "dw_in", None, gdt)
    grads = dict(nmw=dnmw, nmlp=dnmlp, sh1=dsh1, sc1=dsc1, g1=dg1, sh2=dsh2, sc2=dsc2, g2=dg2,
                 win=g_win, lcw=dlcw, lcb=dlcb, wa=dwa, ba=dba_, wx=dwx, bx=dbx, lam=dlam, lnw=dlnw,
                 gcw=dgcw, alog=dalog, dtb=ddtb, gnw=dgnw, wout=g_wout, wup=g_wup, wdown=g_wdown)
    return dx, grads, arrived


def _local_step(x, target, fnw, layers):
    res = []
    for p in layers:
        x, r, _, _ = _layer_fwd(x, p)
        res.append(r)
    dx, loss_blk, dfnw = _loss_head(x, target, fnw)
    grads = [None] * len(layers)
    for l in reversed(range(len(layers))):
        dx, grads[l], _ = _layer_bwd(dx, layers[l], res[l])
    stacked = {k: jnp.stack([g[k] for g in grads]) for k in grads[0]}
    return loss_blk[0, 0], dx, dfnw, stacked


def _prep_layers(norm_mix_w, norm_mlp_w, mod, win_b, lru_conv_w, lru_conv_b, gate_a_w, gate_a_b, gate_x_w, gate_x_b,
                 lru_lambda, lru_norm_w, gdn_conv_w, gdn_a_log, gdn_dt_bias, gdn_norm_w, wout_b, wup_b, wdown_b):
    L = norm_mix_w.shape[0]

    def vec(a):
        return a.reshape(L, 1, -1)

    def lanes(a):
        return jnp.pad(a, ((0, 0), (0, LANES - a.shape[1]))).reshape(L, 1, LANES)

    def taps(w):
        return jnp.pad(w, ((0, 0), (0, SUBLANES - w.shape[1]), (0, 0)))

    m = mod.reshape(L, N_MOD, 1, D_MODEL)
    return dict(
        nmw=vec(norm_mix_w), nmlp=vec(norm_mlp_w),
        sh1=m[:, 0], sc1=m[:, 1], g1=m[:, 2], sh2=m[:, 3], sc2=m[:, 4], g2=m[:, 5],
        win=win_b, lcw=taps(lru_conv_w), lcb=vec(lru_conv_b),
        wa=jax.vmap(_block_diag)(gate_a_w), ba=vec(gate_a_b), wx=jax.vmap(_block_diag)(gate_x_w), bx=vec(gate_x_b),
        lam=vec(lru_lambda), lnw=vec(lru_norm_w),
        gcw=taps(gdn_conv_w), gcb=jnp.zeros((L, 1, 3 * GDN_W), F32),
        alog=lanes(gdn_a_log), dtb=lanes(gdn_dt_bias), gnw=vec(gdn_norm_w),
        wout=wout_b, wup=wup_b, wdown=wdown_b)


def _position():
    x, y, c = lax.axis_index("x"), lax.axis_index("y"), lax.axis_index("c")
    return x, y, c


def _other_chips(x, y):
    return [(1 - x, y), (x, 1 - y), (1 - x, 1 - y)]


def _all_gather_rows(block, name):
    m, n = block.shape

    def body(x_ref, out_ref, send_sems, recv_sems, local_sem):
        x, y, c = _position()
        me, sibling = (x, y, c), (x, y, 1 - c)
        chips = _other_chips(x, y)

        def rows(px, py, pc):
            return out_ref.at[pl.ds((4 * px + 2 * py + pc) * m, m), :]

        def copy(k, blk, to, src=None):
            return pltpu.make_async_remote_copy(
                src_ref=rows(*blk) if src is None else src, dst_ref=rows(*blk),
                send_sem=send_sems.at[k], recv_sem=recv_sems.at[k], device_id=to, device_id_type=MESH)

        mine = pltpu.make_async_copy(x_ref, rows(*me), local_sem)
        mine.start()
        first = [copy(0, me, sibling, src=x_ref)]
        first += [copy(1 + j, me, (*chip, c), src=x_ref) for j, chip in enumerate(chips)]
        for cp in first:
            cp.start()
        passed = [copy(4 + j, (*chip, c), sibling) for j, chip in enumerate(chips)]
        for j, chip in enumerate(chips):
            copy(1 + j, (*chip, c), me).wait_recv()
            passed[j].start()
        copy(0, sibling, me).wait_recv()
        for j, chip in enumerate(chips):
            copy(4 + j, (*chip, 1 - c), me).wait_recv()
        for cp in first + passed:
            cp.wait_send()
        mine.wait()

    return pl.pallas_call(
        body, name=name,
        out_shape=_sds((N_DEV * m, n)),
        in_specs=[pl.BlockSpec(memory_space=pltpu.VMEM)],
        out_specs=pl.BlockSpec(memory_space=pltpu.VMEM),
        scratch_shapes=[pltpu.SemaphoreType.DMA((7,)), pltpu.SemaphoreType.DMA((7,)), pltpu.SemaphoreType.DMA],
        compiler_params=pltpu.CompilerParams(vmem_limit_bytes=VMEM_LIMIT),
    )(block)


def _hbm_specs(n):
    return [pl.BlockSpec(memory_space=pl.ANY)] * n


def _gather_chips(shards, name):
    n = len(shards)

    def body(*refs):
        start, finish = _gather_steps(refs[:n], refs[n:2 * n], *refs[2 * n:])
        start()
        finish()

    return pl.pallas_call(
        body, name=name,
        out_shape=_gather_out_shapes(shards), in_specs=_hbm_specs(n), out_specs=_hbm_specs(n),
        scratch_shapes=_gather_scratch(n),
    )(*shards)


def _gather_out_shapes(shards):
    return [_sds((N_CHIPS, 2, s.shape[0] // 2, s.shape[1]), s.dtype) for s in shards]


def _gather_scratch(n):
    return [pltpu.SemaphoreType.DMA((6 * n,)), pltpu.SemaphoreType.DMA((6 * n,))]


def _gather_steps(ins, outs, send_sems, recv_sems):
    n = len(ins)
    x, y, c = _position()
    chips = _other_chips(x, y)
    me = 2 * x + y

    def first(a, j, slot):
        h = ins[a].shape[0] // 2
        return pltpu.make_async_remote_copy(
            src_ref=ins[a].at[pl.ds(pl.multiple_of(c * h, SUBLANES), h)], dst_ref=outs[a].at[slot, c],
            send_sem=send_sems.at[3 * a + j], recv_sem=recv_sems.at[3 * a + j],
            device_id=(chips[j][0], chips[j][1], c), device_id_type=MESH)

    def second(a, j, half):
        slot = 2 * chips[j][0] + chips[j][1]
        return pltpu.make_async_remote_copy(
            src_ref=outs[a].at[slot, c], dst_ref=outs[a].at[slot, half],
            send_sem=send_sems.at[3 * (n + a) + j], recv_sem=recv_sems.at[3 * (n + a) + j],
            device_id=(x, y, 1 - c), device_id_type=MESH)

    def start():
        for a in range(n):
            for j in range(3):
                first(a, j, me).start()

    def finish():
        for a in range(n):
            for j, (px, py) in enumerate(chips):
                first(a, j, 2 * px + py).wait_recv()
                second(a, j, c).start()
        for a in range(n):
            for j in range(3):
                second(a, j, 1 - c).wait_recv()
        for a in range(n):
            for j in range(3):
                first(a, j, me).wait_send()
                second(a, j, c).wait_send()

    return start, finish


def _send_to_sibling(parts, name):
    n = len(parts)

    def body(*refs):
        ins, outs = refs[:n], refs[n:2 * n]
        send_sems, recv_sems = refs[2 * n:]
        x, y, c = _position()
        copies = [pltpu.make_async_remote_copy(
            src_ref=ins[a].at[1 - c], dst_ref=outs[a], send_sem=send_sems.at[a], recv_sem=recv_sems.at[a],
            device_id=(x, y, 1 - c), device_id_type=MESH) for a in range(n)]
        for cp in copies:
            cp.start()
        for cp in copies:
            cp.wait()

    return pl.pallas_call(
        body, name=name,
        out_shape=[_sds(p.shape[1:], p.dtype) for p in parts],
        in_specs=_hbm_specs(n), out_specs=_hbm_specs(n),
        scratch_shapes=[pltpu.SemaphoreType.DMA((n,)), pltpu.SemaphoreType.DMA((n,))],
    )(*parts)


def _scatter_chips(parts, name):
    n = len(parts)

    def body(*refs):
        start, finish = _scatter_steps(refs[:n], refs[n:2 * n], *refs[2 * n:])
        start()
        finish()

    return pl.pallas_call(
        body, name=name,
        out_shape=[_sds(p.shape, p.dtype) for p in parts], in_specs=_hbm_specs(n), out_specs=_hbm_specs(n),
        scratch_shapes=_scatter_scratch(n),
    )(*parts)


def _scatter_scratch(n):
    return [pltpu.SemaphoreType.DMA((3 * n,)), pltpu.SemaphoreType.DMA((3 * n,))]


def _scatter_steps(ins, outs, send_sems, recv_sems):
    n = len(ins)
    x, y, c = _position()
    chips = _other_chips(x, y)
    me = 2 * x + y

    def copy(a, j, src_slot, dst_slot):
        px, py = chips[j]
        return pltpu.make_async_remote_copy(
            src_ref=ins[a].at[src_slot], dst_ref=outs[a].at[dst_slot], send_sem=send_sems.at[3 * a + j],
            recv_sem=recv_sems.at[3 * a + j], device_id=(px, py, c), device_id_type=MESH)

    def start():
        for a in range(n):
            for j in range(3):
                copy(a, j, 2 * chips[j][0] + chips[j][1], me).start()

    def finish():
        for a in range(n):
            for j, (px, py) in enumerate(chips):
                copy(a, j, me, 2 * px + py).wait_recv()
        for a in range(n):
            for j in range(3):
                copy(a, j, 2 * chips[j][0] + chips[j][1], me).wait_send()

    return start, finish


def _swap_row_halves(arrays, name):
    n = len(arrays)

    def body(*refs):
        outs = refs[n:2 * n]
        send_sems, recv_sems = refs[2 * n:]
        x, y, c = _position()

        def copy(a, half):
            h = outs[a].shape[0] // 2
            rows = outs[a].at[pl.ds(pl.multiple_of(half * h, SUBLANES), h)]
            return pltpu.make_async_remote_copy(
                src_ref=rows, dst_ref=rows, send_sem=send_sems.at[a], recv_sem=recv_sems.at[a],
                device_id=(x, y, 1 - c), device_id_type=MESH)

        sends = [copy(a, c) for a in range(n)]
        for cp in sends:
            cp.start()
        for a in range(n):
            copy(a, 1 - c).wait_recv()
        for cp in sends:
            cp.wait_send()

    return pl.pallas_call(
        body, name=name,
        out_shape=[_sds(a.shape, a.dtype) for a in arrays],
        in_specs=_hbm_specs(n), out_specs=_hbm_specs(n),
        input_output_aliases={a: a for a in range(n)},
        scratch_shapes=[pltpu.SemaphoreType.DMA((n,)), pltpu.SemaphoreType.DMA((n,))],
    )(*arrays)


def _row_tile(rows):
    for t in (512, 256, 128, 64, 32, 16, 8):
        if rows % t == 0:
            return t
    return rows


def _sum_slots(buf, name):
    k, rows, cols = buf.shape
    tm = _row_tile(rows)

    def body(b_ref, o_ref):
        s = b_ref[0]
        for i in range(1, k):
            s = s + b_ref[i]
        o_ref[...] = s

    return pl.pallas_call(
        body, name=name, grid=(rows // tm,),
        in_specs=[pl.BlockSpec((k, tm, cols), lambda i: (0, i, 0))],
        out_specs=pl.BlockSpec((tm, cols), lambda i: (i, 0)),
        out_shape=_sds((rows, cols)),
        compiler_params=_params(("arbitrary",)),
    )(buf)


def _pair_add(part, from_sibling, core, name):
    _, k, h, cols = part.shape
    rows = k * h
    tm = _row_tile(rows)

    def body(core_ref, a_ref, b_ref, o_ref):
        o_ref[...] = (a_ref[0].astype(F32) + b_ref[...].astype(F32)).astype(BF16)

    out = pl.pallas_call(
        body, name=name,
        grid_spec=pltpu.PrefetchScalarGridSpec(
            num_scalar_prefetch=1, grid=(rows // tm,),
            in_specs=[pl.BlockSpec((1, tm, cols), lambda i, cr: (cr[0], i, 0)),
                      pl.BlockSpec((tm, cols), lambda i, cr: (i, 0))],
            out_specs=pl.BlockSpec((tm, cols), lambda i, cr: (i, 0))),
        out_shape=_sds((rows, cols), BF16),
        compiler_params=_params(("arbitrary",)),
    )(core, part.reshape(2, rows, cols), from_sibling.reshape(rows, cols))
    return out.reshape(k, h, cols)


def _chip_sum(arrived, own, place, name):
    _, h, cols = arrived.shape
    tm = min(256, h)
    nb = h // tm

    def body(place_ref, arr_ref, own_ref, g_ref):
        for chip in range(N_CHIPS):
            @pl.when(place_ref[1] == chip)
            def _():
                terms = [own_ref[0] if j == chip else arr_ref[j] for j in range(N_CHIPS)]
                g = terms[0].astype(F32)
                for t in terms[1:]:
                    g = g + t.astype(F32)
                g_ref[...] = g

    return pl.pallas_call(
        body, name=name,
        grid_spec=pltpu.PrefetchScalarGridSpec(
            num_scalar_prefetch=1, grid=(nb,),
            in_specs=[pl.BlockSpec((N_CHIPS, tm, cols), lambda i, pr: (0, i, 0)),
                      pl.BlockSpec((1, tm, cols), lambda i, pr: (pr[1], i, 0))],
            out_specs=pl.BlockSpec((tm, cols), lambda i, pr: (pr[0] * nb + i, 0))),
        out_shape=_sds((2 * h, cols)),
        compiler_params=_params(("arbitrary",)),
    )(place, arrived, own)


def _adam_layer(g, w, m, v, outs, layer, name):
    rows, cols = g.shape
    tm = _row_tile(rows)

    def body(g_ref, w_ref, m_ref, v_ref, *refs):
        og_ref, od_ref, om_ref, ov_ref = refs[4:]
        gr = g_ref[...]
        og_ref[0] = gr
        d, nm, nv = _adam_math(w_ref[0], gr, m_ref[0], v_ref[0])
        od_ref[0] = d
        om_ref[0] = nm
        ov_ref[0] = nv

    slab = pl.BlockSpec((1, tm, cols), lambda i: (layer, i, 0))
    return pl.pallas_call(
        body, name=name, grid=(rows // tm,),
        in_specs=[pl.BlockSpec((tm, cols), lambda i: (i, 0)), slab, slab, slab] + _hbm_specs(4),
        out_specs=[slab] * 4, out_shape=[_sds(o.shape) for o in outs],
        input_output_aliases={4 + i: i for i in range(4)},
        compiler_params=_params(("arbitrary",)),
    )(g, w, m, v, *outs)


def _adam_math(w, g, m, v):
    m = ADAM_B1 * m + (1.0 - ADAM_B1) * g
    v = ADAM_B2 * v + (1.0 - ADAM_B2) * jnp.square(g)
    m_hat = m / (1.0 - ADAM_B1 ** ADAM_STEP)
    v_hat = v / (1.0 - ADAM_B2 ** ADAM_STEP)
    delta = -ADAM_LR * (m_hat / (jnp.sqrt(v_hat) + ADAM_EPS) + ADAM_WD * w)
    return delta, m, v


def _adam(w, g, m, v, name):
    rows, cols = w.shape
    tm = _row_tile(rows)

    def body(w_ref, g_ref, m_ref, v_ref, d_ref, nm_ref, nv_ref):
        d, nm, nv = _adam_math(w_ref[...], g_ref[...], m_ref[...], v_ref[...])
        d_ref[...] = d
        nm_ref[...] = nm
        nv_ref[...] = nv

    spec = pl.BlockSpec((tm, cols), lambda i: (i, 0))
    return pl.pallas_call(
        body, name=name, grid=(rows // tm,), in_specs=[spec] * 4, out_specs=[spec] * 3,
        out_shape=[_sds((rows, cols))] * 3, compiler_params=_params(("arbitrary",)),
    )(w, g, m, v)


def _mod_fwd(c_all, w_mod, b_mod_cols):
    L, _, n = w_mod.shape

    def body(c_ref, w_ref, b_ref, o_ref):
        o_ref[0] = _hdot(_silu(c_ref[...]), w_ref[0]) + b_ref[0]

    return pl.pallas_call(
        body, name="mod_fwd", grid=(L,),
        in_specs=[_const((N_DEV, D_MODEL)), pl.BlockSpec((1, D_MODEL, n), lambda l: (l, 0, 0)),
                  pl.BlockSpec((1, 1, n), lambda l: (l, 0, 0))],
        out_specs=pl.BlockSpec((1, N_DEV, n), lambda l: (l, 0, 0)),
        out_shape=_sds((L, N_DEV, n)),
        compiler_params=_params(("arbitrary",)),
    )(c_all, w_mod, b_mod_cols)


def _mod_update(c_all, dmod, w, m, v):
    L, _, n = w.shape
    tn = 512

    def body(c_ref, d_ref, w_ref, m_ref, v_ref, g_ref, dl_ref, nm_ref, nv_ref):
        g = _hdot_tn(_silu(c_ref[...]), d_ref[0])
        g_ref[0] = g
        d, nm, nv = _adam_math(w_ref[0], g, m_ref[0], v_ref[0])
        dl_ref[0] = d
        nm_ref[0] = nm
        nv_ref[0] = nv

    big = pl.BlockSpec((1, D_MODEL, tn), lambda l, j: (l, 0, j))
    return pl.pallas_call(
        body, name="mod_update", grid=(L, n // tn),
        in_specs=[_const((N_DEV, D_MODEL)), pl.BlockSpec((1, N_DEV, tn), lambda l, j: (l, 0, j)), big, big, big],
        out_specs=[big] * 4, out_shape=[_sds(w.shape)] * 4,
        compiler_params=_params(("arbitrary", "arbitrary")),
    )(c_all, dmod, w, m, v)


def _pack_rows(parts, row_multiple):
    flat = jnp.concatenate([p.reshape(-1) for p in parts])
    unit = row_multiple * LANES
    flat = jnp.pad(flat, (0, (-flat.shape[0]) % unit))
    return flat.reshape(-1, LANES)


def _unpack(packed, shapes):
    flat = packed.reshape(-1)
    out, off = [], 0
    for s in shapes:
        n = 1
        for d in s:
            n *= d
        out.append(flat[off:off + n].reshape(s))
        off += n
    return out


def _lane_pad(a):
    return jnp.pad(a, ((0, 0), (0, LANES - a.shape[1])))


WEIGHT_NAMES = ("norm_mix_w", "norm_mlp_w", "w_mod", "b_mod", "w_in", "lru_conv_w", "lru_conv_b", "lru_gate_a_w",
                "lru_gate_a_b", "lru_gate_x_w", "lru_gate_x_b", "lru_lambda", "lru_norm_w", "gdn_conv_w", "gdn_a_log",
                "gdn_dt_bias", "gdn_norm_w", "w_out", "w_up", "w_down", "final_norm_w")


def kernel(x, c, norm_mix_w, norm_mlp_w, w_mod, b_mod, w_in, lru_conv_w, lru_conv_b, lru_gate_a_w, lru_gate_a_b, lru_gate_x_w, lru_gate_x_b, lru_lambda, lru_norm_w, gdn_conv_w, gdn_a_log, gdn_dt_bias, gdn_norm_w, w_out, w_up, w_down, final_norm_w, loss_target, m_norm_mix_w, m_norm_mlp_w, m_w_mod, m_b_mod, m_w_in, m_lru_conv_w, m_lru_conv_b, m_lru_gate_a_w, m_lru_gate_a_b, m_lru_gate_x_w, m_lru_gate_x_b, m_lru_lambda, m_lru_norm_w, m_gdn_conv_w, m_gdn_a_log, m_gdn_dt_bias, m_gdn_norm_w, m_w_out, m_w_up, m_w_down, m_final_norm_w, v_norm_mix_w, v_norm_mlp_w, v_w_mod, v_b_mod, v_w_in, v_lru_conv_w, v_lru_conv_b, v_lru_gate_a_w, v_lru_gate_a_b, v_lru_gate_x_w, v_lru_gate_x_b, v_lru_lambda, v_lru_norm_w, v_gdn_conv_w, v_gdn_a_log, v_gdn_dt_bias, v_gdn_norm_w, v_w_out, v_w_up, v_w_down, v_final_norm_w):
    W = dict(zip(WEIGHT_NAMES, (norm_mix_w, norm_mlp_w, w_mod, b_mod, w_in, lru_conv_w, lru_conv_b, lru_gate_a_w,
                                lru_gate_a_b, lru_gate_x_w, lru_gate_x_b, lru_lambda, lru_norm_w, gdn_conv_w, gdn_a_log,
                                gdn_dt_bias, gdn_norm_w, w_out, w_up, w_down, final_norm_w)))
    M = dict(zip(WEIGHT_NAMES, (m_norm_mix_w, m_norm_mlp_w, m_w_mod, m_b_mod, m_w_in, m_lru_conv_w, m_lru_conv_b,
                                m_lru_gate_a_w, m_lru_gate_a_b, m_lru_gate_x_w, m_lru_gate_x_b, m_lru_lambda,
                                m_lru_norm_w, m_gdn_conv_w, m_gdn_a_log, m_gdn_dt_bias, m_gdn_norm_w, m_w_out, m_w_up,
                                m_w_down, m_final_norm_w)))
    V = dict(zip(WEIGHT_NAMES, (v_norm_mix_w, v_norm_mlp_w, v_w_mod, v_b_mod, v_w_in, v_lru_conv_w, v_lru_conv_b,
                                v_lru_gate_a_w, v_lru_gate_a_b, v_lru_gate_x_w, v_lru_gate_x_b, v_lru_lambda,
                                v_lru_norm_w, v_gdn_conv_w, v_gdn_a_log, v_gdn_dt_bias, v_gdn_norm_w, v_w_out, v_w_up,
                                v_w_down, v_final_norm_w)))
    L = DEPTH
    xi, yi, ci = _position()
    chip = 2 * xi + yi
    dev = 2 * chip + ci
    lcs = LRU_W // N_CHIPS
    gcs = 3 * GDN_W // N_CHIPS
    mcs = N_MOD * D_MODEL // N_CHIPS

    g_in = _all_gather_rows(_pack_rows([c, lru_conv_w, gdn_conv_w], SUBLANES), "gather_small_inputs").reshape(N_DEV, -1)
    c_all = g_in[:, :D_MODEL]
    per_chip = g_in[0::2]
    o1 = D_MODEL + L * 4 * lcs
    lcw_full = per_chip[:, D_MODEL:o1].reshape(N_CHIPS, L, 4, lcs).transpose(1, 2, 0, 3).reshape(L, 4, LRU_W)
    gcw_full = per_chip[:, o1:o1 + L * 4 * gcs].reshape(N_CHIPS, L, 4, gcs).transpose(1, 2, 0, 3).reshape(L, 4, 3 * GDN_W)

    b_cols = lax.dynamic_slice(b_mod, (0, chip * mcs), (L, mcs)).reshape(L, 1, mcs)
    modp = _mod_fwd(c_all, w_mod, b_cols)
    g_mod = _all_gather_rows(modp.reshape(L * N_DEV, mcs), "gather_mod").reshape(N_DEV, L, N_DEV, mcs)
    mod = lax.dynamic_index_in_dim(g_mod[0::2], dev, axis=2, keepdims=False).transpose(1, 0, 2).reshape(L, N_MOD * D_MODEL)

    stacked = _prep_layers(norm_mix_w, norm_mlp_w, mod, None, lcw_full, lru_conv_b, lru_gate_a_w, lru_gate_a_b,
                           lru_gate_x_w, lru_gate_x_b, lru_lambda, lru_norm_w, gcw_full, gdn_a_log, gdn_dt_bias,
                           gdn_norm_w, None, None, None)
    shards = [[w_in[l].astype(BF16), w_out[l].astype(BF16), w_up[l].astype(BF16), w_down[l].astype(BF16)]
              for l in range(L)]

    def with_own(gathered, own):
        return [lax.dynamic_update_slice(got, o.reshape((1,) + got.shape[1:]), (chip, 0, 0, 0)).reshape(
            (N_CHIPS,) + o.shape) for got, o in zip(gathered, own)]

    def mixer_weights(l, gathered):
        win_g, wout_g = with_own(gathered, shards[l][:2])
        return dict(win=jnp.pad(win_g.transpose(1, 0, 2).reshape(D_MODEL, IN_COLS), ((0, 0), (0, IN_PAD - IN_COLS))),
                    wout=wout_g.reshape(D_MODEL, D_MODEL))

    def mlp_weights(l, gathered):
        wup_g, wdown_g = with_own(gathered, shards[l][2:])
        return dict(wup=wup_g.transpose(1, 0, 2).reshape(D_MODEL, D_FF), wdown=wdown_g.reshape(D_FF, D_MODEL))

    mixer = mixer_weights(0, _gather_chips(shards[0][:2], "gather_weights"))
    layers = []
    xs = x[0]
    res = []
    for l in range(L):
        p = {k: v[l] for k, v in stacked.items() if v is not None}
        p.update(mixer)
        xs, r, p, gathered = _layer_fwd(xs, p, shards[l][2:], functools.partial(mlp_weights, l),
                                        shards[l + 1][:2] if l + 1 < L else ())
        res.append(r)
        layers.append(p)
        if l + 1 < L:
            mixer = mixer_weights(l + 1, gathered)
    dx, loss_blk, dfnw = _loss_head(xs, loss_target[0], final_norm_w.reshape(1, D_MODEL))
    loss_local = loss_blk[0, 0]

    big_names = ["w_in", "w_out", "w_up", "w_down"]
    core = jnp.reshape(ci, (1,)).astype(jnp.int32)
    place = jnp.stack([ci, chip]).astype(jnp.int32)
    layer_grads = [None] * L

    big = {nm: [lax.empty(W[nm].shape, F32) for _ in range(4)] for nm in big_names}

    def pair_sums(names, parts):
        from_sibling = _send_to_sibling(parts, "pair_send")
        return [_pair_add(p, r, core, "pair_add_" + nm) for nm, p, r in zip(names, parts, from_sibling)]

    def apply_update(l, names, arrived, pair):
        halves = [_chip_sum(a, own, place, "chip_sum_" + nm) for nm, a, own in zip(names, arrived, pair)]
        full = _swap_row_halves(halves, "pair_swap")
        for nm, gr in zip(names, full):
            big[nm] = _adam_layer(gr, W[nm], M[nm], V[nm], big[nm], l, "adam_" + nm)

    win_pair = []
    for l in reversed(range(L)):
        early_pair = []

        def early(parts, early_pair=early_pair):
            early_pair.extend(pair_sums(big_names[1:], parts))
            return early_pair

        dx, gl, arrived = _layer_bwd(dx, layers[l], res[l], sharded=True, scatter=win_pair, early=early)
        if win_pair:
            apply_update(l + 1, big_names[:1], arrived[:1], win_pair)
        apply_update(l, big_names[1:], arrived[len(win_pair):], early_pair)
        layer_grads[l] = gl
        gwin = gl["win"][:, :IN_COLS].reshape(2, D_MODEL // 2, N_CHIPS, IN_COLS // N_CHIPS).transpose(0, 2, 1, 3)
        win_pair = pair_sums(big_names[:1], [gwin])
    apply_update(0, big_names[:1], _scatter_chips(win_pair, "chip_scatter"), win_pair)
    small_keys = [k for k in layer_grads[0] if k not in ("win", "wout", "wup", "wdown")]
    g = {k: jnp.stack([gl[k] for gl in layer_grads]) for k in small_keys}
    loss = lax.psum(loss_local, ("x", "y", "c"))

    dmod = jnp.concatenate([g["sh1"], g["sc1"], g["g1"], g["sh2"], g["sc2"], g["g2"]], axis=-1)
    small = [dmod, g["nmw"], g["nmlp"], g["lcw"][:, :4], g["lcb"], jax.vmap(_diag_blocks)(g["wa"]), g["ba"],
             jax.vmap(_diag_blocks)(g["wx"]), g["bx"], g["lam"], g["lnw"], g["gcw"][:, :4], g["alog"], g["dtb"],
             g["gnw"], dfnw]
    small_shapes = [(L, N_MOD * D_MODEL), (L, D_MODEL), (L, D_MODEL), (L, 4, LRU_W), (L, LRU_W),
                    (L, LRU_BLOCKS, LRU_BLOCK, LRU_BLOCK), (L, LRU_W), (L, LRU_BLOCKS, LRU_BLOCK, LRU_BLOCK),
                    (L, LRU_W), (L, LRU_W), (L, LRU_W), (L, 4, 3 * GDN_W), (L, LANES), (L, LANES), (L, LANES),
                    (D_MODEL,)]
    small_names = ["b_mod", "norm_mix_w", "norm_mlp_w", None, "lru_conv_b", "lru_gate_a_w", "lru_gate_a_b",
                   "lru_gate_x_w", "lru_gate_x_b", "lru_lambda", "lru_norm_w", None, "gdn_a_log", "gdn_dt_bias",
                   "gdn_norm_w", "final_norm_w"]
    pack_g = _pack_rows(small, 512)
    rows = pack_g.shape[0]
    all_g = _all_gather_rows(pack_g, "gather_small_grads").reshape(N_DEV, rows, LANES)
    tot = _sum_slots(all_g, "sum_small_grads")
    tot_parts = _unpack(tot, small_shapes)

    def pack_state(S_):
        parts = []
        for nm, shp in zip(small_names, small_shapes):
            if nm is None:
                parts.append(jnp.zeros(shp, F32))
            elif nm in ("gdn_a_log", "gdn_dt_bias"):
                parts.append(_lane_pad(S_[nm]))
            else:
                parts.append(S_[nm])
        return _pack_rows(parts, 512)

    upd = _adam(pack_state(W), tot, pack_state(M), pack_state(V), "adam_small")
    upd_parts = [_unpack(u, small_shapes) for u in upd]

    grads, deltas, new_m, new_v = {}, {}, {}, {}
    for k, nm in enumerate(small_names):
        if nm is None:
            continue
        cut = (lambda a: a[:, :HEADS]) if nm in ("gdn_a_log", "gdn_dt_bias") else (lambda a: a)
        grads[nm] = cut(tot_parts[k])
        deltas[nm], new_m[nm], new_v[nm] = (cut(u[k]) for u in upd_parts)

    g_lcw = lax.dynamic_slice(tot_parts[3], (0, 0, chip * lcs), (L, 4, lcs))
    g_gcw = lax.dynamic_slice(tot_parts[11], (0, 0, chip * gcs), (L, 4, gcs))
    conv_shapes = [(L, 4, lcs), (L, 4, gcs)]
    conv_pack = lambda a, b: _pack_rows([a, b], SUBLANES)
    cu = _adam(conv_pack(lru_conv_w, gdn_conv_w), conv_pack(g_lcw, g_gcw), conv_pack(m_lru_conv_w, m_gdn_conv_w),
               conv_pack(v_lru_conv_w, v_gdn_conv_w), "adam_conv")
    cu_parts = [_unpack(u, conv_shapes) for u in cu]
    for k, nm in enumerate(("lru_conv_w", "gdn_conv_w")):
        grads[nm] = (g_lcw, g_gcw)[k]
        deltas[nm], new_m[nm], new_v[nm] = (u[k] for u in cu_parts)

    dmod_all = all_g[:, :L * N_MOD * D_MODEL // LANES].reshape(N_DEV, L, N_MOD * D_MODEL)
    dmod_cols = lax.dynamic_slice(dmod_all, (0, 0, chip * mcs), (N_DEV, L, mcs)).transpose(1, 0, 2)
    grads["w_mod"], deltas["w_mod"], new_m["w_mod"], new_v["w_mod"] = _mod_update(c_all, dmod_cols, w_mod, m_w_mod, v_w_mod)

    for nm in big_names:
        grads[nm], deltas[nm], new_m[nm], new_v[nm] = big[nm]

    out = [loss, dx[None]]
    for group in (grads, deltas, new_m, new_v):
        out += [group[nm].reshape(W[nm].shape) for nm in WEIGHT_NAMES]
    return tuple(out)
```

```python
import functools

import jax
import jax.numpy as jnp
from jax import lax
from jax.experimental import pallas as pl
from jax.experimental.pallas import tpu as pltpu

F32 = jnp.float32
BF16 = jnp.bfloat16
MESH = pl.DeviceIdType.MESH

D_MODEL = 1024
DEPTH = 4
LRU_W = 512
LRU_BLOCKS = 8
LRU_BLOCK = 64
LRU_C = 8.0
HEADS = 4
HEAD_DIM = 128
GDN_W = 512
CHUNK = 128
D_FF = 4096
N_MOD = 6
IN_COLS = 3080
IN_PAD = 3200
NORM_EPS = 1e-6
LANES = 128
SUBLANES = 8
N_DEV = 8
N_CHIPS = 4

ADAM_LR = 0.001
ADAM_B1 = 0.9
ADAM_B2 = 0.999
ADAM_EPS = 1e-08
ADAM_WD = 0.01
ADAM_STEP = 10

VMEM_LIMIT = 56 * 1024 * 1024
HI = lax.Precision.HIGHEST


def _sds(shape, dtype=F32):
    return jax.ShapeDtypeStruct(tuple(shape), dtype)


def _params(sem=None, vmem=VMEM_LIMIT):
    return pltpu.CompilerParams(dimension_semantics=sem, vmem_limit_bytes=vmem)


def _const(shape):
    return pl.BlockSpec(tuple(shape), lambda *_: (0,) * len(shape))


def _row(tm, c, col=0):
    return pl.BlockSpec((tm, c), lambda i: (i, col))


def _dot(a, b):
    return jnp.dot(a, b, preferred_element_type=F32)


def _dot_nt(a, b):
    return lax.dot_general(a, b, (((1,), (1,)), ((), ())), preferred_element_type=F32)


def _dot_tn(a, b):
    return lax.dot_general(a, b, (((0,), (0,)), ((), ())), preferred_element_type=F32)


def _hdot(a, b):
    return jnp.dot(a, b, preferred_element_type=F32, precision=HI)


def _hdot_nt(a, b):
    return lax.dot_general(a, b, (((1,), (1,)), ((), ())), preferred_element_type=F32, precision=HI)


def _hdot_tn(a, b):
    return lax.dot_general(a, b, (((0,), (0,)), ((), ())), preferred_element_type=F32, precision=HI)


_DIMS = {"nn": (((1,), (0,)), ((), ())), "nt": (((1,), (1,)), ((), ())), "tn": (((0,), (0,)), ((), ()))}


def _mm_raw(a, b, dims, passes):
    dn = _DIMS[dims]

    def dot(p, q):
        return lax.dot_general(p, q, dn, preferred_element_type=F32)

    a_hi = a.astype(BF16)
    b_hi = b.astype(BF16)
    if passes == 1:
        return dot(a_hi, b_hi)
    a_lo = (a - a_hi.astype(F32)).astype(BF16)
    b_lo = (b - b_hi.astype(F32)).astype(BF16)
    return dot(a_hi, b_hi) + (dot(a_hi, b_lo) + dot(a_lo, b_hi))


@functools.partial(jax.custom_vjp, nondiff_argnums=(2, 3))
def _mm(a, b, dims, passes):
    return _mm_raw(a, b, dims, passes)


def _mm_fwd(a, b, dims, passes):
    return _mm_raw(a, b, dims, passes), (a, b)


def _mm_bwd(dims, passes, res, ct):
    a, b = res
    if dims == "nn":
        return _mm_raw(ct, b, "nt", passes), _mm_raw(a, ct, "tn", passes)
    if dims == "nt":
        return _mm_raw(ct, b, "nn", passes), _mm_raw(ct, a, "tn", passes)
    return _mm_raw(b, ct, "nt", passes), _mm_raw(a, ct, "nn", passes)


_mm.defvjp(_mm_fwd, _mm_bwd)


def _acc(ref, val, first):
    @pl.when(first)
    def _():
        ref[...] = val

    @pl.when(jnp.logical_not(first))
    def _():
        ref[...] += val


def _colsum(v):
    return jnp.sum(v, axis=0, keepdims=True)


def _rms_parts(x):
    r = lax.rsqrt(jnp.mean(x * x, axis=-1, keepdims=True) + NORM_EPS)
    return x * r, r


def _rms_bwd(dy, xh, r, w):
    dxh = dy * w
    dw = _colsum(dy * xh)
    dx = r * (dxh - xh * jnp.mean(dxh * xh, axis=-1, keepdims=True))
    return dx, dw


def _norm_mod(x, w, sc, sh):
    xh, _ = _rms_parts(x)
    return (xh * w) * (1.0 + sc) + sh


def _norm_mod_bwd(dy, x, w, sc):
    xh, r = _rms_parts(x)
    n = xh * w
    dsh = _colsum(dy)
    dsc = _colsum(dy * n)
    dx, dw = _rms_bwd(dy * (1.0 + sc), xh, r, w)
    return dx, dw, dsc, dsh


def _softplus(x):
    return jnp.maximum(x, 0.0) + jnp.log1p(jnp.exp(-jnp.abs(x)))


def _silu(x):
    return x * jax.nn.sigmoid(x)


def _silu_grad(x):
    s = jax.nn.sigmoid(x)
    return s * (1.0 + x * (1.0 - s))


def _roll_dn(x, d):
    return x if d == 0 else pltpu.roll(x, d, 0)


def _roll_up(x, d):
    return x if d == 0 else pltpu.roll(x, x.shape[0] - d, 0)


def _proj_fwd(x, nw, sc, sh, win):
    S = x.shape[0]
    tm = min(512, S)

    def body(x_ref, nw_ref, sc_ref, sh_ref, w_ref, proj_ref, hb_ref):
        hb = _norm_mod(x_ref[...], nw_ref[...], sc_ref[...], sh_ref[...]).astype(BF16)
        hb_ref[...] = hb
        proj_ref[...] = _dot(hb, w_ref[...])

    vec = _const((1, D_MODEL))
    return pl.pallas_call(
        body, name="proj_fwd", grid=(S // tm,),
        in_specs=[_row(tm, D_MODEL), vec, vec, vec, _const((D_MODEL, IN_PAD))],
        out_specs=[_row(tm, IN_PAD), _row(tm, D_MODEL)],
        out_shape=[_sds((S, IN_PAD)), _sds((S, D_MODEL), BF16)],
        compiler_params=_params(("arbitrary",)),
    )(x, nw, sc, sh, win)


def _proj_bwd(dx1, x, dlx, dly, dqkv, dz, dba, nw, sc, win):
    S = x.shape[0]
    tm = min(512, S)

    def body(dx1_ref, x_ref, dlx_ref, dly_ref, dqkv_ref, dz_ref, dba_ref, nw_ref, sc_ref, w_ref,
             dx_ref, dpb_ref, dnw_ref, dsc_ref, dsh_ref):
        i = pl.program_id(0)
        dpb = jnp.concatenate([dlx_ref[...], dly_ref[...], dqkv_ref[...], dz_ref[...], dba_ref[...]],
                              axis=-1).astype(BF16)
        dpb_ref[...] = dpb
        dh = _dot_nt(dpb, w_ref[...])
        dx, dnw, dsc, dsh = _norm_mod_bwd(dh, x_ref[...], nw_ref[...], sc_ref[...])
        dx_ref[...] = dx1_ref[...] + dx
        _acc(dnw_ref, dnw, i == 0)
        _acc(dsc_ref, dsc, i == 0)
        _acc(dsh_ref, dsh, i == 0)

    vec = _const((1, D_MODEL))
    return pl.pallas_call(
        body, name="proj_bwd", grid=(S // tm,),
        in_specs=[_row(tm, D_MODEL), _row(tm, D_MODEL), _row(tm, LRU_W), _row(tm, LRU_W), _row(tm, 3 * GDN_W),
                  _row(tm, GDN_W), _row(tm, LANES), vec, vec,
                  _const((D_MODEL, IN_PAD))],
        out_specs=[_row(tm, D_MODEL), _row(tm, IN_PAD), vec, vec, vec],
        out_shape=[_sds((S, D_MODEL)), _sds((S, IN_PAD), BF16), _sds((1, D_MODEL)), _sds((1, D_MODEL)),
                   _sds((1, D_MODEL))],
        compiler_params=_params(("arbitrary",)),
    )(dx1, x, dlx, dly, dqkv, dz, dba, nw, sc, win)


def _conv_taps(xx, w, tm):
    y = _roll_dn(xx, 3)[SUBLANES:] * w[0:1]
    y = y + _roll_dn(xx, 2)[SUBLANES:] * w[1:2]
    y = y + _roll_dn(xx, 1)[SUBLANES:] * w[2:3]
    y = y + xx[SUBLANES:] * w[3:4]
    return y


def _conv_fwd(src, col0, C, w8, b, act, name):
    S = src.shape[0]
    tm = min(512, S)
    tc = 512
    hb = tm // SUBLANES
    cb0 = col0 // tc

    def body(x_ref, p_ref, w_ref, b_ref, y_ref):
        i = pl.program_id(0)
        prev = jnp.where(i > 0, p_ref[...], 0.0)
        xx = jnp.concatenate([prev, x_ref[...]], axis=0)
        y = _conv_taps(xx, w_ref[...], tm) + b_ref[...]
        y_ref[...] = _silu(y) if act else y

    return pl.pallas_call(
        body, name=name, grid=(S // tm, C // tc),
        in_specs=[pl.BlockSpec((tm, tc), lambda i, j: (i, cb0 + j)),
                  pl.BlockSpec((SUBLANES, tc), lambda i, j: (jnp.maximum(i * hb - 1, 0), cb0 + j)),
                  pl.BlockSpec((SUBLANES, tc), lambda i, j: (0, j)),
                  pl.BlockSpec((1, tc), lambda i, j: (0, j))],
        out_specs=pl.BlockSpec((tm, tc), lambda i, j: (i, j)),
        out_shape=_sds((S, C)),
        compiler_params=_params(("arbitrary", "arbitrary")),
    )(src, src, w8, b)


def _conv_bwd(src, col0, C, w8, b, dyact, act, name):
    S = src.shape[0]
    tm = min(512, S)
    tc = 512
    hb = tm // SUBLANES
    nt = S // tm
    cb0 = col0 // tc
    last_hb = S // SUBLANES - 1

    def body(x_ref, p_ref, n_ref, dy_ref, dyn_ref, w_ref, b_ref, dx_ref, dw_ref, db_ref):
        i = pl.program_id(1)
        w = w_ref[...]
        prev = jnp.where(i > 0, p_ref[...], 0.0)
        xx = jnp.concatenate([prev, x_ref[...], n_ref[...]], axis=0)
        dy = jnp.concatenate([dy_ref[...], jnp.where(i < nt - 1, dyn_ref[...], 0.0)], axis=0)
        if act:
            ypre = _conv_taps(xx, w, tm + SUBLANES) + b_ref[...]
            dy = dy * _silu_grad(ypre)
        dx = dy[:tm] * w[3:4]
        for d in (1, 2, 3):
            dx = dx + _roll_up(dy, d)[:tm] * w[3 - d:4 - d]
        dx_ref[...] = dx
        xt = xx[:tm + SUBLANES]
        dyt = dy[:tm]
        rows = [_colsum(dyt * _roll_dn(xt, 3 - k)[SUBLANES:]) for k in range(4)]
        dw = jnp.concatenate(rows + [jnp.zeros((SUBLANES - 4, tc), F32)], axis=0)
        _acc(dw_ref, dw, i == 0)
        _acc(db_ref, _colsum(dyt), i == 0)

    return pl.pallas_call(
        body, name=name, grid=(C // tc, nt),
        in_specs=[pl.BlockSpec((tm, tc), lambda j, i: (i, cb0 + j)),
                  pl.BlockSpec((SUBLANES, tc), lambda j, i: (jnp.maximum(i * hb - 1, 0), cb0 + j)),
                  pl.BlockSpec((SUBLANES, tc), lambda j, i: (jnp.minimum((i + 1) * hb, last_hb), cb0 + j)),
                  pl.BlockSpec((tm, tc), lambda j, i: (i, j)),
                  pl.BlockSpec((SUBLANES, tc), lambda j, i: (jnp.minimum((i + 1) * hb, last_hb), j)),
                  pl.BlockSpec((SUBLANES, tc), lambda j, i: (0, j)),
                  pl.BlockSpec((1, tc), lambda j, i: (0, j))],
        out_specs=[pl.BlockSpec((tm, tc), lambda j, i: (i, j)),
                   pl.BlockSpec((SUBLANES, tc), lambda j, i: (0, j)),
                   pl.BlockSpec((1, tc), lambda j, i: (0, j))],
        out_shape=[_sds((S, C)), _sds((SUBLANES, C)), _sds((1, C))],
        compiler_params=_params(("arbitrary", "arbitrary")),
    )(src, src, src, dyact, dyact, w8, b)


def _lru_ab(pre_a, pre_x, xr, lam):
    r = jax.nn.sigmoid(pre_a)
    g = jax.nn.sigmoid(pre_x)
    log_sig = -_softplus(-lam)
    log_a = LRU_C * r * log_sig
    a = jnp.exp(log_a)
    t = jnp.tanh(log_a)
    mult = jnp.sqrt(jnp.maximum(-2.0 * t / (1.0 - t), 1e-12))
    return a, mult * (g * xr)


def _lru_tail(h, ly, lnw):
    xh, _ = _rms_parts(h * jax.nn.gelu(ly))
    return xh * lnw


def _scan_down(a, b):
    n = a.shape[0]
    row = lax.broadcasted_iota(jnp.int32, a.shape, 0)
    d = 1
    while d < n:
        keep = row >= d
        a_s = jnp.where(keep, _roll_dn(a, d), 1.0)
        b_s = jnp.where(keep, _roll_dn(b, d), 0.0)
        b = a * b_s + b
        a = a * a_s
        d *= 2
    return a, b


def _scan_up(a, b):
    n = a.shape[0]
    row = lax.broadcasted_iota(jnp.int32, a.shape, 0)
    d = 1
    while d < n:
        keep = row < n - d
        a_s = jnp.where(keep, _roll_up(a, d), 1.0)
        b_s = jnp.where(keep, _roll_up(b, d), 0.0)
        b = a * b_s + b
        a = a * a_s
        d *= 2
    return a, b


LRU_TM = 256


def _lru_fwd(xr, proj, wa, ba, wx, bx, lam, lnw):
    S = xr.shape[0]
    tm = min(LRU_TM, S)

    def body(xr_ref, ly_ref, wa_ref, ba_ref, wx_ref, bx_ref, lam_ref, lnw_ref, out_ref, h_ref, carry):
        i = pl.program_id(0)

        @pl.when(i == 0)
        def _():
            carry[...] = jnp.zeros_like(carry)

        x = xr_ref[...]
        xb = x.astype(BF16)
        pre_a = _dot(xb, wa_ref[...]) + ba_ref[...]
        pre_x = _dot(xb, wx_ref[...]) + bx_ref[...]
        a, b = _lru_ab(pre_a, pre_x, x, lam_ref[...])
        ca, hl = _scan_down(a, b)
        h = hl + ca * carry[0:1, :]
        carry[0:1, :] = h[tm - 1:tm, :]
        h_ref[...] = h
        out_ref[...] = _lru_tail(h, ly_ref[...], lnw_ref[...])

    vec = _const((1, LRU_W))
    mat = _const((LRU_W, LRU_W))
    return pl.pallas_call(
        body, name="lru_fwd", grid=(S // tm,),
        in_specs=[_row(tm, LRU_W), _row(tm, LRU_W, 1), mat, vec, mat, vec, vec, vec],
        out_specs=[_row(tm, LRU_W), _row(tm, LRU_W)],
        out_shape=[_sds((S, LRU_W)), _sds((S, LRU_W))],
        scratch_shapes=[pltpu.VMEM((SUBLANES, LRU_W), F32)],
        compiler_params=_params(("arbitrary",)),
    )(xr, proj, wa, ba, wx, bx, lam, lnw)


def _lru_bwd(dout, xr, proj, h, wa, ba, wx, bx, lam, lnw):
    S = xr.shape[0]
    tm = min(LRU_TM, S)
    nt = S // tm
    hb = tm // SUBLANES

    def rev(col=0):
        return pl.BlockSpec((tm, LRU_W), lambda i: (nt - 1 - i, col))

    def body(dout_ref, xr_ref, ly_ref, h_ref, hp_ref, wa_ref, ba_ref, wx_ref, bx_ref, lam_ref, lnw_ref,
             dxr_ref, dly_ref, dwa_ref, dba_ref, dwx_ref, dbx_ref, dlam_ref, dlnw_ref, carry):
        i = pl.program_id(0)
        first = i == 0

        @pl.when(first)
        def _():
            carry[...] = jnp.zeros_like(carry)

        x = xr_ref[...]
        xb = x.astype(BF16)
        pre_a = _dot(xb, wa_ref[...]) + ba_ref[...]
        pre_x = _dot(xb, wx_ref[...]) + bx_ref[...]
        (a, b), ab_vjp = jax.vjp(_lru_ab, pre_a, pre_x, x, lam_ref[...])
        h_t = h_ref[...]
        _, tail_vjp = jax.vjp(_lru_tail, h_t, ly_ref[...], lnw_ref[...])
        dh, dly, dlnw = tail_vjp(dout_ref[...])
        dly_ref[...] = dly
        row = lax.broadcasted_iota(jnp.int32, a.shape, 0)
        a_next = jnp.where(row == tm - 1, carry[0:1, :], _roll_up(a, 1))
        ca, gl = _scan_up(a_next, dh)
        g = gl + ca * carry[1:2, :]
        carry[0:1, :] = a[0:1, :]
        carry[1:2, :] = g[0:1, :]
        h_before = jnp.where(i == nt - 1, 0.0, hp_ref[SUBLANES - 1:SUBLANES, :])
        h_prev = jnp.where(row == 0, h_before, _roll_dn(h_t, 1))
        dpa, dpx, dx, dlam = ab_vjp((g * h_prev, g))
        dpab = dpa.astype(BF16)
        dpxb = dpx.astype(BF16)
        dxr_ref[...] = dx + _dot_nt(dpab, wa_ref[...]) + _dot_nt(dpxb, wx_ref[...])
        _acc(dwa_ref, _dot_tn(xb, dpab), first)
        _acc(dwx_ref, _dot_tn(xb, dpxb), first)
        _acc(dba_ref, _colsum(dpa), first)
        _acc(dbx_ref, _colsum(dpx), first)
        _acc(dlam_ref, dlam, first)
        _acc(dlnw_ref, dlnw, first)

    vec = _const((1, LRU_W))
    mat = _const((LRU_W, LRU_W))
    return pl.pallas_call(
        body, name="lru_bwd", grid=(nt,),
        in_specs=[rev(), rev(), rev(1), rev(),
                  pl.BlockSpec((SUBLANES, LRU_W), lambda i: (jnp.maximum((nt - 1 - i) * hb - 1, 0), 0)),
                  mat, vec, mat, vec, vec, vec],
        out_specs=[rev(), rev(), mat, vec, mat, vec, vec, vec],
        out_shape=[_sds((S, LRU_W)), _sds((S, LRU_W)), _sds((LRU_W, LRU_W)), _sds((1, LRU_W)),
                   _sds((LRU_W, LRU_W)), _sds((1, LRU_W)), _sds((1, LRU_W)), _sds((1, LRU_W))],
        scratch_shapes=[pltpu.VMEM((SUBLANES, LRU_W), F32)],
        compiler_params=_params(("arbitrary",)),
    )(dout, xr, proj, h, h, wa, ba, wx, bx, lam, lnw)


def _lane_pick(row_or_tile, lane):
    idx = lax.broadcasted_iota(jnp.int32, row_or_tile.shape, 1)
    return jnp.sum(jnp.where(idx == lane, row_or_tile, 0.0), axis=-1, keepdims=True)


def _unit_lower_inverses(los):
    n = los[0].shape[0]
    ri = lax.broadcasted_iota(jnp.int32, (n, n), 0)
    ci = lax.broadcasted_iota(jnp.int32, (n, n), 1)
    eye = (ri == ci).astype(F32)

    def lower_left_of(s):
        same_block = (ri & ~(2 * s - 1)) == (ci & ~(2 * s - 1))
        return same_block & ((ri & s) != 0) & ((ci & s) == 0)

    invs = [eye - jnp.where(lower_left_of(1), lo, 0.0) for lo in los]
    s = 2
    while s < n:
        lower_left = lower_left_of(s)
        left = [_mm_raw(inv, jnp.where(lower_left, lo, 0.0), "nn", 3) for inv, lo in zip(invs, los)]
        invs = [inv - _mm_raw(t, inv, "nn", 3) for inv, t in zip(invs, left)]
        s *= 2
    return invs


@jax.custom_vjp
def _unit_lower_inverses_diff(los):
    return _unit_lower_inverses(los)


def _unit_lower_inverses_fwd(los):
    invs = _unit_lower_inverses(los)
    return invs, invs


def _unit_lower_inverses_bwd(invs, cts):
    right = [_mm_raw(ct, inv, "nt", 3) for ct, inv in zip(cts, invs)]
    return ([-_mm_raw(inv, r, "tn", 3) for inv, r in zip(invs, right)],)


_unit_lower_inverses_diff.defvjp(_unit_lower_inverses_fwd, _unit_lower_inverses_bwd)


GDN_STEP_CHUNKS = 2


def _gdn_chunk(qs, ks, vs, bas, alog, dtb, states, inverses=_unit_lower_inverses, mm=_mm_raw):
    C = qs[0].shape[0]
    nchunks = len(bas)
    items = [(c, h) for c in range(nchunks) for h in range(HEADS)]
    ri = lax.broadcasted_iota(jnp.int32, (C, C), 0)
    ci = lax.broadcasted_iota(jnp.int32, (C, C), 1)
    causal = ri >= ci
    strict = ri > ci
    tri = causal.astype(F32)
    betas = [jax.nn.sigmoid(_lane_pick(bas[c], h)) for c, h in items]
    gs = [-jnp.exp(_lane_pick(alog, h)) * _softplus(_lane_pick(bas[c], h + HEADS) + _lane_pick(dtb, h))
          for c, h in items]
    qn = [q * lax.rsqrt(jnp.sum(q * q, axis=-1, keepdims=True) + 1e-6) * (HEAD_DIM ** -0.5) for q in qs]
    kn = [k * lax.rsqrt(jnp.sum(k * k, axis=-1, keepdims=True) + 1e-6) for k in ks]
    gc = [_hdot(tri, jnp.broadcast_to(g, (C, C))) for g in gs]
    decay = [jnp.where(causal, jnp.exp(jnp.where(causal, c - c.T, 0.0)), 0.0) for c in gc]
    eg = [jnp.exp(c) for c in gc]
    kb = [k * b for k, b in zip(kn, betas)]
    vb = [v * b for v, b in zip(vs, betas)]
    los = [jnp.where(strict, mm(a, k, "nt", 1) * d, 0.0) for a, k, d in zip(kb, kn, decay)]
    attn = [jnp.where(causal, mm(q, k, "nt", 1) * d, 0.0) for q, k, d in zip(qn, kn, decay)]
    tinv = inverses(los)
    u = [mm(t, x, "nn", 3) for t, x in zip(tinv, vb)]
    w = [mm(t, a * e, "nn", 3) for t, a, e in zip(tinv, kb, eg)]
    g_last = [c[C - 1:C, :] for c in gc]
    k_tail = [k * jnp.exp(gl - c) for k, gl, c in zip(kn, g_last, gc)]
    q_dec = [q * e for q, e in zip(qn, eg)]
    outs = []
    for c in range(nchunks):
        idx = range(c * HEADS, (c + 1) * HEADS)
        v_new = [u[i] - mm(w[i], s, "nn", 1) for i, s in zip(idx, states)]
        o_state = [mm(q_dec[i], s, "nn", 1) for i, s in zip(idx, states)]
        outs += [a + mm(attn[i], vn, "nn", 1) for i, a, vn in zip(idx, o_state, v_new)]
        states = [s * jnp.exp(g_last[i]) + mm(k_tail[i], vn, "tn", 1) for i, s, vn in zip(idx, states, v_new)]
    return outs, states


def _gdn_fwd(qkv, proj, alog, dtb, gather=()):
    S = qkv.shape[0]
    per = min(GDN_STEP_CHUNKS, S // CHUNK)
    T = per * CHUNK
    nc = S // T
    nk = len(gather)
    assert CHUNK == HEAD_DIM

    def body(*refs):
        q_ref, k_ref, v_ref, ba_ref, alog_ref, dtb_ref = refs[:6]
        o_ref, st_ref = refs[6 + nk:8 + nk]
        state = refs[8 + 2 * nk]
        if nk:
            start, finish = _gather_steps(refs[6:6 + nk], refs[8 + nk:8 + 2 * nk], *refs[9 + 2 * nk:])
            pl.when(pl.program_id(0) == 0)(start)

        @pl.when(pl.program_id(0) == 0)
        def _():
            state[...] = jnp.zeros_like(state)

        sls = [slice(hd * HEAD_DIM, (hd + 1) * HEAD_DIM) for hd in range(HEADS)]
        rows = [slice(c * CHUNK, (c + 1) * CHUNK) for c in range(per)]
        s0 = [state[hd] for hd in range(HEADS)]
        for hd in range(HEADS):
            st_ref[hd, 0] = s0[hd]
        o, s1 = _gdn_chunk([q_ref[r, sl] for r in rows for sl in sls], [k_ref[r, sl] for r in rows for sl in sls],
                           [v_ref[r, sl] for r in rows for sl in sls], [ba_ref[r, :] for r in rows],
                           alog_ref[...], dtb_ref[...], s0)
        for c, r in enumerate(rows):
            for hd in range(HEADS):
                o_ref[r, sls[hd]] = o[c * HEADS + hd]
        for hd in range(HEADS):
            state[hd] = s1[hd]
        if nk:
            pl.when(pl.program_id(0) == nc - 1)(finish)

    def col(j):
        return pl.BlockSpec((T, GDN_W),lambda n: (n, j))

    vec = _const((1, LANES))
    outs = pl.pallas_call(
        body, name="gdn_fwd", grid=(nc,),
        in_specs=[col(0), col(1), col(2), pl.BlockSpec((T, LANES),lambda n: (n, IN_PAD // LANES - 1)), vec, vec]
        + _hbm_specs(nk),
        out_specs=[col(0), pl.BlockSpec((HEADS, 1, HEAD_DIM, HEAD_DIM), lambda n: (0, n, 0, 0))] + _hbm_specs(nk),
        out_shape=[_sds((S, GDN_W)), _sds((HEADS, nc, HEAD_DIM, HEAD_DIM))] + (_gather_out_shapes(gather) if nk else []),
        scratch_shapes=[pltpu.VMEM((HEADS, HEAD_DIM, HEAD_DIM), F32)] + (_gather_scratch(nk) if nk else []),
        compiler_params=_params(("arbitrary",)),
    )(qkv, qkv, qkv, proj, alog, dtb, *gather)
    return outs[0], outs[1], list(outs[2:])


def _gdn_bwd(do, qkv, proj, states, alog, dtb, scatter=()):
    S = qkv.shape[0]
    per = min(GDN_STEP_CHUNKS, S // CHUNK)
    T = per * CHUNK
    nc = S // T
    nk = len(scatter)

    def body(*refs):
        do_ref, q_ref, k_ref, v_ref, ba_ref, st_ref, alog_ref, dtb_ref = refs[:8]
        dqkv_ref, dba_ref, dalog_ref, ddtb_ref = refs[8 + nk:12 + nk]
        dstate = refs[12 + 2 * nk]
        n = pl.program_id(0)
        if nk:
            start, finish = _scatter_steps(refs[8:8 + nk], refs[12 + nk:12 + 2 * nk], *refs[13 + 2 * nk:])
            pl.when(n == 0)(start)

        @pl.when(n == 0)
        def _():
            dstate[...] = jnp.zeros_like(dstate)

        sls = [slice(hd * HEAD_DIM, (hd + 1) * HEAD_DIM) for hd in range(HEADS)]
        rows = [slice(c * CHUNK, (c + 1) * CHUNK) for c in range(per)]
        fn = functools.partial(_gdn_chunk, inverses=_unit_lower_inverses_diff, mm=_mm)
        _, vjp = jax.vjp(fn, [q_ref[r, sl] for r in rows for sl in sls], [k_ref[r, sl] for r in rows for sl in sls],
                         [v_ref[r, sl] for r in rows for sl in sls], [ba_ref[r, :] for r in rows],
                         alog_ref[...], dtb_ref[...], [st_ref[hd, 0] for hd in range(HEADS)])
        dq, dk, dv, dba, dalog, ddtb, ds = vjp(([do_ref[r, sl] for r in rows for sl in sls],
                                                [dstate[hd] for hd in range(HEADS)]))
        for c, r in enumerate(rows):
            for hd in range(HEADS):
                i = c * HEADS + hd
                dqkv_ref[r, sls[hd]] = dq[i]
                dqkv_ref[r, GDN_W + hd * HEAD_DIM:GDN_W + (hd + 1) * HEAD_DIM] = dk[i]
                dqkv_ref[r, 2 * GDN_W + hd * HEAD_DIM:2 * GDN_W + (hd + 1) * HEAD_DIM] = dv[i]
            dba_ref[r, :] = dba[c]
        for hd in range(HEADS):
            dstate[hd] = ds[hd]
        _acc(dalog_ref, dalog, n == 0)
        _acc(ddtb_ref, ddtb, n == 0)
        if nk:
            pl.when(n == nc - 1)(finish)

    def col(j):
        return pl.BlockSpec((T, GDN_W),lambda n: (nc - 1 - n, j))

    vec = _const((1, LANES))
    outs = pl.pallas_call(
        body, name="gdn_bwd", grid=(nc,),
        in_specs=[col(0), col(0), col(1), col(2),
                  pl.BlockSpec((T, LANES),lambda n: (nc - 1 - n, IN_PAD // LANES - 1)),
                  pl.BlockSpec((HEADS, 1, HEAD_DIM, HEAD_DIM), lambda n: (0, nc - 1 - n, 0, 0)), vec, vec]
        + _hbm_specs(nk),
        out_specs=[pl.BlockSpec((T, 3 * GDN_W),lambda n: (nc - 1 - n, 0)),
                   pl.BlockSpec((T, LANES),lambda n: (nc - 1 - n, 0)), vec, vec] + _hbm_specs(nk),
        out_shape=[_sds((S, 3 * GDN_W)), _sds((S, LANES)), _sds((1, LANES)), _sds((1, LANES))]
        + [_sds(p.shape, p.dtype) for p in scatter],
        scratch_shapes=[pltpu.VMEM((HEADS, HEAD_DIM, HEAD_DIM), F32)] + (_scatter_scratch(nk) if nk else []),
        compiler_params=_params(("arbitrary",)),
    )(do, qkv, qkv, qkv, proj, states, alog, dtb, *scatter)
    return outs[0], outs[1], outs[2], outs[3], list(outs[4:])


def _gdn_gate(o, z, gnw):
    outs = []
    for hd in range(HEADS):
        sl = slice(hd * HEAD_DIM, (hd + 1) * HEAD_DIM)
        xh, _ = _rms_parts(o[:, sl])
        outs.append(xh * gnw * _silu(z[:, sl]))
    return jnp.concatenate(outs, axis=-1)


def _out_fwd(x, out_lru, o, proj, gnw, g1, wout):
    S = x.shape[0]
    tm = min(512, S)

    def body(x_ref, lru_ref, o_ref, z_ref, gnw_ref, g1_ref, w_ref, x1_ref, cat_ref):
        cat = jnp.concatenate([lru_ref[...], _gdn_gate(o_ref[...], z_ref[...], gnw_ref[...])], axis=-1).astype(BF16)
        cat_ref[...] = cat
        x1_ref[...] = x_ref[...] + g1_ref[...] * _dot(cat, w_ref[...])

    return pl.pallas_call(
        body, name="out_fwd", grid=(S // tm,),
        in_specs=[_row(tm, D_MODEL), _row(tm, LRU_W), _row(tm, GDN_W), _row(tm, GDN_W, 5), _const((1, LANES)),
                  _const((1, D_MODEL)), _const((D_MODEL, D_MODEL))],
        out_specs=[_row(tm, D_MODEL), _row(tm, D_MODEL)],
        out_shape=[_sds((S, D_MODEL)), _sds((S, D_MODEL), BF16)],
        compiler_params=_params(("arbitrary",)),
    )(x, out_lru, o, proj, gnw, g1, wout)


def _out_bwd(dx1, cat, o, proj, gnw, g1, wout):
    S = dx1.shape[0]
    tm = min(512, S)

    def body(dx1_ref, cat_ref, o_ref, z_ref, gnw_ref, g1_ref, w_ref,
             dlru_ref, do_ref, dz_ref, dmb_ref, dgnw_ref, dg1_ref):
        i = pl.program_id(0)
        d1 = dx1_ref[...]
        mix = _dot(cat_ref[...], w_ref[...])
        _acc(dg1_ref, _colsum(d1 * mix), i == 0)
        dmb = (d1 * g1_ref[...]).astype(BF16)
        dmb_ref[...] = dmb
        dcat = _dot_nt(dmb, w_ref[...])
        dlru_ref[...] = dcat[:, :LRU_W]
        _, vjp = jax.vjp(_gdn_gate, o_ref[...], z_ref[...], gnw_ref[...])
        do, dz, dgnw = vjp(dcat[:, LRU_W:])
        do_ref[...] = do
        dz_ref[...] = dz
        _acc(dgnw_ref, dgnw, i == 0)

    return pl.pallas_call(
        body, name="out_bwd", grid=(S // tm,),
        in_specs=[_row(tm, D_MODEL), _row(tm, D_MODEL), _row(tm, GDN_W), _row(tm, GDN_W, 5), _const((1, LANES)),
                  _const((1, D_MODEL)), _const((D_MODEL, D_MODEL))],
        out_specs=[_row(tm, LRU_W), _row(tm, GDN_W), _row(tm, GDN_W), _row(tm, D_MODEL), _const((1, LANES)),
                   _const((1, D_MODEL))],
        out_shape=[_sds((S, LRU_W)), _sds((S, GDN_W)), _sds((S, GDN_W)), _sds((S, D_MODEL), BF16), _sds((1, LANES)),
                   _sds((1, D_MODEL))],
        compiler_params=_params(("arbitrary",)),
    )(dx1, cat, o, proj, gnw, g1, wout)


MLP_TM = 256


def _load_once(step, pairs, sem):
    @pl.when(step == 0)
    def _():
        copies = [pltpu.make_async_copy(src, dst, sem.at[k]) for k, (src, dst) in enumerate(pairs)]
        for cp in copies:
            cp.start()
        for cp in copies:
            cp.wait()


def _mlp_fwd(x1, nw, sc, sh, g2, wup, wdown, gather=()):
    S = x1.shape[0]
    tm = min(MLP_TM, S)
    nt = S // tm
    nk = len(gather)

    def body(*refs):
        x_ref, nw_ref, sc_ref, sh_ref, g2_ref, wup_hbm, wdown_hbm = refs[:7]
        x2_ref = refs[7 + nk]
        wup, wdown, sem = refs[8 + 2 * nk:11 + 2 * nk]
        step = pl.program_id(0)
        if nk:
            start, finish = _gather_steps(refs[7:7 + nk], refs[8 + nk:8 + 2 * nk], *refs[11 + 2 * nk:])
            pl.when(step == 0)(start)
        _load_once(step, [(wup_hbm, wup), (wdown_hbm, wdown)], sem)
        x = x_ref[...]
        hb = _norm_mod(x, nw_ref[...], sc_ref[...], sh_ref[...]).astype(BF16)
        r = jnp.maximum(_dot(hb, wup[...]), 0.0)
        x2_ref[...] = x + g2_ref[...] * _dot((r * r).astype(BF16), wdown[...])
        if nk:
            pl.when(step == nt - 1)(finish)

    vec = _const((1, D_MODEL))
    anyspec = pl.BlockSpec(memory_space=pl.ANY)
    outs = pl.pallas_call(
        body, name="mlp_fwd", grid=(nt,),
        in_specs=[_row(tm, D_MODEL), vec, vec, vec, vec, anyspec, anyspec] + _hbm_specs(nk),
        out_specs=[_row(tm, D_MODEL)] + _hbm_specs(nk),
        out_shape=[_sds((S, D_MODEL))] + (_gather_out_shapes(gather) if nk else []),
        scratch_shapes=[pltpu.VMEM((D_MODEL, D_FF), BF16), pltpu.VMEM((D_FF, D_MODEL), BF16),
                        pltpu.SemaphoreType.DMA((2,))] + (_gather_scratch(nk) if nk else []),
        compiler_params=_params(("arbitrary",)),
    )(x1, nw, sc, sh, g2, wup, wdown, *gather)
    return outs[0], list(outs[1:])


def _mlp_bwd(dx2, x1, nw, sc, sh, g2, wup, wdown):
    S = x1.shape[0]
    tm = min(MLP_TM, S)

    def body(dx2_ref, x_ref, nw_ref, sc_ref, sh_ref, g2_ref, wup_hbm, wdown_hbm,
             dx1_ref, hb_ref, dupb_ref, actb_ref, d2b_ref, dnw_ref, dsc_ref, dsh_ref, wup, wdown, sem):
        i = pl.program_id(0)
        _load_once(i, [(wup_hbm, wup), (wdown_hbm, wdown)], sem)
        x = x_ref[...]
        d2 = dx2_ref[...]
        hb = _norm_mod(x, nw_ref[...], sc_ref[...], sh_ref[...]).astype(BF16)
        hb_ref[...] = hb
        r = jnp.maximum(_dot(hb, wup[...]), 0.0)
        actb = (r * r).astype(BF16)
        actb_ref[...] = actb
        d2b_ref[...] = d2.astype(BF16)
        ddb = (d2 * g2_ref[...]).astype(BF16)
        dupb = (_dot_nt(ddb, wdown[...]) * (2.0 * r)).astype(BF16)
        dupb_ref[...] = dupb
        dh = _dot_nt(dupb, wup[...])
        dx, dnw, dsc, dsh = _norm_mod_bwd(dh, x, nw_ref[...], sc_ref[...])
        dx1_ref[...] = d2 + dx
        _acc(dnw_ref, dnw, i == 0)
        _acc(dsc_ref, dsc, i == 0)
        _acc(dsh_ref, dsh, i == 0)

    vec = _const((1, D_MODEL))
    anyspec = pl.BlockSpec(memory_space=pl.ANY)
    return pl.pallas_call(
        body, name="mlp_bwd", grid=(S // tm,),
        in_specs=[_row(tm, D_MODEL), _row(tm, D_MODEL), vec, vec, vec, vec, anyspec, anyspec],
        out_specs=[_row(tm, D_MODEL), _row(tm, D_MODEL), _row(tm, D_FF), _row(tm, D_FF), _row(tm, D_MODEL),
                   vec, vec, vec],
        out_shape=[_sds((S, D_MODEL)), _sds((S, D_MODEL), BF16), _sds((S, D_FF), BF16), _sds((S, D_FF), BF16),
                   _sds((S, D_MODEL), BF16), _sds((1, D_MODEL)), _sds((1, D_MODEL)), _sds((1, D_MODEL))],
        scratch_shapes=[pltpu.VMEM((D_MODEL, D_FF), BF16), pltpu.VMEM((D_FF, D_MODEL), BF16),
                        pltpu.SemaphoreType.DMA((2,))],
        compiler_params=_params(("arbitrary",)),
    )(dx2, x1, nw, sc, sh, g2, wup, wdown)


def _dw_down(act, d2b, g2, wdown, sharded, out_dtype):
    K, M = act.shape
    N = d2b.shape[1]
    tk = min(2048, K)
    nk = K // tk
    if sharded:
        h = M // (2 * N_CHIPS)
        tm = 2 * h
        out_spec = pl.BlockSpec((2, 1, h, N), lambda i, k: (0, i, 0, 0))
        out_shape = _sds((2, N_CHIPS, h, N), out_dtype)
    else:
        tm = min(512, M)
        out_spec = pl.BlockSpec((tm, N), lambda i, k: (i, 0))
        out_shape = _sds((M, N), out_dtype)

    def body(a_ref, b_ref, g2_ref, w_ref, o_ref, dg2_ref, acc):
        i = pl.program_id(0)
        k = pl.program_id(1)
        _acc(acc, _dot_tn(a_ref[...], b_ref[...]), k == 0)

        @pl.when(k == nk - 1)
        def _():
            g = acc[...]
            _acc(dg2_ref, _colsum(g * w_ref[...].astype(F32)), i == 0)
            out = (g * g2_ref[...]).astype(o_ref.dtype)
            if sharded:
                o_ref[0, 0] = out[:h]
                o_ref[1, 0] = out[h:]
            else:
                o_ref[...] = out

    vec = pl.BlockSpec((1, N), lambda i, k: (0, 0))
    return pl.pallas_call(
        body, name="dw_down", grid=(M // tm, nk),
        in_specs=[pl.BlockSpec((tk, tm), lambda i, k: (k, i)), pl.BlockSpec((tk, N), lambda i, k: (k, 0)), vec,
                  pl.BlockSpec((tm, N), lambda i, k: (i, 0))],
        out_specs=[out_spec, vec], out_shape=[out_shape, _sds((1, N))],
        scratch_shapes=[pltpu.VMEM((tm, N), F32)],
        compiler_params=_params(("arbitrary", "arbitrary")),
    )(act, d2b, g2, wdown)


def _matmul_tn(a, b, name, shards=None, out_dtype=F32):
    K, M = a.shape
    N = b.shape[1]
    tk = min(2048, K)
    if shards == "cols":
        tm, tn = M // 2, N // N_CHIPS
        out_spec = pl.BlockSpec((1, 1, tm, tn), lambda i, j, k: (i, j, 0, 0))
        out_shape = _sds((2, N_CHIPS, tm, tn))
    elif shards == "rows":
        h, tn = M // (2 * N_CHIPS), N
        tm = max(512, 2 * h)
        per_tile = tm // (2 * h)
        out_spec = pl.BlockSpec((2, per_tile, h, tn), lambda i, j, k: (0, i, 0, 0))
        out_shape = _sds((2, N_CHIPS, h, tn))
    else:
        tm = min(512, M)
        if N % 640 == 0:
            tn, tk = N, min(1024, K)
        else:
            tn = min(1024, N)
        out_spec = pl.BlockSpec((tm, tn), lambda i, j, k: (i, j))
        out_shape = _sds((M, N))
    nk = K // tk

    def body(a_ref, b_ref, o_ref, acc):
        k = pl.program_id(2)
        _acc(acc, _dot_tn(a_ref[...], b_ref[...]), k == 0)

        @pl.when(k == nk - 1)
        def _():
            if shards == "rows":
                for s in range(per_tile):
                    for half in range(2):
                        r0 = (2 * s + half) * h
                        o_ref[half, s] = acc[r0:r0 + h, :].astype(o_ref.dtype)
            else:
                o_ref[...] = acc[...].reshape(o_ref.shape).astype(o_ref.dtype)

    return pl.pallas_call(
        body, name=name, grid=(M // tm, N // tn, nk),
        in_specs=[pl.BlockSpec((tk, tm), lambda i, j, k: (k, i)), pl.BlockSpec((tk, tn), lambda i, j, k: (k, j))],
        out_specs=out_spec, out_shape=_sds(out_shape.shape, out_dtype),
        scratch_shapes=[pltpu.VMEM((tm, tn), F32)],
        compiler_params=_params(("arbitrary", "arbitrary", "arbitrary")),
    )(a, b)


def _loss_head(x, target, fnw):
    S = x.shape[0]
    tm = min(512, S)

    def body(x_ref, t_ref, w_ref, dx_ref, loss_ref, dw_ref):
        i = pl.program_id(0)
        w = w_ref[...]
        xh, r = _rms_parts(x_ref[...])
        err = xh * w - t_ref[...]
        part = 0.5 * jnp.sum(jnp.mean(err * err, axis=-1, keepdims=True), axis=0, keepdims=True)
        _acc(loss_ref, jnp.broadcast_to(part, (SUBLANES, LANES)), i == 0)
        dx, dw = _rms_bwd(err * (1.0 / D_MODEL), xh, r, w)
        dx_ref[...] = dx
        _acc(dw_ref, dw, i == 0)

    vec = _const((1, D_MODEL))
    return pl.pallas_call(
        body, name="loss_head", grid=(S // tm,),
        in_specs=[_row(tm, D_MODEL), _row(tm, D_MODEL), vec],
        out_specs=[_row(tm, D_MODEL), _const((SUBLANES, LANES)), vec],
        out_shape=[_sds((S, D_MODEL)), _sds((SUBLANES, LANES)), _sds((1, D_MODEL))],
        compiler_params=_params(("arbitrary",)),
    )(x, target, fnw)


def _block_diag(w):
    eye = jnp.eye(LRU_BLOCKS, dtype=w.dtype)
    return (eye[:, None, :, None] * w[:, :, None, :]).reshape(LRU_W, LRU_W)


def _diag_blocks(m):
    m4 = m.reshape(LRU_BLOCKS, LRU_BLOCK, LRU_BLOCKS, LRU_BLOCK)
    return jnp.stack([m4[g, :, g, :] for g in range(LRU_BLOCKS)])


def _layer_fwd(x, p, mlp_shards=(), mlp_weights=None, next_shards=()):
    proj, h1b = _proj_fwd(x, p["nmw"], p["sc1"], p["sh1"], p["win"])
    xr = _conv_fwd(proj, 0, LRU_W, p["lcw"], p["lcb"], False, "conv_lru_fwd")
    out_lru, h = _lru_fwd(xr, proj, p["wa"].astype(BF16), p["ba"], p["wx"].astype(BF16), p["bx"], p["lam"], p["lnw"])
    qkv = _conv_fwd(proj, 2 * LRU_W, 3 * GDN_W, p["gcw"], p["gcb"], True, "conv_gdn_fwd")
    o, states, gathered = _gdn_fwd(qkv, proj, p["alog"], p["dtb"], mlp_shards)
    if mlp_weights is not None:
        p = {**p, **mlp_weights(gathered)}
    x1, cat = _out_fwd(x, out_lru, o, proj, p["gnw"], p["g1"], p["wout"])
    x2, gathered_next = _mlp_fwd(x1, p["nmlp"], p["sc2"], p["sh2"], p["g2"], p["wup"], p["wdown"], next_shards)
    res = dict(x=x, proj=proj, h1b=h1b, xr=xr, h=h, qkv=qkv, o=o, states=states, x1=x1, cat=cat)
    return x2, res, p, gathered_next


def _layer_bwd(dx2, p, r, sharded=False, scatter=(), early=None):
    dx1, h2b, dupb, actb, d2b, dnmlp, dsc2, dsh2 = _mlp_bwd(
        dx2, r["x1"], p["nmlp"], p["sc2"], p["sh2"], p["g2"], p["wup"], p["wdown"])
    gdt = BF16 if sharded else F32
    g_wup = _matmul_tn(h2b, dupb, "dw_up", "cols" if sharded else None, gdt)
    g_wdown, dg2 = _dw_down(actb, d2b, p["g2"], p["wdown"], sharded, gdt)
    dlru, do, dz, dmb, dgnw, dg1 = _out_bwd(dx1, r["cat"], r["o"], r["proj"], p["gnw"], p["g1"], p["wout"])
    g_wout = _matmul_tn(r["cat"], dmb, "dw_out", "rows" if sharded else None, gdt)
    if early is not None:
        scatter = list(scatter) + early([g_wout, g_wup, g_wdown])
    dqkv_act, dba, dalog, ddtb, arrived = _gdn_bwd(do, r["qkv"], r["proj"], r["states"], p["alog"], p["dtb"], scatter)
    dqkv, dgcw, _ = _conv_bwd(r["proj"], 2 * LRU_W, 3 * GDN_W, p["gcw"], p["gcb"], dqkv_act, True, "conv_gdn_bwd")
    wab = p["wa"].astype(BF16)
    wxb = p["wx"].astype(BF16)
    dxr, dly, dwa, dba_, dwx, dbx, dlam, dlnw = _lru_bwd(
        dlru, r["xr"], r["proj"], r["h"], wab, p["ba"], wxb, p["bx"], p["lam"], p["lnw"])
    dlx, dlcw, dlcb = _conv_bwd(r["proj"], 0, LRU_W, p["lcw"], p["lcb"], dxr, False, "conv_lru_bwd")
    dx, dpb, dnmw, dsc1, dsh1 = _proj_bwd(dx1, r["x"], dlx, dly, dqkv, dz, dba, p["nmw"], p["sc1"], p["win"])
    g_win = _matmul_tn(r["h1b"], dpb, "dw_in", None, gdt)
    grads = dict(nmw=dnmw, nmlp=dnmlp, sh1=dsh1, sc1=dsc1, g1=dg1, sh2=dsh2, sc2=dsc2, g2=dg2,
                 win=g_win, lcw=dlcw, lcb=dlcb, wa=dwa, ba=dba_, wx=dwx, bx=dbx, lam=dlam, lnw=dlnw,
                 gcw=dgcw, alog=dalog, dtb=ddtb, gnw=dgnw, wout=g_wout, wup=g_wup, wdown=g_wdown)
    return dx, grads, arrived


def _local_step(x, target, fnw, layers):
    res = []
    for p in layers:
        x, r, _, _ = _layer_fwd(x, p)
        res.append(r)
    dx, loss_blk, dfnw = _loss_head(x, target, fnw)
    grads = [None] * len(layers)
    for l in reversed(range(len(layers))):
        dx, grads[l], _ = _layer_bwd(dx, layers[l], res[l])
    stacked = {k: jnp.stack([g[k] for g in grads]) for k in grads[0]}
    return loss_blk[0, 0], dx, dfnw, stacked


def _prep_layers(norm_mix_w, norm_mlp_w, mod, win_b, lru_conv_w, lru_conv_b, gate_a_w, gate_a_b, gate_x_w, gate_x_b,
                 lru_lambda, lru_norm_w, gdn_conv_w, gdn_a_log, gdn_dt_bias, gdn_norm_w, wout_b, wup_b, wdown_b):
    L = norm_mix_w.shape[0]

    def vec(a):
        return a.reshape(L, 1, -1)

    def lanes(a):
        return jnp.pad(a, ((0, 0), (0, LANES - a.shape[1]))).reshape(L, 1, LANES)

    def taps(w):
        return jnp.pad(w, ((0, 0), (0, SUBLANES - w.shape[1]), (0, 0)))

    m = mod.reshape(L, N_MOD, 1, D_MODEL)
    return dict(
        nmw=vec(norm_mix_w), nmlp=vec(norm_mlp_w),
        sh1=m[:, 0], sc1=m[:, 1], g1=m[:, 2], sh2=m[:, 3], sc2=m[:, 4], g2=m[:, 5],
        win=win_b, lcw=taps(lru_conv_w), lcb=vec(lru_conv_b),
        wa=jax.vmap(_block_diag)(gate_a_w), ba=vec(gate_a_b), wx=jax.vmap(_block_diag)(gate_x_w), bx=vec(gate_x_b),
        lam=vec(lru_lambda), lnw=vec(lru_norm_w),
        gcw=taps(gdn_conv_w), gcb=jnp.zeros((L, 1, 3 * GDN_W), F32),
        alog=lanes(gdn_a_log), dtb=lanes(gdn_dt_bias), gnw=vec(gdn_norm_w),
        wout=wout_b, wup=wup_b, wdown=wdown_b)


def _position():
    x, y, c = lax.axis_index("x"), lax.axis_index("y"), lax.axis_index("c")
    return x, y, c


def _other_chips(x, y):
    return [(1 - x, y), (x, 1 - y), (1 - x, 1 - y)]


def _all_gather_rows(block, name):
    m, n = block.shape

    def body(x_ref, out_ref, send_sems, recv_sems, local_sem):
        x, y, c = _position()
        me, sibling = (x, y, c), (x, y, 1 - c)
        chips = _other_chips(x, y)

        def rows(px, py, pc):
            return out_ref.at[pl.ds((4 * px + 2 * py + pc) * m, m), :]

        def copy(k, blk, to, src=None):
            return pltpu.make_async_remote_copy(
                src_ref=rows(*blk) if src is None else src, dst_ref=rows(*blk),
                send_sem=send_sems.at[k], recv_sem=recv_sems.at[k], device_id=to, device_id_type=MESH)

        mine = pltpu.make_async_copy(x_ref, rows(*me), local_sem)
        mine.start()
        first = [copy(0, me, sibling, src=x_ref)]
        first += [copy(1 + j, me, (*chip, c), src=x_ref) for j, chip in enumerate(chips)]
        for cp in first:
            cp.start()
        passed = [copy(4 + j, (*chip, c), sibling) for j, chip in enumerate(chips)]
        for j, chip in enumerate(chips):
            copy(1 + j, (*chip, c), me).wait_recv()
            passed[j].start()
        copy(0, sibling, me).wait_recv()
        for j, chip in enumerate(chips):
            copy(4 + j, (*chip, 1 - c), me).wait_recv()
        for cp in first + passed:
            cp.wait_send()
        mine.wait()

    return pl.pallas_call(
        body, name=name,
        out_shape=_sds((N_DEV * m, n)),
        in_specs=[pl.BlockSpec(memory_space=pltpu.VMEM)],
        out_specs=pl.BlockSpec(memory_space=pltpu.VMEM),
        scratch_shapes=[pltpu.SemaphoreType.DMA((7,)), pltpu.SemaphoreType.DMA((7,)), pltpu.SemaphoreType.DMA],
        compiler_params=pltpu.CompilerParams(vmem_limit_bytes=VMEM_LIMIT),
    )(block)


def _hbm_specs(n):
    return [pl.BlockSpec(memory_space=pl.ANY)] * n


def _gather_chips(shards, name):
    n = len(shards)

    def body(*refs):
        start, finish = _gather_steps(refs[:n], refs[n:2 * n], *refs[2 * n:])
        start()
        finish()

    return pl.pallas_call(
        body, name=name,
        out_shape=_gather_out_shapes(shards), in_specs=_hbm_specs(n), out_specs=_hbm_specs(n),
        scratch_shapes=_gather_scratch(n),
    )(*shards)


def _gather_out_shapes(shards):
    return [_sds((N_CHIPS, 2, s.shape[0] // 2, s.shape[1]), s.dtype) for s in shards]


def _gather_scratch(n):
    return [pltpu.SemaphoreType.DMA((6 * n,)), pltpu.SemaphoreType.DMA((6 * n,))]


def _gather_steps(ins, outs, send_sems, recv_sems):
    n = len(ins)
    x, y, c = _position()
    chips = _other_chips(x, y)
    me = 2 * x + y

    def first(a, j, slot):
        h = ins[a].shape[0] // 2
        return pltpu.make_async_remote_copy(
            src_ref=ins[a].at[pl.ds(pl.multiple_of(c * h, SUBLANES), h)], dst_ref=outs[a].at[slot, c],
            send_sem=send_sems.at[3 * a + j], recv_sem=recv_sems.at[3 * a + j],
            device_id=(chips[j][0], chips[j][1], c), device_id_type=MESH)

    def second(a, j, half):
        slot = 2 * chips[j][0] + chips[j][1]
        return pltpu.make_async_remote_copy(
            src_ref=outs[a].at[slot, c], dst_ref=outs[a].at[slot, half],
            send_sem=send_sems.at[3 * (n + a) + j], recv_sem=recv_sems.at[3 * (n + a) + j],
            device_id=(x, y, 1 - c), device_id_type=MESH)

    def start():
        for a in range(n):
            for j in range(3):
                first(a, j, me).start()

    def finish():
        for a in range(n):
            for j, (px, py) in enumerate(chips):
                first(a, j, 2 * px + py).wait_recv()
                second(a, j, c).start()
        for a in range(n):
            for j in range(3):
                second(a, j, 1 - c).wait_recv()
        for a in range(n):
            for j in range(3):
                first(a, j, me).wait_send()
                second(a, j, c).wait_send()

    return start, finish


def _send_to_sibling(parts, name):
    n = len(parts)

    def body(*refs):
        ins, outs = refs[:n], refs[n:2 * n]
        send_sems, recv_sems = refs[2 * n:]
        x, y, c = _position()
        copies = [pltpu.make_async_remote_copy(
            src_ref=ins[a].at[1 - c], dst_ref=outs[a], send_sem=send_sems.at[a], recv_sem=recv_sems.at[a],
            device_id=(x, y, 1 - c), device_id_type=MESH) for a in range(n)]
        for cp in copies:
            cp.start()
        for cp in copies:
            cp.wait()

    return pl.pallas_call(
        body, name=name,
        out_shape=[_sds(p.shape[1:], p.dtype) for p in parts],
        in_specs=_hbm_specs(n), out_specs=_hbm_specs(n),
        scratch_shapes=[pltpu.SemaphoreType.DMA((n,)), pltpu.SemaphoreType.DMA((n,))],
    )(*parts)


def _scatter_chips(parts, name):
    n = len(parts)

    def body(*refs):
        start, finish = _scatter_steps(refs[:n], refs[n:2 * n], *refs[2 * n:])
        start()
        finish()

    return pl.pallas_call(
        body, name=name,
        out_shape=[_sds(p.shape, p.dtype) for p in parts], in_specs=_hbm_specs(n), out_specs=_hbm_specs(n),
        scratch_shapes=_scatter_scratch(n),
    )(*parts)


def _scatter_scratch(n):
    return [pltpu.SemaphoreType.DMA((3 * n,)), pltpu.SemaphoreType.DMA((3 * n,))]


def _scatter_steps(ins, outs, send_sems, recv_sems):
    n = len(ins)
    x, y, c = _position()
    chips = _other_chips(x, y)
    me = 2 * x + y

    def copy(a, j, src_slot, dst_slot):
        px, py = chips[j]
        return pltpu.make_async_remote_copy(
            src_ref=ins[a].at[src_slot], dst_ref=outs[a].at[dst_slot], send_sem=send_sems.at[3 * a + j],
            recv_sem=recv_sems.at[3 * a + j], device_id=(px, py, c), device_id_type=MESH)

    def start():
        for a in range(n):
            for j in range(3):
                copy(a, j, 2 * chips[j][0] + chips[j][1], me).start()

    def finish():
        for a in range(n):
            for j, (px, py) in enumerate(chips):
                copy(a, j, me, 2 * px + py).wait_recv()
        for a in range(n):
            for j in range(3):
                copy(a, j, 2 * chips[j][0] + chips[j][1], me).wait_send()

    return start, finish


def _swap_row_halves(arrays, name):
    n = len(arrays)

    def body(*refs):
        outs = refs[n:2 * n]
        send_sems, recv_sems = refs[2 * n:]
        x, y, c = _position()

        def copy(a, half):
            h = outs[a].shape[0] // 2
            rows = outs[a].at[pl.ds(pl.multiple_of(half * h, SUBLANES), h)]
            return pltpu.make_async_remote_copy(
                src_ref=rows, dst_ref=rows, send_sem=send_sems.at[a], recv_sem=recv_sems.at[a],
                device_id=(x, y, 1 - c), device_id_type=MESH)

        sends = [copy(a, c) for a in range(n)]
        for cp in sends:
            cp.start()
        for a in range(n):
            copy(a, 1 - c).wait_recv()
        for cp in sends:
            cp.wait_send()

    return pl.pallas_call(
        body, name=name,
        out_shape=[_sds(a.shape, a.dtype) for a in arrays],
        in_specs=_hbm_specs(n), out_specs=_hbm_specs(n),
        input_output_aliases={a: a for a in range(n)},
        scratch_shapes=[pltpu.SemaphoreType.DMA((n,)), pltpu.SemaphoreType.DMA((n,))],
    )(*arrays)


def _row_tile(rows):
    for t in (512, 256, 128, 64, 32, 16, 8):
        if rows % t == 0:
            return t
    return rows


def _sum_slots(buf, name):
    k, rows, cols = buf.shape
    tm = _row_tile(rows)

    def body(b_ref, o_ref):
        s = b_ref[0]
        for i in range(1, k):
            s = s + b_ref[i]
        o_ref[...] = s

    return pl.pallas_call(
        body, name=name, grid=(rows // tm,),
        in_specs=[pl.BlockSpec((k, tm, cols), lambda i: (0, i, 0))],
        out_specs=pl.BlockSpec((tm, cols), lambda i: (i, 0)),
        out_shape=_sds((rows, cols)),
        compiler_params=_params(("arbitrary",)),
    )(buf)


def _pair_add(part, from_sibling, core, name):
    _, k, h, cols = part.shape
    rows = k * h
    tm = _row_tile(rows)

    def body(core_ref, a_ref, b_ref, o_ref):
        o_ref[...] = (a_ref[0].astype(F32) + b_ref[...].astype(F32)).astype(BF16)

    out = pl.pallas_call(
        body, name=name,
        grid_spec=pltpu.PrefetchScalarGridSpec(
            num_scalar_prefetch=1, grid=(rows // tm,),
            in_specs=[pl.BlockSpec((1, tm, cols), lambda i, cr: (cr[0], i, 0)),
                      pl.BlockSpec((tm, cols), lambda i, cr: (i, 0))],
            out_specs=pl.BlockSpec((tm, cols), lambda i, cr: (i, 0))),
        out_shape=_sds((rows, cols), BF16),
        compiler_params=_params(("arbitrary",)),
    )(core, part.reshape(2, rows, cols), from_sibling.reshape(rows, cols))
    return out.reshape(k, h, cols)


def _chip_sum(arrived, own, place, name):
    _, h, cols = arrived.shape
    tm = min(256, h)
    nb = h // tm

    def body(place_ref, arr_ref, own_ref, g_ref):
        for chip in range(N_CHIPS):
            @pl.when(place_ref[1] == chip)
            def _():
                terms = [own_ref[0] if j == chip else arr_ref[j] for j in range(N_CHIPS)]
                g = terms[0].astype(F32)
                for t in terms[1:]:
                    g = g + t.astype(F32)
                g_ref[...] = g

    return pl.pallas_call(
        body, name=name,
        grid_spec=pltpu.PrefetchScalarGridSpec(
            num_scalar_prefetch=1, grid=(nb,),
            in_specs=[pl.BlockSpec((N_CHIPS, tm, cols), lambda i, pr: (0, i, 0)),
                      pl.BlockSpec((1, tm, cols), lambda i, pr: (pr[1], i, 0))],
            out_specs=pl.BlockSpec((tm, cols), lambda i, pr: (pr[0] * nb + i, 0))),
        out_shape=_sds((2 * h, cols)),
        compiler_params=_params(("arbitrary",)),
    )(place, arrived, own)


def _adam_layer(g, w, m, v, outs, layer, name):
    rows, cols = g.shape
    tm = _row_tile(rows)

    def body(g_ref, w_ref, m_ref, v_ref, *refs):
        og_ref, od_ref, om_ref, ov_ref = refs[4:]
        gr = g_ref[...]
        og_ref[0] = gr
        d, nm, nv = _adam_math(w_ref[0], gr, m_ref[0], v_ref[0])
        od_ref[0] = d
        om_ref[0] = nm
        ov_ref[0] = nv

    slab = pl.BlockSpec((1, tm, cols), lambda i: (layer, i, 0))
    return pl.pallas_call(
        body, name=name, grid=(rows // tm,),
        in_specs=[pl.BlockSpec((tm, cols), lambda i: (i, 0)), slab, slab, slab] + _hbm_specs(4),
        out_specs=[slab] * 4, out_shape=[_sds(o.shape) for o in outs],
        input_output_aliases={4 + i: i for i in range(4)},
        compiler_params=_params(("arbitrary",)),
    )(g, w, m, v, *outs)


def _adam_math(w, g, m, v):
    m = ADAM_B1 * m + (1.0 - ADAM_B1) * g
    v = ADAM_B2 * v + (1.0 - ADAM_B2) * jnp.square(g)
    m_hat = m / (1.0 - ADAM_B1 ** ADAM_STEP)
    v_hat = v / (1.0 - ADAM_B2 ** ADAM_STEP)
    delta = -ADAM_LR * (m_hat / (jnp.sqrt(v_hat) + ADAM_EPS) + ADAM_WD * w)
    return delta, m, v


def _adam(w, g, m, v, name):
    rows, cols = w.shape
    tm = _row_tile(rows)

    def body(w_ref, g_ref, m_ref, v_ref, d_ref, nm_ref, nv_ref):
        d, nm, nv = _adam_math(w_ref[...], g_ref[...], m_ref[...], v_ref[...])
        d_ref[...] = d
        nm_ref[...] = nm
        nv_ref[...] = nv

    spec = pl.BlockSpec((tm, cols), lambda i: (i, 0))
    return pl.pallas_call(
        body, name=name, grid=(rows // tm,), in_specs=[spec] * 4, out_specs=[spec] * 3,
        out_shape=[_sds((rows, cols))] * 3, compiler_params=_params(("arbitrary",)),
    )(w, g, m, v)


def _mod_fwd(c_all, w_mod, b_mod_cols):
    L, _, n = w_mod.shape

    def body(c_ref, w_ref, b_ref, o_ref):
        o_ref[0] = _hdot(_silu(c_ref[...]), w_ref[0]) + b_ref[0]

    return pl.pallas_call(
        body, name="mod_fwd", grid=(L,),
        in_specs=[_const((N_DEV, D_MODEL)), pl.BlockSpec((1, D_MODEL, n), lambda l: (l, 0, 0)),
                  pl.BlockSpec((1, 1, n), lambda l: (l, 0, 0))],
        out_specs=pl.BlockSpec((1, N_DEV, n), lambda l: (l, 0, 0)),
        out_shape=_sds((L, N_DEV, n)),
        compiler_params=_params(("arbitrary",)),
    )(c_all, w_mod, b_mod_cols)


def _mod_update(c_all, dmod, w, m, v):
    L, _, n = w.shape
    tn = 512

    def body(c_ref, d_ref, w_ref, m_ref, v_ref, g_ref, dl_ref, nm_ref, nv_ref):
        g = _hdot_tn(_silu(c_ref[...]), d_ref[0])
        g_ref[0] = g
        d, nm, nv = _adam_math(w_ref[0], g, m_ref[0], v_ref[0])
        dl_ref[0] = d
        nm_ref[0] = nm
        nv_ref[0] = nv

    big = pl.BlockSpec((1, D_MODEL, tn), lambda l, j: (l, 0, j))
    return pl.pallas_call(
        body, name="mod_update", grid=(L, n // tn),
        in_specs=[_const((N_DEV, D_MODEL)), pl.BlockSpec((1, N_DEV, tn), lambda l, j: (l, 0, j)), big, big, big],
        out_specs=[big] * 4, out_shape=[_sds(w.shape)] * 4,
        compiler_params=_params(("arbitrary", "arbitrary")),
    )(c_all, dmod, w, m, v)


def _pack_rows(parts, row_multiple):
    flat = jnp.concatenate([p.reshape(-1) for p in parts])
    unit = row_multiple * LANES
    flat = jnp.pad(flat, (0, (-flat.shape[0]) % unit))
    return flat.reshape(-1, LANES)


def _unpack(packed, shapes):
    flat = packed.reshape(-1)
    out, off = [], 0
    for s in shapes:
        n = 1
        for d in s:
            n *= d
        out.append(flat[off:off + n].reshape(s))
        off += n
    return out


def _lane_pad(a):
    return jnp.pad(a, ((0, 0), (0, LANES - a.shape[1])))


WEIGHT_NAMES = ("norm_mix_w", "norm_mlp_w", "w_mod", "b_mod", "w_in", "lru_conv_w", "lru_conv_b", "lru_gate_a_w",
                "lru_gate_a_b", "lru_gate_x_w", "lru_gate_x_b", "lru_lambda", "lru_norm_w", "gdn_conv_w", "gdn_a_log",
                "gdn_dt_bias", "gdn_norm_w", "w_out", "w_up", "w_down", "final_norm_w")


def kernel(x, c, norm_mix_w, norm_mlp_w, w_mod, b_mod, w_in, lru_conv_w, lru_conv_b, lru_gate_a_w, lru_gate_a_b, lru_gate_x_w, lru_gate_x_b, lru_lambda, lru_norm_w, gdn_conv_w, gdn_a_log, gdn_dt_bias, gdn_norm_w, w_out, w_up, w_down, final_norm_w, loss_target, m_norm_mix_w, m_norm_mlp_w, m_w_mod, m_b_mod, m_w_in, m_lru_conv_w, m_lru_conv_b, m_lru_gate_a_w, m_lru_gate_a_b, m_lru_gate_x_w, m_lru_gate_x_b, m_lru_lambda, m_lru_norm_w, m_gdn_conv_w, m_gdn_a_log, m_gdn_dt_bias, m_gdn_norm_w, m_w_out, m_w_up, m_w_down, m_final_norm_w, v_norm_mix_w, v_norm_mlp_w, v_w_mod, v_b_mod, v_w_in, v_lru_conv_w, v_lru_conv_b, v_lru_gate_a_w, v_lru_gate_a_b, v_lru_gate_x_w, v_lru_gate_x_b, v_lru_lambda, v_lru_norm_w, v_gdn_conv_w, v_gdn_a_log, v_gdn_dt_bias, v_gdn_norm_w, v_w_out, v_w_up, v_w_down, v_final_norm_w):
    W = dict(zip(WEIGHT_NAMES, (norm_mix_w, norm_mlp_w, w_mod, b_mod, w_in, lru_conv_w, lru_conv_b, lru_gate_a_w,
                                lru_gate_a_b, lru_gate_x_w, lru_gate_x_b, lru_lambda, lru_norm_w, gdn_conv_w, gdn_a_log,
                                gdn_dt_bias, gdn_norm_w, w_out, w_up, w_down, final_norm_w)))
    M = dict(zip(WEIGHT_NAMES, (m_norm_mix_w, m_norm_mlp_w, m_w_mod, m_b_mod, m_w_in, m_lru_conv_w, m_lru_conv_b,
                                m_lru_gate_a_w, m_lru_gate_a_b, m_lru_gate_x_w, m_lru_gate_x_b, m_lru_lambda,
                                m_lru_norm_w, m_gdn_conv_w, m_gdn_a_log, m_gdn_dt_bias, m_gdn_norm_w, m_w_out, m_w_up,
                                m_w_down, m_final_norm_w)))
    V = dict(zip(WEIGHT_NAMES, (v_norm_mix_w, v_norm_mlp_w, v_w_mod, v_b_mod, v_w_in, v_lru_conv_w, v_lru_conv_b,
                                v_lru_gate_a_w, v_lru_gate_a_b, v_lru_gate_x_w, v_lru_gate_x_b, v_lru_lambda,
                                v_lru_norm_w, v_gdn_conv_w, v_gdn_a_log, v_gdn_dt_bias, v_gdn_norm_w, v_w_out, v_w_up,
                                v_w_down, v_final_norm_w)))
    L = DEPTH
    xi, yi, ci = _position()
    chip = 2 * xi + yi
    dev = 2 * chip + ci
    lcs = LRU_W // N_CHIPS
    gcs = 3 * GDN_W // N_CHIPS
    mcs = N_MOD * D_MODEL // N_CHIPS

    g_in = _all_gather_rows(_pack_rows([c, lru_conv_w, gdn_conv_w], SUBLANES), "gather_small_inputs").reshape(N_DEV, -1)
    c_all = g_in[:, :D_MODEL]
    per_chip = g_in[0::2]
    o1 = D_MODEL + L * 4 * lcs
    lcw_full = per_chip[:, D_MODEL:o1].reshape(N_CHIPS, L, 4, lcs).transpose(1, 2, 0, 3).reshape(L, 4, LRU_W)
    gcw_full = per_chip[:, o1:o1 + L * 4 * gcs].reshape(N_CHIPS, L, 4, gcs).transpose(1, 2, 0, 3).reshape(L, 4, 3 * GDN_W)

    b_cols = lax.dynamic_slice(b_mod, (0, chip * mcs), (L, mcs)).reshape(L, 1, mcs)
    modp = _mod_fwd(c_all, w_mod, b_cols)
    g_mod = _all_gather_rows(modp.reshape(L * N_DEV, mcs), "gather_mod").reshape(N_DEV, L, N_DEV, mcs)
    mod = lax.dynamic_index_in_dim(g_mod[0::2], dev, axis=2, keepdims=False).transpose(1, 0, 2).reshape(L, N_MOD * D_MODEL)

    stacked = _prep_layers(norm_mix_w, norm_mlp_w, mod, None, lcw_full, lru_conv_b, lru_gate_a_w, lru_gate_a_b,
                           lru_gate_x_w, lru_gate_x_b, lru_lambda, lru_norm_w, gcw_full, gdn_a_log, gdn_dt_bias,
                           gdn_norm_w, None, None, None)
    shards = [[w_in[l].astype(BF16), w_out[l].astype(BF16), w_up[l].astype(BF16), w_down[l].astype(BF16)]
              for l in range(L)]

    def with_own(gathered, own):
        return [lax.dynamic_update_slice(got, o.reshape((1,) + got.shape[1:]), (chip, 0, 0, 0)).reshape(
            (N_CHIPS,) + o.shape) for got, o in zip(gathered, own)]

    def mixer_weights(l, gathered):
        win_g, wout_g = with_own(gathered, shards[l][:2])
        return dict(win=jnp.pad(win_g.transpose(1, 0, 2).reshape(D_MODEL, IN_COLS), ((0, 0), (0, IN_PAD - IN_COLS))),
                    wout=wout_g.reshape(D_MODEL, D_MODEL))

    def mlp_weights(l, gathered):
        wup_g, wdown_g = with_own(gathered, shards[l][2:])
        return dict(wup=wup_g.transpose(1, 0, 2).reshape(D_MODEL, D_FF), wdown=wdown_g.reshape(D_FF, D_MODEL))

    mixer = mixer_weights(0, _gather_chips(shards[0][:2], "gather_weights"))
    layers = []
    xs = x[0]
    res = []
    for l in range(L):
        p = {k: v[l] for k, v in stacked.items() if v is not None}
        p.update(mixer)
        xs, r, p, gathered = _layer_fwd(xs, p, shards[l][2:], functools.partial(mlp_weights, l),
                                        shards[l + 1][:2] if l + 1 < L else ())
        res.append(r)
        layers.append(p)
        if l + 1 < L:
            mixer = mixer_weights(l + 1, gathered)
    dx, loss_blk, dfnw = _loss_head(xs, loss_target[0], final_norm_w.reshape(1, D_MODEL))
    loss_local = loss_blk[0, 0]

    big_names = ["w_in", "w_out", "w_up", "w_down"]
    core = jnp.reshape(ci, (1,)).astype(jnp.int32)
    place = jnp.stack([ci, chip]).astype(jnp.int32)
    layer_grads = [None] * L

    big = {nm: [lax.empty(W[nm].shape, F32) for _ in range(4)] for nm in big_names}

    def pair_sums(names, parts):
        from_sibling = _send_to_sibling(parts, "pair_send")
        return [_pair_add(p, r, core, "pair_add_" + nm) for nm, p, r in zip(names, parts, from_sibling)]

    def apply_update(l, names, arrived, pair):
        halves = [_chip_sum(a, own, place, "chip_sum_" + nm) for nm, a, own in zip(names, arrived, pair)]
        full = _swap_row_halves(halves, "pair_swap")
        for nm, gr in zip(names, full):
            big[nm] = _adam_layer(gr, W[nm], M[nm], V[nm], big[nm], l, "adam_" + nm)

    win_pair = []
    for l in reversed(range(L)):
        early_pair = []

        def early(parts, early_pair=early_pair):
            early_pair.extend(pair_sums(big_names[1:], parts))
            return early_pair

        dx, gl, arrived = _layer_bwd(dx, layers[l], res[l], sharded=True, scatter=win_pair, early=early)
        if win_pair:
            apply_update(l + 1, big_names[:1], arrived[:1], win_pair)
        apply_update(l, big_names[1:], arrived[len(win_pair):], early_pair)
        layer_grads[l] = gl
        gwin = gl["win"][:, :IN_COLS].reshape(2, D_MODEL // 2, N_CHIPS, IN_COLS // N_CHIPS).transpose(0, 2, 1, 3)
        win_pair = pair_sums(big_names[:1], [gwin])
    apply_update(0, big_names[:1], _scatter_chips(win_pair, "chip_scatter"), win_pair)
    small_keys = [k for k in layer_grads[0] if k not in ("win", "wout", "wup", "wdown")]
    g = {k: jnp.stack([gl[k] for gl in layer_grads]) for k in small_keys}
    loss = lax.psum(loss_local, ("x", "y", "c"))

    dmod = jnp.concatenate([g["sh1"], g["sc1"], g["g1"], g["sh2"], g["sc2"], g["g2"]], axis=-1)
    small = [dmod, g["nmw"], g["nmlp"], g["lcw"][:, :4], g["lcb"], jax.vmap(_diag_blocks)(g["wa"]), g["ba"],
             jax.vmap(_diag_blocks)(g["wx"]), g["bx"], g["lam"], g["lnw"], g["gcw"][:, :4], g["alog"], g["dtb"],
             g["gnw"], dfnw]
    small_shapes = [(L, N_MOD * D_MODEL), (L, D_MODEL), (L, D_MODEL), (L, 4, LRU_W), (L, LRU_W),
                    (L, LRU_BLOCKS, LRU_BLOCK, LRU_BLOCK), (L, LRU_W), (L, LRU_BLOCKS, LRU_BLOCK, LRU_BLOCK),
                    (L, LRU_W), (L, LRU_W), (L, LRU_W), (L, 4, 3 * GDN_W), (L, LANES), (L, LANES), (L, LANES),
                    (D_MODEL,)]
    small_names = ["b_mod", "norm_mix_w", "norm_mlp_w", None, "lru_conv_b", "lru_gate_a_w", "lru_gate_a_b",
                   "lru_gate_x_w", "lru_gate_x_b", "lru_lambda", "lru_norm_w", None, "gdn_a_log", "gdn_dt_bias",
                   "gdn_norm_w", "final_norm_w"]
    pack_g = _pack_rows(small, 512)
    rows = pack_g.shape[0]
    all_g = _all_gather_rows(pack_g, "gather_small_grads").reshape(N_DEV, rows, LANES)
    tot = _sum_slots(all_g, "sum_small_grads")
    tot_parts = _unpack(tot, small_shapes)

    def pack_state(S_):
        parts = []
        for nm, shp in zip(small_names, small_shapes):
            if nm is None:
                parts.append(jnp.zeros(shp, F32))
            elif nm in ("gdn_a_log", "gdn_dt_bias"):
                parts.append(_lane_pad(S_[nm]))
            else:
                parts.append(S_[nm])
        return _pack_rows(parts, 512)

    upd = _adam(pack_state(W), tot, pack_state(M), pack_state(V), "adam_small")
    upd_parts = [_unpack(u, small_shapes) for u in upd]

    grads, deltas, new_m, new_v = {}, {}, {}, {}
    for k, nm in enumerate(small_names):
        if nm is None:
            continue
        cut = (lambda a: a[:, :HEADS]) if nm in ("gdn_a_log", "gdn_dt_bias") else (lambda a: a)
        grads[nm] = cut(tot_parts[k])
        deltas[nm], new_m[nm], new_v[nm] = (cut(u[k]) for u in upd_parts)

    g_lcw = lax.dynamic_slice(tot_parts[3], (0, 0, chip * lcs), (L, 4, lcs))
    g_gcw = lax.dynamic_slice(tot_parts[11], (0, 0, chip * gcs), (L, 4, gcs))
    conv_shapes = [(L, 4, lcs), (L, 4, gcs)]
    conv_pack = lambda a, b: _pack_rows([a, b], SUBLANES)
    cu = _adam(conv_pack(lru_conv_w, gdn_conv_w), conv_pack(g_lcw, g_gcw), conv_pack(m_lru_conv_w, m_gdn_conv_w),
               conv_pack(v_lru_conv_w, v_gdn_conv_w), "adam_conv")
    cu_parts = [_unpack(u, conv_shapes) for u in cu]
    for k, nm in enumerate(("lru_conv_w", "gdn_conv_w")):
        grads[nm] = (g_lcw, g_gcw)[k]
        deltas[nm], new_m[nm], new_v[nm] = (u[k] for u in cu_parts)

    dmod_all = all_g[:, :L * N_MOD * D_MODEL // LANES].reshape(N_DEV, L, N_MOD * D_MODEL)
    dmod_cols = lax.dynamic_slice(dmod_all, (0, 0, chip * mcs), (N_DEV, L, mcs)).transpose(1, 0, 2)
    grads["w_mod"], deltas["w_mod"], new_m["w_mod"], new_v["w_mod"] = _mod_update(c_all, dmod_cols, w_mod, m_w_mod, v_w_mod)

    for nm in big_names:
        grads[nm], deltas[nm], new_m[nm], new_v[nm] = big[nm]

    out = [loss, dx[None]]
    for group in (grads, deltas, new_m, new_v):
        out += [group[nm].reshape(W[nm].shape) for nm in WEIGHT_NAMES]
    return tuple(out)
```

```python
import functools

import jax
import jax.numpy as jnp
from jax import lax
from jax.experimental import pallas as pl
from jax.experimental.pallas import tpu as pltpu

F32 = jnp.float32
BF16 = jnp.bfloat16
MESH = pl.DeviceIdType.MESH

D_MODEL = 1024
DEPTH = 4
LRU_W = 512
LRU_BLOCKS = 8
LRU_BLOCK = 64
LRU_C = 8.0
HEADS = 4
HEAD_DIM = 128
GDN_W = 512
CHUNK = 128
D_FF = 4096
N_MOD = 6
IN_COLS = 3080
IN_PAD = 3200
NORM_EPS = 1e-6
LANES = 128
SUBLANES = 8
N_DEV = 8
N_CHIPS = 4

ADAM_LR = 0.001
ADAM_B1 = 0.9
ADAM_B2 = 0.999
ADAM_EPS = 1e-08
ADAM_WD = 0.01
ADAM_STEP = 10

VMEM_LIMIT = 56 * 1024 * 1024
HI = lax.Precision.HIGHEST


def _sds(shape, dtype=F32):
    return jax.ShapeDtypeStruct(tuple(shape), dtype)


def _params(sem=None, vmem=VMEM_LIMIT):
    return pltpu.CompilerParams(dimension_semantics=sem, vmem_limit_bytes=vmem)


def _const(shape):
    return pl.BlockSpec(tuple(shape), lambda *_: (0,) * len(shape))


def _row(tm, c, col=0):
    return pl.BlockSpec((tm, c), lambda i: (i, col))


def _dot(a, b):
    return jnp.dot(a, b, preferred_element_type=F32)


def _dot_nt(a, b):
    return lax.dot_general(a, b, (((1,), (1,)), ((), ())), preferred_element_type=F32)


def _dot_tn(a, b):
    return lax.dot_general(a, b, (((0,), (0,)), ((), ())), preferred_element_type=F32)


def _hdot(a, b):
    return jnp.dot(a, b, preferred_element_type=F32, precision=HI)


def _hdot_nt(a, b):
    return lax.dot_general(a, b, (((1,), (1,)), ((), ())), preferred_element_type=F32, precision=HI)


def _hdot_tn(a, b):
    return lax.dot_general(a, b, (((0,), (0,)), ((), ())), preferred_element_type=F32, precision=HI)


_DIMS = {"nn": (((1,), (0,)), ((), ())), "nt": (((1,), (1,)), ((), ())), "tn": (((0,), (0,)), ((), ()))}


def _mm_raw(a, b, dims, passes):
    dn = _DIMS[dims]

    def dot(p, q):
        return lax.dot_general(p, q, dn, preferred_element_type=F32)

    a_hi = a.astype(BF16)
    b_hi = b.astype(BF16)
    if passes == 1:
        return dot(a_hi, b_hi)
    a_lo = (a - a_hi.astype(F32)).astype(BF16)
    b_lo = (b - b_hi.astype(F32)).astype(BF16)
    return dot(a_hi, b_hi) + (dot(a_hi, b_lo) + dot(a_lo, b_hi))


@functools.partial(jax.custom_vjp, nondiff_argnums=(2, 3))
def _mm(a, b, dims, passes):
    return _mm_raw(a, b, dims, passes)


def _mm_fwd(a, b, dims, passes):
    return _mm_raw(a, b, dims, passes), (a, b)


def _mm_bwd(dims, passes, res, ct):
    a, b = res
    if dims == "nn":
        return _mm_raw(ct, b, "nt", passes), _mm_raw(a, ct, "tn", passes)
    if dims == "nt":
        return _mm_raw(ct, b, "nn", passes), _mm_raw(ct, a, "tn", passes)
    return _mm_raw(b, ct, "nt", passes), _mm_raw(a, ct, "nn", passes)


_mm.defvjp(_mm_fwd, _mm_bwd)


def _acc(ref, val, first):
    @pl.when(first)
    def _():
        ref[...] = val

    @pl.when(jnp.logical_not(first))
    def _():
        ref[...] += val


def _colsum(v):
    return jnp.sum(v, axis=0, keepdims=True)


def _rms_parts(x):
    r = lax.rsqrt(jnp.mean(x * x, axis=-1, keepdims=True) + NORM_EPS)
    return x * r, r


def _rms_bwd(dy, xh, r, w):
    dxh = dy * w
    dw = _colsum(dy * xh)
    dx = r * (dxh - xh * jnp.mean(dxh * xh, axis=-1, keepdims=True))
    return dx, dw


def _norm_mod(x, w, sc, sh):
    xh, _ = _rms_parts(x)
    return (xh * w) * (1.0 + sc) + sh


def _norm_mod_bwd(dy, x, w, sc):
    xh, r = _rms_parts(x)
    n = xh * w
    dsh = _colsum(dy)
    dsc = _colsum(dy * n)
    dx, dw = _rms_bwd(dy * (1.0 + sc), xh, r, w)
    return dx, dw, dsc, dsh


def _softplus(x):
    return jnp.maximum(x, 0.0) + jnp.log1p(jnp.exp(-jnp.abs(x)))


def _silu(x):
    return x * jax.nn.sigmoid(x)


def _silu_grad(x):
    s = jax.nn.sigmoid(x)
    return s * (1.0 + x * (1.0 - s))


def _roll_dn(x, d):
    return x if d == 0 else pltpu.roll(x, d, 0)


def _roll_up(x, d):
    return x if d == 0 else pltpu.roll(x, x.shape[0] - d, 0)


def _proj_fwd(x, nw, sc, sh, win):
    S = x.shape[0]
    tm = min(512, S)

    def body(x_ref, nw_ref, sc_ref, sh_ref, w_ref, proj_ref, hb_ref):
        hb = _norm_mod(x_ref[...], nw_ref[...], sc_ref[...], sh_ref[...]).astype(BF16)
        hb_ref[...] = hb
        proj_ref[...] = _dot(hb, w_ref[...])

    vec = _const((1, D_MODEL))
    return pl.pallas_call(
        body, name="proj_fwd", grid=(S // tm,),
        in_specs=[_row(tm, D_MODEL), vec, vec, vec, _const((D_MODEL, IN_PAD))],
        out_specs=[_row(tm, IN_PAD), _row(tm, D_MODEL)],
        out_shape=[_sds((S, IN_PAD)), _sds((S, D_MODEL), BF16)],
        compiler_params=_params(("arbitrary",)),
    )(x, nw, sc, sh, win)


def _proj_bwd(dx1, x, dlx, dly, dqkv, dz, dba, nw, sc, win):
    S = x.shape[0]
    tm = min(512, S)

    def body(dx1_ref, x_ref, dlx_ref, dly_ref, dqkv_ref, dz_ref, dba_ref, nw_ref, sc_ref, w_ref,
             dx_ref, dpb_ref, dnw_ref, dsc_ref, dsh_ref):
        i = pl.program_id(0)
        dpb = jnp.concatenate([dlx_ref[...], dly_ref[...], dqkv_ref[...], dz_ref[...], dba_ref[...]],
                              axis=-1).astype(BF16)
        dpb_ref[...] = dpb
        dh = _dot_nt(dpb, w_ref[...])
        dx, dnw, dsc, dsh = _norm_mod_bwd(dh, x_ref[...], nw_ref[...], sc_ref[...])
        dx_ref[...] = dx1_ref[...] + dx
        _acc(dnw_ref, dnw, i == 0)
        _acc(dsc_ref, dsc, i == 0)
        _acc(dsh_ref, dsh, i == 0)

    vec = _const((1, D_MODEL))
    return pl.pallas_call(
        body, name="proj_bwd", grid=(S // tm,),
        in_specs=[_row(tm, D_MODEL), _row(tm, D_MODEL), _row(tm, LRU_W), _row(tm, LRU_W), _row(tm, 3 * GDN_W),
                  _row(tm, GDN_W), _row(tm, LANES), vec, vec,
                  _const((D_MODEL, IN_PAD))],
        out_specs=[_row(tm, D_MODEL), _row(tm, IN_PAD), vec, vec, vec],
        out_shape=[_sds((S, D_MODEL)), _sds((S, IN_PAD), BF16), _sds((1, D_MODEL)), _sds((1, D_MODEL)),
                   _sds((1, D_MODEL))],
        compiler_params=_params(("arbitrary",)),
    )(dx1, x, dlx, dly, dqkv, dz, dba, nw, sc, win)


def _conv_taps(xx, w, tm):
    y = _roll_dn(xx, 3)[SUBLANES:] * w[0:1]
    y = y + _roll_dn(xx, 2)[SUBLANES:] * w[1:2]
    y = y + _roll_dn(xx, 1)[SUBLANES:] * w[2:3]
    y = y + xx[SUBLANES:] * w[3:4]
    return y


def _conv_fwd(src, col0, C, w8, b, act, name):
    S = src.shape[0]
    tm = min(512, S)
    tc = 512
    hb = tm // SUBLANES
    cb0 = col0 // tc

    def body(x_ref, p_ref, w_ref, b_ref, y_ref):
        i = pl.program_id(0)
        prev = jnp.where(i > 0, p_ref[...], 0.0)
        xx = jnp.concatenate([prev, x_ref[...]], axis=0)
        y = _conv_taps(xx, w_ref[...], tm) + b_ref[...]
        y_ref[...] = _silu(y) if act else y

    return pl.pallas_call(
        body, name=name, grid=(S // tm, C // tc),
        in_specs=[pl.BlockSpec((tm, tc), lambda i, j: (i, cb0 + j)),
                  pl.BlockSpec((SUBLANES, tc), lambda i, j: (jnp.maximum(i * hb - 1, 0), cb0 + j)),
                  pl.BlockSpec((SUBLANES, tc), lambda i, j: (0, j)),
                  pl.BlockSpec((1, tc), lambda i, j: (0, j))],
        out_specs=pl.BlockSpec((tm, tc), lambda i, j: (i, j)),
        out_shape=_sds((S, C)),
        compiler_params=_params(("arbitrary", "arbitrary")),
    )(src, src, w8, b)


def _conv_bwd(src, col0, C, w8, b, dyact, act, name):
    S = src.shape[0]
    tm = min(512, S)
    tc = 512
    hb = tm // SUBLANES
    nt = S // tm
    cb0 = col0 // tc
    last_hb = S // SUBLANES - 1

    def body(x_ref, p_ref, n_ref, dy_ref, dyn_ref, w_ref, b_ref, dx_ref, dw_ref, db_ref):
        i = pl.program_id(1)
        w = w_ref[...]
        prev = jnp.where(i > 0, p_ref[...], 0.0)
        xx = jnp.concatenate([prev, x_ref[...], n_ref[...]], axis=0)
        dy = jnp.concatenate([dy_ref[...], jnp.where(i < nt - 1, dyn_ref[...], 0.0)], axis=0)
        if act:
            ypre = _conv_taps(xx, w, tm + SUBLANES) + b_ref[...]
            dy = dy * _silu_grad(ypre)
        dx = dy[:tm] * w[3:4]
        for d in (1, 2, 3):
            dx = dx + _roll_up(dy, d)[:tm] * w[3 - d:4 - d]
        dx_ref[...] = dx
        xt = xx[:tm + SUBLANES]
        dyt = dy[:tm]
        rows = [_colsum(dyt * _roll_dn(xt, 3 - k)[SUBLANES:]) for k in range(4)]
        dw = jnp.concatenate(rows + [jnp.zeros((SUBLANES - 4, tc), F32)], axis=0)
        _acc(dw_ref, dw, i == 0)
        _acc(db_ref, _colsum(dyt), i == 0)

    return pl.pallas_call(
        body, name=name, grid=(C // tc, nt),
        in_specs=[pl.BlockSpec((tm, tc), lambda j, i: (i, cb0 + j)),
                  pl.BlockSpec((SUBLANES, tc), lambda j, i: (jnp.maximum(i * hb - 1, 0), cb0 + j)),
                  pl.BlockSpec((SUBLANES, tc), lambda j, i: (jnp.minimum((i + 1) * hb, last_hb), cb0 + j)),
                  pl.BlockSpec((tm, tc), lambda j, i: (i, j)),
                  pl.BlockSpec((SUBLANES, tc), lambda j, i: (jnp.minimum((i + 1) * hb, last_hb), j)),
                  pl.BlockSpec((SUBLANES, tc), lambda j, i: (0, j)),
                  pl.BlockSpec((1, tc), lambda j, i: (0, j))],
        out_specs=[pl.BlockSpec((tm, tc), lambda j, i: (i, j)),
                   pl.BlockSpec((SUBLANES, tc), lambda j, i: (0, j)),
                   pl.BlockSpec((1, tc), lambda j, i: (0, j))],
        out_shape=[_sds((S, C)), _sds((SUBLANES, C)), _sds((1, C))],
        compiler_params=_params(("arbitrary", "arbitrary")),
    )(src, src, src, dyact, dyact, w8, b)


def _lru_ab(pre_a, pre_x, xr, lam):
    r = jax.nn.sigmoid(pre_a)
    g = jax.nn.sigmoid(pre_x)
    log_sig = -_softplus(-lam)
    log_a = LRU_C * r * log_sig
    a = jnp.exp(log_a)
    t = jnp.tanh(log_a)
    mult = jnp.sqrt(jnp.maximum(-2.0 * t / (1.0 - t), 1e-12))
    return a, mult * (g * xr)


def _lru_tail(h, ly, lnw):
    xh, _ = _rms_parts(h * jax.nn.gelu(ly))
    return xh * lnw


def _scan_down(a, b):
    n = a.shape[0]
    row = lax.broadcasted_iota(jnp.int32, a.shape, 0)
    d = 1
    while d < n:
        keep = row >= d
        a_s = jnp.where(keep, _roll_dn(a, d), 1.0)
        b_s = jnp.where(keep, _roll_dn(b, d), 0.0)
        b = a * b_s + b
        a = a * a_s
        d *= 2
    return a, b


def _scan_up(a, b):
    n = a.shape[0]
    row = lax.broadcasted_iota(jnp.int32, a.shape, 0)
    d = 1
    while d < n:
        keep = row < n - d
        a_s = jnp.where(keep, _roll_up(a, d), 1.0)
        b_s = jnp.where(keep, _roll_up(b, d), 0.0)
        b = a * b_s + b
        a = a * a_s
        d *= 2
    return a, b


LRU_TM = 256


def _lru_fwd(xr, proj, wa, ba, wx, bx, lam, lnw):
    S = xr.shape[0]
    tm = min(LRU_TM, S)

    def body(xr_ref, ly_ref, wa_ref, ba_ref, wx_ref, bx_ref, lam_ref, lnw_ref, out_ref, h_ref, carry):
        i = pl.program_id(0)

        @pl.when(i == 0)
        def _():
            carry[...] = jnp.zeros_like(carry)

        x = xr_ref[...]
        xb = x.astype(BF16)
        pre_a = _dot(xb, wa_ref[...]) + ba_ref[...]
        pre_x = _dot(xb, wx_ref[...]) + bx_ref[...]
        a, b = _lru_ab(pre_a, pre_x, x, lam_ref[...])
        ca, hl = _scan_down(a, b)
        h = hl + ca * carry[0:1, :]
        carry[0:1, :] = h[tm - 1:tm, :]
        h_ref[...] = h
        out_ref[...] = _lru_tail(h, ly_ref[...], lnw_ref[...])

    vec = _const((1, LRU_W))
    mat = _const((LRU_W, LRU_W))
    return pl.pallas_call(
        body, name="lru_fwd", grid=(S // tm,),
        in_specs=[_row(tm, LRU_W), _row(tm, LRU_W, 1), mat, vec, mat, vec, vec, vec],
        out_specs=[_row(tm, LRU_W), _row(tm, LRU_W)],
        out_shape=[_sds((S, LRU_W)), _sds((S, LRU_W))],
        scratch_shapes=[pltpu.VMEM((SUBLANES, LRU_W), F32)],
        compiler_params=_params(("arbitrary",)),
    )(xr, proj, wa, ba, wx, bx, lam, lnw)


def _lru_bwd(dout, xr, proj, h, wa, ba, wx, bx, lam, lnw):
    S = xr.shape[0]
    tm = min(LRU_TM, S)
    nt = S // tm
    hb = tm // SUBLANES

    def rev(col=0):
        return pl.BlockSpec((tm, LRU_W), lambda i: (nt - 1 - i, col))

    def body(dout_ref, xr_ref, ly_ref, h_ref, hp_ref, wa_ref, ba_ref, wx_ref, bx_ref, lam_ref, lnw_ref,
             dxr_ref, dly_ref, dwa_ref, dba_ref, dwx_ref, dbx_ref, dlam_ref, dlnw_ref, carry):
        i = pl.program_id(0)
        first = i == 0

        @pl.when(first)
        def _():
            carry[...] = jnp.zeros_like(carry)

        x = xr_ref[...]
        xb = x.astype(BF16)
        pre_a = _dot(xb, wa_ref[...]) + ba_ref[...]
        pre_x = _dot(xb, wx_ref[...]) + bx_ref[...]
        (a, b), ab_vjp = jax.vjp(_lru_ab, pre_a, pre_x, x, lam_ref[...])
        h_t = h_ref[...]
        _, tail_vjp = jax.vjp(_lru_tail, h_t, ly_ref[...], lnw_ref[...])
        dh, dly, dlnw = tail_vjp(dout_ref[...])
        dly_ref[...] = dly
        row = lax.broadcasted_iota(jnp.int32, a.shape, 0)
        a_next = jnp.where(row == tm - 1, carry[0:1, :], _roll_up(a, 1))
        ca, gl = _scan_up(a_next, dh)
        g = gl + ca * carry[1:2, :]
        carry[0:1, :] = a[0:1, :]
        carry[1:2, :] = g[0:1, :]
        h_before = jnp.where(i == nt - 1, 0.0, hp_ref[SUBLANES - 1:SUBLANES, :])
        h_prev = jnp.where(row == 0, h_before, _roll_dn(h_t, 1))
        dpa, dpx, dx, dlam = ab_vjp((g * h_prev, g))
        dpab = dpa.astype(BF16)
        dpxb = dpx.astype(BF16)
        dxr_ref[...] = dx + _dot_nt(dpab, wa_ref[...]) + _dot_nt(dpxb, wx_ref[...])
        _acc(dwa_ref, _dot_tn(xb, dpab), first)
        _acc(dwx_ref, _dot_tn(xb, dpxb), first)
        _acc(dba_ref, _colsum(dpa), first)
        _acc(dbx_ref, _colsum(dpx), first)
        _acc(dlam_ref, dlam, first)
        _acc(dlnw_ref, dlnw, first)

    vec = _const((1, LRU_W))
    mat = _const((LRU_W, LRU_W))
    return pl.pallas_call(
        body, name="lru_bwd", grid=(nt,),
        in_specs=[rev(), rev(), rev(1), rev(),
                  pl.BlockSpec((SUBLANES, LRU_W), lambda i: (jnp.maximum((nt - 1 - i) * hb - 1, 0), 0)),
                  mat, vec, mat, vec, vec, vec],
        out_specs=[rev(), rev(), mat, vec, mat, vec, vec, vec],
        out_shape=[_sds((S, LRU_W)), _sds((S, LRU_W)), _sds((LRU_W, LRU_W)), _sds((1, LRU_W)),
                   _sds((LRU_W, LRU_W)), _sds((1, LRU_W)), _sds((1, LRU_W)), _sds((1, LRU_W))],
        scratch_shapes=[pltpu.VMEM((SUBLANES, LRU_W), F32)],
        compiler_params=_params(("arbitrary",)),
    )(dout, xr, proj, h, h, wa, ba, wx, bx, lam, lnw)


def _lane_pick(row_or_tile, lane):
    idx = lax.broadcasted_iota(jnp.int32, row_or_tile.shape, 1)
    return jnp.sum(jnp.where(idx == lane, row_or_tile, 0.0), axis=-1, keepdims=True)


def _unit_lower_inverses(los):
    n = los[0].shape[0]
    ri = lax.broadcasted_iota(jnp.int32, (n, n), 0)
    ci = lax.broadcasted_iota(jnp.int32, (n, n), 1)
    eye = (ri == ci).astype(F32)

    def lower_left_of(s):
        same_block = (ri & ~(2 * s - 1)) == (ci & ~(2 * s - 1))
        return same_block & ((ri & s) != 0) & ((ci & s) == 0)

    invs = [eye - jnp.where(lower_left_of(1), lo, 0.0) for lo in los]
    s = 2
    while s < n:
        lower_left = lower_left_of(s)
        left = [_mm_raw(inv, jnp.where(lower_left, lo, 0.0), "nn", 3) for inv, lo in zip(invs, los)]
        invs = [inv - _mm_raw(t, inv, "nn", 3) for inv, t in zip(invs, left)]
        s *= 2
    return invs


@jax.custom_vjp
def _unit_lower_inverses_diff(los):
    return _unit_lower_inverses(los)


def _unit_lower_inverses_fwd(los):
    invs = _unit_lower_inverses(los)
    return invs, invs


def _unit_lower_inverses_bwd(invs, cts):
    right = [_mm_raw(ct, inv, "nt", 3) for ct, inv in zip(cts, invs)]
    return ([-_mm_raw(inv, r, "tn", 3) for inv, r in zip(invs, right)],)


_unit_lower_inverses_diff.defvjp(_unit_lower_inverses_fwd, _unit_lower_inverses_bwd)


GDN_STEP_CHUNKS = 2


def _gdn_chunk(qs, ks, vs, bas, alog, dtb, states, inverses=_unit_lower_inverses, mm=_mm_raw):
    C = qs[0].shape[0]
    nchunks = len(bas)
    items = [(c, h) for c in range(nchunks) for h in range(HEADS)]
    ri = lax.broadcasted_iota(jnp.int32, (C, C), 0)
    ci = lax.broadcasted_iota(jnp.int32, (C, C), 1)
    causal = ri >= ci
    strict = ri > ci
    tri = causal.astype(F32)
    betas = [jax.nn.sigmoid(_lane_pick(bas[c], h)) for c, h in items]
    gs = [-jnp.exp(_lane_pick(alog, h)) * _softplus(_lane_pick(bas[c], h + HEADS) + _lane_pick(dtb, h))
          for c, h in items]
    qn = [q * lax.rsqrt(jnp.sum(q * q, axis=-1, keepdims=True) + 1e-6) * (HEAD_DIM ** -0.5) for q in qs]
    kn = [k * lax.rsqrt(jnp.sum(k * k, axis=-1, keepdims=True) + 1e-6) for k in ks]
    gc = [_hdot(tri, jnp.broadcast_to(g, (C, C))) for g in gs]
    decay = [jnp.where(causal, jnp.exp(jnp.where(causal, c - c.T, 0.0)), 0.0) for c in gc]
    eg = [jnp.exp(c) for c in gc]
    kb = [k * b for k, b in zip(kn, betas)]
    vb = [v * b for v, b in zip(vs, betas)]
    los = [jnp.where(strict, mm(a, k, "nt", 1) * d, 0.0) for a, k, d in zip(kb, kn, decay)]
    attn = [jnp.where(causal, mm(q, k, "nt", 1) * d, 0.0) for q, k, d in zip(qn, kn, decay)]
    tinv = inverses(los)
    u = [mm(t, x, "nn", 3) for t, x in zip(tinv, vb)]
    w = [mm(t, a * e, "nn", 3) for t, a, e in zip(tinv, kb, eg)]
    g_last = [c[C - 1:C, :] for c in gc]
    k_tail = [k * jnp.exp(gl - c) for k, gl, c in zip(kn, g_last, gc)]
    q_dec = [q * e for q, e in zip(qn, eg)]
    outs = []
    for c in range(nchunks):
        idx = range(c * HEADS, (c + 1) * HEADS)
        v_new = [u[i] - mm(w[i], s, "nn", 1) for i, s in zip(idx, states)]
        o_state = [mm(q_dec[i], s, "nn", 1) for i, s in zip(idx, states)]
        outs += [a + mm(attn[i], vn, "nn", 1) for i, a, vn in zip(idx, o_state, v_new)]
        states = [s * jnp.exp(g_last[i]) + mm(k_tail[i], vn, "tn", 1) for i, s, vn in zip(idx, states, v_new)]
    return outs, states


def _gdn_fwd(qkv, proj, alog, dtb, gather=()):
    S = qkv.shape[0]
    per = min(GDN_STEP_CHUNKS, S // CHUNK)
    T = per * CHUNK
    nc = S // T
    nk = len(gather)
    assert CHUNK == HEAD_DIM

    def body(*refs):
        q_ref, k_ref, v_ref, ba_ref, alog_ref, dtb_ref = refs[:6]
        o_ref, st_ref = refs[6 + nk:8 + nk]
        state = refs[8 + 2 * nk]
        if nk:
            start, finish = _gather_steps(refs[6:6 + nk], refs[8 + nk:8 + 2 * nk], *refs[9 + 2 * nk:])
            pl.when(pl.program_id(0) == 0)(start)

        @pl.when(pl.program_id(0) == 0)
        def _():
            state[...] = jnp.zeros_like(state)

        sls = [slice(hd * HEAD_DIM, (hd + 1) * HEAD_DIM) for hd in range(HEADS)]
        rows = [slice(c * CHUNK, (c + 1) * CHUNK) for c in range(per)]
        s0 = [state[hd] for hd in range(HEADS)]
        for hd in range(HEADS):
            st_ref[hd, 0] = s0[hd]
        o, s1 = _gdn_chunk([q_ref[r, sl] for r in rows for sl in sls], [k_ref[r, sl] for r in rows for sl in sls],
                           [v_ref[r, sl] for r in rows for sl in sls], [ba_ref[r, :] for r in rows],
                           alog_ref[...], dtb_ref[...], s0)
        for c, r in enumerate(rows):
            for hd in range(HEADS):
                o_ref[r, sls[hd]] = o[c * HEADS + hd]
        for hd in range(HEADS):
            state[hd] = s1[hd]
        if nk:
            pl.when(pl.program_id(0) == nc - 1)(finish)

    def col(j):
        return pl.BlockSpec((T, GDN_W),lambda n: (n, j))

    vec = _const((1, LANES))
    outs = pl.pallas_call(
        body, name="gdn_fwd", grid=(nc,),
        in_specs=[col(0), col(1), col(2), pl.BlockSpec((T, LANES),lambda n: (n, IN_PAD // LANES - 1)), vec, vec]
        + _hbm_specs(nk),
        out_specs=[col(0), pl.BlockSpec((HEADS, 1, HEAD_DIM, HEAD_DIM), lambda n: (0, n, 0, 0))] + _hbm_specs(nk),
        out_shape=[_sds((S, GDN_W)), _sds((HEADS, nc, HEAD_DIM, HEAD_DIM))] + (_gather_out_shapes(gather) if nk else []),
        scratch_shapes=[pltpu.VMEM((HEADS, HEAD_DIM, HEAD_DIM), F32)] + (_gather_scratch(nk) if nk else []),
        compiler_params=_params(("arbitrary",)),
    )(qkv, qkv, qkv, proj, alog, dtb, *gather)
    return outs[0], outs[1], list(outs[2:])


def _gdn_bwd(do, qkv, proj, states, alog, dtb, scatter=()):
    S = qkv.shape[0]
    per = min(GDN_STEP_CHUNKS, S // CHUNK)
    T = per * CHUNK
    nc = S // T
    nk = len(scatter)

    def body(*refs):
        do_ref, q_ref, k_ref, v_ref, ba_ref, st_ref, alog_ref, dtb_ref = refs[:8]
        dqkv_ref, dba_ref, dalog_ref, ddtb_ref = refs[8 + nk:12 + nk]
        dstate = refs[12 + 2 * nk]
        n = pl.program_id(0)
        if nk:
            start, finish = _scatter_steps(refs[8:8 + nk], refs[12 + nk:12 + 2 * nk], *refs[13 + 2 * nk:])
            pl.when(n == 0)(start)

        @pl.when(n == 0)
        def _():
            dstate[...] = jnp.zeros_like(dstate)

        sls = [slice(hd * HEAD_DIM, (hd + 1) * HEAD_DIM) for hd in range(HEADS)]
        rows = [slice(c * CHUNK, (c + 1) * CHUNK) for c in range(per)]
        fn = functools.partial(_gdn_chunk, inverses=_unit_lower_inverses_diff, mm=_mm)
        _, vjp = jax.vjp(fn, [q_ref[r, sl] for r in rows for sl in sls], [k_ref[r, sl] for r in rows for sl in sls],
                         [v_ref[r, sl] for r in rows for sl in sls], [ba_ref[r, :] for r in rows],
                         alog_ref[...], dtb_ref[...], [st_ref[hd, 0] for hd in range(HEADS)])
        dq, dk, dv, dba, dalog, ddtb, ds = vjp(([do_ref[r, sl] for r in rows for sl in sls],
                                                [dstate[hd] for hd in range(HEADS)]))
        for c, r in enumerate(rows):
            for hd in range(HEADS):
                i = c * HEADS + hd
                dqkv_ref[r, sls[hd]] = dq[i]
                dqkv_ref[r, GDN_W + hd * HEAD_DIM:GDN_W + (hd + 1) * HEAD_DIM] = dk[i]
                dqkv_ref[r, 2 * GDN_W + hd * HEAD_DIM:2 * GDN_W + (hd + 1) * HEAD_DIM] = dv[i]
            dba_ref[r, :] = dba[c]
        for hd in range(HEADS):
            dstate[hd] = ds[hd]
        _acc(dalog_ref, dalog, n == 0)
        _acc(ddtb_ref, ddtb, n == 0)
        if nk:
            pl.when(n == nc - 1)(finish)

    def col(j):
        return pl.BlockSpec((T, GDN_W),lambda n: (nc - 1 - n, j))

    vec = _const((1, LANES))
    outs = pl.pallas_call(
        body, name="gdn_bwd", grid=(nc,),
        in_specs=[col(0), col(0), col(1), col(2),
                  pl.BlockSpec((T, LANES),lambda n: (nc - 1 - n, IN_PAD // LANES - 1)),
                  pl.BlockSpec((HEADS, 1, HEAD_DIM, HEAD_DIM), lambda n: (0, nc - 1 - n, 0, 0)), vec, vec]
        + _hbm_specs(nk),
        out_specs=[pl.BlockSpec((T, 3 * GDN_W),lambda n: (nc - 1 - n, 0)),
                   pl.BlockSpec((T, LANES),lambda n: (nc - 1 - n, 0)), vec, vec] + _hbm_specs(nk),
        out_shape=[_sds((S, 3 * GDN_W)), _sds((S, LANES)), _sds((1, LANES)), _sds((1, LANES))]
        + [_sds(p.shape, p.dtype) for p in scatter],
        scratch_shapes=[pltpu.VMEM((HEADS, HEAD_DIM, HEAD_DIM), F32)] + (_scatter_scratch(nk) if nk else []),
        compiler_params=_params(("arbitrary",)),
    )(do, qkv, qkv, qkv, proj, states, alog, dtb, *scatter)
    return outs[0], outs[1], outs[2], outs[3], list(outs[4:])


def _gdn_gate(o, z, gnw):
    outs = []
    for hd in range(HEADS):
        sl = slice(hd * HEAD_DIM, (hd + 1) * HEAD_DIM)
        xh, _ = _rms_parts(o[:, sl])
        outs.append(xh * gnw * _silu(z[:, sl]))
    return jnp.concatenate(outs, axis=-1)


def _out_fwd(x, out_lru, o, proj, gnw, g1, wout):
    S = x.shape[0]
    tm = min(512, S)

    def body(x_ref, lru_ref, o_ref, z_ref, gnw_ref, g1_ref, w_ref, x1_ref, cat_ref):
        cat = jnp.concatenate([lru_ref[...], _gdn_gate(o_ref[...], z_ref[...], gnw_ref[...])], axis=-1).astype(BF16)
        cat_ref[...] = cat
        x1_ref[...] = x_ref[...] + g1_ref[...] * _dot(cat, w_ref[...])

    return pl.pallas_call(
        body, name="out_fwd", grid=(S // tm,),
        in_specs=[_row(tm, D_MODEL), _row(tm, LRU_W), _row(tm, GDN_W), _row(tm, GDN_W, 5), _const((1, LANES)),
                  _const((1, D_MODEL)), _const((D_MODEL, D_MODEL))],
        out_specs=[_row(tm, D_MODEL), _row(tm, D_MODEL)],
        out_shape=[_sds((S, D_MODEL)), _sds((S, D_MODEL), BF16)],
        compiler_params=_params(("arbitrary",)),
    )(x, out_lru, o, proj, gnw, g1, wout)


def _out_bwd(dx1, cat, o, proj, gnw, g1, wout):
    S = dx1.shape[0]
    tm = min(512, S)

    def body(dx1_ref, cat_ref, o_ref, z_ref, gnw_ref, g1_ref, w_ref,
             dlru_ref, do_ref, dz_ref, dmb_ref, dgnw_ref, dg1_ref):
        i = pl.program_id(0)
        d1 = dx1_ref[...]
        mix = _dot(cat_ref[...], w_ref[...])
        _acc(dg1_ref, _colsum(d1 * mix), i == 0)
        dmb = (d1 * g1_ref[...]).astype(BF16)
        dmb_ref[...] = dmb
        dcat = _dot_nt(dmb, w_ref[...])
        dlru_ref[...] = dcat[:, :LRU_W]
        _, vjp = jax.vjp(_gdn_gate, o_ref[...], z_ref[...], gnw_ref[...])
        do, dz, dgnw = vjp(dcat[:, LRU_W:])
        do_ref[...] = do
        dz_ref[...] = dz
        _acc(dgnw_ref, dgnw, i == 0)

    return pl.pallas_call(
        body, name="out_bwd", grid=(S // tm,),
        in_specs=[_row(tm, D_MODEL), _row(tm, D_MODEL), _row(tm, GDN_W), _row(tm, GDN_W, 5), _const((1, LANES)),
                  _const((1, D_MODEL)), _const((D_MODEL, D_MODEL))],
        out_specs=[_row(tm, LRU_W), _row(tm, GDN_W), _row(tm, GDN_W), _row(tm, D_MODEL), _const((1, LANES)),
                   _const((1, D_MODEL))],
        out_shape=[_sds((S, LRU_W)), _sds((S, GDN_W)), _sds((S, GDN_W)), _sds((S, D_MODEL), BF16), _sds((1, LANES)),
                   _sds((1, D_MODEL))],
        compiler_params=_params(("arbitrary",)),
    )(dx1, cat, o, proj, gnw, g1, wout)


MLP_TM = 256


def _load_once(step, pairs, sem):
    @pl.when(step == 0)
    def _():
        copies = [pltpu.make_async_copy(src, dst, sem.at[k]) for k, (src, dst) in enumerate(pairs)]
        for cp in copies:
            cp.start()
        for cp in copies:
            cp.wait()


def _mlp_fwd(x1, nw, sc, sh, g2, wup, wdown, gather=()):
    S = x1.shape[0]
    tm = min(MLP_TM, S)
    nt = S // tm
    nk = len(gather)

    def body(*refs):
        x_ref, nw_ref, sc_ref, sh_ref, g2_ref, wup_hbm, wdown_hbm = refs[:7]
        x2_ref = refs[7 + nk]
        wup, wdown, sem = refs[8 + 2 * nk:11 + 2 * nk]
        step = pl.program_id(0)
        if nk:
            start, finish = _gather_steps(refs[7:7 + nk], refs[8 + nk:8 + 2 * nk], *refs[11 + 2 * nk:])
            pl.when(step == 0)(start)
        _load_once(step, [(wup_hbm, wup), (wdown_hbm, wdown)], sem)
        x = x_ref[...]
        hb = _norm_mod(x, nw_ref[...], sc_ref[...], sh_ref[...]).astype(BF16)
        r = jnp.maximum(_dot(hb, wup[...]), 0.0)
        x2_ref[...] = x + g2_ref[...] * _dot((r * r).astype(BF16), wdown[...])
        if nk:
            pl.when(step == nt - 1)(finish)

    vec = _const((1, D_MODEL))
    anyspec = pl.BlockSpec(memory_space=pl.ANY)
    outs = pl.pallas_call(
        body, name="mlp_fwd", grid=(nt,),
        in_specs=[_row(tm, D_MODEL), vec, vec, vec, vec, anyspec, anyspec] + _hbm_specs(nk),
        out_specs=[_row(tm, D_MODEL)] + _hbm_specs(nk),
        out_shape=[_sds((S, D_MODEL))] + (_gather_out_shapes(gather) if nk else []),
        scratch_shapes=[pltpu.VMEM((D_MODEL, D_FF), BF16), pltpu.VMEM((D_FF, D_MODEL), BF16),
                        pltpu.SemaphoreType.DMA((2,))] + (_gather_scratch(nk) if nk else []),
        compiler_params=_params(("arbitrary",)),
    )(x1, nw, sc, sh, g2, wup, wdown, *gather)
    return outs[0], list(outs[1:])


def _mlp_bwd(dx2, x1, nw, sc, sh, g2, wup, wdown):
    S = x1.shape[0]
    tm = min(MLP_TM, S)

    def body(dx2_ref, x_ref, nw_ref, sc_ref, sh_ref, g2_ref, wup_hbm, wdown_hbm,
             dx1_ref, hb_ref, dupb_ref, actb_ref, d2b_ref, dnw_ref, dsc_ref, dsh_ref, wup, wdown, sem):
        i = pl.program_id(0)
        _load_once(i, [(wup_hbm, wup), (wdown_hbm, wdown)], sem)
        x = x_ref[...]
        d2 = dx2_ref[...]
        hb = _norm_mod(x, nw_ref[...], sc_ref[...], sh_ref[...]).astype(BF16)
        hb_ref[...] = hb
        r = jnp.maximum(_dot(hb, wup[...]), 0.0)
        actb = (r * r).astype(BF16)
        actb_ref[...] = actb
        d2b_ref[...] = d2.astype(BF16)
        ddb = (d2 * g2_ref[...]).astype(BF16)
        dupb = (_dot_nt(ddb, wdown[...]) * (2.0 * r)).astype(BF16)
        dupb_ref[...] = dupb
        dh = _dot_nt(dupb, wup[...])
        dx, dnw, dsc, dsh = _norm_mod_bwd(dh, x, nw_ref[...], sc_ref[...])
        dx1_ref[...] = d2 + dx
        _acc(dnw_ref, dnw, i == 0)
        _acc(dsc_ref, dsc, i == 0)
        _acc(dsh_ref, dsh, i == 0)

    vec = _const((1, D_MODEL))
    anyspec = pl.BlockSpec(memory_space=pl.ANY)
    return pl.pallas_call(
        body, name="mlp_bwd", grid=(S // tm,),
        in_specs=[_row(tm, D_MODEL), _row(tm, D_MODEL), vec, vec, vec, vec, anyspec, anyspec],
        out_specs=[_row(tm, D_MODEL), _row(tm, D_MODEL), _row(tm, D_FF), _row(tm, D_FF), _row(tm, D_MODEL),
                   vec, vec, vec],
        out_shape=[_sds((S, D_MODEL)), _sds((S, D_MODEL), BF16), _sds((S, D_FF), BF16), _sds((S, D_FF), BF16),
                   _sds((S, D_MODEL), BF16), _sds((1, D_MODEL)), _sds((1, D_MODEL)), _sds((1, D_MODEL))],
        scratch_shapes=[pltpu.VMEM((D_MODEL, D_FF), BF16), pltpu.VMEM((D_FF, D_MODEL), BF16),
                        pltpu.SemaphoreType.DMA((2,))],
        compiler_params=_params(("arbitrary",)),
    )(dx2, x1, nw, sc, sh, g2, wup, wdown)


def _dw_down(act, d2b, g2, wdown, sharded, out_dtype):
    K, M = act.shape
    N = d2b.shape[1]
    tk = min(2048, K)
    nk = K // tk
    if sharded:
        h = M // (2 * N_CHIPS)
        tm = 2 * h
        out_spec = pl.BlockSpec((2, 1, h, N), lambda i, k: (0, i, 0, 0))
        out_shape = _sds((2, N_CHIPS, h, N), out_dtype)
    else:
        tm = min(512, M)
        out_spec = pl.BlockSpec((tm, N), lambda i, k: (i, 0))
        out_shape = _sds((M, N), out_dtype)

    def body(a_ref, b_ref, g2_ref, w_ref, o_ref, dg2_ref, acc):
        i = pl.program_id(0)
        k = pl.program_id(1)
        _acc(acc, _dot_tn(a_ref[...], b_ref[...]), k == 0)

        @pl.when(k == nk - 1)
        def _():
            g = acc[...]
            _acc(dg2_ref, _colsum(g * w_ref[...].astype(F32)), i == 0)
            out = (g * g2_ref[...]).astype(o_ref.dtype)
            if sharded:
                o_ref[0, 0] = out[:h]
                o_ref[1, 0] = out[h:]
            else:
                o_ref[...] = out

    vec = pl.BlockSpec((1, N), lambda i, k: (0, 0))
    return pl.pallas_call(
        body, name="dw_down", grid=(M // tm, nk),
        in_specs=[pl.BlockSpec((tk, tm), lambda i, k: (k, i)), pl.BlockSpec((tk, N), lambda i, k: (k, 0)), vec,
                  pl.BlockSpec((tm, N), lambda i, k: (i, 0))],
        out_specs=[out_spec, vec], out_shape=[out_shape, _sds((1, N))],
        scratch_shapes=[pltpu.VMEM((tm, N), F32)],
        compiler_params=_params(("arbitrary", "arbitrary")),
    )(act, d2b, g2, wdown)


def _matmul_tn(a, b, name, shards=None, out_dtype=F32):
    K, M = a.shape
    N = b.shape[1]
    tk = min(2048, K)
    if shards == "cols":
        tm, tn = M, N // N_CHIPS
        out_spec = pl.BlockSpec((2, 1, M // 2, tn), lambda i, j, k: (0, j, 0, 0))
        out_shape = _sds((2, N_CHIPS, M // 2, tn))
    elif shards == "rows":
        h, tn = M // (2 * N_CHIPS), N
        tm = max(512, 2 * h)
        per_tile = tm // (2 * h)
        out_spec = pl.BlockSpec((2, per_tile, h, tn), lambda i, j, k: (0, i, 0, 0))
        out_shape = _sds((2, N_CHIPS, h, tn))
    else:
        tm = min(512, M)
        if N % 640 == 0:
            tn, tk = N, min(1024, K)
        else:
            tn = min(1024, N)
        out_spec = pl.BlockSpec((tm, tn), lambda i, j, k: (i, j))
        out_shape = _sds((M, N))
    nk = K // tk

    def body(a_ref, b_ref, o_ref, acc):
        k = pl.program_id(2)
        _acc(acc, _dot_tn(a_ref[...], b_ref[...]), k == 0)

        @pl.when(k == nk - 1)
        def _():
            if shards == "rows":
                for s in range(per_tile):
                    for half in range(2):
                        r0 = (2 * s + half) * h
                        o_ref[half, s] = acc[r0:r0 + h, :].astype(o_ref.dtype)
            elif shards == "cols":
                o_ref[0, 0] = acc[:M // 2, :].astype(o_ref.dtype)
                o_ref[1, 0] = acc[M // 2:, :].astype(o_ref.dtype)
            else:
                o_ref[...] = acc[...].reshape(o_ref.shape).astype(o_ref.dtype)

    return pl.pallas_call(
        body, name=name, grid=(M // tm, N // tn, nk),
        in_specs=[pl.BlockSpec((tk, tm), lambda i, j, k: (k, i)), pl.BlockSpec((tk, tn), lambda i, j, k: (k, j))],
        out_specs=out_spec, out_shape=_sds(out_shape.shape, out_dtype),
        scratch_shapes=[pltpu.VMEM((tm, tn), F32)],
        compiler_params=_params(("arbitrary", "arbitrary", "arbitrary")),
    )(a, b)


def _loss_head(x, target, fnw):
    S = x.shape[0]
    tm = min(512, S)

    def body(x_ref, t_ref, w_ref, dx_ref, loss_ref, dw_ref):
        i = pl.program_id(0)
        w = w_ref[...]
        xh, r = _rms_parts(x_ref[...])
        err = xh * w - t_ref[...]
        part = 0.5 * jnp.sum(jnp.mean(err * err, axis=-1, keepdims=True), axis=0, keepdims=True)
        _acc(loss_ref, jnp.broadcast_to(part, (SUBLANES, LANES)), i == 0)
        dx, dw = _rms_bwd(err * (1.0 / D_MODEL), xh, r, w)
        dx_ref[...] = dx
        _acc(dw_ref, dw, i == 0)

    vec = _const((1, D_MODEL))
    return pl.pallas_call(
        body, name="loss_head", grid=(S // tm,),
        in_specs=[_row(tm, D_MODEL), _row(tm, D_MODEL), vec],
        out_specs=[_row(tm, D_MODEL), _const((SUBLANES, LANES)), vec],
        out_shape=[_sds((S, D_MODEL)), _sds((SUBLANES, LANES)), _sds((1, D_MODEL))],
        compiler_params=_params(("arbitrary",)),
    )(x, target, fnw)


def _block_diag(w):
    eye = jnp.eye(LRU_BLOCKS, dtype=w.dtype)
    return (eye[:, None, :, None] * w[:, :, None, :]).reshape(LRU_W, LRU_W)


def _diag_blocks(m):
    m4 = m.reshape(LRU_BLOCKS, LRU_BLOCK, LRU_BLOCKS, LRU_BLOCK)
    return jnp.stack([m4[g, :, g, :] for g in range(LRU_BLOCKS)])


def _layer_fwd(x, p, mlp_shards=(), mlp_weights=None, next_shards=()):
    proj, h1b = _proj_fwd(x, p["nmw"], p["sc1"], p["sh1"], p["win"])
    xr = _conv_fwd(proj, 0, LRU_W, p["lcw"], p["lcb"], False, "conv_lru_fwd")
    out_lru, h = _lru_fwd(xr, proj, p["wa"].astype(BF16), p["ba"], p["wx"].astype(BF16), p["bx"], p["lam"], p["lnw"])
    qkv = _conv_fwd(proj, 2 * LRU_W, 3 * GDN_W, p["gcw"], p["gcb"], True, "conv_gdn_fwd")
    o, states, gathered = _gdn_fwd(qkv, proj, p["alog"], p["dtb"], mlp_shards)
    if mlp_weights is not None:
        p = {**p, **mlp_weights(gathered)}
    x1, cat = _out_fwd(x, out_lru, o, proj, p["gnw"], p["g1"], p["wout"])
    x2, gathered_next = _mlp_fwd(x1, p["nmlp"], p["sc2"], p["sh2"], p["g2"], p["wup"], p["wdown"], next_shards)
    res = dict(x=x, proj=proj, h1b=h1b, xr=xr, h=h, qkv=qkv, o=o, states=states, x1=x1, cat=cat)
    return x2, res, p, gathered_next


def _layer_bwd(dx2, p, r, sharded=False, scatter=(), early=None):
    dx1, h2b, dupb, actb, d2b, dnmlp, dsc2, dsh2 = _mlp_bwd(
        dx2, r["x1"], p["nmlp"], p["sc2"], p["sh2"], p["g2"], p["wup"], p["wdown"])
    gdt = BF16 if sharded else F32
    g_wup = _matmul_tn(h2b, dupb, "dw_up", "cols" if sharded else None, gdt)
    g_wdown, dg2 = _dw_down(actb, d2b, p["g2"], p["wdown"], sharded, gdt)
    dlru, do, dz, dmb, dgnw, dg1 = _out_bwd(dx1, r["cat"], r["o"], r["proj"], p["gnw"], p["g1"], p["wout"])
    g_wout = _matmul_tn(r["cat"], dmb, "dw_out", "rows" if sharded else None, gdt)
    if early is not None:
        scatter = list(scatter) + early([g_wout, g_wup, g_wdown])
    dqkv_act, dba, dalog, ddtb, arrived = _gdn_bwd(do, r["qkv"], r["proj"], r["states"], p["alog"], p["dtb"], scatter)
    dqkv, dgcw, _ = _conv_bwd(r["proj"], 2 * LRU_W, 3 * GDN_W, p["gcw"], p["gcb"], dqkv_act, True, "conv_gdn_bwd")
    wab = p["wa"].astype(BF16)
    wxb = p["wx"].astype(BF16)
    dxr, dly, dwa, dba_, dwx, dbx, dlam, dlnw = _lru_bwd(
        dlru, r["xr"], r["proj"], r["h"], wab, p["ba"], wxb, p["bx"], p["lam"], p["lnw"])
    dlx, dlcw, dlcb = _conv_bwd(r["proj"], 0, LRU_W, p["lcw"], p["lcb"], dxr, False, "conv_lru_bwd")
    dx, dpb, dnmw, dsc1, dsh1 = _proj_bwd(dx1, r["x"], dlx, dly, dqkv, dz, dba, p["nmw"], p["sc1"], p["win"])
    g_win = _matmul_tn(r["h1b"], dpb, "dw_in", None, gdt)
    grads = dict(nmw=dnmw, nmlp=dnmlp, sh1=dsh1, sc1=dsc1, g1=dg1, sh2=dsh2, sc2=dsc2, g2=dg2,
                 win=g_win, lcw=dlcw, lcb=dlcb, wa=dwa, ba=dba_, wx=dwx, bx=dbx, lam=dlam, lnw=dlnw,
                 gcw=dgcw, alog=dalog, dtb=ddtb, gnw=dgnw, wout=g_wout, wup=g_wup, wdown=g_wdown)
    return dx, grads, arrived


def _local_step(x, target, fnw, layers):
    res = []
    for p in layers:
        x, r, _, _ = _layer_fwd(x, p)
        res.append(r)
    dx, loss_blk, dfnw = _loss_head(x, target, fnw)
    grads = [None] * len(layers)
    for l in reversed(range(len(layers))):
        dx, grads[l], _ = _layer_bwd(dx, layers[l], res[l])
    stacked = {k: jnp.stack([g[k] for g in grads]) for k in grads[0]}
    return loss_blk[0, 0], dx, dfnw, stacked


def _prep_layers(norm_mix_w, norm_mlp_w, mod, win_b, lru_conv_w, lru_conv_b, gate_a_w, gate_a_b, gate_x_w, gate_x_b,
                 lru_lambda, lru_norm_w, gdn_conv_w, gdn_a_log, gdn_dt_bias, gdn_norm_w, wout_b, wup_b, wdown_b):
    L = norm_mix_w.shape[0]

    def vec(a):
        return a.reshape(L, 1, -1)

    def lanes(a):
        return jnp.pad(a, ((0, 0), (0, LANES - a.shape[1]))).reshape(L, 1, LANES)

    def taps(w):
        return jnp.pad(w, ((0, 0), (0, SUBLANES - w.shape[1]), (0, 0)))

    m = mod.reshape(L, N_MOD, 1, D_MODEL)
    return dict(
        nmw=vec(norm_mix_w), nmlp=vec(norm_mlp_w),
        sh1=m[:, 0], sc1=m[:, 1], g1=m[:, 2], sh2=m[:, 3], sc2=m[:, 4], g2=m[:, 5],
        win=win_b, lcw=taps(lru_conv_w), lcb=vec(lru_conv_b),
        wa=jax.vmap(_block_diag)(gate_a_w), ba=vec(gate_a_b), wx=jax.vmap(_block_diag)(gate_x_w), bx=vec(gate_x_b),
        lam=vec(lru_lambda), lnw=vec(lru_norm_w),
        gcw=taps(gdn_conv_w), gcb=jnp.zeros((L, 1, 3 * GDN_W), F32),
        alog=lanes(gdn_a_log), dtb=lanes(gdn_dt_bias), gnw=vec(gdn_norm_w),
        wout=wout_b, wup=wup_b, wdown=wdown_b)


def _position():
    x, y, c = lax.axis_index("x"), lax.axis_index("y"), lax.axis_index("c")
    return x, y, c


def _other_chips(x, y):
    return [(1 - x, y), (x, 1 - y), (1 - x, 1 - y)]


def _all_gather_rows(block, name):
    m, n = block.shape

    def body(x_ref, out_ref, send_sems, recv_sems, local_sem):
        x, y, c = _position()
        me, sibling = (x, y, c), (x, y, 1 - c)
        chips = _other_chips(x, y)

        def rows(px, py, pc):
            return out_ref.at[pl.ds((4 * px + 2 * py + pc) * m, m), :]

        def copy(k, blk, to, src=None):
            return pltpu.make_async_remote_copy(
                src_ref=rows(*blk) if src is None else src, dst_ref=rows(*blk),
                send_sem=send_sems.at[k], recv_sem=recv_sems.at[k], device_id=to, device_id_type=MESH)

        mine = pltpu.make_async_copy(x_ref, rows(*me), local_sem)
        mine.start()
        first = [copy(0, me, sibling, src=x_ref)]
        first += [copy(1 + j, me, (*chip, c), src=x_ref) for j, chip in enumerate(chips)]
        for cp in first:
            cp.start()
        passed = [copy(4 + j, (*chip, c), sibling) for j, chip in enumerate(chips)]
        for j, chip in enumerate(chips):
            copy(1 + j, (*chip, c), me).wait_recv()
            passed[j].start()
        copy(0, sibling, me).wait_recv()
        for j, chip in enumerate(chips):
            copy(4 + j, (*chip, 1 - c), me).wait_recv()
        for cp in first + passed:
            cp.wait_send()
        mine.wait()

    return pl.pallas_call(
        body, name=name,
        out_shape=_sds((N_DEV * m, n)),
        in_specs=[pl.BlockSpec(memory_space=pltpu.VMEM)],
        out_specs=pl.BlockSpec(memory_space=pltpu.VMEM),
        scratch_shapes=[pltpu.SemaphoreType.DMA((7,)), pltpu.SemaphoreType.DMA((7,)), pltpu.SemaphoreType.DMA],
        compiler_params=pltpu.CompilerParams(vmem_limit_bytes=VMEM_LIMIT),
    )(block)


def _hbm_specs(n):
    return [pl.BlockSpec(memory_space=pl.ANY)] * n


def _gather_chips(shards, name):
    n = len(shards)

    def body(*refs):
        start, finish = _gather_steps(refs[:n], refs[n:2 * n], *refs[2 * n:])
        start()
        finish()

    return pl.pallas_call(
        body, name=name,
        out_shape=_gather_out_shapes(shards), in_specs=_hbm_specs(n), out_specs=_hbm_specs(n),
        scratch_shapes=_gather_scratch(n),
    )(*shards)


def _gather_out_shapes(shards):
    return [_sds((N_CHIPS, 2, s.shape[0] // 2, s.shape[1]), s.dtype) for s in shards]


def _gather_scratch(n):
    return [pltpu.SemaphoreType.DMA((6 * n,)), pltpu.SemaphoreType.DMA((6 * n,))]


def _gather_steps(ins, outs, send_sems, recv_sems):
    n = len(ins)
    x, y, c = _position()
    chips = _other_chips(x, y)
    me = 2 * x + y

    def first(a, j, slot):
        h = ins[a].shape[0] // 2
        return pltpu.make_async_remote_copy(
            src_ref=ins[a].at[pl.ds(pl.multiple_of(c * h, SUBLANES), h)], dst_ref=outs[a].at[slot, c],
            send_sem=send_sems.at[3 * a + j], recv_sem=recv_sems.at[3 * a + j],
            device_id=(chips[j][0], chips[j][1], c), device_id_type=MESH)

    def second(a, j, half):
        slot = 2 * chips[j][0] + chips[j][1]
        return pltpu.make_async_remote_copy(
            src_ref=outs[a].at[slot, c], dst_ref=outs[a].at[slot, half],
            send_sem=send_sems.at[3 * (n + a) + j], recv_sem=recv_sems.at[3 * (n + a) + j],
            device_id=(x, y, 1 - c), device_id_type=MESH)

    def start():
        for a in range(n):
            for j in range(3):
                first(a, j, me).start()

    def finish():
        for a in range(n):
            for j, (px, py) in enumerate(chips):
                first(a, j, 2 * px + py).wait_recv()
                second(a, j, c).start()
        for a in range(n):
            for j in range(3):
                second(a, j, 1 - c).wait_recv()
        for a in range(n):
            for j in range(3):
                first(a, j, me).wait_send()
                second(a, j, c).wait_send()

    return start, finish


def _send_to_sibling(parts, name):
    n = len(parts)

    def body(*refs):
        ins, outs = refs[:n], refs[n:2 * n]
        send_sems, recv_sems = refs[2 * n:]
        x, y, c = _position()
        copies = [pltpu.make_async_remote_copy(
            src_ref=ins[a].at[1 - c], dst_ref=outs[a], send_sem=send_sems.at[a], recv_sem=recv_sems.at[a],
            device_id=(x, y, 1 - c), device_id_type=MESH) for a in range(n)]
        for cp in copies:
            cp.start()
        for cp in copies:
            cp.wait()

    return pl.pallas_call(
        body, name=name,
        out_shape=[_sds(p.shape[1:], p.dtype) for p in parts],
        in_specs=_hbm_specs(n), out_specs=_hbm_specs(n),
        scratch_shapes=[pltpu.SemaphoreType.DMA((n,)), pltpu.SemaphoreType.DMA((n,))],
    )(*parts)


def _scatter_chips(parts, name):
    n = len(parts)

    def body(*refs):
        start, finish = _scatter_steps(refs[:n], refs[n:2 * n], *refs[2 * n:])
        start()
        finish()

    return pl.pallas_call(
        body, name=name,
        out_shape=[_sds(p.shape, p.dtype) for p in parts], in_specs=_hbm_specs(n), out_specs=_hbm_specs(n),
        scratch_shapes=_scatter_scratch(n),
    )(*parts)


def _scatter_scratch(n):
    return [pltpu.SemaphoreType.DMA((3 * n,)), pltpu.SemaphoreType.DMA((3 * n,))]


def _scatter_steps(ins, outs, send_sems, recv_sems):
    n = len(ins)
    x, y, c = _position()
    chips = _other_chips(x, y)
    me = 2 * x + y

    def copy(a, j, src_slot, dst_slot):
        px, py = chips[j]
        return pltpu.make_async_remote_copy(
            src_ref=ins[a].at[src_slot], dst_ref=outs[a].at[dst_slot], send_sem=send_sems.at[3 * a + j],
            recv_sem=recv_sems.at[3 * a + j], device_id=(px, py, c), device_id_type=MESH)

    def start():
        for a in range(n):
            for j in range(3):
                copy(a, j, 2 * chips[j][0] + chips[j][1], me).start()

    def finish():
        for a in range(n):
            for j, (px, py) in enumerate(chips):
                copy(a, j, me, 2 * px + py).wait_recv()
        for a in range(n):
            for j in range(3):
                copy(a, j, 2 * chips[j][0] + chips[j][1], me).wait_send()

    return start, finish


def _swap_row_halves(arrays, name):
    n = len(arrays)

    def body(*refs):
        outs = refs[n:2 * n]
        send_sems, recv_sems = refs[2 * n:]
        x, y, c = _position()

        def copy(a, half):
            h = outs[a].shape[0] // 2
            rows = outs[a].at[pl.ds(pl.multiple_of(half * h, SUBLANES), h)]
            return pltpu.make_async_remote_copy(
                src_ref=rows, dst_ref=rows, send_sem=send_sems.at[a], recv_sem=recv_sems.at[a],
                device_id=(x, y, 1 - c), device_id_type=MESH)

        sends = [copy(a, c) for a in range(n)]
        for cp in sends:
            cp.start()
        for a in range(n):
            copy(a, 1 - c).wait_recv()
        for cp in sends:
            cp.wait_send()

    return pl.pallas_call(
        body, name=name,
        out_shape=[_sds(a.shape, a.dtype) for a in arrays],
        in_specs=_hbm_specs(n), out_specs=_hbm_specs(n),
        input_output_aliases={a: a for a in range(n)},
        scratch_shapes=[pltpu.SemaphoreType.DMA((n,)), pltpu.SemaphoreType.DMA((n,))],
    )(*arrays)


def _row_tile(rows):
    for t in (512, 256, 128, 64, 32, 16, 8):
        if rows % t == 0:
            return t
    return rows


def _sum_slots(buf, name):
    k, rows, cols = buf.shape
    tm = _row_tile(rows)

    def body(b_ref, o_ref):
        s = b_ref[0]
        for i in range(1, k):
            s = s + b_ref[i]
        o_ref[...] = s

    return pl.pallas_call(
        body, name=name, grid=(rows // tm,),
        in_specs=[pl.BlockSpec((k, tm, cols), lambda i: (0, i, 0))],
        out_specs=pl.BlockSpec((tm, cols), lambda i: (i, 0)),
        out_shape=_sds((rows, cols)),
        compiler_params=_params(("arbitrary",)),
    )(buf)


def _pair_add(part, from_sibling, core, name):
    _, k, h, cols = part.shape
    rows = k * h
    tm = _row_tile(rows)

    def body(core_ref, a_ref, b_ref, o_ref):
        o_ref[...] = (a_ref[0].astype(F32) + b_ref[...].astype(F32)).astype(BF16)

    out = pl.pallas_call(
        body, name=name,
        grid_spec=pltpu.PrefetchScalarGridSpec(
            num_scalar_prefetch=1, grid=(rows // tm,),
            in_specs=[pl.BlockSpec((1, tm, cols), lambda i, cr: (cr[0], i, 0)),
                      pl.BlockSpec((tm, cols), lambda i, cr: (i, 0))],
            out_specs=pl.BlockSpec((tm, cols), lambda i, cr: (i, 0))),
        out_shape=_sds((rows, cols), BF16),
        compiler_params=_params(("arbitrary",)),
    )(core, part.reshape(2, rows, cols), from_sibling.reshape(rows, cols))
    return out.reshape(k, h, cols)


def _chip_sum(arrived, own, place, name):
    _, h, cols = arrived.shape
    tm = min(256, h)
    nb = h // tm

    def body(place_ref, arr_ref, own_ref, g_ref):
        for chip in range(N_CHIPS):
            @pl.when(place_ref[1] == chip)
            def _():
                terms = [own_ref[0] if j == chip else arr_ref[j] for j in range(N_CHIPS)]
                g = terms[0].astype(F32)
                for t in terms[1:]:
                    g = g + t.astype(F32)
                g_ref[...] = g

    return pl.pallas_call(
        body, name=name,
        grid_spec=pltpu.PrefetchScalarGridSpec(
            num_scalar_prefetch=1, grid=(nb,),
            in_specs=[pl.BlockSpec((N_CHIPS, tm, cols), lambda i, pr: (0, i, 0)),
                      pl.BlockSpec((1, tm, cols), lambda i, pr: (pr[1], i, 0))],
            out_specs=pl.BlockSpec((tm, cols), lambda i, pr: (pr[0] * nb + i, 0))),
        out_shape=_sds((2 * h, cols)),
        compiler_params=_params(("arbitrary",)),
    )(place, arrived, own)


def _adam_layer(g, w, m, v, outs, layer, name):
    rows, cols = g.shape
    tm = _row_tile(rows)

    def body(g_ref, w_ref, m_ref, v_ref, *refs):
        og_ref, od_ref, om_ref, ov_ref = refs[4:]
        gr = g_ref[...]
        og_ref[0] = gr
        d, nm, nv = _adam_math(w_ref[0], gr, m_ref[0], v_ref[0])
        od_ref[0] = d
        om_ref[0] = nm
        ov_ref[0] = nv

    slab = pl.BlockSpec((1, tm, cols), lambda i: (layer, i, 0))
    return pl.pallas_call(
        body, name=name, grid=(rows // tm,),
        in_specs=[pl.BlockSpec((tm, cols), lambda i: (i, 0)), slab, slab, slab] + _hbm_specs(4),
        out_specs=[slab] * 4, out_shape=[_sds(o.shape) for o in outs],
        input_output_aliases={4 + i: i for i in range(4)},
        compiler_params=_params(("arbitrary",)),
    )(g, w, m, v, *outs)


def _adam_math(w, g, m, v):
    m = ADAM_B1 * m + (1.0 - ADAM_B1) * g
    v = ADAM_B2 * v + (1.0 - ADAM_B2) * jnp.square(g)
    m_hat = m / (1.0 - ADAM_B1 ** ADAM_STEP)
    v_hat = v / (1.0 - ADAM_B2 ** ADAM_STEP)
    delta = -ADAM_LR * (m_hat / (jnp.sqrt(v_hat) + ADAM_EPS) + ADAM_WD * w)
    return delta, m, v


def _adam(w, g, m, v, name):
    rows, cols = w.shape
    tm = _row_tile(rows)

    def body(w_ref, g_ref, m_ref, v_ref, d_ref, nm_ref, nv_ref):
        d, nm, nv = _adam_math(w_ref[...], g_ref[...], m_ref[...], v_ref[...])
        d_ref[...] = d
        nm_ref[...] = nm
        nv_ref[...] = nv

    spec = pl.BlockSpec((tm, cols), lambda i: (i, 0))
    return pl.pallas_call(
        body, name=name, grid=(rows // tm,), in_specs=[spec] * 4, out_specs=[spec] * 3,
        out_shape=[_sds((rows, cols))] * 3, compiler_params=_params(("arbitrary",)),
    )(w, g, m, v)


def _mod_fwd(c_all, w_mod, b_mod_cols):
    L, _, n = w_mod.shape

    def body(c_ref, w_ref, b_ref, o_ref):
        o_ref[0] = _hdot(_silu(c_ref[...]), w_ref[0]) + b_ref[0]

    return pl.pallas_call(
        body, name="mod_fwd", grid=(L,),
        in_specs=[_const((N_DEV, D_MODEL)), pl.BlockSpec((1, D_MODEL, n), lambda l: (l, 0, 0)),
                  pl.BlockSpec((1, 1, n), lambda l: (l, 0, 0))],
        out_specs=pl.BlockSpec((1, N_DEV, n), lambda l: (l, 0, 0)),
        out_shape=_sds((L, N_DEV, n)),
        compiler_params=_params(("arbitrary",)),
    )(c_all, w_mod, b_mod_cols)


def _mod_update(c_all, dmod, w, m, v):
    L, _, n = w.shape
    tn = 512

    def body(c_ref, d_ref, w_ref, m_ref, v_ref, g_ref, dl_ref, nm_ref, nv_ref):
        g = _hdot_tn(_silu(c_ref[...]), d_ref[0])
        g_ref[0] = g
        d, nm, nv = _adam_math(w_ref[0], g, m_ref[0], v_ref[0])
        dl_ref[0] = d
        nm_ref[0] = nm
        nv_ref[0] = nv

    big = pl.BlockSpec((1, D_MODEL, tn), lambda l, j: (l, 0, j))
    return pl.pallas_call(
        body, name="mod_update", grid=(L, n // tn),
        in_specs=[_const((N_DEV, D_MODEL)), pl.BlockSpec((1, N_DEV, tn), lambda l, j: (l, 0, j)), big, big, big],
        out_specs=[big] * 4, out_shape=[_sds(w.shape)] * 4,
        compiler_params=_params(("arbitrary", "arbitrary")),
    )(c_all, dmod, w, m, v)


def _pack_rows(parts, row_multiple):
    flat = jnp.concatenate([p.reshape(-1) for p in parts])
    unit = row_multiple * LANES
    flat = jnp.pad(flat, (0, (-flat.shape[0]) % unit))
    return flat.reshape(-1, LANES)


def _unpack(packed, shapes):
    flat = packed.reshape(-1)
    out, off = [], 0
    for s in shapes:
        n = 1
        for d in s:
            n *= d
        out.append(flat[off:off + n].reshape(s))
        off += n
    return out


def _lane_pad(a):
    return jnp.pad(a, ((0, 0), (0, LANES - a.shape[1])))


WEIGHT_NAMES = ("norm_mix_w", "norm_mlp_w", "w_mod", "b_mod", "w_in", "lru_conv_w", "lru_conv_b", "lru_gate_a_w",
                "lru_gate_a_b", "lru_gate_x_w", "lru_gate_x_b", "lru_lambda", "lru_norm_w", "gdn_conv_w", "gdn_a_log",
                "gdn_dt_bias", "gdn_norm_w", "w_out", "w_up", "w_down", "final_norm_w")


def kernel(x, c, norm_mix_w, norm_mlp_w, w_mod, b_mod, w_in, lru_conv_w, lru_conv_b, lru_gate_a_w, lru_gate_a_b, lru_gate_x_w, lru_gate_x_b, lru_lambda, lru_norm_w, gdn_conv_w, gdn_a_log, gdn_dt_bias, gdn_norm_w, w_out, w_up, w_down, final_norm_w, loss_target, m_norm_mix_w, m_norm_mlp_w, m_w_mod, m_b_mod, m_w_in, m_lru_conv_w, m_lru_conv_b, m_lru_gate_a_w, m_lru_gate_a_b, m_lru_gate_x_w, m_lru_gate_x_b, m_lru_lambda, m_lru_norm_w, m_gdn_conv_w, m_gdn_a_log, m_gdn_dt_bias, m_gdn_norm_w, m_w_out, m_w_up, m_w_down, m_final_norm_w, v_norm_mix_w, v_norm_mlp_w, v_w_mod, v_b_mod, v_w_in, v_lru_conv_w, v_lru_conv_b, v_lru_gate_a_w, v_lru_gate_a_b, v_lru_gate_x_w, v_lru_gate_x_b, v_lru_lambda, v_lru_norm_w, v_gdn_conv_w, v_gdn_a_log, v_gdn_dt_bias, v_gdn_norm_w, v_w_out, v_w_up, v_w_down, v_final_norm_w):
    W = dict(zip(WEIGHT_NAMES, (norm_mix_w, norm_mlp_w, w_mod, b_mod, w_in, lru_conv_w, lru_conv_b, lru_gate_a_w,
                                lru_gate_a_b, lru_gate_x_w, lru_gate_x_b, lru_lambda, lru_norm_w, gdn_conv_w, gdn_a_log,
                                gdn_dt_bias, gdn_norm_w, w_out, w_up, w_down, final_norm_w)))
    M = dict(zip(WEIGHT_NAMES, (m_norm_mix_w, m_norm_mlp_w, m_w_mod, m_b_mod, m_w_in, m_lru_conv_w, m_lru_conv_b,
                                m_lru_gate_a_w, m_lru_gate_a_b, m_lru_gate_x_w, m_lru_gate_x_b, m_lru_lambda,
                                m_lru_norm_w, m_gdn_conv_w, m_gdn_a_log, m_gdn_dt_bias, m_gdn_norm_w, m_w_out, m_w_up,
                                m_w_down, m_final_norm_w)))
    V = dict(zip(WEIGHT_NAMES, (v_norm_mix_w, v_norm_mlp_w, v_w_mod, v_b_mod, v_w_in, v_lru_conv_w, v_lru_conv_b,
                                v_lru_gate_a_w, v_lru_gate_a_b, v_lru_gate_x_w, v_lru_gate_x_b, v_lru_lambda,
                                v_lru_norm_w, v_gdn_conv_w, v_gdn_a_log, v_gdn_dt_bias, v_gdn_norm_w, v_w_out, v_w_up,
                                v_w_down, v_final_norm_w)))
    L = DEPTH
    xi, yi, ci = _position()
    chip = 2 * xi + yi
    dev = 2 * chip + ci
    lcs = LRU_W // N_CHIPS
    gcs = 3 * GDN_W // N_CHIPS
    mcs = N_MOD * D_MODEL // N_CHIPS

    g_in = _all_gather_rows(_pack_rows([c, lru_conv_w, gdn_conv_w], SUBLANES), "gather_small_inputs").reshape(N_DEV, -1)
    c_all = g_in[:, :D_MODEL]
    per_chip = g_in[0::2]
    o1 = D_MODEL + L * 4 * lcs
    lcw_full = per_chip[:, D_MODEL:o1].reshape(N_CHIPS, L, 4, lcs).transpose(1, 2, 0, 3).reshape(L, 4, LRU_W)
    gcw_full = per_chip[:, o1:o1 + L * 4 * gcs].reshape(N_CHIPS, L, 4, gcs).transpose(1, 2, 0, 3).reshape(L, 4, 3 * GDN_W)

    b_cols = lax.dynamic_slice(b_mod, (0, chip * mcs), (L, mcs)).reshape(L, 1, mcs)
    modp = _mod_fwd(c_all, w_mod, b_cols)
    g_mod = _all_gather_rows(modp.reshape(L * N_DEV, mcs), "gather_mod").reshape(N_DEV, L, N_DEV, mcs)
    mod = lax.dynamic_index_in_dim(g_mod[0::2], dev, axis=2, keepdims=False).transpose(1, 0, 2).reshape(L, N_MOD * D_MODEL)

    stacked = _prep_layers(norm_mix_w, norm_mlp_w, mod, None, lcw_full, lru_conv_b, lru_gate_a_w, lru_gate_a_b,
                           lru_gate_x_w, lru_gate_x_b, lru_lambda, lru_norm_w, gcw_full, gdn_a_log, gdn_dt_bias,
                           gdn_norm_w, None, None, None)
    shards = [[w_in[l].astype(BF16), w_out[l].astype(BF16), w_up[l].astype(BF16), w_down[l].astype(BF16)]
              for l in range(L)]

    def with_own(gathered, own):
        return [lax.dynamic_update_slice(got, o.reshape((1,) + got.shape[1:]), (chip, 0, 0, 0)).reshape(
            (N_CHIPS,) + o.shape) for got, o in zip(gathered, own)]

    def mixer_weights(l, gathered):
        win_g, wout_g = with_own(gathered, shards[l][:2])
        return dict(win=jnp.pad(win_g.transpose(1, 0, 2).reshape(D_MODEL, IN_COLS), ((0, 0), (0, IN_PAD - IN_COLS))),
                    wout=wout_g.reshape(D_MODEL, D_MODEL))

    def mlp_weights(l, gathered):
        wup_g, wdown_g = with_own(gathered, shards[l][2:])
        return dict(wup=wup_g.transpose(1, 0, 2).reshape(D_MODEL, D_FF), wdown=wdown_g.reshape(D_FF, D_MODEL))

    mixer = mixer_weights(0, _gather_chips(shards[0][:2], "gather_weights"))
    layers = []
    xs = x[0]
    res = []
    for l in range(L):
        p = {k: v[l] for k, v in stacked.items() if v is not None}
        p.update(mixer)
        xs, r, p, gathered = _layer_fwd(xs, p, shards[l][2:], functools.partial(mlp_weights, l),
                                        shards[l + 1][:2] if l + 1 < L else ())
        res.append(r)
        layers.append(p)
        if l + 1 < L:
            mixer = mixer_weights(l + 1, gathered)
    dx, loss_blk, dfnw = _loss_head(xs, loss_target[0], final_norm_w.reshape(1, D_MODEL))
    loss_local = loss_blk[0, 0]

    big_names = ["w_in", "w_out", "w_up", "w_down"]
    core = jnp.reshape(ci, (1,)).astype(jnp.int32)
    place = jnp.stack([ci, chip]).astype(jnp.int32)
    layer_grads = [None] * L

    big = {nm: [lax.empty(W[nm].shape, F32) for _ in range(4)] for nm in big_names}

    def pair_sums(names, parts):
        from_sibling = _send_to_sibling(parts, "pair_send")
        return [_pair_add(p, r, core, "pair_add_" + nm) for nm, p, r in zip(names, parts, from_sibling)]

    def apply_update(items, arrived, pair):
        halves = [_chip_sum(a, own, place, "chip_sum_" + nm) for (_, nm), a, own in zip(items, arrived, pair)]
        full = _swap_row_halves(halves, "pair_swap")
        for (l, nm), gr in zip(items, full):
            big[nm] = _adam_layer(gr, W[nm], M[nm], V[nm], big[nm], l, "adam_" + nm)

    win_pair = []
    for l in reversed(range(L)):
        early_pair = []

        def early(parts, early_pair=early_pair):
            early_pair.extend(pair_sums(big_names[1:], parts))
            return early_pair

        dx, gl, arrived = _layer_bwd(dx, layers[l], res[l], sharded=True, scatter=win_pair, early=early)
        items = [(l + 1, big_names[0])] * len(win_pair) + [(l, nm) for nm in big_names[1:]]
        apply_update(items, arrived, win_pair + early_pair)
        layer_grads[l] = gl
        gwin = gl["win"][:, :IN_COLS].reshape(2, D_MODEL // 2, N_CHIPS, IN_COLS // N_CHIPS).transpose(0, 2, 1, 3)
        win_pair = pair_sums(big_names[:1], [gwin])
    apply_update([(0, big_names[0])], _scatter_chips(win_pair, "chip_scatter"), win_pair)
    small_keys = [k for k in layer_grads[0] if k not in ("win", "wout", "wup", "wdown")]
    g = {k: jnp.stack([gl[k] for gl in layer_grads]) for k in small_keys}
    loss = lax.psum(loss_local, ("x", "y", "c"))

    dmod = jnp.concatenate([g["sh1"], g["sc1"], g["g1"], g["sh2"], g["sc2"], g["g2"]], axis=-1)
    small = [dmod, g["nmw"], g["nmlp"], g["lcw"][:, :4], g["lcb"], jax.vmap(_diag_blocks)(g["wa"]), g["ba"],
             jax.vmap(_diag_blocks)(g["wx"]), g["bx"], g["lam"], g["lnw"], g["gcw"][:, :4], g["alog"], g["dtb"],
             g["gnw"], dfnw]
    small_shapes = [(L, N_MOD * D_MODEL), (L, D_MODEL), (L, D_MODEL), (L, 4, LRU_W), (L, LRU_W),
                    (L, LRU_BLOCKS, LRU_BLOCK, LRU_BLOCK), (L, LRU_W), (L, LRU_BLOCKS, LRU_BLOCK, LRU_BLOCK),
                    (L, LRU_W), (L, LRU_W), (L, LRU_W), (L, 4, 3 * GDN_W), (L, LANES), (L, LANES), (L, LANES),
                    (D_MODEL,)]
    small_names = ["b_mod", "norm_mix_w", "norm_mlp_w", None, "lru_conv_b", "lru_gate_a_w", "lru_gate_a_b",
                   "lru_gate_x_w", "lru_gate_x_b", "lru_lambda", "lru_norm_w", None, "gdn_a_log", "gdn_dt_bias",
                   "gdn_norm_w", "final_norm_w"]
    pack_g = _pack_rows(small, 512)
    rows = pack_g.shape[0]
    all_g = _all_gather_rows(pack_g, "gather_small_grads").reshape(N_DEV, rows, LANES)
    tot = _sum_slots(all_g, "sum_small_grads")
    tot_parts = _unpack(tot, small_shapes)

    def pack_state(S_):
        parts = []
        for nm, shp in zip(small_names, small_shapes):
            if nm is None:
                parts.append(jnp.zeros(shp, F32))
            elif nm in ("gdn_a_log", "gdn_dt_bias"):
                parts.append(_lane_pad(S_[nm]))
            else:
                parts.append(S_[nm])
        return _pack_rows(parts, 512)

    upd = _adam(pack_state(W), tot, pack_state(M), pack_state(V), "adam_small")
    upd_parts = [_unpack(u, small_shapes) for u in upd]

    grads, deltas, new_m, new_v = {}, {}, {}, {}
    for k, nm in enumerate(small_names):
        if nm is None:
            continue
        cut = (lambda a: a[:, :HEADS]) if nm in ("gdn_a_log", "gdn_dt_bias") else (lambda a: a)
        grads[nm] = cut(tot_parts[k])
        deltas[nm], new_m[nm], new_v[nm] = (cut(u[k]) for u in upd_parts)

    g_lcw = lax.dynamic_slice(tot_parts[3], (0, 0, chip * lcs), (L, 4, lcs))
    g_gcw = lax.dynamic_slice(tot_parts[11], (0, 0, chip * gcs), (L, 4, gcs))
    conv_shapes = [(L, 4, lcs), (L, 4, gcs)]
    conv_pack = lambda a, b: _pack_rows([a, b], SUBLANES)
    cu = _adam(conv_pack(lru_conv_w, gdn_conv_w), conv_pack(g_lcw, g_gcw), conv_pack(m_lru_conv_w, m_gdn_conv_w),
               conv_pack(v_lru_conv_w, v_gdn_conv_w), "adam_conv")
    cu_parts = [_unpack(u, conv_shapes) for u in cu]
    for k, nm in enumerate(("lru_conv_w", "gdn_conv_w")):
        grads[nm] = (g_lcw, g_gcw)[k]
        deltas[nm], new_m[nm], new_v[nm] = (u[k] for u in cu_parts)

    dmod_all = all_g[:, :L * N_MOD * D_MODEL // LANES].reshape(N_DEV, L, N_MOD * D_MODEL)
    dmod_cols = lax.dynamic_slice(dmod_all, (0, 0, chip * mcs), (N_DEV, L, mcs)).transpose(1, 0, 2)
    grads["w_mod"], deltas["w_mod"], new_m["w_mod"], new_v["w_mod"] = _mod_update(c_all, dmod_cols, w_mod, m_w_mod, v_w_mod)

    for nm in big_names:
        grads[nm], deltas[nm], new_m[nm], new_v[nm] = big[nm]

    out = [loss, dx[None]]
    for group in (grads, deltas, new_m, new_v):
        out += [group[nm].reshape(W[nm].shape) for nm in WEIGHT_NAMES]
    return tuple(out)
```

```python
import functools

import jax
import jax.numpy as jnp
from jax import lax
from jax.experimental import pallas as pl
from jax.experimental.pallas import tpu as pltpu

F32 = jnp.float32
BF16 = jnp.bfloat16
MESH = pl.DeviceIdType.MESH

D_MODEL = 1024
DEPTH = 4
LRU_W = 512
LRU_BLOCKS = 8
LRU_BLOCK = 64
LRU_C = 8.0
HEADS = 4
HEAD_DIM = 128
GDN_W = 512
CHUNK = 128
D_FF = 4096
N_MOD = 6
IN_COLS = 3080
IN_PAD = 3200
NORM_EPS = 1e-6
LANES = 128
SUBLANES = 8
N_DEV = 8
N_CHIPS = 4

ADAM_LR = 0.001
ADAM_B1 = 0.9
ADAM_B2 = 0.999
ADAM_EPS = 1e-08
ADAM_WD = 0.01
ADAM_STEP = 10

VMEM_LIMIT = 56 * 1024 * 1024
HI = lax.Precision.HIGHEST


def _sds(shape, dtype=F32):
    return jax.ShapeDtypeStruct(tuple(shape), dtype)


def _params(sem=None, vmem=VMEM_LIMIT):
    return pltpu.CompilerParams(dimension_semantics=sem, vmem_limit_bytes=vmem)


def _const(shape):
    return pl.BlockSpec(tuple(shape), lambda *_: (0,) * len(shape))


def _row(tm, c, col=0):
    return pl.BlockSpec((tm, c), lambda i: (i, col))


def _dot(a, b):
    return jnp.dot(a, b, preferred_element_type=F32)


def _dot_nt(a, b):
    return lax.dot_general(a, b, (((1,), (1,)), ((), ())), preferred_element_type=F32)


def _dot_tn(a, b):
    return lax.dot_general(a, b, (((0,), (0,)), ((), ())), preferred_element_type=F32)


def _hdot(a, b):
    return jnp.dot(a, b, preferred_element_type=F32, precision=HI)


def _hdot_nt(a, b):
    return lax.dot_general(a, b, (((1,), (1,)), ((), ())), preferred_element_type=F32, precision=HI)


def _hdot_tn(a, b):
    return lax.dot_general(a, b, (((0,), (0,)), ((), ())), preferred_element_type=F32, precision=HI)


_DIMS = {"nn": (((1,), (0,)), ((), ())), "nt": (((1,), (1,)), ((), ())), "tn": (((0,), (0,)), ((), ()))}


def _mm_raw(a, b, dims, passes):
    dn = _DIMS[dims]

    def dot(p, q):
        return lax.dot_general(p, q, dn, preferred_element_type=F32)

    a_hi = a.astype(BF16)
    b_hi = b.astype(BF16)
    if passes == 1:
        return dot(a_hi, b_hi)
    a_lo = (a - a_hi.astype(F32)).astype(BF16)
    b_lo = (b - b_hi.astype(F32)).astype(BF16)
    return dot(a_hi, b_hi) + (dot(a_hi, b_lo) + dot(a_lo, b_hi))


@functools.partial(jax.custom_vjp, nondiff_argnums=(2, 3))
def _mm(a, b, dims, passes):
    return _mm_raw(a, b, dims, passes)


def _mm_fwd(a, b, dims, passes):
    return _mm_raw(a, b, dims, passes), (a, b)


def _mm_bwd(dims, passes, res, ct):
    a, b = res
    if dims == "nn":
        return _mm_raw(ct, b, "nt", passes), _mm_raw(a, ct, "tn", passes)
    if dims == "nt":
        return _mm_raw(ct, b, "nn", passes), _mm_raw(ct, a, "tn", passes)
    return _mm_raw(b, ct, "nt", passes), _mm_raw(a, ct, "nn", passes)


_mm.defvjp(_mm_fwd, _mm_bwd)


def _acc(ref, val, first):
    @pl.when(first)
    def _():
        ref[...] = val

    @pl.when(jnp.logical_not(first))
    def _():
        ref[...] += val


def _colsum(v):
    return jnp.sum(v, axis=0, keepdims=True)


def _rms_parts(x):
    r = lax.rsqrt(jnp.mean(x * x, axis=-1, keepdims=True) + NORM_EPS)
    return x * r, r


def _rms_bwd(dy, xh, r, w):
    dxh = dy * w
    dw = _colsum(dy * xh)
    dx = r * (dxh - xh * jnp.mean(dxh * xh, axis=-1, keepdims=True))
    return dx, dw


def _norm_mod(x, w, sc, sh):
    xh, _ = _rms_parts(x)
    return (xh * w) * (1.0 + sc) + sh


def _norm_mod_bwd(dy, x, w, sc):
    xh, r = _rms_parts(x)
    n = xh * w
    dsh = _colsum(dy)
    dsc = _colsum(dy * n)
    dx, dw = _rms_bwd(dy * (1.0 + sc), xh, r, w)
    return dx, dw, dsc, dsh


def _softplus(x):
    return jnp.maximum(x, 0.0) + jnp.log1p(jnp.exp(-jnp.abs(x)))


def _silu(x):
    return x * jax.nn.sigmoid(x)


def _silu_grad(x):
    s = jax.nn.sigmoid(x)
    return s * (1.0 + x * (1.0 - s))


def _roll_dn(x, d):
    return x if d == 0 else pltpu.roll(x, d, 0)


def _roll_up(x, d):
    return x if d == 0 else pltpu.roll(x, x.shape[0] - d, 0)


def _proj_fwd(x, nw, sc, sh, win):
    S = x.shape[0]
    tm = min(512, S)

    def body(x_ref, nw_ref, sc_ref, sh_ref, w_ref, proj_ref, hb_ref):
        hb = _norm_mod(x_ref[...], nw_ref[...], sc_ref[...], sh_ref[...]).astype(BF16)
        hb_ref[...] = hb
        proj_ref[...] = _dot(hb, w_ref[...])

    vec = _const((1, D_MODEL))
    return pl.pallas_call(
        body, name="proj_fwd", grid=(S // tm,),
        in_specs=[_row(tm, D_MODEL), vec, vec, vec, _const((D_MODEL, IN_PAD))],
        out_specs=[_row(tm, IN_PAD), _row(tm, D_MODEL)],
        out_shape=[_sds((S, IN_PAD)), _sds((S, D_MODEL), BF16)],
        compiler_params=_params(("arbitrary",)),
    )(x, nw, sc, sh, win)


def _proj_bwd(dx1, x, dlx, dly, dqkv, dz, dba, nw, sc, win):
    S = x.shape[0]
    tm = min(512, S)

    def body(dx1_ref, x_ref, dlx_ref, dly_ref, dqkv_ref, dz_ref, dba_ref, nw_ref, sc_ref, w_ref,
             dx_ref, dpb_ref, dnw_ref, dsc_ref, dsh_ref):
        i = pl.program_id(0)
        dpb = jnp.concatenate([dlx_ref[...], dly_ref[...], dqkv_ref[...], dz_ref[...], dba_ref[...]],
                              axis=-1).astype(BF16)
        dpb_ref[...] = dpb
        dh = _dot_nt(dpb, w_ref[...])
        dx, dnw, dsc, dsh = _norm_mod_bwd(dh, x_ref[...], nw_ref[...], sc_ref[...])
        dx_ref[...] = dx1_ref[...] + dx
        _acc(dnw_ref, dnw, i == 0)
        _acc(dsc_ref, dsc, i == 0)
        _acc(dsh_ref, dsh, i == 0)

    vec = _const((1, D_MODEL))
    return pl.pallas_call(
        body, name="proj_bwd", grid=(S // tm,),
        in_specs=[_row(tm, D_MODEL), _row(tm, D_MODEL), _row(tm, LRU_W), _row(tm, LRU_W), _row(tm, 3 * GDN_W),
                  _row(tm, GDN_W), _row(tm, LANES), vec, vec,
                  _const((D_MODEL, IN_PAD))],
        out_specs=[_row(tm, D_MODEL), _row(tm, IN_PAD), vec, vec, vec],
        out_shape=[_sds((S, D_MODEL)), _sds((S, IN_PAD), BF16), _sds((1, D_MODEL)), _sds((1, D_MODEL)),
                   _sds((1, D_MODEL))],
        compiler_params=_params(("arbitrary",)),
    )(dx1, x, dlx, dly, dqkv, dz, dba, nw, sc, win)


def _conv_taps(xx, w, tm):
    y = _roll_dn(xx, 3)[SUBLANES:] * w[0:1]
    y = y + _roll_dn(xx, 2)[SUBLANES:] * w[1:2]
    y = y + _roll_dn(xx, 1)[SUBLANES:] * w[2:3]
    y = y + xx[SUBLANES:] * w[3:4]
    return y


def _conv_fwd(src, col0, C, w8, b, act, name):
    S = src.shape[0]
    tm = min(512, S)
    tc = 512
    hb = tm // SUBLANES
    cb0 = col0 // tc

    def body(x_ref, p_ref, w_ref, b_ref, y_ref):
        i = pl.program_id(0)
        prev = jnp.where(i > 0, p_ref[...], 0.0)
        xx = jnp.concatenate([prev, x_ref[...]], axis=0)
        y = _conv_taps(xx, w_ref[...], tm) + b_ref[...]
        y_ref[...] = _silu(y) if act else y

    return pl.pallas_call(
        body, name=name, grid=(S // tm, C // tc),
        in_specs=[pl.BlockSpec((tm, tc), lambda i, j: (i, cb0 + j)),
                  pl.BlockSpec((SUBLANES, tc), lambda i, j: (jnp.maximum(i * hb - 1, 0), cb0 + j)),
                  pl.BlockSpec((SUBLANES, tc), lambda i, j: (0, j)),
                  pl.BlockSpec((1, tc), lambda i, j: (0, j))],
        out_specs=pl.BlockSpec((tm, tc), lambda i, j: (i, j)),
        out_shape=_sds((S, C)),
        compiler_params=_params(("arbitrary", "arbitrary")),
    )(src, src, w8, b)


def _conv_bwd(src, col0, C, w8, b, dyact, act, name):
    S = src.shape[0]
    tm = min(512, S)
    tc = 512
    hb = tm // SUBLANES
    nt = S // tm
    cb0 = col0 // tc
    last_hb = S // SUBLANES - 1

    def body(x_ref, p_ref, n_ref, dy_ref, dyn_ref, w_ref, b_ref, dx_ref, dw_ref, db_ref):
        i = pl.program_id(1)
        w = w_ref[...]
        prev = jnp.where(i > 0, p_ref[...], 0.0)
        xx = jnp.concatenate([prev, x_ref[...], n_ref[...]], axis=0)
        dy = jnp.concatenate([dy_ref[...], jnp.where(i < nt - 1, dyn_ref[...], 0.0)], axis=0)
        if act:
            ypre = _conv_taps(xx, w, tm + SUBLANES) + b_ref[...]
            dy = dy * _silu_grad(ypre)
        dx = dy[:tm] * w[3:4]
        for d in (1, 2, 3):
            dx = dx + _roll_up(dy, d)[:tm] * w[3 - d:4 - d]
        dx_ref[...] = dx
        xt = xx[:tm + SUBLANES]
        dyt = dy[:tm]
        rows = [_colsum(dyt * _roll_dn(xt, 3 - k)[SUBLANES:]) for k in range(4)]
        dw = jnp.concatenate(rows + [jnp.zeros((SUBLANES - 4, tc), F32)], axis=0)
        _acc(dw_ref, dw, i == 0)
        _acc(db_ref, _colsum(dyt), i == 0)

    return pl.pallas_call(
        body, name=name, grid=(C // tc, nt),
        in_specs=[pl.BlockSpec((tm, tc), lambda j, i: (i, cb0 + j)),
                  pl.BlockSpec((SUBLANES, tc), lambda j, i: (jnp.maximum(i * hb - 1, 0), cb0 + j)),
                  pl.BlockSpec((SUBLANES, tc), lambda j, i: (jnp.minimum((i + 1) * hb, last_hb), cb0 + j)),
                  pl.BlockSpec((tm, tc), lambda j, i: (i, j)),
                  pl.BlockSpec((SUBLANES, tc), lambda j, i: (jnp.minimum((i + 1) * hb, last_hb), j)),
                  pl.BlockSpec((SUBLANES, tc), lambda j, i: (0, j)),
                  pl.BlockSpec((1, tc), lambda j, i: (0, j))],
        out_specs=[pl.BlockSpec((tm, tc), lambda j, i: (i, j)),
                   pl.BlockSpec((SUBLANES, tc), lambda j, i: (0, j)),
                   pl.BlockSpec((1, tc), lambda j, i: (0, j))],
        out_shape=[_sds((S, C)), _sds((SUBLANES, C)), _sds((1, C))],
        compiler_params=_params(("arbitrary", "arbitrary")),
    )(src, src, src, dyact, dyact, w8, b)


def _lru_ab(pre_a, pre_x, xr, lam):
    r = jax.nn.sigmoid(pre_a)
    g = jax.nn.sigmoid(pre_x)
    log_sig = -_softplus(-lam)
    log_a = LRU_C * r * log_sig
    a = jnp.exp(log_a)
    t = jnp.tanh(log_a)
    mult = jnp.sqrt(jnp.maximum(-2.0 * t / (1.0 - t), 1e-12))
    return a, mult * (g * xr)


def _lru_tail(h, ly, lnw):
    xh, _ = _rms_parts(h * jax.nn.gelu(ly))
    return xh * lnw


def _scan_down(a, b):
    n = a.shape[0]
    blk = min(SCAN_BLOCK, n)
    in_block = lax.broadcasted_iota(jnp.int32, a.shape, 0) & (blk - 1)
    d = 1
    while d < blk:
        keep = in_block >= d
        a_s = jnp.where(keep, _roll_dn(a, d), 1.0)
        b_s = jnp.where(keep, _roll_dn(b, d), 0.0)
        b = a * b_s + b
        a = a * a_s
        d *= 2
    blocks_a, blocks_b = [a[:blk]], [b[:blk]]
    for j in range(1, n // blk):
        aj, bj = a[j * blk:(j + 1) * blk], b[j * blk:(j + 1) * blk]
        blocks_b.append(bj + aj * blocks_b[-1][blk - 1:blk])
        blocks_a.append(aj * blocks_a[-1][blk - 1:blk])
    return jnp.concatenate(blocks_a, axis=0), jnp.concatenate(blocks_b, axis=0)


def _scan_up(a, b):
    n = a.shape[0]
    blk = min(SCAN_BLOCK, n)
    in_block = lax.broadcasted_iota(jnp.int32, a.shape, 0) & (blk - 1)
    d = 1
    while d < blk:
        keep = in_block < blk - d
        a_s = jnp.where(keep, _roll_up(a, d), 1.0)
        b_s = jnp.where(keep, _roll_up(b, d), 0.0)
        b = a * b_s + b
        a = a * a_s
        d *= 2
    nb = n // blk
    blocks_a, blocks_b = [a[(nb - 1) * blk:]], [b[(nb - 1) * blk:]]
    for j in range(nb - 2, -1, -1):
        aj, bj = a[j * blk:(j + 1) * blk], b[j * blk:(j + 1) * blk]
        blocks_b.insert(0, bj + aj * blocks_b[0][0:1])
        blocks_a.insert(0, aj * blocks_a[0][0:1])
    return jnp.concatenate(blocks_a, axis=0), jnp.concatenate(blocks_b, axis=0)


LRU_TM = 256
SCAN_BLOCK = 32


def _lru_fwd(xr, proj, wa, ba, wx, bx, lam, lnw):
    S = xr.shape[0]
    tm = min(LRU_TM, S)

    def body(xr_ref, ly_ref, wa_ref, ba_ref, wx_ref, bx_ref, lam_ref, lnw_ref, out_ref, h_ref, carry):
        i = pl.program_id(0)

        @pl.when(i == 0)
        def _():
            carry[...] = jnp.zeros_like(carry)

        x = xr_ref[...]
        xb = x.astype(BF16)
        pre_a = _dot(xb, wa_ref[...]) + ba_ref[...]
        pre_x = _dot(xb, wx_ref[...]) + bx_ref[...]
        a, b = _lru_ab(pre_a, pre_x, x, lam_ref[...])
        ca, hl = _scan_down(a, b)
        h = hl + ca * carry[0:1, :]
        carry[0:1, :] = h[tm - 1:tm, :]
        h_ref[...] = h
        out_ref[...] = _lru_tail(h, ly_ref[...], lnw_ref[...])

    vec = _const((1, LRU_W))
    mat = _const((LRU_W, LRU_W))
    return pl.pallas_call(
        body, name="lru_fwd", grid=(S // tm,),
        in_specs=[_row(tm, LRU_W), _row(tm, LRU_W, 1), mat, vec, mat, vec, vec, vec],
        out_specs=[_row(tm, LRU_W), _row(tm, LRU_W)],
        out_shape=[_sds((S, LRU_W)), _sds((S, LRU_W))],
        scratch_shapes=[pltpu.VMEM((SUBLANES, LRU_W), F32)],
        compiler_params=_params(("arbitrary",)),
    )(xr, proj, wa, ba, wx, bx, lam, lnw)


def _lru_bwd(dout, xr, proj, h, wa, ba, wx, bx, lam, lnw):
    S = xr.shape[0]
    tm = min(LRU_TM, S)
    nt = S // tm
    hb = tm // SUBLANES

    def rev(col=0):
        return pl.BlockSpec((tm, LRU_W), lambda i: (nt - 1 - i, col))

    def body(dout_ref, xr_ref, ly_ref, h_ref, hp_ref, wa_ref, ba_ref, wx_ref, bx_ref, lam_ref, lnw_ref,
             dxr_ref, dly_ref, dwa_ref, dba_ref, dwx_ref, dbx_ref, dlam_ref, dlnw_ref, carry):
        i = pl.program_id(0)
        first = i == 0

        @pl.when(first)
        def _():
            carry[...] = jnp.zeros_like(carry)

        x = xr_ref[...]
        xb = x.astype(BF16)
        pre_a = _dot(xb, wa_ref[...]) + ba_ref[...]
        pre_x = _dot(xb, wx_ref[...]) + bx_ref[...]
        (a, b), ab_vjp = jax.vjp(_lru_ab, pre_a, pre_x, x, lam_ref[...])
        h_t = h_ref[...]
        _, tail_vjp = jax.vjp(_lru_tail, h_t, ly_ref[...], lnw_ref[...])
        dh, dly, dlnw = tail_vjp(dout_ref[...])
        dly_ref[...] = dly
        row = lax.broadcasted_iota(jnp.int32, a.shape, 0)
        a_next = jnp.where(row == tm - 1, carry[0:1, :], _roll_up(a, 1))
        ca, gl = _scan_up(a_next, dh)
        g = gl + ca * carry[1:2, :]
        carry[0:1, :] = a[0:1, :]
        carry[1:2, :] = g[0:1, :]
        h_before = jnp.where(i == nt - 1, 0.0, hp_ref[SUBLANES - 1:SUBLANES, :])
        h_prev = jnp.where(row == 0, h_before, _roll_dn(h_t, 1))
        dpa, dpx, dx, dlam = ab_vjp((g * h_prev, g))
        dpab = dpa.astype(BF16)
        dpxb = dpx.astype(BF16)
        dxr_ref[...] = dx + _dot_nt(dpab, wa_ref[...]) + _dot_nt(dpxb, wx_ref[...])
        _acc(dwa_ref, _dot_tn(xb, dpab), first)
        _acc(dwx_ref, _dot_tn(xb, dpxb), first)
        _acc(dba_ref, _colsum(dpa), first)
        _acc(dbx_ref, _colsum(dpx), first)
        _acc(dlam_ref, dlam, first)
        _acc(dlnw_ref, dlnw, first)

    vec = _const((1, LRU_W))
    mat = _const((LRU_W, LRU_W))
    return pl.pallas_call(
        body, name="lru_bwd", grid=(nt,),
        in_specs=[rev(), rev(), rev(1), rev(),
                  pl.BlockSpec((SUBLANES, LRU_W), lambda i: (jnp.maximum((nt - 1 - i) * hb - 1, 0), 0)),
                  mat, vec, mat, vec, vec, vec],
        out_specs=[rev(), rev(), mat, vec, mat, vec, vec, vec],
        out_shape=[_sds((S, LRU_W)), _sds((S, LRU_W)), _sds((LRU_W, LRU_W)), _sds((1, LRU_W)),
                   _sds((LRU_W, LRU_W)), _sds((1, LRU_W)), _sds((1, LRU_W)), _sds((1, LRU_W))],
        scratch_shapes=[pltpu.VMEM((SUBLANES, LRU_W), F32)],
        compiler_params=_params(("arbitrary",)),
    )(dout, xr, proj, h, h, wa, ba, wx, bx, lam, lnw)


def _lane_pick(row_or_tile, lane):
    idx = lax.broadcasted_iota(jnp.int32, row_or_tile.shape, 1)
    return jnp.sum(jnp.where(idx == lane, row_or_tile, 0.0), axis=-1, keepdims=True)


def _unit_lower_inverses(los):
    n = los[0].shape[0]
    ri = lax.broadcasted_iota(jnp.int32, (n, n), 0)
    ci = lax.broadcasted_iota(jnp.int32, (n, n), 1)
    eye = (ri == ci).astype(F32)

    def lower_left_of(s):
        same_block = (ri & ~(2 * s - 1)) == (ci & ~(2 * s - 1))
        return same_block & ((ri & s) != 0) & ((ci & s) == 0)

    invs = [eye - jnp.where(lower_left_of(1), lo, 0.0) for lo in los]
    s = 2
    while s < n:
        lower_left = lower_left_of(s)
        left = [_mm_raw(inv, jnp.where(lower_left, lo, 0.0), "nn", 3) for inv, lo in zip(invs, los)]
        invs = [inv - _mm_raw(t, inv, "nn", 3) for inv, t in zip(invs, left)]
        s *= 2
    return invs


@jax.custom_vjp
def _unit_lower_inverses_diff(los):
    return _unit_lower_inverses(los)


def _unit_lower_inverses_fwd(los):
    invs = _unit_lower_inverses(los)
    return invs, invs


def _unit_lower_inverses_bwd(invs, cts):
    right = [_mm_raw(ct, inv, "nt", 3) for ct, inv in zip(cts, invs)]
    return ([-_mm_raw(inv, r, "tn", 3) for inv, r in zip(invs, right)],)


_unit_lower_inverses_diff.defvjp(_unit_lower_inverses_fwd, _unit_lower_inverses_bwd)


GDN_STEP_CHUNKS = 2


def _gdn_chunk(qs, ks, vs, bas, alog, dtb, states, inverses=_unit_lower_inverses, mm=_mm_raw):
    C = qs[0].shape[0]
    nchunks = len(bas)
    items = [(c, h) for c in range(nchunks) for h in range(HEADS)]
    ri = lax.broadcasted_iota(jnp.int32, (C, C), 0)
    ci = lax.broadcasted_iota(jnp.int32, (C, C), 1)
    causal = ri >= ci
    strict = ri > ci
    tri = causal.astype(F32)
    betas = [jax.nn.sigmoid(_lane_pick(bas[c], h)) for c, h in items]
    gs = [-jnp.exp(_lane_pick(alog, h)) * _softplus(_lane_pick(bas[c], h + HEADS) + _lane_pick(dtb, h))
          for c, h in items]
    qn = [q * lax.rsqrt(jnp.sum(q * q, axis=-1, keepdims=True) + 1e-6) * (HEAD_DIM ** -0.5) for q in qs]
    kn = [k * lax.rsqrt(jnp.sum(k * k, axis=-1, keepdims=True) + 1e-6) for k in ks]
    gc = [_hdot(tri, jnp.broadcast_to(g, (C, C))) for g in gs]
    decay = [jnp.where(causal, jnp.exp(jnp.where(causal, c - c.T, 0.0)), 0.0) for c in gc]
    eg = [jnp.exp(c) for c in gc]
    kb = [k * b for k, b in zip(kn, betas)]
    vb = [v * b for v, b in zip(vs, betas)]
    los = [jnp.where(strict, mm(a, k, "nt", 1) * d, 0.0) for a, k, d in zip(kb, kn, decay)]
    attn = [jnp.where(causal, mm(q, k, "nt", 1) * d, 0.0) for q, k, d in zip(qn, kn, decay)]
    tinv = inverses(los)
    u = [mm(t, x, "nn", 3) for t, x in zip(tinv, vb)]
    w = [mm(t, a * e, "nn", 3) for t, a, e in zip(tinv, kb, eg)]
    g_last = [c[C - 1:C, :] for c in gc]
    k_tail = [k * jnp.exp(gl - c) for k, gl, c in zip(kn, g_last, gc)]
    q_dec = [q * e for q, e in zip(qn, eg)]
    outs = []
    for c in range(nchunks):
        idx = range(c * HEADS, (c + 1) * HEADS)
        v_new = [u[i] - mm(w[i], s, "nn", 1) for i, s in zip(idx, states)]
        o_state = [mm(q_dec[i], s, "nn", 1) for i, s in zip(idx, states)]
        outs += [a + mm(attn[i], vn, "nn", 1) for i, a, vn in zip(idx, o_state, v_new)]
        states = [s * jnp.exp(g_last[i]) + mm(k_tail[i], vn, "tn", 1) for i, s, vn in zip(idx, states, v_new)]
    return outs, states


def _gdn_fwd(qkv, proj, alog, dtb, gather=()):
    S = qkv.shape[0]
    per = min(GDN_STEP_CHUNKS, S // CHUNK)
    T = per * CHUNK
    nc = S // T
    nk = len(gather)
    assert CHUNK == HEAD_DIM

    def body(*refs):
        q_ref, k_ref, v_ref, ba_ref, alog_ref, dtb_ref = refs[:6]
        o_ref, st_ref = refs[6 + nk:8 + nk]
        state = refs[8 + 2 * nk]
        if nk:
            start, finish = _gather_steps(refs[6:6 + nk], refs[8 + nk:8 + 2 * nk], *refs[9 + 2 * nk:])
            pl.when(pl.program_id(0) == 0)(start)

        @pl.when(pl.program_id(0) == 0)
        def _():
            state[...] = jnp.zeros_like(state)

        sls = [slice(hd * HEAD_DIM, (hd + 1) * HEAD_DIM) for hd in range(HEADS)]
        rows = [slice(c * CHUNK, (c + 1) * CHUNK) for c in range(per)]
        s0 = [state[hd] for hd in range(HEADS)]
        for hd in range(HEADS):
            st_ref[hd, 0] = s0[hd]
        o, s1 = _gdn_chunk([q_ref[r, sl] for r in rows for sl in sls], [k_ref[r, sl] for r in rows for sl in sls],
                           [v_ref[r, sl] for r in rows for sl in sls], [ba_ref[r, :] for r in rows],
                           alog_ref[...], dtb_ref[...], s0)
        for c, r in enumerate(rows):
            for hd in range(HEADS):
                o_ref[r, sls[hd]] = o[c * HEADS + hd]
        for hd in range(HEADS):
            state[hd] = s1[hd]
        if nk:
            pl.when(pl.program_id(0) == nc - 1)(finish)

    def col(j):
        return pl.BlockSpec((T, GDN_W),lambda n: (n, j))

    vec = _const((1, LANES))
    outs = pl.pallas_call(
        body, name="gdn_fwd", grid=(nc,),
        in_specs=[col(0), col(1), col(2), pl.BlockSpec((T, LANES),lambda n: (n, IN_PAD // LANES - 1)), vec, vec]
        + _hbm_specs(nk),
        out_specs=[col(0), pl.BlockSpec((HEADS, 1, HEAD_DIM, HEAD_DIM), lambda n: (0, n, 0, 0))] + _hbm_specs(nk),
        out_shape=[_sds((S, GDN_W)), _sds((HEADS, nc, HEAD_DIM, HEAD_DIM))] + (_gather_out_shapes(gather) if nk else []),
        scratch_shapes=[pltpu.VMEM((HEADS, HEAD_DIM, HEAD_DIM), F32)] + (_gather_scratch(nk) if nk else []),
        compiler_params=_params(("arbitrary",)),
    )(qkv, qkv, qkv, proj, alog, dtb, *gather)
    return outs[0], outs[1], list(outs[2:])


def _gdn_bwd(do, qkv, proj, states, alog, dtb, scatter=()):
    S = qkv.shape[0]
    per = min(GDN_STEP_CHUNKS, S // CHUNK)
    T = per * CHUNK
    nc = S // T
    nk = len(scatter)

    def body(*refs):
        do_ref, q_ref, k_ref, v_ref, ba_ref, st_ref, alog_ref, dtb_ref = refs[:8]
        dqkv_ref, dba_ref, dalog_ref, ddtb_ref = refs[8 + nk:12 + nk]
        dstate = refs[12 + 2 * nk]
        n = pl.program_id(0)
        if nk:
            start, finish = _scatter_steps(refs[8:8 + nk], refs[12 + nk:12 + 2 * nk], *refs[13 + 2 * nk:])
            pl.when(n == 0)(start)

        @pl.when(n == 0)
        def _():
            dstate[...] = jnp.zeros_like(dstate)

        sls = [slice(hd * HEAD_DIM, (hd + 1) * HEAD_DIM) for hd in range(HEADS)]
        rows = [slice(c * CHUNK, (c + 1) * CHUNK) for c in range(per)]
        fn = functools.partial(_gdn_chunk, inverses=_unit_lower_inverses_diff, mm=_mm)
        _, vjp = jax.vjp(fn, [q_ref[r, sl] for r in rows for sl in sls], [k_ref[r, sl] for r in rows for sl in sls],
                         [v_ref[r, sl] for r in rows for sl in sls], [ba_ref[r, :] for r in rows],
                         alog_ref[...], dtb_ref[...], [st_ref[hd, 0] for hd in range(HEADS)])
        dq, dk, dv, dba, dalog, ddtb, ds = vjp(([do_ref[r, sl] for r in rows for sl in sls],
                                                [dstate[hd] for hd in range(HEADS)]))
        for c, r in enumerate(rows):
            for hd in range(HEADS):
                i = c * HEADS + hd
                dqkv_ref[r, sls[hd]] = dq[i]
                dqkv_ref[r, GDN_W + hd * HEAD_DIM:GDN_W + (hd + 1) * HEAD_DIM] = dk[i]
                dqkv_ref[r, 2 * GDN_W + hd * HEAD_DIM:2 * GDN_W + (hd + 1) * HEAD_DIM] = dv[i]
            dba_ref[r, :] = dba[c]
        for hd in range(HEADS):
            dstate[hd] = ds[hd]
        _acc(dalog_ref, dalog, n == 0)
        _acc(ddtb_ref, ddtb, n == 0)
        if nk:
            pl.when(n == nc - 1)(finish)

    def col(j):
        return pl.BlockSpec((T, GDN_W),lambda n: (nc - 1 - n, j))

    vec = _const((1, LANES))
    outs = pl.pallas_call(
        body, name="gdn_bwd", grid=(nc,),
        in_specs=[col(0), col(0), col(1), col(2),
                  pl.BlockSpec((T, LANES),lambda n: (nc - 1 - n, IN_PAD // LANES - 1)),
                  pl.BlockSpec((HEADS, 1, HEAD_DIM, HEAD_DIM), lambda n: (0, nc - 1 - n, 0, 0)), vec, vec]
        + _hbm_specs(nk),
        out_specs=[pl.BlockSpec((T, 3 * GDN_W),lambda n: (nc - 1 - n, 0)),
                   pl.BlockSpec((T, LANES),lambda n: (nc - 1 - n, 0)), vec, vec] + _hbm_specs(nk),
        out_shape=[_sds((S, 3 * GDN_W)), _sds((S, LANES)), _sds((1, LANES)), _sds((1, LANES))]
        + [_sds(p.shape, p.dtype) for p in scatter],
        scratch_shapes=[pltpu.VMEM((HEADS, HEAD_DIM, HEAD_DIM), F32)] + (_scatter_scratch(nk) if nk else []),
        compiler_params=_params(("arbitrary",)),
    )(do, qkv, qkv, qkv, proj, states, alog, dtb, *scatter)
    return outs[0], outs[1], outs[2], outs[3], list(outs[4:])


def _gdn_gate(o, z, gnw):
    outs = []
    for hd in range(HEADS):
        sl = slice(hd * HEAD_DIM, (hd + 1) * HEAD_DIM)
        xh, _ = _rms_parts(o[:, sl])
        outs.append(xh * gnw * _silu(z[:, sl]))
    return jnp.concatenate(outs, axis=-1)


def _out_fwd(x, out_lru, o, proj, gnw, g1, wout):
    S = x.shape[0]
    tm = min(512, S)

    def body(x_ref, lru_ref, o_ref, z_ref, gnw_ref, g1_ref, w_ref, x1_ref, cat_ref):
        cat = jnp.concatenate([lru_ref[...], _gdn_gate(o_ref[...], z_ref[...], gnw_ref[...])], axis=-1).astype(BF16)
        cat_ref[...] = cat
        x1_ref[...] = x_ref[...] + g1_ref[...] * _dot(cat, w_ref[...])

    return pl.pallas_call(
        body, name="out_fwd", grid=(S // tm,),
        in_specs=[_row(tm, D_MODEL), _row(tm, LRU_W), _row(tm, GDN_W), _row(tm, GDN_W, 5), _const((1, LANES)),
                  _const((1, D_MODEL)), _const((D_MODEL, D_MODEL))],
        out_specs=[_row(tm, D_MODEL), _row(tm, D_MODEL)],
        out_shape=[_sds((S, D_MODEL)), _sds((S, D_MODEL), BF16)],
        compiler_params=_params(("arbitrary",)),
    )(x, out_lru, o, proj, gnw, g1, wout)


def _out_bwd(dx1, cat, o, proj, gnw, g1, wout):
    S = dx1.shape[0]
    tm = min(512, S)

    def body(dx1_ref, cat_ref, o_ref, z_ref, gnw_ref, g1_ref, w_ref,
             dlru_ref, do_ref, dz_ref, dmb_ref, dgnw_ref, dg1_ref):
        i = pl.program_id(0)
        d1 = dx1_ref[...]
        mix = _dot(cat_ref[...], w_ref[...])
        _acc(dg1_ref, _colsum(d1 * mix), i == 0)
        dmb = (d1 * g1_ref[...]).astype(BF16)
        dmb_ref[...] = dmb
        dcat = _dot_nt(dmb, w_ref[...])
        dlru_ref[...] = dcat[:, :LRU_W]
        _, vjp = jax.vjp(_gdn_gate, o_ref[...], z_ref[...], gnw_ref[...])
        do, dz, dgnw = vjp(dcat[:, LRU_W:])
        do_ref[...] = do
        dz_ref[...] = dz
        _acc(dgnw_ref, dgnw, i == 0)

    return pl.pallas_call(
        body, name="out_bwd", grid=(S // tm,),
        in_specs=[_row(tm, D_MODEL), _row(tm, D_MODEL), _row(tm, GDN_W), _row(tm, GDN_W, 5), _const((1, LANES)),
                  _const((1, D_MODEL)), _const((D_MODEL, D_MODEL))],
        out_specs=[_row(tm, LRU_W), _row(tm, GDN_W), _row(tm, GDN_W), _row(tm, D_MODEL), _const((1, LANES)),
                   _const((1, D_MODEL))],
        out_shape=[_sds((S, LRU_W)), _sds((S, GDN_W)), _sds((S, GDN_W)), _sds((S, D_MODEL), BF16), _sds((1, LANES)),
                   _sds((1, D_MODEL))],
        compiler_params=_params(("arbitrary",)),
    )(dx1, cat, o, proj, gnw, g1, wout)


MLP_TM = 256


def _load_once(step, pairs, sem):
    @pl.when(step == 0)
    def _():
        copies = [pltpu.make_async_copy(src, dst, sem.at[k]) for k, (src, dst) in enumerate(pairs)]
        for cp in copies:
            cp.start()
        for cp in copies:
            cp.wait()


def _mlp_fwd(x1, nw, sc, sh, g2, wup, wdown, gather=()):
    S = x1.shape[0]
    tm = min(MLP_TM, S)
    nt = S // tm
    nk = len(gather)

    def body(*refs):
        x_ref, nw_ref, sc_ref, sh_ref, g2_ref, wup_hbm, wdown_hbm = refs[:7]
        x2_ref = refs[7 + nk]
        wup, wdown, sem = refs[8 + 2 * nk:11 + 2 * nk]
        step = pl.program_id(0)
        if nk:
            start, finish = _gather_steps(refs[7:7 + nk], refs[8 + nk:8 + 2 * nk], *refs[11 + 2 * nk:])
            pl.when(step == 0)(start)
        _load_once(step, [(wup_hbm, wup), (wdown_hbm, wdown)], sem)
        x = x_ref[...]
        hb = _norm_mod(x, nw_ref[...], sc_ref[...], sh_ref[...]).astype(BF16)
        r = jnp.maximum(_dot(hb, wup[...]), 0.0)
        x2_ref[...] = x + g2_ref[...] * _dot((r * r).astype(BF16), wdown[...])
        if nk:
            pl.when(step == nt - 1)(finish)

    vec = _const((1, D_MODEL))
    anyspec = pl.BlockSpec(memory_space=pl.ANY)
    outs = pl.pallas_call(
        body, name="mlp_fwd", grid=(nt,),
        in_specs=[_row(tm, D_MODEL), vec, vec, vec, vec, anyspec, anyspec] + _hbm_specs(nk),
        out_specs=[_row(tm, D_MODEL)] + _hbm_specs(nk),
        out_shape=[_sds((S, D_MODEL))] + (_gather_out_shapes(gather) if nk else []),
        scratch_shapes=[pltpu.VMEM((D_MODEL, D_FF), BF16), pltpu.VMEM((D_FF, D_MODEL), BF16),
                        pltpu.SemaphoreType.DMA((2,))] + (_gather_scratch(nk) if nk else []),
        compiler_params=_params(("arbitrary",)),
    )(x1, nw, sc, sh, g2, wup, wdown, *gather)
    return outs[0], list(outs[1:])


def _mlp_bwd(dx2, x1, nw, sc, sh, g2, wup, wdown):
    S = x1.shape[0]
    tm = min(MLP_TM, S)

    def body(dx2_ref, x_ref, nw_ref, sc_ref, sh_ref, g2_ref, wup_hbm, wdown_hbm,
             dx1_ref, hb_ref, dupb_ref, actb_ref, d2b_ref, dnw_ref, dsc_ref, dsh_ref, wup, wdown, sem):
        i = pl.program_id(0)
        _load_once(i, [(wup_hbm, wup), (wdown_hbm, wdown)], sem)
        x = x_ref[...]
        d2 = dx2_ref[...]
        hb = _norm_mod(x, nw_ref[...], sc_ref[...], sh_ref[...]).astype(BF16)
        hb_ref[...] = hb
        r = jnp.maximum(_dot(hb, wup[...]), 0.0)
        actb = (r * r).astype(BF16)
        actb_ref[...] = actb
        d2b_ref[...] = d2.astype(BF16)
        ddb = (d2 * g2_ref[...]).astype(BF16)
        dupb = (_dot_nt(ddb, wdown[...]) * (2.0 * r)).astype(BF16)
        dupb_ref[...] = dupb
        dh = _dot_nt(dupb, wup[...])
        dx, dnw, dsc, dsh = _norm_mod_bwd(dh, x, nw_ref[...], sc_ref[...])
        dx1_ref[...] = d2 + dx
        _acc(dnw_ref, dnw, i == 0)
        _acc(dsc_ref, dsc, i == 0)
        _acc(dsh_ref, dsh, i == 0)

    vec = _const((1, D_MODEL))
    anyspec = pl.BlockSpec(memory_space=pl.ANY)
    return pl.pallas_call(
        body, name="mlp_bwd", grid=(S // tm,),
        in_specs=[_row(tm, D_MODEL), _row(tm, D_MODEL), vec, vec, vec, vec, anyspec, anyspec],
        out_specs=[_row(tm, D_MODEL), _row(tm, D_MODEL), _row(tm, D_FF), _row(tm, D_FF), _row(tm, D_MODEL),
                   vec, vec, vec],
        out_shape=[_sds((S, D_MODEL)), _sds((S, D_MODEL), BF16), _sds((S, D_FF), BF16), _sds((S, D_FF), BF16),
                   _sds((S, D_MODEL), BF16), _sds((1, D_MODEL)), _sds((1, D_MODEL)), _sds((1, D_MODEL))],
        scratch_shapes=[pltpu.VMEM((D_MODEL, D_FF), BF16), pltpu.VMEM((D_FF, D_MODEL), BF16),
                        pltpu.SemaphoreType.DMA((2,))],
        compiler_params=_params(("arbitrary",)),
    )(dx2, x1, nw, sc, sh, g2, wup, wdown)


def _dw_down(act, d2b, g2, wdown, sharded, out_dtype):
    K, M = act.shape
    N = d2b.shape[1]
    tk = min(2048, K)
    nk = K // tk
    if sharded:
        h = M // (2 * N_CHIPS)
        tm = 2 * h
        out_spec = pl.BlockSpec((2, 1, h, N), lambda i, k: (0, i, 0, 0))
        out_shape = _sds((2, N_CHIPS, h, N), out_dtype)
    else:
        tm = min(512, M)
        out_spec = pl.BlockSpec((tm, N), lambda i, k: (i, 0))
        out_shape = _sds((M, N), out_dtype)

    def body(a_ref, b_ref, g2_ref, w_ref, o_ref, dg2_ref, acc):
        i = pl.program_id(0)
        k = pl.program_id(1)
        _acc(acc, _dot_tn(a_ref[...], b_ref[...]), k == 0)

        @pl.when(k == nk - 1)
        def _():
            g = acc[...]
            _acc(dg2_ref, _colsum(g * w_ref[...].astype(F32)), i == 0)
            out = (g * g2_ref[...]).astype(o_ref.dtype)
            if sharded:
                o_ref[0, 0] = out[:h]
                o_ref[1, 0] = out[h:]
            else:
                o_ref[...] = out

    vec = pl.BlockSpec((1, N), lambda i, k: (0, 0))
    return pl.pallas_call(
        body, name="dw_down", grid=(M // tm, nk),
        in_specs=[pl.BlockSpec((tk, tm), lambda i, k: (k, i)), pl.BlockSpec((tk, N), lambda i, k: (k, 0)), vec,
                  pl.BlockSpec((tm, N), lambda i, k: (i, 0))],
        out_specs=[out_spec, vec], out_shape=[out_shape, _sds((1, N))],
        scratch_shapes=[pltpu.VMEM((tm, N), F32)],
        compiler_params=_params(("arbitrary", "arbitrary")),
    )(act, d2b, g2, wdown)


def _matmul_tn(a, b, name, shards=None, out_dtype=F32):
    K, M = a.shape
    N = b.shape[1]
    tk = min(2048, K)
    if shards == "cols":
        tm, tn = M, N // N_CHIPS
        out_spec = pl.BlockSpec((2, 1, M // 2, tn), lambda i, j, k: (0, j, 0, 0))
        out_shape = _sds((2, N_CHIPS, M // 2, tn))
    elif shards == "rows":
        h, tn = M // (2 * N_CHIPS), N
        tm = max(512, 2 * h)
        per_tile = tm // (2 * h)
        out_spec = pl.BlockSpec((2, per_tile, h, tn), lambda i, j, k: (0, i, 0, 0))
        out_shape = _sds((2, N_CHIPS, h, tn))
    else:
        tm = min(512, M)
        if N % 640 == 0:
            tn, tk = N, min(1024, K)
        else:
            tn = min(1024, N)
        out_spec = pl.BlockSpec((tm, tn), lambda i, j, k: (i, j))
        out_shape = _sds((M, N))
    nk = K // tk

    def body(a_ref, b_ref, o_ref, acc):
        k = pl.program_id(2)
        _acc(acc, _dot_tn(a_ref[...], b_ref[...]), k == 0)

        @pl.when(k == nk - 1)
        def _():
            if shards == "rows":
                for s in range(per_tile):
                    for half in range(2):
                        r0 = (2 * s + half) * h
                        o_ref[half, s] = acc[r0:r0 + h, :].astype(o_ref.dtype)
            elif shards == "cols":
                o_ref[0, 0] = acc[:M // 2, :].astype(o_ref.dtype)
                o_ref[1, 0] = acc[M // 2:, :].astype(o_ref.dtype)
            else:
                o_ref[...] = acc[...].reshape(o_ref.shape).astype(o_ref.dtype)

    return pl.pallas_call(
        body, name=name, grid=(M // tm, N // tn, nk),
        in_specs=[pl.BlockSpec((tk, tm), lambda i, j, k: (k, i)), pl.BlockSpec((tk, tn), lambda i, j, k: (k, j))],
        out_specs=out_spec, out_shape=_sds(out_shape.shape, out_dtype),
        scratch_shapes=[pltpu.VMEM((tm, tn), F32)],
        compiler_params=_params(("arbitrary", "arbitrary", "arbitrary")),
    )(a, b)


def _loss_head(x, target, fnw):
    S = x.shape[0]
    tm = min(512, S)

    def body(x_ref, t_ref, w_ref, dx_ref, loss_ref, dw_ref):
        i = pl.program_id(0)
        w = w_ref[...]
        xh, r = _rms_parts(x_ref[...])
        err = xh * w - t_ref[...]
        part = 0.5 * jnp.sum(jnp.mean(err * err, axis=-1, keepdims=True), axis=0, keepdims=True)
        _acc(loss_ref, jnp.broadcast_to(part, (SUBLANES, LANES)), i == 0)
        dx, dw = _rms_bwd(err * (1.0 / D_MODEL), xh, r, w)
        dx_ref[...] = dx
        _acc(dw_ref, dw, i == 0)

    vec = _const((1, D_MODEL))
    return pl.pallas_call(
        body, name="loss_head", grid=(S // tm,),
        in_specs=[_row(tm, D_MODEL), _row(tm, D_MODEL), vec],
        out_specs=[_row(tm, D_MODEL), _const((SUBLANES, LANES)), vec],
        out_shape=[_sds((S, D_MODEL)), _sds((SUBLANES, LANES)), _sds((1, D_MODEL))],
        compiler_params=_params(("arbitrary",)),
    )(x, target, fnw)


def _block_diag(w):
    eye = jnp.eye(LRU_BLOCKS, dtype=w.dtype)
    return (eye[:, None, :, None] * w[:, :, None, :]).reshape(LRU_W, LRU_W)


def _diag_blocks(m):
    m4 = m.reshape(LRU_BLOCKS, LRU_BLOCK, LRU_BLOCKS, LRU_BLOCK)
    return jnp.stack([m4[g, :, g, :] for g in range(LRU_BLOCKS)])


def _layer_fwd(x, p, mlp_shards=(), mlp_weights=None, next_shards=()):
    proj, h1b = _proj_fwd(x, p["nmw"], p["sc1"], p["sh1"], p["win"])
    xr = _conv_fwd(proj, 0, LRU_W, p["lcw"], p["lcb"], False, "conv_lru_fwd")
    out_lru, h = _lru_fwd(xr, proj, p["wa"].astype(BF16), p["ba"], p["wx"].astype(BF16), p["bx"], p["lam"], p["lnw"])
    qkv = _conv_fwd(proj, 2 * LRU_W, 3 * GDN_W, p["gcw"], p["gcb"], True, "conv_gdn_fwd")
    o, states, gathered = _gdn_fwd(qkv, proj, p["alog"], p["dtb"], mlp_shards)
    if mlp_weights is not None:
        p = {**p, **mlp_weights(gathered)}
    x1, cat = _out_fwd(x, out_lru, o, proj, p["gnw"], p["g1"], p["wout"])
    x2, gathered_next = _mlp_fwd(x1, p["nmlp"], p["sc2"], p["sh2"], p["g2"], p["wup"], p["wdown"], next_shards)
    res = dict(x=x, proj=proj, h1b=h1b, xr=xr, h=h, qkv=qkv, o=o, states=states, x1=x1, cat=cat)
    return x2, res, p, gathered_next


def _layer_bwd(dx2, p, r, sharded=False, scatter=(), early=None):
    dx1, h2b, dupb, actb, d2b, dnmlp, dsc2, dsh2 = _mlp_bwd(
        dx2, r["x1"], p["nmlp"], p["sc2"], p["sh2"], p["g2"], p["wup"], p["wdown"])
    gdt = BF16 if sharded else F32
    g_wup = _matmul_tn(h2b, dupb, "dw_up", "cols" if sharded else None, gdt)
    g_wdown, dg2 = _dw_down(actb, d2b, p["g2"], p["wdown"], sharded, gdt)
    dlru, do, dz, dmb, dgnw, dg1 = _out_bwd(dx1, r["cat"], r["o"], r["proj"], p["gnw"], p["g1"], p["wout"])
    g_wout = _matmul_tn(r["cat"], dmb, "dw_out", "rows" if sharded else None, gdt)
    if early is not None:
        scatter = list(scatter) + early([g_wout, g_wup, g_wdown])
    dqkv_act, dba, dalog, ddtb, arrived = _gdn_bwd(do, r["qkv"], r["proj"], r["states"], p["alog"], p["dtb"], scatter)
    dqkv, dgcw, _ = _conv_bwd(r["proj"], 2 * LRU_W, 3 * GDN_W, p["gcw"], p["gcb"], dqkv_act, True, "conv_gdn_bwd")
    wab = p["wa"].astype(BF16)
    wxb = p["wx"].astype(BF16)
    dxr, dly, dwa, dba_, dwx, dbx, dlam, dlnw = _lru_bwd(
        dlru, r["xr"], r["proj"], r["h"], wab, p["ba"], wxb, p["bx"], p["lam"], p["lnw"])
    dlx, dlcw, dlcb = _conv_bwd(r["proj"], 0, LRU_W, p["lcw"], p["lcb"], dxr, False, "conv_lru_bwd")
    dx, dpb, dnmw, dsc1, dsh1 = _proj_bwd(dx1, r["x"], dlx, dly, dqkv, dz, dba, p["nmw"], p["sc1"], p["win"])
    g_win = _matmul_tn(r["h1b"], dpb, "dw_in", None, gdt)
    grads = dict(nmw=dnmw, nmlp=dnmlp, sh1=dsh1, sc1=dsc1, g1=dg1, sh2=dsh2, sc2=dsc2, g2=dg2,
                 win=g_win, lcw=dlcw, lcb=dlcb, wa=dwa, ba=dba_, wx=dwx, bx=dbx, lam=dlam, lnw=dlnw,
                 gcw=dgcw, alog=dalog, dtb=ddtb, gnw=dgnw, wout=g_wout, wup=g_wup, wdown=g_wdown)
    return dx, grads, arrived


def _local_step(x, target, fnw, layers):
    res = []
    for p in layers:
        x, r, _, _ = _layer_fwd(x, p)
        res.append(r)
    dx, loss_blk, dfnw = _loss_head(x, target, fnw)
    grads = [None] * len(layers)
    for l in reversed(range(len(layers))):
        dx, grads[l], _ = _layer_bwd(dx, layers[l], res[l])
    stacked = {k: jnp.stack([g[k] for g in grads]) for k in grads[0]}
    return loss_blk[0, 0], dx, dfnw, stacked


def _prep_layers(norm_mix_w, norm_mlp_w, mod, win_b, lru_conv_w, lru_conv_b, gate_a_w, gate_a_b, gate_x_w, gate_x_b,
                 lru_lambda, lru_norm_w, gdn_conv_w, gdn_a_log, gdn_dt_bias, gdn_norm_w, wout_b, wup_b, wdown_b):
    L = norm_mix_w.shape[0]

    def vec(a):
        return a.reshape(L, 1, -1)

    def lanes(a):
        return jnp.pad(a, ((0, 0), (0, LANES - a.shape[1]))).reshape(L, 1, LANES)

    def taps(w):
        return jnp.pad(w, ((0, 0), (0, SUBLANES - w.shape[1]), (0, 0)))

    m = mod.reshape(L, N_MOD, 1, D_MODEL)
    return dict(
        nmw=vec(norm_mix_w), nmlp=vec(norm_mlp_w),
        sh1=m[:, 0], sc1=m[:, 1], g1=m[:, 2], sh2=m[:, 3], sc2=m[:, 4], g2=m[:, 5],
        win=win_b, lcw=taps(lru_conv_w), lcb=vec(lru_conv_b),
        wa=jax.vmap(_block_diag)(gate_a_w), ba=vec(gate_a_b), wx=jax.vmap(_block_diag)(gate_x_w), bx=vec(gate_x_b),
        lam=vec(lru_lambda), lnw=vec(lru_norm_w),
        gcw=taps(gdn_conv_w), gcb=jnp.zeros((L, 1, 3 * GDN_W), F32),
        alog=lanes(gdn_a_log), dtb=lanes(gdn_dt_bias), gnw=vec(gdn_norm_w),
        wout=wout_b, wup=wup_b, wdown=wdown_b)


def _position():
    x, y, c = lax.axis_index("x"), lax.axis_index("y"), lax.axis_index("c")
    return x, y, c


def _other_chips(x, y):
    return [(1 - x, y), (x, 1 - y), (1 - x, 1 - y)]


def _all_gather_rows(block, name):
    m, n = block.shape

    def body(x_ref, out_ref, send_sems, recv_sems, local_sem):
        x, y, c = _position()
        me, sibling = (x, y, c), (x, y, 1 - c)
        chips = _other_chips(x, y)

        def rows(px, py, pc):
            return out_ref.at[pl.ds((4 * px + 2 * py + pc) * m, m), :]

        def copy(k, blk, to, src=None):
            return pltpu.make_async_remote_copy(
                src_ref=rows(*blk) if src is None else src, dst_ref=rows(*blk),
                send_sem=send_sems.at[k], recv_sem=recv_sems.at[k], device_id=to, device_id_type=MESH)

        mine = pltpu.make_async_copy(x_ref, rows(*me), local_sem)
        mine.start()
        first = [copy(0, me, sibling, src=x_ref)]
        first += [copy(1 + j, me, (*chip, c), src=x_ref) for j, chip in enumerate(chips)]
        for cp in first:
            cp.start()
        passed = [copy(4 + j, (*chip, c), sibling) for j, chip in enumerate(chips)]
        for j, chip in enumerate(chips):
            copy(1 + j, (*chip, c), me).wait_recv()
            passed[j].start()
        copy(0, sibling, me).wait_recv()
        for j, chip in enumerate(chips):
            copy(4 + j, (*chip, 1 - c), me).wait_recv()
        for cp in first + passed:
            cp.wait_send()
        mine.wait()

    return pl.pallas_call(
        body, name=name,
        out_shape=_sds((N_DEV * m, n)),
        in_specs=[pl.BlockSpec(memory_space=pltpu.VMEM)],
        out_specs=pl.BlockSpec(memory_space=pltpu.VMEM),
        scratch_shapes=[pltpu.SemaphoreType.DMA((7,)), pltpu.SemaphoreType.DMA((7,)), pltpu.SemaphoreType.DMA],
        compiler_params=pltpu.CompilerParams(vmem_limit_bytes=VMEM_LIMIT),
    )(block)


def _hbm_specs(n):
    return [pl.BlockSpec(memory_space=pl.ANY)] * n


def _gather_chips(shards, name):
    n = len(shards)

    def body(*refs):
        start, finish = _gather_steps(refs[:n], refs[n:2 * n], *refs[2 * n:])
        start()
        finish()

    return pl.pallas_call(
        body, name=name,
        out_shape=_gather_out_shapes(shards), in_specs=_hbm_specs(n), out_specs=_hbm_specs(n),
        scratch_shapes=_gather_scratch(n),
    )(*shards)


def _gather_out_shapes(shards):
    return [_sds((N_CHIPS, 2, s.shape[0] // 2, s.shape[1]), s.dtype) for s in shards]


def _gather_scratch(n):
    return [pltpu.SemaphoreType.DMA((6 * n,)), pltpu.SemaphoreType.DMA((6 * n,))]


def _gather_steps(ins, outs, send_sems, recv_sems):
    n = len(ins)
    x, y, c = _position()
    chips = _other_chips(x, y)
    me = 2 * x + y

    def first(a, j, slot):
        h = ins[a].shape[0] // 2
        return pltpu.make_async_remote_copy(
            src_ref=ins[a].at[pl.ds(pl.multiple_of(c * h, SUBLANES), h)], dst_ref=outs[a].at[slot, c],
            send_sem=send_sems.at[3 * a + j], recv_sem=recv_sems.at[3 * a + j],
            device_id=(chips[j][0], chips[j][1], c), device_id_type=MESH)

    def second(a, j, half):
        slot = 2 * chips[j][0] + chips[j][1]
        return pltpu.make_async_remote_copy(
            src_ref=outs[a].at[slot, c], dst_ref=outs[a].at[slot, half],
            send_sem=send_sems.at[3 * (n + a) + j], recv_sem=recv_sems.at[3 * (n + a) + j],
            device_id=(x, y, 1 - c), device_id_type=MESH)

    def start():
        for a in range(n):
            for j in range(3):
                first(a, j, me).start()

    def finish():
        for a in range(n):
            for j, (px, py) in enumerate(chips):
                first(a, j, 2 * px + py).wait_recv()
                second(a, j, c).start()
        for a in range(n):
            for j in range(3):
                second(a, j, 1 - c).wait_recv()
        for a in range(n):
            for j in range(3):
                first(a, j, me).wait_send()
                second(a, j, c).wait_send()

    return start, finish


def _send_to_sibling(parts, name):
    n = len(parts)

    def body(*refs):
        ins, outs = refs[:n], refs[n:2 * n]
        send_sems, recv_sems = refs[2 * n:]
        x, y, c = _position()
        copies = [pltpu.make_async_remote_copy(
            src_ref=ins[a].at[1 - c], dst_ref=outs[a], send_sem=send_sems.at[a], recv_sem=recv_sems.at[a],
            device_id=(x, y, 1 - c), device_id_type=MESH) for a in range(n)]
        for cp in copies:
            cp.start()
        for cp in copies:
            cp.wait()

    return pl.pallas_call(
        body, name=name,
        out_shape=[_sds(p.shape[1:], p.dtype) for p in parts],
        in_specs=_hbm_specs(n), out_specs=_hbm_specs(n),
        scratch_shapes=[pltpu.SemaphoreType.DMA((n,)), pltpu.SemaphoreType.DMA((n,))],
    )(*parts)


def _scatter_chips(parts, name):
    n = len(parts)

    def body(*refs):
        start, finish = _scatter_steps(refs[:n], refs[n:2 * n], *refs[2 * n:])
        start()
        finish()

    return pl.pallas_call(
        body, name=name,
        out_shape=[_sds(p.shape, p.dtype) for p in parts], in_specs=_hbm_specs(n), out_specs=_hbm_specs(n),
        scratch_shapes=_scatter_scratch(n),
    )(*parts)


def _scatter_scratch(n):
    return [pltpu.SemaphoreType.DMA((3 * n,)), pltpu.SemaphoreType.DMA((3 * n,))]


def _scatter_steps(ins, outs, send_sems, recv_sems):
    n = len(ins)
    x, y, c = _position()
    chips = _other_chips(x, y)
    me = 2 * x + y

    def copy(a, j, src_slot, dst_slot):
        px, py = chips[j]
        return pltpu.make_async_remote_copy(
            src_ref=ins[a].at[src_slot], dst_ref=outs[a].at[dst_slot], send_sem=send_sems.at[3 * a + j],
            recv_sem=recv_sems.at[3 * a + j], device_id=(px, py, c), device_id_type=MESH)

    def start():
        for a in range(n):
            for j in range(3):
                copy(a, j, 2 * chips[j][0] + chips[j][1], me).start()

    def finish():
        for a in range(n):
            for j, (px, py) in enumerate(chips):
                copy(a, j, me, 2 * px + py).wait_recv()
        for a in range(n):
            for j in range(3):
                copy(a, j, 2 * chips[j][0] + chips[j][1], me).wait_send()

    return start, finish


def _swap_row_halves(arrays, name):
    n = len(arrays)

    def body(*refs):
        outs = refs[n:2 * n]
        send_sems, recv_sems = refs[2 * n:]
        x, y, c = _position()

        def copy(a, half):
            h = outs[a].shape[0] // 2
            rows = outs[a].at[pl.ds(pl.multiple_of(half * h, SUBLANES), h)]
            return pltpu.make_async_remote_copy(
                src_ref=rows, dst_ref=rows, send_sem=send_sems.at[a], recv_sem=recv_sems.at[a],
                device_id=(x, y, 1 - c), device_id_type=MESH)

        sends = [copy(a, c) for a in range(n)]
        for cp in sends:
            cp.start()
        for a in range(n):
            copy(a, 1 - c).wait_recv()
        for cp in sends:
            cp.wait_send()

    return pl.pallas_call(
        body, name=name,
        out_shape=[_sds(a.shape, a.dtype) for a in arrays],
        in_specs=_hbm_specs(n), out_specs=_hbm_specs(n),
        input_output_aliases={a: a for a in range(n)},
        scratch_shapes=[pltpu.SemaphoreType.DMA((n,)), pltpu.SemaphoreType.DMA((n,))],
    )(*arrays)


def _row_tile(rows):
    for t in (512, 256, 128, 64, 32, 16, 8):
        if rows % t == 0:
            return t
    return rows


def _sum_slots(buf, name):
    k, rows, cols = buf.shape
    tm = _row_tile(rows)

    def body(b_ref, o_ref):
        s = b_ref[0]
        for i in range(1, k):
            s = s + b_ref[i]
        o_ref[...] = s

    return pl.pallas_call(
        body, name=name, grid=(rows // tm,),
        in_specs=[pl.BlockSpec((k, tm, cols), lambda i: (0, i, 0))],
        out_specs=pl.BlockSpec((tm, cols), lambda i: (i, 0)),
        out_shape=_sds((rows, cols)),
        compiler_params=_params(("arbitrary",)),
    )(buf)


def _pair_add(part, from_sibling, core, name):
    _, k, h, cols = part.shape
    rows = k * h
    tm = _row_tile(rows)

    def body(core_ref, a_ref, b_ref, o_ref):
        o_ref[...] = (a_ref[0].astype(F32) + b_ref[...].astype(F32)).astype(BF16)

    out = pl.pallas_call(
        body, name=name,
        grid_spec=pltpu.PrefetchScalarGridSpec(
            num_scalar_prefetch=1, grid=(rows // tm,),
            in_specs=[pl.BlockSpec((1, tm, cols), lambda i, cr: (cr[0], i, 0)),
                      pl.BlockSpec((tm, cols), lambda i, cr: (i, 0))],
            out_specs=pl.BlockSpec((tm, cols), lambda i, cr: (i, 0))),
        out_shape=_sds((rows, cols), BF16),
        compiler_params=_params(("arbitrary",)),
    )(core, part.reshape(2, rows, cols), from_sibling.reshape(rows, cols))
    return out.reshape(k, h, cols)


def _chip_sum(arrived, own, place, name):
    _, h, cols = arrived.shape
    tm = min(256, h)
    nb = h // tm

    def body(place_ref, arr_ref, own_ref, g_ref):
        for chip in range(N_CHIPS):
            @pl.when(place_ref[1] == chip)
            def _():
                terms = [own_ref[0] if j == chip else arr_ref[j] for j in range(N_CHIPS)]
                g = terms[0].astype(F32)
                for t in terms[1:]:
                    g = g + t.astype(F32)
                g_ref[...] = g

    return pl.pallas_call(
        body, name=name,
        grid_spec=pltpu.PrefetchScalarGridSpec(
            num_scalar_prefetch=1, grid=(nb,),
            in_specs=[pl.BlockSpec((N_CHIPS, tm, cols), lambda i, pr: (0, i, 0)),
                      pl.BlockSpec((1, tm, cols), lambda i, pr: (pr[1], i, 0))],
            out_specs=pl.BlockSpec((tm, cols), lambda i, pr: (pr[0] * nb + i, 0))),
        out_shape=_sds((2 * h, cols)),
        compiler_params=_params(("arbitrary",)),
    )(place, arrived, own)


def _adam_layer(g, w, m, v, outs, layer, name):
    rows, cols = g.shape
    tm = _row_tile(rows)

    def body(g_ref, w_ref, m_ref, v_ref, *refs):
        og_ref, od_ref, om_ref, ov_ref = refs[4:]
        gr = g_ref[...]
        og_ref[0] = gr
        d, nm, nv = _adam_math(w_ref[0], gr, m_ref[0], v_ref[0])
        od_ref[0] = d
        om_ref[0] = nm
        ov_ref[0] = nv

    slab = pl.BlockSpec((1, tm, cols), lambda i: (layer, i, 0))
    return pl.pallas_call(
        body, name=name, grid=(rows // tm,),
        in_specs=[pl.BlockSpec((tm, cols), lambda i: (i, 0)), slab, slab, slab] + _hbm_specs(4),
        out_specs=[slab] * 4, out_shape=[_sds(o.shape) for o in outs],
        input_output_aliases={4 + i: i for i in range(4)},
        compiler_params=_params(("arbitrary",)),
    )(g, w, m, v, *outs)


def _adam_math(w, g, m, v):
    m = ADAM_B1 * m + (1.0 - ADAM_B1) * g
    v = ADAM_B2 * v + (1.0 - ADAM_B2) * jnp.square(g)
    m_hat = m / (1.0 - ADAM_B1 ** ADAM_STEP)
    v_hat = v / (1.0 - ADAM_B2 ** ADAM_STEP)
    delta = -ADAM_LR * (m_hat / (jnp.sqrt(v_hat) + ADAM_EPS) + ADAM_WD * w)
    return delta, m, v


def _adam(w, g, m, v, name):
    rows, cols = w.shape
    tm = _row_tile(rows)

    def body(w_ref, g_ref, m_ref, v_ref, d_ref, nm_ref, nv_ref):
        d, nm, nv = _adam_math(w_ref[...], g_ref[...], m_ref[...], v_ref[...])
        d_ref[...] = d
        nm_ref[...] = nm
        nv_ref[...] = nv

    spec = pl.BlockSpec((tm, cols), lambda i: (i, 0))
    return pl.pallas_call(
        body, name=name, grid=(rows // tm,), in_specs=[spec] * 4, out_specs=[spec] * 3,
        out_shape=[_sds((rows, cols))] * 3, compiler_params=_params(("arbitrary",)),
    )(w, g, m, v)


def _mod_fwd(c_all, w_mod, b_mod_cols):
    L, _, n = w_mod.shape

    def body(c_ref, w_ref, b_ref, o_ref):
        o_ref[0] = _hdot(_silu(c_ref[...]), w_ref[0]) + b_ref[0]

    return pl.pallas_call(
        body, name="mod_fwd", grid=(L,),
        in_specs=[_const((N_DEV, D_MODEL)), pl.BlockSpec((1, D_MODEL, n), lambda l: (l, 0, 0)),
                  pl.BlockSpec((1, 1, n), lambda l: (l, 0, 0))],
        out_specs=pl.BlockSpec((1, N_DEV, n), lambda l: (l, 0, 0)),
        out_shape=_sds((L, N_DEV, n)),
        compiler_params=_params(("arbitrary",)),
    )(c_all, w_mod, b_mod_cols)


def _mod_update(c_all, dmod, w, m, v):
    L, _, n = w.shape
    tn = 512

    def body(c_ref, d_ref, w_ref, m_ref, v_ref, g_ref, dl_ref, nm_ref, nv_ref):
        g = _hdot_tn(_silu(c_ref[...]), d_ref[0])
        g_ref[0] = g
        d, nm, nv = _adam_math(w_ref[0], g, m_ref[0], v_ref[0])
        dl_ref[0] = d
        nm_ref[0] = nm
        nv_ref[0] = nv

    big = pl.BlockSpec((1, D_MODEL, tn), lambda l, j: (l, 0, j))
    return pl.pallas_call(
        body, name="mod_update", grid=(L, n // tn),
        in_specs=[_const((N_DEV, D_MODEL)), pl.BlockSpec((1, N_DEV, tn), lambda l, j: (l, 0, j)), big, big, big],
        out_specs=[big] * 4, out_shape=[_sds(w.shape)] * 4,
        compiler_params=_params(("arbitrary", "arbitrary")),
    )(c_all, dmod, w, m, v)


def _pack_rows(parts, row_multiple):
    flat = jnp.concatenate([p.reshape(-1) for p in parts])
    unit = row_multiple * LANES
    flat = jnp.pad(flat, (0, (-flat.shape[0]) % unit))
    return flat.reshape(-1, LANES)


def _unpack(packed, shapes):
    flat = packed.reshape(-1)
    out, off = [], 0
    for s in shapes:
        n = 1
        for d in s:
            n *= d
        out.append(flat[off:off + n].reshape(s))
        off += n
    return out


def _lane_pad(a):
    return jnp.pad(a, ((0, 0), (0, LANES - a.shape[1])))


WEIGHT_NAMES = ("norm_mix_w", "norm_mlp_w", "w_mod", "b_mod", "w_in", "lru_conv_w", "lru_conv_b", "lru_gate_a_w",
                "lru_gate_a_b", "lru_gate_x_w", "lru_gate_x_b", "lru_lambda", "lru_norm_w", "gdn_conv_w", "gdn_a_log",
                "gdn_dt_bias", "gdn_norm_w", "w_out", "w_up", "w_down", "final_norm_w")


def kernel(x, c, norm_mix_w, norm_mlp_w, w_mod, b_mod, w_in, lru_conv_w, lru_conv_b, lru_gate_a_w, lru_gate_a_b, lru_gate_x_w, lru_gate_x_b, lru_lambda, lru_norm_w, gdn_conv_w, gdn_a_log, gdn_dt_bias, gdn_norm_w, w_out, w_up, w_down, final_norm_w, loss_target, m_norm_mix_w, m_norm_mlp_w, m_w_mod, m_b_mod, m_w_in, m_lru_conv_w, m_lru_conv_b, m_lru_gate_a_w, m_lru_gate_a_b, m_lru_gate_x_w, m_lru_gate_x_b, m_lru_lambda, m_lru_norm_w, m_gdn_conv_w, m_gdn_a_log, m_gdn_dt_bias, m_gdn_norm_w, m_w_out, m_w_up, m_w_down, m_final_norm_w, v_norm_mix_w, v_norm_mlp_w, v_w_mod, v_b_mod, v_w_in, v_lru_conv_w, v_lru_conv_b, v_lru_gate_a_w, v_lru_gate_a_b, v_lru_gate_x_w, v_lru_gate_x_b, v_lru_lambda, v_lru_norm_w, v_gdn_conv_w, v_gdn_a_log, v_gdn_dt_bias, v_gdn_norm_w, v_w_out, v_w_up, v_w_down, v_final_norm_w):
    W = dict(zip(WEIGHT_NAMES, (norm_mix_w, norm_mlp_w, w_mod, b_mod, w_in, lru_conv_w, lru_conv_b, lru_gate_a_w,
                                lru_gate_a_b, lru_gate_x_w, lru_gate_x_b, lru_lambda, lru_norm_w, gdn_conv_w, gdn_a_log,
                                gdn_dt_bias, gdn_norm_w, w_out, w_up, w_down, final_norm_w)))
    M = dict(zip(WEIGHT_NAMES, (m_norm_mix_w, m_norm_mlp_w, m_w_mod, m_b_mod, m_w_in, m_lru_conv_w, m_lru_conv_b,
                                m_lru_gate_a_w, m_lru_gate_a_b, m_lru_gate_x_w, m_lru_gate_x_b, m_lru_lambda,
                                m_lru_norm_w, m_gdn_conv_w, m_gdn_a_log, m_gdn_dt_bias, m_gdn_norm_w, m_w_out, m_w_up,
                                m_w_down, m_final_norm_w)))
    V = dict(zip(WEIGHT_NAMES, (v_norm_mix_w, v_norm_mlp_w, v_w_mod, v_b_mod, v_w_in, v_lru_conv_w, v_lru_conv_b,
                                v_lru_gate_a_w, v_lru_gate_a_b, v_lru_gate_x_w, v_lru_gate_x_b, v_lru_lambda,
                                v_lru_norm_w, v_gdn_conv_w, v_gdn_a_log, v_gdn_dt_bias, v_gdn_norm_w, v_w_out, v_w_up,
                                v_w_down, v_final_norm_w)))
    L = DEPTH
    xi, yi, ci = _position()
    chip = 2 * xi + yi
    dev = 2 * chip + ci
    lcs = LRU_W // N_CHIPS
    gcs = 3 * GDN_W // N_CHIPS
    mcs = N_MOD * D_MODEL // N_CHIPS

    g_in = _all_gather_rows(_pack_rows([c, lru_conv_w, gdn_conv_w], SUBLANES), "gather_small_inputs").reshape(N_DEV, -1)
    c_all = g_in[:, :D_MODEL]
    per_chip = g_in[0::2]
    o1 = D_MODEL + L * 4 * lcs
    lcw_full = per_chip[:, D_MODEL:o1].reshape(N_CHIPS, L, 4, lcs).transpose(1, 2, 0, 3).reshape(L, 4, LRU_W)
    gcw_full = per_chip[:, o1:o1 + L * 4 * gcs].reshape(N_CHIPS, L, 4, gcs).transpose(1, 2, 0, 3).reshape(L, 4, 3 * GDN_W)

    b_cols = lax.dynamic_slice(b_mod, (0, chip * mcs), (L, mcs)).reshape(L, 1, mcs)
    modp = _mod_fwd(c_all, w_mod, b_cols)
    g_mod = _all_gather_rows(modp.reshape(L * N_DEV, mcs), "gather_mod").reshape(N_DEV, L, N_DEV, mcs)
    mod = lax.dynamic_index_in_dim(g_mod[0::2], dev, axis=2, keepdims=False).transpose(1, 0, 2).reshape(L, N_MOD * D_MODEL)

    stacked = _prep_layers(norm_mix_w, norm_mlp_w, mod, None, lcw_full, lru_conv_b, lru_gate_a_w, lru_gate_a_b,
                           lru_gate_x_w, lru_gate_x_b, lru_lambda, lru_norm_w, gcw_full, gdn_a_log, gdn_dt_bias,
                           gdn_norm_w, None, None, None)
    shards = [[w_in[l].astype(BF16), w_out[l].astype(BF16), w_up[l].astype(BF16), w_down[l].astype(BF16)]
              for l in range(L)]

    def with_own(gathered, own):
        return [lax.dynamic_update_slice(got, o.reshape((1,) + got.shape[1:]), (chip, 0, 0, 0)).reshape(
            (N_CHIPS,) + o.shape) for got, o in zip(gathered, own)]

    def mixer_weights(l, gathered):
        win_g, wout_g = with_own(gathered, shards[l][:2])
        return dict(win=jnp.pad(win_g.transpose(1, 0, 2).reshape(D_MODEL, IN_COLS), ((0, 0), (0, IN_PAD - IN_COLS))),
                    wout=wout_g.reshape(D_MODEL, D_MODEL))

    def mlp_weights(l, gathered):
        wup_g, wdown_g = with_own(gathered, shards[l][2:])
        return dict(wup=wup_g.transpose(1, 0, 2).reshape(D_MODEL, D_FF), wdown=wdown_g.reshape(D_FF, D_MODEL))

    mixer = mixer_weights(0, _gather_chips(shards[0][:2], "gather_weights"))
    layers = []
    xs = x[0]
    res = []
    for l in range(L):
        p = {k: v[l] for k, v in stacked.items() if v is not None}
        p.update(mixer)
        xs, r, p, gathered = _layer_fwd(xs, p, shards[l][2:], functools.partial(mlp_weights, l),
                                        shards[l + 1][:2] if l + 1 < L else ())
        res.append(r)
        layers.append(p)
        if l + 1 < L:
            mixer = mixer_weights(l + 1, gathered)
    dx, loss_blk, dfnw = _loss_head(xs, loss_target[0], final_norm_w.reshape(1, D_MODEL))
    loss_local = loss_blk[0, 0]

    big_names = ["w_in", "w_out", "w_up", "w_down"]
    core = jnp.reshape(ci, (1,)).astype(jnp.int32)
    place = jnp.stack([ci, chip]).astype(jnp.int32)
    layer_grads = [None] * L

    big = {nm: [lax.empty(W[nm].shape, F32) for _ in range(4)] for nm in big_names}

    def pair_sums(names, parts):
        from_sibling = _send_to_sibling(parts, "pair_send")
        return [_pair_add(p, r, core, "pair_add_" + nm) for nm, p, r in zip(names, parts, from_sibling)]

    def apply_update(items, arrived, pair):
        halves = [_chip_sum(a, own, place, "chip_sum_" + nm) for (_, nm), a, own in zip(items, arrived, pair)]
        full = _swap_row_halves(halves, "pair_swap")
        for (l, nm), gr in zip(items, full):
            big[nm] = _adam_layer(gr, W[nm], M[nm], V[nm], big[nm], l, "adam_" + nm)

    win_pair = []
    for l in reversed(range(L)):
        early_pair = []

        def early(parts, early_pair=early_pair):
            early_pair.extend(pair_sums(big_names[1:], parts))
            return early_pair

        dx, gl, arrived = _layer_bwd(dx, layers[l], res[l], sharded=True, scatter=win_pair, early=early)
        items = [(l + 1, big_names[0])] * len(win_pair) + [(l, nm) for nm in big_names[1:]]
        apply_update(items, arrived, win_pair + early_pair)
        layer_grads[l] = gl
        gwin = gl["win"][:, :IN_COLS].reshape(2, D_MODEL // 2, N_CHIPS, IN_COLS // N_CHIPS).transpose(0, 2, 1, 3)
        win_pair = pair_sums(big_names[:1], [gwin])
    apply_update([(0, big_names[0])], _scatter_chips(win_pair, "chip_scatter"), win_pair)
    small_keys = [k for k in layer_grads[0] if k not in ("win", "wout", "wup", "wdown")]
    g = {k: jnp.stack([gl[k] for gl in layer_grads]) for k in small_keys}
    loss = lax.psum(loss_local, ("x", "y", "c"))

    dmod = jnp.concatenate([g["sh1"], g["sc1"], g["g1"], g["sh2"], g["sc2"], g["g2"]], axis=-1)
    small = [dmod, g["nmw"], g["nmlp"], g["lcw"][:, :4], g["lcb"], jax.vmap(_diag_blocks)(g["wa"]), g["ba"],
             jax.vmap(_diag_blocks)(g["wx"]), g["bx"], g["lam"], g["lnw"], g["gcw"][:, :4], g["alog"], g["dtb"],
             g["gnw"], dfnw]
    small_shapes = [(L, N_MOD * D_MODEL), (L, D_MODEL), (L, D_MODEL), (L, 4, LRU_W), (L, LRU_W),
                    (L, LRU_BLOCKS, LRU_BLOCK, LRU_BLOCK), (L, LRU_W), (L, LRU_BLOCKS, LRU_BLOCK, LRU_BLOCK),
                    (L, LRU_W), (L, LRU_W), (L, LRU_W), (L, 4, 3 * GDN_W), (L, LANES), (L, LANES), (L, LANES),
                    (D_MODEL,)]
    small_names = ["b_mod", "norm_mix_w", "norm_mlp_w", None, "lru_conv_b", "lru_gate_a_w", "lru_gate_a_b",
                   "lru_gate_x_w", "lru_gate_x_b", "lru_lambda", "lru_norm_w", None, "gdn_a_log", "gdn_dt_bias",
                   "gdn_norm_w", "final_norm_w"]
    pack_g = _pack_rows(small, 512)
    rows = pack_g.shape[0]
    all_g = _all_gather_rows(pack_g, "gather_small_grads").reshape(N_DEV, rows, LANES)
    tot = _sum_slots(all_g, "sum_small_grads")
    tot_parts = _unpack(tot, small_shapes)

    def pack_state(S_):
        parts = []
        for nm, shp in zip(small_names, small_shapes):
            if nm is None:
                parts.append(jnp.zeros(shp, F32))
            elif nm in ("gdn_a_log", "gdn_dt_bias"):
                parts.append(_lane_pad(S_[nm]))
            else:
                parts.append(S_[nm])
        return _pack_rows(parts, 512)

    upd = _adam(pack_state(W), tot, pack_state(M), pack_state(V), "adam_small")
    upd_parts = [_unpack(u, small_shapes) for u in upd]

    grads, deltas, new_m, new_v = {}, {}, {}, {}
    for k, nm in enumerate(small_names):
        if nm is None:
            continue
        cut = (lambda a: a[:, :HEADS]) if nm in ("gdn_a_log", "gdn_dt_bias") else (lambda a: a)
        grads[nm] = cut(tot_parts[k])
        deltas[nm], new_m[nm], new_v[nm] = (cut(u[k]) for u in upd_parts)

    g_lcw = lax.dynamic_slice(tot_parts[3], (0, 0, chip * lcs), (L, 4, lcs))
    g_gcw = lax.dynamic_slice(tot_parts[11], (0, 0, chip * gcs), (L, 4, gcs))
    conv_shapes = [(L, 4, lcs), (L, 4, gcs)]
    conv_pack = lambda a, b: _pack_rows([a, b], SUBLANES)
    cu = _adam(conv_pack(lru_conv_w, gdn_conv_w), conv_pack(g_lcw, g_gcw), conv_pack(m_lru_conv_w, m_gdn_conv_w),
               conv_pack(v_lru_conv_w, v_gdn_conv_w), "adam_conv")
    cu_parts = [_unpack(u, conv_shapes) for u in cu]
    for k, nm in enumerate(("lru_conv_w", "gdn_conv_w")):
        grads[nm] = (g_lcw, g_gcw)[k]
        deltas[nm], new_m[nm], new_v[nm] = (u[k] for u in cu_parts)

    dmod_all = all_g[:, :L * N_MOD * D_MODEL // LANES].reshape(N_DEV, L, N_MOD * D_MODEL)
    dmod_cols = lax.dynamic_slice(dmod_all, (0, 0, chip * mcs), (N_DEV, L, mcs)).transpose(1, 0, 2)
    grads["w_mod"], deltas["w_mod"], new_m["w_mod"], new_v["w_mod"] = _mod_update(c_all, dmod_cols, w_mod, m_w_mod, v_w_mod)

    for nm in big_names:
        grads[nm], deltas[nm], new_m[nm], new_v[nm] = big[nm]

    out = [loss, dx[None]]
    for group in (grads, deltas, new_m, new_v):
        out += [group[nm].reshape(W[nm].shape) for nm in WEIGHT_NAMES]
    return tuple(out)
```
